```python
import jax, jax.numpy as jnp
from jax import lax
import numpy as np

D_MODEL = 2048
BATCH = 8
SEQ = 8192
DEPTH = 1

CHUNK = 64
MEM_LEN = 256
D_MIX = D_MODEL
D_POOL = D_MIX // 2
POOL_WINDOWS = (2, 4, 8, 16)
N_POOL_GROUPS = len(POOL_WINDOWS)
POOL_GROUP = D_POOL // N_POOL_GROUPS
D_GLA = D_MIX - D_POOL
GLA_HEADS = 4
GLA_DV = D_GLA // GLA_HEADS
GLA_DK = GLA_DV // 2
D_GLA_K = GLA_HEADS * GLA_DK
GLA_GATE_RANK = 16
GLA_GATE_TEMP = 16.0
XATTN_HEADS = 4
XATTN_HEAD_DIM = D_MODEL // XATTN_HEADS
D_FF = ((8 * D_MODEL // 3 + 255) // 256) * 256
RMS_EPS = 1e-6
IN_SPLITS = (D_POOL,
             D_POOL + D_GLA_K,
             D_POOL + 2 * D_GLA_K,
             D_POOL + 2 * D_GLA_K + D_GLA,
             D_POOL + 2 * D_GLA_K + 2 * D_GLA)
IN_COLS = D_POOL + 2 * D_GLA_K + 2 * D_GLA + GLA_GATE_RANK

kernel_name = "hymba_pool_gla_macaron_memxattn"


def rms_norm(x, gain):
    xf = x.astype(jnp.float32)
    y = xf * lax.rsqrt(jnp.mean(xf * xf, axis=-1, keepdims=True) + RMS_EPS)
    return (y * gain.astype(jnp.float32)).astype(x.dtype)


def swiglu(h, w_gate, w_up, w_down):
    return (jax.nn.silu(h @ w_gate) * (h @ w_up)) @ w_down


def pool_mixer(u, pool_w, pool_scale):
    b, s, _ = u.shape
    uf = u.astype(jnp.float32).reshape(b, s, N_POOL_GROUPS, POOL_GROUP)
    cs = jnp.cumsum(uf, axis=1)
    count = jnp.arange(1, s + 1, dtype=jnp.float32)
    diffs = []
    for g, w in enumerate(POOL_WINDOWS):
        cs_g = cs[:, :, g]
        lagged = jnp.pad(cs_g, ((0, 0), (w, 0), (0, 0)))[:, :s]
        mean = (cs_g - lagged) / jnp.minimum(count, float(w))[None, :, None]
        diffs.append(mean - uf[:, :, g])
    d = jnp.stack(diffs, axis=2).astype(u.dtype)
    y = jnp.einsum('bsgc,gcd->bsgd', d, pool_w)
    return y.reshape(b, s, D_POOL) * pool_scale


def gla_mixer(q, k, v, g, a_lr, w_a2, b_a, head_norm):
    b, s, _ = q.shape
    nc = s // CHUNK
    dt = v.dtype
    log_a = jax.nn.log_sigmoid((a_lr @ w_a2 + b_a).astype(jnp.float32)) / GLA_GATE_TEMP

    def chunks(t, d):
        return t.astype(jnp.float32).reshape(b, nc, CHUNK, GLA_HEADS, d)

    qc = chunks(q, GLA_DK) * (GLA_DK ** -0.5)
    kc = chunks(k, GLA_DK)
    vc = chunks(v, GLA_DV)
    cum = jnp.cumsum(chunks(log_a, GLA_DK), axis=2)
    b_end = cum[:, :, -1:]
    k_dec = kc * jnp.exp(b_end - cum)
    q_dec = qc * jnp.exp(b_end)
    scores = jnp.einsum('bnihk,bnjhk->bnhij', qc, k_dec)
    o_intra = jnp.einsum('bnhij,bnjhv->bnihv', scores, vc)

    def step(state, xs):
        q_c, k_c, v_c, decay_c = xs
        o = jnp.einsum('bihk,bhkv->bihv', q_c, state)
        state = decay_c[..., None] * state + jnp.einsum('bjhk,bjhv->bhkv', k_c, v_c)
        return state, o

    xs = (jnp.moveaxis(q_dec, 1, 0), jnp.moveaxis(k_dec, 1, 0),
          jnp.moveaxis(vc, 1, 0), jnp.moveaxis(jnp.exp(b_end[:, :, 0]), 1, 0))
    s0 = jnp.zeros((b, GLA_HEADS, GLA_DK, GLA_DV), jnp.float32)
    _, o_inter = lax.scan(step, s0, xs)
    o = o_intra + jnp.moveaxis(o_inter, 0, 1)
    o = o * lax.rsqrt(jnp.mean(o * o, axis=-1, keepdims=True) + RMS_EPS)
    o = o.reshape(b, s, D_GLA) * head_norm.astype(jnp.float32)
    return o.astype(dt) * jax.nn.silu(g)


def mem_cross_attention(h, mem_h, w_q, w_kv, w_o):
    b, s, _ = h.shape
    q = (h @ w_q).reshape(b, s, XATTN_HEADS, XATTN_HEAD_DIM)
    kv = (mem_h @ w_kv).reshape(b, mem_h.shape[1], 2, XATTN_HEADS, XATTN_HEAD_DIM)
    k, v = kv[:, :, 0], kv[:, :, 1]
    logits = jnp.einsum('bshd,bmhd->bhsm', q, k).astype(jnp.float32) * (XATTN_HEAD_DIM ** -0.5)
    p = jax.nn.softmax(logits, axis=-1).astype(v.dtype)
    o = jnp.einsum('bhsm,bmhd->bshd', p, v).reshape(b, s, D_MODEL)
    return o @ w_o


def _fwd_setup_inputs(seed: int = 0) -> dict:
    key = jax.random.key(seed)
    ks = jax.random.split(key, 32)

    def dense(k, shape, fan_in):
        return jax.random.normal(k, shape, jnp.float32) * (fan_in ** -0.5)

    def gain(k, shape):
        return 1.0 + 0.02 * jax.random.normal(k, shape, jnp.float32)

    L = DEPTH
    return {
        "x": jax.random.normal(ks[0], (BATCH, SEQ, D_MODEL), jnp.float32),
        "mem": jax.random.normal(ks[1], (BATCH, MEM_LEN, D_MODEL), jnp.float32),
        "ffn1_norm": gain(ks[2], (L, D_MODEL)),
        "ffn1_w_gate": dense(ks[3], (L, D_MODEL, D_FF), D_MODEL),
        "ffn1_w_up": dense(ks[4], (L, D_MODEL, D_FF), D_MODEL),
        "ffn1_w_down": dense(ks[5], (L, D_FF, D_MODEL), D_FF),
        "mix_norm": gain(ks[6], (L, D_MODEL)),
        "w_in": dense(ks[7], (L, D_MODEL, IN_COLS), D_MODEL),
        "pool_w": dense(ks[8], (L, N_POOL_GROUPS, POOL_GROUP, POOL_GROUP), POOL_GROUP),
        "pool_scale": 1.0 + 0.1 * jax.random.normal(ks[9], (L, D_POOL), jnp.float32),
        "gla_w_a2": dense(ks[10], (L, GLA_GATE_RANK, D_GLA_K), GLA_GATE_RANK),
        "gla_b_a": 0.1 * jax.random.normal(ks[11], (L, D_GLA_K), jnp.float32),
        "gla_head_norm": gain(ks[12], (L, D_GLA)),
        "w_out": dense(ks[13], (L, D_MIX, D_MODEL), D_MIX),
        "xattn_norm": gain(ks[14], (L, D_MODEL)),
        "mem_norm": gain(ks[15], (L, D_MODEL)),
        "xattn_w_q": dense(ks[16], (L, D_MODEL, D_MODEL), D_MODEL),
        "xattn_w_kv": dense(ks[17], (L, D_MODEL, 2 * D_MODEL), D_MODEL),
        "xattn_w_o": dense(ks[18], (L, D_MODEL, D_MODEL), D_MODEL),
        "ffn2_norm": gain(ks[19], (L, D_MODEL)),
        "ffn2_w_gate": dense(ks[20], (L, D_MODEL, D_FF), D_MODEL),
        "ffn2_w_up": dense(ks[21], (L, D_MODEL, D_FF), D_MODEL),
        "ffn2_w_down": dense(ks[22], (L, D_FF, D_MODEL), D_FF),
        "final_norm": gain(ks[23], (D_MODEL,)),
    }


def _fwd_reference(x, mem, ffn1_norm, ffn1_w_gate, ffn1_w_up, ffn1_w_down, mix_norm, w_in,
              pool_w, pool_scale, gla_w_a2, gla_b_a, gla_head_norm, w_out,
              xattn_norm, mem_norm, xattn_w_q, xattn_w_kv, xattn_w_o,
              ffn2_norm, ffn2_w_gate, ffn2_w_up, ffn2_w_down, final_norm):
    for l in range(DEPTH):
        x = x + 0.5 * swiglu(rms_norm(x, ffn1_norm[l]), ffn1_w_gate[l], ffn1_w_up[l], ffn1_w_down[l])
        h = rms_norm(x, mix_norm[l])
        proj = h @ w_in[l]
        u, q, k, v, g, a_lr = jnp.split(proj, list(IN_SPLITS), axis=-1)
        y_pool = pool_mixer(u, pool_w[l], pool_scale[l])
        y_gla = gla_mixer(q, k, v, g, a_lr, gla_w_a2[l], gla_b_a[l], gla_head_norm[l])
        x = x + jnp.concatenate([y_pool, y_gla], axis=-1) @ w_out[l]
        x = x + mem_cross_attention(rms_norm(x, xattn_norm[l]), rms_norm(mem, mem_norm[l]),
                                    xattn_w_q[l], xattn_w_kv[l], xattn_w_o[l])
        x = x + 0.5 * swiglu(rms_norm(x, ffn2_norm[l]), ffn2_w_gate[l], ffn2_w_up[l], ffn2_w_down[l])
    return rms_norm(x, final_norm)


import jax as _jax
import jax.numpy as _jnp

TWIN_FORMAT = 'train_step'
FWD_PARAMS = ['x', 'mem', 'ffn1_norm', 'ffn1_w_gate', 'ffn1_w_up', 'ffn1_w_down', 'mix_norm', 'w_in', 'pool_w', 'pool_scale', 'gla_w_a2', 'gla_b_a', 'gla_head_norm', 'w_out', 'xattn_norm', 'mem_norm', 'xattn_w_q', 'xattn_w_kv', 'xattn_w_o', 'ffn2_norm', 'ffn2_w_gate', 'ffn2_w_up', 'ffn2_w_down', 'final_norm']
TWIN_WEIGHTS = ['ffn1_norm', 'ffn1_w_gate', 'ffn1_w_up', 'ffn1_w_down', 'mix_norm', 'w_in', 'pool_w', 'pool_scale', 'gla_w_a2', 'gla_b_a', 'gla_head_norm', 'w_out', 'xattn_norm', 'mem_norm', 'xattn_w_q', 'xattn_w_kv', 'xattn_w_o', 'ffn2_norm', 'ffn2_w_gate', 'ffn2_w_up', 'ffn2_w_down', 'final_norm']
TWIN_DIFF_INPUT = 'x'
TWIN_INPUTS = ['x', 'mem', 'ffn1_norm', 'ffn1_w_gate', 'ffn1_w_up', 'ffn1_w_down', 'mix_norm', 'w_in', 'pool_w', 'pool_scale', 'gla_w_a2', 'gla_b_a', 'gla_head_norm', 'w_out', 'xattn_norm', 'mem_norm', 'xattn_w_q', 'xattn_w_kv', 'xattn_w_o', 'ffn2_norm', 'ffn2_w_gate', 'ffn2_w_up', 'ffn2_w_down', 'final_norm', 'loss_target', 'm_ffn1_norm', 'm_ffn1_w_gate', 'm_ffn1_w_up', 'm_ffn1_w_down', 'm_mix_norm', 'm_w_in', 'm_pool_w', 'm_pool_scale', 'm_gla_w_a2', 'm_gla_b_a', 'm_gla_head_norm', 'm_w_out', 'm_xattn_norm', 'm_mem_norm', 'm_xattn_w_q', 'm_xattn_w_kv', 'm_xattn_w_o', 'm_ffn2_norm', 'm_ffn2_w_gate', 'm_ffn2_w_up', 'm_ffn2_w_down', 'm_final_norm', 'v_ffn1_norm', 'v_ffn1_w_gate', 'v_ffn1_w_up', 'v_ffn1_w_down', 'v_mix_norm', 'v_w_in', 'v_pool_w', 'v_pool_scale', 'v_gla_w_a2', 'v_gla_b_a', 'v_gla_head_norm', 'v_w_out', 'v_xattn_norm', 'v_mem_norm', 'v_xattn_w_q', 'v_xattn_w_kv', 'v_xattn_w_o', 'v_ffn2_norm', 'v_ffn2_w_gate', 'v_ffn2_w_up', 'v_ffn2_w_down', 'v_final_norm']
TWIN_OUTPUTS = ['loss', 'grad_x', 'grad_ffn1_norm', 'grad_ffn1_w_gate', 'grad_ffn1_w_up', 'grad_ffn1_w_down', 'grad_mix_norm', 'grad_w_in', 'grad_pool_w', 'grad_pool_scale', 'grad_gla_w_a2', 'grad_gla_b_a', 'grad_gla_head_norm', 'grad_w_out', 'grad_xattn_norm', 'grad_mem_norm', 'grad_xattn_w_q', 'grad_xattn_w_kv', 'grad_xattn_w_o', 'grad_ffn2_norm', 'grad_ffn2_w_gate', 'grad_ffn2_w_up', 'grad_ffn2_w_down', 'grad_final_norm', 'delta_ffn1_norm', 'delta_ffn1_w_gate', 'delta_ffn1_w_up', 'delta_ffn1_w_down', 'delta_mix_norm', 'delta_w_in', 'delta_pool_w', 'delta_pool_scale', 'delta_gla_w_a2', 'delta_gla_b_a', 'delta_gla_head_norm', 'delta_w_out', 'delta_xattn_norm', 'delta_mem_norm', 'delta_xattn_w_q', 'delta_xattn_w_kv', 'delta_xattn_w_o', 'delta_ffn2_norm', 'delta_ffn2_w_gate', 'delta_ffn2_w_up', 'delta_ffn2_w_down', 'delta_final_norm', 'new_m_ffn1_norm', 'new_m_ffn1_w_gate', 'new_m_ffn1_w_up', 'new_m_ffn1_w_down', 'new_m_mix_norm', 'new_m_w_in', 'new_m_pool_w', 'new_m_pool_scale', 'new_m_gla_w_a2', 'new_m_gla_b_a', 'new_m_gla_head_norm', 'new_m_w_out', 'new_m_xattn_norm', 'new_m_mem_norm', 'new_m_xattn_w_q', 'new_m_xattn_w_kv', 'new_m_xattn_w_o', 'new_m_ffn2_norm', 'new_m_ffn2_w_gate', 'new_m_ffn2_w_up', 'new_m_ffn2_w_down', 'new_m_final_norm', 'new_v_ffn1_norm', 'new_v_ffn1_w_gate', 'new_v_ffn1_w_up', 'new_v_ffn1_w_down', 'new_v_mix_norm', 'new_v_w_in', 'new_v_pool_w', 'new_v_pool_scale', 'new_v_gla_w_a2', 'new_v_gla_b_a', 'new_v_gla_head_norm', 'new_v_w_out', 'new_v_xattn_norm', 'new_v_mem_norm', 'new_v_xattn_w_q', 'new_v_xattn_w_kv', 'new_v_xattn_w_o', 'new_v_ffn2_norm', 'new_v_ffn2_w_gate', 'new_v_ffn2_w_up', 'new_v_ffn2_w_down', 'new_v_final_norm']
TWIN_LEAF_KINDS = {'loss': 'loss', 'grad_x': 'grad_x', 'grad_ffn1_norm': 'grad_w', 'grad_ffn1_w_gate': 'grad_w', 'grad_ffn1_w_up': 'grad_w', 'grad_ffn1_w_down': 'grad_w', 'grad_mix_norm': 'grad_w', 'grad_w_in': 'grad_w', 'grad_pool_w': 'grad_w', 'grad_pool_scale': 'grad_w', 'grad_gla_w_a2': 'grad_w', 'grad_gla_b_a': 'grad_w', 'grad_gla_head_norm': 'grad_w', 'grad_w_out': 'grad_w', 'grad_xattn_norm': 'grad_w', 'grad_mem_norm': 'grad_w', 'grad_xattn_w_q': 'grad_w', 'grad_xattn_w_kv': 'grad_w', 'grad_xattn_w_o': 'grad_w', 'grad_ffn2_norm': 'grad_w', 'grad_ffn2_w_gate': 'grad_w', 'grad_ffn2_w_up': 'grad_w', 'grad_ffn2_w_down': 'grad_w', 'grad_final_norm': 'grad_w', 'delta_ffn1_norm': 'delta_w', 'delta_ffn1_w_gate': 'delta_w', 'delta_ffn1_w_up': 'delta_w', 'delta_ffn1_w_down': 'delta_w', 'delta_mix_norm': 'delta_w', 'delta_w_in': 'delta_w', 'delta_pool_w': 'delta_w', 'delta_pool_scale': 'delta_w', 'delta_gla_w_a2': 'delta_w', 'delta_gla_b_a': 'delta_w', 'delta_gla_head_norm': 'delta_w', 'delta_w_out': 'delta_w', 'delta_xattn_norm': 'delta_w', 'delta_mem_norm': 'delta_w', 'delta_xattn_w_q': 'delta_w', 'delta_xattn_w_kv': 'delta_w', 'delta_xattn_w_o': 'delta_w', 'delta_ffn2_norm': 'delta_w', 'delta_ffn2_w_gate': 'delta_w', 'delta_ffn2_w_up': 'delta_w', 'delta_ffn2_w_down': 'delta_w', 'delta_final_norm': 'delta_w', 'new_m_ffn1_norm': 'new_m', 'new_m_ffn1_w_gate': 'new_m', 'new_m_ffn1_w_up': 'new_m', 'new_m_ffn1_w_down': 'new_m', 'new_m_mix_norm': 'new_m', 'new_m_w_in': 'new_m', 'new_m_pool_w': 'new_m', 'new_m_pool_scale': 'new_m', 'new_m_gla_w_a2': 'new_m', 'new_m_gla_b_a': 'new_m', 'new_m_gla_head_norm': 'new_m', 'new_m_w_out': 'new_m', 'new_m_xattn_norm': 'new_m', 'new_m_mem_norm': 'new_m', 'new_m_xattn_w_q': 'new_m', 'new_m_xattn_w_kv': 'new_m', 'new_m_xattn_w_o': 'new_m', 'new_m_ffn2_norm': 'new_m', 'new_m_ffn2_w_gate': 'new_m', 'new_m_ffn2_w_up': 'new_m', 'new_m_ffn2_w_down': 'new_m', 'new_m_final_norm': 'new_m', 'new_v_ffn1_norm': 'new_v', 'new_v_ffn1_w_gate': 'new_v', 'new_v_ffn1_w_up': 'new_v', 'new_v_ffn1_w_down': 'new_v', 'new_v_mix_norm': 'new_v', 'new_v_w_in': 'new_v', 'new_v_pool_w': 'new_v', 'new_v_pool_scale': 'new_v', 'new_v_gla_w_a2': 'new_v', 'new_v_gla_b_a': 'new_v', 'new_v_gla_head_norm': 'new_v', 'new_v_w_out': 'new_v', 'new_v_xattn_norm': 'new_v', 'new_v_mem_norm': 'new_v', 'new_v_xattn_w_q': 'new_v', 'new_v_xattn_w_kv': 'new_v', 'new_v_xattn_w_o': 'new_v', 'new_v_ffn2_norm': 'new_v', 'new_v_ffn2_w_gate': 'new_v', 'new_v_ffn2_w_up': 'new_v', 'new_v_ffn2_w_down': 'new_v', 'new_v_final_norm': 'new_v'}


def _forward(args):
    return _fwd_reference(*[args[k] for k in FWD_PARAMS])


def _output_shape():
    def fwd():
        inp = _fwd_setup_inputs(0)
        return _fwd_reference(*[inp[k] for k in FWD_PARAMS])
    out = _jax.eval_shape(fwd)
    return out.shape, out.dtype

N_MICROBATCH = 1
ADAM_LR = 0.001
ADAM_B1 = 0.9
ADAM_B2 = 0.999
ADAM_EPS = 1e-08
ADAM_WD = 0.01
ADAM_STEP = 10
PER_EXAMPLE_BATCH_AXIS = {'x': 0, 'mem': 0, 'loss_target': 0}
SHARED_INPUTS = []
_WEIGHT_DTYPES = {'ffn1_norm': _jnp.float32, 'ffn1_w_gate': _jnp.float32, 'ffn1_w_up': _jnp.float32, 'ffn1_w_down': _jnp.float32, 'mix_norm': _jnp.float32, 'w_in': _jnp.float32, 'pool_w': _jnp.float32, 'pool_scale': _jnp.float32, 'gla_w_a2': _jnp.float32, 'gla_b_a': _jnp.float32, 'gla_head_norm': _jnp.float32, 'w_out': _jnp.float32, 'xattn_norm': _jnp.float32, 'mem_norm': _jnp.float32, 'xattn_w_q': _jnp.float32, 'xattn_w_kv': _jnp.float32, 'xattn_w_o': _jnp.float32, 'ffn2_norm': _jnp.float32, 'ffn2_w_gate': _jnp.float32, 'ffn2_w_up': _jnp.float32, 'ffn2_w_down': _jnp.float32, 'final_norm': _jnp.float32}
MOMENT_SCALE = {'ffn1_norm': 6.075968e-02, 'ffn1_w_gate': 2.655032e-02, 'ffn1_w_up': 2.568620e-02, 'ffn1_w_down': 4.264076e-02, 'mix_norm': 1.087036e-01, 'w_in': 7.552864e-02, 'pool_w': 8.883638e-02, 'pool_scale': 9.418821e-02, 'gla_w_a2': 9.873515e-03, 'gla_b_a': 4.278178e-02, 'gla_head_norm': 5.982343e-02, 'w_out': 7.560774e-02, 'xattn_norm': 1.032083e-02, 'mem_norm': 1.495127e-02, 'xattn_w_q': 1.014999e-02, 'xattn_w_kv': 1.023549e-02, 'xattn_w_o': 1.035834e-02, 'ffn2_norm': 4.161042e-02, 'ffn2_w_gate': 1.757804e-02, 'ffn2_w_up': 1.703892e-02, 'ffn2_w_down': 2.827159e-02, 'final_norm': 3.196066e+01}


def _to_microbatches(a, axis):
    t = _jnp.moveaxis(a, axis, 0)
    t = t.reshape((N_MICROBATCH, t.shape[0] // N_MICROBATCH) + t.shape[1:])
    return _jnp.moveaxis(t, 1, axis + 1)


def setup_inputs(seed: int = 0) -> dict:
    inp = _fwd_setup_inputs(seed)
    key = _jax.random.fold_in(_jax.random.key(seed), 7919)
    shape, _ = _output_shape()
    out = dict(inp)
    out["loss_target"] = _jax.random.normal(_jax.random.fold_in(key, 0), shape, _jnp.float32)
    for i, name in enumerate(TWIN_WEIGHTS):
        w = inp[name].astype(_jnp.float32)
        if MOMENT_SCALE is None:
            s = _jnp.sqrt(_jnp.mean(_jnp.square(w)) + 1e-30)
        else:
            s = MOMENT_SCALE[name]
        km, kv = _jax.random.split(_jax.random.fold_in(key, i + 1))
        out[name] = w
        out["m_" + name] = s * _jax.random.normal(km, w.shape, _jnp.float32)
        out["v_" + name] = (s * s) * _jax.random.uniform(kv, w.shape, _jnp.float32, 0.5, 1.5)
    if N_MICROBATCH > 1:
        for name, axis in PER_EXAMPLE_BATCH_AXIS.items():
            out[name] = _to_microbatches(out[name], axis)
    return {'x': out['x'], 'mem': out['mem'], 'ffn1_norm': out['ffn1_norm'], 'ffn1_w_gate': out['ffn1_w_gate'], 'ffn1_w_up': out['ffn1_w_up'], 'ffn1_w_down': out['ffn1_w_down'], 'mix_norm': out['mix_norm'], 'w_in': out['w_in'], 'pool_w': out['pool_w'], 'pool_scale': out['pool_scale'], 'gla_w_a2': out['gla_w_a2'], 'gla_b_a': out['gla_b_a'], 'gla_head_norm': out['gla_head_norm'], 'w_out': out['w_out'], 'xattn_norm': out['xattn_norm'], 'mem_norm': out['mem_norm'], 'xattn_w_q': out['xattn_w_q'], 'xattn_w_kv': out['xattn_w_kv'], 'xattn_w_o': out['xattn_w_o'], 'ffn2_norm': out['ffn2_norm'], 'ffn2_w_gate': out['ffn2_w_gate'], 'ffn2_w_up': out['ffn2_w_up'], 'ffn2_w_down': out['ffn2_w_down'], 'final_norm': out['final_norm'], 'loss_target': out['loss_target'], 'm_ffn1_norm': out['m_ffn1_norm'], 'm_ffn1_w_gate': out['m_ffn1_w_gate'], 'm_ffn1_w_up': out['m_ffn1_w_up'], 'm_ffn1_w_down': out['m_ffn1_w_down'], 'm_mix_norm': out['m_mix_norm'], 'm_w_in': out['m_w_in'], 'm_pool_w': out['m_pool_w'], 'm_pool_scale': out['m_pool_scale'], 'm_gla_w_a2': out['m_gla_w_a2'], 'm_gla_b_a': out['m_gla_b_a'], 'm_gla_head_norm': out['m_gla_head_norm'], 'm_w_out': out['m_w_out'], 'm_xattn_norm': out['m_xattn_norm'], 'm_mem_norm': out['m_mem_norm'], 'm_xattn_w_q': out['m_xattn_w_q'], 'm_xattn_w_kv': out['m_xattn_w_kv'], 'm_xattn_w_o': out['m_xattn_w_o'], 'm_ffn2_norm': out['m_ffn2_norm'], 'm_ffn2_w_gate': out['m_ffn2_w_gate'], 'm_ffn2_w_up': out['m_ffn2_w_up'], 'm_ffn2_w_down': out['m_ffn2_w_down'], 'm_final_norm': out['m_final_norm'], 'v_ffn1_norm': out['v_ffn1_norm'], 'v_ffn1_w_gate': out['v_ffn1_w_gate'], 'v_ffn1_w_up': out['v_ffn1_w_up'], 'v_ffn1_w_down': out['v_ffn1_w_down'], 'v_mix_norm': out['v_mix_norm'], 'v_w_in': out['v_w_in'], 'v_pool_w': out['v_pool_w'], 'v_pool_scale': out['v_pool_scale'], 'v_gla_w_a2': out['v_gla_w_a2'], 'v_gla_b_a': out['v_gla_b_a'], 'v_gla_head_norm': out['v_gla_head_norm'], 'v_w_out': out['v_w_out'], 'v_xattn_norm': out['v_xattn_norm'], 'v_mem_norm': out['v_mem_norm'], 'v_xattn_w_q': out['v_xattn_w_q'], 'v_xattn_w_kv': out['v_xattn_w_kv'], 'v_xattn_w_o': out['v_xattn_w_o'], 'v_ffn2_norm': out['v_ffn2_norm'], 'v_ffn2_w_gate': out['v_ffn2_w_gate'], 'v_ffn2_w_up': out['v_ffn2_w_up'], 'v_ffn2_w_down': out['v_ffn2_w_down'], 'v_final_norm': out['v_final_norm']}


def _loss(weights, diff, rest, loss_target):
    with _jax.named_scope("forward"):
        args = {**rest, TWIN_DIFF_INPUT: diff, **{k: w.astype(_WEIGHT_DTYPES[k]) for k, w in weights.items()}}
        y = _forward(args)
    with _jax.named_scope("loss_head"):
        err = _jnp.square(y.astype(_jnp.float32) - loss_target)
        return 0.5 * _jnp.sum(_jnp.mean(err, axis=-1)) if err.ndim else 0.5 * err


def _adamw(w, g, m, v):
    m = ADAM_B1 * m + (1.0 - ADAM_B1) * g
    v = ADAM_B2 * v + (1.0 - ADAM_B2) * _jnp.square(g)
    m_hat = m / (1.0 - ADAM_B1 ** ADAM_STEP)
    v_hat = v / (1.0 - ADAM_B2 ** ADAM_STEP)
    delta = -ADAM_LR * (m_hat / (_jnp.sqrt(v_hat) + ADAM_EPS) + ADAM_WD * w)
    return delta, m, v


def reference(x, mem, ffn1_norm, ffn1_w_gate, ffn1_w_up, ffn1_w_down, mix_norm, w_in, pool_w, pool_scale, gla_w_a2, gla_b_a, gla_head_norm, w_out, xattn_norm, mem_norm, xattn_w_q, xattn_w_kv, xattn_w_o, ffn2_norm, ffn2_w_gate, ffn2_w_up, ffn2_w_down, final_norm, loss_target, m_ffn1_norm, m_ffn1_w_gate, m_ffn1_w_up, m_ffn1_w_down, m_mix_norm, m_w_in, m_pool_w, m_pool_scale, m_gla_w_a2, m_gla_b_a, m_gla_head_norm, m_w_out, m_xattn_norm, m_mem_norm, m_xattn_w_q, m_xattn_w_kv, m_xattn_w_o, m_ffn2_norm, m_ffn2_w_gate, m_ffn2_w_up, m_ffn2_w_down, m_final_norm, v_ffn1_norm, v_ffn1_w_gate, v_ffn1_w_up, v_ffn1_w_down, v_mix_norm, v_w_in, v_pool_w, v_pool_scale, v_gla_w_a2, v_gla_b_a, v_gla_head_norm, v_w_out, v_xattn_norm, v_mem_norm, v_xattn_w_q, v_xattn_w_kv, v_xattn_w_o, v_ffn2_norm, v_ffn2_w_gate, v_ffn2_w_up, v_ffn2_w_down, v_final_norm):
    given = dict(x=x, mem=mem, ffn1_norm=ffn1_norm, ffn1_w_gate=ffn1_w_gate, ffn1_w_up=ffn1_w_up, ffn1_w_down=ffn1_w_down, mix_norm=mix_norm, w_in=w_in, pool_w=pool_w, pool_scale=pool_scale, gla_w_a2=gla_w_a2, gla_b_a=gla_b_a, gla_head_norm=gla_head_norm, w_out=w_out, xattn_norm=xattn_norm, mem_norm=mem_norm, xattn_w_q=xattn_w_q, xattn_w_kv=xattn_w_kv, xattn_w_o=xattn_w_o, ffn2_norm=ffn2_norm, ffn2_w_gate=ffn2_w_gate, ffn2_w_up=ffn2_w_up, ffn2_w_down=ffn2_w_down, final_norm=final_norm, loss_target=loss_target, m_ffn1_norm=m_ffn1_norm, m_ffn1_w_gate=m_ffn1_w_gate, m_ffn1_w_up=m_ffn1_w_up, m_ffn1_w_down=m_ffn1_w_down, m_mix_norm=m_mix_norm, m_w_in=m_w_in, m_pool_w=m_pool_w, m_pool_scale=m_pool_scale, m_gla_w_a2=m_gla_w_a2, m_gla_b_a=m_gla_b_a, m_gla_head_norm=m_gla_head_norm, m_w_out=m_w_out, m_xattn_norm=m_xattn_norm, m_mem_norm=m_mem_norm, m_xattn_w_q=m_xattn_w_q, m_xattn_w_kv=m_xattn_w_kv, m_xattn_w_o=m_xattn_w_o, m_ffn2_norm=m_ffn2_norm, m_ffn2_w_gate=m_ffn2_w_gate, m_ffn2_w_up=m_ffn2_w_up, m_ffn2_w_down=m_ffn2_w_down, m_final_norm=m_final_norm, v_ffn1_norm=v_ffn1_norm, v_ffn1_w_gate=v_ffn1_w_gate, v_ffn1_w_up=v_ffn1_w_up, v_ffn1_w_down=v_ffn1_w_down, v_mix_norm=v_mix_norm, v_w_in=v_w_in, v_pool_w=v_pool_w, v_pool_scale=v_pool_scale, v_gla_w_a2=v_gla_w_a2, v_gla_b_a=v_gla_b_a, v_gla_head_norm=v_gla_head_norm, v_w_out=v_w_out, v_xattn_norm=v_xattn_norm, v_mem_norm=v_mem_norm, v_xattn_w_q=v_xattn_w_q, v_xattn_w_kv=v_xattn_w_kv, v_xattn_w_o=v_xattn_w_o, v_ffn2_norm=v_ffn2_norm, v_ffn2_w_gate=v_ffn2_w_gate, v_ffn2_w_up=v_ffn2_w_up, v_ffn2_w_down=v_ffn2_w_down, v_final_norm=v_final_norm)
    weights = {n: given[n] for n in TWIN_WEIGHTS}
    shared = {n: given[n] for n in SHARED_INPUTS}
    per_example = {n: given[n] for n in ['x', 'mem']}
    grad_fn = _jax.value_and_grad(_loss, argnums=(0, 1))

    def one_microbatch(ex, loss_target):
        ex = dict(ex)
        diff = ex.pop(TWIN_DIFF_INPUT)
        return grad_fn(weights, diff, {**shared, **ex}, loss_target)

    if N_MICROBATCH == 1:
        loss, (grad_w, grad_x) = one_microbatch(per_example, given["loss_target"])
    else:
        def body(carry, xs):
            loss_sum, grad_sum = carry
            l_k, (gw_k, gx_k) = one_microbatch(xs[0], xs[1])
            with _jax.named_scope("update"):
                return (loss_sum + l_k, _jax.tree.map(_jnp.add, grad_sum, gw_k)), gx_k

        init = (_jnp.zeros((), _jnp.float32), _jax.tree.map(_jnp.zeros_like, weights))
        (loss, grad_w), grad_x = _jax.lax.scan(body, init, (per_example, given["loss_target"]))
    with _jax.named_scope("update"):
        delta_w, new_m, new_v = {}, {}, {}
        for n in TWIN_WEIGHTS:
            delta_w[n], new_m[n], new_v[n] = _adamw(weights[n], grad_w[n], given["m_" + n], given["v_" + n])
    return (loss, grad_x, *[grad_w[n] for n in TWIN_WEIGHTS], *[delta_w[n] for n in TWIN_WEIGHTS],
            *[new_m[n] for n in TWIN_WEIGHTS], *[new_v[n] for n in TWIN_WEIGHTS])
```

```python
import functools

import jax
import jax.numpy as jnp
from jax import lax
from jax.experimental import pallas as pl
from jax.experimental.pallas import tpu as pltpu

F32 = jnp.float32
BF16 = jnp.bfloat16
MESH = pl.DeviceIdType.MESH

RMS_EPS = 1e-6
CHUNK = 64
POOL_WINDOWS = (2, 4, 8, 16)
POOL_HALO = 16
N_HEADS = 4
GATE_TEMP = 16.0
ADAM_LR, ADAM_B1, ADAM_B2, ADAM_EPS, ADAM_WD, ADAM_STEP = 0.001, 0.9, 0.999, 1e-08, 0.01, 10
N_SHARDS = 4
LANES = 128
VMEM_LIMIT = 58 * 1024 * 1024

ANY = pl.BlockSpec(memory_space=pl.ANY)


def _params(**kw):
    return pltpu.CompilerParams(vmem_limit_bytes=VMEM_LIMIT, **kw)


def _tile(n, want):
    for unit in (LANES, 8):
        t = (min(want, n) // unit) * unit
        while t >= unit:
            if n % t == 0:
                return t
            t -= unit
    return n


def _dot(a, b, dims):
    return lax.dot_general(a, b, (dims, ((), ())), preferred_element_type=F32)


def _nn(a, b):
    return _dot(a, b, ((1,), (0,)))


def _nt(a, b):
    return _dot(a, b, ((1,), (1,)))


def _tn(a, b):
    return _dot(a, b, ((0,), (0,)))


def _sigmoid(x):
    return 1.0 / (1.0 + jnp.exp(-x))


def _matmul(a, b, *, mode, name, out_dtype, tm=512, tn=2048, tk=2048, res=None, scale=1.0, b_groups=False, out_groups=0):
    if mode == "tn":
        K, M = a.shape
    else:
        M, K = a.shape
    if mode == "nn":
        if b_groups:
            G, _, Nj = b.shape
            N = G * Nj
        else:
            N = b.shape[1]
    elif mode == "nt":
        if b_groups:
            G, N, Kj = b.shape
            assert G * Kj == K
        else:
            N = b.shape[0]
    else:
        N = b.shape[1]
    tm = _tile(M, tm)
    if mode == "nn" and b_groups:
        tn = _tile(Nj, tn)
    elif out_groups:
        tn = _tile(N // out_groups, tn)
    else:
        tn = _tile(N, tn)
    if mode == "nt" and b_groups:
        tk = _tile(Kj, tk)
    else:
        tk = _tile(K, tk)
    nk = K // tk
    grid = (M // tm, N // tn, nk)

    if mode == "tn":
        a_spec = pl.BlockSpec((tk, tm), lambda i, j, k: (k, i))
        b_spec = pl.BlockSpec((tk, tn), lambda i, j, k: (k, j))
        dims = ((0,), (0,))
    elif mode == "nn":
        a_spec = pl.BlockSpec((tm, tk), lambda i, j, k: (i, k))
        if b_groups:
            npj = Nj // tn
            b_spec = pl.BlockSpec((None, tk, tn), lambda i, j, k: (j // npj, k, j % npj))
        else:
            b_spec = pl.BlockSpec((tk, tn), lambda i, j, k: (k, j))
        dims = ((1,), (0,))
    else:
        a_spec = pl.BlockSpec((tm, tk), lambda i, j, k: (i, k))
        if b_groups:
            kpj = Kj // tk
            b_spec = pl.BlockSpec((None, tn, tk), lambda i, j, k: (k // kpj, j, k % kpj))
        else:
            b_spec = pl.BlockSpec((tn, tk), lambda i, j, k: (j, k))
        dims = ((1,), (1,))
    if out_groups:
        npj = (N // out_groups) // tn
        o_spec = pl.BlockSpec((None, tm, tn), lambda i, j, k: (j // npj, i, j % npj))
        out_shape = jax.ShapeDtypeStruct((out_groups, M, N // out_groups), out_dtype)
    else:
        o_spec = pl.BlockSpec((tm, tn), lambda i, j, k: (i, j))
        out_shape = jax.ShapeDtypeStruct((M, N), out_dtype)
    in_specs = [a_spec, b_spec]
    operands = [a, b]
    if res is not None:
        in_specs.append(pl.BlockSpec((tm, tn), lambda i, j, k: (i, j)))
        operands.append(res)
    has_res = res is not None

    def body(*refs):
        if has_res:
            a_ref, b_ref, r_ref, o_ref = refs[:4]
        else:
            a_ref, b_ref, o_ref = refs[:3]
            r_ref = None

        def finish(acc):
            if scale != 1.0:
                acc = acc * scale
            if r_ref is not None:
                acc = r_ref[...] + acc
            o_ref[...] = acc.astype(o_ref.dtype)

        part = _dot(a_ref[...], b_ref[...], dims)
        if nk == 1:
            finish(part)
        else:
            acc_ref = refs[-1]
            k = pl.program_id(2)

            @pl.when(k == 0)
            def _():
                acc_ref[...] = part

            @pl.when(k > 0)
            def _():
                acc_ref[...] += part

            @pl.when(k == nk - 1)
            def _():
                finish(acc_ref[...])

    scratch = [] if nk == 1 else [pltpu.VMEM((tm, tn), F32)]
    return pl.pallas_call(body, name=name, grid=grid, in_specs=in_specs, out_specs=o_spec, out_shape=out_shape,
                          scratch_shapes=scratch, compiler_params=_params())(*operands)


def _rms_fwd(x, gain, name, tm=256):
    S, D = x.shape
    tm = _tile(S, tm)

    def body(x_ref, g_ref, o_ref):
        xv = x_ref[...]
        r = lax.rsqrt(jnp.mean(xv * xv, axis=-1, keepdims=True) + RMS_EPS)
        o_ref[...] = (xv * r * g_ref[...]).astype(o_ref.dtype)

    return pl.pallas_call(body, name=name, grid=(S // tm,),
                          in_specs=[pl.BlockSpec((tm, D), lambda i: (i, 0)), pl.BlockSpec((1, D), lambda i: (0, 0))],
                          out_specs=pl.BlockSpec((tm, D), lambda i: (i, 0)),
                          out_shape=jax.ShapeDtypeStruct((S, D), BF16), compiler_params=_params())(x, gain)


def _rms_bwd(x, gain, dh, dres, name, half=False, tm=256):
    S, D = x.shape
    tm = _tile(S, tm)
    has_res = dres is not None

    def body(*refs):
        if has_res:
            x_ref, g_ref, dh_ref, dr_ref = refs[:4]
            outs = refs[4:]
        else:
            x_ref, g_ref, dh_ref = refs[:3]
            dr_ref = None
            outs = refs[3:]
        dx_ref, dg_ref = outs[0], outs[-1]
        xv = x_ref[...]
        dhv = dh_ref[...].astype(F32)
        r = lax.rsqrt(jnp.mean(xv * xv, axis=-1, keepdims=True) + RMS_EPS)
        gy = dhv * g_ref[...]
        dx = r * gy - xv * (r * r * r) * jnp.mean(gy * xv, axis=-1, keepdims=True)
        if dr_ref is not None:
            dx = dx + dr_ref[...]
        dx_ref[...] = dx
        if half:
            outs[1][...] = (0.5 * dx).astype(BF16)
        part = jnp.sum(dhv * xv * r, axis=0, keepdims=True)

        @pl.when(pl.program_id(0) == 0)
        def _():
            dg_ref[...] = part

        @pl.when(pl.program_id(0) > 0)
        def _():
            dg_ref[...] += part

    row = pl.BlockSpec((tm, D), lambda i: (i, 0))
    vec = pl.BlockSpec((1, D), lambda i: (0, 0))
    in_specs = [row, vec, row] + ([row] if has_res else [])
    operands = [x, gain, dh] + ([dres] if has_res else [])
    out_specs = [row] + ([row] if half else []) + [vec]
    out_shape = [jax.ShapeDtypeStruct((S, D), F32)] + ([jax.ShapeDtypeStruct((S, D), BF16)] if half else []) + [
        jax.ShapeDtypeStruct((1, D), F32)]
    return pl.pallas_call(body, name=name, grid=(S // tm,), in_specs=in_specs, out_specs=out_specs, out_shape=out_shape,
                          compiler_params=_params())(*operands)


def _loss_head(x, gain, target, name, tm=256):
    S, D = x.shape
    tm = _tile(S, tm)

    def body(x_ref, g_ref, t_ref, sq_ref, dx_ref, dxh_ref, dg_ref):
        xv = x_ref[...]
        r = lax.rsqrt(jnp.mean(xv * xv, axis=-1, keepdims=True) + RMS_EPS)
        xn = xv * r
        err = xn * g_ref[...] - t_ref[...]
        dout = err * (1.0 / D)
        gy = dout * g_ref[...]
        dx = r * gy - xv * (r * r * r) * jnp.mean(gy * xv, axis=-1, keepdims=True)
        dx_ref[...] = dx
        dxh_ref[...] = (0.5 * dx).astype(BF16)
        sq = jnp.sum(err * err, axis=0, keepdims=True)
        dg = jnp.sum(dout * xn, axis=0, keepdims=True)

        @pl.when(pl.program_id(0) == 0)
        def _():
            sq_ref[...] = sq
            dg_ref[...] = dg

        @pl.when(pl.program_id(0) > 0)
        def _():
            sq_ref[...] += sq
            dg_ref[...] += dg

    row = pl.BlockSpec((tm, D), lambda i: (i, 0))
    vec = pl.BlockSpec((1, D), lambda i: (0, 0))
    return pl.pallas_call(body, name=name, grid=(S // tm,), in_specs=[row, vec, row], out_specs=[vec, row, row, vec],
                          out_shape=[jax.ShapeDtypeStruct((1, D), F32), jax.ShapeDtypeStruct((S, D), F32),
                                     jax.ShapeDtypeStruct((S, D), BF16), jax.ShapeDtypeStruct((1, D), F32)],
                          compiler_params=_params())(x, gain, target)


def _cast(x, dtype, name, scale=1.0, tm=256):
    S, D = x.shape
    tm = _tile(S, tm)

    def body(x_ref, o_ref):
        o_ref[...] = (x_ref[...] * scale).astype(o_ref.dtype)

    row = pl.BlockSpec((tm, D), lambda i: (i, 0))
    return pl.pallas_call(body, name=name, grid=(S // tm,), in_specs=[row], out_specs=row,
                          out_shape=jax.ShapeDtypeStruct((S, D), dtype), compiler_params=_params())(x)


def _ffn_up(h, wg, wu, name, tm=512):
    S, D = h.shape
    G, _, Fj = wg.shape
    tm = _tile(S, tm)

    def body(h_ref, wg_ref, wu_ref, a_ref, b_ref, hid_ref):
        hv = h_ref[...]
        a = _nn(hv, wg_ref[...])
        b = _nn(hv, wu_ref[...])
        a_ref[...] = a.astype(BF16)
        b_ref[...] = b.astype(BF16)
        hid_ref[...] = (a * _sigmoid(a) * b).astype(BF16)

    w_spec = pl.BlockSpec((None, D, Fj), lambda g, i: (g, 0, 0))
    o_spec = pl.BlockSpec((tm, Fj), lambda g, i: (i, g))
    out = jax.ShapeDtypeStruct((S, G * Fj), BF16)
    return pl.pallas_call(body, name=name, grid=(G, S // tm),
                          in_specs=[pl.BlockSpec((tm, D), lambda g, i: (i, 0)), w_spec, w_spec],
                          out_specs=[o_spec, o_spec, o_spec], out_shape=[out, out, out], compiler_params=_params())(h, wg, wu)


def _ffn_dact(dxh, wd, a, b, name, tm=512):
    S, D = dxh.shape
    G, Fj, _ = wd.shape
    tm = _tile(S, tm)

    def body(dx_ref, wd_ref, a_ref, b_ref, da_ref, db_ref, hid_ref):
        dhid = _nt(dx_ref[...], wd_ref[...])
        av = a_ref[...].astype(F32)
        bv = b_ref[...].astype(F32)
        s = _sigmoid(av)
        silu = av * s
        da_ref[...] = (dhid * bv * (s * (1.0 + av * (1.0 - s)))).astype(BF16)
        db_ref[...] = (dhid * silu).astype(BF16)
        hid_ref[...] = (silu * bv).astype(BF16)

    blk = pl.BlockSpec((tm, Fj), lambda g, i: (i, g))
    out = jax.ShapeDtypeStruct((S, G * Fj), BF16)
    return pl.pallas_call(body, name=name, grid=(G, S // tm),
                          in_specs=[pl.BlockSpec((tm, D), lambda g, i: (i, 0)),
                                    pl.BlockSpec((None, Fj, D), lambda g, i: (g, 0, 0)), blk, blk],
                          out_specs=[blk, blk, blk], out_shape=[out, out, out], compiler_params=_params())(dxh, wd, a, b)


def _pool_fwd(proj, pool_w, pool_scale, name, tm=512):
    S = proj.shape[0]
    NG, C, _ = pool_w.shape
    DP = NG * C
    tm = _tile(S, tm)
    hb = tm // POOL_HALO
    n_ext = tm + POOL_HALO

    def body(u_ref, halo_ref, w_ref, sc_ref, y_ref, d_ref):
        i = pl.program_id(0)
        t = lax.broadcasted_iota(jnp.int32, (tm, 1), 0) + i * tm
        for g, win in enumerate(POOL_WINDOWS):
            cols = slice(g * C, (g + 1) * C)
            ug = u_ref[:, cols]
            halo = jnp.where(i > 0, halo_ref[:, cols], 0.0)
            acc = jnp.concatenate([halo, ug], axis=0)
            step = 1
            while step < win:
                acc = acc + pltpu.roll(acc, step, 0)
                step *= 2
            count = jnp.minimum(t + 1, win).astype(F32)
            d = (acc[POOL_HALO:, :] / count - ug).astype(BF16)
            d_ref[:, cols] = d
            y_ref[:, cols] = (_nn(d, w_ref[g]) * sc_ref[:, cols]).astype(BF16)

    del n_ext
    return pl.pallas_call(
        body, name=name, grid=(S // tm,),
        in_specs=[pl.BlockSpec((tm, DP), lambda i: (i, 0)),
                  pl.BlockSpec((POOL_HALO, DP), lambda i: (jnp.maximum(i * hb - 1, 0), 0)),
                  pl.BlockSpec((NG, C, C), lambda i: (0, 0, 0)), pl.BlockSpec((1, DP), lambda i: (0, 0))],
        out_specs=[pl.BlockSpec((tm, DP), lambda i: (i, 0)), pl.BlockSpec((tm, DP), lambda i: (i, 0))],
        out_shape=[jax.ShapeDtypeStruct((S, DP), BF16), jax.ShapeDtypeStruct((S, DP), BF16)],
        compiler_params=_params())(proj, proj, pool_w, pool_scale)


def _pool_bwd(dymix, d, pool_w, pool_scale, name, tm=512):
    S = dymix.shape[0]
    NG, C, _ = pool_w.shape
    DP = NG * C
    tm = _tile(S, tm)
    hb = tm // POOL_HALO
    nb = S // tm
    n_ext = tm + POOL_HALO
    last_halo = S // POOL_HALO - 1

    def body(dy_ref, halo_ref, d_ref, w_ref, sc_ref, du_ref, dw_ref, dsc_ref):
        i = pl.program_id(0)
        t = lax.broadcasted_iota(jnp.int32, (n_ext, 1), 0) + i * tm
        for g, win in enumerate(POOL_WINDOWS):
            cols = slice(g * C, (g + 1) * C)
            dy = dy_ref[:, cols]
            halo = jnp.where(i < nb - 1, halo_ref[:, cols], 0.0)
            sc = sc_ref[:, cols]
            dv = d_ref[:, cols]
            e_ext = (jnp.concatenate([dy, halo], axis=0) * sc).astype(BF16)
            dd = _nt(e_ext, w_ref[g])
            count = jnp.minimum(t + 1, win).astype(F32)
            acc = dd / count
            step = 1
            while step < win:
                acc = acc + pltpu.roll(acc, n_ext - step, 0)
                step *= 2
            du_ref[:, cols] = (acc[:tm, :] - dd[:tm, :]).astype(BF16)
            dw = _tn(dv, e_ext[:tm, :])
            dsc = jnp.sum(dy * _nn(dv, w_ref[g]), axis=0, keepdims=True)

            @pl.when(i == 0)
            def _():
                dw_ref[g] = dw
                dsc_ref[:, cols] = dsc

            @pl.when(i > 0)
            def _():
                dw_ref[g] += dw
                dsc_ref[:, cols] += dsc

    return pl.pallas_call(
        body, name=name, grid=(nb,),
        in_specs=[pl.BlockSpec((tm, DP), lambda i: (i, 0)),
                  pl.BlockSpec((POOL_HALO, DP), lambda i: (jnp.minimum((i + 1) * hb, last_halo), 0)),
                  pl.BlockSpec((tm, DP), lambda i: (i, 0)),
                  pl.BlockSpec((NG, C, C), lambda i: (0, 0, 0)), pl.BlockSpec((1, DP), lambda i: (0, 0))],
        out_specs=[pl.BlockSpec((tm, DP), lambda i: (i, 0)), pl.BlockSpec((NG, C, C), lambda i: (0, 0, 0)),
                   pl.BlockSpec((1, DP), lambda i: (0, 0))],
        out_shape=[jax.ShapeDtypeStruct((S, DP), BF16), jax.ShapeDtypeStruct((NG, C, C), F32),
                   jax.ShapeDtypeStruct((1, DP), F32)],
        compiler_params=_params())(dymix, dymix, d, pool_w, pool_scale)


def _chunk_scan(v, rows, reverse):
    n = v.shape[0]
    step = 1
    while step < CHUNK:
        if reverse:
            v = v + jnp.where(rows < CHUNK - step, pltpu.roll(v, n - step, 0), 0.0)
        else:
            v = v + jnp.where(rows >= step, pltpu.roll(v, step, 0), 0.0)
        step *= 2
    return v


def _log_decay(alr, w_a2, b_a):
    z = _nn(alr.astype(BF16), w_a2) + b_a
    la = (jnp.minimum(z, 0.0) - jnp.log(1.0 + jnp.exp(-jnp.abs(z)))) * (1.0 / GATE_TEMP)
    return z, la


def _gla_specs(DP, DKT, DV, tb, bmap):
    return [pl.BlockSpec((tb, DKT), lambda i: (bmap(i), DP // DKT)),
            pl.BlockSpec((tb, DKT), lambda i: (bmap(i), DP // DKT + 1)),
            pl.BlockSpec((tb, DV), lambda i: (bmap(i), (DP + 2 * DKT) // DV)),
            pl.BlockSpec((tb, DV), lambda i: (bmap(i), (DP + 2 * DKT) // DV + 1)),
            pl.BlockSpec((tb, LANES), lambda i: (bmap(i), (DP + 2 * DKT + 2 * DV) // LANES))]


def _gla_fwd(proj, w_a2, b_a, head_norm, DP, name, tb=512):
    S = proj.shape[0]
    DKT = b_a.shape[1]
    DV = head_norm.shape[1]
    dk, dv = DKT // N_HEADS, DV // N_HEADS
    tb = _tile(S, tb)
    ncb = tb // CHUNK
    qscale = dk ** -0.5

    def body(q_ref, k_ref, v_ref, g_ref, alr_ref, wa_ref, ba_ref, hn_ref, y_ref, st_out_ref, st_ref, kdec_ref, gam_ref):
        @pl.when(pl.program_id(0) == 0)
        def _():
            st_ref[...] = jnp.zeros_like(st_ref)

        rows = lax.broadcasted_iota(jnp.int32, (tb, 1), 0) % CHUNK
        _, la = _log_decay(alr_ref[...], wa_ref[...], ba_ref[...])
        tail = _chunk_scan(la, rows, True)
        kdec_ref[...] = k_ref[...] * jnp.exp(tail - la)
        gam_ref[...] = jnp.exp(tail)

        def chunk(c, carry):
            r0 = pl.multiple_of(c * CHUNK, CHUNK)
            rs = pl.ds(r0, CHUNK)
            gam = gam_ref[pl.ds(r0, 1), :]
            for h in range(N_HEADS):
                kc = slice(h * dk, (h + 1) * dk)
                vc = slice(h * dv, (h + 1) * dv)
                st = st_ref[h] * gam[:, kc] + _tn(v_ref[rs, vc].astype(BF16), kdec_ref[rs, kc].astype(BF16))
                st_ref[h] = st
                st_out_ref[c, h] = st
                o = _nt((q_ref[rs, kc] * qscale).astype(BF16), st.astype(BF16))
                r = lax.rsqrt(jnp.mean(o * o, axis=-1, keepdims=True) + RMS_EPS)
                gv = g_ref[rs, vc]
                y_ref[rs, vc] = (o * r * hn_ref[:, vc] * (gv * _sigmoid(gv))).astype(BF16)
            return carry

        lax.fori_loop(0, ncb, chunk, 0)

    full = lambda shape: pl.BlockSpec(shape, lambda i: (0,) * len(shape))
    return pl.pallas_call(
        body, name=name, grid=(S // tb,),
        in_specs=_gla_specs(DP, DKT, DV, tb, lambda i: i) + [full((LANES, DKT)), full((1, DKT)), full((1, DV))],
        out_specs=[pl.BlockSpec((tb, DV), lambda i: (i, 0)), pl.BlockSpec((ncb, N_HEADS, dv, dk), lambda i: (i, 0, 0, 0))],
        out_shape=[jax.ShapeDtypeStruct((S, DV), BF16), jax.ShapeDtypeStruct((S // CHUNK, N_HEADS, dv, dk), F32)],
        scratch_shapes=[pltpu.VMEM((N_HEADS, dv, dk), F32), pltpu.VMEM((tb, DKT), F32), pltpu.VMEM((tb, DKT), F32)],
        compiler_params=_params())(proj, proj, proj, proj, proj, w_a2, b_a, head_norm)


def _gla_bwd(proj, states, dymix, w_a2, b_a, head_norm, DP, name, tb=512):
    S = proj.shape[0]
    DKT = b_a.shape[1]
    DV = head_norm.shape[1]
    dk, dv = DKT // N_HEADS, DV // N_HEADS
    tb = _tile(S, tb)
    ncb = tb // CHUNK
    nb = S // tb
    qscale = dk ** -0.5
    rev = lambda i: nb - 1 - i

    def body(q_ref, k_ref, v_ref, g_ref, alr_ref, st_blk_ref, st_prev_ref, dy_ref, wa_ref, ba_ref, hn_ref,
             dq_ref, dk_ref, dv_ref, dg_ref, dalr_ref, dwa_ref, dba_ref, dhn_ref,
             dst_ref, kdec_ref, dec_ref, gam_ref, e_ref, dla_ref, dhn_acc_ref):
        i = pl.program_id(0)
        blk = rev(i)

        @pl.when(i == 0)
        def _():
            dst_ref[...] = jnp.zeros_like(dst_ref)

        dhn_acc_ref[...] = jnp.zeros_like(dhn_acc_ref)
        rows = lax.broadcasted_iota(jnp.int32, (tb, 1), 0) % CHUNK
        z, la = _log_decay(alr_ref[...], wa_ref[...], ba_ref[...])
        tail = _chunk_scan(la, rows, True)
        dec_ref[...] = jnp.exp(tail - la)
        kdec_ref[...] = k_ref[...] * dec_ref[...]
        gam_ref[...] = jnp.exp(tail)

        def chunk(cc, carry):
            c = ncb - 1 - cc
            r0 = pl.multiple_of(c * CHUNK, CHUNK)
            rs = pl.ds(r0, CHUNK)
            gam = gam_ref[pl.ds(r0, 1), :]
            first = jnp.logical_and(blk == 0, c == 0)
            gdg = []
            for h in range(N_HEADS):
                kc = slice(h * dk, (h + 1) * dk)
                vc = slice(h * dv, (h + 1) * dv)
                st = st_blk_ref[c, h]
                st_prev = jnp.where(c > 0, st_blk_ref[jnp.maximum(c - 1, 0), h], st_prev_ref[0, h])
                st_prev = jnp.where(first, 0.0, st_prev)
                qs = (q_ref[rs, kc] * qscale).astype(BF16)
                stb = st.astype(BF16)
                o = _nt(qs, stb)
                r = lax.rsqrt(jnp.mean(o * o, axis=-1, keepdims=True) + RMS_EPS)
                gv = g_ref[rs, vc]
                sg = _sigmoid(gv)
                dy = dy_ref[rs, vc]
                hn = hn_ref[:, vc]
                on = o * r
                dg_ref[rs, vc] = (dy * on * hn * (sg * (1.0 + gv * (1.0 - sg)))).astype(BF16)
                don = dy * (gv * sg)
                dhn_acc_ref[:, vc] += jnp.sum(don * on, axis=0, keepdims=True)
                dn = don * hn
                do = (r * dn - o * (r * r * r) * jnp.mean(dn * o, axis=-1, keepdims=True)).astype(BF16)
                dq_ref[rs, kc] = (_nn(do, stb) * qscale).astype(BF16)
                dst = dst_ref[h] + _tn(do, qs)
                dstb = dst.astype(BF16)
                kdec = kdec_ref[rs, kc]
                dv_ref[rs, vc] = _nt(kdec.astype(BF16), dstb).astype(BF16)
                dkdec = _nn(v_ref[rs, vc].astype(BF16), dstb)
                dk_ref[rs, kc] = (dkdec * dec_ref[rs, kc]).astype(BF16)
                e_ref[rs, kc] = dkdec * kdec
                gdg.append(jnp.sum(dst * st_prev, axis=0, keepdims=True) * gam[:, kc])
                dst_ref[h] = dst * gam[:, kc]
            dla_ref[rs, :] = jnp.broadcast_to(jnp.concatenate(gdg, axis=1), (CHUNK, DKT))
            return carry

        lax.fori_loop(0, ncb, chunk, 0)

        ev = e_ref[...]
        dla = dla_ref[...] + _chunk_scan(ev, rows, False) - ev
        dz = dla * (1.0 / GATE_TEMP) * (1.0 - _sigmoid(z))
        dzb = dz.astype(BF16)
        dalr_ref[...] = _nt(dzb, wa_ref[...]).astype(BF16)
        dwa = _tn(alr_ref[...].astype(BF16), dzb)
        dba = jnp.sum(dz, axis=0, keepdims=True)

        @pl.when(i == 0)
        def _():
            dwa_ref[...] = dwa
            dba_ref[...] = dba
            dhn_ref[...] = dhn_acc_ref[...]

        @pl.when(i > 0)
        def _():
            dwa_ref[...] += dwa
            dba_ref[...] += dba
            dhn_ref[...] += dhn_acc_ref[...]

    full = lambda shape: pl.BlockSpec(shape, lambda i: (0,) * len(shape))
    rowblk = lambda w: pl.BlockSpec((tb, w), lambda i: (rev(i), 0))
    return pl.pallas_call(
        body, name=name, grid=(nb,),
        in_specs=_gla_specs(DP, DKT, DV, tb, rev) + [
            pl.BlockSpec((ncb, N_HEADS, dv, dk), lambda i: (rev(i), 0, 0, 0)),
            pl.BlockSpec((1, N_HEADS, dv, dk), lambda i: (jnp.maximum(rev(i) * ncb - 1, 0), 0, 0, 0)),
            pl.BlockSpec((tb, DV), lambda i: (rev(i), DP // DV)),
            full((LANES, DKT)), full((1, DKT)), full((1, DV))],
        out_specs=[rowblk(DKT), rowblk(DKT), rowblk(DV), rowblk(DV), rowblk(LANES),
                   full((LANES, DKT)), full((1, DKT)), full((1, DV))],
        out_shape=[jax.ShapeDtypeStruct((S, DKT), BF16), jax.ShapeDtypeStruct((S, DKT), BF16),
                   jax.ShapeDtypeStruct((S, DV), BF16), jax.ShapeDtypeStruct((S, DV), BF16),
                   jax.ShapeDtypeStruct((S, LANES), BF16), jax.ShapeDtypeStruct((LANES, DKT), F32),
                   jax.ShapeDtypeStruct((1, DKT), F32), jax.ShapeDtypeStruct((1, DV), F32)],
        scratch_shapes=[pltpu.VMEM((N_HEADS, dv, dk), F32)] + [pltpu.VMEM((tb, DKT), F32)] * 5 + [pltpu.VMEM((1, DV), F32)],
        compiler_params=_params())(proj, proj, proj, proj, proj, states, states, dymix, w_a2, b_a, head_norm)


def _xattn_fwd(q, kv, name, tm=512):
    S, D = q.shape
    M = kv.shape[0]
    hd = D // N_HEADS
    tm = _tile(S, tm)
    scale = hd ** -0.5

    def body(q_ref, k_ref, v_ref, o_ref):
        for h in range(N_HEADS):
            hc = slice(h * hd, (h + 1) * hd)
            s = _nt(q_ref[:, hc], k_ref[:, hc]) * scale
            p = jnp.exp(s - jnp.max(s, axis=-1, keepdims=True))
            p = p / jnp.sum(p, axis=-1, keepdims=True)
            o_ref[:, hc] = _nn(p.astype(BF16), v_ref[:, hc]).astype(BF16)

    return pl.pallas_call(body, name=name, grid=(S // tm,),
                          in_specs=[pl.BlockSpec((tm, D), lambda i: (i, 0)), pl.BlockSpec((M, D), lambda i: (0, 0)),
                                    pl.BlockSpec((M, D), lambda i: (0, 1))],
                          out_specs=pl.BlockSpec((tm, D), lambda i: (i, 0)),
                          out_shape=jax.ShapeDtypeStruct((S, D), BF16), compiler_params=_params())(q, kv, kv)


def _xattn_bwd(q, kv, do, name, tm=512):
    S, D = q.shape
    M = kv.shape[0]
    hd = D // N_HEADS
    tm = _tile(S, tm)
    scale = hd ** -0.5

    def body(q_ref, k_ref, v_ref, do_ref, dq_ref, dkv_ref):
        first = pl.program_id(0) == 0
        for h in range(N_HEADS):
            hc = slice(h * hd, (h + 1) * hd)
            vcols = slice(D + h * hd, D + (h + 1) * hd)
            qh = q_ref[:, hc]
            kh = k_ref[:, hc]
            doh = do_ref[:, hc]
            s = _nt(qh, kh) * scale
            p = jnp.exp(s - jnp.max(s, axis=-1, keepdims=True))
            p = p / jnp.sum(p, axis=-1, keepdims=True)
            dvh = _tn(p.astype(BF16), doh)
            dp = _nt(doh, v_ref[:, hc])
            ds = ((p * (dp - jnp.sum(dp * p, axis=-1, keepdims=True))) * scale).astype(BF16)
            dq_ref[:, hc] = _nn(ds, kh).astype(BF16)
            dkh = _tn(ds, qh)

            @pl.when(first)
            def _():
                dkv_ref[:, hc] = dkh
                dkv_ref[:, vcols] = dvh

            @pl.when(jnp.logical_not(first))
            def _():
                dkv_ref[:, hc] += dkh
                dkv_ref[:, vcols] += dvh

    row = pl.BlockSpec((tm, D), lambda i: (i, 0))
    return pl.pallas_call(body, name=name, grid=(S // tm,),
                          in_specs=[row, pl.BlockSpec((M, D), lambda i: (0, 0)), pl.BlockSpec((M, D), lambda i: (0, 1)), row],
                          out_specs=[row, pl.BlockSpec((M, 2 * D), lambda i: (0, 0))],
                          out_shape=[jax.ShapeDtypeStruct((S, D), BF16), jax.ShapeDtypeStruct((M, 2 * D), F32)],
                          compiler_params=_params())(q, kv, kv, do)


def _ffn_fwd(x, gain, wg, wu, wd, tag):
    h = _rms_fwd(x, gain, f"{tag}_norm")
    a, b, hid = _ffn_up(h, wg, wu, f"{tag}_up")
    G, Fj, D = wd.shape
    x_out = _matmul(hid, wd.reshape(G * Fj, D), mode="nn", name=f"{tag}_down", out_dtype=F32, res=x, scale=0.5, tk=Fj)
    return x_out, (h, a, b)


def _ffn_bwd(dx, dxh, x_in, gain, wg, wu, wd, saved, tag):
    h, a, b = saved
    G, Fj, D = wd.shape
    da, db, hid = _ffn_dact(dxh, wd, a, b, f"{tag}_dact")
    dwd = _matmul(hid, dxh, mode="tn", name=f"{tag}_dwd", out_dtype=F32, tm=Fj, tn=1024, tk=1024)
    dwg = _matmul(h, da, mode="tn", name=f"{tag}_dwg", out_dtype=F32, tm=1024, tn=Fj, tk=1024, out_groups=G)
    dwu = _matmul(h, db, mode="tn", name=f"{tag}_dwu", out_dtype=F32, tm=1024, tn=Fj, tk=1024, out_groups=G)
    dh = _matmul(da, wg, mode="nt", name=f"{tag}_dh_gate", out_dtype=F32, tk=Fj, b_groups=True)
    dh = _matmul(db, wu, mode="nt", name=f"{tag}_dh_up", out_dtype=F32, tk=Fj, b_groups=True, res=dh)
    return dh, dwd, dwg, dwu


def _local_step(x, mem, target, w):
    S, D = x.shape
    DP = w["pool_scale"].shape[1]
    DV = w["gla_head_norm"].shape[1]
    g = {}

    x1, ffn1_saved = _ffn_fwd(x, w["ffn1_norm"], w["ffn1_w_gate"], w["ffn1_w_up"], w["ffn1_w_down"], "ffn1")
    h2 = _rms_fwd(x1, w["mix_norm"], "mix_norm")
    proj = _matmul(h2, w["w_in"], mode="nn", name="w_in", out_dtype=F32, tn=1408)
    y_pool, dpool = _pool_fwd(proj, w["pool_w"], w["pool_scale"], "pool_fwd")
    y_gla, states = _gla_fwd(proj, w["gla_w_a2"], w["gla_b_a"], w["gla_head_norm"], DP, "gla_fwd")
    ymix = jnp.concatenate([y_pool, y_gla], axis=1)
    x2 = _matmul(ymix, w["w_out"], mode="nn", name="w_out", out_dtype=F32, res=x1)
    h3 = _rms_fwd(x2, w["xattn_norm"], "xattn_norm")
    mh = _rms_fwd(mem, w["mem_norm"], "mem_norm")
    q = _matmul(h3, w["xattn_w_q"], mode="nn", name="xattn_q", out_dtype=BF16)
    kv = _matmul(mh, w["xattn_w_kv"], mode="nn", name="xattn_kv", out_dtype=BF16, b_groups=True, tn=1024)
    o = _xattn_fwd(q, kv, "xattn_fwd")
    x3 = _matmul(o, w["xattn_w_o"], mode="nn", name="xattn_o", out_dtype=F32, res=x2)
    x4, ffn2_saved = _ffn_fwd(x3, w["ffn2_norm"], w["ffn2_w_gate"], w["ffn2_w_up"], w["ffn2_w_down"], "ffn2")
    sq, dx4, dx4h, g["final_norm"] = _loss_head(x4, w["final_norm"], target, "loss_head")

    dh, g["ffn2_w_down"], g["ffn2_w_gate"], g["ffn2_w_up"] = _ffn_bwd(
        dx4, dx4h, x3, w["ffn2_norm"], w["ffn2_w_gate"], w["ffn2_w_up"], w["ffn2_w_down"], ffn2_saved, "ffn2")
    dx3, g["ffn2_norm"] = _rms_bwd(x3, w["ffn2_norm"], dh, dx4, "ffn2_norm_bwd")
    dx3b = _cast(dx3, BF16, "dx3_cast")
    g["xattn_w_o"] = _matmul(o, dx3b, mode="tn", name="xattn_dwo", out_dtype=F32, tm=1024, tn=1024, tk=1024)
    do = _matmul(dx3b, w["xattn_w_o"], mode="nt", name="xattn_do", out_dtype=BF16)
    dq, dkv = _xattn_bwd(q, kv, do, "xattn_bwd")
    g["xattn_w_q"] = _matmul(h3, dq, mode="tn", name="xattn_dwq", out_dtype=F32, tm=1024, tn=1024, tk=1024)
    dh3 = _matmul(dq, w["xattn_w_q"], mode="nt", name="xattn_dh", out_dtype=F32)
    dkvb = _cast(dkv, BF16, "dkv_cast")
    g["xattn_w_kv"] = _matmul(mh, dkvb, mode="tn", name="xattn_dwkv", out_dtype=F32, tm=1024, tn=1024,
                              out_groups=N_SHARDS)
    dmh = _matmul(dkvb, w["xattn_w_kv"], mode="nt", name="xattn_dmh", out_dtype=F32, b_groups=True, tk=1024)
    _, g["mem_norm"] = _rms_bwd(mem, w["mem_norm"], dmh, None, "mem_norm_bwd")
    dx2, g["xattn_norm"] = _rms_bwd(x2, w["xattn_norm"], dh3, dx3, "xattn_norm_bwd")
    dx2b = _cast(dx2, BF16, "dx2_cast")
    g["w_out"] = _matmul(ymix, dx2b, mode="tn", name="dw_out", out_dtype=F32, tm=1024, tn=1024, tk=1024)
    dymix = _matmul(dx2b, w["w_out"], mode="nt", name="dymix", out_dtype=F32)
    du, g["pool_w"], g["pool_scale"] = _pool_bwd(dymix, dpool, w["pool_w"], w["pool_scale"], "pool_bwd")
    dq_g, dk_g, dv_g, dg_g, dalr, g["gla_w_a2"], g["gla_b_a"], g["gla_head_norm"] = _gla_bwd(
        proj, states, dymix, w["gla_w_a2"], w["gla_b_a"], w["gla_head_norm"], DP, "gla_bwd")
    dproj = jnp.concatenate([du, dq_g, dk_g, dv_g, dg_g, dalr], axis=1)
    g["w_in"] = _matmul(h2, dproj, mode="tn", name="dw_in", out_dtype=F32, tm=1024, tn=1408, tk=1024)
    dh2 = _matmul(dproj, w["w_in"], mode="nt", name="dh2", out_dtype=F32, tk=1408)
    dx1, dx1h, g["mix_norm"] = _rms_bwd(x1, w["mix_norm"], dh2, dx2, "mix_norm_bwd", half=True)
    dh, g["ffn1_w_down"], g["ffn1_w_gate"], g["ffn1_w_up"] = _ffn_bwd(
        dx1, dx1h, x, w["ffn1_norm"], w["ffn1_w_gate"], w["ffn1_w_up"], w["ffn1_w_down"], ffn1_saved, "ffn1")
    dx0, g["ffn1_norm"] = _rms_bwd(x, w["ffn1_norm"], dh, dx1, "ffn1_norm_bwd")
    return sq, dx0, g


def _place():
    x, y, c = lax.axis_index("x"), lax.axis_index("y"), lax.axis_index("c")
    chips = [(1 - x, y), (x, 1 - y), (1 - x, 1 - y)]
    return x, y, c, chips


def _all_gather(ws, name):
    R, C = ws.shape
    hr = R // 2

    def body(ws_ref, out_ref, send_sems, recv_sems, local_sem):
        x, y, c, chips = _place()
        sibling = (x, y, 1 - c)

        def half(px, py, pc):
            return out_ref.at[2 * px + py, pl.ds(pc * hr, hr), :]

        def copy(k, block, to, src=None):
            return pltpu.make_async_remote_copy(src_ref=half(*block) if src is None else src, dst_ref=half(*block),
                                                send_sem=send_sems.at[k], recv_sem=recv_sems.at[k],
                                                device_id=to, device_id_type=MESH)

        mine = pltpu.make_async_copy(ws_ref, out_ref.at[2 * x + y], local_sem)
        mine.start()
        first = [copy(j, (x, y, c), (*chip, c), src=ws_ref.at[pl.ds(c * hr, hr), :]) for j, chip in enumerate(chips)]
        for cp in first:
            cp.start()
        passed = [copy(3 + j, (*chip, c), sibling) for j, chip in enumerate(chips)]
        for j, chip in enumerate(chips):
            copy(j, (*chip, c), (x, y, c)).wait_recv()
            passed[j].start()
        for j, chip in enumerate(chips):
            copy(3 + j, (*chip, 1 - c), (x, y, c)).wait_recv()
        for cp in first + passed:
            cp.wait_send()
        mine.wait()

    return pl.pallas_call(body, name=name, in_specs=[ANY], out_specs=ANY,
                          out_shape=jax.ShapeDtypeStruct((N_SHARDS, R, C), ws.dtype),
                          scratch_shapes=[pltpu.SemaphoreType.DMA((6,)), pltpu.SemaphoreType.DMA((6,)),
                                          pltpu.SemaphoreType.DMA])(ws)


def _pair_exchange(gfull, name):
    G, R, C = gfull.shape
    hr = R // 2

    def body(g_ref, out_ref, send_sem, recv_sem):
        x, y, c, _ = _place()
        cp = pltpu.make_async_remote_copy(src_ref=g_ref.at[:, pl.ds((1 - c) * hr, hr), :], dst_ref=out_ref,
                                          send_sem=send_sem, recv_sem=recv_sem, device_id=(x, y, 1 - c), device_id_type=MESH)
        cp.start()
        cp.wait()

    return pl.pallas_call(body, name=name, in_specs=[ANY], out_specs=ANY, out_shape=jax.ShapeDtypeStruct((G, hr, C), F32),
                          scratch_shapes=[pltpu.SemaphoreType.DMA, pltpu.SemaphoreType.DMA])(gfull)


def _pair_add(gfull, other, name):
    G, R, C = gfull.shape
    hr = R // 2
    tr = _tile(hr, max(8, (2 * 1024 * 1024) // (4 * C) // 8 * 8))
    nr = hr // tr
    c = lax.axis_index("c")
    cidx = jnp.reshape(c, (1,)).astype(jnp.int32)

    def body(c_ref, a_ref, b_ref, o_ref):
        o_ref[...] = a_ref[...] + b_ref[...]

    grid_spec = pltpu.PrefetchScalarGridSpec(
        num_scalar_prefetch=1, grid=(G, nr),
        in_specs=[pl.BlockSpec((None, tr, C), lambda g, r, cr: (g, cr[0] * nr + r, 0)),
                  pl.BlockSpec((None, tr, C), lambda g, r, cr: (g, r, 0))],
        out_specs=pl.BlockSpec((None, tr, C), lambda g, r, cr: (g, r, 0)))
    return pl.pallas_call(body, name=name, grid_spec=grid_spec, out_shape=jax.ShapeDtypeStruct((G, hr, C), F32),
                          compiler_params=_params())(cidx, gfull, other)


def _chip_exchange(part, name):
    G, R2, C = part.shape

    def body(p_ref, out_ref, send_sems, recv_sems, local_sem):
        x, y, c, chips = _place()
        me = 2 * x + y
        mine = pltpu.make_async_copy(p_ref.at[me], out_ref.at[me], local_sem)
        mine.start()
        copies = []
        for j, (px, py) in enumerate(chips):
            copies.append(pltpu.make_async_remote_copy(src_ref=p_ref.at[2 * px + py], dst_ref=out_ref.at[me],
                                                       send_sem=send_sems.at[j], recv_sem=recv_sems.at[j],
                                                       device_id=(px, py, c), device_id_type=MESH))
        for cp in copies:
            cp.start()
        for j, (px, py) in enumerate(chips):
            pltpu.make_async_remote_copy(src_ref=p_ref.at[me], dst_ref=out_ref.at[2 * px + py],
                                         send_sem=send_sems.at[j], recv_sem=recv_sems.at[j],
                                         device_id=(px, py, c), device_id_type=MESH).wait_recv()
        for cp in copies:
            cp.wait_send()
        mine.wait()

    return pl.pallas_call(body, name=name, in_specs=[ANY], out_specs=ANY, out_shape=jax.ShapeDtypeStruct((G, R2, C), F32),
                          scratch_shapes=[pltpu.SemaphoreType.DMA((3,)), pltpu.SemaphoreType.DMA((3,)),
                                          pltpu.SemaphoreType.DMA])(part)


def _sum_slots(slots, name):
    G, R2, C = slots.shape
    tr = _tile(R2, max(8, (1024 * 1024) // (4 * C) // 8 * 8))

    def body(s_ref, o_ref):
        acc = s_ref[0]
        for u in range(1, G):
            acc = acc + s_ref[u]
        o_ref[...] = acc

    return pl.pallas_call(body, name=name, grid=(R2 // tr,), in_specs=[pl.BlockSpec((G, tr, C), lambda r: (0, r, 0))],
                          out_specs=pl.BlockSpec((tr, C), lambda r: (r, 0)), out_shape=jax.ShapeDtypeStruct((R2, C), F32),
                          compiler_params=_params())(slots)


def _pair_join(mine_half, name):
    R2, C = mine_half.shape

    def body(h_ref, out_ref, send_sem, recv_sem, local_sem):
        x, y, c, _ = _place()
        mine = pltpu.make_async_copy(h_ref, out_ref.at[pl.ds(c * R2, R2), :], local_sem)
        mine.start()
        cp = pltpu.make_async_remote_copy(src_ref=h_ref, dst_ref=out_ref.at[pl.ds(c * R2, R2), :], send_sem=send_sem,
                                          recv_sem=recv_sem, device_id=(x, y, 1 - c), device_id_type=MESH)
        cp.start()
        pltpu.make_async_remote_copy(src_ref=h_ref, dst_ref=out_ref.at[pl.ds((1 - c) * R2, R2), :], send_sem=send_sem,
                                     recv_sem=recv_sem, device_id=(x, y, 1 - c), device_id_type=MESH).wait_recv()
        cp.wait_send()
        mine.wait()

    return pl.pallas_call(body, name=name, in_specs=[ANY], out_specs=ANY, out_shape=jax.ShapeDtypeStruct((2 * R2, C), F32),
                          scratch_shapes=[pltpu.SemaphoreType.DMA, pltpu.SemaphoreType.DMA, pltpu.SemaphoreType.DMA])(mine_half)


def _reduce_scatter(gfull, tag):
    other = _pair_exchange(gfull, f"{tag}_pair_exchange")
    part = _pair_add(gfull, other, f"{tag}_pair_add")
    slots = _chip_exchange(part, f"{tag}_chip_exchange")
    half = _sum_slots(slots, f"{tag}_chip_sum")
    return _pair_join(half, f"{tag}_pair_join")


def _all_reduce_small(v, name):
    R, C = v.shape

    def gather_body(v_ref, out_ref, send_sems, recv_sems, local_sem):
        x, y, c, _ = _place()
        me = 4 * x + 2 * y + c
        mine = pltpu.make_async_copy(v_ref, out_ref.at[me], local_sem)
        mine.start()
        flips = [(fx, fy, fc) for fx in (0, 1) for fy in (0, 1) for fc in (0, 1)][1:]
        copies = []
        for j, (fx, fy, fc) in enumerate(flips):
            peer = (x ^ fx, y ^ fy, c ^ fc)
            copies.append(pltpu.make_async_remote_copy(src_ref=v_ref, dst_ref=out_ref.at[me], send_sem=send_sems.at[j],
                                                       recv_sem=recv_sems.at[j], device_id=peer, device_id_type=MESH))
        for cp in copies:
            cp.start()
        for j, (fx, fy, fc) in enumerate(flips):
            peer = (x ^ fx, y ^ fy, c ^ fc)
            pltpu.make_async_remote_copy(src_ref=v_ref, dst_ref=out_ref.at[4 * peer[0] + 2 * peer[1] + peer[2]],
                                         send_sem=send_sems.at[j], recv_sem=recv_sems.at[j], device_id=peer,
                                         device_id_type=MESH).wait_recv()
        for cp in copies:
            cp.wait_send()
        mine.wait()

    slots = pl.pallas_call(gather_body, name=name, in_specs=[ANY], out_specs=ANY,
                           out_shape=jax.ShapeDtypeStruct((8, R, C), F32),
                           scratch_shapes=[pltpu.SemaphoreType.DMA((7,)), pltpu.SemaphoreType.DMA((7,)),
                                           pltpu.SemaphoreType.DMA])(v)
    return _sum_slots(slots, f"{name}_sum")


def _adamw(w, g, m, v, name):
    R, C = w.shape
    tr = _tile(R, max(8, (512 * 1024) // (4 * C) // 8 * 8))
    bc1 = 1.0 - ADAM_B1 ** ADAM_STEP
    bc2 = 1.0 - ADAM_B2 ** ADAM_STEP

    def body(w_ref, g_ref, m_ref, v_ref, d_ref, nm_ref, nv_ref):
        gv = g_ref[...]
        nm = ADAM_B1 * m_ref[...] + (1.0 - ADAM_B1) * gv
        nv = ADAM_B2 * v_ref[...] + (1.0 - ADAM_B2) * (gv * gv)
        nm_ref[...] = nm
        nv_ref[...] = nv
        d_ref[...] = -ADAM_LR * ((nm / bc1) / (jnp.sqrt(nv / bc2) + ADAM_EPS) + ADAM_WD * w_ref[...])

    blk = pl.BlockSpec((tr, C), lambda r: (r, 0))
    out = jax.ShapeDtypeStruct((R, C), F32)
    return pl.pallas_call(body, name=name, grid=(R // tr,), in_specs=[blk] * 4, out_specs=[blk] * 3, out_shape=[out] * 3,
                          compiler_params=_params())(w, g, m, v)


WEIGHTS = ['ffn1_norm', 'ffn1_w_gate', 'ffn1_w_up', 'ffn1_w_down', 'mix_norm', 'w_in', 'pool_w', 'pool_scale', 'gla_w_a2',
           'gla_b_a', 'gla_head_norm', 'w_out', 'xattn_norm', 'mem_norm', 'xattn_w_q', 'xattn_w_kv', 'xattn_w_o', 'ffn2_norm',
           'ffn2_w_gate', 'ffn2_w_up', 'ffn2_w_down', 'final_norm']
SHARDED = ['ffn1_w_gate', 'ffn1_w_up', 'ffn1_w_down', 'w_in', 'pool_w', 'gla_w_a2', 'w_out', 'xattn_w_q', 'xattn_w_kv',
           'xattn_w_o', 'ffn2_w_gate', 'ffn2_w_up', 'ffn2_w_down']
REPLICATED = [n for n in WEIGHTS if n not in SHARDED]
SMALL_COLS = 512


def _as2d(a):
    return a.reshape(-1, a.shape[-1])


def _gather_weights(wl):
    w = {}
    for n in REPLICATED:
        w[n] = wl[n].reshape(1, -1)
    gathered = {}
    for n in SHARDED:
        dtype = F32 if n == "gla_w_a2" else BF16
        gathered[n] = _all_gather(_as2d(wl[n]).astype(dtype), f"gather_{n}")
    for n in ("ffn1_w_gate", "ffn1_w_up", "ffn1_w_down", "ffn2_w_gate", "ffn2_w_up", "ffn2_w_down", "xattn_w_kv"):
        w[n] = gathered[n]
    for n in ("w_out", "xattn_w_q", "xattn_w_o"):
        G, R, C = gathered[n].shape
        w[n] = gathered[n].reshape(G * R, C)
    G, D, CI = gathered["w_in"].shape
    w_in = jnp.transpose(gathered["w_in"], (1, 0, 2)).reshape(D, G * CI)
    rank = wl["gla_w_a2"].shape[1]
    main = G * CI - rank
    w["w_in"] = jnp.concatenate([w_in[:, :main], jnp.pad(w_in[:, main:], ((0, 0), (0, LANES - rank)))], axis=1)
    NG, CJ, C = wl["pool_w"].shape[1:]
    w["pool_w"] = jnp.transpose(gathered["pool_w"].reshape(G, NG, CJ, C), (1, 0, 2, 3)).reshape(NG, G * CJ, C)
    a2 = jnp.transpose(gathered["gla_w_a2"], (1, 0, 2)).reshape(rank, -1)
    w["gla_w_a2"] = jnp.pad(a2, ((0, LANES - rank), (0, 0))).astype(BF16)
    return w


def _shard_major(name, gfull, wl):
    R, C = _as2d(wl[name]).shape
    if name in ("ffn1_w_gate", "ffn1_w_up", "ffn2_w_gate", "ffn2_w_up", "xattn_w_kv"):
        return gfull
    if name in ("ffn1_w_down", "ffn2_w_down", "w_out", "xattn_w_q", "xattn_w_o"):
        return gfull.reshape(N_SHARDS, R, C)
    if name == "w_in":
        return jnp.transpose(gfull[:, :N_SHARDS * C].reshape(R, N_SHARDS, C), (1, 0, 2))
    if name == "pool_w":
        NG, CJ, _ = wl[name].shape[1:]
        return jnp.transpose(gfull.reshape(NG, N_SHARDS, CJ, C), (1, 0, 2, 3)).reshape(N_SHARDS, R, C)
    assert name == "gla_w_a2"
    return jnp.transpose(gfull[:R].reshape(R, N_SHARDS, C), (1, 0, 2))


def kernel(x, mem, ffn1_norm, ffn1_w_gate, ffn1_w_up, ffn1_w_down, mix_norm, w_in, pool_w, pool_scale, gla_w_a2, gla_b_a, gla_head_norm, w_out, xattn_norm, mem_norm, xattn_w_q, xattn_w_kv, xattn_w_o, ffn2_norm, ffn2_w_gate, ffn2_w_up, ffn2_w_down, final_norm, loss_target, m_ffn1_norm, m_ffn1_w_gate, m_ffn1_w_up, m_ffn1_w_down, m_mix_norm, m_w_in, m_pool_w, m_pool_scale, m_gla_w_a2, m_gla_b_a, m_gla_head_norm, m_w_out, m_xattn_norm, m_mem_norm, m_xattn_w_q, m_xattn_w_kv, m_xattn_w_o, m_ffn2_norm, m_ffn2_w_gate, m_ffn2_w_up, m_ffn2_w_down, m_final_norm, v_ffn1_norm, v_ffn1_w_gate, v_ffn1_w_up, v_ffn1_w_down, v_mix_norm, v_w_in, v_pool_w, v_pool_scale, v_gla_w_a2, v_gla_b_a, v_gla_head_norm, v_w_out, v_xattn_norm, v_mem_norm, v_xattn_w_q, v_xattn_w_kv, v_xattn_w_o, v_ffn2_norm, v_ffn2_w_gate, v_ffn2_w_up, v_ffn2_w_down, v_final_norm):
    given = dict(locals())
    wl = {n: given[n] for n in WEIGHTS}
    ml = {n: given["m_" + n] for n in WEIGHTS}
    vl = {n: given["v_" + n] for n in WEIGHTS}

    w = _gather_weights(wl)
    sq, dx0, g = _local_step(x[0], mem[0], loss_target[0], w)
    loss = lax.psum(0.5 * jnp.sum(sq) / x.shape[-1], ("x", "y", "c"))

    grads = {}
    for n in SHARDED:
        grads[n] = _reduce_scatter(_shard_major(n, g[n], wl), n)
    widths = [wl[n].size for n in REPLICATED]
    total = sum(widths)
    rows = -(-total // SMALL_COLS)
    rows = -(-rows // 8) * 8
    packed = jnp.concatenate([g[n].reshape(-1) for n in REPLICATED] + [jnp.zeros((rows * SMALL_COLS - total,), F32)])
    summed = _all_reduce_small(packed.reshape(rows, SMALL_COLS), "small_all_reduce").reshape(-1)
    off = 0
    for n, width in zip(REPLICATED, widths):
        grads[n] = summed[off:off + width].reshape(1, width)
        off += width

    out_g, out_d, out_m, out_v = [], [], [], []
    for n in WEIGHTS:
        shape = wl[n].shape
        g2 = grads[n]
        d, nm, nv = _adamw(wl[n].reshape(g2.shape), g2, ml[n].reshape(g2.shape), vl[n].reshape(g2.shape), f"adamw_{n}")
        out_g.append(g2.reshape(shape))
        out_d.append(d.reshape(shape))
        out_m.append(nm.reshape(shape))
        out_v.append(nv.reshape(shape))
    return (loss, dx0.reshape(x.shape), *out_g, *out_d, *out_m, *out_v)
```

```python
import functools

import jax
import jax.numpy as jnp
from jax import lax
from jax.experimental import pallas as pl
from jax.experimental.pallas import tpu as pltpu

F32 = jnp.float32
BF16 = jnp.bfloat16
MESH = pl.DeviceIdType.MESH

RMS_EPS = 1e-6
CHUNK = 64
POOL_WINDOWS = (2, 4, 8, 16)
POOL_HALO = 16
N_HEADS = 4
GATE_TEMP = 16.0
ADAM_LR, ADAM_B1, ADAM_B2, ADAM_EPS, ADAM_WD, ADAM_STEP = 0.001, 0.9, 0.999, 1e-08, 0.01, 10
N_SHARDS = 4
LANES = 128
MXU_COLS = 256
VMEM_LIMIT = 58 * 1024 * 1024

ANY = pl.BlockSpec(memory_space=pl.ANY)
HBM = pl.BlockSpec(memory_space=pltpu.HBM)
SEM = pl.BlockSpec(memory_space=pltpu.SEMAPHORE)
EFFECT = pltpu.SideEffectType.DATAFLOW_SIDE_EFFECTING


def _params(**kw):
    return pltpu.CompilerParams(vmem_limit_bytes=VMEM_LIMIT, **kw)


def _tile(n, want):
    for unit in (LANES, 8):
        t = (min(want, n) // unit) * unit
        while t >= unit:
            if n % t == 0:
                return t
            t -= unit
    return n


def _dot(a, b, dims):
    return lax.dot_general(a, b, (dims, ((), ())), preferred_element_type=F32)


def _nn(a, b):
    return _dot(a, b, ((1,), (0,)))


def _nt(a, b):
    return _dot(a, b, ((1,), (1,)))


def _tn(a, b):
    return _dot(a, b, ((0,), (0,)))


def _sigmoid(x):
    return 1.0 / (1.0 + jnp.exp(-x))


def _matmul(a, b, *, mode, name, out_dtype, tm=512, tn=2048, tk=2048, res=None, scale=1.0, b_groups=False, out_groups=0,
            dep=None):
    if mode == "tn":
        K, M = a.shape
    else:
        M, K = a.shape
    if mode == "nn":
        if b_groups:
            G, _, Nj = b.shape
            N = G * Nj
        else:
            N = b.shape[1]
    elif mode == "nt":
        if b_groups:
            G, N, Kj = b.shape
            assert G * Kj == K
        else:
            N = b.shape[0]
    else:
        N = b.shape[1]
    tm = _tile(M, tm)
    if mode == "nn" and b_groups:
        tn = _tile(Nj, tn)
    elif out_groups:
        tn = _tile(N // out_groups, tn)
    else:
        tn = _tile(N, tn)
    if mode == "nt" and b_groups:
        tk = _tile(Kj, tk)
    else:
        tk = _tile(K, tk)
    nk = K // tk
    grid = (M // tm, N // tn, nk)

    if mode == "tn":
        a_spec = pl.BlockSpec((tk, tm), lambda i, j, k: (k, i))
        b_spec = pl.BlockSpec((tk, tn), lambda i, j, k: (k, j))
        dims = ((0,), (0,))
    elif mode == "nn":
        a_spec = pl.BlockSpec((tm, tk), lambda i, j, k: (i, k))
        if b_groups:
            npj = Nj // tn
            b_spec = pl.BlockSpec((None, tk, tn), lambda i, j, k: (j // npj, k, j % npj))
        else:
            b_spec = pl.BlockSpec((tk, tn), lambda i, j, k: (k, j))
        dims = ((1,), (0,))
    else:
        a_spec = pl.BlockSpec((tm, tk), lambda i, j, k: (i, k))
        if b_groups:
            kpj = Kj // tk
            b_spec = pl.BlockSpec((None, tn, tk), lambda i, j, k: (k // kpj, j, k % kpj))
        else:
            b_spec = pl.BlockSpec((tn, tk), lambda i, j, k: (j, k))
        dims = ((1,), (1,))
    if out_groups:
        npj = (N // out_groups) // tn
        o_spec = pl.BlockSpec((None, tm, tn), lambda i, j, k: (j // npj, i, j % npj))
        out_shape = jax.ShapeDtypeStruct((out_groups, M, N // out_groups), out_dtype)
    else:
        o_spec = pl.BlockSpec((tm, tn), lambda i, j, k: (i, j))
        out_shape = jax.ShapeDtypeStruct((M, N), out_dtype)
    in_specs = [a_spec, b_spec]
    operands = [a, b]
    if res is not None:
        in_specs.append(pl.BlockSpec((tm, tn), lambda i, j, k: (i, j)))
        operands.append(res)
    has_res = res is not None
    n_dep = 0 if dep is None else 1
    if dep is not None:
        in_specs.append(pl.BlockSpec(dep.shape, lambda i, j, k: (0, 0)))
        operands.append(dep)

    def body(*refs):
        if has_res:
            a_ref, b_ref, r_ref = refs[:3]
        else:
            a_ref, b_ref = refs[:2]
            r_ref = None
        o_ref = refs[2 + has_res + n_dep]

        def finish(acc):
            if scale != 1.0:
                acc = acc * scale
            if r_ref is not None:
                acc = r_ref[...] + acc
            o_ref[...] = acc.astype(o_ref.dtype)

        part = _dot(a_ref[...], b_ref[...], dims)
        if nk == 1:
            finish(part)
        else:
            acc_ref = refs[-1]
            k = pl.program_id(2)

            @pl.when(k == 0)
            def _():
                acc_ref[...] = part

            @pl.when(k > 0)
            def _():
                acc_ref[...] += part

            @pl.when(k == nk - 1)
            def _():
                finish(acc_ref[...])

    scratch = [] if nk == 1 else [pltpu.VMEM((tm, tn), F32)]
    return pl.pallas_call(body, name=name, grid=grid, in_specs=in_specs, out_specs=o_spec, out_shape=out_shape,
                          scratch_shapes=scratch, compiler_params=_params())(*operands)


def _rms_fwd(x, gain, name, tm=256, dep=None):
    S, D = x.shape
    tm = _tile(S, tm)

    def body(x_ref, g_ref, *rest):
        o_ref = rest[-1]
        xv = x_ref[...]
        r = lax.rsqrt(jnp.mean(xv * xv, axis=-1, keepdims=True) + RMS_EPS)
        o_ref[...] = (xv * r * g_ref[...]).astype(o_ref.dtype)

    in_specs = [pl.BlockSpec((tm, D), lambda i: (i, 0)), pl.BlockSpec((1, D), lambda i: (0, 0))]
    operands = [x, gain]
    if dep is not None:
        in_specs.append(pl.BlockSpec(dep.shape, lambda i: (0, 0)))
        operands.append(dep)
    return pl.pallas_call(body, name=name, grid=(S // tm,), in_specs=in_specs,
                          out_specs=pl.BlockSpec((tm, D), lambda i: (i, 0)),
                          out_shape=jax.ShapeDtypeStruct((S, D), BF16), compiler_params=_params())(*operands)


def _rms_bwd(x, gain, dh, dres, name, half=False, tm=256):
    S, D = x.shape
    tm = _tile(S, tm)
    has_res = dres is not None

    def body(*refs):
        if has_res:
            x_ref, g_ref, dh_ref, dr_ref = refs[:4]
            outs = refs[4:]
        else:
            x_ref, g_ref, dh_ref = refs[:3]
            dr_ref = None
            outs = refs[3:]
        dx_ref, dg_ref = outs[0], outs[-1]
        xv = x_ref[...]
        dhv = dh_ref[...].astype(F32)
        r = lax.rsqrt(jnp.mean(xv * xv, axis=-1, keepdims=True) + RMS_EPS)
        gy = dhv * g_ref[...]
        dx = r * gy - xv * (r * r * r) * jnp.mean(gy * xv, axis=-1, keepdims=True)
        if dr_ref is not None:
            dx = dx + dr_ref[...]
        dx_ref[...] = dx
        if half:
            outs[1][...] = (0.5 * dx).astype(BF16)
        part = jnp.sum(dhv * xv * r, axis=0, keepdims=True)

        @pl.when(pl.program_id(0) == 0)
        def _():
            dg_ref[...] = part

        @pl.when(pl.program_id(0) > 0)
        def _():
            dg_ref[...] += part

    row = pl.BlockSpec((tm, D), lambda i: (i, 0))
    vec = pl.BlockSpec((1, D), lambda i: (0, 0))
    in_specs = [row, vec, row] + ([row] if has_res else [])
    operands = [x, gain, dh] + ([dres] if has_res else [])
    out_specs = [row] + ([row] if half else []) + [vec]
    out_shape = [jax.ShapeDtypeStruct((S, D), F32)] + ([jax.ShapeDtypeStruct((S, D), BF16)] if half else []) + [
        jax.ShapeDtypeStruct((1, D), F32)]
    return pl.pallas_call(body, name=name, grid=(S // tm,), in_specs=in_specs, out_specs=out_specs, out_shape=out_shape,
                          compiler_params=_params())(*operands)


def _loss_head(x, gain, target, name, tm=256):
    S, D = x.shape
    tm = _tile(S, tm)

    def body(x_ref, g_ref, t_ref, sq_ref, dx_ref, dxh_ref, dg_ref):
        xv = x_ref[...]
        r = lax.rsqrt(jnp.mean(xv * xv, axis=-1, keepdims=True) + RMS_EPS)
        xn = xv * r
        err = xn * g_ref[...] - t_ref[...]
        dout = err * (1.0 / D)
        gy = dout * g_ref[...]
        dx = r * gy - xv * (r * r * r) * jnp.mean(gy * xv, axis=-1, keepdims=True)
        dx_ref[...] = dx
        dxh_ref[...] = (0.5 * dx).astype(BF16)
        sq = jnp.sum(err * err, axis=0, keepdims=True)
        dg = jnp.sum(dout * xn, axis=0, keepdims=True)

        @pl.when(pl.program_id(0) == 0)
        def _():
            sq_ref[...] = sq
            dg_ref[...] = dg

        @pl.when(pl.program_id(0) > 0)
        def _():
            sq_ref[...] += sq
            dg_ref[...] += dg

    row = pl.BlockSpec((tm, D), lambda i: (i, 0))
    vec = pl.BlockSpec((1, D), lambda i: (0, 0))
    return pl.pallas_call(body, name=name, grid=(S // tm,), in_specs=[row, vec, row], out_specs=[vec, row, row, vec],
                          out_shape=[jax.ShapeDtypeStruct((1, D), F32), jax.ShapeDtypeStruct((S, D), F32),
                                     jax.ShapeDtypeStruct((S, D), BF16), jax.ShapeDtypeStruct((1, D), F32)],
                          compiler_params=_params())(x, gain, target)


def _cast(x, dtype, name, scale=1.0, tm=256):
    S, D = x.shape
    tm = _tile(S, tm)

    def body(x_ref, o_ref):
        o_ref[...] = (x_ref[...] * scale).astype(o_ref.dtype)

    row = pl.BlockSpec((tm, D), lambda i: (i, 0))
    return pl.pallas_call(body, name=name, grid=(S // tm,), in_specs=[row], out_specs=row,
                          out_shape=jax.ShapeDtypeStruct((S, D), dtype), compiler_params=_params())(x)


def _ffn_up(h, wg, wu, name, tm=512):
    S, D = h.shape
    G, _, Fj = wg.shape
    tm = _tile(S, tm)

    def body(h_ref, wg_ref, wu_ref, a_ref, b_ref, hid_ref):
        hv = h_ref[...]
        a = _nn(hv, wg_ref[...])
        b = _nn(hv, wu_ref[...])
        a_ref[...] = a.astype(BF16)
        b_ref[...] = b.astype(BF16)
        hid_ref[...] = (a * _sigmoid(a) * b).astype(BF16)

    w_spec = pl.BlockSpec((None, D, Fj), lambda g, i: (g, 0, 0))
    o_spec = pl.BlockSpec((tm, Fj), lambda g, i: (i, g))
    out = jax.ShapeDtypeStruct((S, G * Fj), BF16)
    return pl.pallas_call(body, name=name, grid=(G, S // tm),
                          in_specs=[pl.BlockSpec((tm, D), lambda g, i: (i, 0)), w_spec, w_spec],
                          out_specs=[o_spec, o_spec, o_spec], out_shape=[out, out, out], compiler_params=_params())(h, wg, wu)


def _ffn_dact(dxh, wd, a, b, name, tm=512):
    S, D = dxh.shape
    G, Fj, _ = wd.shape
    tm = _tile(S, tm)

    def body(dx_ref, wd_ref, a_ref, b_ref, da_ref, db_ref, hid_ref):
        dxv = dx_ref[...]
        for c0 in range(0, Fj, MXU_COLS):
            cs = slice(c0, min(c0 + MXU_COLS, Fj))
            dhid = _nt(dxv, wd_ref[cs, :])
            av = a_ref[:, cs].astype(F32)
            bv = b_ref[:, cs].astype(F32)
            s = _sigmoid(av)
            silu = av * s
            da_ref[:, cs] = (dhid * bv * (s * (1.0 + av * (1.0 - s)))).astype(BF16)
            db_ref[:, cs] = (dhid * silu).astype(BF16)
            hid_ref[:, cs] = (silu * bv).astype(BF16)

    blk = pl.BlockSpec((tm, Fj), lambda g, i: (i, g))
    out = jax.ShapeDtypeStruct((S, G * Fj), BF16)
    return pl.pallas_call(body, name=name, grid=(G, S // tm),
                          in_specs=[pl.BlockSpec((tm, D), lambda g, i: (i, 0)),
                                    pl.BlockSpec((None, Fj, D), lambda g, i: (g, 0, 0)), blk, blk],
                          out_specs=[blk, blk, blk], out_shape=[out, out, out], compiler_params=_params())(dxh, wd, a, b)


def _pool_fwd(proj, pool_w, pool_scale, name, tm=512):
    S = proj.shape[0]
    NG, C, _ = pool_w.shape
    DP = NG * C
    tm = _tile(S, tm)
    hb = tm // POOL_HALO
    n_ext = tm + POOL_HALO

    def body(u_ref, halo_ref, w_ref, sc_ref, y_ref, d_ref):
        i = pl.program_id(0)
        t = lax.broadcasted_iota(jnp.int32, (tm, 1), 0) + i * tm
        for g, win in enumerate(POOL_WINDOWS):
            cols = slice(g * C, (g + 1) * C)
            ug = u_ref[:, cols]
            halo = jnp.where(i > 0, halo_ref[:, cols], 0.0)
            acc = jnp.concatenate([halo, ug], axis=0)
            step = 1
            while step < win:
                acc = acc + pltpu.roll(acc, step, 0)
                step *= 2
            count = jnp.minimum(t + 1, win).astype(F32)
            d = (acc[POOL_HALO:, :] / count - ug).astype(BF16)
            d_ref[:, cols] = d
            y_ref[:, cols] = (_nn(d, w_ref[g]) * sc_ref[:, cols]).astype(BF16)

    del n_ext
    return pl.pallas_call(
        body, name=name, grid=(S // tm,),
        in_specs=[pl.BlockSpec((tm, DP), lambda i: (i, 0)),
                  pl.BlockSpec((POOL_HALO, DP), lambda i: (jnp.maximum(i * hb - 1, 0), 0)),
                  pl.BlockSpec((NG, C, C), lambda i: (0, 0, 0)), pl.BlockSpec((1, DP), lambda i: (0, 0))],
        out_specs=[pl.BlockSpec((tm, DP), lambda i: (i, 0)), pl.BlockSpec((tm, DP), lambda i: (i, 0))],
        out_shape=[jax.ShapeDtypeStruct((S, DP), BF16), jax.ShapeDtypeStruct((S, DP), BF16)],
        compiler_params=_params())(proj, proj, pool_w, pool_scale)


def _pool_bwd(dymix, d, pool_w, pool_scale, name, tm=512):
    S = dymix.shape[0]
    NG, C, _ = pool_w.shape
    DP = NG * C
    tm = _tile(S, tm)
    hb = tm // POOL_HALO
    nb = S // tm
    n_ext = tm + POOL_HALO
    last_halo = S // POOL_HALO - 1

    def body(dy_ref, halo_ref, d_ref, w_ref, sc_ref, du_ref, dw_ref, dsc_ref):
        i = pl.program_id(0)
        t = lax.broadcasted_iota(jnp.int32, (n_ext, 1), 0) + i * tm
        for g, win in enumerate(POOL_WINDOWS):
            cols = slice(g * C, (g + 1) * C)
            dy = dy_ref[:, cols]
            halo = jnp.where(i < nb - 1, halo_ref[:, cols], 0.0)
            sc = sc_ref[:, cols]
            dv = d_ref[:, cols]
            e_ext = (jnp.concatenate([dy, halo], axis=0) * sc).astype(BF16)
            dd = _nt(e_ext, w_ref[g])
            count = jnp.minimum(t + 1, win).astype(F32)
            acc = dd / count
            step = 1
            while step < win:
                acc = acc + pltpu.roll(acc, n_ext - step, 0)
                step *= 2
            du_ref[:, cols] = (acc[:tm, :] - dd[:tm, :]).astype(BF16)
            dw = _tn(dv, e_ext[:tm, :])
            dsc = jnp.sum(dy * _nn(dv, w_ref[g]), axis=0, keepdims=True)

            @pl.when(i == 0)
            def _():
                dw_ref[g] = dw
                dsc_ref[:, cols] = dsc

            @pl.when(i > 0)
            def _():
                dw_ref[g] += dw
                dsc_ref[:, cols] += dsc

    return pl.pallas_call(
        body, name=name, grid=(nb,),
        in_specs=[pl.BlockSpec((tm, DP), lambda i: (i, 0)),
                  pl.BlockSpec((POOL_HALO, DP), lambda i: (jnp.minimum((i + 1) * hb, last_halo), 0)),
                  pl.BlockSpec((tm, DP), lambda i: (i, 0)),
                  pl.BlockSpec((NG, C, C), lambda i: (0, 0, 0)), pl.BlockSpec((1, DP), lambda i: (0, 0))],
        out_specs=[pl.BlockSpec((tm, DP), lambda i: (i, 0)), pl.BlockSpec((NG, C, C), lambda i: (0, 0, 0)),
                   pl.BlockSpec((1, DP), lambda i: (0, 0))],
        out_shape=[jax.ShapeDtypeStruct((S, DP), BF16), jax.ShapeDtypeStruct((NG, C, C), F32),
                   jax.ShapeDtypeStruct((1, DP), F32)],
        compiler_params=_params())(dymix, dymix, d, pool_w, pool_scale)


def _chunk_scan(v, rows, reverse):
    n = v.shape[0]
    step = 1
    while step < CHUNK:
        if reverse:
            v = v + jnp.where(rows < CHUNK - step, pltpu.roll(v, n - step, 0), 0.0)
        else:
            v = v + jnp.where(rows >= step, pltpu.roll(v, step, 0), 0.0)
        step *= 2
    return v


def _log_decay(alr, w_a2, b_a):
    z = _nn(alr.astype(BF16), w_a2) + b_a
    la = (jnp.minimum(z, 0.0) - jnp.log(1.0 + jnp.exp(-jnp.abs(z)))) * (1.0 / GATE_TEMP)
    return z, la


def _gla_specs(DP, DKT, DV, tb, bmap):
    return [pl.BlockSpec((tb, DKT), lambda i: (bmap(i), DP // DKT)),
            pl.BlockSpec((tb, DKT), lambda i: (bmap(i), DP // DKT + 1)),
            pl.BlockSpec((tb, DV), lambda i: (bmap(i), (DP + 2 * DKT) // DV)),
            pl.BlockSpec((tb, DV), lambda i: (bmap(i), (DP + 2 * DKT) // DV + 1)),
            pl.BlockSpec((tb, LANES), lambda i: (bmap(i), (DP + 2 * DKT + 2 * DV) // LANES))]


def _gla_fwd(proj, w_a2, b_a, head_norm, DP, name, tb=512):
    S = proj.shape[0]
    DKT = b_a.shape[1]
    DV = head_norm.shape[1]
    dk, dv = DKT // N_HEADS, DV // N_HEADS
    tb = _tile(S, tb)
    ncb = tb // CHUNK
    qscale = dk ** -0.5

    def body(q_ref, k_ref, v_ref, g_ref, alr_ref, wa_ref, ba_ref, hn_ref, y_ref, st_out_ref, st_ref, kdec_ref, gam_ref):
        @pl.when(pl.program_id(0) == 0)
        def _():
            st_ref[...] = jnp.zeros_like(st_ref)

        rows = lax.broadcasted_iota(jnp.int32, (tb, 1), 0) % CHUNK
        _, la = _log_decay(alr_ref[...], wa_ref[...], ba_ref[...])
        tail = _chunk_scan(la, rows, True)
        kdec_ref[...] = k_ref[...] * jnp.exp(tail - la)
        gam_ref[...] = jnp.exp(tail)

        def chunk(c, carry):
            r0 = pl.multiple_of(c * CHUNK, CHUNK)
            rs = pl.ds(r0, CHUNK)
            gam = gam_ref[pl.ds(r0, 1), :]
            for h in range(N_HEADS):
                kc = slice(h * dk, (h + 1) * dk)
                vc = slice(h * dv, (h + 1) * dv)
                st = st_ref[h] * gam[:, kc] + _tn(v_ref[rs, vc].astype(BF16), kdec_ref[rs, kc].astype(BF16))
                st_ref[h] = st
                st_out_ref[c, h] = st
                o = _nt((q_ref[rs, kc] * qscale).astype(BF16), st.astype(BF16))
                r = lax.rsqrt(jnp.mean(o * o, axis=-1, keepdims=True) + RMS_EPS)
                gv = g_ref[rs, vc]
                y_ref[rs, vc] = (o * r * hn_ref[:, vc] * (gv * _sigmoid(gv))).astype(BF16)
            return carry

        lax.fori_loop(0, ncb, chunk, 0)

    full = lambda shape: pl.BlockSpec(shape, lambda i: (0,) * len(shape))
    return pl.pallas_call(
        body, name=name, grid=(S // tb,),
        in_specs=_gla_specs(DP, DKT, DV, tb, lambda i: i) + [full((LANES, DKT)), full((1, DKT)), full((1, DV))],
        out_specs=[pl.BlockSpec((tb, DV), lambda i: (i, 0)), pl.BlockSpec((ncb, N_HEADS, dv, dk), lambda i: (i, 0, 0, 0))],
        out_shape=[jax.ShapeDtypeStruct((S, DV), BF16), jax.ShapeDtypeStruct((S // CHUNK, N_HEADS, dv, dk), F32)],
        scratch_shapes=[pltpu.VMEM((N_HEADS, dv, dk), F32), pltpu.VMEM((tb, DKT), F32), pltpu.VMEM((tb, DKT), F32)],
        compiler_params=_params())(proj, proj, proj, proj, proj, w_a2, b_a, head_norm)


def _gla_bwd(proj, states, dymix, w_a2, b_a, head_norm, DP, name, tb=512):
    S = proj.shape[0]
    DKT = b_a.shape[1]
    DV = head_norm.shape[1]
    dk, dv = DKT // N_HEADS, DV // N_HEADS
    tb = _tile(S, tb)
    ncb = tb // CHUNK
    nb = S // tb
    qscale = dk ** -0.5
    rev = lambda i: nb - 1 - i

    def body(q_ref, k_ref, v_ref, g_ref, alr_ref, st_blk_ref, st_prev_ref, dy_ref, wa_ref, ba_ref, hn_ref,
             dq_ref, dk_ref, dv_ref, dg_ref, dalr_ref, dwa_ref, dba_ref, dhn_ref,
             dst_ref, kdec_ref, dec_ref, gam_ref, e_ref, dla_ref, dhn_acc_ref):
        i = pl.program_id(0)
        blk = rev(i)

        @pl.when(i == 0)
        def _():
            dst_ref[...] = jnp.zeros_like(dst_ref)

        dhn_acc_ref[...] = jnp.zeros_like(dhn_acc_ref)
        rows = lax.broadcasted_iota(jnp.int32, (tb, 1), 0) % CHUNK
        z, la = _log_decay(alr_ref[...], wa_ref[...], ba_ref[...])
        tail = _chunk_scan(la, rows, True)
        dec_ref[...] = jnp.exp(tail - la)
        kdec_ref[...] = k_ref[...] * dec_ref[...]
        gam_ref[...] = jnp.exp(tail)

        def chunk(cc, carry):
            c = ncb - 1 - cc
            r0 = pl.multiple_of(c * CHUNK, CHUNK)
            rs = pl.ds(r0, CHUNK)
            gam = gam_ref[pl.ds(r0, 1), :]
            first = jnp.logical_and(blk == 0, c == 0)
            gdg = []
            for h in range(N_HEADS):
                kc = slice(h * dk, (h + 1) * dk)
                vc = slice(h * dv, (h + 1) * dv)
                st = st_blk_ref[c, h]
                st_prev = jnp.where(c > 0, st_blk_ref[jnp.maximum(c - 1, 0), h], st_prev_ref[0, h])
                st_prev = jnp.where(first, 0.0, st_prev)
                qs = (q_ref[rs, kc] * qscale).astype(BF16)
                stb = st.astype(BF16)
                o = _nt(qs, stb)
                r = lax.rsqrt(jnp.mean(o * o, axis=-1, keepdims=True) + RMS_EPS)
                gv = g_ref[rs, vc]
                sg = _sigmoid(gv)
                dy = dy_ref[rs, vc]
                hn = hn_ref[:, vc]
                on = o * r
                dg_ref[rs, vc] = (dy * on * hn * (sg * (1.0 + gv * (1.0 - sg)))).astype(BF16)
                don = dy * (gv * sg)
                dhn_acc_ref[:, vc] += jnp.sum(don * on, axis=0, keepdims=True)
                dn = don * hn
                do = (r * dn - o * (r * r * r) * jnp.mean(dn * o, axis=-1, keepdims=True)).astype(BF16)
                dq_ref[rs, kc] = (_nn(do, stb) * qscale).astype(BF16)
                dst = dst_ref[h] + _tn(do, qs)
                dstb = dst.astype(BF16)
                kdec = kdec_ref[rs, kc]
                dv_ref[rs, vc] = _nt(kdec.astype(BF16), dstb).astype(BF16)
                dkdec = _nn(v_ref[rs, vc].astype(BF16), dstb)
                dk_ref[rs, kc] = (dkdec * dec_ref[rs, kc]).astype(BF16)
                e_ref[rs, kc] = dkdec * kdec
                gdg.append(jnp.sum(dst * st_prev, axis=0, keepdims=True) * gam[:, kc])
                dst_ref[h] = dst * gam[:, kc]
            dla_ref[rs, :] = jnp.broadcast_to(jnp.concatenate(gdg, axis=1), (CHUNK, DKT))
            return carry

        lax.fori_loop(0, ncb, chunk, 0)

        ev = e_ref[...]
        dla = dla_ref[...] + _chunk_scan(ev, rows, False) - ev
        dz = dla * (1.0 / GATE_TEMP) * (1.0 - _sigmoid(z))
        dzb = dz.astype(BF16)
        dalr_ref[...] = _nt(dzb, wa_ref[...]).astype(BF16)
        dwa = _tn(alr_ref[...].astype(BF16), dzb)
        dba = jnp.sum(dz, axis=0, keepdims=True)

        @pl.when(i == 0)
        def _():
            dwa_ref[...] = dwa
            dba_ref[...] = dba
            dhn_ref[...] = dhn_acc_ref[...]

        @pl.when(i > 0)
        def _():
            dwa_ref[...] += dwa
            dba_ref[...] += dba
            dhn_ref[...] += dhn_acc_ref[...]

    full = lambda shape: pl.BlockSpec(shape, lambda i: (0,) * len(shape))
    rowblk = lambda w: pl.BlockSpec((tb, w), lambda i: (rev(i), 0))
    return pl.pallas_call(
        body, name=name, grid=(nb,),
        in_specs=_gla_specs(DP, DKT, DV, tb, rev) + [
            pl.BlockSpec((ncb, N_HEADS, dv, dk), lambda i: (rev(i), 0, 0, 0)),
            pl.BlockSpec((1, N_HEADS, dv, dk), lambda i: (jnp.maximum(rev(i) * ncb - 1, 0), 0, 0, 0)),
            pl.BlockSpec((tb, DV), lambda i: (rev(i), DP // DV)),
            full((LANES, DKT)), full((1, DKT)), full((1, DV))],
        out_specs=[rowblk(DKT), rowblk(DKT), rowblk(DV), rowblk(DV), rowblk(LANES),
                   full((LANES, DKT)), full((1, DKT)), full((1, DV))],
        out_shape=[jax.ShapeDtypeStruct((S, DKT), BF16), jax.ShapeDtypeStruct((S, DKT), BF16),
                   jax.ShapeDtypeStruct((S, DV), BF16), jax.ShapeDtypeStruct((S, DV), BF16),
                   jax.ShapeDtypeStruct((S, LANES), BF16), jax.ShapeDtypeStruct((LANES, DKT), F32),
                   jax.ShapeDtypeStruct((1, DKT), F32), jax.ShapeDtypeStruct((1, DV), F32)],
        scratch_shapes=[pltpu.VMEM((N_HEADS, dv, dk), F32)] + [pltpu.VMEM((tb, DKT), F32)] * 5 + [pltpu.VMEM((1, DV), F32)],
        compiler_params=_params())(proj, proj, proj, proj, proj, states, states, dymix, w_a2, b_a, head_norm)


def _xattn_fwd(q, kv, name, tm=512):
    S, D = q.shape
    M = kv.shape[0]
    hd = D // N_HEADS
    tm = _tile(S, tm)
    scale = hd ** -0.5

    def body(q_ref, k_ref, v_ref, o_ref):
        for h in range(N_HEADS):
            hc = slice(h * hd, (h + 1) * hd)
            s = _nt(q_ref[:, hc], k_ref[:, hc]) * scale
            p = jnp.exp(s - jnp.max(s, axis=-1, keepdims=True))
            p = p / jnp.sum(p, axis=-1, keepdims=True)
            o_ref[:, hc] = _nn(p.astype(BF16), v_ref[:, hc]).astype(BF16)

    return pl.pallas_call(body, name=name, grid=(S // tm,),
                          in_specs=[pl.BlockSpec((tm, D), lambda i: (i, 0)), pl.BlockSpec((M, D), lambda i: (0, 0)),
                                    pl.BlockSpec((M, D), lambda i: (0, 1))],
                          out_specs=pl.BlockSpec((tm, D), lambda i: (i, 0)),
                          out_shape=jax.ShapeDtypeStruct((S, D), BF16), compiler_params=_params())(q, kv, kv)


def _xattn_bwd(q, kv, do, name, tm=512):
    S, D = q.shape
    M = kv.shape[0]
    hd = D // N_HEADS
    tm = _tile(S, tm)
    scale = hd ** -0.5

    def body(q_ref, k_ref, v_ref, do_ref, dq_ref, dkv_ref):
        first = pl.program_id(0) == 0
        for h in range(N_HEADS):
            hc = slice(h * hd, (h + 1) * hd)
            vcols = slice(D + h * hd, D + (h + 1) * hd)
            qh = q_ref[:, hc]
            kh = k_ref[:, hc]
            doh = do_ref[:, hc]
            s = _nt(qh, kh) * scale
            p = jnp.exp(s - jnp.max(s, axis=-1, keepdims=True))
            p = p / jnp.sum(p, axis=-1, keepdims=True)
            dvh = _tn(p.astype(BF16), doh)
            dp = _nt(doh, v_ref[:, hc])
            ds = ((p * (dp - jnp.sum(dp * p, axis=-1, keepdims=True))) * scale).astype(BF16)
            dq_ref[:, hc] = _nn(ds, kh).astype(BF16)
            dkh = _tn(ds, qh)

            @pl.when(first)
            def _():
                dkv_ref[:, hc] = dkh
                dkv_ref[:, vcols] = dvh

            @pl.when(jnp.logical_not(first))
            def _():
                dkv_ref[:, hc] += dkh
                dkv_ref[:, vcols] += dvh

    row = pl.BlockSpec((tm, D), lambda i: (i, 0))
    return pl.pallas_call(body, name=name, grid=(S // tm,),
                          in_specs=[row, pl.BlockSpec((M, D), lambda i: (0, 0)), pl.BlockSpec((M, D), lambda i: (0, 1)), row],
                          out_specs=[row, pl.BlockSpec((M, 2 * D), lambda i: (0, 0))],
                          out_shape=[jax.ShapeDtypeStruct((S, D), BF16), jax.ShapeDtypeStruct((M, 2 * D), F32)],
                          compiler_params=_params())(q, kv, kv, do)


def _local_step(x, mem, target, vec, weight, emit, dep0):
    DP = vec["pool_scale"].shape[1]
    g = {}
    pending = [None]

    def mm(a, b, **kw):
        dep, pending[0] = pending[0], None
        return _matmul(a, b, dep=dep, **kw)

    def send(name, gfull):
        token = emit(name, gfull)
        pending[0] = token if pending[0] is None else pending[0] + token

    def ffn_fwd(xin, tag, dep):
        h = _rms_fwd(xin, vec[f"{tag}_norm"], f"{tag}_norm", dep=dep)
        a, b, hid = _ffn_up(h, weight(f"{tag}_w_gate", h), weight(f"{tag}_w_up", h), f"{tag}_up")
        wd = weight(f"{tag}_w_down", a)
        G, Fj, D = wd.shape
        xo = _matmul(hid, wd.reshape(G * Fj, D), mode="nn", name=f"{tag}_down", out_dtype=F32, res=xin, scale=0.5,
                     tn=1024, tk=G * Fj)
        return xo, (h, a, b)

    def ffn_bwd(dxh, saved, tag):
        h, a, b = saved
        wg, wu, wd = weight(f"{tag}_w_gate"), weight(f"{tag}_w_up"), weight(f"{tag}_w_down")
        G, Fj, D = wd.shape
        da, db, hid = _ffn_dact(dxh, wd, a, b, f"{tag}_dact")
        send(f"{tag}_w_down", mm(hid, dxh, mode="tn", name=f"{tag}_dwd", out_dtype=F32, tm=Fj, tn=1024, tk=1024))
        send(f"{tag}_w_gate", mm(h, da, mode="tn", name=f"{tag}_dwg", out_dtype=F32, tm=1024, tn=Fj, tk=1024, out_groups=G))
        send(f"{tag}_w_up", mm(h, db, mode="tn", name=f"{tag}_dwu", out_dtype=F32, tm=1024, tn=Fj, tk=1024, out_groups=G))
        dh = mm(da, wg, mode="nt", name=f"{tag}_dh_gate", out_dtype=F32, tk=Fj, b_groups=True)
        return mm(db, wu, mode="nt", name=f"{tag}_dh_up", out_dtype=F32, tk=Fj, b_groups=True, res=dh)

    x1, ffn1_saved = ffn_fwd(x, "ffn1", dep0)
    h2 = _rms_fwd(x1, vec["mix_norm"], "mix_norm")
    w_in = weight("w_in", h2)
    proj = _matmul(h2, w_in, mode="nn", name="w_in", out_dtype=F32, tn=1408)
    pool_w, w_a2 = weight("pool_w", h2), weight("gla_w_a2", h2)
    y_pool, dpool = _pool_fwd(proj, pool_w, vec["pool_scale"], "pool_fwd")
    y_gla, states = _gla_fwd(proj, w_a2, vec["gla_b_a"], vec["gla_head_norm"], DP, "gla_fwd")
    ymix = jnp.concatenate([y_pool, y_gla], axis=1)
    w_out = weight("w_out", y_gla)
    x2 = _matmul(ymix, w_out, mode="nn", name="w_out", out_dtype=F32, res=x1)
    h3 = _rms_fwd(x2, vec["xattn_norm"], "xattn_norm")
    mh = _rms_fwd(mem, vec["mem_norm"], "mem_norm")
    w_q = weight("xattn_w_q", h3)
    q = _matmul(h3, w_q, mode="nn", name="xattn_q", out_dtype=BF16)
    w_kv = weight("xattn_w_kv", q)
    kv = _matmul(mh, w_kv, mode="nn", name="xattn_kv", out_dtype=BF16, b_groups=True, tn=1024)
    o = _xattn_fwd(q, kv, "xattn_fwd")
    w_o = weight("xattn_w_o", o)
    x3 = _matmul(o, w_o, mode="nn", name="xattn_o", out_dtype=F32, res=x2)
    x4, ffn2_saved = ffn_fwd(x3, "ffn2", None)
    sq, dx4, dx4h, g["final_norm"] = _loss_head(x4, vec["final_norm"], target, "loss_head")

    dh = ffn_bwd(dx4h, ffn2_saved, "ffn2")
    dx3, g["ffn2_norm"] = _rms_bwd(x3, vec["ffn2_norm"], dh, dx4, "ffn2_norm_bwd")
    dx3b = _cast(dx3, BF16, "dx3_cast")
    send("xattn_w_o", mm(o, dx3b, mode="tn", name="xattn_dwo", out_dtype=F32, tm=1024, tn=1024, tk=1024))
    do = mm(dx3b, w_o, mode="nt", name="xattn_do", out_dtype=BF16)
    dq, dkv = _xattn_bwd(q, kv, do, "xattn_bwd")
    send("xattn_w_q", mm(h3, dq, mode="tn", name="xattn_dwq", out_dtype=F32, tm=1024, tn=1024, tk=1024))
    dh3 = mm(dq, w_q, mode="nt", name="xattn_dh", out_dtype=F32)
    dkvb = _cast(dkv, BF16, "dkv_cast")
    send("xattn_w_kv", mm(mh, dkvb, mode="tn", name="xattn_dwkv", out_dtype=F32, tm=1024, tn=1024, out_groups=N_SHARDS))
    dmh = mm(dkvb, w_kv, mode="nt", name="xattn_dmh", out_dtype=F32, b_groups=True, tk=1024)
    _, g["mem_norm"] = _rms_bwd(mem, vec["mem_norm"], dmh, None, "mem_norm_bwd")
    dx2, g["xattn_norm"] = _rms_bwd(x2, vec["xattn_norm"], dh3, dx3, "xattn_norm_bwd")
    dx2b = _cast(dx2, BF16, "dx2_cast")
    send("w_out", mm(ymix, dx2b, mode="tn", name="dw_out", out_dtype=F32, tm=1024, tn=1024, tk=1024))
    dymix = mm(dx2b, w_out, mode="nt", name="dymix", out_dtype=F32)
    du, dpool_w, g["pool_scale"] = _pool_bwd(dymix, dpool, pool_w, vec["pool_scale"], "pool_bwd")
    send("pool_w", dpool_w)
    dq_g, dk_g, dv_g, dg_g, dalr, dw_a2, g["gla_b_a"], g["gla_head_norm"] = _gla_bwd(
        proj, states, dymix, w_a2, vec["gla_b_a"], vec["gla_head_norm"], DP, "gla_bwd")
    send("gla_w_a2", dw_a2)
    dproj = jnp.concatenate([du, dq_g, dk_g, dv_g, dg_g, dalr], axis=1)
    send("w_in", mm(h2, dproj, mode="tn", name="dw_in", out_dtype=F32, tm=1024, tn=1408, tk=1024))
    dh2 = mm(dproj, w_in, mode="nt", name="dh2", out_dtype=F32, tn=1024, tk=dproj.shape[1])
    dx1, dx1h, g["mix_norm"] = _rms_bwd(x1, vec["mix_norm"], dh2, dx2, "mix_norm_bwd", half=True)
    dh = ffn_bwd(dx1h, ffn1_saved, "ffn1")
    dx0, g["ffn1_norm"] = _rms_bwd(x, vec["ffn1_norm"], dh, dx1, "ffn1_norm_bwd")
    return sq, dx0, g


def _place():
    x, y, c = lax.axis_index("x"), lax.axis_index("y"), lax.axis_index("c")
    chips = [(1 - x, y), (x, 1 - y), (1 - x, 1 - y)]
    return x, y, c, chips


def _ids():
    return jnp.stack([2 * lax.axis_index("x") + lax.axis_index("y"), lax.axis_index("c")]).astype(jnp.int32)


def _hbm(a):
    return pltpu.with_memory_space_constraint(a, pltpu.HBM)


def _cast_to_slot(w2d, dtype, name, dep=None):
    R, C = w2d.shape
    tr = _tile(R, max(16, (1 << 20) // (4 * C) // 16 * 16))

    def body(i_ref, w_ref, *rest):
        rest[-1][...] = w_ref[...].astype(dtype)

    in_specs = [pl.BlockSpec((tr, C), lambda r, i: (r, 0))]
    operands = [w2d]
    if dep is not None:
        in_specs.append(pl.BlockSpec(dep.shape, lambda r, i: (0, 0)))
        operands.append(dep)
    grid_spec = pltpu.PrefetchScalarGridSpec(num_scalar_prefetch=1, grid=(R // tr,), in_specs=in_specs,
                                             out_specs=pl.BlockSpec((None, tr, C), lambda r, i: (i[0], r, 0)))
    return pl.pallas_call(body, name=name, grid_spec=grid_spec, out_shape=jax.ShapeDtypeStruct((N_SHARDS, R, C), dtype),
                          compiler_params=_params())(_ids(), *operands)


def _gather_copies(buf_ref, send_sems, recv_sems, incoming):
    x, y, c, chips = _place()
    hr = buf_ref.shape[1] // 2
    copies = []
    for j, (px, py) in enumerate(chips):
        slot = 2 * px + py if incoming else 2 * x + y
        half = buf_ref.at[slot, pl.ds(c * hr, hr), :]
        copies.append(pltpu.make_async_remote_copy(src_ref=half, dst_ref=half, send_sem=send_sems.at[j],
                                                   recv_sem=recv_sems.at[j], device_id=(px, py, c), device_id_type=MESH))
    return copies


def _gather_start(buf, name):
    def body(b_ref, send_sems, recv_sems, b_thru, token):
        for cp in _gather_copies(b_ref, send_sems, recv_sems, False):
            cp.start()
        token[...] = jnp.zeros_like(token)

    return pl.pallas_call(
        body, name=name,
        out_shape=(pltpu.SemaphoreType.DMA((3,)), pltpu.SemaphoreType.DMA((3,)), pltpu.HBM(buf.shape, buf.dtype),
                   jax.ShapeDtypeStruct((8, LANES), F32)),
        in_specs=(HBM,), out_specs=(SEM, SEM, HBM, pl.BlockSpec(memory_space=pltpu.VMEM)), input_output_aliases={0: 2},
        compiler_params=pltpu.CompilerParams(has_side_effects=EFFECT))(_hbm(buf))


def _gather_wait(send_sems, recv_sems, buf_thru, after, name):
    def body(b_ref, send_sems, recv_sems, after_ref, b_out):
        for cp in _gather_copies(b_ref, send_sems, recv_sems, False):
            cp.wait_send()
        for cp in _gather_copies(b_ref, send_sems, recv_sems, True):
            cp.wait_recv()

    return pl.pallas_call(
        body, name=name, out_shape=pltpu.HBM(buf_thru.shape, buf_thru.dtype),
        in_specs=(HBM, SEM, SEM, ANY), out_specs=HBM, input_output_aliases={0: 0},
        compiler_params=pltpu.CompilerParams(has_side_effects=EFFECT))(buf_thru, send_sems, recv_sems, after)


def _gather_forward(buf, name):
    G, R, C = buf.shape
    hr = R // 2

    def body(b_ref, o_ref, send_sems, recv_sems):
        x, y, c, chips = _place()
        copies = []
        for j, (px, py) in enumerate(chips):
            half = o_ref.at[2 * px + py, pl.ds(c * hr, hr), :]
            copies.append(pltpu.make_async_remote_copy(src_ref=half, dst_ref=half, send_sem=send_sems.at[j],
                                                       recv_sem=recv_sems.at[j], device_id=(x, y, 1 - c),
                                                       device_id_type=MESH))
        for cp in copies:
            cp.start()
        for j, (px, py) in enumerate(chips):
            half = o_ref.at[2 * px + py, pl.ds((1 - c) * hr, hr), :]
            pltpu.make_async_remote_copy(src_ref=half, dst_ref=half, send_sem=send_sems.at[j], recv_sem=recv_sems.at[j],
                                         device_id=(x, y, 1 - c), device_id_type=MESH).wait_recv()
        for cp in copies:
            cp.wait_send()

    return pl.pallas_call(body, name=name, in_specs=[ANY], out_specs=ANY, out_shape=jax.ShapeDtypeStruct(buf.shape, buf.dtype),
                          input_output_aliases={0: 0},
                          scratch_shapes=[pltpu.SemaphoreType.DMA((3,)), pltpu.SemaphoreType.DMA((3,))])(buf)


def _pair_exchange(gfull, name):
    G, R, C = gfull.shape
    hr = R // 2

    def body(g_ref, out_ref, send_sem, recv_sem):
        x, y, c, _ = _place()
        cp = pltpu.make_async_remote_copy(src_ref=g_ref.at[:, pl.ds((1 - c) * hr, hr), :], dst_ref=out_ref,
                                          send_sem=send_sem, recv_sem=recv_sem, device_id=(x, y, 1 - c), device_id_type=MESH)
        cp.start()
        cp.wait()

    return pl.pallas_call(body, name=name, in_specs=[ANY], out_specs=ANY, out_shape=jax.ShapeDtypeStruct((G, hr, C), F32),
                          scratch_shapes=[pltpu.SemaphoreType.DMA, pltpu.SemaphoreType.DMA])(gfull)


def _pair_add(gfull, other, name):
    G, R, C = gfull.shape
    hr = R // 2
    tr = _tile(hr, max(8, (2 * 1024 * 1024) // (4 * C) // 8 * 8))
    nr = hr // tr
    c = lax.axis_index("c")
    cidx = jnp.reshape(c, (1,)).astype(jnp.int32)

    def body(c_ref, a_ref, b_ref, o_ref):
        o_ref[...] = a_ref[...] + b_ref[...]

    grid_spec = pltpu.PrefetchScalarGridSpec(
        num_scalar_prefetch=1, grid=(G, nr),
        in_specs=[pl.BlockSpec((None, tr, C), lambda g, r, cr: (g, cr[0] * nr + r, 0)),
                  pl.BlockSpec((None, tr, C), lambda g, r, cr: (g, r, 0))],
        out_specs=pl.BlockSpec((None, tr, C), lambda g, r, cr: (g, r, 0)))
    return pl.pallas_call(body, name=name, grid_spec=grid_spec, out_shape=jax.ShapeDtypeStruct((G, hr, C), F32),
                          compiler_params=_params())(cidx, gfull, other)


def _chip_copies(p_ref, land_ref, send_sems, recv_sems, incoming):
    x, y, c, chips = _place()
    me = 2 * x + y
    copies = []
    for j, (px, py) in enumerate(chips):
        dst = land_ref.at[2 * px + py] if incoming else land_ref.at[me]
        copies.append(pltpu.make_async_remote_copy(src_ref=p_ref.at[2 * px + py], dst_ref=dst, send_sem=send_sems.at[j],
                                                   recv_sem=recv_sems.at[j], device_id=(px, py, c), device_id_type=MESH))
    return copies


def _chip_start(part, name):
    def body(p_ref, land_ref, send_sems, recv_sems, p_thru, land_thru, token):
        for cp in _chip_copies(p_ref, land_ref, send_sems, recv_sems, False):
            cp.start()
        token[...] = jnp.zeros_like(token)

    return pl.pallas_call(
        body, name=name,
        out_shape=(pltpu.SemaphoreType.DMA((3,)), pltpu.SemaphoreType.DMA((3,)), pltpu.HBM(part.shape, F32),
                   pltpu.HBM(part.shape, F32), jax.ShapeDtypeStruct((8, LANES), F32)),
        in_specs=(HBM, HBM), out_specs=(SEM, SEM, HBM, HBM, pl.BlockSpec(memory_space=pltpu.VMEM)),
        input_output_aliases={0: 2, 1: 3},
        compiler_params=pltpu.CompilerParams(has_side_effects=EFFECT))(_hbm(part), _hbm(lax.empty(part.shape, F32)))


def _chip_wait(send_sems, recv_sems, p_thru, land_thru, after, name):
    def body(p_ref, land_ref, send_sems, recv_sems, after_ref, p_out, land_out):
        for cp in _chip_copies(p_ref, land_ref, send_sems, recv_sems, False):
            cp.wait_send()
        for cp in _chip_copies(p_ref, land_ref, send_sems, recv_sems, True):
            cp.wait_recv()

    return pl.pallas_call(
        body, name=name, out_shape=(pltpu.HBM(p_thru.shape, F32), pltpu.HBM(p_thru.shape, F32)),
        in_specs=(HBM, HBM, SEM, SEM, ANY), out_specs=(HBM, HBM), input_output_aliases={0: 0, 1: 1},
        compiler_params=pltpu.CompilerParams(has_side_effects=EFFECT))(p_thru, land_thru, send_sems, recv_sems, after)


def _chip_sum(part, slots, name):
    G, R2, C = part.shape
    tr = _tile(R2, max(8, (1 << 20) // (4 * C) // 8 * 8))
    nr = R2 // tr

    def body(i_ref, p_ref, *rest):
        o_ref = rest[-1]
        acc = None
        for u in range(G):
            val = jnp.where(i_ref[0] == u, p_ref[...], rest[u][...])
            acc = val if acc is None else acc + val
        o_ref[...] = acc

    def slot_spec(u):
        return pl.BlockSpec((None, tr, C), lambda r, i: (jnp.where(i[0] == u, (u + 1) % G, u), r, 0))

    grid_spec = pltpu.PrefetchScalarGridSpec(
        num_scalar_prefetch=1, grid=(nr,),
        in_specs=[pl.BlockSpec((None, tr, C), lambda r, i: (i[0], r, 0))] + [slot_spec(u) for u in range(G)],
        out_specs=pl.BlockSpec((tr, C), lambda r, i: (i[1] * nr + r, 0)))
    return pl.pallas_call(body, name=name, grid_spec=grid_spec, out_shape=jax.ShapeDtypeStruct((2 * R2, C), F32),
                          compiler_params=_params())(_ids(), part, slots, slots, slots, slots)


def _sum_slots(slots, name):
    G, R2, C = slots.shape
    tr = _tile(R2, max(8, (1024 * 1024) // (4 * C) // 8 * 8))

    def body(s_ref, o_ref):
        acc = s_ref[0]
        for u in range(1, G):
            acc = acc + s_ref[u]
        o_ref[...] = acc

    return pl.pallas_call(body, name=name, grid=(R2 // tr,), in_specs=[pl.BlockSpec((G, tr, C), lambda r: (0, r, 0))],
                          out_specs=pl.BlockSpec((tr, C), lambda r: (r, 0)), out_shape=jax.ShapeDtypeStruct((R2, C), F32),
                          compiler_params=_params())(slots)


def _pair_join(full, name):
    R, C = full.shape
    R2 = R // 2

    def body(f_ref, o_ref, send_sem, recv_sem):
        x, y, c, _ = _place()
        mine = o_ref.at[pl.ds(c * R2, R2), :]
        theirs = o_ref.at[pl.ds((1 - c) * R2, R2), :]
        cp = pltpu.make_async_remote_copy(src_ref=mine, dst_ref=mine, send_sem=send_sem, recv_sem=recv_sem,
                                          device_id=(x, y, 1 - c), device_id_type=MESH)
        cp.start()
        pltpu.make_async_remote_copy(src_ref=theirs, dst_ref=theirs, send_sem=send_sem, recv_sem=recv_sem,
                                     device_id=(x, y, 1 - c), device_id_type=MESH).wait_recv()
        cp.wait_send()

    return pl.pallas_call(body, name=name, in_specs=[ANY], out_specs=ANY, out_shape=jax.ShapeDtypeStruct((R, C), F32),
                          input_output_aliases={0: 0},
                          scratch_shapes=[pltpu.SemaphoreType.DMA, pltpu.SemaphoreType.DMA])(full)


def _all_reduce_small(v, name):
    R, C = v.shape

    def gather_body(v_ref, out_ref, send_sems, recv_sems, local_sem):
        x, y, c, _ = _place()
        me = 4 * x + 2 * y + c
        mine = pltpu.make_async_copy(v_ref, out_ref.at[me], local_sem)
        mine.start()
        flips = [(fx, fy, fc) for fx in (0, 1) for fy in (0, 1) for fc in (0, 1)][1:]
        copies = []
        for j, (fx, fy, fc) in enumerate(flips):
            peer = (x ^ fx, y ^ fy, c ^ fc)
            copies.append(pltpu.make_async_remote_copy(src_ref=v_ref, dst_ref=out_ref.at[me], send_sem=send_sems.at[j],
                                                       recv_sem=recv_sems.at[j], device_id=peer, device_id_type=MESH))
        for cp in copies:
            cp.start()
        for j, (fx, fy, fc) in enumerate(flips):
            peer = (x ^ fx, y ^ fy, c ^ fc)
            pltpu.make_async_remote_copy(src_ref=v_ref, dst_ref=out_ref.at[4 * peer[0] + 2 * peer[1] + peer[2]],
                                         send_sem=send_sems.at[j], recv_sem=recv_sems.at[j], device_id=peer,
                                         device_id_type=MESH).wait_recv()
        for cp in copies:
            cp.wait_send()
        mine.wait()

    slots = pl.pallas_call(gather_body, name=name, in_specs=[ANY], out_specs=ANY,
                           out_shape=jax.ShapeDtypeStruct((8, R, C), F32),
                           scratch_shapes=[pltpu.SemaphoreType.DMA((7,)), pltpu.SemaphoreType.DMA((7,)),
                                           pltpu.SemaphoreType.DMA])(v)
    return _sum_slots(slots, f"{name}_sum")


def _adamw(w, g, m, v, name):
    R, C = w.shape
    tr = _tile(R, max(8, (512 * 1024) // (4 * C) // 8 * 8))
    bc1 = 1.0 - ADAM_B1 ** ADAM_STEP
    bc2 = 1.0 - ADAM_B2 ** ADAM_STEP

    def body(w_ref, g_ref, m_ref, v_ref, d_ref, nm_ref, nv_ref):
        gv = g_ref[...]
        nm = ADAM_B1 * m_ref[...] + (1.0 - ADAM_B1) * gv
        nv = ADAM_B2 * v_ref[...] + (1.0 - ADAM_B2) * (gv * gv)
        nm_ref[...] = nm
        nv_ref[...] = nv
        d_ref[...] = -ADAM_LR * ((nm / bc1) / (jnp.sqrt(nv / bc2) + ADAM_EPS) + ADAM_WD * w_ref[...])

    blk = pl.BlockSpec((tr, C), lambda r: (r, 0))
    out = jax.ShapeDtypeStruct((R, C), F32)
    return pl.pallas_call(body, name=name, grid=(R // tr,), in_specs=[blk] * 4, out_specs=[blk] * 3, out_shape=[out] * 3,
                          compiler_params=_params())(w, g, m, v)


WEIGHTS = ['ffn1_norm', 'ffn1_w_gate', 'ffn1_w_up', 'ffn1_w_down', 'mix_norm', 'w_in', 'pool_w', 'pool_scale', 'gla_w_a2',
           'gla_b_a', 'gla_head_norm', 'w_out', 'xattn_norm', 'mem_norm', 'xattn_w_q', 'xattn_w_kv', 'xattn_w_o', 'ffn2_norm',
           'ffn2_w_gate', 'ffn2_w_up', 'ffn2_w_down', 'final_norm']
SHARDED = ['ffn1_w_gate', 'ffn1_w_up', 'ffn1_w_down', 'w_in', 'pool_w', 'gla_w_a2', 'w_out', 'xattn_w_q', 'xattn_w_kv',
           'xattn_w_o', 'ffn2_w_gate', 'ffn2_w_up', 'ffn2_w_down']
REPLICATED = [n for n in WEIGHTS if n not in SHARDED]
SMALL_COLS = 512


def _as2d(a):
    return a.reshape(-1, a.shape[-1])


def _finish_weight(name, gathered, wl):
    G, R, C = gathered.shape
    rank = wl["gla_w_a2"].shape[1]
    if name in ("w_out", "xattn_w_q", "xattn_w_o"):
        return gathered.reshape(G * R, C)
    if name == "w_in":
        w_in = jnp.transpose(gathered, (1, 0, 2)).reshape(R, G * C)
        main = G * C - rank
        return jnp.concatenate([w_in[:, :main], jnp.pad(w_in[:, main:], ((0, 0), (0, LANES - rank)))], axis=1)
    if name == "pool_w":
        NG, CJ, _ = wl[name].shape[1:]
        return jnp.transpose(gathered.reshape(G, NG, CJ, C), (1, 0, 2, 3)).reshape(NG, G * CJ, C)
    if name == "gla_w_a2":
        a2 = jnp.transpose(gathered, (1, 0, 2)).reshape(rank, G * C)
        return jnp.pad(a2, ((0, LANES - rank), (0, 0))).astype(BF16)
    return gathered


def _start_gathers(wl):
    started = {}
    token = None
    for n in SHARDED:
        dtype = F32 if n == "gla_w_a2" else BF16
        buf = _cast_to_slot(_as2d(wl[n]), dtype, f"slot_{n}", dep=token)
        send_sems, recv_sems, thru, token = _gather_start(buf, f"gather_start_{n}")
        started[n] = (send_sems, recv_sems, thru)
    cache = {}

    def weight(n, after=None):
        if n not in cache:
            buf = _gather_wait(*started[n], after, f"gather_wait_{n}")
            cache[n] = _finish_weight(n, _gather_forward(buf, f"gather_forward_{n}"), wl)
        return cache[n]

    return weight, token


def _shard_major(name, gfull, wl):
    R, C = _as2d(wl[name]).shape
    if name in ("ffn1_w_gate", "ffn1_w_up", "ffn2_w_gate", "ffn2_w_up", "xattn_w_kv"):
        return gfull
    if name in ("ffn1_w_down", "ffn2_w_down", "w_out", "xattn_w_q", "xattn_w_o"):
        return gfull.reshape(N_SHARDS, R, C)
    if name == "w_in":
        return jnp.transpose(gfull[:, :N_SHARDS * C].reshape(R, N_SHARDS, C), (1, 0, 2))
    if name == "pool_w":
        NG, CJ, _ = wl[name].shape[1:]
        return jnp.transpose(gfull.reshape(NG, N_SHARDS, CJ, C), (1, 0, 2, 3)).reshape(N_SHARDS, R, C)
    assert name == "gla_w_a2"
    return jnp.transpose(gfull[:R].reshape(R, N_SHARDS, C), (1, 0, 2))


def kernel(x, mem, ffn1_norm, ffn1_w_gate, ffn1_w_up, ffn1_w_down, mix_norm, w_in, pool_w, pool_scale, gla_w_a2, gla_b_a, gla_head_norm, w_out, xattn_norm, mem_norm, xattn_w_q, xattn_w_kv, xattn_w_o, ffn2_norm, ffn2_w_gate, ffn2_w_up, ffn2_w_down, final_norm, loss_target, m_ffn1_norm, m_ffn1_w_gate, m_ffn1_w_up, m_ffn1_w_down, m_mix_norm, m_w_in, m_pool_w, m_pool_scale, m_gla_w_a2, m_gla_b_a, m_gla_head_norm, m_w_out, m_xattn_norm, m_mem_norm, m_xattn_w_q, m_xattn_w_kv, m_xattn_w_o, m_ffn2_norm, m_ffn2_w_gate, m_ffn2_w_up, m_ffn2_w_down, m_final_norm, v_ffn1_norm, v_ffn1_w_gate, v_ffn1_w_up, v_ffn1_w_down, v_mix_norm, v_w_in, v_pool_w, v_pool_scale, v_gla_w_a2, v_gla_b_a, v_gla_head_norm, v_w_out, v_xattn_norm, v_mem_norm, v_xattn_w_q, v_xattn_w_kv, v_xattn_w_o, v_ffn2_norm, v_ffn2_w_gate, v_ffn2_w_up, v_ffn2_w_down, v_final_norm):
    given = dict(locals())
    wl = {n: given[n] for n in WEIGHTS}
    ml = {n: given["m_" + n] for n in WEIGHTS}
    vl = {n: given["v_" + n] for n in WEIGHTS}

    vec = {n: wl[n].reshape(1, -1) for n in REPLICATED}
    weight, dep0 = _start_gathers(wl)
    in_flight = {}

    def emit(n, gfull):
        gsm = _shard_major(n, gfull, wl)
        part = _pair_add(gsm, _pair_exchange(gsm, f"{n}_pair_exchange"), f"{n}_pair_add")
        *in_flight[n], token = _chip_start(part, f"{n}_chip_start")
        return token

    sq, dx0, g = _local_step(x[0], mem[0], loss_target[0], vec, weight, emit, dep0)
    loss = lax.psum(0.5 * jnp.sum(sq) / x.shape[-1], ("x", "y", "c"))

    grads = {}
    for n in in_flight:
        part, slots = _chip_wait(*in_flight[n], dx0, f"{n}_chip_wait")
        grads[n] = _pair_join(_chip_sum(part, slots, f"{n}_chip_sum"), f"{n}_pair_join")
    widths = [wl[n].size for n in REPLICATED]
    total = sum(widths)
    rows = -(-total // SMALL_COLS)
    rows = -(-rows // 8) * 8
    packed = jnp.concatenate([g[n].reshape(-1) for n in REPLICATED] + [jnp.zeros((rows * SMALL_COLS - total,), F32)])
    summed = _all_reduce_small(packed.reshape(rows, SMALL_COLS), "small_all_reduce").reshape(-1)
    off = 0
    for n, width in zip(REPLICATED, widths):
        grads[n] = summed[off:off + width].reshape(1, width)
        off += width

    out_g, out_d, out_m, out_v = [], [], [], []
    for n in WEIGHTS:
        shape = wl[n].shape
        g2 = grads[n]
        d, nm, nv = _adamw(wl[n].reshape(g2.shape), g2, ml[n].reshape(g2.shape), vl[n].reshape(g2.shape), f"adamw_{n}")
        out_g.append(g2.reshape(shape))
        out_d.append(d.reshape(shape))
        out_m.append(nm.reshape(shape))
        out_v.append(nv.reshape(shape))
    return (loss, dx0.reshape(x.shape), *out_g, *out_d, *out_m, *out_v)
```

```python
import functools

import jax
import jax.numpy as jnp
from jax import lax
from jax.experimental import pallas as pl
from jax.experimental.pallas import tpu as pltpu

F32 = jnp.float32
BF16 = jnp.bfloat16
MESH = pl.DeviceIdType.MESH

RMS_EPS = 1e-6
CHUNK = 64
POOL_WINDOWS = (2, 4, 8, 16)
POOL_HALO = 16
N_HEADS = 4
GATE_TEMP = 16.0
ADAM_LR, ADAM_B1, ADAM_B2, ADAM_EPS, ADAM_WD, ADAM_STEP = 0.001, 0.9, 0.999, 1e-08, 0.01, 10
N_SHARDS = 4
LANES = 128
MXU_COLS = 256
VMEM_LIMIT = 58 * 1024 * 1024

ANY = pl.BlockSpec(memory_space=pl.ANY)
HBM = pl.BlockSpec(memory_space=pltpu.HBM)
SEM = pl.BlockSpec(memory_space=pltpu.SEMAPHORE)
EFFECT = pltpu.SideEffectType.DATAFLOW_SIDE_EFFECTING


def _params(**kw):
    return pltpu.CompilerParams(vmem_limit_bytes=VMEM_LIMIT, **kw)


def _tile(n, want):
    for unit in (LANES, 8):
        t = (min(want, n) // unit) * unit
        while t >= unit:
            if n % t == 0:
                return t
            t -= unit
    return n


def _dot(a, b, dims):
    return lax.dot_general(a, b, (dims, ((), ())), preferred_element_type=F32)


def _nn(a, b):
    return _dot(a, b, ((1,), (0,)))


def _nt(a, b):
    return _dot(a, b, ((1,), (1,)))


def _tn(a, b):
    return _dot(a, b, ((0,), (0,)))


def _sigmoid(x):
    return 1.0 / (1.0 + jnp.exp(-x))


def _matmul(a, b, *, mode, name, out_dtype, tm=512, tn=2048, tk=2048, res=None, scale=1.0, b_groups=False, out_groups=0,
            dep=()):
    if mode == "tn":
        K, M = a.shape
    else:
        M, K = a.shape
    if mode == "nn":
        if b_groups:
            G, _, Nj = b.shape
            N = G * Nj
        else:
            N = b.shape[1]
    elif mode == "nt":
        if b_groups:
            G, N, Kj = b.shape
            assert G * Kj == K
        else:
            N = b.shape[0]
    else:
        N = b.shape[1]
    tm = _tile(M, tm)
    if mode == "nn" and b_groups:
        tn = _tile(Nj, tn)
    elif out_groups:
        tn = _tile(N // out_groups, tn)
    else:
        tn = _tile(N, tn)
    if mode == "nt" and b_groups:
        tk = _tile(Kj, tk)
    else:
        tk = _tile(K, tk)
    nk = K // tk
    grid = (M // tm, N // tn, nk)

    if mode == "tn":
        a_spec = pl.BlockSpec((tk, tm), lambda i, j, k: (k, i))
        b_spec = pl.BlockSpec((tk, tn), lambda i, j, k: (k, j))
        dims = ((0,), (0,))
    elif mode == "nn":
        a_spec = pl.BlockSpec((tm, tk), lambda i, j, k: (i, k))
        if b_groups:
            npj = Nj // tn
            b_spec = pl.BlockSpec((None, tk, tn), lambda i, j, k: (j // npj, k, j % npj))
        else:
            b_spec = pl.BlockSpec((tk, tn), lambda i, j, k: (k, j))
        dims = ((1,), (0,))
    else:
        a_spec = pl.BlockSpec((tm, tk), lambda i, j, k: (i, k))
        if b_groups:
            kpj = Kj // tk
            b_spec = pl.BlockSpec((None, tn, tk), lambda i, j, k: (k // kpj, j, k % kpj))
        else:
            b_spec = pl.BlockSpec((tn, tk), lambda i, j, k: (j, k))
        dims = ((1,), (1,))
    if out_groups:
        npj = (N // out_groups) // tn
        o_spec = pl.BlockSpec((None, tm, tn), lambda i, j, k: (j // npj, i, j % npj))
        out_shape = jax.ShapeDtypeStruct((out_groups, M, N // out_groups), out_dtype)
    else:
        o_spec = pl.BlockSpec((tm, tn), lambda i, j, k: (i, j))
        out_shape = jax.ShapeDtypeStruct((M, N), out_dtype)
    in_specs = [a_spec, b_spec]
    operands = [a, b]
    if res is not None:
        in_specs.append(pl.BlockSpec((tm, tn), lambda i, j, k: (i, j)))
        operands.append(res)
    has_res = res is not None
    n_dep = len(dep)
    for d in dep:
        in_specs.append(pl.BlockSpec(d.shape, lambda i, j, k: (0, 0)))
        operands.append(d)

    def body(*refs):
        if has_res:
            a_ref, b_ref, r_ref = refs[:3]
        else:
            a_ref, b_ref = refs[:2]
            r_ref = None
        o_ref = refs[2 + has_res + n_dep]

        def finish(acc):
            if scale != 1.0:
                acc = acc * scale
            if r_ref is not None:
                acc = r_ref[...] + acc
            o_ref[...] = acc.astype(o_ref.dtype)

        part = _dot(a_ref[...], b_ref[...], dims)
        if nk == 1:
            finish(part)
        else:
            acc_ref = o_ref if in_place else refs[-1]
            k = pl.program_id(2)

            @pl.when(k == 0)
            def _():
                acc_ref[...] = part

            @pl.when(k > 0)
            def _():
                acc_ref[...] += part

            if not in_place:
                @pl.when(k == nk - 1)
                def _():
                    finish(acc_ref[...])

    in_place = out_dtype == F32 and res is None and scale == 1.0
    scratch = [] if nk == 1 or in_place else [pltpu.VMEM((tm, tn), F32)]
    return pl.pallas_call(body, name=name, grid=grid, in_specs=in_specs, out_specs=o_spec, out_shape=out_shape,
                          scratch_shapes=scratch, compiler_params=_params())(*operands)


def _rms_fwd(x, gain, name, tm=256, dep=None):
    S, D = x.shape
    tm = _tile(S, tm)

    def body(x_ref, g_ref, *rest):
        o_ref = rest[-1]
        xv = x_ref[...]
        r = lax.rsqrt(jnp.mean(xv * xv, axis=-1, keepdims=True) + RMS_EPS)
        o_ref[...] = (xv * r * g_ref[...]).astype(o_ref.dtype)

    in_specs = [pl.BlockSpec((tm, D), lambda i: (i, 0)), pl.BlockSpec((1, D), lambda i: (0, 0))]
    operands = [x, gain]
    if dep is not None:
        in_specs.append(pl.BlockSpec(dep.shape, lambda i: (0, 0)))
        operands.append(dep)
    return pl.pallas_call(body, name=name, grid=(S // tm,), in_specs=in_specs,
                          out_specs=pl.BlockSpec((tm, D), lambda i: (i, 0)),
                          out_shape=jax.ShapeDtypeStruct((S, D), BF16), compiler_params=_params())(*operands)


def _rms_bwd(x, gain, dh, dres, name, half=False, tm=256):
    S, D = x.shape
    tm = _tile(S, tm)
    has_res = dres is not None

    def body(*refs):
        if has_res:
            x_ref, g_ref, dh_ref, dr_ref = refs[:4]
            outs = refs[4:]
        else:
            x_ref, g_ref, dh_ref = refs[:3]
            dr_ref = None
            outs = refs[3:]
        dx_ref, dg_ref = outs[0], outs[-1]
        xv = x_ref[...]
        dhv = dh_ref[...].astype(F32)
        r = lax.rsqrt(jnp.mean(xv * xv, axis=-1, keepdims=True) + RMS_EPS)
        gy = dhv * g_ref[...]
        dx = r * gy - xv * (r * r * r) * jnp.mean(gy * xv, axis=-1, keepdims=True)
        if dr_ref is not None:
            dx = dx + dr_ref[...]
        dx_ref[...] = dx
        if half:
            outs[1][...] = (0.5 * dx).astype(BF16)
        part = jnp.sum(dhv * xv * r, axis=0, keepdims=True)

        @pl.when(pl.program_id(0) == 0)
        def _():
            dg_ref[...] = part

        @pl.when(pl.program_id(0) > 0)
        def _():
            dg_ref[...] += part

    row = pl.BlockSpec((tm, D), lambda i: (i, 0))
    vec = pl.BlockSpec((1, D), lambda i: (0, 0))
    in_specs = [row, vec, row] + ([row] if has_res else [])
    operands = [x, gain, dh] + ([dres] if has_res else [])
    out_specs = [row] + ([row] if half else []) + [vec]
    out_shape = [jax.ShapeDtypeStruct((S, D), F32)] + ([jax.ShapeDtypeStruct((S, D), BF16)] if half else []) + [
        jax.ShapeDtypeStruct((1, D), F32)]
    return pl.pallas_call(body, name=name, grid=(S // tm,), in_specs=in_specs, out_specs=out_specs, out_shape=out_shape,
                          compiler_params=_params())(*operands)


def _loss_head(x, gain, target, name, tm=256):
    S, D = x.shape
    tm = _tile(S, tm)

    def body(x_ref, g_ref, t_ref, sq_ref, dx_ref, dxh_ref, dg_ref):
        xv = x_ref[...]
        r = lax.rsqrt(jnp.mean(xv * xv, axis=-1, keepdims=True) + RMS_EPS)
        xn = xv * r
        err = xn * g_ref[...] - t_ref[...]
        dout = err * (1.0 / D)
        gy = dout * g_ref[...]
        dx = r * gy - xv * (r * r * r) * jnp.mean(gy * xv, axis=-1, keepdims=True)
        dx_ref[...] = dx
        dxh_ref[...] = (0.5 * dx).astype(BF16)
        sq = jnp.sum(err * err, axis=0, keepdims=True)
        dg = jnp.sum(dout * xn, axis=0, keepdims=True)

        @pl.when(pl.program_id(0) == 0)
        def _():
            sq_ref[...] = sq
            dg_ref[...] = dg

        @pl.when(pl.program_id(0) > 0)
        def _():
            sq_ref[...] += sq
            dg_ref[...] += dg

    row = pl.BlockSpec((tm, D), lambda i: (i, 0))
    vec = pl.BlockSpec((1, D), lambda i: (0, 0))
    return pl.pallas_call(body, name=name, grid=(S // tm,), in_specs=[row, vec, row], out_specs=[vec, row, row, vec],
                          out_shape=[jax.ShapeDtypeStruct((1, D), F32), jax.ShapeDtypeStruct((S, D), F32),
                                     jax.ShapeDtypeStruct((S, D), BF16), jax.ShapeDtypeStruct((1, D), F32)],
                          compiler_params=_params())(x, gain, target)


def _cast(x, dtype, name, scale=1.0, tm=256):
    S, D = x.shape
    tm = _tile(S, tm)

    def body(x_ref, o_ref):
        o_ref[...] = (x_ref[...] * scale).astype(o_ref.dtype)

    row = pl.BlockSpec((tm, D), lambda i: (i, 0))
    return pl.pallas_call(body, name=name, grid=(S // tm,), in_specs=[row], out_specs=row,
                          out_shape=jax.ShapeDtypeStruct((S, D), dtype), compiler_params=_params())(x)


def _ffn_up(h, wg, wu, name, tm=512):
    S, D = h.shape
    G, _, Fj = wg.shape
    tm = _tile(S, tm)

    def body(h_ref, wg_ref, wu_ref, a_ref, b_ref, hid_ref):
        hv = h_ref[...]
        a = _nn(hv, wg_ref[...])
        b = _nn(hv, wu_ref[...])
        a_ref[...] = a.astype(BF16)
        b_ref[...] = b.astype(BF16)
        hid_ref[...] = (a * _sigmoid(a) * b).astype(BF16)

    w_spec = pl.BlockSpec((None, D, Fj), lambda g, i: (g, 0, 0))
    o_spec = pl.BlockSpec((tm, Fj), lambda g, i: (i, g))
    out = jax.ShapeDtypeStruct((S, G * Fj), BF16)
    return pl.pallas_call(body, name=name, grid=(G, S // tm),
                          in_specs=[pl.BlockSpec((tm, D), lambda g, i: (i, 0)), w_spec, w_spec],
                          out_specs=[o_spec, o_spec, o_spec], out_shape=[out, out, out], compiler_params=_params())(h, wg, wu)


def _ffn_dact(dxh, wd, a, b, name, tm=512):
    S, D = dxh.shape
    G, Fj, _ = wd.shape
    tm = _tile(S, tm)

    def body(dx_ref, wd_ref, a_ref, b_ref, da_ref, db_ref, hid_ref):
        dxv = dx_ref[...]
        for c0 in range(0, Fj, MXU_COLS):
            cs = slice(c0, min(c0 + MXU_COLS, Fj))
            dhid = _nt(dxv, wd_ref[cs, :])
            av = a_ref[:, cs].astype(F32)
            bv = b_ref[:, cs].astype(F32)
            s = _sigmoid(av)
            silu = av * s
            da_ref[:, cs] = (dhid * bv * (s * (1.0 + av * (1.0 - s)))).astype(BF16)
            db_ref[:, cs] = (dhid * silu).astype(BF16)
            hid_ref[:, cs] = (silu * bv).astype(BF16)

    blk = pl.BlockSpec((tm, Fj), lambda g, i: (i, g))
    out = jax.ShapeDtypeStruct((S, G * Fj), BF16)
    return pl.pallas_call(body, name=name, grid=(G, S // tm),
                          in_specs=[pl.BlockSpec((tm, D), lambda g, i: (i, 0)),
                                    pl.BlockSpec((None, Fj, D), lambda g, i: (g, 0, 0)), blk, blk],
                          out_specs=[blk, blk, blk], out_shape=[out, out, out], compiler_params=_params())(dxh, wd, a, b)


def _ffn_dh(da, db, wg, wu, name, dep=(), tm=512):
    S = da.shape[0]
    G, D, Fj = wg.shape
    tm = _tile(S, tm)

    def body(da_ref, db_ref, wg_ref, wu_ref, *rest):
        o_ref = rest[-1]
        part = _nt(da_ref[...], wg_ref[...]) + _nt(db_ref[...], wu_ref[...])

        @pl.when(pl.program_id(1) == 0)
        def _():
            o_ref[...] = part

        @pl.when(pl.program_id(1) > 0)
        def _():
            o_ref[...] += part

    act = pl.BlockSpec((tm, Fj), lambda i, g: (i, g))
    w_spec = pl.BlockSpec((None, D, Fj), lambda i, g: (g, 0, 0))
    in_specs = [act, act, w_spec, w_spec] + [pl.BlockSpec(d.shape, lambda i, g: (0, 0)) for d in dep]
    return pl.pallas_call(body, name=name, grid=(S // tm, G), in_specs=in_specs,
                          out_specs=pl.BlockSpec((tm, D), lambda i, g: (i, 0)),
                          out_shape=jax.ShapeDtypeStruct((S, D), F32), compiler_params=_params())(da, db, wg, wu, *dep)


def _pool_fwd(proj, pool_w, pool_scale, name, tm=512):
    S = proj.shape[0]
    NG, C, _ = pool_w.shape
    DP = NG * C
    tm = _tile(S, tm)
    hb = tm // POOL_HALO
    n_ext = tm + POOL_HALO

    def body(u_ref, halo_ref, w_ref, sc_ref, y_ref, d_ref):
        i = pl.program_id(0)
        t = lax.broadcasted_iota(jnp.int32, (tm, 1), 0) + i * tm
        for g, win in enumerate(POOL_WINDOWS):
            cols = slice(g * C, (g + 1) * C)
            ug = u_ref[:, cols]
            halo = jnp.where(i > 0, halo_ref[:, cols], 0.0)
            acc = jnp.concatenate([halo, ug], axis=0)
            step = 1
            while step < win:
                acc = acc + pltpu.roll(acc, step, 0)
                step *= 2
            count = jnp.minimum(t + 1, win).astype(F32)
            d = (acc[POOL_HALO:, :] / count - ug).astype(BF16)
            d_ref[:, cols] = d
            y_ref[:, cols] = (_nn(d, w_ref[g]) * sc_ref[:, cols]).astype(BF16)

    del n_ext
    return pl.pallas_call(
        body, name=name, grid=(S // tm,),
        in_specs=[pl.BlockSpec((tm, DP), lambda i: (i, 0)),
                  pl.BlockSpec((POOL_HALO, DP), lambda i: (jnp.maximum(i * hb - 1, 0), 0)),
                  pl.BlockSpec((NG, C, C), lambda i: (0, 0, 0)), pl.BlockSpec((1, DP), lambda i: (0, 0))],
        out_specs=[pl.BlockSpec((tm, DP), lambda i: (i, 0)), pl.BlockSpec((tm, DP), lambda i: (i, 0))],
        out_shape=[jax.ShapeDtypeStruct((S, DP), BF16), jax.ShapeDtypeStruct((S, DP), BF16)],
        compiler_params=_params())(proj, proj, pool_w, pool_scale)


def _pool_bwd(dymix, d, pool_w, pool_scale, name, tm=512):
    S = dymix.shape[0]
    NG, C, _ = pool_w.shape
    DP = NG * C
    tm = _tile(S, tm)
    hb = tm // POOL_HALO
    nb = S // tm
    n_ext = tm + POOL_HALO
    last_halo = S // POOL_HALO - 1

    def body(dy_ref, halo_ref, d_ref, w_ref, sc_ref, du_ref, dw_ref, dsc_ref):
        i = pl.program_id(0)
        t = lax.broadcasted_iota(jnp.int32, (n_ext, 1), 0) + i * tm
        for g, win in enumerate(POOL_WINDOWS):
            cols = slice(g * C, (g + 1) * C)
            dy = dy_ref[:, cols]
            halo = jnp.where(i < nb - 1, halo_ref[:, cols], 0.0)
            sc = sc_ref[:, cols]
            dv = d_ref[:, cols]
            e_ext = (jnp.concatenate([dy, halo], axis=0) * sc).astype(BF16)
            dd = _nt(e_ext, w_ref[g])
            count = jnp.minimum(t + 1, win).astype(F32)
            acc = dd / count
            step = 1
            while step < win:
                acc = acc + pltpu.roll(acc, n_ext - step, 0)
                step *= 2
            du_ref[:, cols] = (acc[:tm, :] - dd[:tm, :]).astype(BF16)
            dw = _tn(dv, e_ext[:tm, :])
            dsc = jnp.sum(dy * _nn(dv, w_ref[g]), axis=0, keepdims=True)

            @pl.when(i == 0)
            def _():
                dw_ref[g] = dw
                dsc_ref[:, cols] = dsc

            @pl.when(i > 0)
            def _():
                dw_ref[g] += dw
                dsc_ref[:, cols] += dsc

    return pl.pallas_call(
        body, name=name, grid=(nb,),
        in_specs=[pl.BlockSpec((tm, DP), lambda i: (i, 0)),
                  pl.BlockSpec((POOL_HALO, DP), lambda i: (jnp.minimum((i + 1) * hb, last_halo), 0)),
                  pl.BlockSpec((tm, DP), lambda i: (i, 0)),
                  pl.BlockSpec((NG, C, C), lambda i: (0, 0, 0)), pl.BlockSpec((1, DP), lambda i: (0, 0))],
        out_specs=[pl.BlockSpec((tm, DP), lambda i: (i, 0)), pl.BlockSpec((NG, C, C), lambda i: (0, 0, 0)),
                   pl.BlockSpec((1, DP), lambda i: (0, 0))],
        out_shape=[jax.ShapeDtypeStruct((S, DP), BF16), jax.ShapeDtypeStruct((NG, C, C), F32),
                   jax.ShapeDtypeStruct((1, DP), F32)],
        compiler_params=_params())(dymix, dymix, d, pool_w, pool_scale)


def _chunk_scan(v, rows, reverse):
    n = v.shape[0]
    step = 1
    while step < CHUNK:
        if reverse:
            v = v + jnp.where(rows < CHUNK - step, pltpu.roll(v, n - step, 0), 0.0)
        else:
            v = v + jnp.where(rows >= step, pltpu.roll(v, step, 0), 0.0)
        step *= 2
    return v


def _log_decay(alr, w_a2, b_a):
    z = _nn(alr.astype(BF16), w_a2) + b_a
    la = (jnp.minimum(z, 0.0) - jnp.log(1.0 + jnp.exp(-jnp.abs(z)))) * (1.0 / GATE_TEMP)
    return z, la


def _gla_specs(DP, DKT, DV, tb, bmap):
    return [pl.BlockSpec((tb, DKT), lambda i: (bmap(i), DP // DKT)),
            pl.BlockSpec((tb, DKT), lambda i: (bmap(i), DP // DKT + 1)),
            pl.BlockSpec((tb, DV), lambda i: (bmap(i), (DP + 2 * DKT) // DV)),
            pl.BlockSpec((tb, DV), lambda i: (bmap(i), (DP + 2 * DKT) // DV + 1)),
            pl.BlockSpec((tb, LANES), lambda i: (bmap(i), (DP + 2 * DKT + 2 * DV) // LANES))]


def _gla_fwd(proj, w_a2, b_a, head_norm, DP, name, tb=512):
    S = proj.shape[0]
    DKT = b_a.shape[1]
    DV = head_norm.shape[1]
    dk, dv = DKT // N_HEADS, DV // N_HEADS
    tb = _tile(S, tb)
    ncb = tb // CHUNK
    qscale = dk ** -0.5

    def body(q_ref, k_ref, v_ref, g_ref, alr_ref, wa_ref, ba_ref, hn_ref, y_ref, st_out_ref, st_ref, kdec_ref, gam_ref):
        @pl.when(pl.program_id(0) == 0)
        def _():
            st_ref[...] = jnp.zeros_like(st_ref)

        rows = lax.broadcasted_iota(jnp.int32, (tb, 1), 0) % CHUNK
        _, la = _log_decay(alr_ref[...], wa_ref[...], ba_ref[...])
        tail = _chunk_scan(la, rows, True)
        kdec_ref[...] = k_ref[...] * jnp.exp(tail - la)
        gam_ref[...] = jnp.exp(tail)

        def chunk(c, carry):
            r0 = pl.multiple_of(c * CHUNK, CHUNK)
            rs = pl.ds(r0, CHUNK)
            gam = gam_ref[pl.ds(r0, 1), :]
            for h in range(N_HEADS):
                kc = slice(h * dk, (h + 1) * dk)
                vc = slice(h * dv, (h + 1) * dv)
                st = st_ref[h] * gam[:, kc] + _tn(v_ref[rs, vc].astype(BF16), kdec_ref[rs, kc].astype(BF16))
                st_ref[h] = st
                st_out_ref[c, h] = st
                o = _nt((q_ref[rs, kc] * qscale).astype(BF16), st.astype(BF16))
                r = lax.rsqrt(jnp.mean(o * o, axis=-1, keepdims=True) + RMS_EPS)
                gv = g_ref[rs, vc]
                y_ref[rs, vc] = (o * r * hn_ref[:, vc] * (gv * _sigmoid(gv))).astype(BF16)
            return carry

        lax.fori_loop(0, ncb, chunk, 0)

    full = lambda shape: pl.BlockSpec(shape, lambda i: (0,) * len(shape))
    return pl.pallas_call(
        body, name=name, grid=(S // tb,),
        in_specs=_gla_specs(DP, DKT, DV, tb, lambda i: i) + [full((LANES, DKT)), full((1, DKT)), full((1, DV))],
        out_specs=[pl.BlockSpec((tb, DV), lambda i: (i, 0)), pl.BlockSpec((ncb, N_HEADS, dv, dk), lambda i: (i, 0, 0, 0))],
        out_shape=[jax.ShapeDtypeStruct((S, DV), BF16), jax.ShapeDtypeStruct((S // CHUNK, N_HEADS, dv, dk), F32)],
        scratch_shapes=[pltpu.VMEM((N_HEADS, dv, dk), F32), pltpu.VMEM((tb, DKT), F32), pltpu.VMEM((tb, DKT), F32)],
        compiler_params=_params())(proj, proj, proj, proj, proj, w_a2, b_a, head_norm)


def _gla_bwd(proj, states, dymix, w_a2, b_a, head_norm, DP, name, tb=512):
    S = proj.shape[0]
    DKT = b_a.shape[1]
    DV = head_norm.shape[1]
    dk, dv = DKT // N_HEADS, DV // N_HEADS
    tb = _tile(S, tb)
    ncb = tb // CHUNK
    nb = S // tb
    qscale = dk ** -0.5
    rev = lambda i: nb - 1 - i

    def body(q_ref, k_ref, v_ref, g_ref, alr_ref, st_blk_ref, st_prev_ref, dy_ref, wa_ref, ba_ref, hn_ref,
             dq_ref, dk_ref, dv_ref, dg_ref, dalr_ref, dwa_ref, dba_ref, dhn_ref,
             dst_ref, kdec_ref, dec_ref, gam_ref, e_ref, dla_ref, dhn_acc_ref):
        i = pl.program_id(0)
        blk = rev(i)

        @pl.when(i == 0)
        def _():
            dst_ref[...] = jnp.zeros_like(dst_ref)

        dhn_acc_ref[...] = jnp.zeros_like(dhn_acc_ref)
        rows = lax.broadcasted_iota(jnp.int32, (tb, 1), 0) % CHUNK
        z, la = _log_decay(alr_ref[...], wa_ref[...], ba_ref[...])
        tail = _chunk_scan(la, rows, True)
        dec_ref[...] = jnp.exp(tail - la)
        kdec_ref[...] = k_ref[...] * dec_ref[...]
        gam_ref[...] = jnp.exp(tail)

        def chunk(cc, carry):
            c = ncb - 1 - cc
            r0 = pl.multiple_of(c * CHUNK, CHUNK)
            rs = pl.ds(r0, CHUNK)
            gam = gam_ref[pl.ds(r0, 1), :]
            first = jnp.logical_and(blk == 0, c == 0)
            gdg = []
            for h in range(N_HEADS):
                kc = slice(h * dk, (h + 1) * dk)
                vc = slice(h * dv, (h + 1) * dv)
                st = st_blk_ref[c, h]
                st_prev = jnp.where(c > 0, st_blk_ref[jnp.maximum(c - 1, 0), h], st_prev_ref[0, h])
                st_prev = jnp.where(first, 0.0, st_prev)
                qs = (q_ref[rs, kc] * qscale).astype(BF16)
                stb = st.astype(BF16)
                o = _nt(qs, stb)
                r = lax.rsqrt(jnp.mean(o * o, axis=-1, keepdims=True) + RMS_EPS)
                gv = g_ref[rs, vc]
                sg = _sigmoid(gv)
                dy = dy_ref[rs, vc]
                hn = hn_ref[:, vc]
                on = o * r
                dg_ref[rs, vc] = (dy * on * hn * (sg * (1.0 + gv * (1.0 - sg)))).astype(BF16)
                don = dy * (gv * sg)
                dhn_acc_ref[:, vc] += jnp.sum(don * on, axis=0, keepdims=True)
                dn = don * hn
                do = (r * dn - o * (r * r * r) * jnp.mean(dn * o, axis=-1, keepdims=True)).astype(BF16)
                dq_ref[rs, kc] = (_nn(do, stb) * qscale).astype(BF16)
                dst = dst_ref[h] + _tn(do, qs)
                dstb = dst.astype(BF16)
                kdec = kdec_ref[rs, kc]
                dv_ref[rs, vc] = _nt(kdec.astype(BF16), dstb).astype(BF16)
                dkdec = _nn(v_ref[rs, vc].astype(BF16), dstb)
                dk_ref[rs, kc] = (dkdec * dec_ref[rs, kc]).astype(BF16)
                e_ref[rs, kc] = dkdec * kdec
                gdg.append(jnp.sum(dst * st_prev, axis=0, keepdims=True) * gam[:, kc])
                dst_ref[h] = dst * gam[:, kc]
            dla_ref[rs, :] = jnp.broadcast_to(jnp.concatenate(gdg, axis=1), (CHUNK, DKT))
            return carry

        lax.fori_loop(0, ncb, chunk, 0)

        ev = e_ref[...]
        dla = dla_ref[...] + _chunk_scan(ev, rows, False) - ev
        dz = dla * (1.0 / GATE_TEMP) * (1.0 - _sigmoid(z))
        dzb = dz.astype(BF16)
        dalr_ref[...] = _nt(dzb, wa_ref[...]).astype(BF16)
        dwa = _tn(alr_ref[...].astype(BF16), dzb)
        dba = jnp.sum(dz, axis=0, keepdims=True)

        @pl.when(i == 0)
        def _():
            dwa_ref[...] = dwa
            dba_ref[...] = dba
            dhn_ref[...] = dhn_acc_ref[...]

        @pl.when(i > 0)
        def _():
            dwa_ref[...] += dwa
            dba_ref[...] += dba
            dhn_ref[...] += dhn_acc_ref[...]

    full = lambda shape: pl.BlockSpec(shape, lambda i: (0,) * len(shape))
    rowblk = lambda w: pl.BlockSpec((tb, w), lambda i: (rev(i), 0))
    return pl.pallas_call(
        body, name=name, grid=(nb,),
        in_specs=_gla_specs(DP, DKT, DV, tb, rev) + [
            pl.BlockSpec((ncb, N_HEADS, dv, dk), lambda i: (rev(i), 0, 0, 0)),
            pl.BlockSpec((1, N_HEADS, dv, dk), lambda i: (jnp.maximum(rev(i) * ncb - 1, 0), 0, 0, 0)),
            pl.BlockSpec((tb, DV), lambda i: (rev(i), DP // DV)),
            full((LANES, DKT)), full((1, DKT)), full((1, DV))],
        out_specs=[rowblk(DKT), rowblk(DKT), rowblk(DV), rowblk(DV), rowblk(LANES),
                   full((LANES, DKT)), full((1, DKT)), full((1, DV))],
        out_shape=[jax.ShapeDtypeStruct((S, DKT), BF16), jax.ShapeDtypeStruct((S, DKT), BF16),
                   jax.ShapeDtypeStruct((S, DV), BF16), jax.ShapeDtypeStruct((S, DV), BF16),
                   jax.ShapeDtypeStruct((S, LANES), BF16), jax.ShapeDtypeStruct((LANES, DKT), F32),
                   jax.ShapeDtypeStruct((1, DKT), F32), jax.ShapeDtypeStruct((1, DV), F32)],
        scratch_shapes=[pltpu.VMEM((N_HEADS, dv, dk), F32)] + [pltpu.VMEM((tb, DKT), F32)] * 5 + [pltpu.VMEM((1, DV), F32)],
        compiler_params=_params())(proj, proj, proj, proj, proj, states, states, dymix, w_a2, b_a, head_norm)


def _xattn_fwd(q, kv, name, tm=512):
    S, D = q.shape
    M = kv.shape[0]
    hd = D // N_HEADS
    tm = _tile(S, tm)
    scale = hd ** -0.5

    def body(q_ref, k_ref, v_ref, o_ref):
        for h in range(N_HEADS):
            hc = slice(h * hd, (h + 1) * hd)
            s = _nt(q_ref[:, hc], k_ref[:, hc]) * scale
            p = jnp.exp(s - jnp.max(s, axis=-1, keepdims=True))
            p = p / jnp.sum(p, axis=-1, keepdims=True)
            o_ref[:, hc] = _nn(p.astype(BF16), v_ref[:, hc]).astype(BF16)

    return pl.pallas_call(body, name=name, grid=(S // tm,),
                          in_specs=[pl.BlockSpec((tm, D), lambda i: (i, 0)), pl.BlockSpec((M, D), lambda i: (0, 0)),
                                    pl.BlockSpec((M, D), lambda i: (0, 1))],
                          out_specs=pl.BlockSpec((tm, D), lambda i: (i, 0)),
                          out_shape=jax.ShapeDtypeStruct((S, D), BF16), compiler_params=_params())(q, kv, kv)


def _xattn_bwd(q, kv, do, name, tm=512):
    S, D = q.shape
    M = kv.shape[0]
    hd = D // N_HEADS
    tm = _tile(S, tm)
    scale = hd ** -0.5

    def body(q_ref, k_ref, v_ref, do_ref, dq_ref, dkv_ref):
        first = pl.program_id(0) == 0
        for h in range(N_HEADS):
            hc = slice(h * hd, (h + 1) * hd)
            vcols = slice(D + h * hd, D + (h + 1) * hd)
            qh = q_ref[:, hc]
            kh = k_ref[:, hc]
            doh = do_ref[:, hc]
            s = _nt(qh, kh) * scale
            p = jnp.exp(s - jnp.max(s, axis=-1, keepdims=True))
            p = p / jnp.sum(p, axis=-1, keepdims=True)
            dvh = _tn(p.astype(BF16), doh)
            dp = _nt(doh, v_ref[:, hc])
            ds = ((p * (dp - jnp.sum(dp * p, axis=-1, keepdims=True))) * scale).astype(BF16)
            dq_ref[:, hc] = _nn(ds, kh).astype(BF16)
            dkh = _tn(ds, qh)

            @pl.when(first)
            def _():
                dkv_ref[:, hc] = dkh
                dkv_ref[:, vcols] = dvh

            @pl.when(jnp.logical_not(first))
            def _():
                dkv_ref[:, hc] += dkh
                dkv_ref[:, vcols] += dvh

    row = pl.BlockSpec((tm, D), lambda i: (i, 0))
    return pl.pallas_call(body, name=name, grid=(S // tm,),
                          in_specs=[row, pl.BlockSpec((M, D), lambda i: (0, 0)), pl.BlockSpec((M, D), lambda i: (0, 1)), row],
                          out_specs=[row, pl.BlockSpec((M, 2 * D), lambda i: (0, 0))],
                          out_shape=[jax.ShapeDtypeStruct((S, D), BF16), jax.ShapeDtypeStruct((M, 2 * D), F32)],
                          compiler_params=_params())(q, kv, kv, do)


def _local_step(x, mem, target, vec, weight, emit, dep0):
    DP = vec["pool_scale"].shape[1]
    g = {}
    pending = []
    begun = []
    emit_begin, emit_finish = emit

    def behind(fn, *a, **kw):
        dep = tuple(pending)
        pending.clear()
        out = fn(*a, dep=dep, **kw)
        while begun:
            pending.append(emit_finish(begun.pop(0), out))
        return out

    def mm(a, b, **kw):
        return behind(_matmul, a, b, **kw)

    def send(name, gfull):
        pending.append(emit_begin(name, gfull))
        begun.append(name)

    def ffn_fwd(xin, tag, dep):
        h = _rms_fwd(xin, vec[f"{tag}_norm"], f"{tag}_norm", dep=dep)
        a, b, hid = _ffn_up(h, weight(f"{tag}_w_gate", h), weight(f"{tag}_w_up", h), f"{tag}_up")
        wd = weight(f"{tag}_w_down", a)
        G, Fj, D = wd.shape
        xo = _matmul(hid, wd.reshape(G * Fj, D), mode="nn", name=f"{tag}_down", out_dtype=F32, res=xin, scale=0.5,
                     tn=1024, tk=G * Fj)
        return xo, (h, a, b)

    def ffn_bwd(dxh, saved, tag, last):
        h, a, b = saved
        wg, wu, wd = weight(f"{tag}_w_gate"), weight(f"{tag}_w_up"), weight(f"{tag}_w_down")
        G, Fj, D = wd.shape
        da, db, hid = _ffn_dact(dxh, wd, a, b, f"{tag}_dact")
        send(f"{tag}_w_down", mm(hid, dxh, mode="tn", name=f"{tag}_dwd", out_dtype=F32, tm=Fj, tn=1024))
        send(f"{tag}_w_gate", mm(h, da, mode="tn", name=f"{tag}_dwg", out_dtype=F32, tm=1024, tn=Fj, out_groups=G))
        dwu = mm(h, db, mode="tn", name=f"{tag}_dwu", out_dtype=F32, tm=1024, tn=Fj, out_groups=G)
        if last:
            pending.append(emit_begin(f"{tag}_w_up", dwu))
            pending.append(emit_finish(f"{tag}_w_up", pending[-1]))
        else:
            send(f"{tag}_w_up", dwu)
        return behind(_ffn_dh, da, db, wg, wu, f"{tag}_dh")

    x1, ffn1_saved = ffn_fwd(x, "ffn1", dep0)
    h2 = _rms_fwd(x1, vec["mix_norm"], "mix_norm")
    w_in = weight("w_in", h2)
    proj = _matmul(h2, w_in, mode="nn", name="w_in", out_dtype=F32, tn=1408)
    pool_w, w_a2 = weight("pool_w", h2), weight("gla_w_a2", h2)
    y_pool, dpool = _pool_fwd(proj, pool_w, vec["pool_scale"], "pool_fwd")
    y_gla, states = _gla_fwd(proj, w_a2, vec["gla_b_a"], vec["gla_head_norm"], DP, "gla_fwd")
    ymix = jnp.concatenate([y_pool, y_gla], axis=1)
    w_out = weight("w_out", y_gla)
    x2 = _matmul(ymix, w_out, mode="nn", name="w_out", out_dtype=F32, res=x1)
    h3 = _rms_fwd(x2, vec["xattn_norm"], "xattn_norm")
    mh = _rms_fwd(mem, vec["mem_norm"], "mem_norm")
    w_q = weight("xattn_w_q", h3)
    q = _matmul(h3, w_q, mode="nn", name="xattn_q", out_dtype=BF16)
    w_kv = weight("xattn_w_kv", q)
    kv = _matmul(mh, w_kv, mode="nn", name="xattn_kv", out_dtype=BF16, b_groups=True, tn=1024)
    o = _xattn_fwd(q, kv, "xattn_fwd")
    w_o = weight("xattn_w_o", o)
    x3 = _matmul(o, w_o, mode="nn", name="xattn_o", out_dtype=F32, res=x2)
    x4, ffn2_saved = ffn_fwd(x3, "ffn2", None)
    sq, dx4, dx4h, g["final_norm"] = _loss_head(x4, vec["final_norm"], target, "loss_head")

    dh = ffn_bwd(dx4h, ffn2_saved, "ffn2", False)
    dx3, g["ffn2_norm"] = _rms_bwd(x3, vec["ffn2_norm"], dh, dx4, "ffn2_norm_bwd")
    dx3b = _cast(dx3, BF16, "dx3_cast")
    send("xattn_w_o", mm(o, dx3b, mode="tn", name="xattn_dwo", out_dtype=F32, tm=1024, tn=1024))
    do = mm(dx3b, w_o, mode="nt", name="xattn_do", out_dtype=BF16)
    dq, dkv = _xattn_bwd(q, kv, do, "xattn_bwd")
    send("xattn_w_q", mm(h3, dq, mode="tn", name="xattn_dwq", out_dtype=F32, tm=1024, tn=1024))
    dh3 = mm(dq, w_q, mode="nt", name="xattn_dh", out_dtype=F32)
    dkvb = _cast(dkv, BF16, "dkv_cast")
    send("xattn_w_kv", mm(mh, dkvb, mode="tn", name="xattn_dwkv", out_dtype=F32, tm=1024, tn=1024, out_groups=N_SHARDS))
    dmh = mm(dkvb, w_kv, mode="nt", name="xattn_dmh", out_dtype=F32, b_groups=True, tk=1024)
    _, g["mem_norm"] = _rms_bwd(mem, vec["mem_norm"], dmh, None, "mem_norm_bwd")
    pending.append(g["mem_norm"])
    dx2, g["xattn_norm"] = _rms_bwd(x2, vec["xattn_norm"], dh3, dx3, "xattn_norm_bwd")
    dx2b = _cast(dx2, BF16, "dx2_cast")
    send("w_out", mm(ymix, dx2b, mode="tn", name="dw_out", out_dtype=F32, tm=1024, tn=1024))
    dymix = mm(dx2b, w_out, mode="nt", name="dymix", out_dtype=F32)
    du, dpool_w, g["pool_scale"] = _pool_bwd(dymix, dpool, pool_w, vec["pool_scale"], "pool_bwd")
    send("pool_w", dpool_w)
    dq_g, dk_g, dv_g, dg_g, dalr, dw_a2, g["gla_b_a"], g["gla_head_norm"] = _gla_bwd(
        proj, states, dymix, w_a2, vec["gla_b_a"], vec["gla_head_norm"], DP, "gla_bwd")
    send("gla_w_a2", dw_a2)
    dproj = jnp.concatenate([du, dq_g, dk_g, dv_g, dg_g, dalr], axis=1)
    send("w_in", mm(h2, dproj, mode="tn", name="dw_in", out_dtype=F32, tm=1024, tn=1408))
    dh2 = mm(dproj, w_in, mode="nt", name="dh2", out_dtype=F32, tn=1024, tk=dproj.shape[1])
    dx1, dx1h, g["mix_norm"] = _rms_bwd(x1, vec["mix_norm"], dh2, dx2, "mix_norm_bwd", half=True)
    dh = ffn_bwd(dx1h, ffn1_saved, "ffn1", True)
    dx0, g["ffn1_norm"] = _rms_bwd(x, vec["ffn1_norm"], dh, dx1, "ffn1_norm_bwd")
    return sq, dx0, g


def _place():
    x, y, c = lax.axis_index("x"), lax.axis_index("y"), lax.axis_index("c")
    chips = [(1 - x, y), (x, 1 - y), (1 - x, 1 - y)]
    return x, y, c, chips


def _ids():
    return jnp.stack([2 * lax.axis_index("x") + lax.axis_index("y"), lax.axis_index("c")]).astype(jnp.int32)


def _hbm(a):
    return pltpu.with_memory_space_constraint(a, pltpu.HBM)


def _cast_to_slot(w2d, dtype, name, dep=None):
    R, C = w2d.shape
    tr = _tile(R, max(16, (4 << 20) // (4 * C) // 16 * 16))

    def body(i_ref, w_ref, *rest):
        rest[-1][...] = w_ref[...].astype(dtype)

    in_specs = [pl.BlockSpec((tr, C), lambda r, i: (r, 0))]
    operands = [w2d]
    if dep is not None:
        in_specs.append(pl.BlockSpec(dep.shape, lambda r, i: (0, 0)))
        operands.append(dep)
    grid_spec = pltpu.PrefetchScalarGridSpec(num_scalar_prefetch=1, grid=(R // tr,), in_specs=in_specs,
                                             out_specs=pl.BlockSpec((None, tr, C), lambda r, i: (i[0], r, 0)))
    return pl.pallas_call(body, name=name, grid_spec=grid_spec, out_shape=jax.ShapeDtypeStruct((N_SHARDS, R, C), dtype),
                          compiler_params=_params())(_ids(), *operands)


def _gather_copies(buf_ref, send_sems, recv_sems, incoming):
    x, y, c, chips = _place()
    hr = buf_ref.shape[1] // 2
    copies = []
    for j, (px, py) in enumerate(chips):
        slot = 2 * px + py if incoming else 2 * x + y
        half = buf_ref.at[slot, pl.ds(c * hr, hr), :]
        copies.append(pltpu.make_async_remote_copy(src_ref=half, dst_ref=half, send_sem=send_sems.at[j],
                                                   recv_sem=recv_sems.at[j], device_id=(px, py, c), device_id_type=MESH))
    return copies


def _gather_start(buf, name):
    def body(b_ref, send_sems, recv_sems, b_thru, token):
        for cp in _gather_copies(b_ref, send_sems, recv_sems, False):
            cp.start()
        token[...] = jnp.zeros_like(token)

    return pl.pallas_call(
        body, name=name,
        out_shape=(pltpu.SemaphoreType.DMA((3,)), pltpu.SemaphoreType.DMA((3,)), pltpu.HBM(buf.shape, buf.dtype),
                   jax.ShapeDtypeStruct((8, LANES), F32)),
        in_specs=(HBM,), out_specs=(SEM, SEM, HBM, pl.BlockSpec(memory_space=pltpu.VMEM)), input_output_aliases={0: 2},
        compiler_params=pltpu.CompilerParams(has_side_effects=EFFECT))(_hbm(buf))


def _gather_wait(send_sems, recv_sems, buf_thru, after, name):
    def body(b_ref, send_sems, recv_sems, after_ref, b_out):
        for cp in _gather_copies(b_ref, send_sems, recv_sems, False):
            cp.wait_send()
        for cp in _gather_copies(b_ref, send_sems, recv_sems, True):
            cp.wait_recv()

    return pl.pallas_call(
        body, name=name, out_shape=pltpu.HBM(buf_thru.shape, buf_thru.dtype),
        in_specs=(HBM, SEM, SEM, ANY), out_specs=HBM, input_output_aliases={0: 0},
        compiler_params=pltpu.CompilerParams(has_side_effects=EFFECT))(buf_thru, send_sems, recv_sems, after)


def _gather_forward(buf, name):
    G, R, C = buf.shape
    hr = R // 2

    def body(b_ref, o_ref, send_sems, recv_sems):
        x, y, c, chips = _place()
        copies = []
        for j, (px, py) in enumerate(chips):
            half = o_ref.at[2 * px + py, pl.ds(c * hr, hr), :]
            copies.append(pltpu.make_async_remote_copy(src_ref=half, dst_ref=half, send_sem=send_sems.at[j],
                                                       recv_sem=recv_sems.at[j], device_id=(x, y, 1 - c),
                                                       device_id_type=MESH))
        for cp in copies:
            cp.start()
        for j, (px, py) in enumerate(chips):
            half = o_ref.at[2 * px + py, pl.ds((1 - c) * hr, hr), :]
            pltpu.make_async_remote_copy(src_ref=half, dst_ref=half, send_sem=send_sems.at[j], recv_sem=recv_sems.at[j],
                                         device_id=(x, y, 1 - c), device_id_type=MESH).wait_recv()
        for cp in copies:
            cp.wait_send()

    return pl.pallas_call(body, name=name, in_specs=[ANY], out_specs=ANY, out_shape=jax.ShapeDtypeStruct(buf.shape, buf.dtype),
                          input_output_aliases={0: 0},
                          scratch_shapes=[pltpu.SemaphoreType.DMA((3,)), pltpu.SemaphoreType.DMA((3,))])(buf)


def _pair_copy(g_ref, land_ref, send_sem, recv_sem):
    x, y, c, _ = _place()
    hr = g_ref.shape[1] // 2
    return pltpu.make_async_remote_copy(src_ref=g_ref.at[:, pl.ds((1 - c) * hr, hr), :], dst_ref=land_ref,
                                        send_sem=send_sem, recv_sem=recv_sem, device_id=(x, y, 1 - c), device_id_type=MESH)


def _pair_start(gfull, name):
    G, R, C = gfull.shape

    def body(g_ref, land_ref, send_sem, recv_sem, g_thru, land_thru, token):
        _pair_copy(g_ref, land_ref, send_sem, recv_sem).start()
        token[...] = jnp.zeros_like(token)

    return pl.pallas_call(
        body, name=name,
        out_shape=(pltpu.SemaphoreType.DMA(()), pltpu.SemaphoreType.DMA(()), pltpu.HBM(gfull.shape, F32),
                   pltpu.HBM((G, R // 2, C), F32), jax.ShapeDtypeStruct((8, LANES), F32)),
        in_specs=(HBM, HBM), out_specs=(SEM, SEM, HBM, HBM, pl.BlockSpec(memory_space=pltpu.VMEM)),
        input_output_aliases={0: 2, 1: 3},
        compiler_params=pltpu.CompilerParams(has_side_effects=EFFECT))(_hbm(gfull), _hbm(lax.empty((G, R // 2, C), F32)))


def _pair_wait(send_sem, recv_sem, g_thru, land_thru, after, name):
    def body(g_ref, land_ref, send_sem, recv_sem, after_ref, g_out, land_out):
        cp = _pair_copy(g_ref, land_ref, send_sem, recv_sem)
        cp.wait_send()
        cp.wait_recv()

    return pl.pallas_call(
        body, name=name, out_shape=(pltpu.HBM(g_thru.shape, F32), pltpu.HBM(land_thru.shape, F32)),
        in_specs=(HBM, HBM, SEM, SEM, ANY), out_specs=(HBM, HBM), input_output_aliases={0: 0, 1: 1},
        compiler_params=pltpu.CompilerParams(has_side_effects=EFFECT))(g_thru, land_thru, send_sem, recv_sem, after)


def _pair_add(gfull, other, name):
    G, R, C = gfull.shape
    hr = R // 2
    tr = _tile(hr, max(8, (2 * 1024 * 1024) // (4 * C) // 8 * 8))
    nr = hr // tr
    c = lax.axis_index("c")
    cidx = jnp.reshape(c, (1,)).astype(jnp.int32)

    def body(c_ref, a_ref, b_ref, o_ref):
        o_ref[...] = a_ref[...] + b_ref[...]

    grid_spec = pltpu.PrefetchScalarGridSpec(
        num_scalar_prefetch=1, grid=(G, nr),
        in_specs=[pl.BlockSpec((None, tr, C), lambda g, r, cr: (g, cr[0] * nr + r, 0)),
                  pl.BlockSpec((None, tr, C), lambda g, r, cr: (g, r, 0))],
        out_specs=pl.BlockSpec((None, tr, C), lambda g, r, cr: (g, r, 0)))
    return pl.pallas_call(body, name=name, grid_spec=grid_spec, out_shape=jax.ShapeDtypeStruct((G, hr, C), F32),
                          compiler_params=_params())(cidx, gfull, other)


def _chip_copies(p_ref, land_ref, send_sems, recv_sems, incoming):
    x, y, c, chips = _place()
    me = 2 * x + y
    copies = []
    for j, (px, py) in enumerate(chips):
        dst = land_ref.at[2 * px + py] if incoming else land_ref.at[me]
        copies.append(pltpu.make_async_remote_copy(src_ref=p_ref.at[2 * px + py], dst_ref=dst, send_sem=send_sems.at[j],
                                                   recv_sem=recv_sems.at[j], device_id=(px, py, c), device_id_type=MESH))
    return copies


def _chip_start(part, name):
    def body(p_ref, land_ref, send_sems, recv_sems, p_thru, land_thru, token):
        for cp in _chip_copies(p_ref, land_ref, send_sems, recv_sems, False):
            cp.start()
        token[...] = jnp.zeros_like(token)

    return pl.pallas_call(
        body, name=name,
        out_shape=(pltpu.SemaphoreType.DMA((3,)), pltpu.SemaphoreType.DMA((3,)), pltpu.HBM(part.shape, F32),
                   pltpu.HBM(part.shape, F32), jax.ShapeDtypeStruct((8, LANES), F32)),
        in_specs=(HBM, HBM), out_specs=(SEM, SEM, HBM, HBM, pl.BlockSpec(memory_space=pltpu.VMEM)),
        input_output_aliases={0: 2, 1: 3},
        compiler_params=pltpu.CompilerParams(has_side_effects=EFFECT))(_hbm(part), _hbm(lax.empty(part.shape, F32)))


def _chip_wait(send_sems, recv_sems, p_thru, land_thru, after, name):
    def body(p_ref, land_ref, send_sems, recv_sems, after_ref, p_out, land_out):
        for cp in _chip_copies(p_ref, land_ref, send_sems, recv_sems, False):
            cp.wait_send()
        for cp in _chip_copies(p_ref, land_ref, send_sems, recv_sems, True):
            cp.wait_recv()

    return pl.pallas_call(
        body, name=name, out_shape=(pltpu.HBM(p_thru.shape, F32), pltpu.HBM(p_thru.shape, F32)),
        in_specs=(HBM, HBM, SEM, SEM, ANY), out_specs=(HBM, HBM), input_output_aliases={0: 0, 1: 1},
        compiler_params=pltpu.CompilerParams(has_side_effects=EFFECT))(p_thru, land_thru, send_sems, recv_sems, after)


def _chip_sum(part, slots, name):
    G, R2, C = part.shape
    tr = _tile(R2, max(8, (1 << 20) // (4 * C) // 8 * 8))
    nr = R2 // tr

    def body(i_ref, p_ref, *rest):
        o_ref = rest[-1]
        acc = None
        for u in range(G):
            val = jnp.where(i_ref[0] == u, p_ref[...], rest[u][...])
            acc = val if acc is None else acc + val
        o_ref[...] = acc

    def slot_spec(u):
        return pl.BlockSpec((None, tr, C), lambda r, i: (jnp.where(i[0] == u, (u + 1) % G, u), r, 0))

    grid_spec = pltpu.PrefetchScalarGridSpec(
        num_scalar_prefetch=1, grid=(nr,),
        in_specs=[pl.BlockSpec((None, tr, C), lambda r, i: (i[0], r, 0))] + [slot_spec(u) for u in range(G)],
        out_specs=pl.BlockSpec((tr, C), lambda r, i: (i[1] * nr + r, 0)))
    return pl.pallas_call(body, name=name, grid_spec=grid_spec, out_shape=jax.ShapeDtypeStruct((2 * R2, C), F32),
                          compiler_params=_params())(_ids(), part, slots, slots, slots, slots)


def _sum_slots(slots, name):
    G, R2, C = slots.shape
    tr = _tile(R2, max(8, (1024 * 1024) // (4 * C) // 8 * 8))

    def body(s_ref, o_ref):
        acc = s_ref[0]
        for u in range(1, G):
            acc = acc + s_ref[u]
        o_ref[...] = acc

    return pl.pallas_call(body, name=name, grid=(R2 // tr,), in_specs=[pl.BlockSpec((G, tr, C), lambda r: (0, r, 0))],
                          out_specs=pl.BlockSpec((tr, C), lambda r: (r, 0)), out_shape=jax.ShapeDtypeStruct((R2, C), F32),
                          compiler_params=_params())(slots)


def _pair_join(full, name):
    R, C = full.shape
    R2 = R // 2

    def body(f_ref, o_ref, send_sem, recv_sem):
        x, y, c, _ = _place()
        mine = o_ref.at[pl.ds(c * R2, R2), :]
        theirs = o_ref.at[pl.ds((1 - c) * R2, R2), :]
        cp = pltpu.make_async_remote_copy(src_ref=mine, dst_ref=mine, send_sem=send_sem, recv_sem=recv_sem,
                                          device_id=(x, y, 1 - c), device_id_type=MESH)
        cp.start()
        pltpu.make_async_remote_copy(src_ref=theirs, dst_ref=theirs, send_sem=send_sem, recv_sem=recv_sem,
                                     device_id=(x, y, 1 - c), device_id_type=MESH).wait_recv()
        cp.wait_send()

    return pl.pallas_call(body, name=name, in_specs=[ANY], out_specs=ANY, out_shape=jax.ShapeDtypeStruct((R, C), F32),
                          input_output_aliases={0: 0},
                          scratch_shapes=[pltpu.SemaphoreType.DMA, pltpu.SemaphoreType.DMA])(full)


def _all_reduce_small(v, name):
    R, C = v.shape

    def gather_body(v_ref, out_ref, send_sems, recv_sems, local_sem):
        x, y, c, _ = _place()
        me = 4 * x + 2 * y + c
        mine = pltpu.make_async_copy(v_ref, out_ref.at[me], local_sem)
        mine.start()
        flips = [(fx, fy, fc) for fx in (0, 1) for fy in (0, 1) for fc in (0, 1)][1:]
        copies = []
        for j, (fx, fy, fc) in enumerate(flips):
            peer = (x ^ fx, y ^ fy, c ^ fc)
            copies.append(pltpu.make_async_remote_copy(src_ref=v_ref, dst_ref=out_ref.at[me], send_sem=send_sems.at[j],
                                                       recv_sem=recv_sems.at[j], device_id=peer, device_id_type=MESH))
        for cp in copies:
            cp.start()
        for j, (fx, fy, fc) in enumerate(flips):
            peer = (x ^ fx, y ^ fy, c ^ fc)
            pltpu.make_async_remote_copy(src_ref=v_ref, dst_ref=out_ref.at[4 * peer[0] + 2 * peer[1] + peer[2]],
                                         send_sem=send_sems.at[j], recv_sem=recv_sems.at[j], device_id=peer,
                                         device_id_type=MESH).wait_recv()
        for cp in copies:
            cp.wait_send()
        mine.wait()

    slots = pl.pallas_call(gather_body, name=name, in_specs=[ANY], out_specs=ANY,
                           out_shape=jax.ShapeDtypeStruct((8, R, C), F32),
                           scratch_shapes=[pltpu.SemaphoreType.DMA((7,)), pltpu.SemaphoreType.DMA((7,)),
                                           pltpu.SemaphoreType.DMA])(v)
    return _sum_slots(slots, f"{name}_sum")


def _adamw(w, g, m, v, name):
    R, C = w.shape
    tr = _tile(R, max(8, (2 << 20) // (4 * C) // 8 * 8))
    bc1 = 1.0 - ADAM_B1 ** ADAM_STEP
    bc2 = 1.0 - ADAM_B2 ** ADAM_STEP

    def body(w_ref, g_ref, m_ref, v_ref, d_ref, nm_ref, nv_ref):
        gv = g_ref[...]
        nm = ADAM_B1 * m_ref[...] + (1.0 - ADAM_B1) * gv
        nv = ADAM_B2 * v_ref[...] + (1.0 - ADAM_B2) * (gv * gv)
        nm_ref[...] = nm
        nv_ref[...] = nv
        d_ref[...] = -ADAM_LR * ((nm / bc1) / (jnp.sqrt(nv / bc2) + ADAM_EPS) + ADAM_WD * w_ref[...])

    blk = pl.BlockSpec((tr, C), lambda r: (r, 0))
    out = jax.ShapeDtypeStruct((R, C), F32)
    return pl.pallas_call(body, name=name, grid=(R // tr,), in_specs=[blk] * 4, out_specs=[blk] * 3, out_shape=[out] * 3,
                          compiler_params=_params())(w, g, m, v)


WEIGHTS = ['ffn1_norm', 'ffn1_w_gate', 'ffn1_w_up', 'ffn1_w_down', 'mix_norm', 'w_in', 'pool_w', 'pool_scale', 'gla_w_a2',
           'gla_b_a', 'gla_head_norm', 'w_out', 'xattn_norm', 'mem_norm', 'xattn_w_q', 'xattn_w_kv', 'xattn_w_o', 'ffn2_norm',
           'ffn2_w_gate', 'ffn2_w_up', 'ffn2_w_down', 'final_norm']
SHARDED = ['ffn1_w_gate', 'ffn1_w_up', 'ffn1_w_down', 'w_in', 'pool_w', 'gla_w_a2', 'w_out', 'xattn_w_q', 'xattn_w_kv',
           'xattn_w_o', 'ffn2_w_gate', 'ffn2_w_up', 'ffn2_w_down']
REPLICATED = [n for n in WEIGHTS if n not in SHARDED]
SMALL_COLS = 512


def _as2d(a):
    return a.reshape(-1, a.shape[-1])


def _finish_weight(name, gathered, wl):
    G, R, C = gathered.shape
    rank = wl["gla_w_a2"].shape[1]
    if name in ("w_out", "xattn_w_q", "xattn_w_o"):
        return gathered.reshape(G * R, C)
    if name == "w_in":
        w_in = jnp.transpose(gathered, (1, 0, 2)).reshape(R, G * C)
        main = G * C - rank
        return jnp.concatenate([w_in[:, :main], jnp.pad(w_in[:, main:], ((0, 0), (0, LANES - rank)))], axis=1)
    if name == "pool_w":
        NG, CJ, _ = wl[name].shape[1:]
        return jnp.transpose(gathered.reshape(G, NG, CJ, C), (1, 0, 2, 3)).reshape(NG, G * CJ, C)
    if name == "gla_w_a2":
        a2 = jnp.transpose(gathered, (1, 0, 2)).reshape(rank, G * C)
        return jnp.pad(a2, ((0, LANES - rank), (0, 0))).astype(BF16)
    return gathered


def _start_gathers(wl):
    started = {}
    token = None
    for n in SHARDED:
        dtype = F32 if n == "gla_w_a2" else BF16
        buf = _cast_to_slot(_as2d(wl[n]), dtype, f"slot_{n}", dep=token)
        send_sems, recv_sems, thru, token = _gather_start(buf, f"gather_start_{n}")
        started[n] = (send_sems, recv_sems, thru)
    cache = {}

    def weight(n, after=None):
        if n not in cache:
            buf = _gather_wait(*started[n], after, f"gather_wait_{n}")
            cache[n] = _finish_weight(n, _gather_forward(buf, f"gather_forward_{n}"), wl)
        return cache[n]

    return weight, token


def _shard_major(name, gfull, wl):
    R, C = _as2d(wl[name]).shape
    if name in ("ffn1_w_gate", "ffn1_w_up", "ffn2_w_gate", "ffn2_w_up", "xattn_w_kv"):
        return gfull
    if name in ("ffn1_w_down", "ffn2_w_down", "w_out", "xattn_w_q", "xattn_w_o"):
        return gfull.reshape(N_SHARDS, R, C)
    if name == "w_in":
        return jnp.transpose(gfull[:, :N_SHARDS * C].reshape(R, N_SHARDS, C), (1, 0, 2))
    if name == "pool_w":
        NG, CJ, _ = wl[name].shape[1:]
        return jnp.transpose(gfull.reshape(NG, N_SHARDS, CJ, C), (1, 0, 2, 3)).reshape(N_SHARDS, R, C)
    assert name == "gla_w_a2"
    return jnp.transpose(gfull[:R].reshape(R, N_SHARDS, C), (1, 0, 2))


def kernel(x, mem, ffn1_norm, ffn1_w_gate, ffn1_w_up, ffn1_w_down, mix_norm, w_in, pool_w, pool_scale, gla_w_a2, gla_b_a, gla_head_norm, w_out, xattn_norm, mem_norm, xattn_w_q, xattn_w_kv, xattn_w_o, ffn2_norm, ffn2_w_gate, ffn2_w_up, ffn2_w_down, final_norm, loss_target, m_ffn1_norm, m_ffn1_w_gate, m_ffn1_w_up, m_ffn1_w_down, m_mix_norm, m_w_in, m_pool_w, m_pool_scale, m_gla_w_a2, m_gla_b_a, m_gla_head_norm, m_w_out, m_xattn_norm, m_mem_norm, m_xattn_w_q, m_xattn_w_kv, m_xattn_w_o, m_ffn2_norm, m_ffn2_w_gate, m_ffn2_w_up, m_ffn2_w_down, m_final_norm, v_ffn1_norm, v_ffn1_w_gate, v_ffn1_w_up, v_ffn1_w_down, v_mix_norm, v_w_in, v_pool_w, v_pool_scale, v_gla_w_a2, v_gla_b_a, v_gla_head_norm, v_w_out, v_xattn_norm, v_mem_norm, v_xattn_w_q, v_xattn_w_kv, v_xattn_w_o, v_ffn2_norm, v_ffn2_w_gate, v_ffn2_w_up, v_ffn2_w_down, v_final_norm):
    given = dict(locals())
    wl = {n: given[n] for n in WEIGHTS}
    ml = {n: given["m_" + n] for n in WEIGHTS}
    vl = {n: given["v_" + n] for n in WEIGHTS}

    vec = {n: wl[n].reshape(1, -1) for n in REPLICATED}
    weight, dep0 = _start_gathers(wl)
    in_flight = {}

    pair_flight = {}

    def emit_begin(n, gfull):
        *pair_flight[n], token = _pair_start(_shard_major(n, gfull, wl), f"{n}_pair_start")
        return token

    def emit_finish(n, after):
        gsm, other = _pair_wait(*pair_flight.pop(n), after, f"{n}_pair_wait")
        *in_flight[n], token = _chip_start(_pair_add(gsm, other, f"{n}_pair_add"), f"{n}_chip_start")
        return token

    sq, dx0, g = _local_step(x[0], mem[0], loss_target[0], vec, weight, (emit_begin, emit_finish), dep0)
    loss = lax.psum(0.5 * jnp.sum(sq) / x.shape[-1], ("x", "y", "c"))

    grads = {}
    for n in in_flight:
        part, slots = _chip_wait(*in_flight[n], dx0, f"{n}_chip_wait")
        grads[n] = _pair_join(_chip_sum(part, slots, f"{n}_chip_sum"), f"{n}_pair_join")
    widths = [wl[n].size for n in REPLICATED]
    total = sum(widths)
    rows = -(-total // SMALL_COLS)
    rows = -(-rows // 8) * 8
    packed = jnp.concatenate([g[n].reshape(-1) for n in REPLICATED] + [jnp.zeros((rows * SMALL_COLS - total,), F32)])
    summed = _all_reduce_small(packed.reshape(rows, SMALL_COLS), "small_all_reduce").reshape(-1)
    off = 0
    for n, width in zip(REPLICATED, widths):
        grads[n] = summed[off:off + width].reshape(1, width)
        off += width

    out_g, out_d, out_m, out_v = [], [], [], []
    for n in WEIGHTS:
        shape = wl[n].shape
        g2 = grads[n]
        d, nm, nv = _adamw(wl[n].reshape(g2.shape), g2, ml[n].reshape(g2.shape), vl[n].reshape(g2.shape), f"adamw_{n}")
        out_g.append(g2.reshape(shape))
        out_d.append(d.reshape(shape))
        out_m.append(nm.reshape(shape))
        out_v.append(nv.reshape(shape))
    return (loss, dx0.reshape(x.shape), *out_g, *out_d, *out_m, *out_v)
```

```python
import functools

import jax
import jax.numpy as jnp
from jax import lax
from jax.experimental import pallas as pl
from jax.experimental.pallas import tpu as pltpu

F32 = jnp.float32
BF16 = jnp.bfloat16
MESH = pl.DeviceIdType.MESH

RMS_EPS = 1e-6
CHUNK = 64
POOL_WINDOWS = (2, 4, 8, 16)
POOL_HALO = 16
N_HEADS = 4
GATE_TEMP = 16.0
ADAM_LR, ADAM_B1, ADAM_B2, ADAM_EPS, ADAM_WD, ADAM_STEP = 0.001, 0.9, 0.999, 1e-08, 0.01, 10
N_SHARDS = 4
LANES = 128
MXU_COLS = 256
VMEM_LIMIT = 58 * 1024 * 1024

ANY = pl.BlockSpec(memory_space=pl.ANY)
HBM = pl.BlockSpec(memory_space=pltpu.HBM)
SEM = pl.BlockSpec(memory_space=pltpu.SEMAPHORE)
EFFECT = pltpu.SideEffectType.DATAFLOW_SIDE_EFFECTING


def _params(**kw):
    return pltpu.CompilerParams(vmem_limit_bytes=VMEM_LIMIT, **kw)


def _tile(n, want):
    for unit in (LANES, 8):
        t = (min(want, n) // unit) * unit
        while t >= unit:
            if n % t == 0:
                return t
            t -= unit
    return n


def _dot(a, b, dims):
    return lax.dot_general(a, b, (dims, ((), ())), preferred_element_type=F32)


def _nn(a, b):
    return _dot(a, b, ((1,), (0,)))


def _nt(a, b):
    return _dot(a, b, ((1,), (1,)))


def _tn(a, b):
    return _dot(a, b, ((0,), (0,)))


def _sigmoid(x):
    return 1.0 / (1.0 + jnp.exp(-x))


def _matmul(a, b, *, mode, name, out_dtype, tm=512, tn=2048, tk=2048, res=None, scale=1.0, b_groups=False, out_groups=0,
            dep=()):
    if mode == "tn":
        K, M = a.shape
    else:
        M, K = a.shape
    if mode == "nn":
        if b_groups:
            G, _, Nj = b.shape
            N = G * Nj
        else:
            N = b.shape[1]
    elif mode == "nt":
        if b_groups:
            G, N, Kj = b.shape
            assert G * Kj == K
        else:
            N = b.shape[0]
    else:
        N = b.shape[1]
    tm = _tile(M, tm)
    if mode == "nn" and b_groups:
        tn = _tile(Nj, tn)
    elif out_groups:
        tn = _tile(N // out_groups, tn)
    else:
        tn = _tile(N, tn)
    if mode == "nt" and b_groups:
        tk = _tile(Kj, tk)
    else:
        tk = _tile(K, tk)
    nk = K // tk
    grid = (M // tm, N // tn, nk)

    if mode == "tn":
        a_spec = pl.BlockSpec((tk, tm), lambda i, j, k: (k, i))
        b_spec = pl.BlockSpec((tk, tn), lambda i, j, k: (k, j))
        dims = ((0,), (0,))
    elif mode == "nn":
        a_spec = pl.BlockSpec((tm, tk), lambda i, j, k: (i, k))
        if b_groups:
            npj = Nj // tn
            b_spec = pl.BlockSpec((None, tk, tn), lambda i, j, k: (j // npj, k, j % npj))
        else:
            b_spec = pl.BlockSpec((tk, tn), lambda i, j, k: (k, j))
        dims = ((1,), (0,))
    else:
        a_spec = pl.BlockSpec((tm, tk), lambda i, j, k: (i, k))
        if b_groups:
            kpj = Kj // tk
            b_spec = pl.BlockSpec((None, tn, tk), lambda i, j, k: (k // kpj, j, k % kpj))
        else:
            b_spec = pl.BlockSpec((tn, tk), lambda i, j, k: (j, k))
        dims = ((1,), (1,))
    if out_groups:
        npj = (N // out_groups) // tn
        o_spec = pl.BlockSpec((None, tm, tn), lambda i, j, k: (j // npj, i, j % npj))
        out_shape = jax.ShapeDtypeStruct((out_groups, M, N // out_groups), out_dtype)
    else:
        o_spec = pl.BlockSpec((tm, tn), lambda i, j, k: (i, j))
        out_shape = jax.ShapeDtypeStruct((M, N), out_dtype)
    in_specs = [a_spec, b_spec]
    operands = [a, b]
    if res is not None:
        in_specs.append(pl.BlockSpec((tm, tn), lambda i, j, k: (i, j)))
        operands.append(res)
    has_res = res is not None
    n_dep = len(dep)
    for d in dep:
        in_specs.append(pl.BlockSpec(d.shape, lambda i, j, k: (0, 0)))
        operands.append(d)

    def body(*refs):
        if has_res:
            a_ref, b_ref, r_ref = refs[:3]
        else:
            a_ref, b_ref = refs[:2]
            r_ref = None
        o_ref = refs[2 + has_res + n_dep]

        def finish(acc):
            if scale != 1.0:
                acc = acc * scale
            if r_ref is not None:
                acc = r_ref[...] + acc
            o_ref[...] = acc.astype(o_ref.dtype)

        part = _dot(a_ref[...], b_ref[...], dims)
        if nk == 1:
            finish(part)
        else:
            acc_ref = o_ref if in_place else refs[-1]
            k = pl.program_id(2)

            @pl.when(k == 0)
            def _():
                acc_ref[...] = part

            @pl.when(k > 0)
            def _():
                acc_ref[...] += part

            if not in_place:
                @pl.when(k == nk - 1)
                def _():
                    finish(acc_ref[...])

    in_place = out_dtype == F32 and res is None and scale == 1.0
    scratch = [] if nk == 1 or in_place else [pltpu.VMEM((tm, tn), F32)]
    return pl.pallas_call(body, name=name, grid=grid, in_specs=in_specs, out_specs=o_spec, out_shape=out_shape,
                          scratch_shapes=scratch, compiler_params=_params())(*operands)


def _rms_fwd(x, gain, name, tm=256, dep=None):
    S, D = x.shape
    tm = _tile(S, tm)

    def body(x_ref, g_ref, *rest):
        o_ref = rest[-1]
        xv = x_ref[...]
        r = lax.rsqrt(jnp.mean(xv * xv, axis=-1, keepdims=True) + RMS_EPS)
        o_ref[...] = (xv * r * g_ref[...]).astype(o_ref.dtype)

    in_specs = [pl.BlockSpec((tm, D), lambda i: (i, 0)), pl.BlockSpec((1, D), lambda i: (0, 0))]
    operands = [x, gain]
    if dep is not None:
        in_specs.append(pl.BlockSpec(dep.shape, lambda i: (0, 0)))
        operands.append(dep)
    return pl.pallas_call(body, name=name, grid=(S // tm,), in_specs=in_specs,
                          out_specs=pl.BlockSpec((tm, D), lambda i: (i, 0)),
                          out_shape=jax.ShapeDtypeStruct((S, D), BF16), compiler_params=_params())(*operands)


def _rms_bwd(x, gain, dh, dres, name, lowp=None, tm=256):
    half = lowp is not None
    S, D = x.shape
    tm = _tile(S, tm)
    has_res = dres is not None

    def body(*refs):
        if has_res:
            x_ref, g_ref, dh_ref, dr_ref = refs[:4]
            outs = refs[4:]
        else:
            x_ref, g_ref, dh_ref = refs[:3]
            dr_ref = None
            outs = refs[3:]
        dx_ref, dg_ref = outs[0], outs[-1]
        xv = x_ref[...]
        dhv = dh_ref[...].astype(F32)
        r = lax.rsqrt(jnp.mean(xv * xv, axis=-1, keepdims=True) + RMS_EPS)
        gy = dhv * g_ref[...]
        dx = r * gy - xv * (r * r * r) * jnp.mean(gy * xv, axis=-1, keepdims=True)
        if dr_ref is not None:
            dx = dx + dr_ref[...]
        dx_ref[...] = dx
        if half:
            outs[1][...] = (dx if lowp == 1.0 else lowp * dx).astype(BF16)
        part = jnp.sum(dhv * xv * r, axis=0, keepdims=True)

        @pl.when(pl.program_id(0) == 0)
        def _():
            dg_ref[...] = part

        @pl.when(pl.program_id(0) > 0)
        def _():
            dg_ref[...] += part

    row = pl.BlockSpec((tm, D), lambda i: (i, 0))
    vec = pl.BlockSpec((1, D), lambda i: (0, 0))
    in_specs = [row, vec, row] + ([row] if has_res else [])
    operands = [x, gain, dh] + ([dres] if has_res else [])
    out_specs = [row] + ([row] if half else []) + [vec]
    out_shape = [jax.ShapeDtypeStruct((S, D), F32)] + ([jax.ShapeDtypeStruct((S, D), BF16)] if half else []) + [
        jax.ShapeDtypeStruct((1, D), F32)]
    return pl.pallas_call(body, name=name, grid=(S // tm,), in_specs=in_specs, out_specs=out_specs, out_shape=out_shape,
                          compiler_params=_params())(*operands)


def _loss_head(x, gain, target, name, tm=256):
    S, D = x.shape
    tm = _tile(S, tm)

    def body(x_ref, g_ref, t_ref, sq_ref, dx_ref, dxh_ref, dg_ref):
        xv = x_ref[...]
        r = lax.rsqrt(jnp.mean(xv * xv, axis=-1, keepdims=True) + RMS_EPS)
        xn = xv * r
        err = xn * g_ref[...] - t_ref[...]
        dout = err * (1.0 / D)
        gy = dout * g_ref[...]
        dx = r * gy - xv * (r * r * r) * jnp.mean(gy * xv, axis=-1, keepdims=True)
        dx_ref[...] = dx
        dxh_ref[...] = (0.5 * dx).astype(BF16)
        sq = jnp.sum(err * err, axis=0, keepdims=True)
        dg = jnp.sum(dout * xn, axis=0, keepdims=True)

        @pl.when(pl.program_id(0) == 0)
        def _():
            sq_ref[...] = sq
            dg_ref[...] = dg

        @pl.when(pl.program_id(0) > 0)
        def _():
            sq_ref[...] += sq
            dg_ref[...] += dg

    row = pl.BlockSpec((tm, D), lambda i: (i, 0))
    vec = pl.BlockSpec((1, D), lambda i: (0, 0))
    return pl.pallas_call(body, name=name, grid=(S // tm,), in_specs=[row, vec, row], out_specs=[vec, row, row, vec],
                          out_shape=[jax.ShapeDtypeStruct((1, D), F32), jax.ShapeDtypeStruct((S, D), F32),
                                     jax.ShapeDtypeStruct((S, D), BF16), jax.ShapeDtypeStruct((1, D), F32)],
                          compiler_params=_params())(x, gain, target)


def _cast(x, dtype, name, scale=1.0, tm=256):
    S, D = x.shape
    tm = _tile(S, tm)

    def body(x_ref, o_ref):
        o_ref[...] = (x_ref[...] * scale).astype(o_ref.dtype)

    row = pl.BlockSpec((tm, D), lambda i: (i, 0))
    return pl.pallas_call(body, name=name, grid=(S // tm,), in_specs=[row], out_specs=row,
                          out_shape=jax.ShapeDtypeStruct((S, D), dtype), compiler_params=_params())(x)


def _ffn_up(h, wg, wu, name, tm=512):
    S, D = h.shape
    G, _, Fj = wg.shape
    tm = _tile(S, tm)

    def body(h_ref, wg_ref, wu_ref, ga_ref, gb_ref, hid_ref):
        hv = h_ref[...]
        a = _nn(hv, wg_ref[...])
        b = _nn(hv, wu_ref[...])
        s = _sigmoid(a)
        silu = a * s
        ga_ref[...] = (b * (s * (1.0 + a * (1.0 - s)))).astype(BF16)
        gb_ref[...] = silu.astype(BF16)
        hid_ref[...] = (silu * b).astype(BF16)

    w_spec = pl.BlockSpec((None, D, Fj), lambda g, i: (g, 0, 0))
    o_spec = pl.BlockSpec((tm, Fj), lambda g, i: (i, g))
    out = jax.ShapeDtypeStruct((S, G * Fj), BF16)
    return pl.pallas_call(body, name=name, grid=(G, S // tm),
                          in_specs=[pl.BlockSpec((tm, D), lambda g, i: (i, 0)), w_spec, w_spec],
                          out_specs=[o_spec, o_spec, o_spec], out_shape=[out, out, out], compiler_params=_params())(h, wg, wu)


def _ffn_dact(dxh, wd, ga, gb, name, tm=512):
    S, D = dxh.shape
    G, Fj, _ = wd.shape
    tm = _tile(S, tm)

    def body(dx_ref, wd_ref, ga_ref, gb_ref, da_ref, db_ref):
        dhid = _nt(dx_ref[...], wd_ref[...])
        da_ref[...] = (dhid * ga_ref[...].astype(F32)).astype(BF16)
        db_ref[...] = (dhid * gb_ref[...].astype(F32)).astype(BF16)

    blk = pl.BlockSpec((tm, Fj), lambda g, i: (i, g))
    out = jax.ShapeDtypeStruct((S, G * Fj), BF16)
    return pl.pallas_call(body, name=name, grid=(G, S // tm),
                          in_specs=[pl.BlockSpec((tm, D), lambda g, i: (i, 0)),
                                    pl.BlockSpec((None, Fj, D), lambda g, i: (g, 0, 0)), blk, blk],
                          out_specs=[blk, blk], out_shape=[out, out], compiler_params=_params())(dxh, wd, ga, gb)


def _ffn_dh(da, db, wg, wu, name, dep=(), tm=512):
    S = da.shape[0]
    G, D, Fj = wg.shape
    tm = _tile(S, tm)

    def body(da_ref, db_ref, wg_ref, wu_ref, *rest):
        o_ref = rest[-1]
        part = _nt(da_ref[...], wg_ref[...]) + _nt(db_ref[...], wu_ref[...])

        @pl.when(pl.program_id(1) == 0)
        def _():
            o_ref[...] = part

        @pl.when(pl.program_id(1) > 0)
        def _():
            o_ref[...] += part

    act = pl.BlockSpec((tm, Fj), lambda i, g: (i, g))
    w_spec = pl.BlockSpec((None, D, Fj), lambda i, g: (g, 0, 0))
    in_specs = [act, act, w_spec, w_spec] + [pl.BlockSpec(d.shape, lambda i, g: (0, 0)) for d in dep]
    return pl.pallas_call(body, name=name, grid=(S // tm, G), in_specs=in_specs,
                          out_specs=pl.BlockSpec((tm, D), lambda i, g: (i, 0)),
                          out_shape=jax.ShapeDtypeStruct((S, D), F32), compiler_params=_params())(da, db, wg, wu, *dep)


def _pool_fwd(proj, pool_w, pool_scale, name, tm=512):
    S = proj.shape[0]
    NG, C, _ = pool_w.shape
    DP = NG * C
    tm = _tile(S, tm)
    hb = tm // POOL_HALO
    n_ext = tm + POOL_HALO

    def body(u_ref, halo_ref, w_ref, sc_ref, y_ref, d_ref):
        i = pl.program_id(0)
        t = lax.broadcasted_iota(jnp.int32, (tm, 1), 0) + i * tm
        for g, win in enumerate(POOL_WINDOWS):
            cols = slice(g * C, (g + 1) * C)
            ug = u_ref[:, cols]
            halo = jnp.where(i > 0, halo_ref[:, cols], 0.0)
            acc = jnp.concatenate([halo, ug], axis=0)
            step = 1
            while step < win:
                acc = acc + pltpu.roll(acc, step, 0)
                step *= 2
            count = jnp.minimum(t + 1, win).astype(F32)
            d = (acc[POOL_HALO:, :] / count - ug).astype(BF16)
            d_ref[:, cols] = d
            y_ref[:, cols] = (_nn(d, w_ref[g]) * sc_ref[:, cols]).astype(BF16)

    del n_ext
    return pl.pallas_call(
        body, name=name, grid=(S // tm,),
        in_specs=[pl.BlockSpec((tm, DP), lambda i: (i, 0)),
                  pl.BlockSpec((POOL_HALO, DP), lambda i: (jnp.maximum(i * hb - 1, 0), 0)),
                  pl.BlockSpec((NG, C, C), lambda i: (0, 0, 0)), pl.BlockSpec((1, DP), lambda i: (0, 0))],
        out_specs=[pl.BlockSpec((tm, DP), lambda i: (i, 0)), pl.BlockSpec((tm, DP), lambda i: (i, 0))],
        out_shape=[jax.ShapeDtypeStruct((S, DP), BF16), jax.ShapeDtypeStruct((S, DP), BF16)],
        compiler_params=_params())(proj, proj, pool_w, pool_scale)


def _pool_bwd(dymix, d, pool_w, pool_scale, name, tm=512):
    S = dymix.shape[0]
    NG, C, _ = pool_w.shape
    DP = NG * C
    tm = _tile(S, tm)
    hb = tm // POOL_HALO
    nb = S // tm
    n_ext = tm + POOL_HALO
    last_halo = S // POOL_HALO - 1

    def body(dy_ref, halo_ref, d_ref, w_ref, sc_ref, du_ref, dw_ref, dsc_ref):
        i = pl.program_id(0)
        t = lax.broadcasted_iota(jnp.int32, (n_ext, 1), 0) + i * tm
        for g, win in enumerate(POOL_WINDOWS):
            cols = slice(g * C, (g + 1) * C)
            dy = dy_ref[:, cols]
            halo = jnp.where(i < nb - 1, halo_ref[:, cols], 0.0)
            sc = sc_ref[:, cols]
            dv = d_ref[:, cols]
            e_ext = (jnp.concatenate([dy, halo], axis=0) * sc).astype(BF16)
            dd = _nt(e_ext, w_ref[g])
            count = jnp.minimum(t + 1, win).astype(F32)
            acc = dd / count
            step = 1
            while step < win:
                acc = acc + pltpu.roll(acc, n_ext - step, 0)
                step *= 2
            du_ref[:, cols] = (acc[:tm, :] - dd[:tm, :]).astype(BF16)
            dw = _tn(dv, e_ext[:tm, :])
            dsc = jnp.sum(dy * _nn(dv, w_ref[g]), axis=0, keepdims=True)

            @pl.when(i == 0)
            def _():
                dw_ref[g] = dw
                dsc_ref[:, cols] = dsc

            @pl.when(i > 0)
            def _():
                dw_ref[g] += dw
                dsc_ref[:, cols] += dsc

    return pl.pallas_call(
        body, name=name, grid=(nb,),
        in_specs=[pl.BlockSpec((tm, DP), lambda i: (i, 0)),
                  pl.BlockSpec((POOL_HALO, DP), lambda i: (jnp.minimum((i + 1) * hb, last_halo), 0)),
                  pl.BlockSpec((tm, DP), lambda i: (i, 0)),
                  pl.BlockSpec((NG, C, C), lambda i: (0, 0, 0)), pl.BlockSpec((1, DP), lambda i: (0, 0))],
        out_specs=[pl.BlockSpec((tm, DP), lambda i: (i, 0)), pl.BlockSpec((NG, C, C), lambda i: (0, 0, 0)),
                   pl.BlockSpec((1, DP), lambda i: (0, 0))],
        out_shape=[jax.ShapeDtypeStruct((S, DP), BF16), jax.ShapeDtypeStruct((NG, C, C), F32),
                   jax.ShapeDtypeStruct((1, DP), F32)],
        compiler_params=_params())(dymix, dymix, d, pool_w, pool_scale)


def _chunk_scan(v, rows, reverse):
    n = v.shape[0]
    step = 1
    while step < CHUNK:
        if reverse:
            v = v + jnp.where(rows < CHUNK - step, pltpu.roll(v, n - step, 0), 0.0)
        else:
            v = v + jnp.where(rows >= step, pltpu.roll(v, step, 0), 0.0)
        step *= 2
    return v


def _log_decay(alr, w_a2, b_a):
    z = _nn(alr.astype(BF16), w_a2) + b_a
    la = (jnp.minimum(z, 0.0) - jnp.log(1.0 + jnp.exp(-jnp.abs(z)))) * (1.0 / GATE_TEMP)
    return z, la


def _gla_specs(DP, DKT, DV, tb, bmap):
    return [pl.BlockSpec((tb, DKT), lambda i: (bmap(i), DP // DKT)),
            pl.BlockSpec((tb, DKT), lambda i: (bmap(i), DP // DKT + 1)),
            pl.BlockSpec((tb, DV), lambda i: (bmap(i), (DP + 2 * DKT) // DV)),
            pl.BlockSpec((tb, DV), lambda i: (bmap(i), (DP + 2 * DKT) // DV + 1)),
            pl.BlockSpec((tb, LANES), lambda i: (bmap(i), (DP + 2 * DKT + 2 * DV) // LANES))]


def _gla_fwd(proj, y_pool, w_a2, b_a, head_norm, name, tb=512):
    S = proj.shape[0]
    DP = y_pool.shape[1]
    DKT = b_a.shape[1]
    DV = head_norm.shape[1]
    dk, dv = DKT // N_HEADS, DV // N_HEADS
    tb = _tile(S, tb)
    ncb = tb // CHUNK
    qscale = dk ** -0.5

    def body(q_ref, k_ref, v_ref, g_ref, alr_ref, yp_ref, wa_ref, ba_ref, hn_ref, y_ref, st_out_ref, st_ref, kdec_ref,
             gam_ref):
        @pl.when(pl.program_id(0) == 0)
        def _():
            st_ref[...] = jnp.zeros_like(st_ref)

        y_ref[:, :DP] = yp_ref[...]

        rows = lax.broadcasted_iota(jnp.int32, (tb, 1), 0) % CHUNK
        _, la = _log_decay(alr_ref[...], wa_ref[...], ba_ref[...])
        tail = _chunk_scan(la, rows, True)
        kdec_ref[...] = k_ref[...] * jnp.exp(tail - la)
        gam_ref[...] = jnp.exp(tail)

        def chunk(c, carry):
            r0 = pl.multiple_of(c * CHUNK, CHUNK)
            rs = pl.ds(r0, CHUNK)
            gam = gam_ref[pl.ds(r0, 1), :]
            heads = range(N_HEADS)
            kcs = [slice(h * dk, (h + 1) * dk) for h in heads]
            vcs = [slice(h * dv, (h + 1) * dv) for h in heads]
            upd = [_tn(v_ref[rs, vcs[h]].astype(BF16), kdec_ref[rs, kcs[h]].astype(BF16)) for h in heads]
            st = [st_ref[h] * gam[:, kcs[h]] + upd[h] for h in heads]
            o = [_nt((q_ref[rs, kcs[h]] * qscale).astype(BF16), st[h].astype(BF16)) for h in heads]
            for h in heads:
                st_ref[h] = st[h]
                st_out_ref[c, h] = st[h]
                r = lax.rsqrt(jnp.mean(o[h] * o[h], axis=-1, keepdims=True) + RMS_EPS)
                gv = g_ref[rs, vcs[h]]
                y_ref[rs, DP + h * dv:DP + (h + 1) * dv] = (o[h] * r * hn_ref[:, vcs[h]] * (gv * _sigmoid(gv))).astype(BF16)
            return carry

        lax.fori_loop(0, ncb, chunk, 0, unroll=2)

    full = lambda shape: pl.BlockSpec(shape, lambda i: (0,) * len(shape))
    return pl.pallas_call(
        body, name=name, grid=(S // tb,),
        in_specs=_gla_specs(DP, DKT, DV, tb, lambda i: i) + [pl.BlockSpec((tb, DP), lambda i: (i, 0)),
                                                            full((LANES, DKT)), full((1, DKT)), full((1, DV))],
        out_specs=[pl.BlockSpec((tb, DP + DV), lambda i: (i, 0)),
                   pl.BlockSpec((ncb, N_HEADS, dv, dk), lambda i: (i, 0, 0, 0))],
        out_shape=[jax.ShapeDtypeStruct((S, DP + DV), BF16), jax.ShapeDtypeStruct((S // CHUNK, N_HEADS, dv, dk), F32)],
        scratch_shapes=[pltpu.VMEM((N_HEADS, dv, dk), F32), pltpu.VMEM((tb, DKT), F32), pltpu.VMEM((tb, DKT), F32)],
        compiler_params=_params())(proj, proj, proj, proj, proj, y_pool, w_a2, b_a, head_norm)


def _gla_bwd(proj, states, dymix, du, w_a2, b_a, head_norm, name, tb=512):
    S = proj.shape[0]
    DP = du.shape[1]
    DKT = b_a.shape[1]
    DV = head_norm.shape[1]
    dk, dv = DKT // N_HEADS, DV // N_HEADS
    tb = _tile(S, tb)
    ncb = tb // CHUNK
    nb = S // tb
    qscale = dk ** -0.5
    rev = lambda i: nb - 1 - i

    q0, k0, v0, g0, a0 = DP, DP + DKT, DP + 2 * DKT, DP + 2 * DKT + DV, DP + 2 * DKT + 2 * DV

    def body(q_ref, k_ref, v_ref, g_ref, alr_ref, st_blk_ref, st_prev_ref, dy_ref, du_ref, wa_ref, ba_ref, hn_ref,
             dp_ref, dwa_ref, dba_ref, dhn_ref,
             dst_ref, kdec_ref, dec_ref, gam_ref, e_ref, dla_ref, dhn_acc_ref):
        i = pl.program_id(0)
        blk = rev(i)
        dp_ref[:, :DP] = du_ref[...]

        @pl.when(i == 0)
        def _():
            dst_ref[...] = jnp.zeros_like(dst_ref)

        dhn_acc_ref[...] = jnp.zeros_like(dhn_acc_ref)
        rows = lax.broadcasted_iota(jnp.int32, (tb, 1), 0) % CHUNK
        z, la = _log_decay(alr_ref[...], wa_ref[...], ba_ref[...])
        tail = _chunk_scan(la, rows, True)
        dec_ref[...] = jnp.exp(tail - la)
        kdec_ref[...] = k_ref[...] * dec_ref[...]
        gam_ref[...] = jnp.exp(tail)

        def chunk(cc, carry):
            c = ncb - 1 - cc
            r0 = pl.multiple_of(c * CHUNK, CHUNK)
            rs = pl.ds(r0, CHUNK)
            gam = gam_ref[pl.ds(r0, 1), :]
            first = jnp.logical_and(blk == 0, c == 0)
            heads = range(N_HEADS)
            kcs = [slice(h * dk, (h + 1) * dk) for h in heads]
            vcs = [slice(h * dv, (h + 1) * dv) for h in heads]
            qs = [(q_ref[rs, kcs[h]] * qscale).astype(BF16) for h in heads]
            stb = [st_blk_ref[c, h].astype(BF16) for h in heads]
            o = [_nt(qs[h], stb[h]) for h in heads]
            do = []
            for h in heads:
                oh = o[h]
                r = lax.rsqrt(jnp.mean(oh * oh, axis=-1, keepdims=True) + RMS_EPS)
                gv = g_ref[rs, vcs[h]]
                sg = _sigmoid(gv)
                dy = dy_ref[rs, vcs[h]]
                hn = hn_ref[:, vcs[h]]
                on = oh * r
                dp_ref[rs, g0 + h * dv:g0 + (h + 1) * dv] = (dy * on * hn * (sg * (1.0 + gv * (1.0 - sg)))).astype(BF16)
                don = dy * (gv * sg)
                dhn_acc_ref[:, vcs[h]] += jnp.sum(don * on, axis=0, keepdims=True)
                dn = don * hn
                do.append((r * dn - oh * (r * r * r) * jnp.mean(dn * oh, axis=-1, keepdims=True)).astype(BF16))
            dqs = [_nn(do[h], stb[h]) for h in heads]
            dst = [dst_ref[h] + _tn(do[h], qs[h]) for h in heads]
            for h in heads:
                dp_ref[rs, q0 + h * dk:q0 + (h + 1) * dk] = (dqs[h] * qscale).astype(BF16)
            dstb = [dst[h].astype(BF16) for h in heads]
            dvh = [_nt(kdec_ref[rs, kcs[h]].astype(BF16), dstb[h]) for h in heads]
            dkdec = [_nn(v_ref[rs, vcs[h]].astype(BF16), dstb[h]) for h in heads]
            gdg = []
            for h in heads:
                dp_ref[rs, v0 + h * dv:v0 + (h + 1) * dv] = dvh[h].astype(BF16)
                dp_ref[rs, k0 + h * dk:k0 + (h + 1) * dk] = (dkdec[h] * dec_ref[rs, kcs[h]]).astype(BF16)
                e_ref[rs, kcs[h]] = dkdec[h] * kdec_ref[rs, kcs[h]]
                st_prev = jnp.where(c > 0, st_blk_ref[jnp.maximum(c - 1, 0), h], st_prev_ref[0, h])
                st_prev = jnp.where(first, 0.0, st_prev)
                gdg.append(jnp.sum(dst[h] * st_prev, axis=0, keepdims=True) * gam[:, kcs[h]])
                dst_ref[h] = dst[h] * gam[:, kcs[h]]
            dla_ref[rs, :] = jnp.broadcast_to(jnp.concatenate(gdg, axis=1), (CHUNK, DKT))
            return carry

        lax.fori_loop(0, ncb, chunk, 0, unroll=2)

        ev = e_ref[...]
        dla = dla_ref[...] + _chunk_scan(ev, rows, False) - ev
        dz = dla * (1.0 / GATE_TEMP) * (1.0 - _sigmoid(z))
        dzb = dz.astype(BF16)
        dp_ref[:, a0:a0 + LANES] = _nt(dzb, wa_ref[...]).astype(BF16)
        dwa = _tn(alr_ref[...].astype(BF16), dzb)
        dba = jnp.sum(dz, axis=0, keepdims=True)

        @pl.when(i == 0)
        def _():
            dwa_ref[...] = dwa
            dba_ref[...] = dba
            dhn_ref[...] = dhn_acc_ref[...]

        @pl.when(i > 0)
        def _():
            dwa_ref[...] += dwa
            dba_ref[...] += dba
            dhn_ref[...] += dhn_acc_ref[...]

    full = lambda shape: pl.BlockSpec(shape, lambda i: (0,) * len(shape))
    rowblk = lambda w: pl.BlockSpec((tb, w), lambda i: (rev(i), 0))
    return pl.pallas_call(
        body, name=name, grid=(nb,),
        in_specs=_gla_specs(DP, DKT, DV, tb, rev) + [
            pl.BlockSpec((ncb, N_HEADS, dv, dk), lambda i: (rev(i), 0, 0, 0)),
            pl.BlockSpec((1, N_HEADS, dv, dk), lambda i: (jnp.maximum(rev(i) * ncb - 1, 0), 0, 0, 0)),
            pl.BlockSpec((tb, DV), lambda i: (rev(i), DP // DV)), rowblk(DP),
            full((LANES, DKT)), full((1, DKT)), full((1, DV))],
        out_specs=[rowblk(a0 + LANES), full((LANES, DKT)), full((1, DKT)), full((1, DV))],
        out_shape=[jax.ShapeDtypeStruct((S, a0 + LANES), BF16), jax.ShapeDtypeStruct((LANES, DKT), F32),
                   jax.ShapeDtypeStruct((1, DKT), F32), jax.ShapeDtypeStruct((1, DV), F32)],
        scratch_shapes=[pltpu.VMEM((N_HEADS, dv, dk), F32)] + [pltpu.VMEM((tb, DKT), F32)] * 5 + [pltpu.VMEM((1, DV), F32)],
        compiler_params=_params())(proj, proj, proj, proj, proj, states, states, dymix, du, w_a2, b_a, head_norm)


def _xattn_fwd(q, kv, name, tm=512):
    S, D = q.shape
    M = kv.shape[0]
    hd = D // N_HEADS
    tm = _tile(S, tm)
    scale = hd ** -0.5

    def body(q_ref, k_ref, v_ref, o_ref):
        for h in range(N_HEADS):
            hc = slice(h * hd, (h + 1) * hd)
            s = _nt(q_ref[:, hc], k_ref[:, hc]) * scale
            p = jnp.exp(s - jnp.max(s, axis=-1, keepdims=True))
            p = p / jnp.sum(p, axis=-1, keepdims=True)
            o_ref[:, hc] = _nn(p.astype(BF16), v_ref[:, hc]).astype(BF16)

    return pl.pallas_call(body, name=name, grid=(S // tm,),
                          in_specs=[pl.BlockSpec((tm, D), lambda i: (i, 0)), pl.BlockSpec((M, D), lambda i: (0, 0)),
                                    pl.BlockSpec((M, D), lambda i: (0, 1))],
                          out_specs=pl.BlockSpec((tm, D), lambda i: (i, 0)),
                          out_shape=jax.ShapeDtypeStruct((S, D), BF16), compiler_params=_params())(q, kv, kv)


def _xattn_bwd(q, kv, do, name, tm=512):
    S, D = q.shape
    M = kv.shape[0]
    hd = D // N_HEADS
    tm = _tile(S, tm)
    scale = hd ** -0.5

    def body(q_ref, k_ref, v_ref, do_ref, dq_ref, dkv_ref):
        first = pl.program_id(0) == 0
        for h in range(N_HEADS):
            hc = slice(h * hd, (h + 1) * hd)
            vcols = slice(D + h * hd, D + (h + 1) * hd)
            qh = q_ref[:, hc]
            kh = k_ref[:, hc]
            doh = do_ref[:, hc]
            s = _nt(qh, kh) * scale
            p = jnp.exp(s - jnp.max(s, axis=-1, keepdims=True))
            p = p / jnp.sum(p, axis=-1, keepdims=True)
            dvh = _tn(p.astype(BF16), doh)
            dp = _nt(doh, v_ref[:, hc])
            ds = ((p * (dp - jnp.sum(dp * p, axis=-1, keepdims=True))) * scale).astype(BF16)
            dq_ref[:, hc] = _nn(ds, kh).astype(BF16)
            dkh = _tn(ds, qh)

            @pl.when(first)
            def _():
                dkv_ref[:, hc] = dkh
                dkv_ref[:, vcols] = dvh

            @pl.when(jnp.logical_not(first))
            def _():
                dkv_ref[:, hc] += dkh
                dkv_ref[:, vcols] += dvh

    row = pl.BlockSpec((tm, D), lambda i: (i, 0))
    return pl.pallas_call(body, name=name, grid=(S // tm,),
                          in_specs=[row, pl.BlockSpec((M, D), lambda i: (0, 0)), pl.BlockSpec((M, D), lambda i: (0, 1)), row],
                          out_specs=[row, pl.BlockSpec((M, 2 * D), lambda i: (0, 0))],
                          out_shape=[jax.ShapeDtypeStruct((S, D), BF16), jax.ShapeDtypeStruct((M, 2 * D), F32)],
                          compiler_params=_params())(q, kv, kv, do)


def _local_step(x, mem, target, vec, weight, emit, dep0):
    DP = vec["pool_scale"].shape[1]
    g = {}
    pending = []
    begun = []
    emit_begin, emit_finish = emit

    def behind(fn, *a, **kw):
        dep = tuple(pending)
        pending.clear()
        out = fn(*a, dep=dep, **kw)
        while begun:
            pending.append(emit_finish(begun.pop(0), out))
        return out

    def mm(a, b, **kw):
        return behind(_matmul, a, b, **kw)

    def send(name, gfull):
        pending.append(emit_begin(name, gfull))
        begun.append(name)

    def ffn_fwd(xin, tag, dep):
        h = _rms_fwd(xin, vec[f"{tag}_norm"], f"{tag}_norm", dep=dep)
        ga, gb, hid = _ffn_up(h, weight(f"{tag}_w_gate", h), weight(f"{tag}_w_up", h), f"{tag}_up")
        wd = weight(f"{tag}_w_down", hid)
        G, Fj, D = wd.shape
        xo = _matmul(hid, wd.reshape(G * Fj, D), mode="nn", name=f"{tag}_down", out_dtype=F32, res=xin, scale=0.5,
                     tn=1024, tk=G * Fj)
        return xo, (h, ga, gb, hid)

    def ffn_bwd(dxh, saved, tag, last):
        h, ga, gb, hid = saved
        wg, wu, wd = weight(f"{tag}_w_gate"), weight(f"{tag}_w_up"), weight(f"{tag}_w_down")
        G, Fj, D = wd.shape
        send(f"{tag}_w_down", mm(hid, dxh, mode="tn", name=f"{tag}_dwd", out_dtype=F32, tm=Fj, tn=1024))
        da, db = _ffn_dact(dxh, wd, ga, gb, f"{tag}_dact")
        send(f"{tag}_w_gate", mm(h, da, mode="tn", name=f"{tag}_dwg", out_dtype=F32, tm=1024, tn=Fj, out_groups=G))
        dwu = mm(h, db, mode="tn", name=f"{tag}_dwu", out_dtype=F32, tm=1024, tn=Fj, out_groups=G)
        if last:
            pending.append(emit_begin(f"{tag}_w_up", dwu))
            pending.append(emit_finish(f"{tag}_w_up", pending[-1]))
        else:
            send(f"{tag}_w_up", dwu)
        return behind(_ffn_dh, da, db, wg, wu, f"{tag}_dh")

    x1, ffn1_saved = ffn_fwd(x, "ffn1", dep0)
    h2 = _rms_fwd(x1, vec["mix_norm"], "mix_norm")
    w_in = weight("w_in", h2)
    proj = _matmul(h2, w_in, mode="nn", name="w_in", out_dtype=F32, tn=1408)
    pool_w, w_a2 = weight("pool_w", h2), weight("gla_w_a2", h2)
    y_pool, dpool = _pool_fwd(proj, pool_w, vec["pool_scale"], "pool_fwd")
    ymix, states = _gla_fwd(proj, y_pool, w_a2, vec["gla_b_a"], vec["gla_head_norm"], "gla_fwd")
    w_out = weight("w_out", ymix)
    x2 = _matmul(ymix, w_out, mode="nn", name="w_out", out_dtype=F32, res=x1)
    h3 = _rms_fwd(x2, vec["xattn_norm"], "xattn_norm")
    mh = _rms_fwd(mem, vec["mem_norm"], "mem_norm")
    w_q = weight("xattn_w_q", h3)
    q = _matmul(h3, w_q, mode="nn", name="xattn_q", out_dtype=BF16)
    w_kv = weight("xattn_w_kv", q)
    kv = _matmul(mh, w_kv, mode="nn", name="xattn_kv", out_dtype=BF16, b_groups=True, tn=1024)
    o = _xattn_fwd(q, kv, "xattn_fwd")
    w_o = weight("xattn_w_o", o)
    x3 = _matmul(o, w_o, mode="nn", name="xattn_o", out_dtype=F32, res=x2)
    x4, ffn2_saved = ffn_fwd(x3, "ffn2", None)
    sq, dx4, dx4h, g["final_norm"] = _loss_head(x4, vec["final_norm"], target, "loss_head")

    dh = ffn_bwd(dx4h, ffn2_saved, "ffn2", False)
    dx3, dx3b, g["ffn2_norm"] = _rms_bwd(x3, vec["ffn2_norm"], dh, dx4, "ffn2_norm_bwd", lowp=1.0)
    send("xattn_w_o", mm(o, dx3b, mode="tn", name="xattn_dwo", out_dtype=F32, tm=1024, tn=1024))
    do = mm(dx3b, w_o, mode="nt", name="xattn_do", out_dtype=BF16)
    dq, dkv = _xattn_bwd(q, kv, do, "xattn_bwd")
    send("xattn_w_q", mm(h3, dq, mode="tn", name="xattn_dwq", out_dtype=F32, tm=1024, tn=1024))
    dh3 = mm(dq, w_q, mode="nt", name="xattn_dh", out_dtype=F32)
    dkvb = _cast(dkv, BF16, "dkv_cast")
    send("xattn_w_kv", mm(mh, dkvb, mode="tn", name="xattn_dwkv", out_dtype=F32, tm=1024, tn=1024, out_groups=N_SHARDS))
    dmh = mm(dkvb, w_kv, mode="nt", name="xattn_dmh", out_dtype=F32, b_groups=True, tk=1024)
    _, g["mem_norm"] = _rms_bwd(mem, vec["mem_norm"], dmh, None, "mem_norm_bwd")
    pending.append(g["mem_norm"])
    dx2, dx2b, g["xattn_norm"] = _rms_bwd(x2, vec["xattn_norm"], dh3, dx3, "xattn_norm_bwd", lowp=1.0)
    send("w_out", mm(ymix, dx2b, mode="tn", name="dw_out", out_dtype=F32, tm=1024, tn=1024))
    dymix = mm(dx2b, w_out, mode="nt", name="dymix", out_dtype=F32)
    du, dpool_w, g["pool_scale"] = _pool_bwd(dymix, dpool, pool_w, vec["pool_scale"], "pool_bwd")
    send("pool_w", dpool_w)
    dproj, dw_a2, g["gla_b_a"], g["gla_head_norm"] = _gla_bwd(
        proj, states, dymix, du, w_a2, vec["gla_b_a"], vec["gla_head_norm"], "gla_bwd")
    send("gla_w_a2", dw_a2)
    send("w_in", mm(h2, dproj, mode="tn", name="dw_in", out_dtype=F32, tm=1024, tn=1408))
    dh2 = mm(dproj, w_in, mode="nt", name="dh2", out_dtype=F32, tn=1024, tk=dproj.shape[1])
    dx1, dx1h, g["mix_norm"] = _rms_bwd(x1, vec["mix_norm"], dh2, dx2, "mix_norm_bwd", lowp=0.5)
    dh = ffn_bwd(dx1h, ffn1_saved, "ffn1", True)
    dx0, g["ffn1_norm"] = _rms_bwd(x, vec["ffn1_norm"], dh, dx1, "ffn1_norm_bwd")
    return sq, dx0, g


def _place():
    x, y, c = lax.axis_index("x"), lax.axis_index("y"), lax.axis_index("c")
    chips = [(1 - x, y), (x, 1 - y), (1 - x, 1 - y)]
    return x, y, c, chips


def _ids():
    return jnp.stack([2 * lax.axis_index("x") + lax.axis_index("y"), lax.axis_index("c")]).astype(jnp.int32)


def _hbm(a):
    return pltpu.with_memory_space_constraint(a, pltpu.HBM)


def _cast_to_slot(w2d, dtype, name, dep=None):
    R, C = w2d.shape
    tr = _tile(R, max(16, (4 << 20) // (4 * C) // 16 * 16))

    def body(i_ref, w_ref, *rest):
        rest[-1][...] = w_ref[...].astype(dtype)

    in_specs = [pl.BlockSpec((tr, C), lambda r, i: (r, 0))]
    operands = [w2d]
    if dep is not None:
        in_specs.append(pl.BlockSpec(dep.shape, lambda r, i: (0, 0)))
        operands.append(dep)
    grid_spec = pltpu.PrefetchScalarGridSpec(num_scalar_prefetch=1, grid=(R // tr,), in_specs=in_specs,
                                             out_specs=pl.BlockSpec((None, tr, C), lambda r, i: (i[0], r, 0)))
    return pl.pallas_call(body, name=name, grid_spec=grid_spec, out_shape=jax.ShapeDtypeStruct((N_SHARDS, R, C), dtype),
                          compiler_params=_params())(_ids(), *operands)


def _gather_copies(buf_ref, send_sems, recv_sems, incoming, whole):
    x, y, c, chips = _place()
    hr = buf_ref.shape[1] // 2
    copies = []
    for j, (px, py) in enumerate(chips):
        slot = 2 * px + py if incoming else 2 * x + y
        part = buf_ref.at[slot] if whole else buf_ref.at[slot, pl.ds(c * hr, hr), :]
        copies.append(pltpu.make_async_remote_copy(src_ref=part, dst_ref=part, send_sem=send_sems.at[j],
                                                   recv_sem=recv_sems.at[j], device_id=(px, py, c), device_id_type=MESH))
    return copies


def _gather_start(buf, name, whole):
    def body(b_ref, send_sems, recv_sems, b_thru, token):
        for cp in _gather_copies(b_ref, send_sems, recv_sems, False, whole):
            cp.start()
        token[...] = jnp.zeros_like(token)

    return pl.pallas_call(
        body, name=name,
        out_shape=(pltpu.SemaphoreType.DMA((3,)), pltpu.SemaphoreType.DMA((3,)), pltpu.HBM(buf.shape, buf.dtype),
                   jax.ShapeDtypeStruct((8, LANES), F32)),
        in_specs=(HBM,), out_specs=(SEM, SEM, HBM, pl.BlockSpec(memory_space=pltpu.VMEM)), input_output_aliases={0: 2},
        compiler_params=pltpu.CompilerParams(has_side_effects=EFFECT))(_hbm(buf))


def _gather_wait(send_sems, recv_sems, buf_thru, after, name, whole):
    def body(b_ref, send_sems, recv_sems, after_ref, b_out):
        for cp in _gather_copies(b_ref, send_sems, recv_sems, False, whole):
            cp.wait_send()
        for cp in _gather_copies(b_ref, send_sems, recv_sems, True, whole):
            cp.wait_recv()

    return pl.pallas_call(
        body, name=name, out_shape=pltpu.HBM(buf_thru.shape, buf_thru.dtype),
        in_specs=(HBM, SEM, SEM, ANY), out_specs=HBM, input_output_aliases={0: 0},
        compiler_params=pltpu.CompilerParams(has_side_effects=EFFECT))(buf_thru, send_sems, recv_sems, after)


def _gather_forward(buf, name):
    G, R, C = buf.shape
    hr = R // 2

    def body(b_ref, o_ref, send_sems, recv_sems):
        x, y, c, chips = _place()
        copies = []
        for j, (px, py) in enumerate(chips):
            half = o_ref.at[2 * px + py, pl.ds(c * hr, hr), :]
            copies.append(pltpu.make_async_remote_copy(src_ref=half, dst_ref=half, send_sem=send_sems.at[j],
                                                       recv_sem=recv_sems.at[j], device_id=(x, y, 1 - c),
                                                       device_id_type=MESH))
        for cp in copies:
            cp.start()
        for j, (px, py) in enumerate(chips):
            half = o_ref.at[2 * px + py, pl.ds((1 - c) * hr, hr), :]
            pltpu.make_async_remote_copy(src_ref=half, dst_ref=half, send_sem=send_sems.at[j], recv_sem=recv_sems.at[j],
                                         device_id=(x, y, 1 - c), device_id_type=MESH).wait_recv()
        for cp in copies:
            cp.wait_send()

    return pl.pallas_call(body, name=name, in_specs=[ANY], out_specs=ANY, out_shape=jax.ShapeDtypeStruct(buf.shape, buf.dtype),
                          input_output_aliases={0: 0},
                          scratch_shapes=[pltpu.SemaphoreType.DMA((3,)), pltpu.SemaphoreType.DMA((3,))])(buf)


def _pair_copy(g_ref, land_ref, send_sem, recv_sem):
    x, y, c, _ = _place()
    hr = g_ref.shape[1] // 2
    return pltpu.make_async_remote_copy(src_ref=g_ref.at[:, pl.ds((1 - c) * hr, hr), :], dst_ref=land_ref,
                                        send_sem=send_sem, recv_sem=recv_sem, device_id=(x, y, 1 - c), device_id_type=MESH)


def _pair_start(gfull, name):
    G, R, C = gfull.shape

    def body(g_ref, land_ref, send_sem, recv_sem, g_thru, land_thru, token):
        _pair_copy(g_ref, land_ref, send_sem, recv_sem).start()
        token[...] = jnp.zeros_like(token)

    return pl.pallas_call(
        body, name=name,
        out_shape=(pltpu.SemaphoreType.DMA(()), pltpu.SemaphoreType.DMA(()), pltpu.HBM(gfull.shape, F32),
                   pltpu.HBM((G, R // 2, C), F32), jax.ShapeDtypeStruct((8, LANES), F32)),
        in_specs=(HBM, HBM), out_specs=(SEM, SEM, HBM, HBM, pl.BlockSpec(memory_space=pltpu.VMEM)),
        input_output_aliases={0: 2, 1: 3},
        compiler_params=pltpu.CompilerParams(has_side_effects=EFFECT))(_hbm(gfull), _hbm(lax.empty((G, R // 2, C), F32)))


def _pair_wait(send_sem, recv_sem, g_thru, land_thru, after, name):
    def body(g_ref, land_ref, send_sem, recv_sem, after_ref, g_out, land_out):
        cp = _pair_copy(g_ref, land_ref, send_sem, recv_sem)
        cp.wait_send()
        cp.wait_recv()

    return pl.pallas_call(
        body, name=name, out_shape=(pltpu.HBM(g_thru.shape, F32), pltpu.HBM(land_thru.shape, F32)),
        in_specs=(HBM, HBM, SEM, SEM, ANY), out_specs=(HBM, HBM), input_output_aliases={0: 0, 1: 1},
        compiler_params=pltpu.CompilerParams(has_side_effects=EFFECT))(g_thru, land_thru, send_sem, recv_sem, after)


def _pair_add(gfull, other, name):
    G, R, C = gfull.shape
    hr = R // 2
    tr = _tile(hr, max(8, (2 * 1024 * 1024) // (4 * C) // 8 * 8))
    nr = hr // tr
    c = lax.axis_index("c")
    cidx = jnp.reshape(c, (1,)).astype(jnp.int32)

    def body(c_ref, a_ref, b_ref, o_ref):
        o_ref[...] = a_ref[...] + b_ref[...]

    grid_spec = pltpu.PrefetchScalarGridSpec(
        num_scalar_prefetch=1, grid=(G, nr),
        in_specs=[pl.BlockSpec((None, tr, C), lambda g, r, cr: (g, cr[0] * nr + r, 0)),
                  pl.BlockSpec((None, tr, C), lambda g, r, cr: (g, r, 0))],
        out_specs=pl.BlockSpec((None, tr, C), lambda g, r, cr: (g, r, 0)))
    return pl.pallas_call(body, name=name, grid_spec=grid_spec, out_shape=jax.ShapeDtypeStruct((G, hr, C), F32),
                          compiler_params=_params())(cidx, gfull, other)


def _chip_copies(p_ref, land_ref, send_sems, recv_sems, incoming):
    x, y, c, chips = _place()
    me = 2 * x + y
    copies = []
    for j, (px, py) in enumerate(chips):
        dst = land_ref.at[2 * px + py] if incoming else land_ref.at[me]
        copies.append(pltpu.make_async_remote_copy(src_ref=p_ref.at[2 * px + py], dst_ref=dst, send_sem=send_sems.at[j],
                                                   recv_sem=recv_sems.at[j], device_id=(px, py, c), device_id_type=MESH))
    return copies


def _chip_start(part, name):
    def body(p_ref, land_ref, send_sems, recv_sems, p_thru, land_thru, token):
        for cp in _chip_copies(p_ref, land_ref, send_sems, recv_sems, False):
            cp.start()
        token[...] = jnp.zeros_like(token)

    return pl.pallas_call(
        body, name=name,
        out_shape=(pltpu.SemaphoreType.DMA((3,)), pltpu.SemaphoreType.DMA((3,)), pltpu.HBM(part.shape, F32),
                   pltpu.HBM(part.shape, F32), jax.ShapeDtypeStruct((8, LANES), F32)),
        in_specs=(HBM, HBM), out_specs=(SEM, SEM, HBM, HBM, pl.BlockSpec(memory_space=pltpu.VMEM)),
        input_output_aliases={0: 2, 1: 3},
        compiler_params=pltpu.CompilerParams(has_side_effects=EFFECT))(_hbm(part), _hbm(lax.empty(part.shape, F32)))


def _chip_wait(send_sems, recv_sems, p_thru, land_thru, after, name):
    def body(p_ref, land_ref, send_sems, recv_sems, after_ref, p_out, land_out):
        for cp in _chip_copies(p_ref, land_ref, send_sems, recv_sems, False):
            cp.wait_send()
        for cp in _chip_copies(p_ref, land_ref, send_sems, recv_sems, True):
            cp.wait_recv()

    return pl.pallas_call(
        body, name=name, out_shape=(pltpu.HBM(p_thru.shape, F32), pltpu.HBM(p_thru.shape, F32)),
        in_specs=(HBM, HBM, SEM, SEM, ANY), out_specs=(HBM, HBM), input_output_aliases={0: 0, 1: 1},
        compiler_params=pltpu.CompilerParams(has_side_effects=EFFECT))(p_thru, land_thru, send_sems, recv_sems, after)


def _chip_sum(part, slots, name):
    G, R2, C = part.shape
    tr = _tile(R2, max(8, (1 << 20) // (4 * C) // 8 * 8))
    nr = R2 // tr

    def body(i_ref, p_ref, *rest):
        o_ref = rest[-1]
        acc = None
        for u in range(G):
            val = jnp.where(i_ref[0] == u, p_ref[...], rest[u][...])
            acc = val if acc is None else acc + val
        o_ref[...] = acc

    def slot_spec(u):
        return pl.BlockSpec((None, tr, C), lambda r, i: (jnp.where(i[0] == u, (u + 1) % G, u), r, 0))

    grid_spec = pltpu.PrefetchScalarGridSpec(
        num_scalar_prefetch=1, grid=(nr,),
        in_specs=[pl.BlockSpec((None, tr, C), lambda r, i: (i[0], r, 0))] + [slot_spec(u) for u in range(G)],
        out_specs=pl.BlockSpec((tr, C), lambda r, i: (i[1] * nr + r, 0)))
    return pl.pallas_call(body, name=name, grid_spec=grid_spec, out_shape=jax.ShapeDtypeStruct((2 * R2, C), F32),
                          compiler_params=_params())(_ids(), part, slots, slots, slots, slots)


def _sum_slots(slots, name):
    G, R2, C = slots.shape
    tr = _tile(R2, max(8, (1024 * 1024) // (4 * C) // 8 * 8))

    def body(s_ref, o_ref):
        acc = s_ref[0]
        for u in range(1, G):
            acc = acc + s_ref[u]
        o_ref[...] = acc

    return pl.pallas_call(body, name=name, grid=(R2 // tr,), in_specs=[pl.BlockSpec((G, tr, C), lambda r: (0, r, 0))],
                          out_specs=pl.BlockSpec((tr, C), lambda r: (r, 0)), out_shape=jax.ShapeDtypeStruct((R2, C), F32),
                          compiler_params=_params())(slots)


def _pair_join(full, name):
    R, C = full.shape
    R2 = R // 2

    def body(f_ref, o_ref, send_sem, recv_sem):
        x, y, c, _ = _place()
        mine = o_ref.at[pl.ds(c * R2, R2), :]
        theirs = o_ref.at[pl.ds((1 - c) * R2, R2), :]
        cp = pltpu.make_async_remote_copy(src_ref=mine, dst_ref=mine, send_sem=send_sem, recv_sem=recv_sem,
                                          device_id=(x, y, 1 - c), device_id_type=MESH)
        cp.start()
        pltpu.make_async_remote_copy(src_ref=theirs, dst_ref=theirs, send_sem=send_sem, recv_sem=recv_sem,
                                     device_id=(x, y, 1 - c), device_id_type=MESH).wait_recv()
        cp.wait_send()

    return pl.pallas_call(body, name=name, in_specs=[ANY], out_specs=ANY, out_shape=jax.ShapeDtypeStruct((R, C), F32),
                          input_output_aliases={0: 0},
                          scratch_shapes=[pltpu.SemaphoreType.DMA, pltpu.SemaphoreType.DMA])(full)


def _all_reduce_small(v, name):
    R, C = v.shape

    def gather_body(v_ref, out_ref, send_sems, recv_sems, local_sem):
        x, y, c, _ = _place()
        me = 4 * x + 2 * y + c
        mine = pltpu.make_async_copy(v_ref, out_ref.at[me], local_sem)
        mine.start()
        flips = [(fx, fy, fc) for fx in (0, 1) for fy in (0, 1) for fc in (0, 1)][1:]
        copies = []
        for j, (fx, fy, fc) in enumerate(flips):
            peer = (x ^ fx, y ^ fy, c ^ fc)
            copies.append(pltpu.make_async_remote_copy(src_ref=v_ref, dst_ref=out_ref.at[me], send_sem=send_sems.at[j],
                                                       recv_sem=recv_sems.at[j], device_id=peer, device_id_type=MESH))
        for cp in copies:
            cp.start()
        for j, (fx, fy, fc) in enumerate(flips):
            peer = (x ^ fx, y ^ fy, c ^ fc)
            pltpu.make_async_remote_copy(src_ref=v_ref, dst_ref=out_ref.at[4 * peer[0] + 2 * peer[1] + peer[2]],
                                         send_sem=send_sems.at[j], recv_sem=recv_sems.at[j], device_id=peer,
                                         device_id_type=MESH).wait_recv()
        for cp in copies:
            cp.wait_send()
        mine.wait()

    slots = pl.pallas_call(gather_body, name=name, in_specs=[ANY], out_specs=ANY,
                           out_shape=jax.ShapeDtypeStruct((8, R, C), F32),
                           scratch_shapes=[pltpu.SemaphoreType.DMA((7,)), pltpu.SemaphoreType.DMA((7,)),
                                           pltpu.SemaphoreType.DMA])(v)
    return _sum_slots(slots, f"{name}_sum")


def _adamw(w, g, m, v, name):
    R, C = w.shape
    tr = _tile(R, max(8, (2 << 20) // (4 * C) // 8 * 8))
    bc1 = 1.0 - ADAM_B1 ** ADAM_STEP
    bc2 = 1.0 - ADAM_B2 ** ADAM_STEP

    def body(w_ref, g_ref, m_ref, v_ref, go_ref, d_ref, nm_ref, nv_ref):
        gv = g_ref[...]
        go_ref[...] = gv
        nm = ADAM_B1 * m_ref[...] + (1.0 - ADAM_B1) * gv
        nv = ADAM_B2 * v_ref[...] + (1.0 - ADAM_B2) * (gv * gv)
        nm_ref[...] = nm
        nv_ref[...] = nv
        d_ref[...] = -ADAM_LR * ((nm / bc1) / (jnp.sqrt(nv / bc2) + ADAM_EPS) + ADAM_WD * w_ref[...])

    blk = pl.BlockSpec((tr, C), lambda r: (r, 0))
    out = jax.ShapeDtypeStruct((R, C), F32)
    return pl.pallas_call(body, name=name, grid=(R // tr,), in_specs=[blk] * 4, out_specs=[blk] * 4, out_shape=[out] * 4,
                          compiler_params=_params())(w, g, m, v)


WEIGHTS = ['ffn1_norm', 'ffn1_w_gate', 'ffn1_w_up', 'ffn1_w_down', 'mix_norm', 'w_in', 'pool_w', 'pool_scale', 'gla_w_a2',
           'gla_b_a', 'gla_head_norm', 'w_out', 'xattn_norm', 'mem_norm', 'xattn_w_q', 'xattn_w_kv', 'xattn_w_o', 'ffn2_norm',
           'ffn2_w_gate', 'ffn2_w_up', 'ffn2_w_down', 'final_norm']
SHARDED = ['ffn1_w_gate', 'ffn1_w_up', 'ffn1_w_down', 'w_in', 'pool_w', 'gla_w_a2', 'w_out', 'xattn_w_q', 'xattn_w_kv',
           'xattn_w_o', 'ffn2_w_gate', 'ffn2_w_up', 'ffn2_w_down']
REPLICATED = [n for n in WEIGHTS if n not in SHARDED]
SMALL_COLS = 512


def _as2d(a):
    return a.reshape(-1, a.shape[-1])


def _finish_weight(name, gathered, wl):
    G, R, C = gathered.shape
    rank = wl["gla_w_a2"].shape[1]
    if name in ("w_out", "xattn_w_q", "xattn_w_o"):
        return gathered.reshape(G * R, C)
    if name == "w_in":
        w_in = jnp.transpose(gathered, (1, 0, 2)).reshape(R, G * C)
        main = G * C - rank
        return jnp.concatenate([w_in[:, :main], jnp.pad(w_in[:, main:], ((0, 0), (0, LANES - rank)))], axis=1)
    if name == "pool_w":
        NG, CJ, _ = wl[name].shape[1:]
        return jnp.transpose(gathered.reshape(G, NG, CJ, C), (1, 0, 2, 3)).reshape(NG, G * CJ, C)
    if name == "gla_w_a2":
        a2 = jnp.transpose(gathered, (1, 0, 2)).reshape(rank, G * C)
        return jnp.pad(a2, ((0, LANES - rank), (0, 0))).astype(BF16)
    return gathered


def _start_gathers(wl):
    started = {}
    token = None
    for n in SHARDED:
        whole = not n.startswith("ffn1")
        buf = _cast_to_slot(_as2d(wl[n]), BF16, f"slot_{n}", dep=token)
        send_sems, recv_sems, thru, token = _gather_start(buf, f"gather_start_{n}", whole)
        started[n] = (send_sems, recv_sems, thru, whole)
    cache = {}

    def weight(n, after=None):
        if n not in cache:
            *handles, whole = started[n]
            buf = _gather_wait(*handles, after, f"gather_wait_{n}", whole)
            if not whole:
                buf = _gather_forward(buf, f"gather_forward_{n}")
            cache[n] = _finish_weight(n, buf, wl)
        return cache[n]

    return weight, token


def _shard_major(name, gfull, wl):
    R, C = _as2d(wl[name]).shape
    if name in ("ffn1_w_gate", "ffn1_w_up", "ffn2_w_gate", "ffn2_w_up", "xattn_w_kv"):
        return gfull
    if name in ("ffn1_w_down", "ffn2_w_down", "w_out", "xattn_w_q", "xattn_w_o"):
        return gfull.reshape(N_SHARDS, R, C)
    if name == "w_in":
        return jnp.transpose(gfull[:, :N_SHARDS * C].reshape(R, N_SHARDS, C), (1, 0, 2))
    if name == "pool_w":
        NG, CJ, _ = wl[name].shape[1:]
        return jnp.transpose(gfull.reshape(NG, N_SHARDS, CJ, C), (1, 0, 2, 3)).reshape(N_SHARDS, R, C)
    assert name == "gla_w_a2"
    return jnp.transpose(gfull[:R].reshape(R, N_SHARDS, C), (1, 0, 2))


def kernel(x, mem, ffn1_norm, ffn1_w_gate, ffn1_w_up, ffn1_w_down, mix_norm, w_in, pool_w, pool_scale, gla_w_a2, gla_b_a, gla_head_norm, w_out, xattn_norm, mem_norm, xattn_w_q, xattn_w_kv, xattn_w_o, ffn2_norm, ffn2_w_gate, ffn2_w_up, ffn2_w_down, final_norm, loss_target, m_ffn1_norm, m_ffn1_w_gate, m_ffn1_w_up, m_ffn1_w_down, m_mix_norm, m_w_in, m_pool_w, m_pool_scale, m_gla_w_a2, m_gla_b_a, m_gla_head_norm, m_w_out, m_xattn_norm, m_mem_norm, m_xattn_w_q, m_xattn_w_kv, m_xattn_w_o, m_ffn2_norm, m_ffn2_w_gate, m_ffn2_w_up, m_ffn2_w_down, m_final_norm, v_ffn1_norm, v_ffn1_w_gate, v_ffn1_w_up, v_ffn1_w_down, v_mix_norm, v_w_in, v_pool_w, v_pool_scale, v_gla_w_a2, v_gla_b_a, v_gla_head_norm, v_w_out, v_xattn_norm, v_mem_norm, v_xattn_w_q, v_xattn_w_kv, v_xattn_w_o, v_ffn2_norm, v_ffn2_w_gate, v_ffn2_w_up, v_ffn2_w_down, v_final_norm):
    given = dict(locals())
    wl = {n: given[n] for n in WEIGHTS}
    ml = {n: given["m_" + n] for n in WEIGHTS}
    vl = {n: given["v_" + n] for n in WEIGHTS}

    vec = {n: wl[n].reshape(1, -1) for n in REPLICATED}
    weight, dep0 = _start_gathers(wl)
    in_flight = {}

    pair_flight = {}

    def emit_begin(n, gfull):
        *pair_flight[n], token = _pair_start(_shard_major(n, gfull, wl), f"{n}_pair_start")
        return token

    def emit_finish(n, after):
        gsm, other = _pair_wait(*pair_flight.pop(n), after, f"{n}_pair_wait")
        *in_flight[n], token = _chip_start(_pair_add(gsm, other, f"{n}_pair_add"), f"{n}_chip_start")
        return token

    sq, dx0, g = _local_step(x[0], mem[0], loss_target[0], vec, weight, (emit_begin, emit_finish), dep0)
    loss = lax.psum(0.5 * jnp.sum(sq) / x.shape[-1], ("x", "y", "c"))

    grads = {}
    for n in in_flight:
        part, slots = _chip_wait(*in_flight[n], dx0, f"{n}_chip_wait")
        grads[n] = _pair_join(_chip_sum(part, slots, f"{n}_chip_sum"), f"{n}_pair_join")
    widths = [wl[n].size for n in REPLICATED]
    total = sum(widths)
    rows = -(-total // SMALL_COLS)
    rows = -(-rows // 8) * 8
    packed = jnp.concatenate([g[n].reshape(-1) for n in REPLICATED] + [jnp.zeros((rows * SMALL_COLS - total,), F32)])
    summed = _all_reduce_small(packed.reshape(rows, SMALL_COLS), "small_all_reduce").reshape(-1)
    off = 0
    for n, width in zip(REPLICATED, widths):
        grads[n] = summed[off:off + width].reshape(1, width)
        off += width

    out_g, out_d, out_m, out_v = [], [], [], []
    for n in WEIGHTS:
        shape = wl[n].shape
        g2 = grads[n]
        go, d, nm, nv = _adamw(wl[n].reshape(g2.shape), g2, ml[n].reshape(g2.shape), vl[n].reshape(g2.shape), f"adamw_{n}")
        out_g.append(go.reshape(shape))
        out_d.append(d.reshape(shape))
        out_m.append(nm.reshape(shape))
        out_v.append(nv.reshape(shape))
    return (loss, dx0.reshape(x.shape), *out_g, *out_d, *out_m, *out_v)
```

```python
import functools

import jax
import jax.numpy as jnp
from jax import lax
from jax.experimental import pallas as pl
from jax.experimental.pallas import tpu as pltpu
from jax.experimental.pallas import tpu_sc as plsc

F32 = jnp.float32
BF16 = jnp.bfloat16
MESH = pl.DeviceIdType.MESH

RMS_EPS = 1e-6
CHUNK = 64
POOL_WINDOWS = (2, 4, 8, 16)
POOL_HALO = 16
N_HEADS = 4
GATE_TEMP = 16.0
ADAM_LR, ADAM_B1, ADAM_B2, ADAM_EPS, ADAM_WD, ADAM_STEP = 0.001, 0.9, 0.999, 1e-08, 0.01, 10
N_SHARDS = 4
LANES = 128
MXU_COLS = 256
VMEM_LIMIT = 58 * 1024 * 1024

ANY = pl.BlockSpec(memory_space=pl.ANY)
HBM = pl.BlockSpec(memory_space=pltpu.HBM)
SEM = pl.BlockSpec(memory_space=pltpu.SEMAPHORE)
EFFECT = pltpu.SideEffectType.DATAFLOW_SIDE_EFFECTING


def _params(**kw):
    return pltpu.CompilerParams(vmem_limit_bytes=VMEM_LIMIT, **kw)


def _tile(n, want):
    for unit in (LANES, 8):
        t = (min(want, n) // unit) * unit
        while t >= unit:
            if n % t == 0:
                return t
            t -= unit
    return n


def _dot(a, b, dims):
    return lax.dot_general(a, b, (dims, ((), ())), preferred_element_type=F32)


def _nn(a, b):
    return _dot(a, b, ((1,), (0,)))


def _nt(a, b):
    return _dot(a, b, ((1,), (1,)))


def _tn(a, b):
    return _dot(a, b, ((0,), (0,)))


def _sigmoid(x):
    return 1.0 / (1.0 + jnp.exp(-x))


def _matmul(a, b, *, mode, name, out_dtype, tm=512, tn=2048, tk=2048, res=None, scale=1.0, b_groups=False, out_groups=0,
            dep=()):
    if mode == "tn":
        K, M = a.shape
    else:
        M, K = a.shape
    if mode == "nn":
        if b_groups:
            G, _, Nj = b.shape
            N = G * Nj
        else:
            N = b.shape[1]
    elif mode == "nt":
        if b_groups:
            G, N, Kj = b.shape
            assert G * Kj == K
        else:
            N = b.shape[0]
    else:
        N = b.shape[1]
    tm = _tile(M, tm)
    if mode == "nn" and b_groups:
        tn = _tile(Nj, tn)
    elif out_groups:
        tn = _tile(N // out_groups, tn)
    else:
        tn = _tile(N, tn)
    if mode == "nt" and b_groups:
        tk = _tile(Kj, tk)
    else:
        tk = _tile(K, tk)
    nk = K // tk
    grid = (M // tm, N // tn, nk)

    if mode == "tn":
        a_spec = pl.BlockSpec((tk, tm), lambda i, j, k: (k, i))
        b_spec = pl.BlockSpec((tk, tn), lambda i, j, k: (k, j))
        dims = ((0,), (0,))
    elif mode == "nn":
        a_spec = pl.BlockSpec((tm, tk), lambda i, j, k: (i, k))
        if b_groups:
            npj = Nj // tn
            b_spec = pl.BlockSpec((None, tk, tn), lambda i, j, k: (j // npj, k, j % npj))
        else:
            b_spec = pl.BlockSpec((tk, tn), lambda i, j, k: (k, j))
        dims = ((1,), (0,))
    else:
        a_spec = pl.BlockSpec((tm, tk), lambda i, j, k: (i, k))
        if b_groups:
            kpj = Kj // tk
            b_spec = pl.BlockSpec((None, tn, tk), lambda i, j, k: (k // kpj, j, k % kpj))
        else:
            b_spec = pl.BlockSpec((tn, tk), lambda i, j, k: (j, k))
        dims = ((1,), (1,))
    if out_groups:
        npj = (N // out_groups) // tn
        o_spec = pl.BlockSpec((None, tm, tn), lambda i, j, k: (j // npj, i, j % npj))
        out_shape = jax.ShapeDtypeStruct((out_groups, M, N // out_groups), out_dtype)
    else:
        o_spec = pl.BlockSpec((tm, tn), lambda i, j, k: (i, j))
        out_shape = jax.ShapeDtypeStruct((M, N), out_dtype)
    in_specs = [a_spec, b_spec]
    operands = [a, b]
    if res is not None:
        in_specs.append(pl.BlockSpec((tm, tn), lambda i, j, k: (i, j)))
        operands.append(res)
    has_res = res is not None
    n_dep = len(dep)
    for d in dep:
        in_specs.append(pl.BlockSpec(d.shape, lambda i, j, k: (0, 0)))
        operands.append(d)

    def body(*refs):
        if has_res:
            a_ref, b_ref, r_ref = refs[:3]
        else:
            a_ref, b_ref = refs[:2]
            r_ref = None
        o_ref = refs[2 + has_res + n_dep]

        def finish(acc):
            if scale != 1.0:
                acc = acc * scale
            if r_ref is not None:
                acc = r_ref[...] + acc
            o_ref[...] = acc.astype(o_ref.dtype)

        part = _dot(a_ref[...], b_ref[...], dims)
        if nk == 1:
            finish(part)
        else:
            acc_ref = o_ref if in_place else refs[-1]
            k = pl.program_id(2)

            @pl.when(k == 0)
            def _():
                acc_ref[...] = part

            @pl.when(k > 0)
            def _():
                acc_ref[...] += part

            if not in_place:
                @pl.when(k == nk - 1)
                def _():
                    finish(acc_ref[...])

    in_place = out_dtype == F32 and res is None and scale == 1.0
    scratch = [] if nk == 1 or in_place else [pltpu.VMEM((tm, tn), F32)]
    return pl.pallas_call(body, name=name, grid=grid, in_specs=in_specs, out_specs=o_spec, out_shape=out_shape,
                          scratch_shapes=scratch, compiler_params=_params())(*operands)


def _rms_fwd(x, gain, name, tm=256, dep=None):
    S, D = x.shape
    tm = _tile(S, tm)

    def body(x_ref, g_ref, *rest):
        o_ref = rest[-1]
        xv = x_ref[...]
        r = lax.rsqrt(jnp.mean(xv * xv, axis=-1, keepdims=True) + RMS_EPS)
        o_ref[...] = (xv * r * g_ref[...]).astype(o_ref.dtype)

    in_specs = [pl.BlockSpec((tm, D), lambda i: (i, 0)), pl.BlockSpec((1, D), lambda i: (0, 0))]
    operands = [x, gain]
    if dep is not None:
        in_specs.append(pl.BlockSpec(dep.shape, lambda i: (0, 0)))
        operands.append(dep)
    return pl.pallas_call(body, name=name, grid=(S // tm,), in_specs=in_specs,
                          out_specs=pl.BlockSpec((tm, D), lambda i: (i, 0)),
                          out_shape=jax.ShapeDtypeStruct((S, D), BF16), compiler_params=_params())(*operands)


def _rms_bwd(x, gain, dh, dres, name, lowp=None, tm=256):
    half = lowp is not None
    S, D = x.shape
    tm = _tile(S, tm)
    has_res = dres is not None

    def body(*refs):
        if has_res:
            x_ref, g_ref, dh_ref, dr_ref = refs[:4]
            outs = refs[4:]
        else:
            x_ref, g_ref, dh_ref = refs[:3]
            dr_ref = None
            outs = refs[3:]
        dx_ref, dg_ref = outs[0], outs[-1]
        xv = x_ref[...]
        dhv = dh_ref[...].astype(F32)
        r = lax.rsqrt(jnp.mean(xv * xv, axis=-1, keepdims=True) + RMS_EPS)
        gy = dhv * g_ref[...]
        dx = r * gy - xv * (r * r * r) * jnp.mean(gy * xv, axis=-1, keepdims=True)
        if dr_ref is not None:
            dx = dx + dr_ref[...]
        dx_ref[...] = dx
        if half:
            outs[1][...] = (dx if lowp == 1.0 else lowp * dx).astype(BF16)
        part = jnp.sum(dhv * xv * r, axis=0, keepdims=True)

        @pl.when(pl.program_id(0) == 0)
        def _():
            dg_ref[...] = part

        @pl.when(pl.program_id(0) > 0)
        def _():
            dg_ref[...] += part

    row = pl.BlockSpec((tm, D), lambda i: (i, 0))
    vec = pl.BlockSpec((1, D), lambda i: (0, 0))
    in_specs = [row, vec, row] + ([row] if has_res else [])
    operands = [x, gain, dh] + ([dres] if has_res else [])
    out_specs = [row] + ([row] if half else []) + [vec]
    out_shape = [jax.ShapeDtypeStruct((S, D), F32)] + ([jax.ShapeDtypeStruct((S, D), BF16)] if half else []) + [
        jax.ShapeDtypeStruct((1, D), F32)]
    return pl.pallas_call(body, name=name, grid=(S // tm,), in_specs=in_specs, out_specs=out_specs, out_shape=out_shape,
                          compiler_params=_params())(*operands)


def _loss_head(x, gain, target, name, tm=256):
    S, D = x.shape
    tm = _tile(S, tm)

    def body(x_ref, g_ref, t_ref, sq_ref, dx_ref, dxh_ref, dg_ref):
        xv = x_ref[...]
        r = lax.rsqrt(jnp.mean(xv * xv, axis=-1, keepdims=True) + RMS_EPS)
        xn = xv * r
        err = xn * g_ref[...] - t_ref[...]
        dout = err * (1.0 / D)
        gy = dout * g_ref[...]
        dx = r * gy - xv * (r * r * r) * jnp.mean(gy * xv, axis=-1, keepdims=True)
        dx_ref[...] = dx
        dxh_ref[...] = (0.5 * dx).astype(BF16)
        sq = jnp.sum(err * err, axis=0, keepdims=True)
        dg = jnp.sum(dout * xn, axis=0, keepdims=True)

        @pl.when(pl.program_id(0) == 0)
        def _():
            sq_ref[...] = sq
            dg_ref[...] = dg

        @pl.when(pl.program_id(0) > 0)
        def _():
            sq_ref[...] += sq
            dg_ref[...] += dg

    row = pl.BlockSpec((tm, D), lambda i: (i, 0))
    vec = pl.BlockSpec((1, D), lambda i: (0, 0))
    return pl.pallas_call(body, name=name, grid=(S // tm,), in_specs=[row, vec, row], out_specs=[vec, row, row, vec],
                          out_shape=[jax.ShapeDtypeStruct((1, D), F32), jax.ShapeDtypeStruct((S, D), F32),
                                     jax.ShapeDtypeStruct((S, D), BF16), jax.ShapeDtypeStruct((1, D), F32)],
                          compiler_params=_params())(x, gain, target)


def _cast(x, dtype, name, scale=1.0, tm=256):
    S, D = x.shape
    tm = _tile(S, tm)

    def body(x_ref, o_ref):
        o_ref[...] = (x_ref[...] * scale).astype(o_ref.dtype)

    row = pl.BlockSpec((tm, D), lambda i: (i, 0))
    return pl.pallas_call(body, name=name, grid=(S // tm,), in_specs=[row], out_specs=row,
                          out_shape=jax.ShapeDtypeStruct((S, D), dtype), compiler_params=_params())(x)


def _ffn_up(h, wg, wu, name, tm=512):
    S, D = h.shape
    G, _, Fj = wg.shape
    tm = _tile(S, tm)

    def body(h_ref, wg_ref, wu_ref, ga_ref, gb_ref, hid_ref):
        hv = h_ref[...]
        a = _nn(hv, wg_ref[...])
        b = _nn(hv, wu_ref[...])
        s = _sigmoid(a)
        silu = a * s
        ga_ref[...] = (b * (s * (1.0 + a * (1.0 - s)))).astype(BF16)
        gb_ref[...] = silu.astype(BF16)
        hid_ref[...] = (silu * b).astype(BF16)

    w_spec = pl.BlockSpec((None, D, Fj), lambda g, i: (g, 0, 0))
    o_spec = pl.BlockSpec((tm, Fj), lambda g, i: (i, g))
    out = jax.ShapeDtypeStruct((S, G * Fj), BF16)
    return pl.pallas_call(body, name=name, grid=(G, S // tm),
                          in_specs=[pl.BlockSpec((tm, D), lambda g, i: (i, 0)), w_spec, w_spec],
                          out_specs=[o_spec, o_spec, o_spec], out_shape=[out, out, out], compiler_params=_params())(h, wg, wu)


def _ffn_dact(dxh, wd, ga, gb, name, tm=512):
    S, D = dxh.shape
    G, Fj, _ = wd.shape
    tm = _tile(S, tm)

    def body(dx_ref, wd_ref, ga_ref, gb_ref, da_ref, db_ref):
        dhid = _nt(dx_ref[...], wd_ref[...])
        da_ref[...] = (dhid * ga_ref[...].astype(F32)).astype(BF16)
        db_ref[...] = (dhid * gb_ref[...].astype(F32)).astype(BF16)

    blk = pl.BlockSpec((tm, Fj), lambda g, i: (i, g))
    out = jax.ShapeDtypeStruct((S, G * Fj), BF16)
    return pl.pallas_call(body, name=name, grid=(G, S // tm),
                          in_specs=[pl.BlockSpec((tm, D), lambda g, i: (i, 0)),
                                    pl.BlockSpec((None, Fj, D), lambda g, i: (g, 0, 0)), blk, blk],
                          out_specs=[blk, blk], out_shape=[out, out], compiler_params=_params())(dxh, wd, ga, gb)


def _ffn_dh(da, db, wg, wu, name, dep=(), tm=512):
    S = da.shape[0]
    G, D, Fj = wg.shape
    tm = _tile(S, tm)

    def body(da_ref, db_ref, wg_ref, wu_ref, *rest):
        o_ref = rest[-1]
        part = _nt(da_ref[...], wg_ref[...]) + _nt(db_ref[...], wu_ref[...])

        @pl.when(pl.program_id(1) == 0)
        def _():
            o_ref[...] = part

        @pl.when(pl.program_id(1) > 0)
        def _():
            o_ref[...] += part

    act = pl.BlockSpec((tm, Fj), lambda i, g: (i, g))
    w_spec = pl.BlockSpec((None, D, Fj), lambda i, g: (g, 0, 0))
    in_specs = [act, act, w_spec, w_spec] + [pl.BlockSpec(d.shape, lambda i, g: (0, 0)) for d in dep]
    return pl.pallas_call(body, name=name, grid=(S // tm, G), in_specs=in_specs,
                          out_specs=pl.BlockSpec((tm, D), lambda i, g: (i, 0)),
                          out_shape=jax.ShapeDtypeStruct((S, D), F32), compiler_params=_params())(da, db, wg, wu, *dep)


def _pool_fwd(proj, pool_w, pool_scale, name, tm=512):
    S = proj.shape[0]
    NG, C, _ = pool_w.shape
    DP = NG * C
    tm = _tile(S, tm)
    hb = tm // POOL_HALO
    n_ext = tm + POOL_HALO

    def body(u_ref, halo_ref, w_ref, sc_ref, y_ref, d_ref):
        i = pl.program_id(0)
        t = lax.broadcasted_iota(jnp.int32, (tm, 1), 0) + i * tm
        for g, win in enumerate(POOL_WINDOWS):
            cols = slice(g * C, (g + 1) * C)
            ug = u_ref[:, cols]
            halo = jnp.where(i > 0, halo_ref[:, cols], 0.0)
            acc = jnp.concatenate([halo, ug], axis=0)
            step = 1
            while step < win:
                acc = acc + pltpu.roll(acc, step, 0)
                step *= 2
            count = jnp.minimum(t + 1, win).astype(F32)
            d = (acc[POOL_HALO:, :] / count - ug).astype(BF16)
            d_ref[:, cols] = d
            y_ref[:, cols] = (_nn(d, w_ref[g]) * sc_ref[:, cols]).astype(BF16)

    del n_ext
    return pl.pallas_call(
        body, name=name, grid=(S // tm,),
        in_specs=[pl.BlockSpec((tm, DP), lambda i: (i, 0)),
                  pl.BlockSpec((POOL_HALO, DP), lambda i: (jnp.maximum(i * hb - 1, 0), 0)),
                  pl.BlockSpec((NG, C, C), lambda i: (0, 0, 0)), pl.BlockSpec((1, DP), lambda i: (0, 0))],
        out_specs=[pl.BlockSpec((tm, DP), lambda i: (i, 0)), pl.BlockSpec((tm, DP), lambda i: (i, 0))],
        out_shape=[jax.ShapeDtypeStruct((S, DP), BF16), jax.ShapeDtypeStruct((S, DP), BF16)],
        compiler_params=_params())(proj, proj, pool_w, pool_scale)


def _pool_bwd(dymix, d, pool_w, pool_scale, name, tm=512):
    S = dymix.shape[0]
    NG, C, _ = pool_w.shape
    DP = NG * C
    tm = _tile(S, tm)
    hb = tm // POOL_HALO
    nb = S // tm
    n_ext = tm + POOL_HALO
    last_halo = S // POOL_HALO - 1

    def body(dy_ref, halo_ref, d_ref, w_ref, sc_ref, du_ref, dw_ref, dsc_ref):
        i = pl.program_id(0)
        t = lax.broadcasted_iota(jnp.int32, (n_ext, 1), 0) + i * tm
        for g, win in enumerate(POOL_WINDOWS):
            cols = slice(g * C, (g + 1) * C)
            dy = dy_ref[:, cols]
            halo = jnp.where(i < nb - 1, halo_ref[:, cols], 0.0)
            sc = sc_ref[:, cols]
            dv = d_ref[:, cols]
            e_ext = (jnp.concatenate([dy, halo], axis=0) * sc).astype(BF16)
            dd = _nt(e_ext, w_ref[g])
            count = jnp.minimum(t + 1, win).astype(F32)
            acc = dd / count
            step = 1
            while step < win:
                acc = acc + pltpu.roll(acc, n_ext - step, 0)
                step *= 2
            du_ref[:, cols] = (acc[:tm, :] - dd[:tm, :]).astype(BF16)
            dw = _tn(dv, e_ext[:tm, :])
            dsc = jnp.sum(dy * _nn(dv, w_ref[g]), axis=0, keepdims=True)

            @pl.when(i == 0)
            def _():
                dw_ref[g] = dw
                dsc_ref[:, cols] = dsc

            @pl.when(i > 0)
            def _():
                dw_ref[g] += dw
                dsc_ref[:, cols] += dsc

    return pl.pallas_call(
        body, name=name, grid=(nb,),
        in_specs=[pl.BlockSpec((tm, DP), lambda i: (i, 0)),
                  pl.BlockSpec((POOL_HALO, DP), lambda i: (jnp.minimum((i + 1) * hb, last_halo), 0)),
                  pl.BlockSpec((tm, DP), lambda i: (i, 0)),
                  pl.BlockSpec((NG, C, C), lambda i: (0, 0, 0)), pl.BlockSpec((1, DP), lambda i: (0, 0))],
        out_specs=[pl.BlockSpec((tm, DP), lambda i: (i, 0)), pl.BlockSpec((NG, C, C), lambda i: (0, 0, 0)),
                   pl.BlockSpec((1, DP), lambda i: (0, 0))],
        out_shape=[jax.ShapeDtypeStruct((S, DP), BF16), jax.ShapeDtypeStruct((NG, C, C), F32),
                   jax.ShapeDtypeStruct((1, DP), F32)],
        compiler_params=_params())(dymix, dymix, d, pool_w, pool_scale)


def _chunk_scan(v, rows, reverse):
    n = v.shape[0]
    step = 1
    while step < CHUNK:
        if reverse:
            v = v + jnp.where(rows < CHUNK - step, pltpu.roll(v, n - step, 0), 0.0)
        else:
            v = v + jnp.where(rows >= step, pltpu.roll(v, step, 0), 0.0)
        step *= 2
    return v


def _log_decay(alr, w_a2, b_a):
    z = _nn(alr.astype(BF16), w_a2) + b_a
    la = (jnp.minimum(z, 0.0) - jnp.log(1.0 + jnp.exp(-jnp.abs(z)))) * (1.0 / GATE_TEMP)
    return z, la


def _gla_specs(DP, DKT, DV, tb, bmap):
    return [pl.BlockSpec((tb, DKT), lambda i: (bmap(i), DP // DKT)),
            pl.BlockSpec((tb, DKT), lambda i: (bmap(i), DP // DKT + 1)),
            pl.BlockSpec((tb, DV), lambda i: (bmap(i), (DP + 2 * DKT) // DV)),
            pl.BlockSpec((tb, DV), lambda i: (bmap(i), (DP + 2 * DKT) // DV + 1)),
            pl.BlockSpec((tb, LANES), lambda i: (bmap(i), (DP + 2 * DKT + 2 * DV) // LANES))]


def _gla_fwd(proj, y_pool, w_a2, b_a, head_norm, name, tb=512):
    S = proj.shape[0]
    DP = y_pool.shape[1]
    DKT = b_a.shape[1]
    DV = head_norm.shape[1]
    dk, dv = DKT // N_HEADS, DV // N_HEADS
    tb = _tile(S, tb)
    ncb = tb // CHUNK
    qscale = dk ** -0.5

    def body(q_ref, k_ref, v_ref, g_ref, alr_ref, yp_ref, wa_ref, ba_ref, hn_ref, y_ref, st_out_ref, st_ref, kdec_ref,
             gam_ref):
        @pl.when(pl.program_id(0) == 0)
        def _():
            st_ref[...] = jnp.zeros_like(st_ref)

        y_ref[:, :DP] = yp_ref[...]

        rows = lax.broadcasted_iota(jnp.int32, (tb, 1), 0) % CHUNK
        _, la = _log_decay(alr_ref[...], wa_ref[...], ba_ref[...])
        tail = _chunk_scan(la, rows, True)
        kdec_ref[...] = k_ref[...] * jnp.exp(tail - la)
        gam_ref[...] = jnp.exp(tail)

        def chunk(c, carry):
            r0 = pl.multiple_of(c * CHUNK, CHUNK)
            rs = pl.ds(r0, CHUNK)
            gam = gam_ref[pl.ds(r0, 1), :]
            heads = range(N_HEADS)
            kcs = [slice(h * dk, (h + 1) * dk) for h in heads]
            vcs = [slice(h * dv, (h + 1) * dv) for h in heads]
            upd = [_tn(v_ref[rs, vcs[h]].astype(BF16), kdec_ref[rs, kcs[h]].astype(BF16)) for h in heads]
            st = [st_ref[h] * gam[:, kcs[h]] + upd[h] for h in heads]
            o = [_nt((q_ref[rs, kcs[h]] * qscale).astype(BF16), st[h].astype(BF16)) for h in heads]
            for h in heads:
                st_ref[h] = st[h]
                st_out_ref[c, h] = st[h]
                r = lax.rsqrt(jnp.mean(o[h] * o[h], axis=-1, keepdims=True) + RMS_EPS)
                gv = g_ref[rs, vcs[h]]
                y_ref[rs, DP + h * dv:DP + (h + 1) * dv] = (o[h] * r * hn_ref[:, vcs[h]] * (gv * _sigmoid(gv))).astype(BF16)
            return carry

        lax.fori_loop(0, ncb, chunk, 0, unroll=2)

    full = lambda shape: pl.BlockSpec(shape, lambda i: (0,) * len(shape))
    return pl.pallas_call(
        body, name=name, grid=(S // tb,),
        in_specs=_gla_specs(DP, DKT, DV, tb, lambda i: i) + [pl.BlockSpec((tb, DP), lambda i: (i, 0)),
                                                            full((LANES, DKT)), full((1, DKT)), full((1, DV))],
        out_specs=[pl.BlockSpec((tb, DP + DV), lambda i: (i, 0)),
                   pl.BlockSpec((ncb, N_HEADS, dv, dk), lambda i: (i, 0, 0, 0))],
        out_shape=[jax.ShapeDtypeStruct((S, DP + DV), BF16), jax.ShapeDtypeStruct((S // CHUNK, N_HEADS, dv, dk), F32)],
        scratch_shapes=[pltpu.VMEM((N_HEADS, dv, dk), F32), pltpu.VMEM((tb, DKT), F32), pltpu.VMEM((tb, DKT), F32)],
        compiler_params=_params())(proj, proj, proj, proj, proj, y_pool, w_a2, b_a, head_norm)


def _gla_bwd(proj, states, dymix, du, w_a2, b_a, head_norm, name, tb=512):
    S = proj.shape[0]
    DP = du.shape[1]
    DKT = b_a.shape[1]
    DV = head_norm.shape[1]
    dk, dv = DKT // N_HEADS, DV // N_HEADS
    tb = _tile(S, tb)
    ncb = tb // CHUNK
    nb = S // tb
    qscale = dk ** -0.5
    rev = lambda i: nb - 1 - i

    q0, k0, v0, g0, a0 = DP, DP + DKT, DP + 2 * DKT, DP + 2 * DKT + DV, DP + 2 * DKT + 2 * DV

    def body(q_ref, k_ref, v_ref, g_ref, alr_ref, st_blk_ref, st_prev_ref, dy_ref, du_ref, wa_ref, ba_ref, hn_ref,
             dp_ref, dwa_ref, dba_ref, dhn_ref,
             dst_ref, kdec_ref, dec_ref, gam_ref, e_ref, dla_ref, dhn_acc_ref):
        i = pl.program_id(0)
        blk = rev(i)
        dp_ref[:, :DP] = du_ref[...]

        @pl.when(i == 0)
        def _():
            dst_ref[...] = jnp.zeros_like(dst_ref)

        dhn_acc_ref[...] = jnp.zeros_like(dhn_acc_ref)
        rows = lax.broadcasted_iota(jnp.int32, (tb, 1), 0) % CHUNK
        z, la = _log_decay(alr_ref[...], wa_ref[...], ba_ref[...])
        tail = _chunk_scan(la, rows, True)
        dec_ref[...] = jnp.exp(tail - la)
        kdec_ref[...] = k_ref[...] * dec_ref[...]
        gam_ref[...] = jnp.exp(tail)

        def chunk(cc, carry):
            c = ncb - 1 - cc
            r0 = pl.multiple_of(c * CHUNK, CHUNK)
            rs = pl.ds(r0, CHUNK)
            gam = gam_ref[pl.ds(r0, 1), :]
            first = jnp.logical_and(blk == 0, c == 0)
            heads = range(N_HEADS)
            kcs = [slice(h * dk, (h + 1) * dk) for h in heads]
            vcs = [slice(h * dv, (h + 1) * dv) for h in heads]
            qs = [(q_ref[rs, kcs[h]] * qscale).astype(BF16) for h in heads]
            stb = [st_blk_ref[c, h].astype(BF16) for h in heads]
            o = [_nt(qs[h], stb[h]) for h in heads]
            do = []
            for h in heads:
                oh = o[h]
                r = lax.rsqrt(jnp.mean(oh * oh, axis=-1, keepdims=True) + RMS_EPS)
                gv = g_ref[rs, vcs[h]]
                sg = _sigmoid(gv)
                dy = dy_ref[rs, vcs[h]]
                hn = hn_ref[:, vcs[h]]
                on = oh * r
                dp_ref[rs, g0 + h * dv:g0 + (h + 1) * dv] = (dy * on * hn * (sg * (1.0 + gv * (1.0 - sg)))).astype(BF16)
                don = dy * (gv * sg)
                dhn_acc_ref[:, vcs[h]] += jnp.sum(don * on, axis=0, keepdims=True)
                dn = don * hn
                do.append((r * dn - oh * (r * r * r) * jnp.mean(dn * oh, axis=-1, keepdims=True)).astype(BF16))
            dqs = [_nn(do[h], stb[h]) for h in heads]
            dst = [dst_ref[h] + _tn(do[h], qs[h]) for h in heads]
            for h in heads:
                dp_ref[rs, q0 + h * dk:q0 + (h + 1) * dk] = (dqs[h] * qscale).astype(BF16)
            dstb = [dst[h].astype(BF16) for h in heads]
            dvh = [_nt(kdec_ref[rs, kcs[h]].astype(BF16), dstb[h]) for h in heads]
            dkdec = [_nn(v_ref[rs, vcs[h]].astype(BF16), dstb[h]) for h in heads]
            gdg = []
            for h in heads:
                dp_ref[rs, v0 + h * dv:v0 + (h + 1) * dv] = dvh[h].astype(BF16)
                dp_ref[rs, k0 + h * dk:k0 + (h + 1) * dk] = (dkdec[h] * dec_ref[rs, kcs[h]]).astype(BF16)
                e_ref[rs, kcs[h]] = dkdec[h] * kdec_ref[rs, kcs[h]]
                st_prev = jnp.where(c > 0, st_blk_ref[jnp.maximum(c - 1, 0), h], st_prev_ref[0, h])
                st_prev = jnp.where(first, 0.0, st_prev)
                gdg.append(jnp.sum(dst[h] * st_prev, axis=0, keepdims=True) * gam[:, kcs[h]])
                dst_ref[h] = dst[h] * gam[:, kcs[h]]
            dla_ref[rs, :] = jnp.broadcast_to(jnp.concatenate(gdg, axis=1), (CHUNK, DKT))
            return carry

        lax.fori_loop(0, ncb, chunk, 0, unroll=2)

        ev = e_ref[...]
        dla = dla_ref[...] + _chunk_scan(ev, rows, False) - ev
        dz = dla * (1.0 / GATE_TEMP) * (1.0 - _sigmoid(z))
        dzb = dz.astype(BF16)
        dp_ref[:, a0:a0 + LANES] = _nt(dzb, wa_ref[...]).astype(BF16)
        dwa = _tn(alr_ref[...].astype(BF16), dzb)
        dba = jnp.sum(dz, axis=0, keepdims=True)

        @pl.when(i == 0)
        def _():
            dwa_ref[...] = dwa
            dba_ref[...] = dba
            dhn_ref[...] = dhn_acc_ref[...]

        @pl.when(i > 0)
        def _():
            dwa_ref[...] += dwa
            dba_ref[...] += dba
            dhn_ref[...] += dhn_acc_ref[...]

    full = lambda shape: pl.BlockSpec(shape, lambda i: (0,) * len(shape))
    rowblk = lambda w: pl.BlockSpec((tb, w), lambda i: (rev(i), 0))
    return pl.pallas_call(
        body, name=name, grid=(nb,),
        in_specs=_gla_specs(DP, DKT, DV, tb, rev) + [
            pl.BlockSpec((ncb, N_HEADS, dv, dk), lambda i: (rev(i), 0, 0, 0)),
            pl.BlockSpec((1, N_HEADS, dv, dk), lambda i: (jnp.maximum(rev(i) * ncb - 1, 0), 0, 0, 0)),
            pl.BlockSpec((tb, DV), lambda i: (rev(i), DP // DV)), rowblk(DP),
            full((LANES, DKT)), full((1, DKT)), full((1, DV))],
        out_specs=[rowblk(a0 + LANES), full((LANES, DKT)), full((1, DKT)), full((1, DV))],
        out_shape=[jax.ShapeDtypeStruct((S, a0 + LANES), BF16), jax.ShapeDtypeStruct((LANES, DKT), F32),
                   jax.ShapeDtypeStruct((1, DKT), F32), jax.ShapeDtypeStruct((1, DV), F32)],
        scratch_shapes=[pltpu.VMEM((N_HEADS, dv, dk), F32)] + [pltpu.VMEM((tb, DKT), F32)] * 5 + [pltpu.VMEM((1, DV), F32)],
        compiler_params=_params())(proj, proj, proj, proj, proj, states, states, dymix, du, w_a2, b_a, head_norm)


def _xattn_fwd(q, kv, name, tm=512):
    S, D = q.shape
    M = kv.shape[0]
    hd = D // N_HEADS
    tm = _tile(S, tm)
    scale = hd ** -0.5

    def body(q_ref, k_ref, v_ref, o_ref):
        for h in range(N_HEADS):
            hc = slice(h * hd, (h + 1) * hd)
            s = _nt(q_ref[:, hc], k_ref[:, hc]) * scale
            p = jnp.exp(s - jnp.max(s, axis=-1, keepdims=True))
            p = p / jnp.sum(p, axis=-1, keepdims=True)
            o_ref[:, hc] = _nn(p.astype(BF16), v_ref[:, hc]).astype(BF16)

    return pl.pallas_call(body, name=name, grid=(S // tm,),
                          in_specs=[pl.BlockSpec((tm, D), lambda i: (i, 0)), pl.BlockSpec((M, D), lambda i: (0, 0)),
                                    pl.BlockSpec((M, D), lambda i: (0, 1))],
                          out_specs=pl.BlockSpec((tm, D), lambda i: (i, 0)),
                          out_shape=jax.ShapeDtypeStruct((S, D), BF16), compiler_params=_params())(q, kv, kv)


def _xattn_bwd(q, kv, do, name, tm=512):
    S, D = q.shape
    M = kv.shape[0]
    hd = D // N_HEADS
    tm = _tile(S, tm)
    scale = hd ** -0.5

    def body(q_ref, k_ref, v_ref, do_ref, dq_ref, dkv_ref):
        first = pl.program_id(0) == 0
        for h in range(N_HEADS):
            hc = slice(h * hd, (h + 1) * hd)
            vcols = slice(D + h * hd, D + (h + 1) * hd)
            qh = q_ref[:, hc]
            kh = k_ref[:, hc]
            doh = do_ref[:, hc]
            s = _nt(qh, kh) * scale
            p = jnp.exp(s - jnp.max(s, axis=-1, keepdims=True))
            p = p / jnp.sum(p, axis=-1, keepdims=True)
            dvh = _tn(p.astype(BF16), doh)
            dp = _nt(doh, v_ref[:, hc])
            ds = ((p * (dp - jnp.sum(dp * p, axis=-1, keepdims=True))) * scale).astype(BF16)
            dq_ref[:, hc] = _nn(ds, kh).astype(BF16)
            dkh = _tn(ds, qh)

            @pl.when(first)
            def _():
                dkv_ref[:, hc] = dkh
                dkv_ref[:, vcols] = dvh

            @pl.when(jnp.logical_not(first))
            def _():
                dkv_ref[:, hc] += dkh
                dkv_ref[:, vcols] += dvh

    row = pl.BlockSpec((tm, D), lambda i: (i, 0))
    return pl.pallas_call(body, name=name, grid=(S // tm,),
                          in_specs=[row, pl.BlockSpec((M, D), lambda i: (0, 0)), pl.BlockSpec((M, D), lambda i: (0, 1)), row],
                          out_specs=[row, pl.BlockSpec((M, 2 * D), lambda i: (0, 0))],
                          out_shape=[jax.ShapeDtypeStruct((S, D), BF16), jax.ShapeDtypeStruct((M, 2 * D), F32)],
                          compiler_params=_params())(q, kv, kv, do)


def _local_step(x, mem, target, vec, weight, emit, dep0):
    DP = vec["pool_scale"].shape[1]
    g = {}
    pending = []
    begun = []
    emit_begin, emit_finish, early_update = emit

    def behind(fn, *a, **kw):
        dep = tuple(pending)
        pending.clear()
        out = fn(*a, dep=dep, **kw)
        while begun:
            pending.append(emit_finish(begun.pop(0), out))
        return out

    def mm(a, b, **kw):
        return behind(_matmul, a, b, **kw)

    def send(name, gfull):
        pending.append(emit_begin(name, gfull))
        begun.append(name)

    def ffn_fwd(xin, tag, dep):
        h = _rms_fwd(xin, vec[f"{tag}_norm"], f"{tag}_norm", dep=dep)
        ga, gb, hid = _ffn_up(h, weight(f"{tag}_w_gate", h), weight(f"{tag}_w_up", h), f"{tag}_up")
        wd = weight(f"{tag}_w_down", hid)
        G, Fj, D = wd.shape
        xo = _matmul(hid, wd.reshape(G * Fj, D), mode="nn", name=f"{tag}_down", out_dtype=F32, res=xin, scale=0.5,
                     tn=1024, tk=G * Fj)
        return xo, (h, ga, gb, hid)

    def ffn_bwd(dxh, saved, tag, last):
        h, ga, gb, hid = saved
        wg, wu, wd = weight(f"{tag}_w_gate"), weight(f"{tag}_w_up"), weight(f"{tag}_w_down")
        G, Fj, D = wd.shape
        send(f"{tag}_w_down", mm(hid, dxh, mode="tn", name=f"{tag}_dwd", out_dtype=F32, tm=Fj, tn=1024))
        da, db = _ffn_dact(dxh, wd, ga, gb, f"{tag}_dact")
        send(f"{tag}_w_gate", mm(h, da, mode="tn", name=f"{tag}_dwg", out_dtype=F32, tm=1024, tn=Fj, out_groups=G))
        dwu = mm(h, db, mode="tn", name=f"{tag}_dwu", out_dtype=F32, tm=1024, tn=Fj, out_groups=G)
        if last:
            pending.append(emit_begin(f"{tag}_w_up", dwu))
            pending.append(emit_finish(f"{tag}_w_up", pending[-1]))
        else:
            send(f"{tag}_w_up", dwu)
        return behind(_ffn_dh, da, db, wg, wu, f"{tag}_dh")

    x1, ffn1_saved = ffn_fwd(x, "ffn1", dep0)
    h2 = _rms_fwd(x1, vec["mix_norm"], "mix_norm")
    w_in = weight("w_in", h2)
    proj = _matmul(h2, w_in, mode="nn", name="w_in", out_dtype=F32, tn=1408)
    pool_w, w_a2 = weight("pool_w", h2), weight("gla_w_a2", h2)
    y_pool, dpool = _pool_fwd(proj, pool_w, vec["pool_scale"], "pool_fwd")
    ymix, states = _gla_fwd(proj, y_pool, w_a2, vec["gla_b_a"], vec["gla_head_norm"], "gla_fwd")
    w_out = weight("w_out", ymix)
    x2 = _matmul(ymix, w_out, mode="nn", name="w_out", out_dtype=F32, res=x1)
    h3 = _rms_fwd(x2, vec["xattn_norm"], "xattn_norm")
    mh = _rms_fwd(mem, vec["mem_norm"], "mem_norm")
    w_q = weight("xattn_w_q", h3)
    q = _matmul(h3, w_q, mode="nn", name="xattn_q", out_dtype=BF16)
    w_kv = weight("xattn_w_kv", q)
    kv = _matmul(mh, w_kv, mode="nn", name="xattn_kv", out_dtype=BF16, b_groups=True, tn=1024)
    o = _xattn_fwd(q, kv, "xattn_fwd")
    w_o = weight("xattn_w_o", o)
    x3 = _matmul(o, w_o, mode="nn", name="xattn_o", out_dtype=F32, res=x2)
    x4, ffn2_saved = ffn_fwd(x3, "ffn2", None)
    sq, dx4, dx4h, g["final_norm"] = _loss_head(x4, vec["final_norm"], target, "loss_head")

    dh = ffn_bwd(dx4h, ffn2_saved, "ffn2", False)
    dx3, dx3b, g["ffn2_norm"] = _rms_bwd(x3, vec["ffn2_norm"], dh, dx4, "ffn2_norm_bwd", lowp=1.0)
    send("xattn_w_o", mm(o, dx3b, mode="tn", name="xattn_dwo", out_dtype=F32, tm=1024, tn=1024))
    do = mm(dx3b, w_o, mode="nt", name="xattn_do", out_dtype=BF16)
    dq, dkv = _xattn_bwd(q, kv, do, "xattn_bwd")
    send("xattn_w_q", mm(h3, dq, mode="tn", name="xattn_dwq", out_dtype=F32, tm=1024, tn=1024))
    dh3 = mm(dq, w_q, mode="nt", name="xattn_dh", out_dtype=F32)
    dkvb = _cast(dkv, BF16, "dkv_cast")
    send("xattn_w_kv", mm(mh, dkvb, mode="tn", name="xattn_dwkv", out_dtype=F32, tm=1024, tn=1024, out_groups=N_SHARDS))
    dmh = mm(dkvb, w_kv, mode="nt", name="xattn_dmh", out_dtype=F32, b_groups=True, tk=1024)
    _, g["mem_norm"] = _rms_bwd(mem, vec["mem_norm"], dmh, None, "mem_norm_bwd")
    pending.append(g["mem_norm"])
    dx2, dx2b, g["xattn_norm"] = _rms_bwd(x2, vec["xattn_norm"], dh3, dx3, "xattn_norm_bwd", lowp=1.0)
    send("w_out", mm(ymix, dx2b, mode="tn", name="dw_out", out_dtype=F32, tm=1024, tn=1024))
    dymix = mm(dx2b, w_out, mode="nt", name="dymix", out_dtype=F32)
    du, dpool_w, g["pool_scale"] = _pool_bwd(dymix, dpool, pool_w, vec["pool_scale"], "pool_bwd")
    send("pool_w", dpool_w)
    dproj, dw_a2, g["gla_b_a"], g["gla_head_norm"] = _gla_bwd(
        proj, states, dymix, du, w_a2, vec["gla_b_a"], vec["gla_head_norm"], "gla_bwd")
    send("gla_w_a2", dw_a2)
    send("w_in", mm(h2, dproj, mode="tn", name="dw_in", out_dtype=F32, tm=1024, tn=1408))
    dh2 = mm(dproj, w_in, mode="nt", name="dh2", out_dtype=F32, tn=1024, tk=dproj.shape[1])
    pending.extend(early_update(dh2))
    dx1, dx1h, g["mix_norm"] = _rms_bwd(x1, vec["mix_norm"], dh2, dx2, "mix_norm_bwd", lowp=0.5)
    dh = ffn_bwd(dx1h, ffn1_saved, "ffn1", True)
    dx0, g["ffn1_norm"] = _rms_bwd(x, vec["ffn1_norm"], dh, dx1, "ffn1_norm_bwd")
    return sq, dx0, g


def _place():
    x, y, c = lax.axis_index("x"), lax.axis_index("y"), lax.axis_index("c")
    chips = [(1 - x, y), (x, 1 - y), (1 - x, 1 - y)]
    return x, y, c, chips


def _ids():
    return jnp.stack([2 * lax.axis_index("x") + lax.axis_index("y"), lax.axis_index("c")]).astype(jnp.int32)


def _hbm(a):
    return pltpu.with_memory_space_constraint(a, pltpu.HBM)


def _cast_to_slot(w2d, dtype, name, dep=None):
    R, C = w2d.shape
    tr = _tile(R, max(16, (4 << 20) // (4 * C) // 16 * 16))

    def body(i_ref, w_ref, *rest):
        rest[-1][...] = w_ref[...].astype(dtype)

    in_specs = [pl.BlockSpec((tr, C), lambda r, i: (r, 0))]
    operands = [w2d]
    if dep is not None:
        in_specs.append(pl.BlockSpec(dep.shape, lambda r, i: (0, 0)))
        operands.append(dep)
    grid_spec = pltpu.PrefetchScalarGridSpec(num_scalar_prefetch=1, grid=(R // tr,), in_specs=in_specs,
                                             out_specs=pl.BlockSpec((None, tr, C), lambda r, i: (i[0], r, 0)))
    return pl.pallas_call(body, name=name, grid_spec=grid_spec, out_shape=jax.ShapeDtypeStruct((N_SHARDS, R, C), dtype),
                          compiler_params=_params())(_ids(), *operands)


def _gather_copies(buf_ref, send_sems, recv_sems, incoming, whole):
    x, y, c, chips = _place()
    hr = buf_ref.shape[1] // 2
    copies = []
    for j, (px, py) in enumerate(chips):
        slot = 2 * px + py if incoming else 2 * x + y
        part = buf_ref.at[slot] if whole else buf_ref.at[slot, pl.ds(c * hr, hr), :]
        copies.append(pltpu.make_async_remote_copy(src_ref=part, dst_ref=part, send_sem=send_sems.at[j],
                                                   recv_sem=recv_sems.at[j], device_id=(px, py, c), device_id_type=MESH))
    return copies


def _gather_start(buf, name, whole):
    def body(b_ref, send_sems, recv_sems, b_thru, token):
        for cp in _gather_copies(b_ref, send_sems, recv_sems, False, whole):
            cp.start()
        token[...] = jnp.zeros_like(token)

    return pl.pallas_call(
        body, name=name,
        out_shape=(pltpu.SemaphoreType.DMA((3,)), pltpu.SemaphoreType.DMA((3,)), pltpu.HBM(buf.shape, buf.dtype),
                   jax.ShapeDtypeStruct((8, LANES), F32)),
        in_specs=(HBM,), out_specs=(SEM, SEM, HBM, pl.BlockSpec(memory_space=pltpu.VMEM)), input_output_aliases={0: 2},
        compiler_params=pltpu.CompilerParams(has_side_effects=EFFECT))(_hbm(buf))


def _gather_wait(send_sems, recv_sems, buf_thru, after, name, whole):
    def body(b_ref, send_sems, recv_sems, after_ref, b_out):
        for cp in _gather_copies(b_ref, send_sems, recv_sems, False, whole):
            cp.wait_send()
        for cp in _gather_copies(b_ref, send_sems, recv_sems, True, whole):
            cp.wait_recv()

    return pl.pallas_call(
        body, name=name, out_shape=pltpu.HBM(buf_thru.shape, buf_thru.dtype),
        in_specs=(HBM, SEM, SEM, ANY), out_specs=HBM, input_output_aliases={0: 0},
        compiler_params=pltpu.CompilerParams(has_side_effects=EFFECT))(buf_thru, send_sems, recv_sems, after)


def _gather_forward(buf, name):
    G, R, C = buf.shape
    hr = R // 2

    def body(b_ref, o_ref, send_sems, recv_sems):
        x, y, c, chips = _place()
        copies = []
        for j, (px, py) in enumerate(chips):
            half = o_ref.at[2 * px + py, pl.ds(c * hr, hr), :]
            copies.append(pltpu.make_async_remote_copy(src_ref=half, dst_ref=half, send_sem=send_sems.at[j],
                                                       recv_sem=recv_sems.at[j], device_id=(x, y, 1 - c),
                                                       device_id_type=MESH))
        for cp in copies:
            cp.start()
        for j, (px, py) in enumerate(chips):
            half = o_ref.at[2 * px + py, pl.ds((1 - c) * hr, hr), :]
            pltpu.make_async_remote_copy(src_ref=half, dst_ref=half, send_sem=send_sems.at[j], recv_sem=recv_sems.at[j],
                                         device_id=(x, y, 1 - c), device_id_type=MESH).wait_recv()
        for cp in copies:
            cp.wait_send()

    return pl.pallas_call(body, name=name, in_specs=[ANY], out_specs=ANY, out_shape=jax.ShapeDtypeStruct(buf.shape, buf.dtype),
                          input_output_aliases={0: 0},
                          scratch_shapes=[pltpu.SemaphoreType.DMA((3,)), pltpu.SemaphoreType.DMA((3,))])(buf)


def _pair_copy(g_ref, land_ref, send_sem, recv_sem):
    x, y, c, _ = _place()
    hr = g_ref.shape[1] // 2
    return pltpu.make_async_remote_copy(src_ref=g_ref.at[:, pl.ds((1 - c) * hr, hr), :], dst_ref=land_ref,
                                        send_sem=send_sem, recv_sem=recv_sem, device_id=(x, y, 1 - c), device_id_type=MESH)


def _pair_start(gfull, name):
    G, R, C = gfull.shape

    def body(g_ref, land_ref, send_sem, recv_sem, g_thru, land_thru, token):
        _pair_copy(g_ref, land_ref, send_sem, recv_sem).start()
        token[...] = jnp.zeros_like(token)

    return pl.pallas_call(
        body, name=name,
        out_shape=(pltpu.SemaphoreType.DMA(()), pltpu.SemaphoreType.DMA(()), pltpu.HBM(gfull.shape, F32),
                   pltpu.HBM((G, R // 2, C), F32), jax.ShapeDtypeStruct((8, LANES), F32)),
        in_specs=(HBM, HBM), out_specs=(SEM, SEM, HBM, HBM, pl.BlockSpec(memory_space=pltpu.VMEM)),
        input_output_aliases={0: 2, 1: 3},
        compiler_params=pltpu.CompilerParams(has_side_effects=EFFECT))(_hbm(gfull), _hbm(lax.empty((G, R // 2, C), F32)))


def _pair_wait(send_sem, recv_sem, g_thru, land_thru, after, name):
    def body(g_ref, land_ref, send_sem, recv_sem, after_ref, g_out, land_out):
        cp = _pair_copy(g_ref, land_ref, send_sem, recv_sem)
        cp.wait_send()
        cp.wait_recv()

    return pl.pallas_call(
        body, name=name, out_shape=(pltpu.HBM(g_thru.shape, F32), pltpu.HBM(land_thru.shape, F32)),
        in_specs=(HBM, HBM, SEM, SEM, ANY), out_specs=(HBM, HBM), input_output_aliases={0: 0, 1: 1},
        compiler_params=pltpu.CompilerParams(has_side_effects=EFFECT))(g_thru, land_thru, send_sem, recv_sem, after)


def _pair_add(gfull, other, name):
    G, R, C = gfull.shape
    hr = R // 2
    tr = _tile(hr, max(8, (2 * 1024 * 1024) // (4 * C) // 8 * 8))
    nr = hr // tr
    c = lax.axis_index("c")
    cidx = jnp.reshape(c, (1,)).astype(jnp.int32)

    def body(c_ref, a_ref, b_ref, o_ref):
        o_ref[...] = a_ref[...] + b_ref[...]

    grid_spec = pltpu.PrefetchScalarGridSpec(
        num_scalar_prefetch=1, grid=(G, nr),
        in_specs=[pl.BlockSpec((None, tr, C), lambda g, r, cr: (g, cr[0] * nr + r, 0)),
                  pl.BlockSpec((None, tr, C), lambda g, r, cr: (g, r, 0))],
        out_specs=pl.BlockSpec((None, tr, C), lambda g, r, cr: (g, r, 0)))
    return pl.pallas_call(body, name=name, grid_spec=grid_spec, out_shape=jax.ShapeDtypeStruct((G, hr, C), F32),
                          compiler_params=_params())(cidx, gfull, other)


def _chip_copies(p_ref, land_ref, send_sems, recv_sems, incoming):
    x, y, c, chips = _place()
    me = 2 * x + y
    copies = []
    for j, (px, py) in enumerate(chips):
        dst = land_ref.at[2 * px + py] if incoming else land_ref.at[me]
        copies.append(pltpu.make_async_remote_copy(src_ref=p_ref.at[2 * px + py], dst_ref=dst, send_sem=send_sems.at[j],
                                                   recv_sem=recv_sems.at[j], device_id=(px, py, c), device_id_type=MESH))
    return copies


def _chip_start(part, name):
    def body(p_ref, land_ref, send_sems, recv_sems, p_thru, land_thru, token):
        for cp in _chip_copies(p_ref, land_ref, send_sems, recv_sems, False):
            cp.start()
        token[...] = jnp.zeros_like(token)

    return pl.pallas_call(
        body, name=name,
        out_shape=(pltpu.SemaphoreType.DMA((3,)), pltpu.SemaphoreType.DMA((3,)), pltpu.HBM(part.shape, F32),
                   pltpu.HBM(part.shape, F32), jax.ShapeDtypeStruct((8, LANES), F32)),
        in_specs=(HBM, HBM), out_specs=(SEM, SEM, HBM, HBM, pl.BlockSpec(memory_space=pltpu.VMEM)),
        input_output_aliases={0: 2, 1: 3},
        compiler_params=pltpu.CompilerParams(has_side_effects=EFFECT))(_hbm(part), _hbm(lax.empty(part.shape, F32)))


def _chip_wait(send_sems, recv_sems, p_thru, land_thru, after, name):
    def body(p_ref, land_ref, send_sems, recv_sems, after_ref, p_out, land_out):
        for cp in _chip_copies(p_ref, land_ref, send_sems, recv_sems, False):
            cp.wait_send()
        for cp in _chip_copies(p_ref, land_ref, send_sems, recv_sems, True):
            cp.wait_recv()

    return pl.pallas_call(
        body, name=name, out_shape=(pltpu.HBM(p_thru.shape, F32), pltpu.HBM(p_thru.shape, F32)),
        in_specs=(HBM, HBM, SEM, SEM, ANY), out_specs=(HBM, HBM), input_output_aliases={0: 0, 1: 1},
        compiler_params=pltpu.CompilerParams(has_side_effects=EFFECT))(p_thru, land_thru, send_sems, recv_sems, after)


def _chip_sum(part, slots, name):
    G, R2, C = part.shape
    tr = _tile(R2, max(8, (1 << 20) // (4 * C) // 8 * 8))
    nr = R2 // tr

    def body(i_ref, p_ref, *rest):
        o_ref = rest[-1]
        acc = None
        for u in range(G):
            val = jnp.where(i_ref[0] == u, p_ref[...], rest[u][...])
            acc = val if acc is None else acc + val
        o_ref[...] = acc

    def slot_spec(u):
        return pl.BlockSpec((None, tr, C), lambda r, i: (jnp.where(i[0] == u, (u + 1) % G, u), r, 0))

    grid_spec = pltpu.PrefetchScalarGridSpec(
        num_scalar_prefetch=1, grid=(nr,),
        in_specs=[pl.BlockSpec((None, tr, C), lambda r, i: (i[0], r, 0))] + [slot_spec(u) for u in range(G)],
        out_specs=pl.BlockSpec((tr, C), lambda r, i: (i[1] * nr + r, 0)))
    return pl.pallas_call(body, name=name, grid_spec=grid_spec, out_shape=jax.ShapeDtypeStruct((2 * R2, C), F32),
                          compiler_params=_params())(_ids(), part, slots, slots, slots, slots)


def _sum_slots(slots, name):
    G, R2, C = slots.shape
    tr = _tile(R2, max(8, (1024 * 1024) // (4 * C) // 8 * 8))

    def body(s_ref, o_ref):
        acc = s_ref[0]
        for u in range(1, G):
            acc = acc + s_ref[u]
        o_ref[...] = acc

    return pl.pallas_call(body, name=name, grid=(R2 // tr,), in_specs=[pl.BlockSpec((G, tr, C), lambda r: (0, r, 0))],
                          out_specs=pl.BlockSpec((tr, C), lambda r: (r, 0)), out_shape=jax.ShapeDtypeStruct((R2, C), F32),
                          compiler_params=_params())(slots)


def _pair_join(full, name):
    R, C = full.shape
    R2 = R // 2

    def body(f_ref, o_ref, token, send_sem, recv_sem):
        x, y, c, _ = _place()
        token[...] = jnp.zeros_like(token)
        mine = o_ref.at[pl.ds(c * R2, R2), :]
        theirs = o_ref.at[pl.ds((1 - c) * R2, R2), :]
        cp = pltpu.make_async_remote_copy(src_ref=mine, dst_ref=mine, send_sem=send_sem, recv_sem=recv_sem,
                                          device_id=(x, y, 1 - c), device_id_type=MESH)
        cp.start()
        pltpu.make_async_remote_copy(src_ref=theirs, dst_ref=theirs, send_sem=send_sem, recv_sem=recv_sem,
                                     device_id=(x, y, 1 - c), device_id_type=MESH).wait_recv()
        cp.wait_send()

    return pl.pallas_call(body, name=name, in_specs=[ANY], out_specs=[ANY, pl.BlockSpec(memory_space=pltpu.VMEM)],
                          out_shape=[jax.ShapeDtypeStruct((R, C), F32), jax.ShapeDtypeStruct((8, LANES), F32)],
                          input_output_aliases={0: 0},
                          scratch_shapes=[pltpu.SemaphoreType.DMA, pltpu.SemaphoreType.DMA])(full)


def _all_reduce_small(v, name):
    R, C = v.shape

    def gather_body(v_ref, out_ref, send_sems, recv_sems, local_sem):
        x, y, c, _ = _place()
        me = 4 * x + 2 * y + c
        mine = pltpu.make_async_copy(v_ref, out_ref.at[me], local_sem)
        mine.start()
        flips = [(fx, fy, fc) for fx in (0, 1) for fy in (0, 1) for fc in (0, 1)][1:]
        copies = []
        for j, (fx, fy, fc) in enumerate(flips):
            peer = (x ^ fx, y ^ fy, c ^ fc)
            copies.append(pltpu.make_async_remote_copy(src_ref=v_ref, dst_ref=out_ref.at[me], send_sem=send_sems.at[j],
                                                       recv_sem=recv_sems.at[j], device_id=peer, device_id_type=MESH))
        for cp in copies:
            cp.start()
        for j, (fx, fy, fc) in enumerate(flips):
            peer = (x ^ fx, y ^ fy, c ^ fc)
            pltpu.make_async_remote_copy(src_ref=v_ref, dst_ref=out_ref.at[4 * peer[0] + 2 * peer[1] + peer[2]],
                                         send_sem=send_sems.at[j], recv_sem=recv_sems.at[j], device_id=peer,
                                         device_id_type=MESH).wait_recv()
        for cp in copies:
            cp.wait_send()
        mine.wait()

    slots = pl.pallas_call(gather_body, name=name, in_specs=[ANY], out_specs=ANY,
                           out_shape=jax.ShapeDtypeStruct((8, R, C), F32),
                           scratch_shapes=[pltpu.SemaphoreType.DMA((7,)), pltpu.SemaphoreType.DMA((7,)),
                                           pltpu.SemaphoreType.DMA])(v)
    return _sum_slots(slots, f"{name}_sum")


def _adamw(w, g, m, v, name):
    R, C = w.shape
    tr = _tile(R, max(8, (2 << 20) // (4 * C) // 8 * 8))
    bc1 = 1.0 - ADAM_B1 ** ADAM_STEP
    bc2 = 1.0 - ADAM_B2 ** ADAM_STEP

    def body(w_ref, g_ref, m_ref, v_ref, go_ref, d_ref, nm_ref, nv_ref):
        gv = g_ref[...]
        go_ref[...] = gv
        nm = ADAM_B1 * m_ref[...] + (1.0 - ADAM_B1) * gv
        nv = ADAM_B2 * v_ref[...] + (1.0 - ADAM_B2) * (gv * gv)
        nm_ref[...] = nm
        nv_ref[...] = nv
        d_ref[...] = -ADAM_LR * ((nm / bc1) / (jnp.sqrt(nv / bc2) + ADAM_EPS) + ADAM_WD * w_ref[...])

    blk = pl.BlockSpec((tr, C), lambda r: (r, 0))
    out = jax.ShapeDtypeStruct((R, C), F32)
    return pl.pallas_call(body, name=name, grid=(R // tr,), in_specs=[blk] * 4, out_specs=[blk] * 4, out_shape=[out] * 4,
                          compiler_params=_params())(w, g, m, v)


SC_TILES = 32
SC_LANES = 16
SC_ROWS = 8


def _adamw_sc(w, g, m, v, name):
    R, C = w.shape
    per_tile = R // SC_TILES
    bc1 = 1.0 - ADAM_B1 ** ADAM_STEP
    bc2 = 1.0 - ADAM_B2 ** ADAM_STEP

    def body(w_hbm, g_hbm, m_hbm, v_hbm, go_hbm, d_hbm, nm_hbm, nv_hbm, wb, gb, mb, vb):
        tile = lax.axis_index("sc_subcore") * 2 + lax.axis_index("sc_core")
        base = tile * per_tile

        @pl.loop(0, per_tile, step=SC_ROWS)
        def _(r):
            rows = pl.ds(base + r, SC_ROWS)
            pltpu.sync_copy(w_hbm.at[rows, :], wb)
            pltpu.sync_copy(g_hbm.at[rows, :], gb)
            pltpu.sync_copy(m_hbm.at[rows, :], mb)
            pltpu.sync_copy(v_hbm.at[rows, :], vb)

            @pl.loop(0, SC_ROWS)
            def _(i):
                @pl.loop(0, C, step=SC_LANES)
                def _(j):
                    at = (i, pl.ds(j, SC_LANES))
                    gv = gb[at]
                    nm = ADAM_B1 * mb[at] + (1.0 - ADAM_B1) * gv
                    nv = ADAM_B2 * vb[at] + (1.0 - ADAM_B2) * (gv * gv)
                    mb[at] = nm
                    vb[at] = nv
                    wb[at] = -ADAM_LR * ((nm / bc1) / (jnp.sqrt(nv / bc2) + ADAM_EPS) + ADAM_WD * wb[at])

            pltpu.sync_copy(gb, go_hbm.at[rows, :])
            pltpu.sync_copy(wb, d_hbm.at[rows, :])
            pltpu.sync_copy(mb, nm_hbm.at[rows, :])
            pltpu.sync_copy(vb, nv_hbm.at[rows, :])

    out = jax.ShapeDtypeStruct((R, C), F32)
    buf = pltpu.VMEM((SC_ROWS, C), F32)
    return pl.kernel(body, name=name, out_type=(out, out, out, out),
                     mesh=plsc.VectorSubcoreMesh(core_axis_name="sc_core", subcore_axis_name="sc_subcore"),
                     scratch_types=[buf, buf, buf, buf])(w, g, m, v)


WEIGHTS = ['ffn1_norm', 'ffn1_w_gate', 'ffn1_w_up', 'ffn1_w_down', 'mix_norm', 'w_in', 'pool_w', 'pool_scale', 'gla_w_a2',
           'gla_b_a', 'gla_head_norm', 'w_out', 'xattn_norm', 'mem_norm', 'xattn_w_q', 'xattn_w_kv', 'xattn_w_o', 'ffn2_norm',
           'ffn2_w_gate', 'ffn2_w_up', 'ffn2_w_down', 'final_norm']
SHARDED = ['ffn1_w_gate', 'ffn1_w_up', 'ffn1_w_down', 'w_in', 'pool_w', 'gla_w_a2', 'w_out', 'xattn_w_q', 'xattn_w_kv',
           'xattn_w_o', 'ffn2_w_gate', 'ffn2_w_up', 'ffn2_w_down']
REPLICATED = [n for n in WEIGHTS if n not in SHARDED]
ON_SPARSECORE = ['ffn2_w_gate', 'ffn2_w_up', 'w_out', 'xattn_w_q', 'xattn_w_kv', 'xattn_w_o']
SMALL_COLS = 512


def _as2d(a):
    return a.reshape(-1, a.shape[-1])


def _finish_weight(name, gathered, wl):
    G, R, C = gathered.shape
    rank = wl["gla_w_a2"].shape[1]
    if name in ("w_out", "xattn_w_q", "xattn_w_o"):
        return gathered.reshape(G * R, C)
    if name == "w_in":
        w_in = jnp.transpose(gathered, (1, 0, 2)).reshape(R, G * C)
        main = G * C - rank
        return jnp.concatenate([w_in[:, :main], jnp.pad(w_in[:, main:], ((0, 0), (0, LANES - rank)))], axis=1)
    if name == "pool_w":
        NG, CJ, _ = wl[name].shape[1:]
        return jnp.transpose(gathered.reshape(G, NG, CJ, C), (1, 0, 2, 3)).reshape(NG, G * CJ, C)
    if name == "gla_w_a2":
        a2 = jnp.transpose(gathered, (1, 0, 2)).reshape(rank, G * C)
        return jnp.pad(a2, ((0, LANES - rank), (0, 0))).astype(BF16)
    return gathered


def _start_gathers(wl):
    started = {}
    token = None
    for n in SHARDED:
        whole = not n.startswith("ffn1")
        buf = _cast_to_slot(_as2d(wl[n]), BF16, f"slot_{n}", dep=token)
        send_sems, recv_sems, thru, token = _gather_start(buf, f"gather_start_{n}", whole)
        started[n] = (send_sems, recv_sems, thru, whole)
    cache = {}

    def weight(n, after=None):
        if n not in cache:
            *handles, whole = started[n]
            buf = _gather_wait(*handles, after, f"gather_wait_{n}", whole)
            if not whole:
                buf = _gather_forward(buf, f"gather_forward_{n}")
            cache[n] = _finish_weight(n, buf, wl)
        return cache[n]

    return weight, token


def _shard_major(name, gfull, wl):
    R, C = _as2d(wl[name]).shape
    if name in ("ffn1_w_gate", "ffn1_w_up", "ffn2_w_gate", "ffn2_w_up", "xattn_w_kv"):
        return gfull
    if name in ("ffn1_w_down", "ffn2_w_down", "w_out", "xattn_w_q", "xattn_w_o"):
        return gfull.reshape(N_SHARDS, R, C)
    if name == "w_in":
        return jnp.transpose(gfull[:, :N_SHARDS * C].reshape(R, N_SHARDS, C), (1, 0, 2))
    if name == "pool_w":
        NG, CJ, _ = wl[name].shape[1:]
        return jnp.transpose(gfull.reshape(NG, N_SHARDS, CJ, C), (1, 0, 2, 3)).reshape(N_SHARDS, R, C)
    assert name == "gla_w_a2"
    return jnp.transpose(gfull[:R].reshape(R, N_SHARDS, C), (1, 0, 2))


def kernel(x, mem, ffn1_norm, ffn1_w_gate, ffn1_w_up, ffn1_w_down, mix_norm, w_in, pool_w, pool_scale, gla_w_a2, gla_b_a, gla_head_norm, w_out, xattn_norm, mem_norm, xattn_w_q, xattn_w_kv, xattn_w_o, ffn2_norm, ffn2_w_gate, ffn2_w_up, ffn2_w_down, final_norm, loss_target, m_ffn1_norm, m_ffn1_w_gate, m_ffn1_w_up, m_ffn1_w_down, m_mix_norm, m_w_in, m_pool_w, m_pool_scale, m_gla_w_a2, m_gla_b_a, m_gla_head_norm, m_w_out, m_xattn_norm, m_mem_norm, m_xattn_w_q, m_xattn_w_kv, m_xattn_w_o, m_ffn2_norm, m_ffn2_w_gate, m_ffn2_w_up, m_ffn2_w_down, m_final_norm, v_ffn1_norm, v_ffn1_w_gate, v_ffn1_w_up, v_ffn1_w_down, v_mix_norm, v_w_in, v_pool_w, v_pool_scale, v_gla_w_a2, v_gla_b_a, v_gla_head_norm, v_w_out, v_xattn_norm, v_mem_norm, v_xattn_w_q, v_xattn_w_kv, v_xattn_w_o, v_ffn2_norm, v_ffn2_w_gate, v_ffn2_w_up, v_ffn2_w_down, v_final_norm):
    given = dict(locals())
    wl = {n: given[n] for n in WEIGHTS}
    ml = {n: given["m_" + n] for n in WEIGHTS}
    vl = {n: given["v_" + n] for n in WEIGHTS}

    vec = {n: wl[n].reshape(1, -1) for n in REPLICATED}
    weight, dep0 = _start_gathers(wl)
    in_flight = {}

    pair_flight = {}

    def emit_begin(n, gfull):
        *pair_flight[n], token = _pair_start(_shard_major(n, gfull, wl), f"{n}_pair_start")
        return token

    def emit_finish(n, after):
        gsm, other = _pair_wait(*pair_flight.pop(n), after, f"{n}_pair_wait")
        *in_flight[n], token = _chip_start(_pair_add(gsm, other, f"{n}_pair_add"), f"{n}_chip_start")
        return token

    grads = {}
    updates = {}

    def reduce_done(n, after):
        part, slots = _chip_wait(*in_flight.pop(n), after, f"{n}_chip_wait")
        grads[n], token = _pair_join(_chip_sum(part, slots, f"{n}_chip_sum"), f"{n}_pair_join")
        return token

    def early_update(after):
        tokens = [reduce_done(n, after) for n in ON_SPARSECORE]
        for n in ON_SPARSECORE:
            g2 = grads[n]
            updates[n] = _adamw_sc(wl[n].reshape(g2.shape), g2, ml[n].reshape(g2.shape), vl[n].reshape(g2.shape),
                                   f"adamw_sc_{n}")
        return tokens

    sq, dx0, g = _local_step(x[0], mem[0], loss_target[0], vec, weight, (emit_begin, emit_finish, early_update), dep0)
    loss = lax.psum(0.5 * jnp.sum(sq) / x.shape[-1], ("x", "y", "c"))

    for n in list(in_flight):
        reduce_done(n, dx0)
    widths = [wl[n].size for n in REPLICATED]
    total = sum(widths)
    rows = -(-total // SMALL_COLS)
    rows = -(-rows // 8) * 8
    packed = jnp.concatenate([g[n].reshape(-1) for n in REPLICATED] + [jnp.zeros((rows * SMALL_COLS - total,), F32)])
    summed = _all_reduce_small(packed.reshape(rows, SMALL_COLS), "small_all_reduce").reshape(-1)
    off = 0
    for n, width in zip(REPLICATED, widths):
        grads[n] = summed[off:off + width].reshape(1, width)
        off += width

    out_g, out_d, out_m, out_v = [], [], [], []
    for n in WEIGHTS:
        shape = wl[n].shape
        g2 = grads[n]
        if n in updates:
            go, d, nm, nv = updates[n]
        else:
            go, d, nm, nv = _adamw(wl[n].reshape(g2.shape), g2, ml[n].reshape(g2.shape), vl[n].reshape(g2.shape),
                                   f"adamw_{n}")
        out_g.append(go.reshape(shape))
        out_d.append(d.reshape(shape))
        out_m.append(nm.reshape(shape))
        out_v.append(nv.reshape(shape))
    return (loss, dx0.reshape(x.shape), *out_g, *out_d, *out_m, *out_v)
```

```python
import functools

import jax
import jax.numpy as jnp
from jax import lax
from jax.experimental import pallas as pl
from jax.experimental.pallas import tpu as pltpu
from jax.experimental.pallas import tpu_sc as plsc

F32 = jnp.float32
BF16 = jnp.bfloat16
MESH = pl.DeviceIdType.MESH

RMS_EPS = 1e-6
CHUNK = 64
POOL_WINDOWS = (2, 4, 8, 16)
POOL_HALO = 16
N_HEADS = 4
GATE_TEMP = 16.0
ADAM_LR, ADAM_B1, ADAM_B2, ADAM_EPS, ADAM_WD, ADAM_STEP = 0.001, 0.9, 0.999, 1e-08, 0.01, 10
N_SHARDS = 4
LANES = 128
MXU_COLS = 256
VMEM_LIMIT = 58 * 1024 * 1024

ANY = pl.BlockSpec(memory_space=pl.ANY)
HBM = pl.BlockSpec(memory_space=pltpu.HBM)
SEM = pl.BlockSpec(memory_space=pltpu.SEMAPHORE)
EFFECT = pltpu.SideEffectType.DATAFLOW_SIDE_EFFECTING


def _params(**kw):
    return pltpu.CompilerParams(vmem_limit_bytes=VMEM_LIMIT, **kw)


def _tile(n, want):
    for unit in (LANES, 8):
        t = (min(want, n) // unit) * unit
        while t >= unit:
            if n % t == 0:
                return t
            t -= unit
    return n


def _dot(a, b, dims):
    return lax.dot_general(a, b, (dims, ((), ())), preferred_element_type=F32)


def _nn(a, b):
    return _dot(a, b, ((1,), (0,)))


def _nt(a, b):
    return _dot(a, b, ((1,), (1,)))


def _tn(a, b):
    return _dot(a, b, ((0,), (0,)))


def _sigmoid(x):
    return 1.0 / (1.0 + jnp.exp(-x))


def _matmul_cost(M, N, K, operands, out_shape):
    nbytes = sum(a.size * a.dtype.itemsize for a in operands) + out_shape.size * out_shape.dtype.itemsize
    return pl.CostEstimate(flops=2 * M * N * K, transcendentals=0, bytes_accessed=nbytes)


def _matmul(a, b, *, mode, name, out_dtype, tm=512, tn=2048, tk=2048, res=None, scale=1.0, b_groups=False, out_groups=0,
            dep=()):
    if mode == "tn":
        K, M = a.shape
    else:
        M, K = a.shape
    if mode == "nn":
        if b_groups:
            G, _, Nj = b.shape
            N = G * Nj
        else:
            N = b.shape[1]
    elif mode == "nt":
        if b_groups:
            G, N, Kj = b.shape
            assert G * Kj == K
        else:
            N = b.shape[0]
    else:
        N = b.shape[1]
    tm = _tile(M, tm)
    if mode == "nn" and b_groups:
        tn = _tile(Nj, tn)
    elif out_groups:
        tn = _tile(N // out_groups, tn)
    else:
        tn = _tile(N, tn)
    if mode == "nt" and b_groups:
        tk = _tile(Kj, tk)
    else:
        tk = _tile(K, tk)
    nk = K // tk
    grid = (M // tm, N // tn, nk)

    if mode == "tn":
        a_spec = pl.BlockSpec((tk, tm), lambda i, j, k: (k, i))
        b_spec = pl.BlockSpec((tk, tn), lambda i, j, k: (k, j))
        dims = ((0,), (0,))
    elif mode == "nn":
        a_spec = pl.BlockSpec((tm, tk), lambda i, j, k: (i, k))
        if b_groups:
            npj = Nj // tn
            b_spec = pl.BlockSpec((None, tk, tn), lambda i, j, k: (j // npj, k, j % npj))
        else:
            b_spec = pl.BlockSpec((tk, tn), lambda i, j, k: (k, j))
        dims = ((1,), (0,))
    else:
        a_spec = pl.BlockSpec((tm, tk), lambda i, j, k: (i, k))
        if b_groups:
            kpj = Kj // tk
            b_spec = pl.BlockSpec((None, tn, tk), lambda i, j, k: (k // kpj, j, k % kpj))
        else:
            b_spec = pl.BlockSpec((tn, tk), lambda i, j, k: (j, k))
        dims = ((1,), (1,))
    if out_groups:
        npj = (N // out_groups) // tn
        o_spec = pl.BlockSpec((None, tm, tn), lambda i, j, k: (j // npj, i, j % npj))
        out_shape = jax.ShapeDtypeStruct((out_groups, M, N // out_groups), out_dtype)
    else:
        o_spec = pl.BlockSpec((tm, tn), lambda i, j, k: (i, j))
        out_shape = jax.ShapeDtypeStruct((M, N), out_dtype)
    in_specs = [a_spec, b_spec]
    operands = [a, b]
    if res is not None:
        in_specs.append(pl.BlockSpec((tm, tn), lambda i, j, k: (i, j)))
        operands.append(res)
    has_res = res is not None
    n_dep = len(dep)
    for d in dep:
        in_specs.append(pl.BlockSpec(d.shape, lambda i, j, k: (0, 0)))
        operands.append(d)

    def body(*refs):
        if has_res:
            a_ref, b_ref, r_ref = refs[:3]
        else:
            a_ref, b_ref = refs[:2]
            r_ref = None
        o_ref = refs[2 + has_res + n_dep]

        def finish(acc):
            if scale != 1.0:
                acc = acc * scale
            if r_ref is not None:
                acc = r_ref[...] + acc
            o_ref[...] = acc.astype(o_ref.dtype)

        part = _dot(a_ref[...], b_ref[...], dims)
        if nk == 1:
            finish(part)
        else:
            acc_ref = o_ref if in_place else refs[-1]
            k = pl.program_id(2)

            @pl.when(k == 0)
            def _():
                acc_ref[...] = part

            @pl.when(k > 0)
            def _():
                acc_ref[...] += part

            if not in_place:
                @pl.when(k == nk - 1)
                def _():
                    finish(acc_ref[...])

    in_place = out_dtype == F32 and res is None and scale == 1.0
    scratch = [] if nk == 1 or in_place else [pltpu.VMEM((tm, tn), F32)]
    return pl.pallas_call(body, name=name, grid=grid, in_specs=in_specs, out_specs=o_spec, out_shape=out_shape,
                          scratch_shapes=scratch, compiler_params=_params(),
                          cost_estimate=_matmul_cost(M, N, K, operands, out_shape))(*operands)


def _rms_fwd(x, gain, name, tm=256, dep=None):
    S, D = x.shape
    tm = _tile(S, tm)

    def body(x_ref, g_ref, *rest):
        o_ref = rest[-1]
        xv = x_ref[...]
        r = lax.rsqrt(jnp.mean(xv * xv, axis=-1, keepdims=True) + RMS_EPS)
        o_ref[...] = (xv * r * g_ref[...]).astype(o_ref.dtype)

    in_specs = [pl.BlockSpec((tm, D), lambda i: (i, 0)), pl.BlockSpec((1, D), lambda i: (0, 0))]
    operands = [x, gain]
    if dep is not None:
        in_specs.append(pl.BlockSpec(dep.shape, lambda i: (0, 0)))
        operands.append(dep)
    return pl.pallas_call(body, name=name, grid=(S // tm,), in_specs=in_specs,
                          out_specs=pl.BlockSpec((tm, D), lambda i: (i, 0)),
                          out_shape=jax.ShapeDtypeStruct((S, D), BF16), compiler_params=_params())(*operands)


def _rms_bwd(x, gain, dh, dres, name, lowp=None, tm=256):
    half = lowp is not None
    S, D = x.shape
    tm = _tile(S, tm)
    has_res = dres is not None

    def body(*refs):
        if has_res:
            x_ref, g_ref, dh_ref, dr_ref = refs[:4]
            outs = refs[4:]
        else:
            x_ref, g_ref, dh_ref = refs[:3]
            dr_ref = None
            outs = refs[3:]
        dx_ref, dg_ref = outs[0], outs[-1]
        xv = x_ref[...]
        dhv = dh_ref[...].astype(F32)
        r = lax.rsqrt(jnp.mean(xv * xv, axis=-1, keepdims=True) + RMS_EPS)
        gy = dhv * g_ref[...]
        dx = r * gy - xv * (r * r * r) * jnp.mean(gy * xv, axis=-1, keepdims=True)
        if dr_ref is not None:
            dx = dx + dr_ref[...]
        dx_ref[...] = dx
        if half:
            outs[1][...] = (dx if lowp == 1.0 else lowp * dx).astype(BF16)
        part = jnp.sum(dhv * xv * r, axis=0, keepdims=True)

        @pl.when(pl.program_id(0) == 0)
        def _():
            dg_ref[...] = part

        @pl.when(pl.program_id(0) > 0)
        def _():
            dg_ref[...] += part

    row = pl.BlockSpec((tm, D), lambda i: (i, 0))
    vec = pl.BlockSpec((1, D), lambda i: (0, 0))
    in_specs = [row, vec, row] + ([row] if has_res else [])
    operands = [x, gain, dh] + ([dres] if has_res else [])
    out_specs = [row] + ([row] if half else []) + [vec]
    out_shape = [jax.ShapeDtypeStruct((S, D), F32)] + ([jax.ShapeDtypeStruct((S, D), BF16)] if half else []) + [
        jax.ShapeDtypeStruct((1, D), F32)]
    return pl.pallas_call(body, name=name, grid=(S // tm,), in_specs=in_specs, out_specs=out_specs, out_shape=out_shape,
                          compiler_params=_params())(*operands)


def _loss_head(x, gain, target, name, tm=256):
    S, D = x.shape
    tm = _tile(S, tm)

    def body(x_ref, g_ref, t_ref, sq_ref, dx_ref, dxh_ref, dg_ref):
        xv = x_ref[...]
        r = lax.rsqrt(jnp.mean(xv * xv, axis=-1, keepdims=True) + RMS_EPS)
        xn = xv * r
        err = xn * g_ref[...] - t_ref[...]
        dout = err * (1.0 / D)
        gy = dout * g_ref[...]
        dx = r * gy - xv * (r * r * r) * jnp.mean(gy * xv, axis=-1, keepdims=True)
        dx_ref[...] = dx
        dxh_ref[...] = (0.5 * dx).astype(BF16)
        sq = jnp.sum(err * err, axis=0, keepdims=True)
        dg = jnp.sum(dout * xn, axis=0, keepdims=True)

        @pl.when(pl.program_id(0) == 0)
        def _():
            sq_ref[...] = sq
            dg_ref[...] = dg

        @pl.when(pl.program_id(0) > 0)
        def _():
            sq_ref[...] += sq
            dg_ref[...] += dg

    row = pl.BlockSpec((tm, D), lambda i: (i, 0))
    vec = pl.BlockSpec((1, D), lambda i: (0, 0))
    return pl.pallas_call(body, name=name, grid=(S // tm,), in_specs=[row, vec, row], out_specs=[vec, row, row, vec],
                          out_shape=[jax.ShapeDtypeStruct((1, D), F32), jax.ShapeDtypeStruct((S, D), F32),
                                     jax.ShapeDtypeStruct((S, D), BF16), jax.ShapeDtypeStruct((1, D), F32)],
                          compiler_params=_params())(x, gain, target)


def _cast(x, dtype, name, scale=1.0, tm=256):
    S, D = x.shape
    tm = _tile(S, tm)

    def body(x_ref, o_ref):
        o_ref[...] = (x_ref[...] * scale).astype(o_ref.dtype)

    row = pl.BlockSpec((tm, D), lambda i: (i, 0))
    return pl.pallas_call(body, name=name, grid=(S // tm,), in_specs=[row], out_specs=row,
                          out_shape=jax.ShapeDtypeStruct((S, D), dtype), compiler_params=_params())(x)


def _ffn_up(h, wg, wu, name, tm=512):
    S, D = h.shape
    G, _, Fj = wg.shape
    tm = _tile(S, tm)

    def body(h_ref, wg_ref, wu_ref, ga_ref, gb_ref, hid_ref):
        hv = h_ref[...]
        a = _nn(hv, wg_ref[...])
        b = _nn(hv, wu_ref[...])
        s = _sigmoid(a)
        silu = a * s
        ga_ref[...] = (b * (s * (1.0 + a * (1.0 - s)))).astype(BF16)
        gb_ref[...] = silu.astype(BF16)
        hid_ref[...] = (silu * b).astype(BF16)

    w_spec = pl.BlockSpec((None, D, Fj), lambda g, i: (g, 0, 0))
    o_spec = pl.BlockSpec((tm, Fj), lambda g, i: (i, g))
    out = jax.ShapeDtypeStruct((S, G * Fj), BF16)
    return pl.pallas_call(body, name=name, grid=(G, S // tm),
                          in_specs=[pl.BlockSpec((tm, D), lambda g, i: (i, 0)), w_spec, w_spec],
                          out_specs=[o_spec, o_spec, o_spec], out_shape=[out, out, out], compiler_params=_params())(h, wg, wu)


def _ffn_dact(dxh, wd, ga, gb, name, tm=512):
    S, D = dxh.shape
    G, Fj, _ = wd.shape
    tm = _tile(S, tm)

    def body(dx_ref, wd_ref, ga_ref, gb_ref, da_ref, db_ref):
        dhid = _nt(dx_ref[...], wd_ref[...])
        da_ref[...] = (dhid * ga_ref[...].astype(F32)).astype(BF16)
        db_ref[...] = (dhid * gb_ref[...].astype(F32)).astype(BF16)

    blk = pl.BlockSpec((tm, Fj), lambda g, i: (i, g))
    out = jax.ShapeDtypeStruct((S, G * Fj), BF16)
    return pl.pallas_call(body, name=name, grid=(G, S // tm),
                          in_specs=[pl.BlockSpec((tm, D), lambda g, i: (i, 0)),
                                    pl.BlockSpec((None, Fj, D), lambda g, i: (g, 0, 0)), blk, blk],
                          out_specs=[blk, blk], out_shape=[out, out], compiler_params=_params())(dxh, wd, ga, gb)


def _ffn_dh(da, db, wg, wu, name, dep=(), tm=512):
    S = da.shape[0]
    G, D, Fj = wg.shape
    tm = _tile(S, tm)

    def body(da_ref, db_ref, wg_ref, wu_ref, *rest):
        o_ref = rest[-1]
        part = _nt(da_ref[...], wg_ref[...]) + _nt(db_ref[...], wu_ref[...])

        @pl.when(pl.program_id(1) == 0)
        def _():
            o_ref[...] = part

        @pl.when(pl.program_id(1) > 0)
        def _():
            o_ref[...] += part

    act = pl.BlockSpec((tm, Fj), lambda i, g: (i, g))
    w_spec = pl.BlockSpec((None, D, Fj), lambda i, g: (g, 0, 0))
    in_specs = [act, act, w_spec, w_spec] + [pl.BlockSpec(d.shape, lambda i, g: (0, 0)) for d in dep]
    return pl.pallas_call(body, name=name, grid=(S // tm, G), in_specs=in_specs,
                          out_specs=pl.BlockSpec((tm, D), lambda i, g: (i, 0)),
                          out_shape=jax.ShapeDtypeStruct((S, D), F32), compiler_params=_params(),
                          cost_estimate=_matmul_cost(S, D, 2 * G * Fj, (da, db, wg, wu), jax.ShapeDtypeStruct((S, D), F32)),
                          )(da, db, wg, wu, *dep)


def _pool_fwd(proj, pool_w, pool_scale, name, tm=512):
    S = proj.shape[0]
    NG, C, _ = pool_w.shape
    DP = NG * C
    tm = _tile(S, tm)
    hb = tm // POOL_HALO
    n_ext = tm + POOL_HALO

    def body(u_ref, halo_ref, w_ref, sc_ref, y_ref, d_ref):
        i = pl.program_id(0)
        t = lax.broadcasted_iota(jnp.int32, (tm, 1), 0) + i * tm
        for g, win in enumerate(POOL_WINDOWS):
            cols = slice(g * C, (g + 1) * C)
            ug = u_ref[:, cols]
            halo = jnp.where(i > 0, halo_ref[:, cols], 0.0)
            acc = jnp.concatenate([halo, ug], axis=0)
            step = 1
            while step < win:
                acc = acc + pltpu.roll(acc, step, 0)
                step *= 2
            count = jnp.minimum(t + 1, win).astype(F32)
            d = (acc[POOL_HALO:, :] / count - ug).astype(BF16)
            d_ref[:, cols] = d
            y_ref[:, cols] = (_nn(d, w_ref[g]) * sc_ref[:, cols]).astype(BF16)

    del n_ext
    return pl.pallas_call(
        body, name=name, grid=(S // tm,),
        in_specs=[pl.BlockSpec((tm, DP), lambda i: (i, 0)),
                  pl.BlockSpec((POOL_HALO, DP), lambda i: (jnp.maximum(i * hb - 1, 0), 0)),
                  pl.BlockSpec((NG, C, C), lambda i: (0, 0, 0)), pl.BlockSpec((1, DP), lambda i: (0, 0))],
        out_specs=[pl.BlockSpec((tm, DP), lambda i: (i, 0)), pl.BlockSpec((tm, DP), lambda i: (i, 0))],
        out_shape=[jax.ShapeDtypeStruct((S, DP), BF16), jax.ShapeDtypeStruct((S, DP), BF16)],
        compiler_params=_params())(proj, proj, pool_w, pool_scale)


def _pool_bwd(dymix, d, pool_w, pool_scale, name, tm=512):
    S = dymix.shape[0]
    NG, C, _ = pool_w.shape
    DP = NG * C
    tm = _tile(S, tm)
    hb = tm // POOL_HALO
    nb = S // tm
    n_ext = tm + POOL_HALO
    last_halo = S // POOL_HALO - 1

    def body(dy_ref, halo_ref, d_ref, w_ref, sc_ref, du_ref, dw_ref, dsc_ref):
        i = pl.program_id(0)
        t = lax.broadcasted_iota(jnp.int32, (n_ext, 1), 0) + i * tm
        for g, win in enumerate(POOL_WINDOWS):
            cols = slice(g * C, (g + 1) * C)
            dy = dy_ref[:, cols]
            halo = jnp.where(i < nb - 1, halo_ref[:, cols], 0.0)
            sc = sc_ref[:, cols]
            dv = d_ref[:, cols]
            e_ext = (jnp.concatenate([dy, halo], axis=0) * sc).astype(BF16)
            dd = _nt(e_ext, w_ref[g])
            count = jnp.minimum(t + 1, win).astype(F32)
            acc = dd / count
            step = 1
            while step < win:
                acc = acc + pltpu.roll(acc, n_ext - step, 0)
                step *= 2
            du_ref[:, cols] = (acc[:tm, :] - dd[:tm, :]).astype(BF16)
            dw = _tn(dv, e_ext[:tm, :])
            dsc = jnp.sum(dy * _nn(dv, w_ref[g]), axis=0, keepdims=True)

            @pl.when(i == 0)
            def _():
                dw_ref[g] = dw
                dsc_ref[:, cols] = dsc

            @pl.when(i > 0)
            def _():
                dw_ref[g] += dw
                dsc_ref[:, cols] += dsc

    return pl.pallas_call(
        body, name=name, grid=(nb,),
        in_specs=[pl.BlockSpec((tm, DP), lambda i: (i, 0)),
                  pl.BlockSpec((POOL_HALO, DP), lambda i: (jnp.minimum((i + 1) * hb, last_halo), 0)),
                  pl.BlockSpec((tm, DP), lambda i: (i, 0)),
                  pl.BlockSpec((NG, C, C), lambda i: (0, 0, 0)), pl.BlockSpec((1, DP), lambda i: (0, 0))],
        out_specs=[pl.BlockSpec((tm, DP), lambda i: (i, 0)), pl.BlockSpec((NG, C, C), lambda i: (0, 0, 0)),
                   pl.BlockSpec((1, DP), lambda i: (0, 0))],
        out_shape=[jax.ShapeDtypeStruct((S, DP), BF16), jax.ShapeDtypeStruct((NG, C, C), F32),
                   jax.ShapeDtypeStruct((1, DP), F32)],
        compiler_params=_params())(dymix, dymix, d, pool_w, pool_scale)


def _chunk_scan(v, rows, reverse):
    n = v.shape[0]
    step = 1
    while step < CHUNK:
        if reverse:
            v = v + jnp.where(rows < CHUNK - step, pltpu.roll(v, n - step, 0), 0.0)
        else:
            v = v + jnp.where(rows >= step, pltpu.roll(v, step, 0), 0.0)
        step *= 2
    return v


def _log_decay(alr, w_a2, b_a):
    z = _nn(alr.astype(BF16), w_a2) + b_a
    la = (jnp.minimum(z, 0.0) - jnp.log(1.0 + jnp.exp(-jnp.abs(z)))) * (1.0 / GATE_TEMP)
    return z, la


def _gla_specs(DP, DKT, DV, tb, bmap):
    return [pl.BlockSpec((tb, DKT), lambda i: (bmap(i), DP // DKT)),
            pl.BlockSpec((tb, DKT), lambda i: (bmap(i), DP // DKT + 1)),
            pl.BlockSpec((tb, DV), lambda i: (bmap(i), (DP + 2 * DKT) // DV)),
            pl.BlockSpec((tb, DV), lambda i: (bmap(i), (DP + 2 * DKT) // DV + 1)),
            pl.BlockSpec((tb, LANES), lambda i: (bmap(i), (DP + 2 * DKT + 2 * DV) // LANES))]


def _gla_fwd(proj, y_pool, w_a2, b_a, head_norm, name, tb=512):
    S = proj.shape[0]
    DP = y_pool.shape[1]
    DKT = b_a.shape[1]
    DV = head_norm.shape[1]
    dk, dv = DKT // N_HEADS, DV // N_HEADS
    tb = _tile(S, tb)
    ncb = tb // CHUNK
    qscale = dk ** -0.5

    def body(q_ref, k_ref, v_ref, g_ref, alr_ref, yp_ref, wa_ref, ba_ref, hn_ref, y_ref, st_out_ref, st_ref, kdec_ref,
             gam_ref):
        @pl.when(pl.program_id(0) == 0)
        def _():
            st_ref[...] = jnp.zeros_like(st_ref)

        y_ref[:, :DP] = yp_ref[...]

        rows = lax.broadcasted_iota(jnp.int32, (tb, 1), 0) % CHUNK
        _, la = _log_decay(alr_ref[...], wa_ref[...], ba_ref[...])
        tail = _chunk_scan(la, rows, True)
        kdec_ref[...] = k_ref[...] * jnp.exp(tail - la)
        gam_ref[...] = jnp.exp(tail)

        def chunk(c, carry):
            r0 = pl.multiple_of(c * CHUNK, CHUNK)
            rs = pl.ds(r0, CHUNK)
            gam = gam_ref[pl.ds(r0, 1), :]
            heads = range(N_HEADS)
            kcs = [slice(h * dk, (h + 1) * dk) for h in heads]
            vcs = [slice(h * dv, (h + 1) * dv) for h in heads]
            upd = [_tn(v_ref[rs, vcs[h]].astype(BF16), kdec_ref[rs, kcs[h]].astype(BF16)) for h in heads]
            st = [st_ref[h] * gam[:, kcs[h]] + upd[h] for h in heads]
            o = [_nt((q_ref[rs, kcs[h]] * qscale).astype(BF16), st[h].astype(BF16)) for h in heads]
            for h in heads:
                st_ref[h] = st[h]
                st_out_ref[c, h] = st[h]
                r = lax.rsqrt(jnp.mean(o[h] * o[h], axis=-1, keepdims=True) + RMS_EPS)
                gv = g_ref[rs, vcs[h]]
                y_ref[rs, DP + h * dv:DP + (h + 1) * dv] = (o[h] * r * hn_ref[:, vcs[h]] * (gv * _sigmoid(gv))).astype(BF16)
            return carry

        lax.fori_loop(0, ncb, chunk, 0, unroll=2)

    full = lambda shape: pl.BlockSpec(shape, lambda i: (0,) * len(shape))
    return pl.pallas_call(
        body, name=name, grid=(S // tb,),
        in_specs=_gla_specs(DP, DKT, DV, tb, lambda i: i) + [pl.BlockSpec((tb, DP), lambda i: (i, 0)),
                                                            full((LANES, DKT)), full((1, DKT)), full((1, DV))],
        out_specs=[pl.BlockSpec((tb, DP + DV), lambda i: (i, 0)),
                   pl.BlockSpec((ncb, N_HEADS, dv, dk), lambda i: (i, 0, 0, 0))],
        out_shape=[jax.ShapeDtypeStruct((S, DP + DV), BF16), jax.ShapeDtypeStruct((S // CHUNK, N_HEADS, dv, dk), F32)],
        scratch_shapes=[pltpu.VMEM((N_HEADS, dv, dk), F32), pltpu.VMEM((tb, DKT), F32), pltpu.VMEM((tb, DKT), F32)],
        compiler_params=_params())(proj, proj, proj, proj, proj, y_pool, w_a2, b_a, head_norm)


def _gla_bwd(proj, states, dymix, du, w_a2, b_a, head_norm, name, tb=512):
    S = proj.shape[0]
    DP = du.shape[1]
    DKT = b_a.shape[1]
    DV = head_norm.shape[1]
    dk, dv = DKT // N_HEADS, DV // N_HEADS
    tb = _tile(S, tb)
    ncb = tb // CHUNK
    nb = S // tb
    qscale = dk ** -0.5
    rev = lambda i: nb - 1 - i

    q0, k0, v0, g0, a0 = DP, DP + DKT, DP + 2 * DKT, DP + 2 * DKT + DV, DP + 2 * DKT + 2 * DV

    def body(q_ref, k_ref, v_ref, g_ref, alr_ref, st_blk_ref, st_prev_ref, dy_ref, du_ref, wa_ref, ba_ref, hn_ref,
             dp_ref, dwa_ref, dba_ref, dhn_ref,
             dst_ref, kdec_ref, dec_ref, gam_ref, e_ref, dla_ref, dhn_acc_ref):
        i = pl.program_id(0)
        blk = rev(i)
        dp_ref[:, :DP] = du_ref[...]

        @pl.when(i == 0)
        def _():
            dst_ref[...] = jnp.zeros_like(dst_ref)

        dhn_acc_ref[...] = jnp.zeros_like(dhn_acc_ref)
        rows = lax.broadcasted_iota(jnp.int32, (tb, 1), 0) % CHUNK
        z, la = _log_decay(alr_ref[...], wa_ref[...], ba_ref[...])
        tail = _chunk_scan(la, rows, True)
        dec_ref[...] = jnp.exp(tail - la)
        kdec_ref[...] = k_ref[...] * dec_ref[...]
        gam_ref[...] = jnp.exp(tail)

        def chunk(cc, carry):
            c = ncb - 1 - cc
            r0 = pl.multiple_of(c * CHUNK, CHUNK)
            rs = pl.ds(r0, CHUNK)
            gam = gam_ref[pl.ds(r0, 1), :]
            first = jnp.logical_and(blk == 0, c == 0)
            heads = range(N_HEADS)
            kcs = [slice(h * dk, (h + 1) * dk) for h in heads]
            vcs = [slice(h * dv, (h + 1) * dv) for h in heads]
            qs = [(q_ref[rs, kcs[h]] * qscale).astype(BF16) for h in heads]
            stb = [st_blk_ref[c, h].astype(BF16) for h in heads]
            o = [_nt(qs[h], stb[h]) for h in heads]
            do = []
            for h in heads:
                oh = o[h]
                r = lax.rsqrt(jnp.mean(oh * oh, axis=-1, keepdims=True) + RMS_EPS)
                gv = g_ref[rs, vcs[h]]
                sg = _sigmoid(gv)
                dy = dy_ref[rs, vcs[h]]
                hn = hn_ref[:, vcs[h]]
                on = oh * r
                dp_ref[rs, g0 + h * dv:g0 + (h + 1) * dv] = (dy * on * hn * (sg * (1.0 + gv * (1.0 - sg)))).astype(BF16)
                don = dy * (gv * sg)
                dhn_acc_ref[:, vcs[h]] += jnp.sum(don * on, axis=0, keepdims=True)
                dn = don * hn
                do.append((r * dn - oh * (r * r * r) * jnp.mean(dn * oh, axis=-1, keepdims=True)).astype(BF16))
            dqs = [_nn(do[h], stb[h]) for h in heads]
            dst = [dst_ref[h] + _tn(do[h], qs[h]) for h in heads]
            for h in heads:
                dp_ref[rs, q0 + h * dk:q0 + (h + 1) * dk] = (dqs[h] * qscale).astype(BF16)
            dstb = [dst[h].astype(BF16) for h in heads]
            dvh = [_nt(kdec_ref[rs, kcs[h]].astype(BF16), dstb[h]) for h in heads]
            dkdec = [_nn(v_ref[rs, vcs[h]].astype(BF16), dstb[h]) for h in heads]
            gdg = []
            for h in heads:
                dp_ref[rs, v0 + h * dv:v0 + (h + 1) * dv] = dvh[h].astype(BF16)
                dp_ref[rs, k0 + h * dk:k0 + (h + 1) * dk] = (dkdec[h] * dec_ref[rs, kcs[h]]).astype(BF16)
                e_ref[rs, kcs[h]] = dkdec[h] * kdec_ref[rs, kcs[h]]
                st_prev = jnp.where(c > 0, st_blk_ref[jnp.maximum(c - 1, 0), h], st_prev_ref[0, h])
                st_prev = jnp.where(first, 0.0, st_prev)
                gdg.append(jnp.sum(dst[h] * st_prev, axis=0, keepdims=True) * gam[:, kcs[h]])
                dst_ref[h] = dst[h] * gam[:, kcs[h]]
            dla_ref[rs, :] = jnp.broadcast_to(jnp.concatenate(gdg, axis=1), (CHUNK, DKT))
            return carry

        lax.fori_loop(0, ncb, chunk, 0, unroll=2)

        ev = e_ref[...]
        dla = dla_ref[...] + _chunk_scan(ev, rows, False) - ev
        dz = dla * (1.0 / GATE_TEMP) * (1.0 - _sigmoid(z))
        dzb = dz.astype(BF16)
        dp_ref[:, a0:a0 + LANES] = _nt(dzb, wa_ref[...]).astype(BF16)
        dwa = _tn(alr_ref[...].astype(BF16), dzb)
        dba = jnp.sum(dz, axis=0, keepdims=True)

        @pl.when(i == 0)
        def _():
            dwa_ref[...] = dwa
            dba_ref[...] = dba
            dhn_ref[...] = dhn_acc_ref[...]

        @pl.when(i > 0)
        def _():
            dwa_ref[...] += dwa
            dba_ref[...] += dba
            dhn_ref[...] += dhn_acc_ref[...]

    full = lambda shape: pl.BlockSpec(shape, lambda i: (0,) * len(shape))
    rowblk = lambda w: pl.BlockSpec((tb, w), lambda i: (rev(i), 0))
    return pl.pallas_call(
        body, name=name, grid=(nb,),
        in_specs=_gla_specs(DP, DKT, DV, tb, rev) + [
            pl.BlockSpec((ncb, N_HEADS, dv, dk), lambda i: (rev(i), 0, 0, 0)),
            pl.BlockSpec((1, N_HEADS, dv, dk), lambda i: (jnp.maximum(rev(i) * ncb - 1, 0), 0, 0, 0)),
            pl.BlockSpec((tb, DV), lambda i: (rev(i), DP // DV)), rowblk(DP),
            full((LANES, DKT)), full((1, DKT)), full((1, DV))],
        out_specs=[rowblk(a0 + LANES), full((LANES, DKT)), full((1, DKT)), full((1, DV))],
        out_shape=[jax.ShapeDtypeStruct((S, a0 + LANES), BF16), jax.ShapeDtypeStruct((LANES, DKT), F32),
                   jax.ShapeDtypeStruct((1, DKT), F32), jax.ShapeDtypeStruct((1, DV), F32)],
        scratch_shapes=[pltpu.VMEM((N_HEADS, dv, dk), F32)] + [pltpu.VMEM((tb, DKT), F32)] * 5 + [pltpu.VMEM((1, DV), F32)],
        compiler_params=_params())(proj, proj, proj, proj, proj, states, states, dymix, du, w_a2, b_a, head_norm)


def _xattn_fwd(q, kv, name, tm=512):
    S, D = q.shape
    M = kv.shape[0]
    hd = D // N_HEADS
    tm = _tile(S, tm)
    scale = hd ** -0.5

    def body(q_ref, k_ref, v_ref, o_ref):
        for h in range(N_HEADS):
            hc = slice(h * hd, (h + 1) * hd)
            s = _nt(q_ref[:, hc], k_ref[:, hc]) * scale
            p = jnp.exp(s - jnp.max(s, axis=-1, keepdims=True))
            p = p / jnp.sum(p, axis=-1, keepdims=True)
            o_ref[:, hc] = _nn(p.astype(BF16), v_ref[:, hc]).astype(BF16)

    return pl.pallas_call(body, name=name, grid=(S // tm,),
                          in_specs=[pl.BlockSpec((tm, D), lambda i: (i, 0)), pl.BlockSpec((M, D), lambda i: (0, 0)),
                                    pl.BlockSpec((M, D), lambda i: (0, 1))],
                          out_specs=pl.BlockSpec((tm, D), lambda i: (i, 0)),
                          out_shape=jax.ShapeDtypeStruct((S, D), BF16), compiler_params=_params())(q, kv, kv)


def _xattn_bwd(q, kv, do, name, tm=512):
    S, D = q.shape
    M = kv.shape[0]
    hd = D // N_HEADS
    tm = _tile(S, tm)
    scale = hd ** -0.5

    def body(q_ref, k_ref, v_ref, do_ref, dq_ref, dkv_ref):
        first = pl.program_id(0) == 0
        for h in range(N_HEADS):
            hc = slice(h * hd, (h + 1) * hd)
            vcols = slice(D + h * hd, D + (h + 1) * hd)
            qh = q_ref[:, hc]
            kh = k_ref[:, hc]
            doh = do_ref[:, hc]
            s = _nt(qh, kh) * scale
            p = jnp.exp(s - jnp.max(s, axis=-1, keepdims=True))
            p = p / jnp.sum(p, axis=-1, keepdims=True)
            dvh = _tn(p.astype(BF16), doh)
            dp = _nt(doh, v_ref[:, hc])
            ds = ((p * (dp - jnp.sum(dp * p, axis=-1, keepdims=True))) * scale).astype(BF16)
            dq_ref[:, hc] = _nn(ds, kh).astype(BF16)
            dkh = _tn(ds, qh)

            @pl.when(first)
            def _():
                dkv_ref[:, hc] = dkh
                dkv_ref[:, vcols] = dvh

            @pl.when(jnp.logical_not(first))
            def _():
                dkv_ref[:, hc] += dkh
                dkv_ref[:, vcols] += dvh

    row = pl.BlockSpec((tm, D), lambda i: (i, 0))
    return pl.pallas_call(body, name=name, grid=(S // tm,),
                          in_specs=[row, pl.BlockSpec((M, D), lambda i: (0, 0)), pl.BlockSpec((M, D), lambda i: (0, 1)), row],
                          out_specs=[row, pl.BlockSpec((M, 2 * D), lambda i: (0, 0))],
                          out_shape=[jax.ShapeDtypeStruct((S, D), BF16), jax.ShapeDtypeStruct((M, 2 * D), F32)],
                          compiler_params=_params())(q, kv, kv, do)


def _local_step(x, mem, target, vec, weight, emit, dep0):
    DP = vec["pool_scale"].shape[1]
    g = {}
    pending = []
    begun = []
    emit_begin, emit_finish, early_update = emit

    def behind(fn, *a, **kw):
        dep = tuple(pending)
        pending.clear()
        out = fn(*a, dep=dep, **kw)
        while begun:
            pending.append(emit_finish(begun.pop(0), out))
        return out

    def mm(a, b, **kw):
        return behind(_matmul, a, b, **kw)

    def send(name, gfull):
        pending.append(emit_begin(name, gfull))
        begun.append(name)

    def ffn_fwd(xin, tag, dep):
        h = _rms_fwd(xin, vec[f"{tag}_norm"], f"{tag}_norm", dep=dep)
        ga, gb, hid = _ffn_up(h, weight(f"{tag}_w_gate", h), weight(f"{tag}_w_up", h), f"{tag}_up")
        wd = weight(f"{tag}_w_down", hid)
        G, Fj, D = wd.shape
        xo = _matmul(hid, wd.reshape(G * Fj, D), mode="nn", name=f"{tag}_down", out_dtype=F32, res=xin, scale=0.5,
                     tn=1024, tk=G * Fj)
        return xo, (h, ga, gb, hid)

    def ffn_bwd(dxh, saved, tag, last):
        h, ga, gb, hid = saved
        wg, wu, wd = weight(f"{tag}_w_gate"), weight(f"{tag}_w_up"), weight(f"{tag}_w_down")
        G, Fj, D = wd.shape
        send(f"{tag}_w_down", mm(hid, dxh, mode="tn", name=f"{tag}_dwd", out_dtype=F32, tm=Fj, tn=1024))
        da, db = _ffn_dact(dxh, wd, ga, gb, f"{tag}_dact")
        send(f"{tag}_w_gate", mm(h, da, mode="tn", name=f"{tag}_dwg", out_dtype=F32, tm=1024, tn=Fj, out_groups=G))
        dwu = mm(h, db, mode="tn", name=f"{tag}_dwu", out_dtype=F32, tm=1024, tn=Fj, out_groups=G)
        if last:
            pending.append(emit_begin(f"{tag}_w_up", dwu))
            pending.append(emit_finish(f"{tag}_w_up", pending[-1]))
        else:
            send(f"{tag}_w_up", dwu)
        return behind(_ffn_dh, da, db, wg, wu, f"{tag}_dh")

    x1, ffn1_saved = ffn_fwd(x, "ffn1", dep0)
    h2 = _rms_fwd(x1, vec["mix_norm"], "mix_norm")
    w_in = weight("w_in", h2)
    proj = _matmul(h2, w_in, mode="nn", name="w_in", out_dtype=F32, tn=1408)
    pool_w, w_a2 = weight("pool_w", h2), weight("gla_w_a2", h2)
    y_pool, dpool = _pool_fwd(proj, pool_w, vec["pool_scale"], "pool_fwd")
    ymix, states = _gla_fwd(proj, y_pool, w_a2, vec["gla_b_a"], vec["gla_head_norm"], "gla_fwd")
    w_out = weight("w_out", ymix)
    x2 = _matmul(ymix, w_out, mode="nn", name="w_out", out_dtype=F32, res=x1)
    h3 = _rms_fwd(x2, vec["xattn_norm"], "xattn_norm")
    mh = _rms_fwd(mem, vec["mem_norm"], "mem_norm")
    w_q = weight("xattn_w_q", h3)
    q = _matmul(h3, w_q, mode="nn", name="xattn_q", out_dtype=BF16)
    w_kv = weight("xattn_w_kv", q)
    kv = _matmul(mh, w_kv, mode="nn", name="xattn_kv", out_dtype=BF16, b_groups=True, tn=1024)
    o = _xattn_fwd(q, kv, "xattn_fwd")
    w_o = weight("xattn_w_o", o)
    x3 = _matmul(o, w_o, mode="nn", name="xattn_o", out_dtype=F32, res=x2)
    x4, ffn2_saved = ffn_fwd(x3, "ffn2", None)
    sq, dx4, dx4h, g["final_norm"] = _loss_head(x4, vec["final_norm"], target, "loss_head")

    dh = ffn_bwd(dx4h, ffn2_saved, "ffn2", False)
    dx3, dx3b, g["ffn2_norm"] = _rms_bwd(x3, vec["ffn2_norm"], dh, dx4, "ffn2_norm_bwd", lowp=1.0)
    send("xattn_w_o", mm(o, dx3b, mode="tn", name="xattn_dwo", out_dtype=F32, tm=1024, tn=1024))
    do = mm(dx3b, w_o, mode="nt", name="xattn_do", out_dtype=BF16)
    dq, dkv = _xattn_bwd(q, kv, do, "xattn_bwd")
    send("xattn_w_q", mm(h3, dq, mode="tn", name="xattn_dwq", out_dtype=F32, tm=1024, tn=1024))
    dh3 = mm(dq, w_q, mode="nt", name="xattn_dh", out_dtype=F32)
    dkvb = _cast(dkv, BF16, "dkv_cast")
    send("xattn_w_kv", mm(mh, dkvb, mode="tn", name="xattn_dwkv", out_dtype=F32, tm=1024, tn=1024, out_groups=N_SHARDS))
    dmh = mm(dkvb, w_kv, mode="nt", name="xattn_dmh", out_dtype=F32, b_groups=True, tk=1024)
    _, g["mem_norm"] = _rms_bwd(mem, vec["mem_norm"], dmh, None, "mem_norm_bwd")
    pending.append(g["mem_norm"])
    dx2, dx2b, g["xattn_norm"] = _rms_bwd(x2, vec["xattn_norm"], dh3, dx3, "xattn_norm_bwd", lowp=1.0)
    send("w_out", mm(ymix, dx2b, mode="tn", name="dw_out", out_dtype=F32, tm=1024, tn=1024))
    dymix = mm(dx2b, w_out, mode="nt", name="dymix", out_dtype=F32)
    du, dpool_w, g["pool_scale"] = _pool_bwd(dymix, dpool, pool_w, vec["pool_scale"], "pool_bwd")
    send("pool_w", dpool_w)
    dproj, dw_a2, g["gla_b_a"], g["gla_head_norm"] = _gla_bwd(
        proj, states, dymix, du, w_a2, vec["gla_b_a"], vec["gla_head_norm"], "gla_bwd")
    send("gla_w_a2", dw_a2)
    send("w_in", mm(h2, dproj, mode="tn", name="dw_in", out_dtype=F32, tm=1024, tn=1408))
    dh2 = mm(dproj, w_in, mode="nt", name="dh2", out_dtype=F32, tn=1024, tk=dproj.shape[1])
    pending.extend(early_update(dh2))
    dx1, dx1h, g["mix_norm"] = _rms_bwd(x1, vec["mix_norm"], dh2, dx2, "mix_norm_bwd", lowp=0.5)
    dh = ffn_bwd(dx1h, ffn1_saved, "ffn1", True)
    dx0, g["ffn1_norm"] = _rms_bwd(x, vec["ffn1_norm"], dh, dx1, "ffn1_norm_bwd")
    return sq, dx0, g


def _place():
    x, y, c = lax.axis_index("x"), lax.axis_index("y"), lax.axis_index("c")
    chips = [(1 - x, y), (x, 1 - y), (1 - x, 1 - y)]
    return x, y, c, chips


def _ids():
    return jnp.stack([2 * lax.axis_index("x") + lax.axis_index("y"), lax.axis_index("c")]).astype(jnp.int32)


def _hbm(a):
    return pltpu.with_memory_space_constraint(a, pltpu.HBM)


def _cast_to_slot(w2d, dtype, name, dep=None):
    R, C = w2d.shape
    tr = _tile(R, max(16, (4 << 20) // (4 * C) // 16 * 16))

    def body(i_ref, w_ref, *rest):
        rest[-1][...] = w_ref[...].astype(dtype)

    in_specs = [pl.BlockSpec((tr, C), lambda r, i: (r, 0))]
    operands = [w2d]
    if dep is not None:
        in_specs.append(pl.BlockSpec(dep.shape, lambda r, i: (0, 0)))
        operands.append(dep)
    grid_spec = pltpu.PrefetchScalarGridSpec(num_scalar_prefetch=1, grid=(R // tr,), in_specs=in_specs,
                                             out_specs=pl.BlockSpec((None, tr, C), lambda r, i: (i[0], r, 0)))
    return pl.pallas_call(body, name=name, grid_spec=grid_spec, out_shape=jax.ShapeDtypeStruct((N_SHARDS, R, C), dtype),
                          compiler_params=_params())(_ids(), *operands)


def _gather_copies(buf_ref, send_sems, recv_sems, incoming, whole):
    x, y, c, chips = _place()
    hr = buf_ref.shape[1] // 2
    copies = []
    for j, (px, py) in enumerate(chips):
        slot = 2 * px + py if incoming else 2 * x + y
        part = buf_ref.at[slot] if whole else buf_ref.at[slot, pl.ds(c * hr, hr), :]
        copies.append(pltpu.make_async_remote_copy(src_ref=part, dst_ref=part, send_sem=send_sems.at[j],
                                                   recv_sem=recv_sems.at[j], device_id=(px, py, c), device_id_type=MESH))
    return copies


def _gather_start(buf, name, whole):
    def body(b_ref, send_sems, recv_sems, b_thru, token):
        for cp in _gather_copies(b_ref, send_sems, recv_sems, False, whole):
            cp.start()
        token[...] = jnp.zeros_like(token)

    return pl.pallas_call(
        body, name=name,
        out_shape=(pltpu.SemaphoreType.DMA((3,)), pltpu.SemaphoreType.DMA((3,)), pltpu.HBM(buf.shape, buf.dtype),
                   jax.ShapeDtypeStruct((8, LANES), F32)),
        in_specs=(HBM,), out_specs=(SEM, SEM, HBM, pl.BlockSpec(memory_space=pltpu.VMEM)), input_output_aliases={0: 2},
        compiler_params=pltpu.CompilerParams(has_side_effects=EFFECT))(_hbm(buf))


def _gather_wait(send_sems, recv_sems, buf_thru, after, name, whole):
    def body(b_ref, send_sems, recv_sems, after_ref, b_out):
        for cp in _gather_copies(b_ref, send_sems, recv_sems, False, whole):
            cp.wait_send()
        for cp in _gather_copies(b_ref, send_sems, recv_sems, True, whole):
            cp.wait_recv()

    return pl.pallas_call(
        body, name=name, out_shape=pltpu.HBM(buf_thru.shape, buf_thru.dtype),
        in_specs=(HBM, SEM, SEM, ANY), out_specs=HBM, input_output_aliases={0: 0},
        compiler_params=pltpu.CompilerParams(has_side_effects=EFFECT))(buf_thru, send_sems, recv_sems, after)


def _gather_forward(buf, name):
    G, R, C = buf.shape
    hr = R // 2

    def body(b_ref, o_ref, send_sems, recv_sems):
        x, y, c, chips = _place()
        copies = []
        for j, (px, py) in enumerate(chips):
            half = o_ref.at[2 * px + py, pl.ds(c * hr, hr), :]
            copies.append(pltpu.make_async_remote_copy(src_ref=half, dst_ref=half, send_sem=send_sems.at[j],
                                                       recv_sem=recv_sems.at[j], device_id=(x, y, 1 - c),
                                                       device_id_type=MESH))
        for cp in copies:
            cp.start()
        for j, (px, py) in enumerate(chips):
            half = o_ref.at[2 * px + py, pl.ds((1 - c) * hr, hr), :]
            pltpu.make_async_remote_copy(src_ref=half, dst_ref=half, send_sem=send_sems.at[j], recv_sem=recv_sems.at[j],
                                         device_id=(x, y, 1 - c), device_id_type=MESH).wait_recv()
        for cp in copies:
            cp.wait_send()

    return pl.pallas_call(body, name=name, in_specs=[ANY], out_specs=ANY, out_shape=jax.ShapeDtypeStruct(buf.shape, buf.dtype),
                          input_output_aliases={0: 0},
                          scratch_shapes=[pltpu.SemaphoreType.DMA((3,)), pltpu.SemaphoreType.DMA((3,))])(buf)


def _pair_copy(g_ref, land_ref, send_sem, recv_sem):
    x, y, c, _ = _place()
    hr = g_ref.shape[1] // 2
    return pltpu.make_async_remote_copy(src_ref=g_ref.at[:, pl.ds((1 - c) * hr, hr), :], dst_ref=land_ref,
                                        send_sem=send_sem, recv_sem=recv_sem, device_id=(x, y, 1 - c), device_id_type=MESH)


def _pair_start(gfull, name):
    G, R, C = gfull.shape

    def body(g_ref, land_ref, send_sem, recv_sem, g_thru, land_thru, token):
        _pair_copy(g_ref, land_ref, send_sem, recv_sem).start()
        token[...] = jnp.zeros_like(token)

    return pl.pallas_call(
        body, name=name,
        out_shape=(pltpu.SemaphoreType.DMA(()), pltpu.SemaphoreType.DMA(()), pltpu.HBM(gfull.shape, F32),
                   pltpu.HBM((G, R // 2, C), F32), jax.ShapeDtypeStruct((8, LANES), F32)),
        in_specs=(HBM, HBM), out_specs=(SEM, SEM, HBM, HBM, pl.BlockSpec(memory_space=pltpu.VMEM)),
        input_output_aliases={0: 2, 1: 3},
        compiler_params=pltpu.CompilerParams(has_side_effects=EFFECT))(_hbm(gfull), _hbm(lax.empty((G, R // 2, C), F32)))


def _pair_wait(send_sem, recv_sem, g_thru, land_thru, after, name):
    def body(g_ref, land_ref, send_sem, recv_sem, after_ref, g_out, land_out):
        cp = _pair_copy(g_ref, land_ref, send_sem, recv_sem)
        cp.wait_send()
        cp.wait_recv()

    return pl.pallas_call(
        body, name=name, out_shape=(pltpu.HBM(g_thru.shape, F32), pltpu.HBM(land_thru.shape, F32)),
        in_specs=(HBM, HBM, SEM, SEM, ANY), out_specs=(HBM, HBM), input_output_aliases={0: 0, 1: 1},
        compiler_params=pltpu.CompilerParams(has_side_effects=EFFECT))(g_thru, land_thru, send_sem, recv_sem, after)


def _pair_add(gfull, other, name):
    G, R, C = gfull.shape
    hr = R // 2
    tr = _tile(hr, max(8, (2 * 1024 * 1024) // (4 * C) // 8 * 8))
    nr = hr // tr
    c = lax.axis_index("c")
    cidx = jnp.reshape(c, (1,)).astype(jnp.int32)

    def body(c_ref, a_ref, b_ref, o_ref):
        o_ref[...] = a_ref[...] + b_ref[...]

    grid_spec = pltpu.PrefetchScalarGridSpec(
        num_scalar_prefetch=1, grid=(G, nr),
        in_specs=[pl.BlockSpec((None, tr, C), lambda g, r, cr: (g, cr[0] * nr + r, 0)),
                  pl.BlockSpec((None, tr, C), lambda g, r, cr: (g, r, 0))],
        out_specs=pl.BlockSpec((None, tr, C), lambda g, r, cr: (g, r, 0)))
    return pl.pallas_call(body, name=name, grid_spec=grid_spec, out_shape=jax.ShapeDtypeStruct((G, hr, C), F32),
                          compiler_params=_params())(cidx, gfull, other)


def _chip_copies(p_ref, land_ref, send_sems, recv_sems, incoming):
    x, y, c, chips = _place()
    me = 2 * x + y
    copies = []
    for j, (px, py) in enumerate(chips):
        dst = land_ref.at[2 * px + py] if incoming else land_ref.at[me]
        copies.append(pltpu.make_async_remote_copy(src_ref=p_ref.at[2 * px + py], dst_ref=dst, send_sem=send_sems.at[j],
                                                   recv_sem=recv_sems.at[j], device_id=(px, py, c), device_id_type=MESH))
    return copies


def _chip_start(part, name):
    def body(p_ref, land_ref, send_sems, recv_sems, p_thru, land_thru, token):
        for cp in _chip_copies(p_ref, land_ref, send_sems, recv_sems, False):
            cp.start()
        token[...] = jnp.zeros_like(token)

    return pl.pallas_call(
        body, name=name,
        out_shape=(pltpu.SemaphoreType.DMA((3,)), pltpu.SemaphoreType.DMA((3,)), pltpu.HBM(part.shape, F32),
                   pltpu.HBM(part.shape, F32), jax.ShapeDtypeStruct((8, LANES), F32)),
        in_specs=(HBM, HBM), out_specs=(SEM, SEM, HBM, HBM, pl.BlockSpec(memory_space=pltpu.VMEM)),
        input_output_aliases={0: 2, 1: 3},
        compiler_params=pltpu.CompilerParams(has_side_effects=EFFECT))(_hbm(part), _hbm(lax.empty(part.shape, F32)))


def _chip_wait(send_sems, recv_sems, p_thru, land_thru, after, name):
    def body(p_ref, land_ref, send_sems, recv_sems, after_ref, p_out, land_out):
        for cp in _chip_copies(p_ref, land_ref, send_sems, recv_sems, False):
            cp.wait_send()
        for cp in _chip_copies(p_ref, land_ref, send_sems, recv_sems, True):
            cp.wait_recv()

    return pl.pallas_call(
        body, name=name, out_shape=(pltpu.HBM(p_thru.shape, F32), pltpu.HBM(p_thru.shape, F32)),
        in_specs=(HBM, HBM, SEM, SEM, ANY), out_specs=(HBM, HBM), input_output_aliases={0: 0, 1: 1},
        compiler_params=pltpu.CompilerParams(has_side_effects=EFFECT))(p_thru, land_thru, send_sems, recv_sems, after)


def _chip_sum(part, slots, name):
    G, R2, C = part.shape
    tr = _tile(R2, max(8, (1 << 20) // (4 * C) // 8 * 8))
    nr = R2 // tr

    def body(i_ref, p_ref, *rest):
        o_ref = rest[-1]
        acc = None
        for u in range(G):
            val = jnp.where(i_ref[0] == u, p_ref[...], rest[u][...])
            acc = val if acc is None else acc + val
        o_ref[...] = acc

    def slot_spec(u):
        return pl.BlockSpec((None, tr, C), lambda r, i: (jnp.where(i[0] == u, (u + 1) % G, u), r, 0))

    grid_spec = pltpu.PrefetchScalarGridSpec(
        num_scalar_prefetch=1, grid=(nr,),
        in_specs=[pl.BlockSpec((None, tr, C), lambda r, i: (i[0], r, 0))] + [slot_spec(u) for u in range(G)],
        out_specs=pl.BlockSpec((tr, C), lambda r, i: (i[1] * nr + r, 0)))
    return pl.pallas_call(body, name=name, grid_spec=grid_spec, out_shape=jax.ShapeDtypeStruct((2 * R2, C), F32),
                          compiler_params=_params())(_ids(), part, slots, slots, slots, slots)


def _sum_slots(slots, name):
    G, R2, C = slots.shape
    tr = _tile(R2, max(8, (1024 * 1024) // (4 * C) // 8 * 8))

    def body(s_ref, o_ref):
        acc = s_ref[0]
        for u in range(1, G):
            acc = acc + s_ref[u]
        o_ref[...] = acc

    return pl.pallas_call(body, name=name, grid=(R2 // tr,), in_specs=[pl.BlockSpec((G, tr, C), lambda r: (0, r, 0))],
                          out_specs=pl.BlockSpec((tr, C), lambda r: (r, 0)), out_shape=jax.ShapeDtypeStruct((R2, C), F32),
                          compiler_params=_params())(slots)


def _pair_join(full, name):
    R, C = full.shape
    R2 = R // 2

    def body(f_ref, o_ref, token, send_sem, recv_sem):
        x, y, c, _ = _place()
        token[...] = jnp.zeros_like(token)
        mine = o_ref.at[pl.ds(c * R2, R2), :]
        theirs = o_ref.at[pl.ds((1 - c) * R2, R2), :]
        cp = pltpu.make_async_remote_copy(src_ref=mine, dst_ref=mine, send_sem=send_sem, recv_sem=recv_sem,
                                          device_id=(x, y, 1 - c), device_id_type=MESH)
        cp.start()
        pltpu.make_async_remote_copy(src_ref=theirs, dst_ref=theirs, send_sem=send_sem, recv_sem=recv_sem,
                                     device_id=(x, y, 1 - c), device_id_type=MESH).wait_recv()
        cp.wait_send()

    return pl.pallas_call(body, name=name, in_specs=[ANY], out_specs=[ANY, pl.BlockSpec(memory_space=pltpu.VMEM)],
                          out_shape=[jax.ShapeDtypeStruct((R, C), F32), jax.ShapeDtypeStruct((8, LANES), F32)],
                          input_output_aliases={0: 0},
                          scratch_shapes=[pltpu.SemaphoreType.DMA, pltpu.SemaphoreType.DMA])(full)


def _all_reduce_small(v, name):
    R, C = v.shape

    def gather_body(v_ref, out_ref, send_sems, recv_sems, local_sem):
        x, y, c, _ = _place()
        me = 4 * x + 2 * y + c
        mine = pltpu.make_async_copy(v_ref, out_ref.at[me], local_sem)
        mine.start()
        flips = [(fx, fy, fc) for fx in (0, 1) for fy in (0, 1) for fc in (0, 1)][1:]
        copies = []
        for j, (fx, fy, fc) in enumerate(flips):
            peer = (x ^ fx, y ^ fy, c ^ fc)
            copies.append(pltpu.make_async_remote_copy(src_ref=v_ref, dst_ref=out_ref.at[me], send_sem=send_sems.at[j],
                                                       recv_sem=recv_sems.at[j], device_id=peer, device_id_type=MESH))
        for cp in copies:
            cp.start()
        for j, (fx, fy, fc) in enumerate(flips):
            peer = (x ^ fx, y ^ fy, c ^ fc)
            pltpu.make_async_remote_copy(src_ref=v_ref, dst_ref=out_ref.at[4 * peer[0] + 2 * peer[1] + peer[2]],
                                         send_sem=send_sems.at[j], recv_sem=recv_sems.at[j], device_id=peer,
                                         device_id_type=MESH).wait_recv()
        for cp in copies:
            cp.wait_send()
        mine.wait()

    slots = pl.pallas_call(gather_body, name=name, in_specs=[ANY], out_specs=ANY,
                           out_shape=jax.ShapeDtypeStruct((8, R, C), F32),
                           scratch_shapes=[pltpu.SemaphoreType.DMA((7,)), pltpu.SemaphoreType.DMA((7,)),
                                           pltpu.SemaphoreType.DMA])(v)
    return _sum_slots(slots, f"{name}_sum")


def _adamw(w, g, m, v, name, dep=()):
    R, C = w.shape
    tr = _tile(R, max(8, (2 << 20) // (4 * C) // 8 * 8))
    bc1 = 1.0 - ADAM_B1 ** ADAM_STEP
    bc2 = 1.0 - ADAM_B2 ** ADAM_STEP

    def body(w_ref, g_ref, m_ref, v_ref, *rest):
        go_ref, d_ref, nm_ref, nv_ref = rest[len(dep):]
        gv = g_ref[...]
        go_ref[...] = gv
        nm = ADAM_B1 * m_ref[...] + (1.0 - ADAM_B1) * gv
        nv = ADAM_B2 * v_ref[...] + (1.0 - ADAM_B2) * (gv * gv)
        nm_ref[...] = nm
        nv_ref[...] = nv
        d_ref[...] = -ADAM_LR * ((nm / bc1) / (jnp.sqrt(nv / bc2) + ADAM_EPS) + ADAM_WD * w_ref[...])

    blk = pl.BlockSpec((tr, C), lambda r: (r, 0))
    out = jax.ShapeDtypeStruct((R, C), F32)
    in_specs = [blk] * 4 + [pl.BlockSpec(d.shape, lambda r: (0, 0)) for d in dep]
    return pl.pallas_call(body, name=name, grid=(R // tr,), in_specs=in_specs, out_specs=[blk] * 4, out_shape=[out] * 4,
                          compiler_params=_params())(w, g, m, v, *dep)


SC_TILES = 32
SC_LANES = 16
SC_ROWS = 8


def _adamw_sc(w, g, m, v, name):
    R, C = w.shape
    per_tile = R // SC_TILES
    bc1 = 1.0 - ADAM_B1 ** ADAM_STEP
    bc2 = 1.0 - ADAM_B2 ** ADAM_STEP

    def body(w_hbm, g_hbm, m_hbm, v_hbm, go_hbm, d_hbm, nm_hbm, nv_hbm, wb, gb, mb, vb):
        tile = lax.axis_index("sc_subcore") * 2 + lax.axis_index("sc_core")
        base = tile * per_tile

        @pl.loop(0, per_tile, step=SC_ROWS)
        def _(r):
            rows = pl.ds(base + r, SC_ROWS)
            pltpu.sync_copy(w_hbm.at[rows, :], wb)
            pltpu.sync_copy(g_hbm.at[rows, :], gb)
            pltpu.sync_copy(m_hbm.at[rows, :], mb)
            pltpu.sync_copy(v_hbm.at[rows, :], vb)

            @pl.loop(0, SC_ROWS)
            def _(i):
                @pl.loop(0, C, step=SC_LANES)
                def _(j):
                    at = (i, pl.ds(j, SC_LANES))
                    gv = gb[at]
                    nm = ADAM_B1 * mb[at] + (1.0 - ADAM_B1) * gv
                    nv = ADAM_B2 * vb[at] + (1.0 - ADAM_B2) * (gv * gv)
                    mb[at] = nm
                    vb[at] = nv
                    wb[at] = -ADAM_LR * ((nm / bc1) / (jnp.sqrt(nv / bc2) + ADAM_EPS) + ADAM_WD * wb[at])

            pltpu.sync_copy(gb, go_hbm.at[rows, :])
            pltpu.sync_copy(wb, d_hbm.at[rows, :])
            pltpu.sync_copy(mb, nm_hbm.at[rows, :])
            pltpu.sync_copy(vb, nv_hbm.at[rows, :])

    out = jax.ShapeDtypeStruct((R, C), F32)
    buf = pltpu.VMEM((SC_ROWS, C), F32)
    return pl.kernel(body, name=name, out_type=(out, out, out, out),
                     mesh=plsc.VectorSubcoreMesh(core_axis_name="sc_core", subcore_axis_name="sc_subcore"),
                     scratch_types=[buf, buf, buf, buf],
                     cost_estimate=pl.CostEstimate(flops=16 * R * C, transcendentals=2 * R * C, bytes_accessed=32 * R * C),
                     )(w, g, m, v)


WEIGHTS = ['ffn1_norm', 'ffn1_w_gate', 'ffn1_w_up', 'ffn1_w_down', 'mix_norm', 'w_in', 'pool_w', 'pool_scale', 'gla_w_a2',
           'gla_b_a', 'gla_head_norm', 'w_out', 'xattn_norm', 'mem_norm', 'xattn_w_q', 'xattn_w_kv', 'xattn_w_o', 'ffn2_norm',
           'ffn2_w_gate', 'ffn2_w_up', 'ffn2_w_down', 'final_norm']
SHARDED = ['ffn1_w_gate', 'ffn1_w_up', 'ffn1_w_down', 'w_in', 'pool_w', 'gla_w_a2', 'w_out', 'xattn_w_q', 'xattn_w_kv',
           'xattn_w_o', 'ffn2_w_gate', 'ffn2_w_up', 'ffn2_w_down']
REPLICATED = [n for n in WEIGHTS if n not in SHARDED]
ON_SPARSECORE = ['ffn2_w_gate', 'ffn2_w_up', 'w_out', 'xattn_w_q', 'xattn_w_kv', 'xattn_w_o']
SMALL_COLS = 512


def _as2d(a):
    return a.reshape(-1, a.shape[-1])


def _finish_weight(name, gathered, wl):
    G, R, C = gathered.shape
    rank = wl["gla_w_a2"].shape[1]
    if name in ("w_out", "xattn_w_q", "xattn_w_o"):
        return gathered.reshape(G * R, C)
    if name == "w_in":
        w_in = jnp.transpose(gathered, (1, 0, 2)).reshape(R, G * C)
        main = G * C - rank
        return jnp.concatenate([w_in[:, :main], jnp.pad(w_in[:, main:], ((0, 0), (0, LANES - rank)))], axis=1)
    if name == "pool_w":
        NG, CJ, _ = wl[name].shape[1:]
        return jnp.transpose(gathered.reshape(G, NG, CJ, C), (1, 0, 2, 3)).reshape(NG, G * CJ, C)
    if name == "gla_w_a2":
        a2 = jnp.transpose(gathered, (1, 0, 2)).reshape(rank, G * C)
        return jnp.pad(a2, ((0, LANES - rank), (0, 0))).astype(BF16)
    return gathered


def _start_gathers(wl):
    started = {}
    token = None
    for n in SHARDED:
        whole = not n.startswith("ffn1")
        buf = _cast_to_slot(_as2d(wl[n]), BF16, f"slot_{n}", dep=token)
        send_sems, recv_sems, thru, token = _gather_start(buf, f"gather_start_{n}", whole)
        started[n] = (send_sems, recv_sems, thru, whole)
    cache = {}

    def weight(n, after=None):
        if n not in cache:
            *handles, whole = started[n]
            buf = _gather_wait(*handles, after, f"gather_wait_{n}", whole)
            if not whole:
                buf = _gather_forward(buf, f"gather_forward_{n}")
            cache[n] = _finish_weight(n, buf, wl)
        return cache[n]

    return weight, token


def _shard_major(name, gfull, wl):
    R, C = _as2d(wl[name]).shape
    if name in ("ffn1_w_gate", "ffn1_w_up", "ffn2_w_gate", "ffn2_w_up", "xattn_w_kv"):
        return gfull
    if name in ("ffn1_w_down", "ffn2_w_down", "w_out", "xattn_w_q", "xattn_w_o"):
        return gfull.reshape(N_SHARDS, R, C)
    if name == "w_in":
        return jnp.transpose(gfull[:, :N_SHARDS * C].reshape(R, N_SHARDS, C), (1, 0, 2))
    if name == "pool_w":
        NG, CJ, _ = wl[name].shape[1:]
        return jnp.transpose(gfull.reshape(NG, N_SHARDS, CJ, C), (1, 0, 2, 3)).reshape(N_SHARDS, R, C)
    assert name == "gla_w_a2"
    return jnp.transpose(gfull[:R].reshape(R, N_SHARDS, C), (1, 0, 2))


def kernel(x, mem, ffn1_norm, ffn1_w_gate, ffn1_w_up, ffn1_w_down, mix_norm, w_in, pool_w, pool_scale, gla_w_a2, gla_b_a, gla_head_norm, w_out, xattn_norm, mem_norm, xattn_w_q, xattn_w_kv, xattn_w_o, ffn2_norm, ffn2_w_gate, ffn2_w_up, ffn2_w_down, final_norm, loss_target, m_ffn1_norm, m_ffn1_w_gate, m_ffn1_w_up, m_ffn1_w_down, m_mix_norm, m_w_in, m_pool_w, m_pool_scale, m_gla_w_a2, m_gla_b_a, m_gla_head_norm, m_w_out, m_xattn_norm, m_mem_norm, m_xattn_w_q, m_xattn_w_kv, m_xattn_w_o, m_ffn2_norm, m_ffn2_w_gate, m_ffn2_w_up, m_ffn2_w_down, m_final_norm, v_ffn1_norm, v_ffn1_w_gate, v_ffn1_w_up, v_ffn1_w_down, v_mix_norm, v_w_in, v_pool_w, v_pool_scale, v_gla_w_a2, v_gla_b_a, v_gla_head_norm, v_w_out, v_xattn_norm, v_mem_norm, v_xattn_w_q, v_xattn_w_kv, v_xattn_w_o, v_ffn2_norm, v_ffn2_w_gate, v_ffn2_w_up, v_ffn2_w_down, v_final_norm):
    given = dict(locals())
    wl = {n: given[n] for n in WEIGHTS}
    ml = {n: given["m_" + n] for n in WEIGHTS}
    vl = {n: given["v_" + n] for n in WEIGHTS}

    vec = {n: wl[n].reshape(1, -1) for n in REPLICATED}
    weight, dep0 = _start_gathers(wl)
    in_flight = {}

    pair_flight = {}

    def emit_begin(n, gfull):
        *pair_flight[n], token = _pair_start(_shard_major(n, gfull, wl), f"{n}_pair_start")
        return token

    def emit_finish(n, after):
        gsm, other = _pair_wait(*pair_flight.pop(n), after, f"{n}_pair_wait")
        *in_flight[n], token = _chip_start(_pair_add(gsm, other, f"{n}_pair_add"), f"{n}_chip_start")
        return token

    grads = {}
    updates = {}

    def reduce_done(n, after):
        part, slots = _chip_wait(*in_flight.pop(n), after, f"{n}_chip_wait")
        grads[n], token = _pair_join(_chip_sum(part, slots, f"{n}_chip_sum"), f"{n}_pair_join")
        return token

    def early_update(after):
        tokens = [reduce_done(n, after) for n in ON_SPARSECORE]
        for n in ON_SPARSECORE:
            g2 = grads[n]
            updates[n] = _adamw_sc(wl[n].reshape(g2.shape), g2, ml[n].reshape(g2.shape), vl[n].reshape(g2.shape),
                                   f"adamw_sc_{n}")
        return tokens

    sq, dx0, g = _local_step(x[0], mem[0], loss_target[0], vec, weight, (emit_begin, emit_finish, early_update), dep0)
    loss = lax.psum(0.5 * jnp.sum(sq) / x.shape[-1], ("x", "y", "c"))

    for n in list(in_flight):
        reduce_done(n, dx0)
    widths = [wl[n].size for n in REPLICATED]
    total = sum(widths)
    rows = -(-total // SMALL_COLS)
    rows = -(-rows // 8) * 8
    packed = jnp.concatenate([g[n].reshape(-1) for n in REPLICATED] + [jnp.zeros((rows * SMALL_COLS - total,), F32)])
    summed = _all_reduce_small(packed.reshape(rows, SMALL_COLS), "small_all_reduce").reshape(-1)
    off = 0
    for n, width in zip(REPLICATED, widths):
        grads[n] = summed[off:off + width].reshape(1, width)
        off += width

    out_g, out_d, out_m, out_v = [], [], [], []
    for n in WEIGHTS:
        shape = wl[n].shape
        g2 = grads[n]
        if n in updates:
            go, d, nm, nv = updates[n]
        else:
            dep = tuple(updates[k][1][:8, :LANES] for k in updates) if n == "w_in" else ()
            go, d, nm, nv = _adamw(wl[n].reshape(g2.shape), g2, ml[n].reshape(g2.shape), vl[n].reshape(g2.shape),
                                   f"adamw_{n}", dep)
        out_g.append(go.reshape(shape))
        out_d.append(d.reshape(shape))
        out_m.append(nm.reshape(shape))
        out_v.append(nv.reshape(shape))
    return (loss, dx0.reshape(x.shape), *out_g, *out_d, *out_m, *out_v)
```

```python
import functools

import jax
import jax.numpy as jnp
from jax import lax
from jax.experimental import pallas as pl
from jax.experimental.pallas import tpu as pltpu
from jax.experimental.pallas import tpu_sc as plsc

F32 = jnp.float32
BF16 = jnp.bfloat16
MESH = pl.DeviceIdType.MESH

RMS_EPS = 1e-6
CHUNK = 64
POOL_WINDOWS = (2, 4, 8, 16)
POOL_HALO = 16
N_HEADS = 4
GATE_TEMP = 16.0
ADAM_LR, ADAM_B1, ADAM_B2, ADAM_EPS, ADAM_WD, ADAM_STEP = 0.001, 0.9, 0.999, 1e-08, 0.01, 10
N_SHARDS = 4
LANES = 128
MXU_COLS = 256
VMEM_LIMIT = 58 * 1024 * 1024

ANY = pl.BlockSpec(memory_space=pl.ANY)
HBM = pl.BlockSpec(memory_space=pltpu.HBM)
SEM = pl.BlockSpec(memory_space=pltpu.SEMAPHORE)
EFFECT = pltpu.SideEffectType.DATAFLOW_SIDE_EFFECTING


def _params(**kw):
    return pltpu.CompilerParams(vmem_limit_bytes=VMEM_LIMIT, **kw)


def _tile(n, want):
    for unit in (LANES, 8):
        t = (min(want, n) // unit) * unit
        while t >= unit:
            if n % t == 0:
                return t
            t -= unit
    return n


def _dot(a, b, dims):
    return lax.dot_general(a, b, (dims, ((), ())), preferred_element_type=F32)


def _nn(a, b):
    return _dot(a, b, ((1,), (0,)))


def _nt(a, b):
    return _dot(a, b, ((1,), (1,)))


def _tn(a, b):
    return _dot(a, b, ((0,), (0,)))


def _sigmoid(x):
    return 1.0 / (1.0 + jnp.exp(-x))


def _matmul_cost(M, N, K, operands, out_shape):
    nbytes = sum(a.size * a.dtype.itemsize for a in operands) + out_shape.size * out_shape.dtype.itemsize
    return pl.CostEstimate(flops=2 * M * N * K, transcendentals=0, bytes_accessed=nbytes)


def _matmul(a, b, *, mode, name, out_dtype, tm=512, tn=2048, tk=2048, res=None, scale=1.0, b_groups=False, out_groups=0,
            dep=()):
    if mode == "tn":
        K, M = a.shape
    else:
        M, K = a.shape
    if mode == "nn":
        if b_groups:
            G, _, Nj = b.shape
            N = G * Nj
        else:
            N = b.shape[1]
    elif mode == "nt":
        if b_groups:
            G, N, Kj = b.shape
            assert G * Kj == K
        else:
            N = b.shape[0]
    else:
        N = b.shape[1]
    tm = _tile(M, tm)
    if mode == "nn" and b_groups:
        tn = _tile(Nj, tn)
    elif out_groups:
        tn = _tile(N // out_groups, tn)
    else:
        tn = _tile(N, tn)
    if mode == "nt" and b_groups:
        tk = _tile(Kj, tk)
    else:
        tk = _tile(K, tk)
    nk = K // tk
    grid = (M // tm, N // tn, nk)

    if mode == "tn":
        a_spec = pl.BlockSpec((tk, tm), lambda i, j, k: (k, i))
        b_spec = pl.BlockSpec((tk, tn), lambda i, j, k: (k, j))
        dims = ((0,), (0,))
    elif mode == "nn":
        a_spec = pl.BlockSpec((tm, tk), lambda i, j, k: (i, k))
        if b_groups:
            npj = Nj // tn
            b_spec = pl.BlockSpec((None, tk, tn), lambda i, j, k: (j // npj, k, j % npj))
        else:
            b_spec = pl.BlockSpec((tk, tn), lambda i, j, k: (k, j))
        dims = ((1,), (0,))
    else:
        a_spec = pl.BlockSpec((tm, tk), lambda i, j, k: (i, k))
        if b_groups:
            kpj = Kj // tk
            b_spec = pl.BlockSpec((None, tn, tk), lambda i, j, k: (k // kpj, j, k % kpj))
        else:
            b_spec = pl.BlockSpec((tn, tk), lambda i, j, k: (j, k))
        dims = ((1,), (1,))
    if out_groups:
        npj = (N // out_groups) // tn
        o_spec = pl.BlockSpec((None, tm, tn), lambda i, j, k: (j // npj, i, j % npj))
        out_shape = jax.ShapeDtypeStruct((out_groups, M, N // out_groups), out_dtype)
    else:
        o_spec = pl.BlockSpec((tm, tn), lambda i, j, k: (i, j))
        out_shape = jax.ShapeDtypeStruct((M, N), out_dtype)
    in_specs = [a_spec, b_spec]
    operands = [a, b]
    if res is not None:
        in_specs.append(pl.BlockSpec((tm, tn), lambda i, j, k: (i, j)))
        operands.append(res)
    has_res = res is not None
    n_dep = len(dep)
    for d in dep:
        in_specs.append(pl.BlockSpec(d.shape, lambda i, j, k: (0, 0)))
        operands.append(d)

    def body(*refs):
        if has_res:
            a_ref, b_ref, r_ref = refs[:3]
        else:
            a_ref, b_ref = refs[:2]
            r_ref = None
        o_ref = refs[2 + has_res + n_dep]

        def finish(acc):
            if scale != 1.0:
                acc = acc * scale
            if r_ref is not None:
                acc = r_ref[...] + acc
            o_ref[...] = acc.astype(o_ref.dtype)

        part = _dot(a_ref[...], b_ref[...], dims)
        if nk == 1:
            finish(part)
        else:
            acc_ref = o_ref if in_place else refs[-1]
            k = pl.program_id(2)

            @pl.when(k == 0)
            def _():
                acc_ref[...] = part

            @pl.when(k > 0)
            def _():
                acc_ref[...] += part

            if not in_place:
                @pl.when(k == nk - 1)
                def _():
                    finish(acc_ref[...])

    in_place = out_dtype == F32 and res is None and scale == 1.0
    scratch = [] if nk == 1 or in_place else [pltpu.VMEM((tm, tn), F32)]
    return pl.pallas_call(body, name=name, grid=grid, in_specs=in_specs, out_specs=o_spec, out_shape=out_shape,
                          scratch_shapes=scratch, compiler_params=_params(),
                          cost_estimate=_matmul_cost(M, N, K, operands, out_shape))(*operands)


def _rms_fwd(x, gain, name, tm=256, dep=None):
    S, D = x.shape
    tm = _tile(S, tm)

    def body(x_ref, g_ref, *rest):
        o_ref = rest[-1]
        xv = x_ref[...]
        r = lax.rsqrt(jnp.mean(xv * xv, axis=-1, keepdims=True) + RMS_EPS)
        o_ref[...] = (xv * r * g_ref[...]).astype(o_ref.dtype)

    in_specs = [pl.BlockSpec((tm, D), lambda i: (i, 0)), pl.BlockSpec((1, D), lambda i: (0, 0))]
    operands = [x, gain]
    if dep is not None:
        in_specs.append(pl.BlockSpec(dep.shape, lambda i: (0, 0)))
        operands.append(dep)
    return pl.pallas_call(body, name=name, grid=(S // tm,), in_specs=in_specs,
                          out_specs=pl.BlockSpec((tm, D), lambda i: (i, 0)),
                          out_shape=jax.ShapeDtypeStruct((S, D), BF16), compiler_params=_params())(*operands)


def _rms_bwd(x, gain, dh, dres, name, lowp=None, tm=256):
    half = lowp is not None
    S, D = x.shape
    tm = _tile(S, tm)
    has_res = dres is not None

    def body(*refs):
        if has_res:
            x_ref, g_ref, dh_ref, dr_ref = refs[:4]
            outs = refs[4:]
        else:
            x_ref, g_ref, dh_ref = refs[:3]
            dr_ref = None
            outs = refs[3:]
        dx_ref, dg_ref = outs[0], outs[-1]
        xv = x_ref[...]
        dhv = dh_ref[...].astype(F32)
        r = lax.rsqrt(jnp.mean(xv * xv, axis=-1, keepdims=True) + RMS_EPS)
        gy = dhv * g_ref[...]
        dx = r * gy - xv * (r * r * r) * jnp.mean(gy * xv, axis=-1, keepdims=True)
        if dr_ref is not None:
            dx = dx + dr_ref[...]
        dx_ref[...] = dx
        if half:
            outs[1][...] = (dx if lowp == 1.0 else lowp * dx).astype(BF16)
        part = jnp.sum(dhv * xv * r, axis=0, keepdims=True)

        @pl.when(pl.program_id(0) == 0)
        def _():
            dg_ref[...] = part

        @pl.when(pl.program_id(0) > 0)
        def _():
            dg_ref[...] += part

    row = pl.BlockSpec((tm, D), lambda i: (i, 0))
    vec = pl.BlockSpec((1, D), lambda i: (0, 0))
    in_specs = [row, vec, row] + ([row] if has_res else [])
    operands = [x, gain, dh] + ([dres] if has_res else [])
    out_specs = [row] + ([row] if half else []) + [vec]
    out_shape = [jax.ShapeDtypeStruct((S, D), F32)] + ([jax.ShapeDtypeStruct((S, D), BF16)] if half else []) + [
        jax.ShapeDtypeStruct((1, D), F32)]
    return pl.pallas_call(body, name=name, grid=(S // tm,), in_specs=in_specs, out_specs=out_specs, out_shape=out_shape,
                          compiler_params=_params())(*operands)


def _loss_head(x, gain, target, name, tm=256):
    S, D = x.shape
    tm = _tile(S, tm)

    def body(x_ref, g_ref, t_ref, sq_ref, dx_ref, dxh_ref, dg_ref):
        xv = x_ref[...]
        r = lax.rsqrt(jnp.mean(xv * xv, axis=-1, keepdims=True) + RMS_EPS)
        xn = xv * r
        err = xn * g_ref[...] - t_ref[...]
        dout = err * (1.0 / D)
        gy = dout * g_ref[...]
        dx = r * gy - xv * (r * r * r) * jnp.mean(gy * xv, axis=-1, keepdims=True)
        dx_ref[...] = dx
        dxh_ref[...] = (0.5 * dx).astype(BF16)
        sq = jnp.sum(err * err, axis=0, keepdims=True)
        dg = jnp.sum(dout * xn, axis=0, keepdims=True)

        @pl.when(pl.program_id(0) == 0)
        def _():
            sq_ref[...] = sq
            dg_ref[...] = dg

        @pl.when(pl.program_id(0) > 0)
        def _():
            sq_ref[...] += sq
            dg_ref[...] += dg

    row = pl.BlockSpec((tm, D), lambda i: (i, 0))
    vec = pl.BlockSpec((1, D), lambda i: (0, 0))
    return pl.pallas_call(body, name=name, grid=(S // tm,), in_specs=[row, vec, row], out_specs=[vec, row, row, vec],
                          out_shape=[jax.ShapeDtypeStruct((1, D), F32), jax.ShapeDtypeStruct((S, D), F32),
                                     jax.ShapeDtypeStruct((S, D), BF16), jax.ShapeDtypeStruct((1, D), F32)],
                          compiler_params=_params())(x, gain, target)


def _cast(x, dtype, name, scale=1.0, tm=256):
    S, D = x.shape
    tm = _tile(S, tm)

    def body(x_ref, o_ref):
        o_ref[...] = (x_ref[...] * scale).astype(o_ref.dtype)

    row = pl.BlockSpec((tm, D), lambda i: (i, 0))
    return pl.pallas_call(body, name=name, grid=(S // tm,), in_specs=[row], out_specs=row,
                          out_shape=jax.ShapeDtypeStruct((S, D), dtype), compiler_params=_params())(x)


def _ffn_up(h, wg, wu, name, tm=512):
    S, D = h.shape
    G, _, Fj = wg.shape
    tm = _tile(S, tm)

    def body(h_ref, wg_ref, wu_ref, ga_ref, gb_ref, hid_ref):
        hv = h_ref[...]
        a = _nn(hv, wg_ref[...])
        b = _nn(hv, wu_ref[...])
        s = _sigmoid(a)
        silu = a * s
        ga_ref[...] = (b * (s * (1.0 + a * (1.0 - s)))).astype(BF16)
        gb_ref[...] = silu.astype(BF16)
        hid_ref[...] = (silu * b).astype(BF16)

    w_spec = pl.BlockSpec((None, D, Fj), lambda g, i: (g, 0, 0))
    o_spec = pl.BlockSpec((tm, Fj), lambda g, i: (i, g))
    out = jax.ShapeDtypeStruct((S, G * Fj), BF16)
    return pl.pallas_call(body, name=name, grid=(G, S // tm),
                          in_specs=[pl.BlockSpec((tm, D), lambda g, i: (i, 0)), w_spec, w_spec],
                          out_specs=[o_spec, o_spec, o_spec], out_shape=[out, out, out], compiler_params=_params())(h, wg, wu)


def _ffn_dact(dxh, wd, ga, gb, name, tm=512):
    S, D = dxh.shape
    G, Fj, _ = wd.shape
    tm = _tile(S, tm)

    def body(dx_ref, wd_ref, ga_ref, gb_ref, da_ref, db_ref):
        dhid = _nt(dx_ref[...], wd_ref[...])
        da_ref[...] = (dhid * ga_ref[...].astype(F32)).astype(BF16)
        db_ref[...] = (dhid * gb_ref[...].astype(F32)).astype(BF16)

    blk = pl.BlockSpec((tm, Fj), lambda g, i: (i, g))
    out = jax.ShapeDtypeStruct((S, G * Fj), BF16)
    return pl.pallas_call(body, name=name, grid=(G, S // tm),
                          in_specs=[pl.BlockSpec((tm, D), lambda g, i: (i, 0)),
                                    pl.BlockSpec((None, Fj, D), lambda g, i: (g, 0, 0)), blk, blk],
                          out_specs=[blk, blk], out_shape=[out, out], compiler_params=_params())(dxh, wd, ga, gb)


def _ffn_dh(da, db, wg, wu, name, dep=(), tm=512):
    S = da.shape[0]
    G, D, Fj = wg.shape
    tm = _tile(S, tm)

    def body(da_ref, db_ref, wg_ref, wu_ref, *rest):
        o_ref = rest[-1]
        part = _nt(da_ref[...], wg_ref[...]) + _nt(db_ref[...], wu_ref[...])

        @pl.when(pl.program_id(1) == 0)
        def _():
            o_ref[...] = part

        @pl.when(pl.program_id(1) > 0)
        def _():
            o_ref[...] += part

    act = pl.BlockSpec((tm, Fj), lambda i, g: (i, g))
    w_spec = pl.BlockSpec((None, D, Fj), lambda i, g: (g, 0, 0))
    in_specs = [act, act, w_spec, w_spec] + [pl.BlockSpec(d.shape, lambda i, g: (0, 0)) for d in dep]
    return pl.pallas_call(body, name=name, grid=(S // tm, G), in_specs=in_specs,
                          out_specs=pl.BlockSpec((tm, D), lambda i, g: (i, 0)),
                          out_shape=jax.ShapeDtypeStruct((S, D), F32), compiler_params=_params(),
                          cost_estimate=_matmul_cost(S, D, 2 * G * Fj, (da, db, wg, wu), jax.ShapeDtypeStruct((S, D), F32)),
                          )(da, db, wg, wu, *dep)


def _pool_fwd(proj, pool_w, pool_scale, name, tm=512):
    S = proj.shape[0]
    NG, C, _ = pool_w.shape
    DP = NG * C
    tm = _tile(S, tm)
    hb = tm // POOL_HALO
    n_ext = tm + POOL_HALO

    def body(u_ref, halo_ref, w_ref, sc_ref, y_ref, d_ref):
        i = pl.program_id(0)
        t = lax.broadcasted_iota(jnp.int32, (tm, 1), 0) + i * tm
        for g, win in enumerate(POOL_WINDOWS):
            cols = slice(g * C, (g + 1) * C)
            ug = u_ref[:, cols]
            halo = jnp.where(i > 0, halo_ref[:, cols], 0.0)
            acc = jnp.concatenate([halo, ug], axis=0)
            step = 1
            while step < win:
                acc = acc + pltpu.roll(acc, step, 0)
                step *= 2
            count = jnp.minimum(t + 1, win).astype(F32)
            d = (acc[POOL_HALO:, :] / count - ug).astype(BF16)
            d_ref[:, cols] = d
            y_ref[:, cols] = (_nn(d, w_ref[g]) * sc_ref[:, cols]).astype(BF16)

    del n_ext
    return pl.pallas_call(
        body, name=name, grid=(S // tm,),
        in_specs=[pl.BlockSpec((tm, DP), lambda i: (i, 0)),
                  pl.BlockSpec((POOL_HALO, DP), lambda i: (jnp.maximum(i * hb - 1, 0), 0)),
                  pl.BlockSpec((NG, C, C), lambda i: (0, 0, 0)), pl.BlockSpec((1, DP), lambda i: (0, 0))],
        out_specs=[pl.BlockSpec((tm, DP), lambda i: (i, 0)), pl.BlockSpec((tm, DP), lambda i: (i, 0))],
        out_shape=[jax.ShapeDtypeStruct((S, DP), BF16), jax.ShapeDtypeStruct((S, DP), BF16)],
        compiler_params=_params())(proj, proj, pool_w, pool_scale)


def _pool_bwd(dymix, d, pool_w, pool_scale, name, tm=512):
    S = dymix.shape[0]
    NG, C, _ = pool_w.shape
    DP = NG * C
    tm = _tile(S, tm)
    hb = tm // POOL_HALO
    nb = S // tm
    n_ext = tm + POOL_HALO
    last_halo = S // POOL_HALO - 1

    def body(dy_ref, halo_ref, d_ref, w_ref, sc_ref, du_ref, dw_ref, dsc_ref):
        i = pl.program_id(0)
        t = lax.broadcasted_iota(jnp.int32, (n_ext, 1), 0) + i * tm
        for g, win in enumerate(POOL_WINDOWS):
            cols = slice(g * C, (g + 1) * C)
            dy = dy_ref[:, cols]
            halo = jnp.where(i < nb - 1, halo_ref[:, cols], 0.0)
            sc = sc_ref[:, cols]
            dv = d_ref[:, cols]
            e_ext = (jnp.concatenate([dy, halo], axis=0) * sc).astype(BF16)
            dd = _nt(e_ext, w_ref[g])
            count = jnp.minimum(t + 1, win).astype(F32)
            acc = dd / count
            step = 1
            while step < win:
                acc = acc + pltpu.roll(acc, n_ext - step, 0)
                step *= 2
            du_ref[:, cols] = (acc[:tm, :] - dd[:tm, :]).astype(BF16)
            dw = _tn(dv, e_ext[:tm, :])
            dsc = jnp.sum(dy * _nn(dv, w_ref[g]), axis=0, keepdims=True)

            @pl.when(i == 0)
            def _():
                dw_ref[g] = dw
                dsc_ref[:, cols] = dsc

            @pl.when(i > 0)
            def _():
                dw_ref[g] += dw
                dsc_ref[:, cols] += dsc

    return pl.pallas_call(
        body, name=name, grid=(nb,),
        in_specs=[pl.BlockSpec((tm, DP), lambda i: (i, 0)),
                  pl.BlockSpec((POOL_HALO, DP), lambda i: (jnp.minimum((i + 1) * hb, last_halo), 0)),
                  pl.BlockSpec((tm, DP), lambda i: (i, 0)),
                  pl.BlockSpec((NG, C, C), lambda i: (0, 0, 0)), pl.BlockSpec((1, DP), lambda i: (0, 0))],
        out_specs=[pl.BlockSpec((tm, DP), lambda i: (i, 0)), pl.BlockSpec((NG, C, C), lambda i: (0, 0, 0)),
                   pl.BlockSpec((1, DP), lambda i: (0, 0))],
        out_shape=[jax.ShapeDtypeStruct((S, DP), BF16), jax.ShapeDtypeStruct((NG, C, C), F32),
                   jax.ShapeDtypeStruct((1, DP), F32)],
        compiler_params=_params())(dymix, dymix, d, pool_w, pool_scale)


def _chunk_scan(v, rows, reverse):
    n = v.shape[0]
    step = 1
    while step < CHUNK:
        if reverse:
            v = v + jnp.where(rows < CHUNK - step, pltpu.roll(v, n - step, 0), 0.0)
        else:
            v = v + jnp.where(rows >= step, pltpu.roll(v, step, 0), 0.0)
        step *= 2
    return v


def _log_decay(alr, w_a2, b_a):
    z = _nn(alr.astype(BF16), w_a2) + b_a
    la = (jnp.minimum(z, 0.0) - jnp.log(1.0 + jnp.exp(-jnp.abs(z)))) * (1.0 / GATE_TEMP)
    return z, la


def _gla_specs(DP, DKT, DV, tb, bmap):
    return [pl.BlockSpec((tb, DKT), lambda i: (bmap(i), DP // DKT)),
            pl.BlockSpec((tb, DKT), lambda i: (bmap(i), DP // DKT + 1)),
            pl.BlockSpec((tb, DV), lambda i: (bmap(i), (DP + 2 * DKT) // DV)),
            pl.BlockSpec((tb, DV), lambda i: (bmap(i), (DP + 2 * DKT) // DV + 1)),
            pl.BlockSpec((tb, LANES), lambda i: (bmap(i), (DP + 2 * DKT + 2 * DV) // LANES))]


def _gla_fwd(proj, y_pool, w_a2, b_a, head_norm, name, tb=512):
    S = proj.shape[0]
    DP = y_pool.shape[1]
    DKT = b_a.shape[1]
    DV = head_norm.shape[1]
    dk, dv = DKT // N_HEADS, DV // N_HEADS
    tb = _tile(S, tb)
    ncb = tb // CHUNK
    qscale = dk ** -0.5

    def body(q_ref, k_ref, v_ref, g_ref, alr_ref, yp_ref, wa_ref, ba_ref, hn_ref, y_ref, st_out_ref, st_ref, kdec_ref,
             gam_ref):
        @pl.when(pl.program_id(0) == 0)
        def _():
            st_ref[...] = jnp.zeros_like(st_ref)

        y_ref[:, :DP] = yp_ref[...]

        rows = lax.broadcasted_iota(jnp.int32, (tb, 1), 0) % CHUNK
        _, la = _log_decay(alr_ref[...], wa_ref[...], ba_ref[...])
        tail = _chunk_scan(la, rows, True)
        kdec_ref[...] = k_ref[...] * jnp.exp(tail - la)
        gam_ref[...] = jnp.exp(tail)

        def chunk(c, carry):
            r0 = pl.multiple_of(c * CHUNK, CHUNK)
            rs = pl.ds(r0, CHUNK)
            gam = gam_ref[pl.ds(r0, 1), :]
            heads = range(N_HEADS)
            kcs = [slice(h * dk, (h + 1) * dk) for h in heads]
            vcs = [slice(h * dv, (h + 1) * dv) for h in heads]
            upd = [_tn(v_ref[rs, vcs[h]].astype(BF16), kdec_ref[rs, kcs[h]].astype(BF16)) for h in heads]
            st = [st_ref[h] * gam[:, kcs[h]] + upd[h] for h in heads]
            o = [_nt((q_ref[rs, kcs[h]] * qscale).astype(BF16), st[h].astype(BF16)) for h in heads]
            for h in heads:
                st_ref[h] = st[h]
                st_out_ref[c, h] = st[h]
                r = lax.rsqrt(jnp.mean(o[h] * o[h], axis=-1, keepdims=True) + RMS_EPS)
                gv = g_ref[rs, vcs[h]]
                y_ref[rs, DP + h * dv:DP + (h + 1) * dv] = (o[h] * r * hn_ref[:, vcs[h]] * (gv * _sigmoid(gv))).astype(BF16)
            return carry

        lax.fori_loop(0, ncb, chunk, 0, unroll=2)

    full = lambda shape: pl.BlockSpec(shape, lambda i: (0,) * len(shape))
    return pl.pallas_call(
        body, name=name, grid=(S // tb,),
        in_specs=_gla_specs(DP, DKT, DV, tb, lambda i: i) + [pl.BlockSpec((tb, DP), lambda i: (i, 0)),
                                                            full((LANES, DKT)), full((1, DKT)), full((1, DV))],
        out_specs=[pl.BlockSpec((tb, DP + DV), lambda i: (i, 0)),
                   pl.BlockSpec((ncb, N_HEADS, dv, dk), lambda i: (i, 0, 0, 0))],
        out_shape=[jax.ShapeDtypeStruct((S, DP + DV), BF16), jax.ShapeDtypeStruct((S // CHUNK, N_HEADS, dv, dk), F32)],
        scratch_shapes=[pltpu.VMEM((N_HEADS, dv, dk), F32), pltpu.VMEM((tb, DKT), F32), pltpu.VMEM((tb, DKT), F32)],
        compiler_params=_params())(proj, proj, proj, proj, proj, y_pool, w_a2, b_a, head_norm)


def _gla_bwd(proj, states, dymix, du, w_a2, b_a, head_norm, name, tb=512):
    S = proj.shape[0]
    DP = du.shape[1]
    DKT = b_a.shape[1]
    DV = head_norm.shape[1]
    dk, dv = DKT // N_HEADS, DV // N_HEADS
    tb = _tile(S, tb)
    ncb = tb // CHUNK
    nb = S // tb
    qscale = dk ** -0.5
    rev = lambda i: nb - 1 - i

    q0, k0, v0, g0, a0 = DP, DP + DKT, DP + 2 * DKT, DP + 2 * DKT + DV, DP + 2 * DKT + 2 * DV

    def body(q_ref, k_ref, v_ref, g_ref, alr_ref, st_blk_ref, st_prev_ref, dy_ref, du_ref, wa_ref, ba_ref, hn_ref,
             dp_ref, dwa_ref, dba_ref, dhn_ref,
             dst_ref, kdec_ref, dec_ref, gam_ref, e_ref, dla_ref, dhn_acc_ref):
        i = pl.program_id(0)
        blk = rev(i)
        dp_ref[:, :DP] = du_ref[...]

        @pl.when(i == 0)
        def _():
            dst_ref[...] = jnp.zeros_like(dst_ref)

        dhn_acc_ref[...] = jnp.zeros_like(dhn_acc_ref)
        rows = lax.broadcasted_iota(jnp.int32, (tb, 1), 0) % CHUNK
        z, la = _log_decay(alr_ref[...], wa_ref[...], ba_ref[...])
        tail = _chunk_scan(la, rows, True)
        dec_ref[...] = jnp.exp(tail - la)
        kdec_ref[...] = k_ref[...] * dec_ref[...]
        gam_ref[...] = jnp.exp(tail)

        def chunk(cc, carry):
            c = ncb - 1 - cc
            r0 = pl.multiple_of(c * CHUNK, CHUNK)
            rs = pl.ds(r0, CHUNK)
            gam = gam_ref[pl.ds(r0, 1), :]
            first = jnp.logical_and(blk == 0, c == 0)
            heads = range(N_HEADS)
            kcs = [slice(h * dk, (h + 1) * dk) for h in heads]
            vcs = [slice(h * dv, (h + 1) * dv) for h in heads]
            qs = [(q_ref[rs, kcs[h]] * qscale).astype(BF16) for h in heads]
            stb = [st_blk_ref[c, h].astype(BF16) for h in heads]
            o = [_nt(qs[h], stb[h]) for h in heads]
            do = []
            for h in heads:
                oh = o[h]
                r = lax.rsqrt(jnp.mean(oh * oh, axis=-1, keepdims=True) + RMS_EPS)
                gv = g_ref[rs, vcs[h]]
                sg = _sigmoid(gv)
                dy = dy_ref[rs, vcs[h]]
                hn = hn_ref[:, vcs[h]]
                on = oh * r
                dp_ref[rs, g0 + h * dv:g0 + (h + 1) * dv] = (dy * on * hn * (sg * (1.0 + gv * (1.0 - sg)))).astype(BF16)
                don = dy * (gv * sg)
                dhn_acc_ref[:, vcs[h]] += jnp.sum(don * on, axis=0, keepdims=True)
                dn = don * hn
                do.append((r * dn - oh * (r * r * r) * jnp.mean(dn * oh, axis=-1, keepdims=True)).astype(BF16))
            dqs = [_nn(do[h], stb[h]) for h in heads]
            dst = [dst_ref[h] + _tn(do[h], qs[h]) for h in heads]
            for h in heads:
                dp_ref[rs, q0 + h * dk:q0 + (h + 1) * dk] = (dqs[h] * qscale).astype(BF16)
            dstb = [dst[h].astype(BF16) for h in heads]
            dvh = [_nt(kdec_ref[rs, kcs[h]].astype(BF16), dstb[h]) for h in heads]
            dkdec = [_nn(v_ref[rs, vcs[h]].astype(BF16), dstb[h]) for h in heads]
            gdg = []
            for h in heads:
                dp_ref[rs, v0 + h * dv:v0 + (h + 1) * dv] = dvh[h].astype(BF16)
                dp_ref[rs, k0 + h * dk:k0 + (h + 1) * dk] = (dkdec[h] * dec_ref[rs, kcs[h]]).astype(BF16)
                e_ref[rs, kcs[h]] = dkdec[h] * kdec_ref[rs, kcs[h]]
                st_prev = jnp.where(c > 0, st_blk_ref[jnp.maximum(c - 1, 0), h], st_prev_ref[0, h])
                st_prev = jnp.where(first, 0.0, st_prev)
                gdg.append(jnp.sum(dst[h] * st_prev, axis=0, keepdims=True) * gam[:, kcs[h]])
                dst_ref[h] = dst[h] * gam[:, kcs[h]]
            dla_ref[rs, :] = jnp.broadcast_to(jnp.concatenate(gdg, axis=1), (CHUNK, DKT))
            return carry

        lax.fori_loop(0, ncb, chunk, 0, unroll=2)

        ev = e_ref[...]
        dla = dla_ref[...] + _chunk_scan(ev, rows, False) - ev
        dz = dla * (1.0 / GATE_TEMP) * (1.0 - _sigmoid(z))
        dzb = dz.astype(BF16)
        dp_ref[:, a0:a0 + LANES] = _nt(dzb, wa_ref[...]).astype(BF16)
        dwa = _tn(alr_ref[...].astype(BF16), dzb)
        dba = jnp.sum(dz, axis=0, keepdims=True)

        @pl.when(i == 0)
        def _():
            dwa_ref[...] = dwa
            dba_ref[...] = dba
            dhn_ref[...] = dhn_acc_ref[...]

        @pl.when(i > 0)
        def _():
            dwa_ref[...] += dwa
            dba_ref[...] += dba
            dhn_ref[...] += dhn_acc_ref[...]

    full = lambda shape: pl.BlockSpec(shape, lambda i: (0,) * len(shape))
    rowblk = lambda w: pl.BlockSpec((tb, w), lambda i: (rev(i), 0))
    return pl.pallas_call(
        body, name=name, grid=(nb,),
        in_specs=_gla_specs(DP, DKT, DV, tb, rev) + [
            pl.BlockSpec((ncb, N_HEADS, dv, dk), lambda i: (rev(i), 0, 0, 0)),
            pl.BlockSpec((1, N_HEADS, dv, dk), lambda i: (jnp.maximum(rev(i) * ncb - 1, 0), 0, 0, 0)),
            pl.BlockSpec((tb, DV), lambda i: (rev(i), DP // DV)), rowblk(DP),
            full((LANES, DKT)), full((1, DKT)), full((1, DV))],
        out_specs=[rowblk(a0 + LANES), full((LANES, DKT)), full((1, DKT)), full((1, DV))],
        out_shape=[jax.ShapeDtypeStruct((S, a0 + LANES), BF16), jax.ShapeDtypeStruct((LANES, DKT), F32),
                   jax.ShapeDtypeStruct((1, DKT), F32), jax.ShapeDtypeStruct((1, DV), F32)],
        scratch_shapes=[pltpu.VMEM((N_HEADS, dv, dk), F32)] + [pltpu.VMEM((tb, DKT), F32)] * 5 + [pltpu.VMEM((1, DV), F32)],
        compiler_params=_params())(proj, proj, proj, proj, proj, states, states, dymix, du, w_a2, b_a, head_norm)


def _xattn_fwd(q, kv, name, tm=512):
    S, D = q.shape
    M = kv.shape[0]
    hd = D // N_HEADS
    tm = _tile(S, tm)
    scale = hd ** -0.5

    def body(q_ref, k_ref, v_ref, o_ref):
        for h in range(N_HEADS):
            hc = slice(h * hd, (h + 1) * hd)
            s = _nt(q_ref[:, hc], k_ref[:, hc]) * scale
            p = jnp.exp(s - jnp.max(s, axis=-1, keepdims=True))
            p = p / jnp.sum(p, axis=-1, keepdims=True)
            o_ref[:, hc] = _nn(p.astype(BF16), v_ref[:, hc]).astype(BF16)

    return pl.pallas_call(body, name=name, grid=(S // tm,),
                          in_specs=[pl.BlockSpec((tm, D), lambda i: (i, 0)), pl.BlockSpec((M, D), lambda i: (0, 0)),
                                    pl.BlockSpec((M, D), lambda i: (0, 1))],
                          out_specs=pl.BlockSpec((tm, D), lambda i: (i, 0)),
                          out_shape=jax.ShapeDtypeStruct((S, D), BF16), compiler_params=_params())(q, kv, kv)


def _xattn_bwd(q, kv, do, name, tm=512):
    S, D = q.shape
    M = kv.shape[0]
    hd = D // N_HEADS
    tm = _tile(S, tm)
    scale = hd ** -0.5

    def body(q_ref, k_ref, v_ref, do_ref, dq_ref, dkv_ref):
        first = pl.program_id(0) == 0
        for h in range(N_HEADS):
            hc = slice(h * hd, (h + 1) * hd)
            vcols = slice(D + h * hd, D + (h + 1) * hd)
            qh = q_ref[:, hc]
            kh = k_ref[:, hc]
            doh = do_ref[:, hc]
            s = _nt(qh, kh) * scale
            p = jnp.exp(s - jnp.max(s, axis=-1, keepdims=True))
            p = p / jnp.sum(p, axis=-1, keepdims=True)
            dvh = _tn(p.astype(BF16), doh)
            dp = _nt(doh, v_ref[:, hc])
            ds = ((p * (dp - jnp.sum(dp * p, axis=-1, keepdims=True))) * scale).astype(BF16)
            dq_ref[:, hc] = _nn(ds, kh).astype(BF16)
            dkh = _tn(ds, qh)

            @pl.when(first)
            def _():
                dkv_ref[:, hc] = dkh
                dkv_ref[:, vcols] = dvh

            @pl.when(jnp.logical_not(first))
            def _():
                dkv_ref[:, hc] += dkh
                dkv_ref[:, vcols] += dvh

    row = pl.BlockSpec((tm, D), lambda i: (i, 0))
    return pl.pallas_call(body, name=name, grid=(S // tm,),
                          in_specs=[row, pl.BlockSpec((M, D), lambda i: (0, 0)), pl.BlockSpec((M, D), lambda i: (0, 1)), row],
                          out_specs=[row, pl.BlockSpec((M, 2 * D), lambda i: (0, 0))],
                          out_shape=[jax.ShapeDtypeStruct((S, D), BF16), jax.ShapeDtypeStruct((M, 2 * D), F32)],
                          compiler_params=_params())(q, kv, kv, do)


def _local_step(x, mem, target, vec, weight, emit, dep0):
    DP = vec["pool_scale"].shape[1]
    g = {}
    pending = []
    begun = []
    summed = []
    emit_begin, emit_finish, emit_send, early_update = emit

    def behind(fn, *a, **kw):
        dep = tuple(pending)
        pending.clear()
        out = fn(*a, dep=dep, **kw)
        while summed:
            pending.append(emit_send(summed.pop(0)))
        while begun:
            name = begun.pop(0)
            token = emit_finish(name, out)
            if token is None:
                summed.append(name)
            else:
                pending.append(token)
        return out

    def mm(a, b, **kw):
        return behind(_matmul, a, b, **kw)

    def send(name, gfull):
        pending.append(emit_begin(name, gfull))
        begun.append(name)

    def ffn_fwd(xin, tag, dep):
        h = _rms_fwd(xin, vec[f"{tag}_norm"], f"{tag}_norm", dep=dep)
        ga, gb, hid = _ffn_up(h, weight(f"{tag}_w_gate", h), weight(f"{tag}_w_up", h), f"{tag}_up")
        wd = weight(f"{tag}_w_down", hid)
        G, Fj, D = wd.shape
        xo = _matmul(hid, wd.reshape(G * Fj, D), mode="nn", name=f"{tag}_down", out_dtype=F32, res=xin, scale=0.5,
                     tn=1024, tk=G * Fj)
        return xo, (h, ga, gb, hid)

    def ffn_bwd(dxh, saved, tag, last):
        h, ga, gb, hid = saved
        wg, wu, wd = weight(f"{tag}_w_gate"), weight(f"{tag}_w_up"), weight(f"{tag}_w_down")
        G, Fj, D = wd.shape
        send(f"{tag}_w_down", mm(hid, dxh, mode="tn", name=f"{tag}_dwd", out_dtype=F32, tm=Fj, tn=1024))
        da, db = _ffn_dact(dxh, wd, ga, gb, f"{tag}_dact")
        send(f"{tag}_w_gate", mm(h, da, mode="tn", name=f"{tag}_dwg", out_dtype=F32, tm=1024, tn=Fj, out_groups=G))
        dwu = mm(h, db, mode="tn", name=f"{tag}_dwu", out_dtype=F32, tm=1024, tn=Fj, out_groups=G)
        if last:
            pending.append(emit_begin(f"{tag}_w_up", dwu))
            pending.append(emit_finish(f"{tag}_w_up", pending[-1]))
        else:
            send(f"{tag}_w_up", dwu)
        return behind(_ffn_dh, da, db, wg, wu, f"{tag}_dh")

    x1, ffn1_saved = ffn_fwd(x, "ffn1", dep0)
    h2 = _rms_fwd(x1, vec["mix_norm"], "mix_norm")
    w_in = weight("w_in", h2)
    proj = _matmul(h2, w_in, mode="nn", name="w_in", out_dtype=F32, tn=1408)
    pool_w, w_a2 = weight("pool_w", h2), weight("gla_w_a2", h2)
    y_pool, dpool = _pool_fwd(proj, pool_w, vec["pool_scale"], "pool_fwd")
    ymix, states = _gla_fwd(proj, y_pool, w_a2, vec["gla_b_a"], vec["gla_head_norm"], "gla_fwd")
    w_out = weight("w_out", ymix)
    x2 = _matmul(ymix, w_out, mode="nn", name="w_out", out_dtype=F32, res=x1)
    h3 = _rms_fwd(x2, vec["xattn_norm"], "xattn_norm")
    mh = _rms_fwd(mem, vec["mem_norm"], "mem_norm")
    w_q = weight("xattn_w_q", h3)
    q = _matmul(h3, w_q, mode="nn", name="xattn_q", out_dtype=BF16)
    w_kv = weight("xattn_w_kv", q)
    kv = _matmul(mh, w_kv, mode="nn", name="xattn_kv", out_dtype=BF16, b_groups=True, tn=1024)
    o = _xattn_fwd(q, kv, "xattn_fwd")
    w_o = weight("xattn_w_o", o)
    x3 = _matmul(o, w_o, mode="nn", name="xattn_o", out_dtype=F32, res=x2)
    x4, ffn2_saved = ffn_fwd(x3, "ffn2", None)
    sq, dx4, dx4h, g["final_norm"] = _loss_head(x4, vec["final_norm"], target, "loss_head")

    dh = ffn_bwd(dx4h, ffn2_saved, "ffn2", False)
    dx3, dx3b, g["ffn2_norm"] = _rms_bwd(x3, vec["ffn2_norm"], dh, dx4, "ffn2_norm_bwd", lowp=1.0)
    send("xattn_w_o", mm(o, dx3b, mode="tn", name="xattn_dwo", out_dtype=F32, tm=1024, tn=1024))
    do = mm(dx3b, w_o, mode="nt", name="xattn_do", out_dtype=BF16)
    dq, dkv = _xattn_bwd(q, kv, do, "xattn_bwd")
    send("xattn_w_q", mm(h3, dq, mode="tn", name="xattn_dwq", out_dtype=F32, tm=1024, tn=1024))
    dh3 = mm(dq, w_q, mode="nt", name="xattn_dh", out_dtype=F32)
    dkvb = _cast(dkv, BF16, "dkv_cast")
    send("xattn_w_kv", mm(mh, dkvb, mode="tn", name="xattn_dwkv", out_dtype=F32, tm=1024, tn=1024, out_groups=N_SHARDS))
    dmh = mm(dkvb, w_kv, mode="nt", name="xattn_dmh", out_dtype=F32, b_groups=True, tk=1024)
    _, g["mem_norm"] = _rms_bwd(mem, vec["mem_norm"], dmh, None, "mem_norm_bwd")
    pending.append(g["mem_norm"])
    dx2, dx2b, g["xattn_norm"] = _rms_bwd(x2, vec["xattn_norm"], dh3, dx3, "xattn_norm_bwd", lowp=1.0)
    send("w_out", mm(ymix, dx2b, mode="tn", name="dw_out", out_dtype=F32, tm=1024, tn=1024))
    dymix = mm(dx2b, w_out, mode="nt", name="dymix", out_dtype=F32)
    du, dpool_w, g["pool_scale"] = _pool_bwd(dymix, dpool, pool_w, vec["pool_scale"], "pool_bwd")
    send("pool_w", dpool_w)
    dproj, dw_a2, g["gla_b_a"], g["gla_head_norm"] = _gla_bwd(
        proj, states, dymix, du, w_a2, vec["gla_b_a"], vec["gla_head_norm"], "gla_bwd")
    send("gla_w_a2", dw_a2)
    send("w_in", mm(h2, dproj, mode="tn", name="dw_in", out_dtype=F32, tm=1024, tn=1408))
    dh2 = mm(dproj, w_in, mode="nt", name="dh2", out_dtype=F32, tn=1024, tk=dproj.shape[1])
    pending.extend(early_update(dh2))
    dx1, dx1h, g["mix_norm"] = _rms_bwd(x1, vec["mix_norm"], dh2, dx2, "mix_norm_bwd", lowp=0.5)
    dh = ffn_bwd(dx1h, ffn1_saved, "ffn1", True)
    dx0, g["ffn1_norm"] = _rms_bwd(x, vec["ffn1_norm"], dh, dx1, "ffn1_norm_bwd")
    return sq, dx0, g


def _place():
    x, y, c = lax.axis_index("x"), lax.axis_index("y"), lax.axis_index("c")
    chips = [(1 - x, y), (x, 1 - y), (1 - x, 1 - y)]
    return x, y, c, chips


def _ids():
    return jnp.stack([2 * lax.axis_index("x") + lax.axis_index("y"), lax.axis_index("c")]).astype(jnp.int32)


def _hbm(a):
    return pltpu.with_memory_space_constraint(a, pltpu.HBM)


def _cast_to_slot(w2d, dtype, name, dep=None):
    R, C = w2d.shape
    tr = _tile(R, max(16, (4 << 20) // (4 * C) // 16 * 16))

    def body(i_ref, w_ref, *rest):
        rest[-1][...] = w_ref[...].astype(dtype)

    in_specs = [pl.BlockSpec((tr, C), lambda r, i: (r, 0))]
    operands = [w2d]
    if dep is not None:
        in_specs.append(pl.BlockSpec(dep.shape, lambda r, i: (0, 0)))
        operands.append(dep)
    grid_spec = pltpu.PrefetchScalarGridSpec(num_scalar_prefetch=1, grid=(R // tr,), in_specs=in_specs,
                                             out_specs=pl.BlockSpec((None, tr, C), lambda r, i: (i[0], r, 0)))
    return pl.pallas_call(body, name=name, grid_spec=grid_spec, out_shape=jax.ShapeDtypeStruct((N_SHARDS, R, C), dtype),
                          compiler_params=_params())(_ids(), *operands)


def _gather_copies(buf_ref, send_sems, recv_sems, incoming, whole):
    x, y, c, chips = _place()
    hr = buf_ref.shape[1] // 2
    copies = []
    for j, (px, py) in enumerate(chips):
        slot = 2 * px + py if incoming else 2 * x + y
        part = buf_ref.at[slot] if whole else buf_ref.at[slot, pl.ds(c * hr, hr), :]
        copies.append(pltpu.make_async_remote_copy(src_ref=part, dst_ref=part, send_sem=send_sems.at[j],
                                                   recv_sem=recv_sems.at[j], device_id=(px, py, c), device_id_type=MESH))
    return copies


def _gather_start(buf, name, whole):
    def body(b_ref, send_sems, recv_sems, b_thru, token):
        for cp in _gather_copies(b_ref, send_sems, recv_sems, False, whole):
            cp.start()
        token[...] = jnp.zeros_like(token)

    return pl.pallas_call(
        body, name=name,
        out_shape=(pltpu.SemaphoreType.DMA((3,)), pltpu.SemaphoreType.DMA((3,)), pltpu.HBM(buf.shape, buf.dtype),
                   jax.ShapeDtypeStruct((8, LANES), F32)),
        in_specs=(HBM,), out_specs=(SEM, SEM, HBM, pl.BlockSpec(memory_space=pltpu.VMEM)), input_output_aliases={0: 2},
        compiler_params=pltpu.CompilerParams(has_side_effects=EFFECT))(_hbm(buf))


def _gather_wait(send_sems, recv_sems, buf_thru, after, name, whole):
    def body(b_ref, send_sems, recv_sems, after_ref, b_out):
        for cp in _gather_copies(b_ref, send_sems, recv_sems, False, whole):
            cp.wait_send()
        for cp in _gather_copies(b_ref, send_sems, recv_sems, True, whole):
            cp.wait_recv()

    return pl.pallas_call(
        body, name=name, out_shape=pltpu.HBM(buf_thru.shape, buf_thru.dtype),
        in_specs=(HBM, SEM, SEM, ANY), out_specs=HBM, input_output_aliases={0: 0},
        compiler_params=pltpu.CompilerParams(has_side_effects=EFFECT))(buf_thru, send_sems, recv_sems, after)


def _gather_forward(buf, name):
    G, R, C = buf.shape
    hr = R // 2

    def body(b_ref, o_ref, send_sems, recv_sems):
        x, y, c, chips = _place()
        copies = []
        for j, (px, py) in enumerate(chips):
            half = o_ref.at[2 * px + py, pl.ds(c * hr, hr), :]
            copies.append(pltpu.make_async_remote_copy(src_ref=half, dst_ref=half, send_sem=send_sems.at[j],
                                                       recv_sem=recv_sems.at[j], device_id=(x, y, 1 - c),
                                                       device_id_type=MESH))
        for cp in copies:
            cp.start()
        for j, (px, py) in enumerate(chips):
            half = o_ref.at[2 * px + py, pl.ds((1 - c) * hr, hr), :]
            pltpu.make_async_remote_copy(src_ref=half, dst_ref=half, send_sem=send_sems.at[j], recv_sem=recv_sems.at[j],
                                         device_id=(x, y, 1 - c), device_id_type=MESH).wait_recv()
        for cp in copies:
            cp.wait_send()

    return pl.pallas_call(body, name=name, in_specs=[ANY], out_specs=ANY, out_shape=jax.ShapeDtypeStruct(buf.shape, buf.dtype),
                          input_output_aliases={0: 0},
                          scratch_shapes=[pltpu.SemaphoreType.DMA((3,)), pltpu.SemaphoreType.DMA((3,))])(buf)


def _pair_copy(g_ref, land_ref, send_sem, recv_sem):
    x, y, c, _ = _place()
    hr = g_ref.shape[1] // 2
    return pltpu.make_async_remote_copy(src_ref=g_ref.at[:, pl.ds((1 - c) * hr, hr), :], dst_ref=land_ref,
                                        send_sem=send_sem, recv_sem=recv_sem, device_id=(x, y, 1 - c), device_id_type=MESH)


def _pair_start(gfull, name):
    G, R, C = gfull.shape

    def body(g_ref, land_ref, send_sem, recv_sem, g_thru, land_thru, token):
        _pair_copy(g_ref, land_ref, send_sem, recv_sem).start()
        token[...] = jnp.zeros_like(token)

    return pl.pallas_call(
        body, name=name,
        out_shape=(pltpu.SemaphoreType.DMA(()), pltpu.SemaphoreType.DMA(()), pltpu.HBM(gfull.shape, F32),
                   pltpu.HBM((G, R // 2, C), F32), jax.ShapeDtypeStruct((8, LANES), F32)),
        in_specs=(HBM, HBM), out_specs=(SEM, SEM, HBM, HBM, pl.BlockSpec(memory_space=pltpu.VMEM)),
        input_output_aliases={0: 2, 1: 3},
        compiler_params=pltpu.CompilerParams(has_side_effects=EFFECT))(_hbm(gfull), _hbm(lax.empty((G, R // 2, C), F32)))


def _pair_wait(send_sem, recv_sem, g_thru, land_thru, after, name):
    def body(g_ref, land_ref, send_sem, recv_sem, after_ref, g_out, land_out):
        cp = _pair_copy(g_ref, land_ref, send_sem, recv_sem)
        cp.wait_send()
        cp.wait_recv()

    return pl.pallas_call(
        body, name=name, out_shape=(pltpu.HBM(g_thru.shape, F32), pltpu.HBM(land_thru.shape, F32)),
        in_specs=(HBM, HBM, SEM, SEM, ANY), out_specs=(HBM, HBM), input_output_aliases={0: 0, 1: 1},
        compiler_params=pltpu.CompilerParams(has_side_effects=EFFECT))(g_thru, land_thru, send_sem, recv_sem, after)


def _pair_add(gfull, other, name):
    G, R, C = gfull.shape
    hr = R // 2
    tr = _tile(hr, max(8, (2 * 1024 * 1024) // (4 * C) // 8 * 8))
    nr = hr // tr
    c = lax.axis_index("c")
    cidx = jnp.reshape(c, (1,)).astype(jnp.int32)

    def body(c_ref, a_ref, b_ref, o_ref):
        o_ref[...] = a_ref[...] + b_ref[...]

    grid_spec = pltpu.PrefetchScalarGridSpec(
        num_scalar_prefetch=1, grid=(G, nr),
        in_specs=[pl.BlockSpec((None, tr, C), lambda g, r, cr: (g, cr[0] * nr + r, 0)),
                  pl.BlockSpec((None, tr, C), lambda g, r, cr: (g, r, 0))],
        out_specs=pl.BlockSpec((None, tr, C), lambda g, r, cr: (g, r, 0)))
    return pl.pallas_call(body, name=name, grid_spec=grid_spec, out_shape=jax.ShapeDtypeStruct((G, hr, C), F32),
                          compiler_params=_params())(cidx, gfull, other)


def _chip_copies(p_ref, land_ref, send_sems, recv_sems, incoming):
    x, y, c, chips = _place()
    me = 2 * x + y
    copies = []
    for j, (px, py) in enumerate(chips):
        dst = land_ref.at[2 * px + py] if incoming else land_ref.at[me]
        copies.append(pltpu.make_async_remote_copy(src_ref=p_ref.at[2 * px + py], dst_ref=dst, send_sem=send_sems.at[j],
                                                   recv_sem=recv_sems.at[j], device_id=(px, py, c), device_id_type=MESH))
    return copies


def _chip_start(part, name):
    def body(p_ref, land_ref, send_sems, recv_sems, p_thru, land_thru, token):
        for cp in _chip_copies(p_ref, land_ref, send_sems, recv_sems, False):
            cp.start()
        token[...] = jnp.zeros_like(token)

    return pl.pallas_call(
        body, name=name,
        out_shape=(pltpu.SemaphoreType.DMA((3,)), pltpu.SemaphoreType.DMA((3,)), pltpu.HBM(part.shape, F32),
                   pltpu.HBM(part.shape, F32), jax.ShapeDtypeStruct((8, LANES), F32)),
        in_specs=(HBM, HBM), out_specs=(SEM, SEM, HBM, HBM, pl.BlockSpec(memory_space=pltpu.VMEM)),
        input_output_aliases={0: 2, 1: 3},
        compiler_params=pltpu.CompilerParams(has_side_effects=EFFECT))(_hbm(part), _hbm(lax.empty(part.shape, F32)))


def _chip_wait(send_sems, recv_sems, p_thru, land_thru, after, name):
    def body(p_ref, land_ref, send_sems, recv_sems, after_ref, p_out, land_out):
        for cp in _chip_copies(p_ref, land_ref, send_sems, recv_sems, False):
            cp.wait_send()
        for cp in _chip_copies(p_ref, land_ref, send_sems, recv_sems, True):
            cp.wait_recv()

    return pl.pallas_call(
        body, name=name, out_shape=(pltpu.HBM(p_thru.shape, F32), pltpu.HBM(p_thru.shape, F32)),
        in_specs=(HBM, HBM, SEM, SEM, ANY), out_specs=(HBM, HBM), input_output_aliases={0: 0, 1: 1},
        compiler_params=pltpu.CompilerParams(has_side_effects=EFFECT))(p_thru, land_thru, send_sems, recv_sems, after)


def _chip_sum(part, slots, name):
    G, R2, C = part.shape
    tr = _tile(R2, max(8, (1 << 20) // (4 * C) // 8 * 8))
    nr = R2 // tr

    def body(i_ref, p_ref, *rest):
        o_ref = rest[-1]
        acc = None
        for u in range(G):
            val = jnp.where(i_ref[0] == u, p_ref[...], rest[u][...])
            acc = val if acc is None else acc + val
        o_ref[...] = acc

    def slot_spec(u):
        return pl.BlockSpec((None, tr, C), lambda r, i: (jnp.where(i[0] == u, (u + 1) % G, u), r, 0))

    grid_spec = pltpu.PrefetchScalarGridSpec(
        num_scalar_prefetch=1, grid=(nr,),
        in_specs=[pl.BlockSpec((None, tr, C), lambda r, i: (i[0], r, 0))] + [slot_spec(u) for u in range(G)],
        out_specs=pl.BlockSpec((tr, C), lambda r, i: (i[1] * nr + r, 0)))
    return pl.pallas_call(body, name=name, grid_spec=grid_spec, out_shape=jax.ShapeDtypeStruct((2 * R2, C), F32),
                          compiler_params=_params())(_ids(), part, slots, slots, slots, slots)


def _sum_slots(slots, name):
    G, R2, C = slots.shape
    tr = _tile(R2, max(8, (1024 * 1024) // (4 * C) // 8 * 8))

    def body(s_ref, o_ref):
        acc = s_ref[0]
        for u in range(1, G):
            acc = acc + s_ref[u]
        o_ref[...] = acc

    return pl.pallas_call(body, name=name, grid=(R2 // tr,), in_specs=[pl.BlockSpec((G, tr, C), lambda r: (0, r, 0))],
                          out_specs=pl.BlockSpec((tr, C), lambda r: (r, 0)), out_shape=jax.ShapeDtypeStruct((R2, C), F32),
                          compiler_params=_params())(slots)


def _pair_join(full, name):
    R, C = full.shape
    R2 = R // 2

    def body(f_ref, o_ref, token, send_sem, recv_sem):
        x, y, c, _ = _place()
        token[...] = jnp.zeros_like(token)
        mine = o_ref.at[pl.ds(c * R2, R2), :]
        theirs = o_ref.at[pl.ds((1 - c) * R2, R2), :]
        cp = pltpu.make_async_remote_copy(src_ref=mine, dst_ref=mine, send_sem=send_sem, recv_sem=recv_sem,
                                          device_id=(x, y, 1 - c), device_id_type=MESH)
        cp.start()
        pltpu.make_async_remote_copy(src_ref=theirs, dst_ref=theirs, send_sem=send_sem, recv_sem=recv_sem,
                                     device_id=(x, y, 1 - c), device_id_type=MESH).wait_recv()
        cp.wait_send()

    return pl.pallas_call(body, name=name, in_specs=[ANY], out_specs=[ANY, pl.BlockSpec(memory_space=pltpu.VMEM)],
                          out_shape=[jax.ShapeDtypeStruct((R, C), F32), jax.ShapeDtypeStruct((8, LANES), F32)],
                          input_output_aliases={0: 0},
                          scratch_shapes=[pltpu.SemaphoreType.DMA, pltpu.SemaphoreType.DMA])(full)


def _all_reduce_small(v, name):
    R, C = v.shape

    def gather_body(v_ref, out_ref, send_sems, recv_sems, local_sem):
        x, y, c, _ = _place()
        me = 4 * x + 2 * y + c
        mine = pltpu.make_async_copy(v_ref, out_ref.at[me], local_sem)
        mine.start()
        flips = [(fx, fy, fc) for fx in (0, 1) for fy in (0, 1) for fc in (0, 1)][1:]
        copies = []
        for j, (fx, fy, fc) in enumerate(flips):
            peer = (x ^ fx, y ^ fy, c ^ fc)
            copies.append(pltpu.make_async_remote_copy(src_ref=v_ref, dst_ref=out_ref.at[me], send_sem=send_sems.at[j],
                                                       recv_sem=recv_sems.at[j], device_id=peer, device_id_type=MESH))
        for cp in copies:
            cp.start()
        for j, (fx, fy, fc) in enumerate(flips):
            peer = (x ^ fx, y ^ fy, c ^ fc)
            pltpu.make_async_remote_copy(src_ref=v_ref, dst_ref=out_ref.at[4 * peer[0] + 2 * peer[1] + peer[2]],
                                         send_sem=send_sems.at[j], recv_sem=recv_sems.at[j], device_id=peer,
                                         device_id_type=MESH).wait_recv()
        for cp in copies:
            cp.wait_send()
        mine.wait()

    slots = pl.pallas_call(gather_body, name=name, in_specs=[ANY], out_specs=ANY,
                           out_shape=jax.ShapeDtypeStruct((8, R, C), F32),
                           scratch_shapes=[pltpu.SemaphoreType.DMA((7,)), pltpu.SemaphoreType.DMA((7,)),
                                           pltpu.SemaphoreType.DMA])(v)
    return _sum_slots(slots, f"{name}_sum")


def _adamw(w, g, m, v, name, dep=()):
    R, C = w.shape
    tr = _tile(R, max(8, (2 << 20) // (4 * C) // 8 * 8))
    bc1 = 1.0 - ADAM_B1 ** ADAM_STEP
    bc2 = 1.0 - ADAM_B2 ** ADAM_STEP

    def body(w_ref, g_ref, m_ref, v_ref, *rest):
        go_ref, d_ref, nm_ref, nv_ref = rest[len(dep):]
        gv = g_ref[...]
        go_ref[...] = gv
        nm = ADAM_B1 * m_ref[...] + (1.0 - ADAM_B1) * gv
        nv = ADAM_B2 * v_ref[...] + (1.0 - ADAM_B2) * (gv * gv)
        nm_ref[...] = nm
        nv_ref[...] = nv
        d_ref[...] = -ADAM_LR * ((nm / bc1) / (jnp.sqrt(nv / bc2) + ADAM_EPS) + ADAM_WD * w_ref[...])

    blk = pl.BlockSpec((tr, C), lambda r: (r, 0))
    out = jax.ShapeDtypeStruct((R, C), F32)
    in_specs = [blk] * 4 + [pl.BlockSpec(d.shape, lambda r: (0, 0)) for d in dep]
    return pl.pallas_call(body, name=name, grid=(R // tr,), in_specs=in_specs, out_specs=[blk] * 4, out_shape=[out] * 4,
                          compiler_params=_params())(w, g, m, v, *dep)


SC_TILES = 32
SC_LANES = 16
SC_ROWS = 8


def _sc_mesh():
    return plsc.VectorSubcoreMesh(core_axis_name="sc_core", subcore_axis_name="sc_subcore")


def _pair_add_sc(gfull, other, name):
    G, R, C = gfull.shape
    hr = R // 2
    tiles_per_shard = SC_TILES // G
    per_tile = hr // SC_ROWS // tiles_per_shard

    def body(g_hbm, o_hbm, out_hbm, gb, ob):
        c = lax.axis_index("c")
        tile = lax.axis_index("sc_subcore") * 2 + lax.axis_index("sc_core")
        t = tile // tiles_per_shard
        first = (tile % tiles_per_shard) * per_tile

        @pl.loop(0, per_tile)
        def _(k):
            rr = (first + k) * SC_ROWS
            pltpu.sync_copy(g_hbm.at[t, pl.ds(c * hr + rr, SC_ROWS), :], gb)
            pltpu.sync_copy(o_hbm.at[t, pl.ds(rr, SC_ROWS), :], ob)

            @pl.loop(0, SC_ROWS)
            def _(i):
                @pl.loop(0, C, step=SC_LANES)
                def _(j):
                    at = (i, pl.ds(j, SC_LANES))
                    gb[at] = gb[at] + ob[at]

            pltpu.sync_copy(gb, out_hbm.at[t, pl.ds(rr, SC_ROWS), :])

    buf = pltpu.VMEM((SC_ROWS, C), F32)
    return pl.kernel(body, name=name, out_type=jax.ShapeDtypeStruct((G, hr, C), F32), mesh=_sc_mesh(),
                     scratch_types=[buf, buf])(gfull, other)


def _adamw_sc(w, g, m, v, name):
    R, C = w.shape
    per_tile = R // SC_TILES
    bc1 = 1.0 - ADAM_B1 ** ADAM_STEP
    bc2 = 1.0 - ADAM_B2 ** ADAM_STEP

    def body(w_hbm, g_hbm, m_hbm, v_hbm, go_hbm, d_hbm, nm_hbm, nv_hbm, wb, gb, mb, vb):
        tile = lax.axis_index("sc_subcore") * 2 + lax.axis_index("sc_core")
        base = tile * per_tile

        @pl.loop(0, per_tile, step=SC_ROWS)
        def _(r):
            rows = pl.ds(base + r, SC_ROWS)
            pltpu.sync_copy(w_hbm.at[rows, :], wb)
            pltpu.sync_copy(g_hbm.at[rows, :], gb)
            pltpu.sync_copy(m_hbm.at[rows, :], mb)
            pltpu.sync_copy(v_hbm.at[rows, :], vb)

            @pl.loop(0, SC_ROWS)
            def _(i):
                @pl.loop(0, C, step=SC_LANES)
                def _(j):
                    at = (i, pl.ds(j, SC_LANES))
                    gv = gb[at]
                    nm = ADAM_B1 * mb[at] + (1.0 - ADAM_B1) * gv
                    nv = ADAM_B2 * vb[at] + (1.0 - ADAM_B2) * (gv * gv)
                    mb[at] = nm
                    vb[at] = nv
                    wb[at] = -ADAM_LR * ((nm / bc1) / (jnp.sqrt(nv / bc2) + ADAM_EPS) + ADAM_WD * wb[at])

            pltpu.sync_copy(gb, go_hbm.at[rows, :])
            pltpu.sync_copy(wb, d_hbm.at[rows, :])
            pltpu.sync_copy(mb, nm_hbm.at[rows, :])
            pltpu.sync_copy(vb, nv_hbm.at[rows, :])

    out = jax.ShapeDtypeStruct((R, C), F32)
    buf = pltpu.VMEM((SC_ROWS, C), F32)
    return pl.kernel(body, name=name, out_type=(out, out, out, out),
                     mesh=plsc.VectorSubcoreMesh(core_axis_name="sc_core", subcore_axis_name="sc_subcore"),
                     scratch_types=[buf, buf, buf, buf],
                     cost_estimate=pl.CostEstimate(flops=16 * R * C, transcendentals=2 * R * C, bytes_accessed=32 * R * C),
                     )(w, g, m, v)


WEIGHTS = ['ffn1_norm', 'ffn1_w_gate', 'ffn1_w_up', 'ffn1_w_down', 'mix_norm', 'w_in', 'pool_w', 'pool_scale', 'gla_w_a2',
           'gla_b_a', 'gla_head_norm', 'w_out', 'xattn_norm', 'mem_norm', 'xattn_w_q', 'xattn_w_kv', 'xattn_w_o', 'ffn2_norm',
           'ffn2_w_gate', 'ffn2_w_up', 'ffn2_w_down', 'final_norm']
SHARDED = ['ffn1_w_gate', 'ffn1_w_up', 'ffn1_w_down', 'w_in', 'pool_w', 'gla_w_a2', 'w_out', 'xattn_w_q', 'xattn_w_kv',
           'xattn_w_o', 'ffn2_w_gate', 'ffn2_w_up', 'ffn2_w_down']
REPLICATED = [n for n in WEIGHTS if n not in SHARDED]
ON_SPARSECORE = ['ffn2_w_gate', 'ffn2_w_up', 'w_out', 'xattn_w_q', 'xattn_w_kv', 'xattn_w_o']
PAIR_SUM_ON_SPARSECORE = ['ffn2_w_down', 'ffn2_w_gate', 'ffn2_w_up', 'xattn_w_o', 'xattn_w_q', 'xattn_w_kv', 'w_out', 'pool_w',
                          'ffn1_w_down']
SMALL_COLS = 512


def _as2d(a):
    return a.reshape(-1, a.shape[-1])


def _finish_weight(name, gathered, wl):
    G, R, C = gathered.shape
    rank = wl["gla_w_a2"].shape[1]
    if name in ("w_out", "xattn_w_q", "xattn_w_o"):
        return gathered.reshape(G * R, C)
    if name == "w_in":
        w_in = jnp.transpose(gathered, (1, 0, 2)).reshape(R, G * C)
        main = G * C - rank
        return jnp.concatenate([w_in[:, :main], jnp.pad(w_in[:, main:], ((0, 0), (0, LANES - rank)))], axis=1)
    if name == "pool_w":
        NG, CJ, _ = wl[name].shape[1:]
        return jnp.transpose(gathered.reshape(G, NG, CJ, C), (1, 0, 2, 3)).reshape(NG, G * CJ, C)
    if name == "gla_w_a2":
        a2 = jnp.transpose(gathered, (1, 0, 2)).reshape(rank, G * C)
        return jnp.pad(a2, ((0, LANES - rank), (0, 0))).astype(BF16)
    return gathered


def _start_gathers(wl):
    started = {}
    token = None
    for n in SHARDED:
        whole = not n.startswith("ffn1")
        buf = _cast_to_slot(_as2d(wl[n]), BF16, f"slot_{n}", dep=token)
        send_sems, recv_sems, thru, token = _gather_start(buf, f"gather_start_{n}", whole)
        started[n] = (send_sems, recv_sems, thru, whole)
    cache = {}

    def weight(n, after=None):
        if n not in cache:
            *handles, whole = started[n]
            buf = _gather_wait(*handles, after, f"gather_wait_{n}", whole)
            if not whole:
                buf = _gather_forward(buf, f"gather_forward_{n}")
            cache[n] = _finish_weight(n, buf, wl)
        return cache[n]

    return weight, token


def _shard_major(name, gfull, wl):
    R, C = _as2d(wl[name]).shape
    if name in ("ffn1_w_gate", "ffn1_w_up", "ffn2_w_gate", "ffn2_w_up", "xattn_w_kv"):
        return gfull
    if name in ("ffn1_w_down", "ffn2_w_down", "w_out", "xattn_w_q", "xattn_w_o"):
        return gfull.reshape(N_SHARDS, R, C)
    if name == "w_in":
        return jnp.transpose(gfull[:, :N_SHARDS * C].reshape(R, N_SHARDS, C), (1, 0, 2))
    if name == "pool_w":
        NG, CJ, _ = wl[name].shape[1:]
        return jnp.transpose(gfull.reshape(NG, N_SHARDS, CJ, C), (1, 0, 2, 3)).reshape(N_SHARDS, R, C)
    assert name == "gla_w_a2"
    return jnp.transpose(gfull[:R].reshape(R, N_SHARDS, C), (1, 0, 2))


def kernel(x, mem, ffn1_norm, ffn1_w_gate, ffn1_w_up, ffn1_w_down, mix_norm, w_in, pool_w, pool_scale, gla_w_a2, gla_b_a, gla_head_norm, w_out, xattn_norm, mem_norm, xattn_w_q, xattn_w_kv, xattn_w_o, ffn2_norm, ffn2_w_gate, ffn2_w_up, ffn2_w_down, final_norm, loss_target, m_ffn1_norm, m_ffn1_w_gate, m_ffn1_w_up, m_ffn1_w_down, m_mix_norm, m_w_in, m_pool_w, m_pool_scale, m_gla_w_a2, m_gla_b_a, m_gla_head_norm, m_w_out, m_xattn_norm, m_mem_norm, m_xattn_w_q, m_xattn_w_kv, m_xattn_w_o, m_ffn2_norm, m_ffn2_w_gate, m_ffn2_w_up, m_ffn2_w_down, m_final_norm, v_ffn1_norm, v_ffn1_w_gate, v_ffn1_w_up, v_ffn1_w_down, v_mix_norm, v_w_in, v_pool_w, v_pool_scale, v_gla_w_a2, v_gla_b_a, v_gla_head_norm, v_w_out, v_xattn_norm, v_mem_norm, v_xattn_w_q, v_xattn_w_kv, v_xattn_w_o, v_ffn2_norm, v_ffn2_w_gate, v_ffn2_w_up, v_ffn2_w_down, v_final_norm):
    given = dict(locals())
    wl = {n: given[n] for n in WEIGHTS}
    ml = {n: given["m_" + n] for n in WEIGHTS}
    vl = {n: given["v_" + n] for n in WEIGHTS}

    vec = {n: wl[n].reshape(1, -1) for n in REPLICATED}
    weight, dep0 = _start_gathers(wl)
    in_flight = {}

    pair_flight = {}

    def emit_begin(n, gfull):
        *pair_flight[n], token = _pair_start(_shard_major(n, gfull, wl), f"{n}_pair_start")
        return token

    summing = {}

    def emit_finish(n, after):
        gsm, other = _pair_wait(*pair_flight.pop(n), after, f"{n}_pair_wait")
        if n in PAIR_SUM_ON_SPARSECORE:
            summing[n] = _pair_add_sc(gsm, other, f"{n}_pair_add_sc")
            return None
        *in_flight[n], token = _chip_start(_pair_add(gsm, other, f"{n}_pair_add"), f"{n}_chip_start")
        return token

    def emit_send(n):
        *in_flight[n], token = _chip_start(summing.pop(n), f"{n}_chip_start")
        return token

    grads = {}
    updates = {}

    def reduce_done(n, after):
        part, slots = _chip_wait(*in_flight.pop(n), after, f"{n}_chip_wait")
        grads[n], token = _pair_join(_chip_sum(part, slots, f"{n}_chip_sum"), f"{n}_pair_join")
        return token

    def early_update(after):
        tokens = [reduce_done(n, after) for n in ON_SPARSECORE]
        for n in ON_SPARSECORE:
            g2 = grads[n]
            updates[n] = _adamw_sc(wl[n].reshape(g2.shape), g2, ml[n].reshape(g2.shape), vl[n].reshape(g2.shape),
                                   f"adamw_sc_{n}")
        return tokens

    sq, dx0, g = _local_step(x[0], mem[0], loss_target[0], vec, weight,
                               (emit_begin, emit_finish, emit_send, early_update), dep0)
    assert not summing
    loss = lax.psum(0.5 * jnp.sum(sq) / x.shape[-1], ("x", "y", "c"))

    for n in list(in_flight):
        reduce_done(n, dx0)
    widths = [wl[n].size for n in REPLICATED]
    total = sum(widths)
    rows = -(-total // SMALL_COLS)
    rows = -(-rows // 8) * 8
    packed = jnp.concatenate([g[n].reshape(-1) for n in REPLICATED] + [jnp.zeros((rows * SMALL_COLS - total,), F32)])
    summed = _all_reduce_small(packed.reshape(rows, SMALL_COLS), "small_all_reduce").reshape(-1)
    off = 0
    for n, width in zip(REPLICATED, widths):
        grads[n] = summed[off:off + width].reshape(1, width)
        off += width

    out_g, out_d, out_m, out_v = [], [], [], []
    for n in WEIGHTS:
        shape = wl[n].shape
        g2 = grads[n]
        if n in updates:
            go, d, nm, nv = updates[n]
        else:
            dep = tuple(updates[k][1][:8, :LANES] for k in updates) if n == "w_in" else ()
            go, d, nm, nv = _adamw(wl[n].reshape(g2.shape), g2, ml[n].reshape(g2.shape), vl[n].reshape(g2.shape),
                                   f"adamw_{n}", dep)
        out_g.append(go.reshape(shape))
        out_d.append(d.reshape(shape))
        out_m.append(nm.reshape(shape))
        out_v.append(nv.reshape(shape))
    return (loss, dx0.reshape(x.shape), *out_g, *out_d, *out_m, *out_v)
```

```python
import functools

import jax
import jax.numpy as jnp
from jax import lax
from jax.experimental import pallas as pl
from jax.experimental.pallas import tpu as pltpu
from jax.experimental.pallas import tpu_sc as plsc

F32 = jnp.float32
BF16 = jnp.bfloat16
MESH = pl.DeviceIdType.MESH

RMS_EPS = 1e-6
CHUNK = 64
POOL_WINDOWS = (2, 4, 8, 16)
POOL_HALO = 16
N_HEADS = 4
GATE_TEMP = 16.0
ADAM_LR, ADAM_B1, ADAM_B2, ADAM_EPS, ADAM_WD, ADAM_STEP = 0.001, 0.9, 0.999, 1e-08, 0.01, 10
N_SHARDS = 4
LANES = 128
MXU_COLS = 256
VMEM_LIMIT = 58 * 1024 * 1024

ANY = pl.BlockSpec(memory_space=pl.ANY)
HBM = pl.BlockSpec(memory_space=pltpu.HBM)
SEM = pl.BlockSpec(memory_space=pltpu.SEMAPHORE)
EFFECT = pltpu.SideEffectType.DATAFLOW_SIDE_EFFECTING


def _params(**kw):
    return pltpu.CompilerParams(vmem_limit_bytes=VMEM_LIMIT, **kw)


def _tile(n, want):
    for unit in (LANES, 8):
        t = (min(want, n) // unit) * unit
        while t >= unit:
            if n % t == 0:
                return t
            t -= unit
    return n


def _dot(a, b, dims):
    return lax.dot_general(a, b, (dims, ((), ())), preferred_element_type=F32)


def _nn(a, b):
    return _dot(a, b, ((1,), (0,)))


def _nt(a, b):
    return _dot(a, b, ((1,), (1,)))


def _tn(a, b):
    return _dot(a, b, ((0,), (0,)))


def _sigmoid(x):
    return 1.0 / (1.0 + jnp.exp(-x))


def _matmul_cost(M, N, K, operands, out_shape):
    nbytes = sum(a.size * a.dtype.itemsize for a in operands) + out_shape.size * out_shape.dtype.itemsize
    return pl.CostEstimate(flops=2 * M * N * K, transcendentals=0, bytes_accessed=nbytes)


def _matmul(a, b, *, mode, name, out_dtype, tm=512, tn=2048, tk=2048, res=None, scale=1.0, b_groups=False, out_groups=0,
            dep=()):
    if mode == "tn":
        K, M = a.shape
    else:
        M, K = a.shape
    if mode == "nn":
        if b_groups:
            G, _, Nj = b.shape
            N = G * Nj
        else:
            N = b.shape[1]
    elif mode == "nt":
        if b_groups:
            G, N, Kj = b.shape
            assert G * Kj == K
        else:
            N = b.shape[0]
    else:
        N = b.shape[1]
    tm = _tile(M, tm)
    if mode == "nn" and b_groups:
        tn = _tile(Nj, tn)
    elif out_groups:
        tn = _tile(N // out_groups, tn)
    else:
        tn = _tile(N, tn)
    if mode == "nt" and b_groups:
        tk = _tile(Kj, tk)
    else:
        tk = _tile(K, tk)
    nk = K // tk
    grid = (M // tm, N // tn, nk)

    if mode == "tn":
        a_spec = pl.BlockSpec((tk, tm), lambda i, j, k: (k, i))
        b_spec = pl.BlockSpec((tk, tn), lambda i, j, k: (k, j))
        dims = ((0,), (0,))
    elif mode == "nn":
        a_spec = pl.BlockSpec((tm, tk), lambda i, j, k: (i, k))
        if b_groups:
            npj = Nj // tn
            b_spec = pl.BlockSpec((None, tk, tn), lambda i, j, k: (j // npj, k, j % npj))
        else:
            b_spec = pl.BlockSpec((tk, tn), lambda i, j, k: (k, j))
        dims = ((1,), (0,))
    else:
        a_spec = pl.BlockSpec((tm, tk), lambda i, j, k: (i, k))
        if b_groups:
            kpj = Kj // tk
            b_spec = pl.BlockSpec((None, tn, tk), lambda i, j, k: (k // kpj, j, k % kpj))
        else:
            b_spec = pl.BlockSpec((tn, tk), lambda i, j, k: (j, k))
        dims = ((1,), (1,))
    if out_groups:
        npj = (N // out_groups) // tn
        o_spec = pl.BlockSpec((None, tm, tn), lambda i, j, k: (j // npj, i, j % npj))
        out_shape = jax.ShapeDtypeStruct((out_groups, M, N // out_groups), out_dtype)
    else:
        o_spec = pl.BlockSpec((tm, tn), lambda i, j, k: (i, j))
        out_shape = jax.ShapeDtypeStruct((M, N), out_dtype)
    in_specs = [a_spec, b_spec]
    operands = [a, b]
    if res is not None:
        in_specs.append(pl.BlockSpec((tm, tn), lambda i, j, k: (i, j)))
        operands.append(res)
    has_res = res is not None
    n_dep = len(dep)
    for d in dep:
        in_specs.append(pl.BlockSpec(d.shape, lambda i, j, k: (0, 0)))
        operands.append(d)

    def body(*refs):
        if has_res:
            a_ref, b_ref, r_ref = refs[:3]
        else:
            a_ref, b_ref = refs[:2]
            r_ref = None
        o_ref = refs[2 + has_res + n_dep]

        def finish(acc):
            if scale != 1.0:
                acc = acc * scale
            if r_ref is not None:
                acc = r_ref[...] + acc
            o_ref[...] = acc.astype(o_ref.dtype)

        part = _dot(a_ref[...], b_ref[...], dims)
        if nk == 1:
            finish(part)
        else:
            acc_ref = o_ref if in_place else refs[-1]
            k = pl.program_id(2)

            @pl.when(k == 0)
            def _():
                acc_ref[...] = part

            @pl.when(k > 0)
            def _():
                acc_ref[...] += part

            if not in_place:
                @pl.when(k == nk - 1)
                def _():
                    finish(acc_ref[...])

    in_place = out_dtype == F32 and res is None and scale == 1.0
    scratch = [] if nk == 1 or in_place else [pltpu.VMEM((tm, tn), F32)]
    return pl.pallas_call(body, name=name, grid=grid, in_specs=in_specs, out_specs=o_spec, out_shape=out_shape,
                          scratch_shapes=scratch, compiler_params=_params(),
                          cost_estimate=_matmul_cost(M, N, K, operands, out_shape))(*operands)


def _rms_fwd(x, gain, name, tm=512, dep=None):
    S, D = x.shape
    tm = _tile(S, tm)

    def body(x_ref, g_ref, *rest):
        o_ref = rest[-1]
        xv = x_ref[...]
        r = lax.rsqrt(jnp.mean(xv * xv, axis=-1, keepdims=True) + RMS_EPS)
        o_ref[...] = (xv * r * g_ref[...]).astype(o_ref.dtype)

    in_specs = [pl.BlockSpec((tm, D), lambda i: (i, 0)), pl.BlockSpec((1, D), lambda i: (0, 0))]
    operands = [x, gain]
    if dep is not None:
        in_specs.append(pl.BlockSpec(dep.shape, lambda i: (0, 0)))
        operands.append(dep)
    return pl.pallas_call(body, name=name, grid=(S // tm,), in_specs=in_specs,
                          out_specs=pl.BlockSpec((tm, D), lambda i: (i, 0)),
                          out_shape=jax.ShapeDtypeStruct((S, D), BF16), compiler_params=_params())(*operands)


def _rms_bwd(x, gain, dh, dres, name, lowp=None, tm=512):
    half = lowp is not None
    S, D = x.shape
    tm = _tile(S, tm)
    has_res = dres is not None

    def body(*refs):
        if has_res:
            x_ref, g_ref, dh_ref, dr_ref = refs[:4]
            outs = refs[4:]
        else:
            x_ref, g_ref, dh_ref = refs[:3]
            dr_ref = None
            outs = refs[3:]
        dx_ref, dg_ref = outs[0], outs[-1]
        xv = x_ref[...]
        dhv = dh_ref[...].astype(F32)
        r = lax.rsqrt(jnp.mean(xv * xv, axis=-1, keepdims=True) + RMS_EPS)
        gy = dhv * g_ref[...]
        dx = r * gy - xv * (r * r * r) * jnp.mean(gy * xv, axis=-1, keepdims=True)
        if dr_ref is not None:
            dx = dx + dr_ref[...]
        dx_ref[...] = dx
        if half:
            outs[1][...] = (dx if lowp == 1.0 else lowp * dx).astype(BF16)
        part = jnp.sum(dhv * xv * r, axis=0, keepdims=True)

        @pl.when(pl.program_id(0) == 0)
        def _():
            dg_ref[...] = part

        @pl.when(pl.program_id(0) > 0)
        def _():
            dg_ref[...] += part

    row = pl.BlockSpec((tm, D), lambda i: (i, 0))
    vec = pl.BlockSpec((1, D), lambda i: (0, 0))
    in_specs = [row, vec, row] + ([row] if has_res else [])
    operands = [x, gain, dh] + ([dres] if has_res else [])
    out_specs = [row] + ([row] if half else []) + [vec]
    out_shape = [jax.ShapeDtypeStruct((S, D), F32)] + ([jax.ShapeDtypeStruct((S, D), BF16)] if half else []) + [
        jax.ShapeDtypeStruct((1, D), F32)]
    return pl.pallas_call(body, name=name, grid=(S // tm,), in_specs=in_specs, out_specs=out_specs, out_shape=out_shape,
                          compiler_params=_params())(*operands)


def _loss_head(x, gain, target, name, tm=512):
    S, D = x.shape
    tm = _tile(S, tm)

    def body(x_ref, g_ref, t_ref, sq_ref, dx_ref, dxh_ref, dg_ref):
        xv = x_ref[...]
        r = lax.rsqrt(jnp.mean(xv * xv, axis=-1, keepdims=True) + RMS_EPS)
        xn = xv * r
        err = xn * g_ref[...] - t_ref[...]
        dout = err * (1.0 / D)
        gy = dout * g_ref[...]
        dx = r * gy - xv * (r * r * r) * jnp.mean(gy * xv, axis=-1, keepdims=True)
        dx_ref[...] = dx
        dxh_ref[...] = (0.5 * dx).astype(BF16)
        sq = jnp.sum(err * err, axis=0, keepdims=True)
        dg = jnp.sum(dout * xn, axis=0, keepdims=True)

        @pl.when(pl.program_id(0) == 0)
        def _():
            sq_ref[...] = sq
            dg_ref[...] = dg

        @pl.when(pl.program_id(0) > 0)
        def _():
            sq_ref[...] += sq
            dg_ref[...] += dg

    row = pl.BlockSpec((tm, D), lambda i: (i, 0))
    vec = pl.BlockSpec((1, D), lambda i: (0, 0))
    return pl.pallas_call(body, name=name, grid=(S // tm,), in_specs=[row, vec, row], out_specs=[vec, row, row, vec],
                          out_shape=[jax.ShapeDtypeStruct((1, D), F32), jax.ShapeDtypeStruct((S, D), F32),
                                     jax.ShapeDtypeStruct((S, D), BF16), jax.ShapeDtypeStruct((1, D), F32)],
                          compiler_params=_params())(x, gain, target)


def _cast(x, dtype, name, scale=1.0, tm=256):
    S, D = x.shape
    tm = _tile(S, tm)

    def body(x_ref, o_ref):
        o_ref[...] = (x_ref[...] * scale).astype(o_ref.dtype)

    row = pl.BlockSpec((tm, D), lambda i: (i, 0))
    return pl.pallas_call(body, name=name, grid=(S // tm,), in_specs=[row], out_specs=row,
                          out_shape=jax.ShapeDtypeStruct((S, D), dtype), compiler_params=_params())(x)


def _ffn_up(h, wg, wu, name, tm=512):
    S, D = h.shape
    G, _, Fj = wg.shape
    tm = _tile(S, tm)

    def body(h_ref, wg_ref, wu_ref, ga_ref, gb_ref, hid_ref):
        hv = h_ref[...]
        a = _nn(hv, wg_ref[...])
        b = _nn(hv, wu_ref[...])
        s = _sigmoid(a)
        silu = a * s
        ga_ref[...] = (b * (s * (1.0 + a * (1.0 - s)))).astype(BF16)
        gb_ref[...] = silu.astype(BF16)
        hid_ref[...] = (silu * b).astype(BF16)

    w_spec = pl.BlockSpec((None, D, Fj), lambda g, i: (g, 0, 0))
    o_spec = pl.BlockSpec((tm, Fj), lambda g, i: (i, g))
    out = jax.ShapeDtypeStruct((S, G * Fj), BF16)
    return pl.pallas_call(body, name=name, grid=(G, S // tm),
                          in_specs=[pl.BlockSpec((tm, D), lambda g, i: (i, 0)), w_spec, w_spec],
                          out_specs=[o_spec, o_spec, o_spec], out_shape=[out, out, out], compiler_params=_params())(h, wg, wu)


def _ffn_dact(dxh, wd, ga, gb, name, tm=512):
    S, D = dxh.shape
    G, Fj, _ = wd.shape
    tm = _tile(S, tm)

    def body(dx_ref, wd_ref, ga_ref, gb_ref, da_ref, db_ref):
        dhid = _nt(dx_ref[...], wd_ref[...])
        da_ref[...] = (dhid * ga_ref[...].astype(F32)).astype(BF16)
        db_ref[...] = (dhid * gb_ref[...].astype(F32)).astype(BF16)

    blk = pl.BlockSpec((tm, Fj), lambda g, i: (i, g))
    out = jax.ShapeDtypeStruct((S, G * Fj), BF16)
    return pl.pallas_call(body, name=name, grid=(G, S // tm),
                          in_specs=[pl.BlockSpec((tm, D), lambda g, i: (i, 0)),
                                    pl.BlockSpec((None, Fj, D), lambda g, i: (g, 0, 0)), blk, blk],
                          out_specs=[blk, blk], out_shape=[out, out], compiler_params=_params())(dxh, wd, ga, gb)


def _ffn_dh(da, db, wg, wu, name, dep=(), tm=512):
    S = da.shape[0]
    G, D, Fj = wg.shape
    tm = _tile(S, tm)

    def body(da_ref, db_ref, wg_ref, wu_ref, *rest):
        o_ref = rest[-1]
        part = _nt(da_ref[...], wg_ref[...]) + _nt(db_ref[...], wu_ref[...])

        @pl.when(pl.program_id(1) == 0)
        def _():
            o_ref[...] = part

        @pl.when(pl.program_id(1) > 0)
        def _():
            o_ref[...] += part

    act = pl.BlockSpec((tm, Fj), lambda i, g: (i, g))
    w_spec = pl.BlockSpec((None, D, Fj), lambda i, g: (g, 0, 0))
    in_specs = [act, act, w_spec, w_spec] + [pl.BlockSpec(d.shape, lambda i, g: (0, 0)) for d in dep]
    return pl.pallas_call(body, name=name, grid=(S // tm, G), in_specs=in_specs,
                          out_specs=pl.BlockSpec((tm, D), lambda i, g: (i, 0)),
                          out_shape=jax.ShapeDtypeStruct((S, D), F32), compiler_params=_params(),
                          cost_estimate=_matmul_cost(S, D, 2 * G * Fj, (da, db, wg, wu), jax.ShapeDtypeStruct((S, D), F32)),
                          )(da, db, wg, wu, *dep)


def _pool_fwd(proj, pool_w, pool_scale, name, tm=512):
    S = proj.shape[0]
    NG, C, _ = pool_w.shape
    DP = NG * C
    tm = _tile(S, tm)
    hb = tm // POOL_HALO
    n_ext = tm + POOL_HALO

    def body(u_ref, halo_ref, w_ref, sc_ref, y_ref, d_ref):
        i = pl.program_id(0)
        t = lax.broadcasted_iota(jnp.int32, (tm, 1), 0) + i * tm
        for g, win in enumerate(POOL_WINDOWS):
            cols = slice(g * C, (g + 1) * C)
            ug = u_ref[:, cols]
            halo = jnp.where(i > 0, halo_ref[:, cols], 0.0)
            acc = jnp.concatenate([halo, ug], axis=0)
            step = 1
            while step < win:
                acc = acc + pltpu.roll(acc, step, 0)
                step *= 2
            count = jnp.minimum(t + 1, win).astype(F32)
            d = (acc[POOL_HALO:, :] / count - ug).astype(BF16)
            d_ref[:, cols] = d
            y_ref[:, cols] = (_nn(d, w_ref[g]) * sc_ref[:, cols]).astype(BF16)

    del n_ext
    return pl.pallas_call(
        body, name=name, grid=(S // tm,),
        in_specs=[pl.BlockSpec((tm, DP), lambda i: (i, 0)),
                  pl.BlockSpec((POOL_HALO, DP), lambda i: (jnp.maximum(i * hb - 1, 0), 0)),
                  pl.BlockSpec((NG, C, C), lambda i: (0, 0, 0)), pl.BlockSpec((1, DP), lambda i: (0, 0))],
        out_specs=[pl.BlockSpec((tm, DP), lambda i: (i, 0)), pl.BlockSpec((tm, DP), lambda i: (i, 0))],
        out_shape=[jax.ShapeDtypeStruct((S, DP), BF16), jax.ShapeDtypeStruct((S, DP), BF16)],
        compiler_params=_params())(proj, proj, pool_w, pool_scale)


def _pool_bwd(dymix, d, pool_w, pool_scale, name, tm=512):
    S = dymix.shape[0]
    NG, C, _ = pool_w.shape
    DP = NG * C
    tm = _tile(S, tm)
    hb = tm // POOL_HALO
    nb = S // tm
    n_ext = tm + POOL_HALO
    last_halo = S // POOL_HALO - 1

    def body(dy_ref, halo_ref, d_ref, w_ref, sc_ref, du_ref, dw_ref, dsc_ref):
        i = pl.program_id(0)
        t = lax.broadcasted_iota(jnp.int32, (n_ext, 1), 0) + i * tm
        for g, win in enumerate(POOL_WINDOWS):
            cols = slice(g * C, (g + 1) * C)
            dy = dy_ref[:, cols]
            halo = jnp.where(i < nb - 1, halo_ref[:, cols], 0.0)
            sc = sc_ref[:, cols]
            dv = d_ref[:, cols]
            e_ext = (jnp.concatenate([dy, halo], axis=0) * sc).astype(BF16)
            dd = _nt(e_ext, w_ref[g])
            count = jnp.minimum(t + 1, win).astype(F32)
            acc = dd / count
            step = 1
            while step < win:
                acc = acc + pltpu.roll(acc, n_ext - step, 0)
                step *= 2
            du_ref[:, cols] = (acc[:tm, :] - dd[:tm, :]).astype(BF16)
            dw = _tn(dv, e_ext[:tm, :])
            dsc = jnp.sum(dy * _nn(dv, w_ref[g]), axis=0, keepdims=True)

            @pl.when(i == 0)
            def _():
                dw_ref[g] = dw
                dsc_ref[:, cols] = dsc

            @pl.when(i > 0)
            def _():
                dw_ref[g] += dw
                dsc_ref[:, cols] += dsc

    return pl.pallas_call(
        body, name=name, grid=(nb,),
        in_specs=[pl.BlockSpec((tm, DP), lambda i: (i, 0)),
                  pl.BlockSpec((POOL_HALO, DP), lambda i: (jnp.minimum((i + 1) * hb, last_halo), 0)),
                  pl.BlockSpec((tm, DP), lambda i: (i, 0)),
                  pl.BlockSpec((NG, C, C), lambda i: (0, 0, 0)), pl.BlockSpec((1, DP), lambda i: (0, 0))],
        out_specs=[pl.BlockSpec((tm, DP), lambda i: (i, 0)), pl.BlockSpec((NG, C, C), lambda i: (0, 0, 0)),
                   pl.BlockSpec((1, DP), lambda i: (0, 0))],
        out_shape=[jax.ShapeDtypeStruct((S, DP), BF16), jax.ShapeDtypeStruct((NG, C, C), F32),
                   jax.ShapeDtypeStruct((1, DP), F32)],
        compiler_params=_params())(dymix, dymix, d, pool_w, pool_scale)


def _chunk_scan(v, rows, reverse):
    n = v.shape[0]
    step = 1
    while step < CHUNK:
        if reverse:
            v = v + jnp.where(rows < CHUNK - step, pltpu.roll(v, n - step, 0), 0.0)
        else:
            v = v + jnp.where(rows >= step, pltpu.roll(v, step, 0), 0.0)
        step *= 2
    return v


def _log_decay(alr, w_a2, b_a):
    z = _nn(alr.astype(BF16), w_a2) + b_a
    la = (jnp.minimum(z, 0.0) - jnp.log(1.0 + jnp.exp(-jnp.abs(z)))) * (1.0 / GATE_TEMP)
    return z, la


def _gla_specs(DP, DKT, DV, tb, bmap):
    return [pl.BlockSpec((tb, DKT), lambda i: (bmap(i), DP // DKT)),
            pl.BlockSpec((tb, DKT), lambda i: (bmap(i), DP // DKT + 1)),
            pl.BlockSpec((tb, DV), lambda i: (bmap(i), (DP + 2 * DKT) // DV)),
            pl.BlockSpec((tb, DV), lambda i: (bmap(i), (DP + 2 * DKT) // DV + 1)),
            pl.BlockSpec((tb, LANES), lambda i: (bmap(i), (DP + 2 * DKT + 2 * DV) // LANES))]


def _gla_fwd(proj, y_pool, w_a2, b_a, head_norm, name, tb=512):
    S = proj.shape[0]
    DP = y_pool.shape[1]
    DKT = b_a.shape[1]
    DV = head_norm.shape[1]
    dk, dv = DKT // N_HEADS, DV // N_HEADS
    tb = _tile(S, tb)
    ncb = tb // CHUNK
    qscale = dk ** -0.5

    def body(q_ref, k_ref, v_ref, g_ref, alr_ref, yp_ref, wa_ref, ba_ref, hn_ref, y_ref, st_out_ref, st_ref, kdec_ref,
             gam_ref):
        @pl.when(pl.program_id(0) == 0)
        def _():
            st_ref[...] = jnp.zeros_like(st_ref)

        y_ref[:, :DP] = yp_ref[...]

        rows = lax.broadcasted_iota(jnp.int32, (tb, 1), 0) % CHUNK
        _, la = _log_decay(alr_ref[...], wa_ref[...], ba_ref[...])
        tail = _chunk_scan(la, rows, True)
        kdec_ref[...] = k_ref[...] * jnp.exp(tail - la)
        gam_ref[...] = jnp.exp(tail)

        def chunk(c, carry):
            r0 = pl.multiple_of(c * CHUNK, CHUNK)
            rs = pl.ds(r0, CHUNK)
            gam = gam_ref[pl.ds(r0, 1), :]
            heads = range(N_HEADS)
            kcs = [slice(h * dk, (h + 1) * dk) for h in heads]
            vcs = [slice(h * dv, (h + 1) * dv) for h in heads]
            upd = [_tn(v_ref[rs, vcs[h]].astype(BF16), kdec_ref[rs, kcs[h]].astype(BF16)) for h in heads]
            st = [st_ref[h] * gam[:, kcs[h]] + upd[h] for h in heads]
            o = [_nt((q_ref[rs, kcs[h]] * qscale).astype(BF16), st[h].astype(BF16)) for h in heads]
            for h in heads:
                st_ref[h] = st[h]
                st_out_ref[c, h] = st[h]
                r = lax.rsqrt(jnp.mean(o[h] * o[h], axis=-1, keepdims=True) + RMS_EPS)
                gv = g_ref[rs, vcs[h]]
                y_ref[rs, DP + h * dv:DP + (h + 1) * dv] = (o[h] * r * hn_ref[:, vcs[h]] * (gv * _sigmoid(gv))).astype(BF16)
            return carry

        lax.fori_loop(0, ncb, chunk, 0, unroll=2)

    full = lambda shape: pl.BlockSpec(shape, lambda i: (0,) * len(shape))
    return pl.pallas_call(
        body, name=name, grid=(S // tb,),
        in_specs=_gla_specs(DP, DKT, DV, tb, lambda i: i) + [pl.BlockSpec((tb, DP), lambda i: (i, 0)),
                                                            full((LANES, DKT)), full((1, DKT)), full((1, DV))],
        out_specs=[pl.BlockSpec((tb, DP + DV), lambda i: (i, 0)),
                   pl.BlockSpec((ncb, N_HEADS, dv, dk), lambda i: (i, 0, 0, 0))],
        out_shape=[jax.ShapeDtypeStruct((S, DP + DV), BF16), jax.ShapeDtypeStruct((S // CHUNK, N_HEADS, dv, dk), F32)],
        scratch_shapes=[pltpu.VMEM((N_HEADS, dv, dk), F32), pltpu.VMEM((tb, DKT), F32), pltpu.VMEM((tb, DKT), F32)],
        compiler_params=_params())(proj, proj, proj, proj, proj, y_pool, w_a2, b_a, head_norm)


def _gla_bwd(proj, states, dymix, du, w_a2, b_a, head_norm, name, tb=512):
    S = proj.shape[0]
    DP = du.shape[1]
    DKT = b_a.shape[1]
    DV = head_norm.shape[1]
    dk, dv = DKT // N_HEADS, DV // N_HEADS
    tb = _tile(S, tb)
    ncb = tb // CHUNK
    nb = S // tb
    qscale = dk ** -0.5
    rev = lambda i: nb - 1 - i

    q0, k0, v0, g0, a0 = DP, DP + DKT, DP + 2 * DKT, DP + 2 * DKT + DV, DP + 2 * DKT + 2 * DV

    def body(q_ref, k_ref, v_ref, g_ref, alr_ref, st_blk_ref, st_prev_ref, dy_ref, du_ref, wa_ref, ba_ref, hn_ref,
             dp_ref, dwa_ref, dba_ref, dhn_ref,
             dst_ref, kdec_ref, dec_ref, gam_ref, e_ref, dla_ref, dhn_acc_ref):
        i = pl.program_id(0)
        blk = rev(i)
        dp_ref[:, :DP] = du_ref[...]

        @pl.when(i == 0)
        def _():
            dst_ref[...] = jnp.zeros_like(dst_ref)

        dhn_acc_ref[...] = jnp.zeros_like(dhn_acc_ref)
        rows = lax.broadcasted_iota(jnp.int32, (tb, 1), 0) % CHUNK
        z, la = _log_decay(alr_ref[...], wa_ref[...], ba_ref[...])
        tail = _chunk_scan(la, rows, True)
        dec_ref[...] = jnp.exp(tail - la)
        kdec_ref[...] = k_ref[...] * dec_ref[...]
        gam_ref[...] = jnp.exp(tail)

        def chunk(cc, carry):
            c = ncb - 1 - cc
            r0 = pl.multiple_of(c * CHUNK, CHUNK)
            rs = pl.ds(r0, CHUNK)
            gam = gam_ref[pl.ds(r0, 1), :]
            first = jnp.logical_and(blk == 0, c == 0)
            heads = range(N_HEADS)
            kcs = [slice(h * dk, (h + 1) * dk) for h in heads]
            vcs = [slice(h * dv, (h + 1) * dv) for h in heads]
            qs = [(q_ref[rs, kcs[h]] * qscale).astype(BF16) for h in heads]
            stb = [st_blk_ref[c, h].astype(BF16) for h in heads]
            o = [_nt(qs[h], stb[h]) for h in heads]
            do = []
            for h in heads:
                oh = o[h]
                r = lax.rsqrt(jnp.mean(oh * oh, axis=-1, keepdims=True) + RMS_EPS)
                gv = g_ref[rs, vcs[h]]
                sg = _sigmoid(gv)
                dy = dy_ref[rs, vcs[h]]
                hn = hn_ref[:, vcs[h]]
                on = oh * r
                dp_ref[rs, g0 + h * dv:g0 + (h + 1) * dv] = (dy * on * hn * (sg * (1.0 + gv * (1.0 - sg)))).astype(BF16)
                don = dy * (gv * sg)
                dhn_acc_ref[:, vcs[h]] += jnp.sum(don * on, axis=0, keepdims=True)
                dn = don * hn
                do.append((r * dn - oh * (r * r * r) * jnp.mean(dn * oh, axis=-1, keepdims=True)).astype(BF16))
            dqs = [_nn(do[h], stb[h]) for h in heads]
            dst = [dst_ref[h] + _tn(do[h], qs[h]) for h in heads]
            for h in heads:
                dp_ref[rs, q0 + h * dk:q0 + (h + 1) * dk] = (dqs[h] * qscale).astype(BF16)
            dstb = [dst[h].astype(BF16) for h in heads]
            dvh = [_nt(kdec_ref[rs, kcs[h]].astype(BF16), dstb[h]) for h in heads]
            dkdec = [_nn(v_ref[rs, vcs[h]].astype(BF16), dstb[h]) for h in heads]
            gdg = []
            for h in heads:
                dp_ref[rs, v0 + h * dv:v0 + (h + 1) * dv] = dvh[h].astype(BF16)
                dp_ref[rs, k0 + h * dk:k0 + (h + 1) * dk] = (dkdec[h] * dec_ref[rs, kcs[h]]).astype(BF16)
                e_ref[rs, kcs[h]] = dkdec[h] * kdec_ref[rs, kcs[h]]
                st_prev = jnp.where(c > 0, st_blk_ref[jnp.maximum(c - 1, 0), h], st_prev_ref[0, h])
                st_prev = jnp.where(first, 0.0, st_prev)
                gdg.append(jnp.sum(dst[h] * st_prev, axis=0, keepdims=True) * gam[:, kcs[h]])
                dst_ref[h] = dst[h] * gam[:, kcs[h]]
            dla_ref[rs, :] = jnp.broadcast_to(jnp.concatenate(gdg, axis=1), (CHUNK, DKT))
            return carry

        lax.fori_loop(0, ncb, chunk, 0, unroll=2)

        ev = e_ref[...]
        dla = dla_ref[...] + _chunk_scan(ev, rows, False) - ev
        dz = dla * (1.0 / GATE_TEMP) * (1.0 - _sigmoid(z))
        dzb = dz.astype(BF16)
        dp_ref[:, a0:a0 + LANES] = _nt(dzb, wa_ref[...]).astype(BF16)
        dwa = _tn(alr_ref[...].astype(BF16), dzb)
        dba = jnp.sum(dz, axis=0, keepdims=True)

        @pl.when(i == 0)
        def _():
            dwa_ref[...] = dwa
            dba_ref[...] = dba
            dhn_ref[...] = dhn_acc_ref[...]

        @pl.when(i > 0)
        def _():
            dwa_ref[...] += dwa
            dba_ref[...] += dba
            dhn_ref[...] += dhn_acc_ref[...]

    full = lambda shape: pl.BlockSpec(shape, lambda i: (0,) * len(shape))
    rowblk = lambda w: pl.BlockSpec((tb, w), lambda i: (rev(i), 0))
    return pl.pallas_call(
        body, name=name, grid=(nb,),
        in_specs=_gla_specs(DP, DKT, DV, tb, rev) + [
            pl.BlockSpec((ncb, N_HEADS, dv, dk), lambda i: (rev(i), 0, 0, 0)),
            pl.BlockSpec((1, N_HEADS, dv, dk), lambda i: (jnp.maximum(rev(i) * ncb - 1, 0), 0, 0, 0)),
            pl.BlockSpec((tb, DV), lambda i: (rev(i), DP // DV)), rowblk(DP),
            full((LANES, DKT)), full((1, DKT)), full((1, DV))],
        out_specs=[rowblk(a0 + LANES), full((LANES, DKT)), full((1, DKT)), full((1, DV))],
        out_shape=[jax.ShapeDtypeStruct((S, a0 + LANES), BF16), jax.ShapeDtypeStruct((LANES, DKT), F32),
                   jax.ShapeDtypeStruct((1, DKT), F32), jax.ShapeDtypeStruct((1, DV), F32)],
        scratch_shapes=[pltpu.VMEM((N_HEADS, dv, dk), F32)] + [pltpu.VMEM((tb, DKT), F32)] * 5 + [pltpu.VMEM((1, DV), F32)],
        compiler_params=_params())(proj, proj, proj, proj, proj, states, states, dymix, du, w_a2, b_a, head_norm)


def _xattn_fwd(q, kv, name, tm=512):
    S, D = q.shape
    M = kv.shape[0]
    hd = D // N_HEADS
    tm = _tile(S, tm)
    scale = hd ** -0.5

    def body(q_ref, k_ref, v_ref, o_ref):
        heads = range(N_HEADS)
        hcs = [slice(h * hd, (h + 1) * hd) for h in heads]
        s = [_nt(q_ref[:, hc], k_ref[:, hc]) * scale for hc in hcs]
        p = []
        for h in heads:
            e = jnp.exp(s[h] - jnp.max(s[h], axis=-1, keepdims=True))
            p.append((e / jnp.sum(e, axis=-1, keepdims=True)).astype(BF16))
        o = [_nn(p[h], v_ref[:, hcs[h]]) for h in heads]
        for h in heads:
            o_ref[:, hcs[h]] = o[h].astype(BF16)

    return pl.pallas_call(body, name=name, grid=(S // tm,),
                          in_specs=[pl.BlockSpec((tm, D), lambda i: (i, 0)), pl.BlockSpec((M, D), lambda i: (0, 0)),
                                    pl.BlockSpec((M, D), lambda i: (0, 1))],
                          out_specs=pl.BlockSpec((tm, D), lambda i: (i, 0)),
                          out_shape=jax.ShapeDtypeStruct((S, D), BF16), compiler_params=_params())(q, kv, kv)


def _xattn_bwd(q, kv, do, name, tm=512):
    S, D = q.shape
    M = kv.shape[0]
    hd = D // N_HEADS
    tm = _tile(S, tm)
    scale = hd ** -0.5

    def body(q_ref, k_ref, v_ref, do_ref, dq_ref, dkv_ref):
        first = pl.program_id(0) == 0
        heads = range(N_HEADS)
        hcs = [slice(h * hd, (h + 1) * hd) for h in heads]
        s = [_nt(q_ref[:, hc], k_ref[:, hc]) * scale for hc in hcs]
        dp = [_nt(do_ref[:, hc], v_ref[:, hc]) for hc in hcs]
        p = []
        for h in heads:
            e = jnp.exp(s[h] - jnp.max(s[h], axis=-1, keepdims=True))
            p.append(e / jnp.sum(e, axis=-1, keepdims=True))
        dvh = [_tn(p[h].astype(BF16), do_ref[:, hcs[h]]) for h in heads]
        ds = [((p[h] * (dp[h] - jnp.sum(dp[h] * p[h], axis=-1, keepdims=True))) * scale).astype(BF16) for h in heads]
        dqh = [_nn(ds[h], k_ref[:, hcs[h]]) for h in heads]
        dkh = [_tn(ds[h], q_ref[:, hcs[h]]) for h in heads]
        for h in heads:
            dq_ref[:, hcs[h]] = dqh[h].astype(BF16)

        @pl.when(first)
        def _():
            for h in heads:
                dkv_ref[:, hcs[h]] = dkh[h]
                dkv_ref[:, D + h * hd:D + (h + 1) * hd] = dvh[h]

        @pl.when(jnp.logical_not(first))
        def _():
            for h in heads:
                dkv_ref[:, hcs[h]] += dkh[h]
                dkv_ref[:, D + h * hd:D + (h + 1) * hd] += dvh[h]

    row = pl.BlockSpec((tm, D), lambda i: (i, 0))
    return pl.pallas_call(body, name=name, grid=(S // tm,),
                          in_specs=[row, pl.BlockSpec((M, D), lambda i: (0, 0)), pl.BlockSpec((M, D), lambda i: (0, 1)), row],
                          out_specs=[row, pl.BlockSpec((M, 2 * D), lambda i: (0, 0))],
                          out_shape=[jax.ShapeDtypeStruct((S, D), BF16), jax.ShapeDtypeStruct((M, 2 * D), F32)],
                          compiler_params=_params())(q, kv, kv, do)


def _local_step(x, mem, target, vec, weight, emit, dep0):
    DP = vec["pool_scale"].shape[1]
    g = {}
    pending = []
    begun = []
    summed = []
    emit_begin, emit_finish, emit_send, early_update = emit

    def behind(fn, *a, **kw):
        dep = tuple(pending)
        pending.clear()
        out = fn(*a, dep=dep, **kw)
        while summed:
            pending.append(emit_send(summed.pop(0)))
        while begun:
            name = begun.pop(0)
            token = emit_finish(name, out)
            if token is None:
                summed.append(name)
            else:
                pending.append(token)
        return out

    def mm(a, b, **kw):
        return behind(_matmul, a, b, **kw)

    def send(name, gfull):
        pending.append(emit_begin(name, gfull))
        begun.append(name)

    def ffn_fwd(xin, tag, dep):
        h = _rms_fwd(xin, vec[f"{tag}_norm"], f"{tag}_norm", dep=dep)
        ga, gb, hid = _ffn_up(h, weight(f"{tag}_w_gate", h), weight(f"{tag}_w_up", h), f"{tag}_up")
        wd = weight(f"{tag}_w_down", hid)
        G, Fj, D = wd.shape
        xo = _matmul(hid, wd.reshape(G * Fj, D), mode="nn", name=f"{tag}_down", out_dtype=F32, res=xin, scale=0.5,
                     tn=1024, tk=G * Fj)
        return xo, (h, ga, gb, hid)

    def ffn_bwd(dxh, saved, tag, last):
        h, ga, gb, hid = saved
        wg, wu, wd = weight(f"{tag}_w_gate"), weight(f"{tag}_w_up"), weight(f"{tag}_w_down")
        G, Fj, D = wd.shape
        send(f"{tag}_w_down", mm(hid, dxh, mode="tn", name=f"{tag}_dwd", out_dtype=F32, tm=Fj, tn=1024))
        da, db = _ffn_dact(dxh, wd, ga, gb, f"{tag}_dact")
        send(f"{tag}_w_gate", mm(h, da, mode="tn", name=f"{tag}_dwg", out_dtype=F32, tm=1024, tn=Fj, out_groups=G))
        dwu = mm(h, db, mode="tn", name=f"{tag}_dwu", out_dtype=F32, tm=1024, tn=Fj, out_groups=G)
        if last:
            pending.append(emit_begin(f"{tag}_w_up", dwu))
            pending.append(emit_finish(f"{tag}_w_up", pending[-1]))
        else:
            send(f"{tag}_w_up", dwu)
        return behind(_ffn_dh, da, db, wg, wu, f"{tag}_dh")

    x1, ffn1_saved = ffn_fwd(x, "ffn1", dep0)
    h2 = _rms_fwd(x1, vec["mix_norm"], "mix_norm")
    w_in = weight("w_in", h2)
    proj = _matmul(h2, w_in, mode="nn", name="w_in", out_dtype=F32, tn=1408)
    pool_w, w_a2 = weight("pool_w", h2), weight("gla_w_a2", h2)
    y_pool, dpool = _pool_fwd(proj, pool_w, vec["pool_scale"], "pool_fwd")
    ymix, states = _gla_fwd(proj, y_pool, w_a2, vec["gla_b_a"], vec["gla_head_norm"], "gla_fwd")
    w_out = weight("w_out", ymix)
    x2 = _matmul(ymix, w_out, mode="nn", name="w_out", out_dtype=F32, res=x1)
    h3 = _rms_fwd(x2, vec["xattn_norm"], "xattn_norm")
    mh = _rms_fwd(mem, vec["mem_norm"], "mem_norm")
    w_q = weight("xattn_w_q", h3)
    q = _matmul(h3, w_q, mode="nn", name="xattn_q", out_dtype=BF16)
    w_kv = weight("xattn_w_kv", q)
    kv = _matmul(mh, w_kv, mode="nn", name="xattn_kv", out_dtype=BF16, b_groups=True, tn=1024)
    o = _xattn_fwd(q, kv, "xattn_fwd")
    w_o = weight("xattn_w_o", o)
    x3 = _matmul(o, w_o, mode="nn", name="xattn_o", out_dtype=F32, res=x2)
    x4, ffn2_saved = ffn_fwd(x3, "ffn2", None)
    sq, dx4, dx4h, g["final_norm"] = _loss_head(x4, vec["final_norm"], target, "loss_head")
    loss = lax.psum(0.5 * jnp.sum(sq) / x.shape[-1], ("x", "y", "c"))
    pending.append(loss.reshape(1, 1))

    dh = ffn_bwd(dx4h, ffn2_saved, "ffn2", False)
    dx3, dx3b, g["ffn2_norm"] = _rms_bwd(x3, vec["ffn2_norm"], dh, dx4, "ffn2_norm_bwd", lowp=1.0)
    send("xattn_w_o", mm(o, dx3b, mode="tn", name="xattn_dwo", out_dtype=F32, tm=1024, tn=1024))
    do = mm(dx3b, w_o, mode="nt", name="xattn_do", out_dtype=BF16)
    dq, dkv = _xattn_bwd(q, kv, do, "xattn_bwd")
    send("xattn_w_q", mm(h3, dq, mode="tn", name="xattn_dwq", out_dtype=F32, tm=1024, tn=1024))
    dh3 = mm(dq, w_q, mode="nt", name="xattn_dh", out_dtype=F32)
    dkvb = _cast(dkv, BF16, "dkv_cast")
    send("xattn_w_kv", mm(mh, dkvb, mode="tn", name="xattn_dwkv", out_dtype=F32, tm=1024, tn=1024, out_groups=N_SHARDS))
    dmh = mm(dkvb, w_kv, mode="nt", name="xattn_dmh", out_dtype=F32, b_groups=True, tk=1024)
    _, g["mem_norm"] = _rms_bwd(mem, vec["mem_norm"], dmh, None, "mem_norm_bwd")
    pending.append(g["mem_norm"])
    dx2, dx2b, g["xattn_norm"] = _rms_bwd(x2, vec["xattn_norm"], dh3, dx3, "xattn_norm_bwd", lowp=1.0)
    send("w_out", mm(ymix, dx2b, mode="tn", name="dw_out", out_dtype=F32, tm=1024, tn=1024))
    dymix = mm(dx2b, w_out, mode="nt", name="dymix", out_dtype=F32)
    du, dpool_w, g["pool_scale"] = _pool_bwd(dymix, dpool, pool_w, vec["pool_scale"], "pool_bwd")
    send("pool_w", dpool_w)
    dproj, dw_a2, g["gla_b_a"], g["gla_head_norm"] = _gla_bwd(
        proj, states, dymix, du, w_a2, vec["gla_b_a"], vec["gla_head_norm"], "gla_bwd")
    send("gla_w_a2", dw_a2)
    send("w_in", mm(h2, dproj, mode="tn", name="dw_in", out_dtype=F32, tm=1024, tn=1408))
    dh2 = mm(dproj, w_in, mode="nt", name="dh2", out_dtype=F32, tn=1024, tk=dproj.shape[1])
    pending.extend(early_update(dh2))
    dx1, dx1h, g["mix_norm"] = _rms_bwd(x1, vec["mix_norm"], dh2, dx2, "mix_norm_bwd", lowp=0.5)
    dh = ffn_bwd(dx1h, ffn1_saved, "ffn1", True)
    dx0, g["ffn1_norm"] = _rms_bwd(x, vec["ffn1_norm"], dh, dx1, "ffn1_norm_bwd")
    return loss, dx0, g


def _place():
    x, y, c = lax.axis_index("x"), lax.axis_index("y"), lax.axis_index("c")
    chips = [(1 - x, y), (x, 1 - y), (1 - x, 1 - y)]
    return x, y, c, chips


def _ids():
    return jnp.stack([2 * lax.axis_index("x") + lax.axis_index("y"), lax.axis_index("c")]).astype(jnp.int32)


def _hbm(a):
    return pltpu.with_memory_space_constraint(a, pltpu.HBM)


def _cast_to_slot(w2d, dtype, name, dep=None):
    R, C = w2d.shape
    tr = _tile(R, max(16, (4 << 20) // (4 * C) // 16 * 16))

    def body(i_ref, w_ref, *rest):
        rest[-1][...] = w_ref[...].astype(dtype)

    in_specs = [pl.BlockSpec((tr, C), lambda r, i: (r, 0))]
    operands = [w2d]
    if dep is not None:
        in_specs.append(pl.BlockSpec(dep.shape, lambda r, i: (0, 0)))
        operands.append(dep)
    grid_spec = pltpu.PrefetchScalarGridSpec(num_scalar_prefetch=1, grid=(R // tr,), in_specs=in_specs,
                                             out_specs=pl.BlockSpec((None, tr, C), lambda r, i: (i[0], r, 0)))
    return pl.pallas_call(body, name=name, grid_spec=grid_spec, out_shape=jax.ShapeDtypeStruct((N_SHARDS, R, C), dtype),
                          compiler_params=_params())(_ids(), *operands)


def _gather_copies(buf_ref, send_sems, recv_sems, incoming, whole):
    x, y, c, chips = _place()
    hr = buf_ref.shape[1] // 2
    copies = []
    for j, (px, py) in enumerate(chips):
        slot = 2 * px + py if incoming else 2 * x + y
        part = buf_ref.at[slot] if whole else buf_ref.at[slot, pl.ds(c * hr, hr), :]
        copies.append(pltpu.make_async_remote_copy(src_ref=part, dst_ref=part, send_sem=send_sems.at[j],
                                                   recv_sem=recv_sems.at[j], device_id=(px, py, c), device_id_type=MESH))
    return copies


def _gather_start(buf, name, whole):
    def body(b_ref, send_sems, recv_sems, b_thru, token):
        for cp in _gather_copies(b_ref, send_sems, recv_sems, False, whole):
            cp.start()
        token[...] = jnp.zeros_like(token)

    return pl.pallas_call(
        body, name=name,
        out_shape=(pltpu.SemaphoreType.DMA((3,)), pltpu.SemaphoreType.DMA((3,)), pltpu.HBM(buf.shape, buf.dtype),
                   jax.ShapeDtypeStruct((8, LANES), F32)),
        in_specs=(HBM,), out_specs=(SEM, SEM, HBM, pl.BlockSpec(memory_space=pltpu.VMEM)), input_output_aliases={0: 2},
        compiler_params=pltpu.CompilerParams(has_side_effects=EFFECT))(_hbm(buf))


def _gather_wait(send_sems, recv_sems, buf_thru, after, name, whole):
    def body(b_ref, send_sems, recv_sems, after_ref, b_out):
        for cp in _gather_copies(b_ref, send_sems, recv_sems, False, whole):
            cp.wait_send()
        for cp in _gather_copies(b_ref, send_sems, recv_sems, True, whole):
            cp.wait_recv()

    return pl.pallas_call(
        body, name=name, out_shape=pltpu.HBM(buf_thru.shape, buf_thru.dtype),
        in_specs=(HBM, SEM, SEM, ANY), out_specs=HBM, input_output_aliases={0: 0},
        compiler_params=pltpu.CompilerParams(has_side_effects=EFFECT))(buf_thru, send_sems, recv_sems, after)


def _gather_forward(buf, name):
    G, R, C = buf.shape
    hr = R // 2

    def body(b_ref, o_ref, send_sems, recv_sems):
        x, y, c, chips = _place()
        copies = []
        for j, (px, py) in enumerate(chips):
            half = o_ref.at[2 * px + py, pl.ds(c * hr, hr), :]
            copies.append(pltpu.make_async_remote_copy(src_ref=half, dst_ref=half, send_sem=send_sems.at[j],
                                                       recv_sem=recv_sems.at[j], device_id=(x, y, 1 - c),
                                                       device_id_type=MESH))
        for cp in copies:
            cp.start()
        for j, (px, py) in enumerate(chips):
            half = o_ref.at[2 * px + py, pl.ds((1 - c) * hr, hr), :]
            pltpu.make_async_remote_copy(src_ref=half, dst_ref=half, send_sem=send_sems.at[j], recv_sem=recv_sems.at[j],
                                         device_id=(x, y, 1 - c), device_id_type=MESH).wait_recv()
        for cp in copies:
            cp.wait_send()

    return pl.pallas_call(body, name=name, in_specs=[ANY], out_specs=ANY, out_shape=jax.ShapeDtypeStruct(buf.shape, buf.dtype),
                          input_output_aliases={0: 0},
                          scratch_shapes=[pltpu.SemaphoreType.DMA((3,)), pltpu.SemaphoreType.DMA((3,))])(buf)


def _pair_copy(g_ref, land_ref, send_sem, recv_sem):
    x, y, c, _ = _place()
    hr = g_ref.shape[1] // 2
    return pltpu.make_async_remote_copy(src_ref=g_ref.at[:, pl.ds((1 - c) * hr, hr), :], dst_ref=land_ref,
                                        send_sem=send_sem, recv_sem=recv_sem, device_id=(x, y, 1 - c), device_id_type=MESH)


def _pair_start(gfull, name):
    G, R, C = gfull.shape

    def body(g_ref, land_ref, send_sem, recv_sem, g_thru, land_thru, token):
        _pair_copy(g_ref, land_ref, send_sem, recv_sem).start()
        token[...] = jnp.zeros_like(token)

    return pl.pallas_call(
        body, name=name,
        out_shape=(pltpu.SemaphoreType.DMA(()), pltpu.SemaphoreType.DMA(()), pltpu.HBM(gfull.shape, F32),
                   pltpu.HBM((G, R // 2, C), F32), jax.ShapeDtypeStruct((8, LANES), F32)),
        in_specs=(HBM, HBM), out_specs=(SEM, SEM, HBM, HBM, pl.BlockSpec(memory_space=pltpu.VMEM)),
        input_output_aliases={0: 2, 1: 3},
        compiler_params=pltpu.CompilerParams(has_side_effects=EFFECT))(_hbm(gfull), _hbm(lax.empty((G, R // 2, C), F32)))


def _pair_wait(send_sem, recv_sem, g_thru, land_thru, after, name):
    def body(g_ref, land_ref, send_sem, recv_sem, after_ref, g_out, land_out):
        cp = _pair_copy(g_ref, land_ref, send_sem, recv_sem)
        cp.wait_send()
        cp.wait_recv()

    return pl.pallas_call(
        body, name=name, out_shape=(pltpu.HBM(g_thru.shape, F32), pltpu.HBM(land_thru.shape, F32)),
        in_specs=(HBM, HBM, SEM, SEM, ANY), out_specs=(HBM, HBM), input_output_aliases={0: 0, 1: 1},
        compiler_params=pltpu.CompilerParams(has_side_effects=EFFECT))(g_thru, land_thru, send_sem, recv_sem, after)


def _pair_add(gfull, other, name):
    G, R, C = gfull.shape
    hr = R // 2
    tr = _tile(hr, max(8, (2 * 1024 * 1024) // (4 * C) // 8 * 8))
    nr = hr // tr
    c = lax.axis_index("c")
    cidx = jnp.reshape(c, (1,)).astype(jnp.int32)

    def body(c_ref, a_ref, b_ref, o_ref):
        o_ref[...] = a_ref[...] + b_ref[...]

    grid_spec = pltpu.PrefetchScalarGridSpec(
        num_scalar_prefetch=1, grid=(G, nr),
        in_specs=[pl.BlockSpec((None, tr, C), lambda g, r, cr: (g, cr[0] * nr + r, 0)),
                  pl.BlockSpec((None, tr, C), lambda g, r, cr: (g, r, 0))],
        out_specs=pl.BlockSpec((None, tr, C), lambda g, r, cr: (g, r, 0)))
    return pl.pallas_call(body, name=name, grid_spec=grid_spec, out_shape=jax.ShapeDtypeStruct((G, hr, C), F32),
                          compiler_params=_params())(cidx, gfull, other)


def _chip_copies(p_ref, land_ref, send_sems, recv_sems, incoming):
    x, y, c, chips = _place()
    me = 2 * x + y
    copies = []
    for j, (px, py) in enumerate(chips):
        dst = land_ref.at[2 * px + py] if incoming else land_ref.at[me]
        copies.append(pltpu.make_async_remote_copy(src_ref=p_ref.at[2 * px + py], dst_ref=dst, send_sem=send_sems.at[j],
                                                   recv_sem=recv_sems.at[j], device_id=(px, py, c), device_id_type=MESH))
    return copies


def _chip_start(part, name):
    def body(p_ref, land_ref, send_sems, recv_sems, p_thru, land_thru, token):
        for cp in _chip_copies(p_ref, land_ref, send_sems, recv_sems, False):
            cp.start()
        token[...] = jnp.zeros_like(token)

    return pl.pallas_call(
        body, name=name,
        out_shape=(pltpu.SemaphoreType.DMA((3,)), pltpu.SemaphoreType.DMA((3,)), pltpu.HBM(part.shape, F32),
                   pltpu.HBM(part.shape, F32), jax.ShapeDtypeStruct((8, LANES), F32)),
        in_specs=(HBM, HBM), out_specs=(SEM, SEM, HBM, HBM, pl.BlockSpec(memory_space=pltpu.VMEM)),
        input_output_aliases={0: 2, 1: 3},
        compiler_params=pltpu.CompilerParams(has_side_effects=EFFECT))(_hbm(part), _hbm(lax.empty(part.shape, F32)))


def _chip_wait(send_sems, recv_sems, p_thru, land_thru, after, name):
    def body(p_ref, land_ref, send_sems, recv_sems, after_ref, p_out, land_out):
        for cp in _chip_copies(p_ref, land_ref, send_sems, recv_sems, False):
            cp.wait_send()
        for cp in _chip_copies(p_ref, land_ref, send_sems, recv_sems, True):
            cp.wait_recv()

    return pl.pallas_call(
        body, name=name, out_shape=(pltpu.HBM(p_thru.shape, F32), pltpu.HBM(p_thru.shape, F32)),
        in_specs=(HBM, HBM, SEM, SEM, ANY), out_specs=(HBM, HBM), input_output_aliases={0: 0, 1: 1},
        compiler_params=pltpu.CompilerParams(has_side_effects=EFFECT))(p_thru, land_thru, send_sems, recv_sems, after)


def _chip_sum(part, slots, name):
    G, R2, C = part.shape
    tr = _tile(R2, max(8, (1 << 20) // (4 * C) // 8 * 8))
    nr = R2 // tr

    def body(i_ref, p_ref, *rest):
        o_ref = rest[-1]
        acc = None
        for u in range(G):
            val = jnp.where(i_ref[0] == u, p_ref[...], rest[u][...])
            acc = val if acc is None else acc + val
        o_ref[...] = acc

    def slot_spec(u):
        return pl.BlockSpec((None, tr, C), lambda r, i: (jnp.where(i[0] == u, (u + 1) % G, u), r, 0))

    grid_spec = pltpu.PrefetchScalarGridSpec(
        num_scalar_prefetch=1, grid=(nr,),
        in_specs=[pl.BlockSpec((None, tr, C), lambda r, i: (i[0], r, 0))] + [slot_spec(u) for u in range(G)],
        out_specs=pl.BlockSpec((tr, C), lambda r, i: (i[1] * nr + r, 0)))
    return pl.pallas_call(body, name=name, grid_spec=grid_spec, out_shape=jax.ShapeDtypeStruct((2 * R2, C), F32),
                          compiler_params=_params())(_ids(), part, slots, slots, slots, slots)


def _sum_slots(slots, name):
    G, R2, C = slots.shape
    tr = _tile(R2, max(8, (1024 * 1024) // (4 * C) // 8 * 8))

    def body(s_ref, o_ref):
        acc = s_ref[0]
        for u in range(1, G):
            acc = acc + s_ref[u]
        o_ref[...] = acc

    return pl.pallas_call(body, name=name, grid=(R2 // tr,), in_specs=[pl.BlockSpec((G, tr, C), lambda r: (0, r, 0))],
                          out_specs=pl.BlockSpec((tr, C), lambda r: (r, 0)), out_shape=jax.ShapeDtypeStruct((R2, C), F32),
                          compiler_params=_params())(slots)


def _pair_join(full, name):
    R, C = full.shape
    R2 = R // 2

    def body(f_ref, o_ref, token, send_sem, recv_sem):
        x, y, c, _ = _place()
        token[...] = jnp.zeros_like(token)
        mine = o_ref.at[pl.ds(c * R2, R2), :]
        theirs = o_ref.at[pl.ds((1 - c) * R2, R2), :]
        cp = pltpu.make_async_remote_copy(src_ref=mine, dst_ref=mine, send_sem=send_sem, recv_sem=recv_sem,
                                          device_id=(x, y, 1 - c), device_id_type=MESH)
        cp.start()
        pltpu.make_async_remote_copy(src_ref=theirs, dst_ref=theirs, send_sem=send_sem, recv_sem=recv_sem,
                                     device_id=(x, y, 1 - c), device_id_type=MESH).wait_recv()
        cp.wait_send()

    return pl.pallas_call(body, name=name, in_specs=[ANY], out_specs=[ANY, pl.BlockSpec(memory_space=pltpu.VMEM)],
                          out_shape=[jax.ShapeDtypeStruct((R, C), F32), jax.ShapeDtypeStruct((8, LANES), F32)],
                          input_output_aliases={0: 0},
                          scratch_shapes=[pltpu.SemaphoreType.DMA, pltpu.SemaphoreType.DMA])(full)


def _all_reduce_small(v, name):
    R, C = v.shape

    def gather_body(v_ref, out_ref, send_sems, recv_sems, local_sem):
        x, y, c, _ = _place()
        me = 4 * x + 2 * y + c
        mine = pltpu.make_async_copy(v_ref, out_ref.at[me], local_sem)
        mine.start()
        flips = [(fx, fy, fc) for fx in (0, 1) for fy in (0, 1) for fc in (0, 1)][1:]
        copies = []
        for j, (fx, fy, fc) in enumerate(flips):
            peer = (x ^ fx, y ^ fy, c ^ fc)
            copies.append(pltpu.make_async_remote_copy(src_ref=v_ref, dst_ref=out_ref.at[me], send_sem=send_sems.at[j],
                                                       recv_sem=recv_sems.at[j], device_id=peer, device_id_type=MESH))
        for cp in copies:
            cp.start()
        for j, (fx, fy, fc) in enumerate(flips):
            peer = (x ^ fx, y ^ fy, c ^ fc)
            pltpu.make_async_remote_copy(src_ref=v_ref, dst_ref=out_ref.at[4 * peer[0] + 2 * peer[1] + peer[2]],
                                         send_sem=send_sems.at[j], recv_sem=recv_sems.at[j], device_id=peer,
                                         device_id_type=MESH).wait_recv()
        for cp in copies:
            cp.wait_send()
        mine.wait()

    slots = pl.pallas_call(gather_body, name=name, in_specs=[ANY], out_specs=ANY,
                           out_shape=jax.ShapeDtypeStruct((8, R, C), F32),
                           scratch_shapes=[pltpu.SemaphoreType.DMA((7,)), pltpu.SemaphoreType.DMA((7,)),
                                           pltpu.SemaphoreType.DMA])(v)
    return _sum_slots(slots, f"{name}_sum")


def _adamw(w, g, m, v, name, dep=()):
    R, C = w.shape
    tr = _tile(R, max(8, (2 << 20) // (4 * C) // 8 * 8))
    bc1 = 1.0 - ADAM_B1 ** ADAM_STEP
    bc2 = 1.0 - ADAM_B2 ** ADAM_STEP

    def body(w_ref, g_ref, m_ref, v_ref, *rest):
        go_ref, d_ref, nm_ref, nv_ref = rest[len(dep):]
        gv = g_ref[...]
        go_ref[...] = gv
        nm = ADAM_B1 * m_ref[...] + (1.0 - ADAM_B1) * gv
        nv = ADAM_B2 * v_ref[...] + (1.0 - ADAM_B2) * (gv * gv)
        nm_ref[...] = nm
        nv_ref[...] = nv
        d_ref[...] = -ADAM_LR * ((nm / bc1) / (jnp.sqrt(nv / bc2) + ADAM_EPS) + ADAM_WD * w_ref[...])

    blk = pl.BlockSpec((tr, C), lambda r: (r, 0))
    out = jax.ShapeDtypeStruct((R, C), F32)
    in_specs = [blk] * 4 + [pl.BlockSpec(d.shape, lambda r: (0, 0)) for d in dep]
    return pl.pallas_call(body, name=name, grid=(R // tr,), in_specs=in_specs, out_specs=[blk] * 4, out_shape=[out] * 4,
                          compiler_params=_params())(w, g, m, v, *dep)


SC_TILES = 32
SC_LANES = 16
SC_ROWS = 8


def _sc_mesh():
    return plsc.VectorSubcoreMesh(core_axis_name="sc_core", subcore_axis_name="sc_subcore")


def _pair_add_sc(gfull, other, name):
    G, R, C = gfull.shape
    hr = R // 2
    tiles_per_shard = SC_TILES // G
    per_tile = hr // SC_ROWS // tiles_per_shard

    def body(g_hbm, o_hbm, out_hbm, gb, ob):
        c = lax.axis_index("c")
        tile = lax.axis_index("sc_subcore") * 2 + lax.axis_index("sc_core")
        t = tile // tiles_per_shard
        first = (tile % tiles_per_shard) * per_tile

        @pl.loop(0, per_tile)
        def _(k):
            rr = (first + k) * SC_ROWS
            pltpu.sync_copy(g_hbm.at[t, pl.ds(c * hr + rr, SC_ROWS), :], gb)
            pltpu.sync_copy(o_hbm.at[t, pl.ds(rr, SC_ROWS), :], ob)

            @pl.loop(0, SC_ROWS)
            def _(i):
                @pl.loop(0, C, step=SC_LANES)
                def _(j):
                    at = (i, pl.ds(j, SC_LANES))
                    gb[at] = gb[at] + ob[at]

            pltpu.sync_copy(gb, out_hbm.at[t, pl.ds(rr, SC_ROWS), :])

    buf = pltpu.VMEM((SC_ROWS, C), F32)
    return pl.kernel(body, name=name, out_type=jax.ShapeDtypeStruct((G, hr, C), F32), mesh=_sc_mesh(),
                     scratch_types=[buf, buf])(gfull, other)


def _adamw_sc(w, g, m, v, name):
    R, C = w.shape
    per_tile = R // SC_TILES
    bc1 = 1.0 - ADAM_B1 ** ADAM_STEP
    bc2 = 1.0 - ADAM_B2 ** ADAM_STEP

    def body(w_hbm, g_hbm, m_hbm, v_hbm, go_hbm, d_hbm, nm_hbm, nv_hbm, wb, gb, mb, vb):
        tile = lax.axis_index("sc_subcore") * 2 + lax.axis_index("sc_core")
        base = tile * per_tile

        @pl.loop(0, per_tile, step=SC_ROWS)
        def _(r):
            rows = pl.ds(base + r, SC_ROWS)
            pltpu.sync_copy(w_hbm.at[rows, :], wb)
            pltpu.sync_copy(g_hbm.at[rows, :], gb)
            pltpu.sync_copy(m_hbm.at[rows, :], mb)
            pltpu.sync_copy(v_hbm.at[rows, :], vb)

            @pl.loop(0, SC_ROWS)
            def _(i):
                @pl.loop(0, C, step=SC_LANES)
                def _(j):
                    at = (i, pl.ds(j, SC_LANES))
                    gv = gb[at]
                    nm = ADAM_B1 * mb[at] + (1.0 - ADAM_B1) * gv
                    nv = ADAM_B2 * vb[at] + (1.0 - ADAM_B2) * (gv * gv)
                    mb[at] = nm
                    vb[at] = nv
                    wb[at] = -ADAM_LR * ((nm / bc1) / (jnp.sqrt(nv / bc2) + ADAM_EPS) + ADAM_WD * wb[at])

            pltpu.sync_copy(gb, go_hbm.at[rows, :])
            pltpu.sync_copy(wb, d_hbm.at[rows, :])
            pltpu.sync_copy(mb, nm_hbm.at[rows, :])
            pltpu.sync_copy(vb, nv_hbm.at[rows, :])

    out = jax.ShapeDtypeStruct((R, C), F32)
    buf = pltpu.VMEM((SC_ROWS, C), F32)
    return pl.kernel(body, name=name, out_type=(out, out, out, out),
                     mesh=plsc.VectorSubcoreMesh(core_axis_name="sc_core", subcore_axis_name="sc_subcore"),
                     scratch_types=[buf, buf, buf, buf],
                     cost_estimate=pl.CostEstimate(flops=16 * R * C, transcendentals=2 * R * C, bytes_accessed=32 * R * C),
                     )(w, g, m, v)


WEIGHTS = ['ffn1_norm', 'ffn1_w_gate', 'ffn1_w_up', 'ffn1_w_down', 'mix_norm', 'w_in', 'pool_w', 'pool_scale', 'gla_w_a2',
           'gla_b_a', 'gla_head_norm', 'w_out', 'xattn_norm', 'mem_norm', 'xattn_w_q', 'xattn_w_kv', 'xattn_w_o', 'ffn2_norm',
           'ffn2_w_gate', 'ffn2_w_up', 'ffn2_w_down', 'final_norm']
SHARDED = ['ffn1_w_gate', 'ffn1_w_up', 'ffn1_w_down', 'w_in', 'pool_w', 'gla_w_a2', 'w_out', 'xattn_w_q', 'xattn_w_kv',
           'xattn_w_o', 'ffn2_w_gate', 'ffn2_w_up', 'ffn2_w_down']
REPLICATED = [n for n in WEIGHTS if n not in SHARDED]
ON_SPARSECORE = ['ffn2_w_gate', 'ffn2_w_up', 'w_out', 'xattn_w_q', 'xattn_w_kv', 'xattn_w_o']
PAIR_SUM_ON_SPARSECORE = ['ffn2_w_down', 'ffn2_w_gate', 'ffn2_w_up', 'xattn_w_o', 'xattn_w_kv', 'pool_w', 'ffn1_w_down']
SMALL_COLS = 512


def _as2d(a):
    return a.reshape(-1, a.shape[-1])


def _finish_weight(name, gathered, wl):
    G, R, C = gathered.shape
    rank = wl["gla_w_a2"].shape[1]
    if name in ("w_out", "xattn_w_q", "xattn_w_o"):
        return gathered.reshape(G * R, C)
    if name == "w_in":
        w_in = jnp.transpose(gathered, (1, 0, 2)).reshape(R, G * C)
        main = G * C - rank
        return jnp.concatenate([w_in[:, :main], jnp.pad(w_in[:, main:], ((0, 0), (0, LANES - rank)))], axis=1)
    if name == "pool_w":
        NG, CJ, _ = wl[name].shape[1:]
        return jnp.transpose(gathered.reshape(G, NG, CJ, C), (1, 0, 2, 3)).reshape(NG, G * CJ, C)
    if name == "gla_w_a2":
        a2 = jnp.transpose(gathered, (1, 0, 2)).reshape(rank, G * C)
        return jnp.pad(a2, ((0, LANES - rank), (0, 0))).astype(BF16)
    return gathered


def _start_gathers(wl):
    started = {}
    token = None
    for n in SHARDED:
        whole = not n.startswith("ffn1")
        buf = _cast_to_slot(_as2d(wl[n]), BF16, f"slot_{n}", dep=token)
        send_sems, recv_sems, thru, token = _gather_start(buf, f"gather_start_{n}", whole)
        started[n] = (send_sems, recv_sems, thru, whole)
    cache = {}

    def weight(n, after=None):
        if n not in cache:
            *handles, whole = started[n]
            buf = _gather_wait(*handles, after, f"gather_wait_{n}", whole)
            if not whole:
                buf = _gather_forward(buf, f"gather_forward_{n}")
            cache[n] = _finish_weight(n, buf, wl)
        return cache[n]

    return weight, token


def _shard_major(name, gfull, wl):
    R, C = _as2d(wl[name]).shape
    if name in ("ffn1_w_gate", "ffn1_w_up", "ffn2_w_gate", "ffn2_w_up", "xattn_w_kv"):
        return gfull
    if name in ("ffn1_w_down", "ffn2_w_down", "w_out", "xattn_w_q", "xattn_w_o"):
        return gfull.reshape(N_SHARDS, R, C)
    if name == "w_in":
        return jnp.transpose(gfull[:, :N_SHARDS * C].reshape(R, N_SHARDS, C), (1, 0, 2))
    if name == "pool_w":
        NG, CJ, _ = wl[name].shape[1:]
        return jnp.transpose(gfull.reshape(NG, N_SHARDS, CJ, C), (1, 0, 2, 3)).reshape(N_SHARDS, R, C)
    assert name == "gla_w_a2"
    return jnp.transpose(gfull[:R].reshape(R, N_SHARDS, C), (1, 0, 2))


def kernel(x, mem, ffn1_norm, ffn1_w_gate, ffn1_w_up, ffn1_w_down, mix_norm, w_in, pool_w, pool_scale, gla_w_a2, gla_b_a, gla_head_norm, w_out, xattn_norm, mem_norm, xattn_w_q, xattn_w_kv, xattn_w_o, ffn2_norm, ffn2_w_gate, ffn2_w_up, ffn2_w_down, final_norm, loss_target, m_ffn1_norm, m_ffn1_w_gate, m_ffn1_w_up, m_ffn1_w_down, m_mix_norm, m_w_in, m_pool_w, m_pool_scale, m_gla_w_a2, m_gla_b_a, m_gla_head_norm, m_w_out, m_xattn_norm, m_mem_norm, m_xattn_w_q, m_xattn_w_kv, m_xattn_w_o, m_ffn2_norm, m_ffn2_w_gate, m_ffn2_w_up, m_ffn2_w_down, m_final_norm, v_ffn1_norm, v_ffn1_w_gate, v_ffn1_w_up, v_ffn1_w_down, v_mix_norm, v_w_in, v_pool_w, v_pool_scale, v_gla_w_a2, v_gla_b_a, v_gla_head_norm, v_w_out, v_xattn_norm, v_mem_norm, v_xattn_w_q, v_xattn_w_kv, v_xattn_w_o, v_ffn2_norm, v_ffn2_w_gate, v_ffn2_w_up, v_ffn2_w_down, v_final_norm):
    given = dict(locals())
    wl = {n: given[n] for n in WEIGHTS}
    ml = {n: given["m_" + n] for n in WEIGHTS}
    vl = {n: given["v_" + n] for n in WEIGHTS}

    vec = {n: wl[n].reshape(1, -1) for n in REPLICATED}
    weight, dep0 = _start_gathers(wl)
    in_flight = {}

    pair_flight = {}

    def emit_begin(n, gfull):
        *pair_flight[n], token = _pair_start(_shard_major(n, gfull, wl), f"{n}_pair_start")
        return token

    summing = {}

    def emit_finish(n, after):
        gsm, other = _pair_wait(*pair_flight.pop(n), after, f"{n}_pair_wait")
        if n in PAIR_SUM_ON_SPARSECORE:
            summing[n] = _pair_add_sc(gsm, other, f"{n}_pair_add_sc")
            return None
        *in_flight[n], token = _chip_start(_pair_add(gsm, other, f"{n}_pair_add"), f"{n}_chip_start")
        return token

    def emit_send(n):
        *in_flight[n], token = _chip_start(summing.pop(n), f"{n}_chip_start")
        return token

    grads = {}
    updates = {}

    def reduce_done(n, after):
        part, slots = _chip_wait(*in_flight.pop(n), after, f"{n}_chip_wait")
        grads[n], token = _pair_join(_chip_sum(part, slots, f"{n}_chip_sum"), f"{n}_pair_join")
        return token

    def early_update(after):
        tokens = [reduce_done(n, after) for n in ON_SPARSECORE]
        for n in ON_SPARSECORE:
            g2 = grads[n]
            updates[n] = _adamw_sc(wl[n].reshape(g2.shape), g2, ml[n].reshape(g2.shape), vl[n].reshape(g2.shape),
                                   f"adamw_sc_{n}")
        return tokens

    loss, dx0, g = _local_step(x[0], mem[0], loss_target[0], vec, weight,
                               (emit_begin, emit_finish, emit_send, early_update), dep0)
    assert not summing

    for n in list(in_flight):
        reduce_done(n, dx0)
    widths = [wl[n].size for n in REPLICATED]
    total = sum(widths)
    rows = -(-total // SMALL_COLS)
    rows = -(-rows // 8) * 8
    packed = jnp.concatenate([g[n].reshape(-1) for n in REPLICATED] + [jnp.zeros((rows * SMALL_COLS - total,), F32)])
    summed = _all_reduce_small(packed.reshape(rows, SMALL_COLS), "small_all_reduce").reshape(-1)
    off = 0
    for n, width in zip(REPLICATED, widths):
        grads[n] = summed[off:off + width].reshape(1, width)
        off += width

    out_g, out_d, out_m, out_v = [], [], [], []
    for n in WEIGHTS:
        shape = wl[n].shape
        g2 = grads[n]
        if n in updates:
            go, d, nm, nv = updates[n]
        else:
            dep = tuple(updates[k][1][:8, :LANES] for k in updates) if n == "w_in" else ()
            go, d, nm, nv = _adamw(wl[n].reshape(g2.shape), g2, ml[n].reshape(g2.shape), vl[n].reshape(g2.shape),
                                   f"adamw_{n}", dep)
        out_g.append(go.reshape(shape))
        out_d.append(d.reshape(shape))
        out_m.append(nm.reshape(shape))
        out_v.append(nv.reshape(shape))
    return (loss, dx0.reshape(x.shape), *out_g, *out_d, *out_m, *out_v)
```

```python
import functools

import jax
import jax.numpy as jnp
from jax import lax
from jax.experimental import pallas as pl
from jax.experimental.pallas import tpu as pltpu
from jax.experimental.pallas import tpu_sc as plsc

F32 = jnp.float32
BF16 = jnp.bfloat16
MESH = pl.DeviceIdType.MESH

RMS_EPS = 1e-6
CHUNK = 64
POOL_WINDOWS = (2, 4, 8, 16)
POOL_HALO = 16
N_HEADS = 4
GATE_TEMP = 16.0
ADAM_LR, ADAM_B1, ADAM_B2, ADAM_EPS, ADAM_WD, ADAM_STEP = 0.001, 0.9, 0.999, 1e-08, 0.01, 10
N_SHARDS = 4
LANES = 128
MXU_COLS = 256
TOKENS_PER_STEP = 2048
VMEM_LIMIT = 58 * 1024 * 1024

ANY = pl.BlockSpec(memory_space=pl.ANY)
HBM = pl.BlockSpec(memory_space=pltpu.HBM)
SEM = pl.BlockSpec(memory_space=pltpu.SEMAPHORE)
EFFECT = pltpu.SideEffectType.DATAFLOW_SIDE_EFFECTING


def _params(**kw):
    return pltpu.CompilerParams(vmem_limit_bytes=VMEM_LIMIT, **kw)


def _tile(n, want):
    for unit in (LANES, 8):
        t = (min(want, n) // unit) * unit
        while t >= unit:
            if n % t == 0:
                return t
            t -= unit
    return n


def _dot(a, b, dims):
    return lax.dot_general(a, b, (dims, ((), ())), preferred_element_type=F32)


def _nn(a, b):
    return _dot(a, b, ((1,), (0,)))


def _nt(a, b):
    return _dot(a, b, ((1,), (1,)))


def _tn(a, b):
    return _dot(a, b, ((0,), (0,)))


def _sigmoid(x):
    return 1.0 / (1.0 + jnp.exp(-x))


def _matmul_cost(M, N, K, operands, out_shape):
    nbytes = sum(a.size * a.dtype.itemsize for a in operands) + out_shape.size * out_shape.dtype.itemsize
    return pl.CostEstimate(flops=2 * M * N * K, transcendentals=0, bytes_accessed=nbytes)


def _matmul(a, b, *, mode, name, out_dtype, tm=512, tn=2048, tk=2048, res=None, scale=1.0, b_groups=False, out_groups=0,
            dep=()):
    if mode == "tn":
        K, M = a.shape
    else:
        M, K = a.shape
    if mode == "nn":
        if b_groups:
            G, _, Nj = b.shape
            N = G * Nj
        else:
            N = b.shape[1]
    elif mode == "nt":
        if b_groups:
            G, N, Kj = b.shape
            assert G * Kj == K
        else:
            N = b.shape[0]
    else:
        N = b.shape[1]
    tm = _tile(M, tm)
    if mode == "nn" and b_groups:
        tn = _tile(Nj, tn)
    elif out_groups:
        tn = _tile(N // out_groups, tn)
    else:
        tn = _tile(N, tn)
    if mode == "nt" and b_groups:
        tk = _tile(Kj, tk)
    else:
        tk = _tile(K, tk)
    nk = K // tk
    grid = (M // tm, N // tn, nk)

    if mode == "tn":
        a_spec = pl.BlockSpec((tk, tm), lambda i, j, k: (k, i))
        b_spec = pl.BlockSpec((tk, tn), lambda i, j, k: (k, j))
        dims = ((0,), (0,))
    elif mode == "nn":
        a_spec = pl.BlockSpec((tm, tk), lambda i, j, k: (i, k))
        if b_groups:
            npj = Nj // tn
            b_spec = pl.BlockSpec((None, tk, tn), lambda i, j, k: (j // npj, k, j % npj))
        else:
            b_spec = pl.BlockSpec((tk, tn), lambda i, j, k: (k, j))
        dims = ((1,), (0,))
    else:
        a_spec = pl.BlockSpec((tm, tk), lambda i, j, k: (i, k))
        if b_groups:
            kpj = Kj // tk
            b_spec = pl.BlockSpec((None, tn, tk), lambda i, j, k: (k // kpj, j, k % kpj))
        else:
            b_spec = pl.BlockSpec((tn, tk), lambda i, j, k: (j, k))
        dims = ((1,), (1,))
    if out_groups:
        npj = (N // out_groups) // tn
        o_spec = pl.BlockSpec((None, tm, tn), lambda i, j, k: (j // npj, i, j % npj))
        out_shape = jax.ShapeDtypeStruct((out_groups, M, N // out_groups), out_dtype)
    else:
        o_spec = pl.BlockSpec((tm, tn), lambda i, j, k: (i, j))
        out_shape = jax.ShapeDtypeStruct((M, N), out_dtype)
    in_specs = [a_spec, b_spec]
    operands = [a, b]
    if res is not None:
        in_specs.append(pl.BlockSpec((tm, tn), lambda i, j, k: (i, j)))
        operands.append(res)
    has_res = res is not None
    n_dep = len(dep)
    for d in dep:
        in_specs.append(pl.BlockSpec(d.shape, lambda i, j, k: (0, 0)))
        operands.append(d)

    def body(*refs):
        if has_res:
            a_ref, b_ref, r_ref = refs[:3]
        else:
            a_ref, b_ref = refs[:2]
            r_ref = None
        o_ref = refs[2 + has_res + n_dep]

        def finish(acc):
            if scale != 1.0:
                acc = acc * scale
            if r_ref is not None:
                acc = r_ref[...] + acc
            o_ref[...] = acc.astype(o_ref.dtype)

        part = _dot(a_ref[...], b_ref[...], dims)
        if nk == 1:
            finish(part)
        else:
            acc_ref = o_ref if in_place else refs[-1]
            k = pl.program_id(2)

            @pl.when(k == 0)
            def _():
                acc_ref[...] = part

            @pl.when(k > 0)
            def _():
                acc_ref[...] += part

            if not in_place:
                @pl.when(k == nk - 1)
                def _():
                    finish(acc_ref[...])

    in_place = out_dtype == F32 and res is None and scale == 1.0
    scratch = [] if nk == 1 or in_place else [pltpu.VMEM((tm, tn), F32)]
    return pl.pallas_call(body, name=name, grid=grid, in_specs=in_specs, out_specs=o_spec, out_shape=out_shape,
                          scratch_shapes=scratch, compiler_params=_params(),
                          cost_estimate=_matmul_cost(M, N, K, operands, out_shape))(*operands)


def _rms_fwd(x, gain, name, tm=512, dep=None):
    S, D = x.shape
    tm = _tile(S, tm)

    def body(x_ref, g_ref, *rest):
        o_ref = rest[-1]
        xv = x_ref[...]
        r = lax.rsqrt(jnp.mean(xv * xv, axis=-1, keepdims=True) + RMS_EPS)
        o_ref[...] = (xv * r * g_ref[...]).astype(o_ref.dtype)

    in_specs = [pl.BlockSpec((tm, D), lambda i: (i, 0)), pl.BlockSpec((1, D), lambda i: (0, 0))]
    operands = [x, gain]
    if dep is not None:
        in_specs.append(pl.BlockSpec(dep.shape, lambda i: (0, 0)))
        operands.append(dep)
    return pl.pallas_call(body, name=name, grid=(S // tm,), in_specs=in_specs,
                          out_specs=pl.BlockSpec((tm, D), lambda i: (i, 0)),
                          out_shape=jax.ShapeDtypeStruct((S, D), BF16), compiler_params=_params())(*operands)


def _rms_bwd(x, gain, dh, dres, name, lowp=None, tm=512):
    half = lowp is not None
    S, D = x.shape
    tm = _tile(S, tm)
    has_res = dres is not None

    def body(*refs):
        if has_res:
            x_ref, g_ref, dh_ref, dr_ref = refs[:4]
            outs = refs[4:]
        else:
            x_ref, g_ref, dh_ref = refs[:3]
            dr_ref = None
            outs = refs[3:]
        dx_ref, dg_ref = outs[0], outs[-1]
        xv = x_ref[...]
        dhv = dh_ref[...].astype(F32)
        r = lax.rsqrt(jnp.mean(xv * xv, axis=-1, keepdims=True) + RMS_EPS)
        gy = dhv * g_ref[...]
        dx = r * gy - xv * (r * r * r) * jnp.mean(gy * xv, axis=-1, keepdims=True)
        if dr_ref is not None:
            dx = dx + dr_ref[...]
        dx_ref[...] = dx
        if half:
            outs[1][...] = (dx if lowp == 1.0 else lowp * dx).astype(BF16)
        part = jnp.sum(dhv * xv * r, axis=0, keepdims=True)

        @pl.when(pl.program_id(0) == 0)
        def _():
            dg_ref[...] = part

        @pl.when(pl.program_id(0) > 0)
        def _():
            dg_ref[...] += part

    row = pl.BlockSpec((tm, D), lambda i: (i, 0))
    vec = pl.BlockSpec((1, D), lambda i: (0, 0))
    in_specs = [row, vec, row] + ([row] if has_res else [])
    operands = [x, gain, dh] + ([dres] if has_res else [])
    out_specs = [row] + ([row] if half else []) + [vec]
    out_shape = [jax.ShapeDtypeStruct((S, D), F32)] + ([jax.ShapeDtypeStruct((S, D), BF16)] if half else []) + [
        jax.ShapeDtypeStruct((1, D), F32)]
    return pl.pallas_call(body, name=name, grid=(S // tm,), in_specs=in_specs, out_specs=out_specs, out_shape=out_shape,
                          compiler_params=_params())(*operands)


def _loss_head(x, gain, target, name, tm=512):
    S, D = x.shape
    tm = _tile(S, tm)

    def body(x_ref, g_ref, t_ref, sq_ref, dx_ref, dxh_ref, dg_ref):
        xv = x_ref[...]
        r = lax.rsqrt(jnp.mean(xv * xv, axis=-1, keepdims=True) + RMS_EPS)
        xn = xv * r
        err = xn * g_ref[...] - t_ref[...]
        dout = err * (1.0 / D)
        gy = dout * g_ref[...]
        dx = r * gy - xv * (r * r * r) * jnp.mean(gy * xv, axis=-1, keepdims=True)
        dx_ref[...] = dx
        dxh_ref[...] = (0.5 * dx).astype(BF16)
        sq = jnp.sum(err * err, axis=0, keepdims=True)
        dg = jnp.sum(dout * xn, axis=0, keepdims=True)

        @pl.when(pl.program_id(0) == 0)
        def _():
            sq_ref[...] = sq
            dg_ref[...] = dg

        @pl.when(pl.program_id(0) > 0)
        def _():
            sq_ref[...] += sq
            dg_ref[...] += dg

    row = pl.BlockSpec((tm, D), lambda i: (i, 0))
    vec = pl.BlockSpec((1, D), lambda i: (0, 0))
    return pl.pallas_call(body, name=name, grid=(S // tm,), in_specs=[row, vec, row], out_specs=[vec, row, row, vec],
                          out_shape=[jax.ShapeDtypeStruct((1, D), F32), jax.ShapeDtypeStruct((S, D), F32),
                                     jax.ShapeDtypeStruct((S, D), BF16), jax.ShapeDtypeStruct((1, D), F32)],
                          compiler_params=_params())(x, gain, target)


def _cast(x, dtype, name, scale=1.0, tm=256):
    S, D = x.shape
    tm = _tile(S, tm)

    def body(x_ref, o_ref):
        o_ref[...] = (x_ref[...] * scale).astype(o_ref.dtype)

    row = pl.BlockSpec((tm, D), lambda i: (i, 0))
    return pl.pallas_call(body, name=name, grid=(S // tm,), in_specs=[row], out_specs=row,
                          out_shape=jax.ShapeDtypeStruct((S, D), dtype), compiler_params=_params())(x)


def _ffn_up(h, wg, wu, name, tm=512):
    S, D = h.shape
    G, _, Fj = wg.shape
    tm = _tile(S, tm)

    def body(h_ref, wg_ref, wu_ref, ga_ref, gb_ref, hid_ref):
        hv = h_ref[...]
        a = _nn(hv, wg_ref[...])
        b = _nn(hv, wu_ref[...])
        s = _sigmoid(a)
        silu = a * s
        ga_ref[...] = (b * (s * (1.0 + a * (1.0 - s)))).astype(BF16)
        gb_ref[...] = silu.astype(BF16)
        hid_ref[...] = (silu * b).astype(BF16)

    w_spec = pl.BlockSpec((None, D, Fj), lambda g, i: (g, 0, 0))
    o_spec = pl.BlockSpec((tm, Fj), lambda g, i: (i, g))
    out = jax.ShapeDtypeStruct((S, G * Fj), BF16)
    return pl.pallas_call(body, name=name, grid=(G, S // tm),
                          in_specs=[pl.BlockSpec((tm, D), lambda g, i: (i, 0)), w_spec, w_spec],
                          out_specs=[o_spec, o_spec, o_spec], out_shape=[out, out, out], compiler_params=_params())(h, wg, wu)


def _ffn_dact(dxh, wd, ga, gb, name, tm=512):
    S, D = dxh.shape
    G, Fj, _ = wd.shape
    tm = _tile(S, tm)

    def body(dx_ref, wd_ref, ga_ref, gb_ref, da_ref, db_ref):
        dhid = _nt(dx_ref[...], wd_ref[...])
        da_ref[...] = (dhid * ga_ref[...].astype(F32)).astype(BF16)
        db_ref[...] = (dhid * gb_ref[...].astype(F32)).astype(BF16)

    blk = pl.BlockSpec((tm, Fj), lambda g, i: (i, g))
    out = jax.ShapeDtypeStruct((S, G * Fj), BF16)
    return pl.pallas_call(body, name=name, grid=(G, S // tm),
                          in_specs=[pl.BlockSpec((tm, D), lambda g, i: (i, 0)),
                                    pl.BlockSpec((None, Fj, D), lambda g, i: (g, 0, 0)), blk, blk],
                          out_specs=[blk, blk], out_shape=[out, out], compiler_params=_params())(dxh, wd, ga, gb)


def _ffn_dh(da, db, wg, wu, name, dep=(), tm=512):
    S = da.shape[0]
    G, D, Fj = wg.shape
    tm = _tile(S, tm)

    def body(da_ref, db_ref, wg_ref, wu_ref, *rest):
        o_ref = rest[-1]
        part = _nt(da_ref[...], wg_ref[...]) + _nt(db_ref[...], wu_ref[...])

        @pl.when(pl.program_id(1) == 0)
        def _():
            o_ref[...] = part

        @pl.when(pl.program_id(1) > 0)
        def _():
            o_ref[...] += part

    act = pl.BlockSpec((tm, Fj), lambda i, g: (i, g))
    w_spec = pl.BlockSpec((None, D, Fj), lambda i, g: (g, 0, 0))
    in_specs = [act, act, w_spec, w_spec] + [pl.BlockSpec(d.shape, lambda i, g: (0, 0)) for d in dep]
    return pl.pallas_call(body, name=name, grid=(S // tm, G), in_specs=in_specs,
                          out_specs=pl.BlockSpec((tm, D), lambda i, g: (i, 0)),
                          out_shape=jax.ShapeDtypeStruct((S, D), F32), compiler_params=_params(),
                          cost_estimate=_matmul_cost(S, D, 2 * G * Fj, (da, db, wg, wu), jax.ShapeDtypeStruct((S, D), F32)),
                          )(da, db, wg, wu, *dep)


def _pool_fwd(proj, pool_w, pool_scale, name, tm=512):
    S = proj.shape[0]
    NG, C, _ = pool_w.shape
    DP = NG * C
    tm = _tile(S, tm)
    hb = tm // POOL_HALO
    n_ext = tm + POOL_HALO

    def body(u_ref, halo_ref, w_ref, sc_ref, y_ref, d_ref):
        i = pl.program_id(0)
        t = lax.broadcasted_iota(jnp.int32, (tm, 1), 0) + i * tm
        for g, win in enumerate(POOL_WINDOWS):
            cols = slice(g * C, (g + 1) * C)
            ug = u_ref[:, cols]
            halo = jnp.where(i > 0, halo_ref[:, cols], 0.0)
            acc = jnp.concatenate([halo, ug], axis=0)
            step = 1
            while step < win:
                acc = acc + pltpu.roll(acc, step, 0)
                step *= 2
            count = jnp.minimum(t + 1, win).astype(F32)
            d = (acc[POOL_HALO:, :] / count - ug).astype(BF16)
            d_ref[:, cols] = d
            y_ref[:, cols] = (_nn(d, w_ref[g]) * sc_ref[:, cols]).astype(BF16)

    del n_ext
    return pl.pallas_call(
        body, name=name, grid=(S // tm,),
        in_specs=[pl.BlockSpec((tm, DP), lambda i: (i, 0)),
                  pl.BlockSpec((POOL_HALO, DP), lambda i: (jnp.maximum(i * hb - 1, 0), 0)),
                  pl.BlockSpec((NG, C, C), lambda i: (0, 0, 0)), pl.BlockSpec((1, DP), lambda i: (0, 0))],
        out_specs=[pl.BlockSpec((tm, DP), lambda i: (i, 0)), pl.BlockSpec((tm, DP), lambda i: (i, 0))],
        out_shape=[jax.ShapeDtypeStruct((S, DP), BF16), jax.ShapeDtypeStruct((S, DP), BF16)],
        compiler_params=_params())(proj, proj, pool_w, pool_scale)


def _pool_bwd(dymix, d, pool_w, pool_scale, name, tm=512):
    S = dymix.shape[0]
    NG, C, _ = pool_w.shape
    DP = NG * C
    tm = _tile(S, tm)
    hb = tm // POOL_HALO
    nb = S // tm
    n_ext = tm + POOL_HALO
    last_halo = S // POOL_HALO - 1

    def body(dy_ref, halo_ref, d_ref, w_ref, sc_ref, du_ref, dw_ref, dsc_ref):
        i = pl.program_id(0)
        t = lax.broadcasted_iota(jnp.int32, (n_ext, 1), 0) + i * tm
        for g, win in enumerate(POOL_WINDOWS):
            cols = slice(g * C, (g + 1) * C)
            dy = dy_ref[:, cols]
            halo = jnp.where(i < nb - 1, halo_ref[:, cols], 0.0)
            sc = sc_ref[:, cols]
            dv = d_ref[:, cols]
            e_ext = (jnp.concatenate([dy, halo], axis=0) * sc).astype(BF16)
            dd = _nt(e_ext, w_ref[g])
            count = jnp.minimum(t + 1, win).astype(F32)
            acc = dd / count
            step = 1
            while step < win:
                acc = acc + pltpu.roll(acc, n_ext - step, 0)
                step *= 2
            du_ref[:, cols] = (acc[:tm, :] - dd[:tm, :]).astype(BF16)
            dw = _tn(dv, e_ext[:tm, :])
            dsc = jnp.sum(dy * _nn(dv, w_ref[g]), axis=0, keepdims=True)

            @pl.when(i == 0)
            def _():
                dw_ref[g] = dw
                dsc_ref[:, cols] = dsc

            @pl.when(i > 0)
            def _():
                dw_ref[g] += dw
                dsc_ref[:, cols] += dsc

    return pl.pallas_call(
        body, name=name, grid=(nb,),
        in_specs=[pl.BlockSpec((tm, DP), lambda i: (i, 0)),
                  pl.BlockSpec((POOL_HALO, DP), lambda i: (jnp.minimum((i + 1) * hb, last_halo), 0)),
                  pl.BlockSpec((tm, DP), lambda i: (i, 0)),
                  pl.BlockSpec((NG, C, C), lambda i: (0, 0, 0)), pl.BlockSpec((1, DP), lambda i: (0, 0))],
        out_specs=[pl.BlockSpec((tm, DP), lambda i: (i, 0)), pl.BlockSpec((NG, C, C), lambda i: (0, 0, 0)),
                   pl.BlockSpec((1, DP), lambda i: (0, 0))],
        out_shape=[jax.ShapeDtypeStruct((S, DP), BF16), jax.ShapeDtypeStruct((NG, C, C), F32),
                   jax.ShapeDtypeStruct((1, DP), F32)],
        compiler_params=_params())(dymix, dymix, d, pool_w, pool_scale)


def _chunk_scan(v, rows, reverse):
    n = v.shape[0]
    step = 1
    while step < CHUNK:
        if reverse:
            v = v + jnp.where(rows < CHUNK - step, pltpu.roll(v, n - step, 0), 0.0)
        else:
            v = v + jnp.where(rows >= step, pltpu.roll(v, step, 0), 0.0)
        step *= 2
    return v


def _log_decay(alr, w_a2, b_a):
    z = _nn(alr.astype(BF16), w_a2) + b_a
    la = (jnp.minimum(z, 0.0) - jnp.log(1.0 + jnp.exp(-jnp.abs(z)))) * (1.0 / GATE_TEMP)
    return z, la


def _gla_specs(DP, DKT, DV, tb, bmap):
    return [pl.BlockSpec((tb, DKT), lambda i: (bmap(i), DP // DKT)),
            pl.BlockSpec((tb, DKT), lambda i: (bmap(i), DP // DKT + 1)),
            pl.BlockSpec((tb, DV), lambda i: (bmap(i), (DP + 2 * DKT) // DV)),
            pl.BlockSpec((tb, DV), lambda i: (bmap(i), (DP + 2 * DKT) // DV + 1)),
            pl.BlockSpec((tb, LANES), lambda i: (bmap(i), (DP + 2 * DKT + 2 * DV) // LANES))]


def _gla_fwd(proj, y_pool, w_a2, b_a, head_norm, name, tb=512):
    S = proj.shape[0]
    DP = y_pool.shape[1]
    DKT = b_a.shape[1]
    DV = head_norm.shape[1]
    dk, dv = DKT // N_HEADS, DV // N_HEADS
    tb = _tile(S, tb)
    ncb = tb // CHUNK
    qscale = dk ** -0.5

    def body(q_ref, k_ref, v_ref, g_ref, alr_ref, yp_ref, wa_ref, ba_ref, hn_ref, y_ref, st_out_ref, st_ref, kdec_ref,
             gam_ref):
        @pl.when(pl.program_id(0) == 0)
        def _():
            st_ref[...] = jnp.zeros_like(st_ref)

        y_ref[:, :DP] = yp_ref[...]

        rows = lax.broadcasted_iota(jnp.int32, (tb, 1), 0) % CHUNK
        _, la = _log_decay(alr_ref[...], wa_ref[...], ba_ref[...])
        tail = _chunk_scan(la, rows, True)
        kdec_ref[...] = k_ref[...] * jnp.exp(tail - la)
        gam_ref[...] = jnp.exp(tail)

        def chunk(c, carry):
            r0 = pl.multiple_of(c * CHUNK, CHUNK)
            rs = pl.ds(r0, CHUNK)
            gam = gam_ref[pl.ds(r0, 1), :]
            heads = range(N_HEADS)
            kcs = [slice(h * dk, (h + 1) * dk) for h in heads]
            vcs = [slice(h * dv, (h + 1) * dv) for h in heads]
            upd = [_tn(v_ref[rs, vcs[h]].astype(BF16), kdec_ref[rs, kcs[h]].astype(BF16)) for h in heads]
            st = [st_ref[h] * gam[:, kcs[h]] + upd[h] for h in heads]
            o = [_nt((q_ref[rs, kcs[h]] * qscale).astype(BF16), st[h].astype(BF16)) for h in heads]
            for h in heads:
                st_ref[h] = st[h]
                st_out_ref[c, h] = st[h]
                r = lax.rsqrt(jnp.mean(o[h] * o[h], axis=-1, keepdims=True) + RMS_EPS)
                gv = g_ref[rs, vcs[h]]
                y_ref[rs, DP + h * dv:DP + (h + 1) * dv] = (o[h] * r * hn_ref[:, vcs[h]] * (gv * _sigmoid(gv))).astype(BF16)
            return carry

        lax.fori_loop(0, ncb, chunk, 0, unroll=2)

    full = lambda shape: pl.BlockSpec(shape, lambda i: (0,) * len(shape))
    return pl.pallas_call(
        body, name=name, grid=(S // tb,),
        in_specs=_gla_specs(DP, DKT, DV, tb, lambda i: i) + [pl.BlockSpec((tb, DP), lambda i: (i, 0)),
                                                            full((LANES, DKT)), full((1, DKT)), full((1, DV))],
        out_specs=[pl.BlockSpec((tb, DP + DV), lambda i: (i, 0)),
                   pl.BlockSpec((ncb, N_HEADS, dv, dk), lambda i: (i, 0, 0, 0))],
        out_shape=[jax.ShapeDtypeStruct((S, DP + DV), BF16), jax.ShapeDtypeStruct((S // CHUNK, N_HEADS, dv, dk), F32)],
        scratch_shapes=[pltpu.VMEM((N_HEADS, dv, dk), F32), pltpu.VMEM((tb, DKT), F32), pltpu.VMEM((tb, DKT), F32)],
        compiler_params=_params())(proj, proj, proj, proj, proj, y_pool, w_a2, b_a, head_norm)


def _gla_bwd(proj, states, dymix, du, w_a2, b_a, head_norm, name, tb=512):
    S = proj.shape[0]
    DP = du.shape[1]
    DKT = b_a.shape[1]
    DV = head_norm.shape[1]
    dk, dv = DKT // N_HEADS, DV // N_HEADS
    tb = _tile(S, tb)
    ncb = tb // CHUNK
    nb = S // tb
    qscale = dk ** -0.5
    rev = lambda i: nb - 1 - i

    q0, k0, v0, g0, a0 = DP, DP + DKT, DP + 2 * DKT, DP + 2 * DKT + DV, DP + 2 * DKT + 2 * DV

    def body(q_ref, k_ref, v_ref, g_ref, alr_ref, st_blk_ref, st_prev_ref, dy_ref, du_ref, wa_ref, ba_ref, hn_ref,
             dp_ref, dwa_ref, dba_ref, dhn_ref,
             dst_ref, kdec_ref, dec_ref, gam_ref, e_ref, dla_ref, dhn_acc_ref):
        i = pl.program_id(0)
        blk = rev(i)
        dp_ref[:, :DP] = du_ref[...]

        @pl.when(i == 0)
        def _():
            dst_ref[...] = jnp.zeros_like(dst_ref)

        dhn_acc_ref[...] = jnp.zeros_like(dhn_acc_ref)
        rows = lax.broadcasted_iota(jnp.int32, (tb, 1), 0) % CHUNK
        z, la = _log_decay(alr_ref[...], wa_ref[...], ba_ref[...])
        tail = _chunk_scan(la, rows, True)
        dec_ref[...] = jnp.exp(tail - la)
        kdec_ref[...] = k_ref[...] * dec_ref[...]
        gam_ref[...] = jnp.exp(tail)

        def chunk(cc, carry):
            c = ncb - 1 - cc
            r0 = pl.multiple_of(c * CHUNK, CHUNK)
            rs = pl.ds(r0, CHUNK)
            gam = gam_ref[pl.ds(r0, 1), :]
            first = jnp.logical_and(blk == 0, c == 0)
            heads = range(N_HEADS)
            kcs = [slice(h * dk, (h + 1) * dk) for h in heads]
            vcs = [slice(h * dv, (h + 1) * dv) for h in heads]
            qs = [(q_ref[rs, kcs[h]] * qscale).astype(BF16) for h in heads]
            stb = [st_blk_ref[c, h].astype(BF16) for h in heads]
            o = [_nt(qs[h], stb[h]) for h in heads]
            do = []
            for h in heads:
                oh = o[h]
                r = lax.rsqrt(jnp.mean(oh * oh, axis=-1, keepdims=True) + RMS_EPS)
                gv = g_ref[rs, vcs[h]]
                sg = _sigmoid(gv)
                dy = dy_ref[rs, vcs[h]]
                hn = hn_ref[:, vcs[h]]
                on = oh * r
                dp_ref[rs, g0 + h * dv:g0 + (h + 1) * dv] = (dy * on * hn * (sg * (1.0 + gv * (1.0 - sg)))).astype(BF16)
                don = dy * (gv * sg)
                dhn_acc_ref[:, vcs[h]] += jnp.sum(don * on, axis=0, keepdims=True)
                dn = don * hn
                do.append((r * dn - oh * (r * r * r) * jnp.mean(dn * oh, axis=-1, keepdims=True)).astype(BF16))
            dqs = [_nn(do[h], stb[h]) for h in heads]
            dst = [dst_ref[h] + _tn(do[h], qs[h]) for h in heads]
            for h in heads:
                dp_ref[rs, q0 + h * dk:q0 + (h + 1) * dk] = (dqs[h] * qscale).astype(BF16)
            dstb = [dst[h].astype(BF16) for h in heads]
            dvh = [_nt(kdec_ref[rs, kcs[h]].astype(BF16), dstb[h]) for h in heads]
            dkdec = [_nn(v_ref[rs, vcs[h]].astype(BF16), dstb[h]) for h in heads]
            gdg = []
            for h in heads:
                dp_ref[rs, v0 + h * dv:v0 + (h + 1) * dv] = dvh[h].astype(BF16)
                dp_ref[rs, k0 + h * dk:k0 + (h + 1) * dk] = (dkdec[h] * dec_ref[rs, kcs[h]]).astype(BF16)
                e_ref[rs, kcs[h]] = dkdec[h] * kdec_ref[rs, kcs[h]]
                st_prev = jnp.where(c > 0, st_blk_ref[jnp.maximum(c - 1, 0), h], st_prev_ref[0, h])
                st_prev = jnp.where(first, 0.0, st_prev)
                gdg.append(jnp.sum(dst[h] * st_prev, axis=0, keepdims=True) * gam[:, kcs[h]])
                dst_ref[h] = dst[h] * gam[:, kcs[h]]
            dla_ref[rs, :] = jnp.broadcast_to(jnp.concatenate(gdg, axis=1), (CHUNK, DKT))
            return carry

        lax.fori_loop(0, ncb, chunk, 0, unroll=2)

        ev = e_ref[...]
        dla = dla_ref[...] + _chunk_scan(ev, rows, False) - ev
        dz = dla * (1.0 / GATE_TEMP) * (1.0 - _sigmoid(z))
        dzb = dz.astype(BF16)
        dp_ref[:, a0:a0 + LANES] = _nt(dzb, wa_ref[...]).astype(BF16)
        dwa = _tn(alr_ref[...].astype(BF16), dzb)
        dba = jnp.sum(dz, axis=0, keepdims=True)

        @pl.when(i == 0)
        def _():
            dwa_ref[...] = dwa
            dba_ref[...] = dba
            dhn_ref[...] = dhn_acc_ref[...]

        @pl.when(i > 0)
        def _():
            dwa_ref[...] += dwa
            dba_ref[...] += dba
            dhn_ref[...] += dhn_acc_ref[...]

    full = lambda shape: pl.BlockSpec(shape, lambda i: (0,) * len(shape))
    rowblk = lambda w: pl.BlockSpec((tb, w), lambda i: (rev(i), 0))
    return pl.pallas_call(
        body, name=name, grid=(nb,),
        in_specs=_gla_specs(DP, DKT, DV, tb, rev) + [
            pl.BlockSpec((ncb, N_HEADS, dv, dk), lambda i: (rev(i), 0, 0, 0)),
            pl.BlockSpec((1, N_HEADS, dv, dk), lambda i: (jnp.maximum(rev(i) * ncb - 1, 0), 0, 0, 0)),
            pl.BlockSpec((tb, DV), lambda i: (rev(i), DP // DV)), rowblk(DP),
            full((LANES, DKT)), full((1, DKT)), full((1, DV))],
        out_specs=[rowblk(a0 + LANES), full((LANES, DKT)), full((1, DKT)), full((1, DV))],
        out_shape=[jax.ShapeDtypeStruct((S, a0 + LANES), BF16), jax.ShapeDtypeStruct((LANES, DKT), F32),
                   jax.ShapeDtypeStruct((1, DKT), F32), jax.ShapeDtypeStruct((1, DV), F32)],
        scratch_shapes=[pltpu.VMEM((N_HEADS, dv, dk), F32)] + [pltpu.VMEM((tb, DKT), F32)] * 5 + [pltpu.VMEM((1, DV), F32)],
        compiler_params=_params())(proj, proj, proj, proj, proj, states, states, dymix, du, w_a2, b_a, head_norm)


def _xattn_fwd(q, kv, name, tm=512):
    S, D = q.shape
    M = kv.shape[0]
    hd = D // N_HEADS
    tm = _tile(S, tm)
    scale = hd ** -0.5

    def body(q_ref, k_ref, v_ref, o_ref):
        heads = range(N_HEADS)
        hcs = [slice(h * hd, (h + 1) * hd) for h in heads]
        s = [_nt(q_ref[:, hc], k_ref[:, hc]) * scale for hc in hcs]
        p = []
        for h in heads:
            e = jnp.exp(s[h] - jnp.max(s[h], axis=-1, keepdims=True))
            p.append((e / jnp.sum(e, axis=-1, keepdims=True)).astype(BF16))
        o = [_nn(p[h], v_ref[:, hcs[h]]) for h in heads]
        for h in heads:
            o_ref[:, hcs[h]] = o[h].astype(BF16)

    return pl.pallas_call(body, name=name, grid=(S // tm,),
                          in_specs=[pl.BlockSpec((tm, D), lambda i: (i, 0)), pl.BlockSpec((M, D), lambda i: (0, 0)),
                                    pl.BlockSpec((M, D), lambda i: (0, 1))],
                          out_specs=pl.BlockSpec((tm, D), lambda i: (i, 0)),
                          out_shape=jax.ShapeDtypeStruct((S, D), BF16), compiler_params=_params())(q, kv, kv)


def _xattn_bwd(q, kv, do, name, tm=512):
    S, D = q.shape
    M = kv.shape[0]
    hd = D // N_HEADS
    tm = _tile(S, tm)
    scale = hd ** -0.5

    def body(q_ref, k_ref, v_ref, do_ref, dq_ref, dkv_ref):
        first = pl.program_id(0) == 0
        heads = range(N_HEADS)
        hcs = [slice(h * hd, (h + 1) * hd) for h in heads]
        s = [_nt(q_ref[:, hc], k_ref[:, hc]) * scale for hc in hcs]
        dp = [_nt(do_ref[:, hc], v_ref[:, hc]) for hc in hcs]
        p = []
        for h in heads:
            e = jnp.exp(s[h] - jnp.max(s[h], axis=-1, keepdims=True))
            p.append(e / jnp.sum(e, axis=-1, keepdims=True))
        dvh = [_tn(p[h].astype(BF16), do_ref[:, hcs[h]]) for h in heads]
        ds = [((p[h] * (dp[h] - jnp.sum(dp[h] * p[h], axis=-1, keepdims=True))) * scale).astype(BF16) for h in heads]
        dqh = [_nn(ds[h], k_ref[:, hcs[h]]) for h in heads]
        dkh = [_tn(ds[h], q_ref[:, hcs[h]]) for h in heads]
        for h in heads:
            dq_ref[:, hcs[h]] = dqh[h].astype(BF16)

        @pl.when(first)
        def _():
            for h in heads:
                dkv_ref[:, hcs[h]] = dkh[h]
                dkv_ref[:, D + h * hd:D + (h + 1) * hd] = dvh[h]

        @pl.when(jnp.logical_not(first))
        def _():
            for h in heads:
                dkv_ref[:, hcs[h]] += dkh[h]
                dkv_ref[:, D + h * hd:D + (h + 1) * hd] += dvh[h]

    row = pl.BlockSpec((tm, D), lambda i: (i, 0))
    return pl.pallas_call(body, name=name, grid=(S // tm,),
                          in_specs=[row, pl.BlockSpec((M, D), lambda i: (0, 0)), pl.BlockSpec((M, D), lambda i: (0, 1)), row],
                          out_specs=[row, pl.BlockSpec((M, 2 * D), lambda i: (0, 0))],
                          out_shape=[jax.ShapeDtypeStruct((S, D), BF16), jax.ShapeDtypeStruct((M, 2 * D), F32)],
                          compiler_params=_params())(q, kv, kv, do)


def _local_step(x, mem, target, vec, weight, emit, dep0):
    DP = vec["pool_scale"].shape[1]
    g = {}
    pending = []
    begun = []
    summed = []
    emit_begin, emit_finish, emit_send, early_update = emit

    def behind(fn, *a, **kw):
        dep = tuple(pending)
        pending.clear()
        out = fn(*a, dep=dep, **kw)
        while summed:
            pending.append(emit_send(summed.pop(0)))
        while begun:
            name = begun.pop(0)
            token = emit_finish(name, out)
            if token is None:
                summed.append(name)
            else:
                pending.append(token)
        return out

    def mm(a, b, **kw):
        return behind(_matmul, a, b, **kw)

    def send(name, gfull):
        pending.append(emit_begin(name, gfull))
        begun.append(name)

    def ffn_fwd(xin, tag, dep):
        h = _rms_fwd(xin, vec[f"{tag}_norm"], f"{tag}_norm", dep=dep)
        ga, gb, hid = _ffn_up(h, weight(f"{tag}_w_gate", h), weight(f"{tag}_w_up", h), f"{tag}_up")
        wd = weight(f"{tag}_w_down", hid)
        G, Fj, D = wd.shape
        xo = _matmul(hid, wd.reshape(G * Fj, D), mode="nn", name=f"{tag}_down", out_dtype=F32, res=xin, scale=0.5,
                     tn=1024, tk=G * Fj)
        return xo, (h, ga, gb, hid)

    def ffn_bwd(dxh, saved, tag, last):
        h, ga, gb, hid = saved
        wg, wu, wd = weight(f"{tag}_w_gate"), weight(f"{tag}_w_up"), weight(f"{tag}_w_down")
        G, Fj, D = wd.shape
        send(f"{tag}_w_down", mm(hid, dxh, mode="tn", name=f"{tag}_dwd", out_dtype=F32, tm=Fj, tn=1024))
        da, db = _ffn_dact(dxh, wd, ga, gb, f"{tag}_dact")
        send(f"{tag}_w_gate", mm(h, da, mode="tn", name=f"{tag}_dwg", out_dtype=F32, tm=1024, tn=Fj, tk=TOKENS_PER_STEP,
                                  out_groups=G))
        dwu = mm(h, db, mode="tn", name=f"{tag}_dwu", out_dtype=F32, tm=1024, tn=Fj, tk=TOKENS_PER_STEP, out_groups=G)
        if last:
            pending.append(emit_begin(f"{tag}_w_up", dwu))
            pending.append(emit_finish(f"{tag}_w_up", pending[-1]))
        else:
            send(f"{tag}_w_up", dwu)
        return behind(_ffn_dh, da, db, wg, wu, f"{tag}_dh")

    x1, ffn1_saved = ffn_fwd(x, "ffn1", dep0)
    h2 = _rms_fwd(x1, vec["mix_norm"], "mix_norm")
    w_in = weight("w_in", h2)
    proj = _matmul(h2, w_in, mode="nn", name="w_in", out_dtype=F32, tn=1408)
    pool_w, w_a2 = weight("pool_w", h2), weight("gla_w_a2", h2)
    y_pool, dpool = _pool_fwd(proj, pool_w, vec["pool_scale"], "pool_fwd")
    ymix, states = _gla_fwd(proj, y_pool, w_a2, vec["gla_b_a"], vec["gla_head_norm"], "gla_fwd")
    w_out = weight("w_out", ymix)
    x2 = _matmul(ymix, w_out, mode="nn", name="w_out", out_dtype=F32, res=x1)
    h3 = _rms_fwd(x2, vec["xattn_norm"], "xattn_norm")
    mh = _rms_fwd(mem, vec["mem_norm"], "mem_norm")
    w_q = weight("xattn_w_q", h3)
    q = _matmul(h3, w_q, mode="nn", name="xattn_q", out_dtype=BF16)
    w_kv = weight("xattn_w_kv", q)
    kv = _matmul(mh, w_kv, mode="nn", name="xattn_kv", out_dtype=BF16, b_groups=True, tn=1024)
    o = _xattn_fwd(q, kv, "xattn_fwd")
    w_o = weight("xattn_w_o", o)
    x3 = _matmul(o, w_o, mode="nn", name="xattn_o", out_dtype=F32, res=x2)
    x4, ffn2_saved = ffn_fwd(x3, "ffn2", None)
    sq, dx4, dx4h, g["final_norm"] = _loss_head(x4, vec["final_norm"], target, "loss_head")
    loss = lax.psum(0.5 * jnp.sum(sq) / x.shape[-1], ("x", "y", "c"))
    pending.append(loss.reshape(1, 1))

    dh = ffn_bwd(dx4h, ffn2_saved, "ffn2", False)
    dx3, dx3b, g["ffn2_norm"] = _rms_bwd(x3, vec["ffn2_norm"], dh, dx4, "ffn2_norm_bwd", lowp=1.0)
    send("xattn_w_o", mm(o, dx3b, mode="tn", name="xattn_dwo", out_dtype=F32, tm=1024, tn=1024, tk=TOKENS_PER_STEP))
    do = mm(dx3b, w_o, mode="nt", name="xattn_do", out_dtype=BF16)
    dq, dkv = _xattn_bwd(q, kv, do, "xattn_bwd")
    send("xattn_w_q", mm(h3, dq, mode="tn", name="xattn_dwq", out_dtype=F32, tm=1024, tn=1024, tk=TOKENS_PER_STEP))
    dh3 = mm(dq, w_q, mode="nt", name="xattn_dh", out_dtype=F32)
    dkvb = _cast(dkv, BF16, "dkv_cast")
    send("xattn_w_kv", mm(mh, dkvb, mode="tn", name="xattn_dwkv", out_dtype=F32, tm=1024, tn=1024, out_groups=N_SHARDS))
    dmh = mm(dkvb, w_kv, mode="nt", name="xattn_dmh", out_dtype=F32, b_groups=True, tk=1024)
    _, g["mem_norm"] = _rms_bwd(mem, vec["mem_norm"], dmh, None, "mem_norm_bwd")
    pending.append(g["mem_norm"])
    dx2, dx2b, g["xattn_norm"] = _rms_bwd(x2, vec["xattn_norm"], dh3, dx3, "xattn_norm_bwd", lowp=1.0)
    send("w_out", mm(ymix, dx2b, mode="tn", name="dw_out", out_dtype=F32, tm=1024, tn=1024, tk=TOKENS_PER_STEP))
    dymix = mm(dx2b, w_out, mode="nt", name="dymix", out_dtype=F32)
    du, dpool_w, g["pool_scale"] = _pool_bwd(dymix, dpool, pool_w, vec["pool_scale"], "pool_bwd")
    send("pool_w", dpool_w)
    dproj, dw_a2, g["gla_b_a"], g["gla_head_norm"] = _gla_bwd(
        proj, states, dymix, du, w_a2, vec["gla_b_a"], vec["gla_head_norm"], "gla_bwd")
    send("gla_w_a2", dw_a2)
    send("w_in", mm(h2, dproj, mode="tn", name="dw_in", out_dtype=F32, tm=1024, tn=1408, tk=TOKENS_PER_STEP))
    dh2 = mm(dproj, w_in, mode="nt", name="dh2", out_dtype=F32, tn=1024, tk=dproj.shape[1])
    pending.extend(early_update(dh2))
    dx1, dx1h, g["mix_norm"] = _rms_bwd(x1, vec["mix_norm"], dh2, dx2, "mix_norm_bwd", lowp=0.5)
    dh = ffn_bwd(dx1h, ffn1_saved, "ffn1", True)
    dx0, g["ffn1_norm"] = _rms_bwd(x, vec["ffn1_norm"], dh, dx1, "ffn1_norm_bwd")
    return loss, dx0, g


def _place():
    x, y, c = lax.axis_index("x"), lax.axis_index("y"), lax.axis_index("c")
    chips = [(1 - x, y), (x, 1 - y), (1 - x, 1 - y)]
    return x, y, c, chips


def _ids():
    return jnp.stack([2 * lax.axis_index("x") + lax.axis_index("y"), lax.axis_index("c")]).astype(jnp.int32)


def _hbm(a):
    return pltpu.with_memory_space_constraint(a, pltpu.HBM)


def _cast_to_slot(w2d, dtype, name, dep=None):
    R, C = w2d.shape
    tr = _tile(R, max(16, (4 << 20) // (4 * C) // 16 * 16))

    def body(i_ref, w_ref, *rest):
        rest[-1][...] = w_ref[...].astype(dtype)

    in_specs = [pl.BlockSpec((tr, C), lambda r, i: (r, 0))]
    operands = [w2d]
    if dep is not None:
        in_specs.append(pl.BlockSpec(dep.shape, lambda r, i: (0, 0)))
        operands.append(dep)
    grid_spec = pltpu.PrefetchScalarGridSpec(num_scalar_prefetch=1, grid=(R // tr,), in_specs=in_specs,
                                             out_specs=pl.BlockSpec((None, tr, C), lambda r, i: (i[0], r, 0)))
    return pl.pallas_call(body, name=name, grid_spec=grid_spec, out_shape=jax.ShapeDtypeStruct((N_SHARDS, R, C), dtype),
                          compiler_params=_params())(_ids(), *operands)


def _gather_copies(buf_ref, send_sems, recv_sems, incoming, whole):
    x, y, c, chips = _place()
    hr = buf_ref.shape[1] // 2
    copies = []
    for j, (px, py) in enumerate(chips):
        slot = 2 * px + py if incoming else 2 * x + y
        part = buf_ref.at[slot] if whole else buf_ref.at[slot, pl.ds(c * hr, hr), :]
        copies.append(pltpu.make_async_remote_copy(src_ref=part, dst_ref=part, send_sem=send_sems.at[j],
                                                   recv_sem=recv_sems.at[j], device_id=(px, py, c), device_id_type=MESH))
    return copies


def _gather_start(buf, name, whole):
    def body(b_ref, send_sems, recv_sems, b_thru, token):
        for cp in _gather_copies(b_ref, send_sems, recv_sems, False, whole):
            cp.start()
        token[...] = jnp.zeros_like(token)

    return pl.pallas_call(
        body, name=name,
        out_shape=(pltpu.SemaphoreType.DMA((3,)), pltpu.SemaphoreType.DMA((3,)), pltpu.HBM(buf.shape, buf.dtype),
                   jax.ShapeDtypeStruct((8, LANES), F32)),
        in_specs=(HBM,), out_specs=(SEM, SEM, HBM, pl.BlockSpec(memory_space=pltpu.VMEM)), input_output_aliases={0: 2},
        compiler_params=pltpu.CompilerParams(has_side_effects=EFFECT))(_hbm(buf))


def _gather_wait(send_sems, recv_sems, buf_thru, after, name, whole):
    def body(b_ref, send_sems, recv_sems, after_ref, b_out):
        for cp in _gather_copies(b_ref, send_sems, recv_sems, False, whole):
            cp.wait_send()
        for cp in _gather_copies(b_ref, send_sems, recv_sems, True, whole):
            cp.wait_recv()

    return pl.pallas_call(
        body, name=name, out_shape=pltpu.HBM(buf_thru.shape, buf_thru.dtype),
        in_specs=(HBM, SEM, SEM, ANY), out_specs=HBM, input_output_aliases={0: 0},
        compiler_params=pltpu.CompilerParams(has_side_effects=EFFECT))(buf_thru, send_sems, recv_sems, after)


def _gather_forward(buf, name):
    G, R, C = buf.shape
    hr = R // 2

    def body(b_ref, o_ref, send_sems, recv_sems):
        x, y, c, chips = _place()
        copies = []
        for j, (px, py) in enumerate(chips):
            half = o_ref.at[2 * px + py, pl.ds(c * hr, hr), :]
            copies.append(pltpu.make_async_remote_copy(src_ref=half, dst_ref=half, send_sem=send_sems.at[j],
                                                       recv_sem=recv_sems.at[j], device_id=(x, y, 1 - c),
                                                       device_id_type=MESH))
        for cp in copies:
            cp.start()
        for j, (px, py) in enumerate(chips):
            half = o_ref.at[2 * px + py, pl.ds((1 - c) * hr, hr), :]
            pltpu.make_async_remote_copy(src_ref=half, dst_ref=half, send_sem=send_sems.at[j], recv_sem=recv_sems.at[j],
                                         device_id=(x, y, 1 - c), device_id_type=MESH).wait_recv()
        for cp in copies:
            cp.wait_send()

    return pl.pallas_call(body, name=name, in_specs=[ANY], out_specs=ANY, out_shape=jax.ShapeDtypeStruct(buf.shape, buf.dtype),
                          input_output_aliases={0: 0},
                          scratch_shapes=[pltpu.SemaphoreType.DMA((3,)), pltpu.SemaphoreType.DMA((3,))])(buf)


def _pair_copy(g_ref, land_ref, send_sem, recv_sem):
    x, y, c, _ = _place()
    hr = g_ref.shape[1] // 2
    return pltpu.make_async_remote_copy(src_ref=g_ref.at[:, pl.ds((1 - c) * hr, hr), :], dst_ref=land_ref,
                                        send_sem=send_sem, recv_sem=recv_sem, device_id=(x, y, 1 - c), device_id_type=MESH)


def _pair_start(gfull, name):
    G, R, C = gfull.shape

    def body(g_ref, land_ref, send_sem, recv_sem, g_thru, land_thru, token):
        _pair_copy(g_ref, land_ref, send_sem, recv_sem).start()
        token[...] = jnp.zeros_like(token)

    return pl.pallas_call(
        body, name=name,
        out_shape=(pltpu.SemaphoreType.DMA(()), pltpu.SemaphoreType.DMA(()), pltpu.HBM(gfull.shape, F32),
                   pltpu.HBM((G, R // 2, C), F32), jax.ShapeDtypeStruct((8, LANES), F32)),
        in_specs=(HBM, HBM), out_specs=(SEM, SEM, HBM, HBM, pl.BlockSpec(memory_space=pltpu.VMEM)),
        input_output_aliases={0: 2, 1: 3},
        compiler_params=pltpu.CompilerParams(has_side_effects=EFFECT))(_hbm(gfull), _hbm(lax.empty((G, R // 2, C), F32)))


def _pair_wait(send_sem, recv_sem, g_thru, land_thru, after, name):
    def body(g_ref, land_ref, send_sem, recv_sem, after_ref, g_out, land_out):
        cp = _pair_copy(g_ref, land_ref, send_sem, recv_sem)
        cp.wait_send()
        cp.wait_recv()

    return pl.pallas_call(
        body, name=name, out_shape=(pltpu.HBM(g_thru.shape, F32), pltpu.HBM(land_thru.shape, F32)),
        in_specs=(HBM, HBM, SEM, SEM, ANY), out_specs=(HBM, HBM), input_output_aliases={0: 0, 1: 1},
        compiler_params=pltpu.CompilerParams(has_side_effects=EFFECT))(g_thru, land_thru, send_sem, recv_sem, after)


def _pair_add(gfull, other, name):
    G, R, C = gfull.shape
    hr = R // 2
    tr = _tile(hr, max(8, (2 * 1024 * 1024) // (4 * C) // 8 * 8))
    nr = hr // tr
    c = lax.axis_index("c")
    cidx = jnp.reshape(c, (1,)).astype(jnp.int32)

    def body(c_ref, a_ref, b_ref, o_ref):
        o_ref[...] = a_ref[...] + b_ref[...]

    grid_spec = pltpu.PrefetchScalarGridSpec(
        num_scalar_prefetch=1, grid=(G, nr),
        in_specs=[pl.BlockSpec((None, tr, C), lambda g, r, cr: (g, cr[0] * nr + r, 0)),
                  pl.BlockSpec((None, tr, C), lambda g, r, cr: (g, r, 0))],
        out_specs=pl.BlockSpec((None, tr, C), lambda g, r, cr: (g, r, 0)))
    return pl.pallas_call(body, name=name, grid_spec=grid_spec, out_shape=jax.ShapeDtypeStruct((G, hr, C), F32),
                          compiler_params=_params())(cidx, gfull, other)


def _chip_copies(p_ref, land_ref, send_sems, recv_sems, incoming):
    x, y, c, chips = _place()
    me = 2 * x + y
    copies = []
    for j, (px, py) in enumerate(chips):
        dst = land_ref.at[2 * px + py] if incoming else land_ref.at[me]
        copies.append(pltpu.make_async_remote_copy(src_ref=p_ref.at[2 * px + py], dst_ref=dst, send_sem=send_sems.at[j],
                                                   recv_sem=recv_sems.at[j], device_id=(px, py, c), device_id_type=MESH))
    return copies


def _chip_start(part, name):
    def body(p_ref, land_ref, send_sems, recv_sems, p_thru, land_thru, token):
        for cp in _chip_copies(p_ref, land_ref, send_sems, recv_sems, False):
            cp.start()
        token[...] = jnp.zeros_like(token)

    return pl.pallas_call(
        body, name=name,
        out_shape=(pltpu.SemaphoreType.DMA((3,)), pltpu.SemaphoreType.DMA((3,)), pltpu.HBM(part.shape, F32),
                   pltpu.HBM(part.shape, F32), jax.ShapeDtypeStruct((8, LANES), F32)),
        in_specs=(HBM, HBM), out_specs=(SEM, SEM, HBM, HBM, pl.BlockSpec(memory_space=pltpu.VMEM)),
        input_output_aliases={0: 2, 1: 3},
        compiler_params=pltpu.CompilerParams(has_side_effects=EFFECT))(_hbm(part), _hbm(lax.empty(part.shape, F32)))


def _chip_wait(send_sems, recv_sems, p_thru, land_thru, after, name):
    def body(p_ref, land_ref, send_sems, recv_sems, after_ref, p_out, land_out):
        for cp in _chip_copies(p_ref, land_ref, send_sems, recv_sems, False):
            cp.wait_send()
        for cp in _chip_copies(p_ref, land_ref, send_sems, recv_sems, True):
            cp.wait_recv()

    return pl.pallas_call(
        body, name=name, out_shape=(pltpu.HBM(p_thru.shape, F32), pltpu.HBM(p_thru.shape, F32)),
        in_specs=(HBM, HBM, SEM, SEM, ANY), out_specs=(HBM, HBM), input_output_aliases={0: 0, 1: 1},
        compiler_params=pltpu.CompilerParams(has_side_effects=EFFECT))(p_thru, land_thru, send_sems, recv_sems, after)


def _chip_sum(part, slots, name):
    G, R2, C = part.shape
    tr = _tile(R2, max(8, (1 << 20) // (4 * C) // 8 * 8))
    nr = R2 // tr

    def body(i_ref, p_ref, *rest):
        o_ref = rest[-1]
        acc = None
        for u in range(G):
            val = jnp.where(i_ref[0] == u, p_ref[...], rest[u][...])
            acc = val if acc is None else acc + val
        o_ref[...] = acc

    def slot_spec(u):
        return pl.BlockSpec((None, tr, C), lambda r, i: (jnp.where(i[0] == u, (u + 1) % G, u), r, 0))

    grid_spec = pltpu.PrefetchScalarGridSpec(
        num_scalar_prefetch=1, grid=(nr,),
        in_specs=[pl.BlockSpec((None, tr, C), lambda r, i: (i[0], r, 0))] + [slot_spec(u) for u in range(G)],
        out_specs=pl.BlockSpec((tr, C), lambda r, i: (i[1] * nr + r, 0)))
    return pl.pallas_call(body, name=name, grid_spec=grid_spec, out_shape=jax.ShapeDtypeStruct((2 * R2, C), F32),
                          compiler_params=_params())(_ids(), part, slots, slots, slots, slots)


def _sum_slots(slots, name):
    G, R2, C = slots.shape
    tr = _tile(R2, max(8, (1024 * 1024) // (4 * C) // 8 * 8))

    def body(s_ref, o_ref):
        acc = s_ref[0]
        for u in range(1, G):
            acc = acc + s_ref[u]
        o_ref[...] = acc

    return pl.pallas_call(body, name=name, grid=(R2 // tr,), in_specs=[pl.BlockSpec((G, tr, C), lambda r: (0, r, 0))],
                          out_specs=pl.BlockSpec((tr, C), lambda r: (r, 0)), out_shape=jax.ShapeDtypeStruct((R2, C), F32),
                          compiler_params=_params())(slots)


def _pair_join(full, name):
    R, C = full.shape
    R2 = R // 2

    def body(f_ref, o_ref, token, send_sem, recv_sem):
        x, y, c, _ = _place()
        token[...] = jnp.zeros_like(token)
        mine = o_ref.at[pl.ds(c * R2, R2), :]
        theirs = o_ref.at[pl.ds((1 - c) * R2, R2), :]
        cp = pltpu.make_async_remote_copy(src_ref=mine, dst_ref=mine, send_sem=send_sem, recv_sem=recv_sem,
                                          device_id=(x, y, 1 - c), device_id_type=MESH)
        cp.start()
        pltpu.make_async_remote_copy(src_ref=theirs, dst_ref=theirs, send_sem=send_sem, recv_sem=recv_sem,
                                     device_id=(x, y, 1 - c), device_id_type=MESH).wait_recv()
        cp.wait_send()

    return pl.pallas_call(body, name=name, in_specs=[ANY], out_specs=[ANY, pl.BlockSpec(memory_space=pltpu.VMEM)],
                          out_shape=[jax.ShapeDtypeStruct((R, C), F32), jax.ShapeDtypeStruct((8, LANES), F32)],
                          input_output_aliases={0: 0},
                          scratch_shapes=[pltpu.SemaphoreType.DMA, pltpu.SemaphoreType.DMA])(full)


def _all_reduce_small(v, name):
    R, C = v.shape

    def gather_body(v_ref, out_ref, send_sems, recv_sems, local_sem):
        x, y, c, _ = _place()
        me = 4 * x + 2 * y + c
        mine = pltpu.make_async_copy(v_ref, out_ref.at[me], local_sem)
        mine.start()
        flips = [(fx, fy, fc) for fx in (0, 1) for fy in (0, 1) for fc in (0, 1)][1:]
        copies = []
        for j, (fx, fy, fc) in enumerate(flips):
            peer = (x ^ fx, y ^ fy, c ^ fc)
            copies.append(pltpu.make_async_remote_copy(src_ref=v_ref, dst_ref=out_ref.at[me], send_sem=send_sems.at[j],
                                                       recv_sem=recv_sems.at[j], device_id=peer, device_id_type=MESH))
        for cp in copies:
            cp.start()
        for j, (fx, fy, fc) in enumerate(flips):
            peer = (x ^ fx, y ^ fy, c ^ fc)
            pltpu.make_async_remote_copy(src_ref=v_ref, dst_ref=out_ref.at[4 * peer[0] + 2 * peer[1] + peer[2]],
                                         send_sem=send_sems.at[j], recv_sem=recv_sems.at[j], device_id=peer,
                                         device_id_type=MESH).wait_recv()
        for cp in copies:
            cp.wait_send()
        mine.wait()

    slots = pl.pallas_call(gather_body, name=name, in_specs=[ANY], out_specs=ANY,
                           out_shape=jax.ShapeDtypeStruct((8, R, C), F32),
                           scratch_shapes=[pltpu.SemaphoreType.DMA((7,)), pltpu.SemaphoreType.DMA((7,)),
                                           pltpu.SemaphoreType.DMA])(v)
    return _sum_slots(slots, f"{name}_sum")


def _adamw(w, g, m, v, name, dep=()):
    R, C = w.shape
    tr = _tile(R, max(8, (2 << 20) // (4 * C) // 8 * 8))
    bc1 = 1.0 - ADAM_B1 ** ADAM_STEP
    bc2 = 1.0 - ADAM_B2 ** ADAM_STEP

    def body(w_ref, g_ref, m_ref, v_ref, *rest):
        go_ref, d_ref, nm_ref, nv_ref = rest[len(dep):]
        gv = g_ref[...]
        go_ref[...] = gv
        nm = ADAM_B1 * m_ref[...] + (1.0 - ADAM_B1) * gv
        nv = ADAM_B2 * v_ref[...] + (1.0 - ADAM_B2) * (gv * gv)
        nm_ref[...] = nm
        nv_ref[...] = nv
        d_ref[...] = -ADAM_LR * ((nm / bc1) / (jnp.sqrt(nv / bc2) + ADAM_EPS) + ADAM_WD * w_ref[...])

    blk = pl.BlockSpec((tr, C), lambda r: (r, 0))
    out = jax.ShapeDtypeStruct((R, C), F32)
    in_specs = [blk] * 4 + [pl.BlockSpec(d.shape, lambda r: (0, 0)) for d in dep]
    return pl.pallas_call(body, name=name, grid=(R // tr,), in_specs=in_specs, out_specs=[blk] * 4, out_shape=[out] * 4,
                          compiler_params=_params())(w, g, m, v, *dep)


SC_TILES = 32
SC_LANES = 16
SC_ROWS = 8


def _sc_mesh():
    return plsc.VectorSubcoreMesh(core_axis_name="sc_core", subcore_axis_name="sc_subcore")


def _pair_add_sc(gfull, other, name):
    G, R, C = gfull.shape
    hr = R // 2
    tiles_per_shard = SC_TILES // G
    per_tile = hr // SC_ROWS // tiles_per_shard

    def body(g_hbm, o_hbm, out_hbm, gb, ob):
        c = lax.axis_index("c")
        tile = lax.axis_index("sc_subcore") * 2 + lax.axis_index("sc_core")
        t = tile // tiles_per_shard
        first = (tile % tiles_per_shard) * per_tile

        @pl.loop(0, per_tile)
        def _(k):
            rr = (first + k) * SC_ROWS
            pltpu.sync_copy(g_hbm.at[t, pl.ds(c * hr + rr, SC_ROWS), :], gb)
            pltpu.sync_copy(o_hbm.at[t, pl.ds(rr, SC_ROWS), :], ob)

            @pl.loop(0, SC_ROWS)
            def _(i):
                @pl.loop(0, C, step=SC_LANES)
                def _(j):
                    at = (i, pl.ds(j, SC_LANES))
                    gb[at] = gb[at] + ob[at]

            pltpu.sync_copy(gb, out_hbm.at[t, pl.ds(rr, SC_ROWS), :])

    buf = pltpu.VMEM((SC_ROWS, C), F32)
    return pl.kernel(body, name=name, out_type=jax.ShapeDtypeStruct((G, hr, C), F32), mesh=_sc_mesh(),
                     scratch_types=[buf, buf])(gfull, other)


def _adamw_sc(w, g, m, v, name):
    R, C = w.shape
    tasks = R // SC_ROWS
    bc1 = 1.0 - ADAM_B1 ** ADAM_STEP
    bc2 = 1.0 - ADAM_B2 ** ADAM_STEP

    def body(w_hbm, g_hbm, m_hbm, v_hbm, go_hbm, d_hbm, nm_hbm, nv_hbm, wb, gb, mb, vb):
        tile = lax.axis_index("sc_subcore") * 2 + lax.axis_index("sc_core")

        @pl.loop((tile * tasks) // SC_TILES, ((tile + 1) * tasks) // SC_TILES)
        def _(task):
            rows = pl.ds(task * SC_ROWS, SC_ROWS)
            pltpu.sync_copy(w_hbm.at[rows, :], wb)
            pltpu.sync_copy(g_hbm.at[rows, :], gb)
            pltpu.sync_copy(m_hbm.at[rows, :], mb)
            pltpu.sync_copy(v_hbm.at[rows, :], vb)

            @pl.loop(0, SC_ROWS)
            def _(i):
                @pl.loop(0, C, step=SC_LANES)
                def _(j):
                    at = (i, pl.ds(j, SC_LANES))
                    gv = gb[at]
                    nm = ADAM_B1 * mb[at] + (1.0 - ADAM_B1) * gv
                    nv = ADAM_B2 * vb[at] + (1.0 - ADAM_B2) * (gv * gv)
                    mb[at] = nm
                    vb[at] = nv
                    wb[at] = -ADAM_LR * ((nm / bc1) / (jnp.sqrt(nv / bc2) + ADAM_EPS) + ADAM_WD * wb[at])

            pltpu.sync_copy(gb, go_hbm.at[rows, :])
            pltpu.sync_copy(wb, d_hbm.at[rows, :])
            pltpu.sync_copy(mb, nm_hbm.at[rows, :])
            pltpu.sync_copy(vb, nv_hbm.at[rows, :])

    out = jax.ShapeDtypeStruct((R, C), F32)
    buf = pltpu.VMEM((SC_ROWS, C), F32)
    return pl.kernel(body, name=name, out_type=(out, out, out, out),
                     mesh=plsc.VectorSubcoreMesh(core_axis_name="sc_core", subcore_axis_name="sc_subcore"),
                     scratch_types=[buf, buf, buf, buf],
                     cost_estimate=pl.CostEstimate(flops=16 * R * C, transcendentals=2 * R * C, bytes_accessed=32 * R * C),
                     )(w, g, m, v)


WEIGHTS = ['ffn1_norm', 'ffn1_w_gate', 'ffn1_w_up', 'ffn1_w_down', 'mix_norm', 'w_in', 'pool_w', 'pool_scale', 'gla_w_a2',
           'gla_b_a', 'gla_head_norm', 'w_out', 'xattn_norm', 'mem_norm', 'xattn_w_q', 'xattn_w_kv', 'xattn_w_o', 'ffn2_norm',
           'ffn2_w_gate', 'ffn2_w_up', 'ffn2_w_down', 'final_norm']
SHARDED = ['ffn1_w_gate', 'ffn1_w_up', 'ffn1_w_down', 'w_in', 'pool_w', 'gla_w_a2', 'w_out', 'xattn_w_q', 'xattn_w_kv',
           'xattn_w_o', 'ffn2_w_gate', 'ffn2_w_up', 'ffn2_w_down']
REPLICATED = [n for n in WEIGHTS if n not in SHARDED]
ON_SPARSECORE = ['ffn2_w_down', 'ffn2_w_gate', 'ffn2_w_up', 'w_out', 'xattn_w_q', 'xattn_w_kv', 'xattn_w_o']
PAIR_SUM_ON_SPARSECORE = ['ffn2_w_down', 'ffn2_w_gate', 'ffn2_w_up', 'xattn_w_o', 'xattn_w_kv', 'pool_w', 'ffn1_w_down']
SMALL_COLS = 512


def _as2d(a):
    return a.reshape(-1, a.shape[-1])


def _finish_weight(name, gathered, wl):
    G, R, C = gathered.shape
    rank = wl["gla_w_a2"].shape[1]
    if name in ("w_out", "xattn_w_q", "xattn_w_o"):
        return gathered.reshape(G * R, C)
    if name == "w_in":
        w_in = jnp.transpose(gathered, (1, 0, 2)).reshape(R, G * C)
        main = G * C - rank
        return jnp.concatenate([w_in[:, :main], jnp.pad(w_in[:, main:], ((0, 0), (0, LANES - rank)))], axis=1)
    if name == "pool_w":
        NG, CJ, _ = wl[name].shape[1:]
        return jnp.transpose(gathered.reshape(G, NG, CJ, C), (1, 0, 2, 3)).reshape(NG, G * CJ, C)
    if name == "gla_w_a2":
        a2 = jnp.transpose(gathered, (1, 0, 2)).reshape(rank, G * C)
        return jnp.pad(a2, ((0, LANES - rank), (0, 0))).astype(BF16)
    return gathered


def _start_gathers(wl):
    started = {}
    token = None
    for n in SHARDED:
        whole = n not in ("ffn1_w_gate", "ffn1_w_up")
        buf = _cast_to_slot(_as2d(wl[n]), BF16, f"slot_{n}", dep=token)
        send_sems, recv_sems, thru, token = _gather_start(buf, f"gather_start_{n}", whole)
        started[n] = (send_sems, recv_sems, thru, whole)
    cache = {}

    def weight(n, after=None):
        if n not in cache:
            *handles, whole = started[n]
            buf = _gather_wait(*handles, after, f"gather_wait_{n}", whole)
            if not whole:
                buf = _gather_forward(buf, f"gather_forward_{n}")
            cache[n] = _finish_weight(n, buf, wl)
        return cache[n]

    return weight, token


def _shard_major(name, gfull, wl):
    R, C = _as2d(wl[name]).shape
    if name in ("ffn1_w_gate", "ffn1_w_up", "ffn2_w_gate", "ffn2_w_up", "xattn_w_kv"):
        return gfull
    if name in ("ffn1_w_down", "ffn2_w_down", "w_out", "xattn_w_q", "xattn_w_o"):
        return gfull.reshape(N_SHARDS, R, C)
    if name == "w_in":
        return jnp.transpose(gfull[:, :N_SHARDS * C].reshape(R, N_SHARDS, C), (1, 0, 2))
    if name == "pool_w":
        NG, CJ, _ = wl[name].shape[1:]
        return jnp.transpose(gfull.reshape(NG, N_SHARDS, CJ, C), (1, 0, 2, 3)).reshape(N_SHARDS, R, C)
    assert name == "gla_w_a2"
    return jnp.transpose(gfull[:R].reshape(R, N_SHARDS, C), (1, 0, 2))


def kernel(x, mem, ffn1_norm, ffn1_w_gate, ffn1_w_up, ffn1_w_down, mix_norm, w_in, pool_w, pool_scale, gla_w_a2, gla_b_a, gla_head_norm, w_out, xattn_norm, mem_norm, xattn_w_q, xattn_w_kv, xattn_w_o, ffn2_norm, ffn2_w_gate, ffn2_w_up, ffn2_w_down, final_norm, loss_target, m_ffn1_norm, m_ffn1_w_gate, m_ffn1_w_up, m_ffn1_w_down, m_mix_norm, m_w_in, m_pool_w, m_pool_scale, m_gla_w_a2, m_gla_b_a, m_gla_head_norm, m_w_out, m_xattn_norm, m_mem_norm, m_xattn_w_q, m_xattn_w_kv, m_xattn_w_o, m_ffn2_norm, m_ffn2_w_gate, m_ffn2_w_up, m_ffn2_w_down, m_final_norm, v_ffn1_norm, v_ffn1_w_gate, v_ffn1_w_up, v_ffn1_w_down, v_mix_norm, v_w_in, v_pool_w, v_pool_scale, v_gla_w_a2, v_gla_b_a, v_gla_head_norm, v_w_out, v_xattn_norm, v_mem_norm, v_xattn_w_q, v_xattn_w_kv, v_xattn_w_o, v_ffn2_norm, v_ffn2_w_gate, v_ffn2_w_up, v_ffn2_w_down, v_final_norm):
    given = dict(locals())
    wl = {n: given[n] for n in WEIGHTS}
    ml = {n: given["m_" + n] for n in WEIGHTS}
    vl = {n: given["v_" + n] for n in WEIGHTS}

    vec = {n: wl[n].reshape(1, -1) for n in REPLICATED}
    weight, dep0 = _start_gathers(wl)
    in_flight = {}

    pair_flight = {}

    def emit_begin(n, gfull):
        *pair_flight[n], token = _pair_start(_shard_major(n, gfull, wl), f"{n}_pair_start")
        return token

    summing = {}

    def emit_finish(n, after):
        gsm, other = _pair_wait(*pair_flight.pop(n), after, f"{n}_pair_wait")
        if n in PAIR_SUM_ON_SPARSECORE:
            summing[n] = _pair_add_sc(gsm, other, f"{n}_pair_add_sc")
            return None
        *in_flight[n], token = _chip_start(_pair_add(gsm, other, f"{n}_pair_add"), f"{n}_chip_start")
        return token

    def emit_send(n):
        *in_flight[n], token = _chip_start(summing.pop(n), f"{n}_chip_start")
        return token

    grads = {}
    updates = {}

    def reduce_done(n, after):
        part, slots = _chip_wait(*in_flight.pop(n), after, f"{n}_chip_wait")
        grads[n], token = _pair_join(_chip_sum(part, slots, f"{n}_chip_sum"), f"{n}_pair_join")
        return token

    def early_update(after):
        tokens = [reduce_done(n, after) for n in ON_SPARSECORE]
        for n in ON_SPARSECORE:
            g2 = grads[n]
            updates[n] = _adamw_sc(wl[n].reshape(g2.shape), g2, ml[n].reshape(g2.shape), vl[n].reshape(g2.shape),
                                   f"adamw_sc_{n}")
        return tokens

    loss, dx0, g = _local_step(x[0], mem[0], loss_target[0], vec, weight,
                               (emit_begin, emit_finish, emit_send, early_update), dep0)
    assert not summing

    for n in list(in_flight):
        reduce_done(n, dx0)
    widths = [wl[n].size for n in REPLICATED]
    total = sum(widths)
    rows = -(-total // SMALL_COLS)
    rows = -(-rows // 8) * 8
    packed = jnp.concatenate([g[n].reshape(-1) for n in REPLICATED] + [jnp.zeros((rows * SMALL_COLS - total,), F32)])
    summed = _all_reduce_small(packed.reshape(rows, SMALL_COLS), "small_all_reduce").reshape(-1)
    off = 0
    for n, width in zip(REPLICATED, widths):
        grads[n] = summed[off:off + width].reshape(1, width)
        off += width

    out_g, out_d, out_m, out_v = [], [], [], []
    for n in WEIGHTS:
        shape = wl[n].shape
        g2 = grads[n]
        if n in updates:
            go, d, nm, nv = updates[n]
        else:
            dep = tuple(updates[k][1][:8, :LANES] for k in updates) if n == "w_in" else ()
            go, d, nm, nv = _adamw(wl[n].reshape(g2.shape), g2, ml[n].reshape(g2.shape), vl[n].reshape(g2.shape),
                                   f"adamw_{n}", dep)
        out_g.append(go.reshape(shape))
        out_d.append(d.reshape(shape))
        out_m.append(nm.reshape(shape))
        out_v.append(nv.reshape(shape))
    return (loss, dx0.reshape(x.shape), *out_g, *out_d, *out_m, *out_v)
```

```python
import functools

import jax
import jax.numpy as jnp
from jax import lax
from jax.experimental import pallas as pl
from jax.experimental.pallas import tpu as pltpu
from jax.experimental.pallas import tpu_sc as plsc

F32 = jnp.float32
BF16 = jnp.bfloat16
MESH = pl.DeviceIdType.MESH

RMS_EPS = 1e-6
CHUNK = 64
POOL_WINDOWS = (2, 4, 8, 16)
POOL_HALO = 16
N_HEADS = 4
GATE_TEMP = 16.0
ADAM_LR, ADAM_B1, ADAM_B2, ADAM_EPS, ADAM_WD, ADAM_STEP = 0.001, 0.9, 0.999, 1e-08, 0.01, 10
N_SHARDS = 4
LANES = 128
MXU_COLS = 256
TOKENS_PER_STEP = 2048
VMEM_LIMIT = 58 * 1024 * 1024

ANY = pl.BlockSpec(memory_space=pl.ANY)
HBM = pl.BlockSpec(memory_space=pltpu.HBM)
SEM = pl.BlockSpec(memory_space=pltpu.SEMAPHORE)
EFFECT = pltpu.SideEffectType.DATAFLOW_SIDE_EFFECTING


def _params(**kw):
    return pltpu.CompilerParams(vmem_limit_bytes=VMEM_LIMIT, **kw)


def _tile(n, want):
    for unit in (LANES, 8):
        t = (min(want, n) // unit) * unit
        while t >= unit:
            if n % t == 0:
                return t
            t -= unit
    return n


def _dot(a, b, dims):
    return lax.dot_general(a, b, (dims, ((), ())), preferred_element_type=F32)


def _nn(a, b):
    return _dot(a, b, ((1,), (0,)))


def _nt(a, b):
    return _dot(a, b, ((1,), (1,)))


def _tn(a, b):
    return _dot(a, b, ((0,), (0,)))


def _sigmoid(x):
    return 1.0 / (1.0 + jnp.exp(-x))


def _matmul_cost(M, N, K, operands, out_shape):
    nbytes = sum(a.size * a.dtype.itemsize for a in operands) + out_shape.size * out_shape.dtype.itemsize
    return pl.CostEstimate(flops=2 * M * N * K, transcendentals=0, bytes_accessed=nbytes)


def _matmul(a, b, *, mode, name, out_dtype, tm=512, tn=2048, tk=2048, res=None, scale=1.0, b_groups=False, out_groups=0,
            dep=()):
    if mode == "tn":
        K, M = a.shape
    else:
        M, K = a.shape
    if mode == "nn":
        if b_groups:
            G, _, Nj = b.shape
            N = G * Nj
        else:
            N = b.shape[1]
    elif mode == "nt":
        if b_groups:
            G, N, Kj = b.shape
            assert G * Kj == K
        else:
            N = b.shape[0]
    else:
        N = b.shape[1]
    tm = _tile(M, tm)
    if mode == "nn" and b_groups:
        tn = _tile(Nj, tn)
    elif out_groups:
        tn = _tile(N // out_groups, tn)
    else:
        tn = _tile(N, tn)
    if mode == "nt" and b_groups:
        tk = _tile(Kj, tk)
    else:
        tk = _tile(K, tk)
    nk = K // tk
    grid = (M // tm, N // tn, nk)

    if mode == "tn":
        a_spec = pl.BlockSpec((tk, tm), lambda i, j, k: (k, i))
        b_spec = pl.BlockSpec((tk, tn), lambda i, j, k: (k, j))
        dims = ((0,), (0,))
    elif mode == "nn":
        a_spec = pl.BlockSpec((tm, tk), lambda i, j, k: (i, k))
        if b_groups:
            npj = Nj // tn
            b_spec = pl.BlockSpec((None, tk, tn), lambda i, j, k: (j // npj, k, j % npj))
        else:
            b_spec = pl.BlockSpec((tk, tn), lambda i, j, k: (k, j))
        dims = ((1,), (0,))
    else:
        a_spec = pl.BlockSpec((tm, tk), lambda i, j, k: (i, k))
        if b_groups:
            kpj = Kj // tk
            b_spec = pl.BlockSpec((None, tn, tk), lambda i, j, k: (k // kpj, j, k % kpj))
        else:
            b_spec = pl.BlockSpec((tn, tk), lambda i, j, k: (j, k))
        dims = ((1,), (1,))
    if out_groups:
        npj = (N // out_groups) // tn
        o_spec = pl.BlockSpec((None, tm, tn), lambda i, j, k: (j // npj, i, j % npj))
        out_shape = jax.ShapeDtypeStruct((out_groups, M, N // out_groups), out_dtype)
    else:
        o_spec = pl.BlockSpec((tm, tn), lambda i, j, k: (i, j))
        out_shape = jax.ShapeDtypeStruct((M, N), out_dtype)
    in_specs = [a_spec, b_spec]
    operands = [a, b]
    if res is not None:
        in_specs.append(pl.BlockSpec((tm, tn), lambda i, j, k: (i, j)))
        operands.append(res)
    has_res = res is not None
    n_dep = len(dep)
    for d in dep:
        in_specs.append(pl.BlockSpec(d.shape, lambda i, j, k: (0, 0)))
        operands.append(d)

    def body(*refs):
        if has_res:
            a_ref, b_ref, r_ref = refs[:3]
        else:
            a_ref, b_ref = refs[:2]
            r_ref = None
        o_ref = refs[2 + has_res + n_dep]

        def finish(acc):
            if scale != 1.0:
                acc = acc * scale
            if r_ref is not None:
                acc = r_ref[...] + acc
            o_ref[...] = acc.astype(o_ref.dtype)

        part = _dot(a_ref[...], b_ref[...], dims)
        if nk == 1:
            finish(part)
        else:
            acc_ref = o_ref if in_place else refs[-1]
            k = pl.program_id(2)

            @pl.when(k == 0)
            def _():
                acc_ref[...] = part

            @pl.when(k > 0)
            def _():
                acc_ref[...] += part

            if not in_place:
                @pl.when(k == nk - 1)
                def _():
                    finish(acc_ref[...])

    in_place = out_dtype == F32 and res is None and scale == 1.0
    scratch = [] if nk == 1 or in_place else [pltpu.VMEM((tm, tn), F32)]
    return pl.pallas_call(body, name=name, grid=grid, in_specs=in_specs, out_specs=o_spec, out_shape=out_shape,
                          scratch_shapes=scratch, compiler_params=_params(),
                          cost_estimate=_matmul_cost(M, N, K, operands, out_shape))(*operands)


def _rms_fwd(x, gain, name, tm=512, dep=None):
    S, D = x.shape
    tm = _tile(S, tm)

    def body(x_ref, g_ref, *rest):
        o_ref = rest[-1]
        xv = x_ref[...]
        r = lax.rsqrt(jnp.mean(xv * xv, axis=-1, keepdims=True) + RMS_EPS)
        o_ref[...] = (xv * r * g_ref[...]).astype(o_ref.dtype)

    in_specs = [pl.BlockSpec((tm, D), lambda i: (i, 0)), pl.BlockSpec((1, D), lambda i: (0, 0))]
    operands = [x, gain]
    if dep is not None:
        in_specs.append(pl.BlockSpec(dep.shape, lambda i: (0, 0)))
        operands.append(dep)
    return pl.pallas_call(body, name=name, grid=(S // tm,), in_specs=in_specs,
                          out_specs=pl.BlockSpec((tm, D), lambda i: (i, 0)),
                          out_shape=jax.ShapeDtypeStruct((S, D), BF16), compiler_params=_params())(*operands)


def _rms_bwd(x, gain, dh, dres, name, lowp=None, tm=512):
    half = lowp is not None
    S, D = x.shape
    tm = _tile(S, tm)
    has_res = dres is not None

    def body(*refs):
        if has_res:
            x_ref, g_ref, dh_ref, dr_ref = refs[:4]
            outs = refs[4:]
        else:
            x_ref, g_ref, dh_ref = refs[:3]
            dr_ref = None
            outs = refs[3:]
        dx_ref, dg_ref = outs[0], outs[-1]
        xv = x_ref[...]
        dhv = dh_ref[...].astype(F32)
        r = lax.rsqrt(jnp.mean(xv * xv, axis=-1, keepdims=True) + RMS_EPS)
        gy = dhv * g_ref[...]
        dx = r * gy - xv * (r * r * r) * jnp.mean(gy * xv, axis=-1, keepdims=True)
        if dr_ref is not None:
            dx = dx + dr_ref[...]
        dx_ref[...] = dx
        if half:
            outs[1][...] = (dx if lowp == 1.0 else lowp * dx).astype(BF16)
        part = jnp.sum(dhv * xv * r, axis=0, keepdims=True)

        @pl.when(pl.program_id(0) == 0)
        def _():
            dg_ref[...] = part

        @pl.when(pl.program_id(0) > 0)
        def _():
            dg_ref[...] += part

    row = pl.BlockSpec((tm, D), lambda i: (i, 0))
    vec = pl.BlockSpec((1, D), lambda i: (0, 0))
    in_specs = [row, vec, row] + ([row] if has_res else [])
    operands = [x, gain, dh] + ([dres] if has_res else [])
    out_specs = [row] + ([row] if half else []) + [vec]
    out_shape = [jax.ShapeDtypeStruct((S, D), F32)] + ([jax.ShapeDtypeStruct((S, D), BF16)] if half else []) + [
        jax.ShapeDtypeStruct((1, D), F32)]
    return pl.pallas_call(body, name=name, grid=(S // tm,), in_specs=in_specs, out_specs=out_specs, out_shape=out_shape,
                          compiler_params=_params())(*operands)


def _loss_head(x, gain, target, name, tm=512):
    S, D = x.shape
    tm = _tile(S, tm)

    def body(x_ref, g_ref, t_ref, sq_ref, dx_ref, dxh_ref, dg_ref):
        xv = x_ref[...]
        r = lax.rsqrt(jnp.mean(xv * xv, axis=-1, keepdims=True) + RMS_EPS)
        xn = xv * r
        err = xn * g_ref[...] - t_ref[...]
        dout = err * (1.0 / D)
        gy = dout * g_ref[...]
        dx = r * gy - xv * (r * r * r) * jnp.mean(gy * xv, axis=-1, keepdims=True)
        dx_ref[...] = dx
        dxh_ref[...] = (0.5 * dx).astype(BF16)
        sq = jnp.sum(err * err, axis=0, keepdims=True)
        dg = jnp.sum(dout * xn, axis=0, keepdims=True)

        @pl.when(pl.program_id(0) == 0)
        def _():
            sq_ref[...] = sq
            dg_ref[...] = dg

        @pl.when(pl.program_id(0) > 0)
        def _():
            sq_ref[...] += sq
            dg_ref[...] += dg

    row = pl.BlockSpec((tm, D), lambda i: (i, 0))
    vec = pl.BlockSpec((1, D), lambda i: (0, 0))
    return pl.pallas_call(body, name=name, grid=(S // tm,), in_specs=[row, vec, row], out_specs=[vec, row, row, vec],
                          out_shape=[jax.ShapeDtypeStruct((1, D), F32), jax.ShapeDtypeStruct((S, D), F32),
                                     jax.ShapeDtypeStruct((S, D), BF16), jax.ShapeDtypeStruct((1, D), F32)],
                          compiler_params=_params())(x, gain, target)


def _cast(x, dtype, name, scale=1.0, tm=256):
    S, D = x.shape
    tm = _tile(S, tm)

    def body(x_ref, o_ref):
        o_ref[...] = (x_ref[...] * scale).astype(o_ref.dtype)

    row = pl.BlockSpec((tm, D), lambda i: (i, 0))
    return pl.pallas_call(body, name=name, grid=(S // tm,), in_specs=[row], out_specs=row,
                          out_shape=jax.ShapeDtypeStruct((S, D), dtype), compiler_params=_params())(x)


def _ffn_up(h, wg, wu, name, tm=512):
    S, D = h.shape
    G, _, Fj = wg.shape
    tm = _tile(S, tm)

    def body(h_ref, wg_ref, wu_ref, ga_ref, gb_ref, hid_ref):
        hv = h_ref[...]
        a = _nn(hv, wg_ref[...])
        b = _nn(hv, wu_ref[...])
        s = _sigmoid(a)
        silu = a * s
        ga_ref[...] = (b * (s * (1.0 + a * (1.0 - s)))).astype(BF16)
        gb_ref[...] = silu.astype(BF16)
        hid_ref[...] = (silu * b).astype(BF16)

    w_spec = pl.BlockSpec((None, D, Fj), lambda g, i: (g, 0, 0))
    o_spec = pl.BlockSpec((tm, Fj), lambda g, i: (i, g))
    out = jax.ShapeDtypeStruct((S, G * Fj), BF16)
    return pl.pallas_call(body, name=name, grid=(G, S // tm),
                          in_specs=[pl.BlockSpec((tm, D), lambda g, i: (i, 0)), w_spec, w_spec],
                          out_specs=[o_spec, o_spec, o_spec], out_shape=[out, out, out], compiler_params=_params())(h, wg, wu)


def _ffn_dact(dxh, wd, ga, gb, name, tm=512):
    S, D = dxh.shape
    G, Fj, _ = wd.shape
    tm = _tile(S, tm)

    def body(dx_ref, wd_ref, ga_ref, gb_ref, da_ref, db_ref):
        dhid = _nt(dx_ref[...], wd_ref[...])
        da_ref[...] = (dhid * ga_ref[...].astype(F32)).astype(BF16)
        db_ref[...] = (dhid * gb_ref[...].astype(F32)).astype(BF16)

    blk = pl.BlockSpec((tm, Fj), lambda g, i: (i, g))
    out = jax.ShapeDtypeStruct((S, G * Fj), BF16)
    return pl.pallas_call(body, name=name, grid=(G, S // tm),
                          in_specs=[pl.BlockSpec((tm, D), lambda g, i: (i, 0)),
                                    pl.BlockSpec((None, Fj, D), lambda g, i: (g, 0, 0)), blk, blk],
                          out_specs=[blk, blk], out_shape=[out, out], compiler_params=_params())(dxh, wd, ga, gb)


def _ffn_dh(da, db, wg, wu, name, dep=(), tm=512):
    S = da.shape[0]
    G, D, Fj = wg.shape
    tm = _tile(S, tm)

    def body(da_ref, db_ref, wg_ref, wu_ref, *rest):
        o_ref = rest[-1]
        part = _nt(da_ref[...], wg_ref[...]) + _nt(db_ref[...], wu_ref[...])

        @pl.when(pl.program_id(1) == 0)
        def _():
            o_ref[...] = part

        @pl.when(pl.program_id(1) > 0)
        def _():
            o_ref[...] += part

    act = pl.BlockSpec((tm, Fj), lambda i, g: (i, g))
    w_spec = pl.BlockSpec((None, D, Fj), lambda i, g: (g, 0, 0))
    in_specs = [act, act, w_spec, w_spec] + [pl.BlockSpec(d.shape, lambda i, g: (0, 0)) for d in dep]
    return pl.pallas_call(body, name=name, grid=(S // tm, G), in_specs=in_specs,
                          out_specs=pl.BlockSpec((tm, D), lambda i, g: (i, 0)),
                          out_shape=jax.ShapeDtypeStruct((S, D), F32), compiler_params=_params(),
                          cost_estimate=_matmul_cost(S, D, 2 * G * Fj, (da, db, wg, wu), jax.ShapeDtypeStruct((S, D), F32)),
                          )(da, db, wg, wu, *dep)


def _pool_fwd(proj, pool_w, pool_scale, name, tm=512):
    S = proj.shape[0]
    NG, C, _ = pool_w.shape
    DP = NG * C
    tm = _tile(S, tm)
    hb = tm // POOL_HALO
    n_ext = tm + POOL_HALO

    def body(u_ref, halo_ref, w_ref, sc_ref, y_ref, d_ref):
        i = pl.program_id(0)
        t = lax.broadcasted_iota(jnp.int32, (tm, 1), 0) + i * tm
        for g, win in enumerate(POOL_WINDOWS):
            cols = slice(g * C, (g + 1) * C)
            ug = u_ref[:, cols]
            halo = jnp.where(i > 0, halo_ref[:, cols], 0.0)
            acc = jnp.concatenate([halo, ug], axis=0)
            step = 1
            while step < win:
                acc = acc + pltpu.roll(acc, step, 0)
                step *= 2
            count = jnp.minimum(t + 1, win).astype(F32)
            d = (acc[POOL_HALO:, :] / count - ug).astype(BF16)
            d_ref[:, cols] = d
            y_ref[:, cols] = (_nn(d, w_ref[g]) * sc_ref[:, cols]).astype(BF16)

    del n_ext
    return pl.pallas_call(
        body, name=name, grid=(S // tm,),
        in_specs=[pl.BlockSpec((tm, DP), lambda i: (i, 0)),
                  pl.BlockSpec((POOL_HALO, DP), lambda i: (jnp.maximum(i * hb - 1, 0), 0)),
                  pl.BlockSpec((NG, C, C), lambda i: (0, 0, 0)), pl.BlockSpec((1, DP), lambda i: (0, 0))],
        out_specs=[pl.BlockSpec((tm, DP), lambda i: (i, 0)), pl.BlockSpec((tm, DP), lambda i: (i, 0))],
        out_shape=[jax.ShapeDtypeStruct((S, DP), BF16), jax.ShapeDtypeStruct((S, DP), BF16)],
        compiler_params=_params())(proj, proj, pool_w, pool_scale)


def _pool_bwd(dymix, d, pool_w, pool_scale, name, tm=512):
    S = dymix.shape[0]
    NG, C, _ = pool_w.shape
    DP = NG * C
    tm = _tile(S, tm)
    hb = tm // POOL_HALO
    nb = S // tm
    n_ext = tm + POOL_HALO
    last_halo = S // POOL_HALO - 1

    def body(dy_ref, halo_ref, d_ref, w_ref, sc_ref, du_ref, dw_ref, dsc_ref):
        i = pl.program_id(0)
        t = lax.broadcasted_iota(jnp.int32, (n_ext, 1), 0) + i * tm
        for g, win in enumerate(POOL_WINDOWS):
            cols = slice(g * C, (g + 1) * C)
            dy = dy_ref[:, cols]
            halo = jnp.where(i < nb - 1, halo_ref[:, cols], 0.0)
            sc = sc_ref[:, cols]
            dv = d_ref[:, cols]
            e_ext = (jnp.concatenate([dy, halo], axis=0) * sc).astype(BF16)
            dd = _nt(e_ext, w_ref[g])
            count = jnp.minimum(t + 1, win).astype(F32)
            acc = dd / count
            step = 1
            while step < win:
                acc = acc + pltpu.roll(acc, n_ext - step, 0)
                step *= 2
            du_ref[:, cols] = (acc[:tm, :] - dd[:tm, :]).astype(BF16)
            dw = _tn(dv, e_ext[:tm, :])
            dsc = jnp.sum(dy * _nn(dv, w_ref[g]), axis=0, keepdims=True)

            @pl.when(i == 0)
            def _():
                dw_ref[g] = dw
                dsc_ref[:, cols] = dsc

            @pl.when(i > 0)
            def _():
                dw_ref[g] += dw
                dsc_ref[:, cols] += dsc

    return pl.pallas_call(
        body, name=name, grid=(nb,),
        in_specs=[pl.BlockSpec((tm, DP), lambda i: (i, 0)),
                  pl.BlockSpec((POOL_HALO, DP), lambda i: (jnp.minimum((i + 1) * hb, last_halo), 0)),
                  pl.BlockSpec((tm, DP), lambda i: (i, 0)),
                  pl.BlockSpec((NG, C, C), lambda i: (0, 0, 0)), pl.BlockSpec((1, DP), lambda i: (0, 0))],
        out_specs=[pl.BlockSpec((tm, DP), lambda i: (i, 0)), pl.BlockSpec((NG, C, C), lambda i: (0, 0, 0)),
                   pl.BlockSpec((1, DP), lambda i: (0, 0))],
        out_shape=[jax.ShapeDtypeStruct((S, DP), BF16), jax.ShapeDtypeStruct((NG, C, C), F32),
                   jax.ShapeDtypeStruct((1, DP), F32)],
        compiler_params=_params())(dymix, dymix, d, pool_w, pool_scale)


def _chunk_scan(v, rows, reverse):
    n = v.shape[0]
    step = 1
    while step < CHUNK:
        if reverse:
            v = v + jnp.where(rows < CHUNK - step, pltpu.roll(v, n - step, 0), 0.0)
        else:
            v = v + jnp.where(rows >= step, pltpu.roll(v, step, 0), 0.0)
        step *= 2
    return v


def _log_decay(alr, w_a2, b_a):
    z = _nn(alr.astype(BF16), w_a2) + b_a
    la = (jnp.minimum(z, 0.0) - jnp.log(1.0 + jnp.exp(-jnp.abs(z)))) * (1.0 / GATE_TEMP)
    return z, la


def _gla_specs(DP, DKT, DV, tb, bmap):
    return [pl.BlockSpec((tb, DKT), lambda i: (bmap(i), DP // DKT)),
            pl.BlockSpec((tb, DKT), lambda i: (bmap(i), DP // DKT + 1)),
            pl.BlockSpec((tb, DV), lambda i: (bmap(i), (DP + 2 * DKT) // DV)),
            pl.BlockSpec((tb, DV), lambda i: (bmap(i), (DP + 2 * DKT) // DV + 1)),
            pl.BlockSpec((tb, LANES), lambda i: (bmap(i), (DP + 2 * DKT + 2 * DV) // LANES))]


def _gla_fwd(proj, y_pool, w_a2, b_a, head_norm, name, tb=512):
    S = proj.shape[0]
    DP = y_pool.shape[1]
    DKT = b_a.shape[1]
    DV = head_norm.shape[1]
    dk, dv = DKT // N_HEADS, DV // N_HEADS
    tb = _tile(S, tb)
    ncb = tb // CHUNK
    qscale = dk ** -0.5

    def body(q_ref, k_ref, v_ref, g_ref, alr_ref, yp_ref, wa_ref, ba_ref, hn_ref, y_ref, st_out_ref, st_ref, kdec_ref,
             gam_ref):
        @pl.when(pl.program_id(0) == 0)
        def _():
            st_ref[...] = jnp.zeros_like(st_ref)

        y_ref[:, :DP] = yp_ref[...]

        rows = lax.broadcasted_iota(jnp.int32, (tb, 1), 0) % CHUNK
        _, la = _log_decay(alr_ref[...], wa_ref[...], ba_ref[...])
        tail = _chunk_scan(la, rows, True)
        kdec_ref[...] = k_ref[...] * jnp.exp(tail - la)
        gam_ref[...] = jnp.exp(tail)

        def chunk(c, carry):
            r0 = pl.multiple_of(c * CHUNK, CHUNK)
            rs = pl.ds(r0, CHUNK)
            gam = gam_ref[pl.ds(r0, 1), :]
            heads = range(N_HEADS)
            kcs = [slice(h * dk, (h + 1) * dk) for h in heads]
            vcs = [slice(h * dv, (h + 1) * dv) for h in heads]
            upd = [_tn(v_ref[rs, vcs[h]].astype(BF16), kdec_ref[rs, kcs[h]].astype(BF16)) for h in heads]
            st = [st_ref[h] * gam[:, kcs[h]] + upd[h] for h in heads]
            o = [_nt((q_ref[rs, kcs[h]] * qscale).astype(BF16), st[h].astype(BF16)) for h in heads]
            for h in heads:
                st_ref[h] = st[h]
                st_out_ref[c, h] = st[h]
                r = lax.rsqrt(jnp.mean(o[h] * o[h], axis=-1, keepdims=True) + RMS_EPS)
                gv = g_ref[rs, vcs[h]]
                y_ref[rs, DP + h * dv:DP + (h + 1) * dv] = (o[h] * r * hn_ref[:, vcs[h]] * (gv * _sigmoid(gv))).astype(BF16)
            return carry

        lax.fori_loop(0, ncb, chunk, 0, unroll=2)

    full = lambda shape: pl.BlockSpec(shape, lambda i: (0,) * len(shape))
    return pl.pallas_call(
        body, name=name, grid=(S // tb,),
        in_specs=_gla_specs(DP, DKT, DV, tb, lambda i: i) + [pl.BlockSpec((tb, DP), lambda i: (i, 0)),
                                                            full((LANES, DKT)), full((1, DKT)), full((1, DV))],
        out_specs=[pl.BlockSpec((tb, DP + DV), lambda i: (i, 0)),
                   pl.BlockSpec((ncb, N_HEADS, dv, dk), lambda i: (i, 0, 0, 0))],
        out_shape=[jax.ShapeDtypeStruct((S, DP + DV), BF16), jax.ShapeDtypeStruct((S // CHUNK, N_HEADS, dv, dk), F32)],
        scratch_shapes=[pltpu.VMEM((N_HEADS, dv, dk), F32), pltpu.VMEM((tb, DKT), F32), pltpu.VMEM((tb, DKT), F32)],
        compiler_params=_params())(proj, proj, proj, proj, proj, y_pool, w_a2, b_a, head_norm)


def _gla_bwd(proj, states, dymix, du, w_a2, b_a, head_norm, name, tb=512):
    S = proj.shape[0]
    DP = du.shape[1]
    DKT = b_a.shape[1]
    DV = head_norm.shape[1]
    dk, dv = DKT // N_HEADS, DV // N_HEADS
    tb = _tile(S, tb)
    ncb = tb // CHUNK
    nb = S // tb
    qscale = dk ** -0.5
    rev = lambda i: nb - 1 - i

    q0, k0, v0, g0, a0 = DP, DP + DKT, DP + 2 * DKT, DP + 2 * DKT + DV, DP + 2 * DKT + 2 * DV

    def body(q_ref, k_ref, v_ref, g_ref, alr_ref, st_blk_ref, st_prev_ref, dy_ref, du_ref, wa_ref, ba_ref, hn_ref,
             dp_ref, dwa_ref, dba_ref, dhn_ref,
             dst_ref, kdec_ref, dec_ref, gam_ref, e_ref, dla_ref, dhn_acc_ref):
        i = pl.program_id(0)
        blk = rev(i)
        dp_ref[:, :DP] = du_ref[...]

        @pl.when(i == 0)
        def _():
            dst_ref[...] = jnp.zeros_like(dst_ref)

        dhn_acc_ref[...] = jnp.zeros_like(dhn_acc_ref)
        rows = lax.broadcasted_iota(jnp.int32, (tb, 1), 0) % CHUNK
        z, la = _log_decay(alr_ref[...], wa_ref[...], ba_ref[...])
        tail = _chunk_scan(la, rows, True)
        dec_ref[...] = jnp.exp(tail - la)
        kdec_ref[...] = k_ref[...] * dec_ref[...]
        gam_ref[...] = jnp.exp(tail)

        def chunk(cc, carry):
            c = ncb - 1 - cc
            r0 = pl.multiple_of(c * CHUNK, CHUNK)
            rs = pl.ds(r0, CHUNK)
            gam = gam_ref[pl.ds(r0, 1), :]
            first = jnp.logical_and(blk == 0, c == 0)
            heads = range(N_HEADS)
            kcs = [slice(h * dk, (h + 1) * dk) for h in heads]
            vcs = [slice(h * dv, (h + 1) * dv) for h in heads]
            qs = [(q_ref[rs, kcs[h]] * qscale).astype(BF16) for h in heads]
            stb = [st_blk_ref[c, h].astype(BF16) for h in heads]
            o = [_nt(qs[h], stb[h]) for h in heads]
            do = []
            for h in heads:
                oh = o[h]
                r = lax.rsqrt(jnp.mean(oh * oh, axis=-1, keepdims=True) + RMS_EPS)
                gv = g_ref[rs, vcs[h]]
                sg = _sigmoid(gv)
                dy = dy_ref[rs, vcs[h]]
                hn = hn_ref[:, vcs[h]]
                on = oh * r
                dp_ref[rs, g0 + h * dv:g0 + (h + 1) * dv] = (dy * on * hn * (sg * (1.0 + gv * (1.0 - sg)))).astype(BF16)
                don = dy * (gv * sg)
                dhn_acc_ref[:, vcs[h]] += jnp.sum(don * on, axis=0, keepdims=True)
                dn = don * hn
                do.append((r * dn - oh * (r * r * r) * jnp.mean(dn * oh, axis=-1, keepdims=True)).astype(BF16))
            dqs = [_nn(do[h], stb[h]) for h in heads]
            dst = [dst_ref[h] + _tn(do[h], qs[h]) for h in heads]
            for h in heads:
                dp_ref[rs, q0 + h * dk:q0 + (h + 1) * dk] = (dqs[h] * qscale).astype(BF16)
            dstb = [dst[h].astype(BF16) for h in heads]
            dvh = [_nt(kdec_ref[rs, kcs[h]].astype(BF16), dstb[h]) for h in heads]
            dkdec = [_nn(v_ref[rs, vcs[h]].astype(BF16), dstb[h]) for h in heads]
            gdg = []
            for h in heads:
                dp_ref[rs, v0 + h * dv:v0 + (h + 1) * dv] = dvh[h].astype(BF16)
                dp_ref[rs, k0 + h * dk:k0 + (h + 1) * dk] = (dkdec[h] * dec_ref[rs, kcs[h]]).astype(BF16)
                e_ref[rs, kcs[h]] = dkdec[h] * kdec_ref[rs, kcs[h]]
                st_prev = jnp.where(c > 0, st_blk_ref[jnp.maximum(c - 1, 0), h], st_prev_ref[0, h])
                st_prev = jnp.where(first, 0.0, st_prev)
                gdg.append(jnp.sum(dst[h] * st_prev, axis=0, keepdims=True) * gam[:, kcs[h]])
                dst_ref[h] = dst[h] * gam[:, kcs[h]]
            dla_ref[rs, :] = jnp.broadcast_to(jnp.concatenate(gdg, axis=1), (CHUNK, DKT))
            return carry

        lax.fori_loop(0, ncb, chunk, 0, unroll=2)

        ev = e_ref[...]
        dla = dla_ref[...] + _chunk_scan(ev, rows, False) - ev
        dz = dla * (1.0 / GATE_TEMP) * (1.0 - _sigmoid(z))
        dzb = dz.astype(BF16)
        dp_ref[:, a0:a0 + LANES] = _nt(dzb, wa_ref[...]).astype(BF16)
        dwa = _tn(alr_ref[...].astype(BF16), dzb)
        dba = jnp.sum(dz, axis=0, keepdims=True)

        @pl.when(i == 0)
        def _():
            dwa_ref[...] = dwa
            dba_ref[...] = dba
            dhn_ref[...] = dhn_acc_ref[...]

        @pl.when(i > 0)
        def _():
            dwa_ref[...] += dwa
            dba_ref[...] += dba
            dhn_ref[...] += dhn_acc_ref[...]

    full = lambda shape: pl.BlockSpec(shape, lambda i: (0,) * len(shape))
    rowblk = lambda w: pl.BlockSpec((tb, w), lambda i: (rev(i), 0))
    return pl.pallas_call(
        body, name=name, grid=(nb,),
        in_specs=_gla_specs(DP, DKT, DV, tb, rev) + [
            pl.BlockSpec((ncb, N_HEADS, dv, dk), lambda i: (rev(i), 0, 0, 0)),
            pl.BlockSpec((1, N_HEADS, dv, dk), lambda i: (jnp.maximum(rev(i) * ncb - 1, 0), 0, 0, 0)),
            pl.BlockSpec((tb, DV), lambda i: (rev(i), DP // DV)), rowblk(DP),
            full((LANES, DKT)), full((1, DKT)), full((1, DV))],
        out_specs=[rowblk(a0 + LANES), full((LANES, DKT)), full((1, DKT)), full((1, DV))],
        out_shape=[jax.ShapeDtypeStruct((S, a0 + LANES), BF16), jax.ShapeDtypeStruct((LANES, DKT), F32),
                   jax.ShapeDtypeStruct((1, DKT), F32), jax.ShapeDtypeStruct((1, DV), F32)],
        scratch_shapes=[pltpu.VMEM((N_HEADS, dv, dk), F32)] + [pltpu.VMEM((tb, DKT), F32)] * 5 + [pltpu.VMEM((1, DV), F32)],
        compiler_params=_params())(proj, proj, proj, proj, proj, states, states, dymix, du, w_a2, b_a, head_norm)


def _xattn_fwd(q, kv, name, tm=512):
    S, D = q.shape
    M = kv.shape[0]
    hd = D // N_HEADS
    tm = _tile(S, tm)
    scale = hd ** -0.5

    def body(q_ref, k_ref, v_ref, o_ref):
        heads = range(N_HEADS)
        hcs = [slice(h * hd, (h + 1) * hd) for h in heads]
        s = [_nt(q_ref[:, hc], k_ref[:, hc]) * scale for hc in hcs]
        p = []
        for h in heads:
            e = jnp.exp(s[h] - jnp.max(s[h], axis=-1, keepdims=True))
            p.append((e / jnp.sum(e, axis=-1, keepdims=True)).astype(BF16))
        o = [_nn(p[h], v_ref[:, hcs[h]]) for h in heads]
        for h in heads:
            o_ref[:, hcs[h]] = o[h].astype(BF16)

    return pl.pallas_call(body, name=name, grid=(S // tm,),
                          in_specs=[pl.BlockSpec((tm, D), lambda i: (i, 0)), pl.BlockSpec((M, D), lambda i: (0, 0)),
                                    pl.BlockSpec((M, D), lambda i: (0, 1))],
                          out_specs=pl.BlockSpec((tm, D), lambda i: (i, 0)),
                          out_shape=jax.ShapeDtypeStruct((S, D), BF16), compiler_params=_params())(q, kv, kv)


def _xattn_bwd(q, kv, do, name, tm=512):
    S, D = q.shape
    M = kv.shape[0]
    hd = D // N_HEADS
    tm = _tile(S, tm)
    scale = hd ** -0.5

    def body(q_ref, k_ref, v_ref, do_ref, dq_ref, dkv_ref):
        first = pl.program_id(0) == 0
        heads = range(N_HEADS)
        hcs = [slice(h * hd, (h + 1) * hd) for h in heads]
        s = [_nt(q_ref[:, hc], k_ref[:, hc]) * scale for hc in hcs]
        dp = [_nt(do_ref[:, hc], v_ref[:, hc]) for hc in hcs]
        p = []
        for h in heads:
            e = jnp.exp(s[h] - jnp.max(s[h], axis=-1, keepdims=True))
            p.append(e / jnp.sum(e, axis=-1, keepdims=True))
        dvh = [_tn(p[h].astype(BF16), do_ref[:, hcs[h]]) for h in heads]
        ds = [((p[h] * (dp[h] - jnp.sum(dp[h] * p[h], axis=-1, keepdims=True))) * scale).astype(BF16) for h in heads]
        dqh = [_nn(ds[h], k_ref[:, hcs[h]]) for h in heads]
        dkh = [_tn(ds[h], q_ref[:, hcs[h]]) for h in heads]
        for h in heads:
            dq_ref[:, hcs[h]] = dqh[h].astype(BF16)

        @pl.when(first)
        def _():
            for h in heads:
                dkv_ref[:, hcs[h]] = dkh[h]
                dkv_ref[:, D + h * hd:D + (h + 1) * hd] = dvh[h]

        @pl.when(jnp.logical_not(first))
        def _():
            for h in heads:
                dkv_ref[:, hcs[h]] += dkh[h]
                dkv_ref[:, D + h * hd:D + (h + 1) * hd] += dvh[h]

    row = pl.BlockSpec((tm, D), lambda i: (i, 0))
    return pl.pallas_call(body, name=name, grid=(S // tm,),
                          in_specs=[row, pl.BlockSpec((M, D), lambda i: (0, 0)), pl.BlockSpec((M, D), lambda i: (0, 1)), row],
                          out_specs=[row, pl.BlockSpec((M, 2 * D), lambda i: (0, 0))],
                          out_shape=[jax.ShapeDtypeStruct((S, D), BF16), jax.ShapeDtypeStruct((M, 2 * D), F32)],
                          compiler_params=_params())(q, kv, kv, do)


def _local_step(x, mem, target, vec, weight, emit, dep0):
    DP = vec["pool_scale"].shape[1]
    g = {}
    pending = []
    begun = []
    summed = []
    emit_begin, emit_finish, emit_send, early_update = emit

    def behind(fn, *a, **kw):
        dep = tuple(pending)
        pending.clear()
        out = fn(*a, dep=dep, **kw)
        while summed:
            pending.append(emit_send(summed.pop(0)))
        while begun:
            name = begun.pop(0)
            token = emit_finish(name, out)
            if token is None:
                summed.append(name)
            else:
                pending.append(token)
        return out

    def mm(a, b, **kw):
        return behind(_matmul, a, b, **kw)

    def send(name, gfull):
        pending.append(emit_begin(name, gfull))
        begun.append(name)

    def ffn_fwd(xin, tag, dep):
        h = _rms_fwd(xin, vec[f"{tag}_norm"], f"{tag}_norm", dep=dep)
        ga, gb, hid = _ffn_up(h, weight(f"{tag}_w_gate", h), weight(f"{tag}_w_up", h), f"{tag}_up")
        wd = weight(f"{tag}_w_down", hid)
        G, Fj, D = wd.shape
        xo = _matmul(hid, wd.reshape(G * Fj, D), mode="nn", name=f"{tag}_down", out_dtype=F32, res=xin, scale=0.5,
                     tn=1024, tk=G * Fj)
        return xo, (h, ga, gb, hid)

    def ffn_bwd(dxh, saved, tag, kept_back=None):
        h, ga, gb, hid = saved
        wg, wu, wd = weight(f"{tag}_w_gate"), weight(f"{tag}_w_up"), weight(f"{tag}_w_down")
        G, Fj, D = wd.shape
        send(f"{tag}_w_down", mm(hid, dxh, mode="tn", name=f"{tag}_dwd", out_dtype=F32, tm=Fj, tn=1024))
        da, db = _ffn_dact(dxh, wd, ga, gb, f"{tag}_dact")
        send(f"{tag}_w_gate", mm(h, da, mode="tn", name=f"{tag}_dwg", out_dtype=F32, tm=1024, tn=Fj, tk=TOKENS_PER_STEP,
                                  out_groups=G))
        send(f"{tag}_w_up", mm(h, db, mode="tn", name=f"{tag}_dwu", out_dtype=F32, tm=1024, tn=Fj, tk=TOKENS_PER_STEP,
                                out_groups=G))
        if kept_back is not None:
            name, gfull = kept_back()
            pending.append(emit_begin(name, gfull))
            pending.append(emit_finish(name, pending[-1]))
        return behind(_ffn_dh, da, db, wg, wu, f"{tag}_dh")

    x1, ffn1_saved = ffn_fwd(x, "ffn1", dep0)
    h2 = _rms_fwd(x1, vec["mix_norm"], "mix_norm")
    w_in = weight("w_in", h2)
    proj = _matmul(h2, w_in, mode="nn", name="w_in", out_dtype=F32, tn=1408)
    pool_w, w_a2 = weight("pool_w", h2), weight("gla_w_a2", h2)
    y_pool, dpool = _pool_fwd(proj, pool_w, vec["pool_scale"], "pool_fwd")
    ymix, states = _gla_fwd(proj, y_pool, w_a2, vec["gla_b_a"], vec["gla_head_norm"], "gla_fwd")
    w_out = weight("w_out", ymix)
    x2 = _matmul(ymix, w_out, mode="nn", name="w_out", out_dtype=F32, res=x1)
    h3 = _rms_fwd(x2, vec["xattn_norm"], "xattn_norm")
    mh = _rms_fwd(mem, vec["mem_norm"], "mem_norm")
    w_q = weight("xattn_w_q", h3)
    q = _matmul(h3, w_q, mode="nn", name="xattn_q", out_dtype=BF16)
    w_kv = weight("xattn_w_kv", q)
    kv = _matmul(mh, w_kv, mode="nn", name="xattn_kv", out_dtype=BF16, b_groups=True, tn=1024)
    o = _xattn_fwd(q, kv, "xattn_fwd")
    w_o = weight("xattn_w_o", o)
    x3 = _matmul(o, w_o, mode="nn", name="xattn_o", out_dtype=F32, res=x2)
    x4, ffn2_saved = ffn_fwd(x3, "ffn2", None)
    sq, dx4, dx4h, g["final_norm"] = _loss_head(x4, vec["final_norm"], target, "loss_head")
    loss = lax.psum(0.5 * jnp.sum(sq) / x.shape[-1], ("x", "y", "c"))
    pending.append(loss.reshape(1, 1))

    dh = ffn_bwd(dx4h, ffn2_saved, "ffn2")
    dx3, dx3b, g["ffn2_norm"] = _rms_bwd(x3, vec["ffn2_norm"], dh, dx4, "ffn2_norm_bwd", lowp=1.0)
    send("xattn_w_o", mm(o, dx3b, mode="tn", name="xattn_dwo", out_dtype=F32, tm=1024, tn=1024, tk=TOKENS_PER_STEP))
    do = mm(dx3b, w_o, mode="nt", name="xattn_do", out_dtype=BF16)
    dq, dkv = _xattn_bwd(q, kv, do, "xattn_bwd")
    send("xattn_w_q", mm(h3, dq, mode="tn", name="xattn_dwq", out_dtype=F32, tm=1024, tn=1024, tk=TOKENS_PER_STEP))
    dh3 = mm(dq, w_q, mode="nt", name="xattn_dh", out_dtype=F32)
    dkvb = _cast(dkv, BF16, "dkv_cast")
    send("xattn_w_kv", mm(mh, dkvb, mode="tn", name="xattn_dwkv", out_dtype=F32, tm=1024, tn=1024, out_groups=N_SHARDS))
    dmh = mm(dkvb, w_kv, mode="nt", name="xattn_dmh", out_dtype=F32, b_groups=True, tk=1024)
    _, g["mem_norm"] = _rms_bwd(mem, vec["mem_norm"], dmh, None, "mem_norm_bwd")
    pending.append(g["mem_norm"])
    dx2, dx2b, g["xattn_norm"] = _rms_bwd(x2, vec["xattn_norm"], dh3, dx3, "xattn_norm_bwd", lowp=1.0)
    send("w_out", mm(ymix, dx2b, mode="tn", name="dw_out", out_dtype=F32, tm=1024, tn=1024, tk=TOKENS_PER_STEP))
    dymix = mm(dx2b, w_out, mode="nt", name="dymix", out_dtype=F32)
    du, dpool_w, g["pool_scale"] = _pool_bwd(dymix, dpool, pool_w, vec["pool_scale"], "pool_bwd")
    send("pool_w", dpool_w)
    dproj, dw_a2, g["gla_b_a"], g["gla_head_norm"] = _gla_bwd(
        proj, states, dymix, du, w_a2, vec["gla_b_a"], vec["gla_head_norm"], "gla_bwd")
    send("gla_w_a2", dw_a2)
    dh2 = mm(dproj, w_in, mode="nt", name="dh2", out_dtype=F32, tn=1024, tk=dproj.shape[1])
    pending.extend(early_update(dh2))
    dx1, dx1h, g["mix_norm"] = _rms_bwd(x1, vec["mix_norm"], dh2, dx2, "mix_norm_bwd", lowp=0.5)
    dh = ffn_bwd(dx1h, ffn1_saved, "ffn1", kept_back=lambda: (
        "w_in", mm(h2, dproj, mode="tn", name="dw_in", out_dtype=F32, tm=1024, tn=1408, tk=TOKENS_PER_STEP)))
    dx0, g["ffn1_norm"] = _rms_bwd(x, vec["ffn1_norm"], dh, dx1, "ffn1_norm_bwd")
    return loss, dx0, g


def _place():
    x, y, c = lax.axis_index("x"), lax.axis_index("y"), lax.axis_index("c")
    chips = [(1 - x, y), (x, 1 - y), (1 - x, 1 - y)]
    return x, y, c, chips


def _ids():
    return jnp.stack([2 * lax.axis_index("x") + lax.axis_index("y"), lax.axis_index("c")]).astype(jnp.int32)


def _hbm(a):
    return pltpu.with_memory_space_constraint(a, pltpu.HBM)


def _cast_to_slot(w2d, dtype, name, dep=None):
    R, C = w2d.shape
    tr = _tile(R, max(16, (4 << 20) // (4 * C) // 16 * 16))

    def body(i_ref, w_ref, *rest):
        rest[-1][...] = w_ref[...].astype(dtype)

    in_specs = [pl.BlockSpec((tr, C), lambda r, i: (r, 0))]
    operands = [w2d]
    if dep is not None:
        in_specs.append(pl.BlockSpec(dep.shape, lambda r, i: (0, 0)))
        operands.append(dep)
    grid_spec = pltpu.PrefetchScalarGridSpec(num_scalar_prefetch=1, grid=(R // tr,), in_specs=in_specs,
                                             out_specs=pl.BlockSpec((None, tr, C), lambda r, i: (i[0], r, 0)))
    return pl.pallas_call(body, name=name, grid_spec=grid_spec, out_shape=jax.ShapeDtypeStruct((N_SHARDS, R, C), dtype),
                          compiler_params=_params())(_ids(), *operands)


def _gather_copies(buf_ref, send_sems, recv_sems, incoming, whole):
    x, y, c, chips = _place()
    hr = buf_ref.shape[1] // 2
    copies = []
    for j, (px, py) in enumerate(chips):
        slot = 2 * px + py if incoming else 2 * x + y
        part = buf_ref.at[slot] if whole else buf_ref.at[slot, pl.ds(c * hr, hr), :]
        copies.append(pltpu.make_async_remote_copy(src_ref=part, dst_ref=part, send_sem=send_sems.at[j],
                                                   recv_sem=recv_sems.at[j], device_id=(px, py, c), device_id_type=MESH))
    return copies


def _gather_start(buf, name, whole):
    def body(b_ref, send_sems, recv_sems, b_thru, token):
        for cp in _gather_copies(b_ref, send_sems, recv_sems, False, whole):
            cp.start()
        token[...] = jnp.zeros_like(token)

    return pl.pallas_call(
        body, name=name,
        out_shape=(pltpu.SemaphoreType.DMA((3,)), pltpu.SemaphoreType.DMA((3,)), pltpu.HBM(buf.shape, buf.dtype),
                   jax.ShapeDtypeStruct((8, LANES), F32)),
        in_specs=(HBM,), out_specs=(SEM, SEM, HBM, pl.BlockSpec(memory_space=pltpu.VMEM)), input_output_aliases={0: 2},
        compiler_params=pltpu.CompilerParams(has_side_effects=EFFECT))(_hbm(buf))


def _gather_wait(send_sems, recv_sems, buf_thru, after, name, whole):
    def body(b_ref, send_sems, recv_sems, after_ref, b_out):
        for cp in _gather_copies(b_ref, send_sems, recv_sems, False, whole):
            cp.wait_send()
        for cp in _gather_copies(b_ref, send_sems, recv_sems, True, whole):
            cp.wait_recv()

    return pl.pallas_call(
        body, name=name, out_shape=pltpu.HBM(buf_thru.shape, buf_thru.dtype),
        in_specs=(HBM, SEM, SEM, ANY), out_specs=HBM, input_output_aliases={0: 0},
        compiler_params=pltpu.CompilerParams(has_side_effects=EFFECT))(buf_thru, send_sems, recv_sems, after)


def _gather_forward(buf, name):
    G, R, C = buf.shape
    hr = R // 2

    def body(b_ref, o_ref, send_sems, recv_sems):
        x, y, c, chips = _place()
        copies = []
        for j, (px, py) in enumerate(chips):
            half = o_ref.at[2 * px + py, pl.ds(c * hr, hr), :]
            copies.append(pltpu.make_async_remote_copy(src_ref=half, dst_ref=half, send_sem=send_sems.at[j],
                                                       recv_sem=recv_sems.at[j], device_id=(x, y, 1 - c),
                                                       device_id_type=MESH))
        for cp in copies:
            cp.start()
        for j, (px, py) in enumerate(chips):
            half = o_ref.at[2 * px + py, pl.ds((1 - c) * hr, hr), :]
            pltpu.make_async_remote_copy(src_ref=half, dst_ref=half, send_sem=send_sems.at[j], recv_sem=recv_sems.at[j],
                                         device_id=(x, y, 1 - c), device_id_type=MESH).wait_recv()
        for cp in copies:
            cp.wait_send()

    return pl.pallas_call(body, name=name, in_specs=[ANY], out_specs=ANY, out_shape=jax.ShapeDtypeStruct(buf.shape, buf.dtype),
                          input_output_aliases={0: 0},
                          scratch_shapes=[pltpu.SemaphoreType.DMA((3,)), pltpu.SemaphoreType.DMA((3,))])(buf)


def _pair_copy(g_ref, land_ref, send_sem, recv_sem):
    x, y, c, _ = _place()
    hr = g_ref.shape[1] // 2
    return pltpu.make_async_remote_copy(src_ref=g_ref.at[:, pl.ds((1 - c) * hr, hr), :], dst_ref=land_ref,
                                        send_sem=send_sem, recv_sem=recv_sem, device_id=(x, y, 1 - c), device_id_type=MESH)


def _pair_start(gfull, name):
    G, R, C = gfull.shape

    def body(g_ref, land_ref, send_sem, recv_sem, g_thru, land_thru, token):
        _pair_copy(g_ref, land_ref, send_sem, recv_sem).start()
        token[...] = jnp.zeros_like(token)

    return pl.pallas_call(
        body, name=name,
        out_shape=(pltpu.SemaphoreType.DMA(()), pltpu.SemaphoreType.DMA(()), pltpu.HBM(gfull.shape, F32),
                   pltpu.HBM((G, R // 2, C), F32), jax.ShapeDtypeStruct((8, LANES), F32)),
        in_specs=(HBM, HBM), out_specs=(SEM, SEM, HBM, HBM, pl.BlockSpec(memory_space=pltpu.VMEM)),
        input_output_aliases={0: 2, 1: 3},
        compiler_params=pltpu.CompilerParams(has_side_effects=EFFECT))(_hbm(gfull), _hbm(lax.empty((G, R // 2, C), F32)))


def _pair_wait(send_sem, recv_sem, g_thru, land_thru, after, name):
    def body(g_ref, land_ref, send_sem, recv_sem, after_ref, g_out, land_out):
        cp = _pair_copy(g_ref, land_ref, send_sem, recv_sem)
        cp.wait_send()
        cp.wait_recv()

    return pl.pallas_call(
        body, name=name, out_shape=(pltpu.HBM(g_thru.shape, F32), pltpu.HBM(land_thru.shape, F32)),
        in_specs=(HBM, HBM, SEM, SEM, ANY), out_specs=(HBM, HBM), input_output_aliases={0: 0, 1: 1},
        compiler_params=pltpu.CompilerParams(has_side_effects=EFFECT))(g_thru, land_thru, send_sem, recv_sem, after)


def _pair_add(gfull, other, name):
    G, R, C = gfull.shape
    hr = R // 2
    tr = _tile(hr, max(8, (2 * 1024 * 1024) // (4 * C) // 8 * 8))
    nr = hr // tr
    c = lax.axis_index("c")
    cidx = jnp.reshape(c, (1,)).astype(jnp.int32)

    def body(c_ref, a_ref, b_ref, o_ref):
        o_ref[...] = a_ref[...] + b_ref[...]

    grid_spec = pltpu.PrefetchScalarGridSpec(
        num_scalar_prefetch=1, grid=(G, nr),
        in_specs=[pl.BlockSpec((None, tr, C), lambda g, r, cr: (g, cr[0] * nr + r, 0)),
                  pl.BlockSpec((None, tr, C), lambda g, r, cr: (g, r, 0))],
        out_specs=pl.BlockSpec((None, tr, C), lambda g, r, cr: (g, r, 0)))
    return pl.pallas_call(body, name=name, grid_spec=grid_spec, out_shape=jax.ShapeDtypeStruct((G, hr, C), F32),
                          compiler_params=_params())(cidx, gfull, other)


def _chip_copies(p_ref, land_ref, send_sems, recv_sems, incoming):
    x, y, c, chips = _place()
    me = 2 * x + y
    copies = []
    for j, (px, py) in enumerate(chips):
        dst = land_ref.at[2 * px + py] if incoming else land_ref.at[me]
        copies.append(pltpu.make_async_remote_copy(src_ref=p_ref.at[2 * px + py], dst_ref=dst, send_sem=send_sems.at[j],
                                                   recv_sem=recv_sems.at[j], device_id=(px, py, c), device_id_type=MESH))
    return copies


def _chip_start(part, name):
    def body(p_ref, land_ref, send_sems, recv_sems, p_thru, land_thru, token):
        for cp in _chip_copies(p_ref, land_ref, send_sems, recv_sems, False):
            cp.start()
        token[...] = jnp.zeros_like(token)

    return pl.pallas_call(
        body, name=name,
        out_shape=(pltpu.SemaphoreType.DMA((3,)), pltpu.SemaphoreType.DMA((3,)), pltpu.HBM(part.shape, F32),
                   pltpu.HBM(part.shape, F32), jax.ShapeDtypeStruct((8, LANES), F32)),
        in_specs=(HBM, HBM), out_specs=(SEM, SEM, HBM, HBM, pl.BlockSpec(memory_space=pltpu.VMEM)),
        input_output_aliases={0: 2, 1: 3},
        compiler_params=pltpu.CompilerParams(has_side_effects=EFFECT))(_hbm(part), _hbm(lax.empty(part.shape, F32)))


def _chip_wait(send_sems, recv_sems, p_thru, land_thru, after, name):
    def body(p_ref, land_ref, send_sems, recv_sems, after_ref, p_out, land_out):
        for cp in _chip_copies(p_ref, land_ref, send_sems, recv_sems, False):
            cp.wait_send()
        for cp in _chip_copies(p_ref, land_ref, send_sems, recv_sems, True):
            cp.wait_recv()

    return pl.pallas_call(
        body, name=name, out_shape=(pltpu.HBM(p_thru.shape, F32), pltpu.HBM(p_thru.shape, F32)),
        in_specs=(HBM, HBM, SEM, SEM, ANY), out_specs=(HBM, HBM), input_output_aliases={0: 0, 1: 1},
        compiler_params=pltpu.CompilerParams(has_side_effects=EFFECT))(p_thru, land_thru, send_sems, recv_sems, after)


def _chip_sum(part, slots, name):
    G, R2, C = part.shape
    tr = _tile(R2, max(8, (1 << 20) // (4 * C) // 8 * 8))
    nr = R2 // tr

    def body(i_ref, p_ref, *rest):
        o_ref = rest[-1]
        acc = None
        for u in range(G):
            val = jnp.where(i_ref[0] == u, p_ref[...], rest[u][...])
            acc = val if acc is None else acc + val
        o_ref[...] = acc

    def slot_spec(u):
        return pl.BlockSpec((None, tr, C), lambda r, i: (jnp.where(i[0] == u, (u + 1) % G, u), r, 0))

    grid_spec = pltpu.PrefetchScalarGridSpec(
        num_scalar_prefetch=1, grid=(nr,),
        in_specs=[pl.BlockSpec((None, tr, C), lambda r, i: (i[0], r, 0))] + [slot_spec(u) for u in range(G)],
        out_specs=pl.BlockSpec((tr, C), lambda r, i: (i[1] * nr + r, 0)))
    return pl.pallas_call(body, name=name, grid_spec=grid_spec, out_shape=jax.ShapeDtypeStruct((2 * R2, C), F32),
                          compiler_params=_params())(_ids(), part, slots, slots, slots, slots)


def _sum_slots(slots, name):
    G, R2, C = slots.shape
    tr = _tile(R2, max(8, (1024 * 1024) // (4 * C) // 8 * 8))

    def body(s_ref, o_ref):
        acc = s_ref[0]
        for u in range(1, G):
            acc = acc + s_ref[u]
        o_ref[...] = acc

    return pl.pallas_call(body, name=name, grid=(R2 // tr,), in_specs=[pl.BlockSpec((G, tr, C), lambda r: (0, r, 0))],
                          out_specs=pl.BlockSpec((tr, C), lambda r: (r, 0)), out_shape=jax.ShapeDtypeStruct((R2, C), F32),
                          compiler_params=_params())(slots)


def _pair_join(full, name):
    R, C = full.shape
    R2 = R // 2

    def body(f_ref, o_ref, token, send_sem, recv_sem):
        x, y, c, _ = _place()
        token[...] = jnp.zeros_like(token)
        mine = o_ref.at[pl.ds(c * R2, R2), :]
        theirs = o_ref.at[pl.ds((1 - c) * R2, R2), :]
        cp = pltpu.make_async_remote_copy(src_ref=mine, dst_ref=mine, send_sem=send_sem, recv_sem=recv_sem,
                                          device_id=(x, y, 1 - c), device_id_type=MESH)
        cp.start()
        pltpu.make_async_remote_copy(src_ref=theirs, dst_ref=theirs, send_sem=send_sem, recv_sem=recv_sem,
                                     device_id=(x, y, 1 - c), device_id_type=MESH).wait_recv()
        cp.wait_send()

    return pl.pallas_call(body, name=name, in_specs=[ANY], out_specs=[ANY, pl.BlockSpec(memory_space=pltpu.VMEM)],
                          out_shape=[jax.ShapeDtypeStruct((R, C), F32), jax.ShapeDtypeStruct((8, LANES), F32)],
                          input_output_aliases={0: 0},
                          scratch_shapes=[pltpu.SemaphoreType.DMA, pltpu.SemaphoreType.DMA])(full)


def _all_reduce_small(v, name):
    R, C = v.shape

    def gather_body(v_ref, out_ref, send_sems, recv_sems, local_sem):
        x, y, c, _ = _place()
        me = 4 * x + 2 * y + c
        mine = pltpu.make_async_copy(v_ref, out_ref.at[me], local_sem)
        mine.start()
        flips = [(fx, fy, fc) for fx in (0, 1) for fy in (0, 1) for fc in (0, 1)][1:]
        copies = []
        for j, (fx, fy, fc) in enumerate(flips):
            peer = (x ^ fx, y ^ fy, c ^ fc)
            copies.append(pltpu.make_async_remote_copy(src_ref=v_ref, dst_ref=out_ref.at[me], send_sem=send_sems.at[j],
                                                       recv_sem=recv_sems.at[j], device_id=peer, device_id_type=MESH))
        for cp in copies:
            cp.start()
        for j, (fx, fy, fc) in enumerate(flips):
            peer = (x ^ fx, y ^ fy, c ^ fc)
            pltpu.make_async_remote_copy(src_ref=v_ref, dst_ref=out_ref.at[4 * peer[0] + 2 * peer[1] + peer[2]],
                                         send_sem=send_sems.at[j], recv_sem=recv_sems.at[j], device_id=peer,
                                         device_id_type=MESH).wait_recv()
        for cp in copies:
            cp.wait_send()
        mine.wait()

    slots = pl.pallas_call(gather_body, name=name, in_specs=[ANY], out_specs=ANY,
                           out_shape=jax.ShapeDtypeStruct((8, R, C), F32),
                           scratch_shapes=[pltpu.SemaphoreType.DMA((7,)), pltpu.SemaphoreType.DMA((7,)),
                                           pltpu.SemaphoreType.DMA])(v)
    return _sum_slots(slots, f"{name}_sum")


def _adamw(w, g, m, v, name, dep=()):
    R, C = w.shape
    tr = _tile(R, max(8, (2 << 20) // (4 * C) // 8 * 8))
    bc1 = 1.0 - ADAM_B1 ** ADAM_STEP
    bc2 = 1.0 - ADAM_B2 ** ADAM_STEP

    def body(w_ref, g_ref, m_ref, v_ref, *rest):
        go_ref, d_ref, nm_ref, nv_ref = rest[len(dep):]
        gv = g_ref[...]
        go_ref[...] = gv
        nm = ADAM_B1 * m_ref[...] + (1.0 - ADAM_B1) * gv
        nv = ADAM_B2 * v_ref[...] + (1.0 - ADAM_B2) * (gv * gv)
        nm_ref[...] = nm
        nv_ref[...] = nv
        d_ref[...] = -ADAM_LR * ((nm / bc1) / (jnp.sqrt(nv / bc2) + ADAM_EPS) + ADAM_WD * w_ref[...])

    blk = pl.BlockSpec((tr, C), lambda r: (r, 0))
    out = jax.ShapeDtypeStruct((R, C), F32)
    in_specs = [blk] * 4 + [pl.BlockSpec(d.shape, lambda r: (0, 0)) for d in dep]
    return pl.pallas_call(body, name=name, grid=(R // tr,), in_specs=in_specs, out_specs=[blk] * 4, out_shape=[out] * 4,
                          compiler_params=_params())(w, g, m, v, *dep)


SC_TILES = 32
SC_LANES = 16
SC_ROWS = 8


def _sc_mesh():
    return plsc.VectorSubcoreMesh(core_axis_name="sc_core", subcore_axis_name="sc_subcore")


def _pair_add_sc(gfull, other, name):
    G, R, C = gfull.shape
    hr = R // 2
    tiles_per_shard = SC_TILES // G
    per_tile = hr // SC_ROWS // tiles_per_shard

    def body(g_hbm, o_hbm, out_hbm, gb, ob):
        c = lax.axis_index("c")
        tile = lax.axis_index("sc_subcore") * 2 + lax.axis_index("sc_core")
        t = tile // tiles_per_shard
        first = (tile % tiles_per_shard) * per_tile

        @pl.loop(0, per_tile)
        def _(k):
            rr = (first + k) * SC_ROWS
            pltpu.sync_copy(g_hbm.at[t, pl.ds(c * hr + rr, SC_ROWS), :], gb)
            pltpu.sync_copy(o_hbm.at[t, pl.ds(rr, SC_ROWS), :], ob)

            @pl.loop(0, SC_ROWS)
            def _(i):
                @pl.loop(0, C, step=SC_LANES)
                def _(j):
                    at = (i, pl.ds(j, SC_LANES))
                    gb[at] = gb[at] + ob[at]

            pltpu.sync_copy(gb, out_hbm.at[t, pl.ds(rr, SC_ROWS), :])

    buf = pltpu.VMEM((SC_ROWS, C), F32)
    return pl.kernel(body, name=name, out_type=jax.ShapeDtypeStruct((G, hr, C), F32), mesh=_sc_mesh(),
                     scratch_types=[buf, buf])(gfull, other)


def _adamw_sc(w, g, m, v, name):
    R, C = w.shape
    tasks = R // SC_ROWS
    bc1 = 1.0 - ADAM_B1 ** ADAM_STEP
    bc2 = 1.0 - ADAM_B2 ** ADAM_STEP

    def body(w_hbm, g_hbm, m_hbm, v_hbm, go_hbm, d_hbm, nm_hbm, nv_hbm, wb, gb, mb, vb):
        tile = lax.axis_index("sc_subcore") * 2 + lax.axis_index("sc_core")

        @pl.loop((tile * tasks) // SC_TILES, ((tile + 1) * tasks) // SC_TILES)
        def _(task):
            rows = pl.ds(task * SC_ROWS, SC_ROWS)
            pltpu.sync_copy(w_hbm.at[rows, :], wb)
            pltpu.sync_copy(g_hbm.at[rows, :], gb)
            pltpu.sync_copy(m_hbm.at[rows, :], mb)
            pltpu.sync_copy(v_hbm.at[rows, :], vb)

            @pl.loop(0, SC_ROWS)
            def _(i):
                @pl.loop(0, C, step=SC_LANES)
                def _(j):
                    at = (i, pl.ds(j, SC_LANES))
                    gv = gb[at]
                    nm = ADAM_B1 * mb[at] + (1.0 - ADAM_B1) * gv
                    nv = ADAM_B2 * vb[at] + (1.0 - ADAM_B2) * (gv * gv)
                    mb[at] = nm
                    vb[at] = nv
                    wb[at] = -ADAM_LR * ((nm / bc1) / (jnp.sqrt(nv / bc2) + ADAM_EPS) + ADAM_WD * wb[at])

            pltpu.sync_copy(gb, go_hbm.at[rows, :])
            pltpu.sync_copy(wb, d_hbm.at[rows, :])
            pltpu.sync_copy(mb, nm_hbm.at[rows, :])
            pltpu.sync_copy(vb, nv_hbm.at[rows, :])

    out = jax.ShapeDtypeStruct((R, C), F32)
    buf = pltpu.VMEM((SC_ROWS, C), F32)
    return pl.kernel(body, name=name, out_type=(out, out, out, out),
                     mesh=plsc.VectorSubcoreMesh(core_axis_name="sc_core", subcore_axis_name="sc_subcore"),
                     scratch_types=[buf, buf, buf, buf],
                     cost_estimate=pl.CostEstimate(flops=16 * R * C, transcendentals=2 * R * C, bytes_accessed=32 * R * C),
                     )(w, g, m, v)


WEIGHTS = ['ffn1_norm', 'ffn1_w_gate', 'ffn1_w_up', 'ffn1_w_down', 'mix_norm', 'w_in', 'pool_w', 'pool_scale', 'gla_w_a2',
           'gla_b_a', 'gla_head_norm', 'w_out', 'xattn_norm', 'mem_norm', 'xattn_w_q', 'xattn_w_kv', 'xattn_w_o', 'ffn2_norm',
           'ffn2_w_gate', 'ffn2_w_up', 'ffn2_w_down', 'final_norm']
SHARDED = ['ffn1_w_gate', 'ffn1_w_up', 'ffn1_w_down', 'w_in', 'pool_w', 'gla_w_a2', 'w_out', 'xattn_w_q', 'xattn_w_kv',
           'xattn_w_o', 'ffn2_w_gate', 'ffn2_w_up', 'ffn2_w_down']
REPLICATED = [n for n in WEIGHTS if n not in SHARDED]
ON_SPARSECORE = ['ffn2_w_down', 'ffn2_w_gate', 'ffn2_w_up', 'w_out', 'xattn_w_q', 'xattn_w_kv', 'xattn_w_o']
PAIR_SUM_ON_SPARSECORE = ['ffn2_w_down', 'ffn2_w_gate', 'ffn2_w_up', 'xattn_w_o', 'xattn_w_kv', 'pool_w', 'ffn1_w_down']
SMALL_COLS = 512


def _as2d(a):
    return a.reshape(-1, a.shape[-1])


def _finish_weight(name, gathered, wl):
    G, R, C = gathered.shape
    rank = wl["gla_w_a2"].shape[1]
    if name in ("w_out", "xattn_w_q", "xattn_w_o"):
        return gathered.reshape(G * R, C)
    if name == "w_in":
        w_in = jnp.transpose(gathered, (1, 0, 2)).reshape(R, G * C)
        main = G * C - rank
        return jnp.concatenate([w_in[:, :main], jnp.pad(w_in[:, main:], ((0, 0), (0, LANES - rank)))], axis=1)
    if name == "pool_w":
        NG, CJ, _ = wl[name].shape[1:]
        return jnp.transpose(gathered.reshape(G, NG, CJ, C), (1, 0, 2, 3)).reshape(NG, G * CJ, C)
    if name == "gla_w_a2":
        a2 = jnp.transpose(gathered, (1, 0, 2)).reshape(rank, G * C)
        return jnp.pad(a2, ((0, LANES - rank), (0, 0))).astype(BF16)
    return gathered


def _start_gathers(wl):
    started = {}
    token = None
    for n in SHARDED:
        whole = n not in ("ffn1_w_gate", "ffn1_w_up")
        buf = _cast_to_slot(_as2d(wl[n]), BF16, f"slot_{n}", dep=token)
        send_sems, recv_sems, thru, token = _gather_start(buf, f"gather_start_{n}", whole)
        started[n] = (send_sems, recv_sems, thru, whole)
    cache = {}

    def weight(n, after=None):
        if n not in cache:
            *handles, whole = started[n]
            buf = _gather_wait(*handles, after, f"gather_wait_{n}", whole)
            if not whole:
                buf = _gather_forward(buf, f"gather_forward_{n}")
            cache[n] = _finish_weight(n, buf, wl)
        return cache[n]

    return weight, token


def _shard_major(name, gfull, wl):
    R, C = _as2d(wl[name]).shape
    if name in ("ffn1_w_gate", "ffn1_w_up", "ffn2_w_gate", "ffn2_w_up", "xattn_w_kv"):
        return gfull
    if name in ("ffn1_w_down", "ffn2_w_down", "w_out", "xattn_w_q", "xattn_w_o"):
        return gfull.reshape(N_SHARDS, R, C)
    if name == "w_in":
        return jnp.transpose(gfull[:, :N_SHARDS * C].reshape(R, N_SHARDS, C), (1, 0, 2))
    if name == "pool_w":
        NG, CJ, _ = wl[name].shape[1:]
        return jnp.transpose(gfull.reshape(NG, N_SHARDS, CJ, C), (1, 0, 2, 3)).reshape(N_SHARDS, R, C)
    assert name == "gla_w_a2"
    return jnp.transpose(gfull[:R].reshape(R, N_SHARDS, C), (1, 0, 2))


def kernel(x, mem, ffn1_norm, ffn1_w_gate, ffn1_w_up, ffn1_w_down, mix_norm, w_in, pool_w, pool_scale, gla_w_a2, gla_b_a, gla_head_norm, w_out, xattn_norm, mem_norm, xattn_w_q, xattn_w_kv, xattn_w_o, ffn2_norm, ffn2_w_gate, ffn2_w_up, ffn2_w_down, final_norm, loss_target, m_ffn1_norm, m_ffn1_w_gate, m_ffn1_w_up, m_ffn1_w_down, m_mix_norm, m_w_in, m_pool_w, m_pool_scale, m_gla_w_a2, m_gla_b_a, m_gla_head_norm, m_w_out, m_xattn_norm, m_mem_norm, m_xattn_w_q, m_xattn_w_kv, m_xattn_w_o, m_ffn2_norm, m_ffn2_w_gate, m_ffn2_w_up, m_ffn2_w_down, m_final_norm, v_ffn1_norm, v_ffn1_w_gate, v_ffn1_w_up, v_ffn1_w_down, v_mix_norm, v_w_in, v_pool_w, v_pool_scale, v_gla_w_a2, v_gla_b_a, v_gla_head_norm, v_w_out, v_xattn_norm, v_mem_norm, v_xattn_w_q, v_xattn_w_kv, v_xattn_w_o, v_ffn2_norm, v_ffn2_w_gate, v_ffn2_w_up, v_ffn2_w_down, v_final_norm):
    given = dict(locals())
    wl = {n: given[n] for n in WEIGHTS}
    ml = {n: given["m_" + n] for n in WEIGHTS}
    vl = {n: given["v_" + n] for n in WEIGHTS}

    vec = {n: wl[n].reshape(1, -1) for n in REPLICATED}
    weight, dep0 = _start_gathers(wl)
    in_flight = {}

    pair_flight = {}

    def emit_begin(n, gfull):
        *pair_flight[n], token = _pair_start(_shard_major(n, gfull, wl), f"{n}_pair_start")
        return token

    summing = {}

    def emit_finish(n, after):
        gsm, other = _pair_wait(*pair_flight.pop(n), after, f"{n}_pair_wait")
        if n in PAIR_SUM_ON_SPARSECORE:
            summing[n] = _pair_add_sc(gsm, other, f"{n}_pair_add_sc")
            return None
        *in_flight[n], token = _chip_start(_pair_add(gsm, other, f"{n}_pair_add"), f"{n}_chip_start")
        return token

    def emit_send(n):
        *in_flight[n], token = _chip_start(summing.pop(n), f"{n}_chip_start")
        return token

    grads = {}
    updates = {}

    def reduce_done(n, after):
        part, slots = _chip_wait(*in_flight.pop(n), after, f"{n}_chip_wait")
        grads[n], token = _pair_join(_chip_sum(part, slots, f"{n}_chip_sum"), f"{n}_pair_join")
        return token

    def early_update(after):
        tokens = [reduce_done(n, after) for n in ON_SPARSECORE]
        for n in ON_SPARSECORE:
            g2 = grads[n]
            updates[n] = _adamw_sc(wl[n].reshape(g2.shape), g2, ml[n].reshape(g2.shape), vl[n].reshape(g2.shape),
                                   f"adamw_sc_{n}")
        return tokens

    loss, dx0, g = _local_step(x[0], mem[0], loss_target[0], vec, weight,
                               (emit_begin, emit_finish, emit_send, early_update), dep0)
    assert not summing

    for n in list(in_flight):
        reduce_done(n, dx0)
    widths = [wl[n].size for n in REPLICATED]
    total = sum(widths)
    rows = -(-total // SMALL_COLS)
    rows = -(-rows // 8) * 8
    packed = jnp.concatenate([g[n].reshape(-1) for n in REPLICATED] + [jnp.zeros((rows * SMALL_COLS - total,), F32)])
    summed = _all_reduce_small(packed.reshape(rows, SMALL_COLS), "small_all_reduce").reshape(-1)
    off = 0
    for n, width in zip(REPLICATED, widths):
        grads[n] = summed[off:off + width].reshape(1, width)
        off += width

    out_g, out_d, out_m, out_v = [], [], [], []
    for n in WEIGHTS:
        shape = wl[n].shape
        g2 = grads[n]
        if n in updates:
            go, d, nm, nv = updates[n]
        else:
            dep = tuple(updates[k][1][:8, :LANES] for k in updates) if n == "w_in" else ()
            go, d, nm, nv = _adamw(wl[n].reshape(g2.shape), g2, ml[n].reshape(g2.shape), vl[n].reshape(g2.shape),
                                   f"adamw_{n}", dep)
        out_g.append(go.reshape(shape))
        out_d.append(d.reshape(shape))
        out_m.append(nm.reshape(shape))
        out_v.append(nv.reshape(shape))
    return (loss, dx0.reshape(x.shape), *out_g, *out_d, *out_m, *out_v)
```

```python
import functools

import jax
import jax.numpy as jnp
from jax import lax
from jax.experimental import pallas as pl
from jax.experimental.pallas import tpu as pltpu
from jax.experimental.pallas import tpu_sc as plsc

F32 = jnp.float32
BF16 = jnp.bfloat16
MESH = pl.DeviceIdType.MESH

RMS_EPS = 1e-6
CHUNK = 64
POOL_WINDOWS = (2, 4, 8, 16)
POOL_HALO = 16
N_HEADS = 4
GATE_TEMP = 16.0
ADAM_LR, ADAM_B1, ADAM_B2, ADAM_EPS, ADAM_WD, ADAM_STEP = 0.001, 0.9, 0.999, 1e-08, 0.01, 10
N_SHARDS = 4
LANES = 128
MXU_COLS = 256
TOKENS_PER_STEP = 2048
VMEM_LIMIT = 58 * 1024 * 1024

ANY = pl.BlockSpec(memory_space=pl.ANY)
HBM = pl.BlockSpec(memory_space=pltpu.HBM)
SEM = pl.BlockSpec(memory_space=pltpu.SEMAPHORE)
EFFECT = pltpu.SideEffectType.DATAFLOW_SIDE_EFFECTING


def _params(**kw):
    return pltpu.CompilerParams(vmem_limit_bytes=VMEM_LIMIT, **kw)


def _tile(n, want):
    for unit in (LANES, 8):
        t = (min(want, n) // unit) * unit
        while t >= unit:
            if n % t == 0:
                return t
            t -= unit
    return n


def _dot(a, b, dims):
    return lax.dot_general(a, b, (dims, ((), ())), preferred_element_type=F32)


def _nn(a, b):
    return _dot(a, b, ((1,), (0,)))


def _nt(a, b):
    return _dot(a, b, ((1,), (1,)))


def _tn(a, b):
    return _dot(a, b, ((0,), (0,)))


def _sigmoid(x):
    return 1.0 / (1.0 + jnp.exp(-x))


def _matmul(a, b, *, mode, name, out_dtype, tm=512, tn=2048, tk=2048, res=None, scale=1.0, b_groups=False, out_groups=0,
            dep=()):
    if mode == "tn":
        K, M = a.shape
    else:
        M, K = a.shape
    if mode == "nn":
        if b_groups:
            G, _, Nj = b.shape
            N = G * Nj
        else:
            N = b.shape[1]
    elif mode == "nt":
        if b_groups:
            G, N, Kj = b.shape
            assert G * Kj == K
        else:
            N = b.shape[0]
    else:
        N = b.shape[1]
    tm = _tile(M, tm)
    if mode == "nn" and b_groups:
        tn = _tile(Nj, tn)
    elif out_groups:
        tn = _tile(N // out_groups, tn)
    else:
        tn = _tile(N, tn)
    if mode == "nt" and b_groups:
        tk = _tile(Kj, tk)
    else:
        tk = _tile(K, tk)
    nk = K // tk
    grid = (M // tm, N // tn, nk)

    if mode == "tn":
        a_spec = pl.BlockSpec((tk, tm), lambda i, j, k: (k, i))
        b_spec = pl.BlockSpec((tk, tn), lambda i, j, k: (k, j))
        dims = ((0,), (0,))
    elif mode == "nn":
        a_spec = pl.BlockSpec((tm, tk), lambda i, j, k: (i, k))
        if b_groups:
            npj = Nj // tn
            b_spec = pl.BlockSpec((None, tk, tn), lambda i, j, k: (j // npj, k, j % npj))
        else:
            b_spec = pl.BlockSpec((tk, tn), lambda i, j, k: (k, j))
        dims = ((1,), (0,))
    else:
        a_spec = pl.BlockSpec((tm, tk), lambda i, j, k: (i, k))
        if b_groups:
            kpj = Kj // tk
            b_spec = pl.BlockSpec((None, tn, tk), lambda i, j, k: (k // kpj, j, k % kpj))
        else:
            b_spec = pl.BlockSpec((tn, tk), lambda i, j, k: (j, k))
        dims = ((1,), (1,))
    if out_groups:
        npj = (N // out_groups) // tn
        o_spec = pl.BlockSpec((None, tm, tn), lambda i, j, k: (j // npj, i, j % npj))
        out_shape = jax.ShapeDtypeStruct((out_groups, M, N // out_groups), out_dtype)
    else:
        o_spec = pl.BlockSpec((tm, tn), lambda i, j, k: (i, j))
        out_shape = jax.ShapeDtypeStruct((M, N), out_dtype)
    in_specs = [a_spec, b_spec]
    operands = [a, b]
    if res is not None:
        in_specs.append(pl.BlockSpec((tm, tn), lambda i, j, k: (i, j)))
        operands.append(res)
    has_res = res is not None
    n_dep = len(dep)
    for d in dep:
        in_specs.append(pl.BlockSpec(d.shape, lambda i, j, k: (0, 0)))
        operands.append(d)

    def body(*refs):
        if has_res:
            a_ref, b_ref, r_ref = refs[:3]
        else:
            a_ref, b_ref = refs[:2]
            r_ref = None
        o_ref = refs[2 + has_res + n_dep]

        def finish(acc):
            if scale != 1.0:
                acc = acc * scale
            if r_ref is not None:
                acc = r_ref[...] + acc
            o_ref[...] = acc.astype(o_ref.dtype)

        part = _dot(a_ref[...], b_ref[...], dims)
        if nk == 1:
            finish(part)
        else:
            acc_ref = o_ref if in_place else refs[-1]
            k = pl.program_id(2)

            @pl.when(k == 0)
            def _():
                acc_ref[...] = part

            @pl.when(k > 0)
            def _():
                acc_ref[...] += part

            if not in_place:
                @pl.when(k == nk - 1)
                def _():
                    finish(acc_ref[...])

    in_place = out_dtype == F32 and res is None and scale == 1.0
    scratch = [] if nk == 1 or in_place else [pltpu.VMEM((tm, tn), F32)]
    return pl.pallas_call(body, name=name, grid=grid, in_specs=in_specs, out_specs=o_spec, out_shape=out_shape,
                          scratch_shapes=scratch, compiler_params=_params())(*operands)


def _rms_fwd(x, gain, name, tm=512, dep=None):
    S, D = x.shape
    tm = _tile(S, tm)

    def body(x_ref, g_ref, *rest):
        o_ref = rest[-1]
        xv = x_ref[...]
        r = lax.rsqrt(jnp.mean(xv * xv, axis=-1, keepdims=True) + RMS_EPS)
        o_ref[...] = (xv * r * g_ref[...]).astype(o_ref.dtype)

    in_specs = [pl.BlockSpec((tm, D), lambda i: (i, 0)), pl.BlockSpec((1, D), lambda i: (0, 0))]
    operands = [x, gain]
    if dep is not None:
        in_specs.append(pl.BlockSpec(dep.shape, lambda i: (0, 0)))
        operands.append(dep)
    return pl.pallas_call(body, name=name, grid=(S // tm,), in_specs=in_specs,
                          out_specs=pl.BlockSpec((tm, D), lambda i: (i, 0)),
                          out_shape=jax.ShapeDtypeStruct((S, D), BF16), compiler_params=_params())(*operands)


def _rms_bwd(x, gain, dh, dres, name, lowp=None, tm=512):
    half = lowp is not None
    S, D = x.shape
    tm = _tile(S, tm)
    has_res = dres is not None

    def body(*refs):
        if has_res:
            x_ref, g_ref, dh_ref, dr_ref = refs[:4]
            outs = refs[4:]
        else:
            x_ref, g_ref, dh_ref = refs[:3]
            dr_ref = None
            outs = refs[3:]
        dx_ref, dg_ref = outs[0], outs[-1]
        xv = x_ref[...]
        dhv = dh_ref[...].astype(F32)
        r = lax.rsqrt(jnp.mean(xv * xv, axis=-1, keepdims=True) + RMS_EPS)
        gy = dhv * g_ref[...]
        dx = r * gy - xv * (r * r * r) * jnp.mean(gy * xv, axis=-1, keepdims=True)
        if dr_ref is not None:
            dx = dx + dr_ref[...]
        dx_ref[...] = dx
        if half:
            outs[1][...] = (dx if lowp == 1.0 else lowp * dx).astype(BF16)
        part = jnp.sum(dhv * xv * r, axis=0, keepdims=True)

        @pl.when(pl.program_id(0) == 0)
        def _():
            dg_ref[...] = part

        @pl.when(pl.program_id(0) > 0)
        def _():
            dg_ref[...] += part

    row = pl.BlockSpec((tm, D), lambda i: (i, 0))
    vec = pl.BlockSpec((1, D), lambda i: (0, 0))
    in_specs = [row, vec, row] + ([row] if has_res else [])
    operands = [x, gain, dh] + ([dres] if has_res else [])
    out_specs = [row] + ([row] if half else []) + [vec]
    out_shape = [jax.ShapeDtypeStruct((S, D), F32)] + ([jax.ShapeDtypeStruct((S, D), BF16)] if half else []) + [
        jax.ShapeDtypeStruct((1, D), F32)]
    return pl.pallas_call(body, name=name, grid=(S // tm,), in_specs=in_specs, out_specs=out_specs, out_shape=out_shape,
                          compiler_params=_params())(*operands)


def _loss_head(x, gain, target, name, tm=512):
    S, D = x.shape
    tm = _tile(S, tm)

    def body(x_ref, g_ref, t_ref, sq_ref, dx_ref, dxh_ref, dg_ref):
        xv = x_ref[...]
        r = lax.rsqrt(jnp.mean(xv * xv, axis=-1, keepdims=True) + RMS_EPS)
        xn = xv * r
        err = xn * g_ref[...] - t_ref[...]
        dout = err * (1.0 / D)
        gy = dout * g_ref[...]
        dx = r * gy - xv * (r * r * r) * jnp.mean(gy * xv, axis=-1, keepdims=True)
        dx_ref[...] = dx
        dxh_ref[...] = (0.5 * dx).astype(BF16)
        sq = jnp.sum(err * err, axis=0, keepdims=True)
        dg = jnp.sum(dout * xn, axis=0, keepdims=True)

        @pl.when(pl.program_id(0) == 0)
        def _():
            sq_ref[...] = sq
            dg_ref[...] = dg

        @pl.when(pl.program_id(0) > 0)
        def _():
            sq_ref[...] += sq
            dg_ref[...] += dg

    row = pl.BlockSpec((tm, D), lambda i: (i, 0))
    vec = pl.BlockSpec((1, D), lambda i: (0, 0))
    return pl.pallas_call(body, name=name, grid=(S // tm,), in_specs=[row, vec, row], out_specs=[vec, row, row, vec],
                          out_shape=[jax.ShapeDtypeStruct((1, D), F32), jax.ShapeDtypeStruct((S, D), F32),
                                     jax.ShapeDtypeStruct((S, D), BF16), jax.ShapeDtypeStruct((1, D), F32)],
                          compiler_params=_params())(x, gain, target)


def _cast(x, dtype, name, scale=1.0, tm=256):
    S, D = x.shape
    tm = _tile(S, tm)

    def body(x_ref, o_ref):
        o_ref[...] = (x_ref[...] * scale).astype(o_ref.dtype)

    row = pl.BlockSpec((tm, D), lambda i: (i, 0))
    return pl.pallas_call(body, name=name, grid=(S // tm,), in_specs=[row], out_specs=row,
                          out_shape=jax.ShapeDtypeStruct((S, D), dtype), compiler_params=_params())(x)


def _ffn_up(h, wg, wu, name, tm=512):
    S, D = h.shape
    G, _, Fj = wg.shape
    tm = _tile(S, tm)

    def body(h_ref, wg_ref, wu_ref, ga_ref, gb_ref, hid_ref):
        hv = h_ref[...]
        a = _nn(hv, wg_ref[...])
        b = _nn(hv, wu_ref[...])
        s = _sigmoid(a)
        silu = a * s
        ga_ref[...] = (b * (s * (1.0 + a * (1.0 - s)))).astype(BF16)
        gb_ref[...] = silu.astype(BF16)
        hid_ref[...] = (silu * b).astype(BF16)

    w_spec = pl.BlockSpec((None, D, Fj), lambda g, i: (g, 0, 0))
    o_spec = pl.BlockSpec((tm, Fj), lambda g, i: (i, g))
    out = jax.ShapeDtypeStruct((S, G * Fj), BF16)
    return pl.pallas_call(body, name=name, grid=(G, S // tm),
                          in_specs=[pl.BlockSpec((tm, D), lambda g, i: (i, 0)), w_spec, w_spec],
                          out_specs=[o_spec, o_spec, o_spec], out_shape=[out, out, out], compiler_params=_params())(h, wg, wu)


def _ffn_dact(dxh, wd, ga, gb, name, tm=512):
    S, D = dxh.shape
    G, Fj, _ = wd.shape
    tm = _tile(S, tm)

    def body(dx_ref, wd_ref, ga_ref, gb_ref, da_ref, db_ref):
        dhid = _nt(dx_ref[...], wd_ref[...])
        da_ref[...] = (dhid * ga_ref[...].astype(F32)).astype(BF16)
        db_ref[...] = (dhid * gb_ref[...].astype(F32)).astype(BF16)

    blk = pl.BlockSpec((tm, Fj), lambda g, i: (i, g))
    out = jax.ShapeDtypeStruct((S, G * Fj), BF16)
    return pl.pallas_call(body, name=name, grid=(G, S // tm),
                          in_specs=[pl.BlockSpec((tm, D), lambda g, i: (i, 0)),
                                    pl.BlockSpec((None, Fj, D), lambda g, i: (g, 0, 0)), blk, blk],
                          out_specs=[blk, blk], out_shape=[out, out], compiler_params=_params())(dxh, wd, ga, gb)


def _ffn_dh(da, db, wg, wu, name, dep=(), tm=512):
    S = da.shape[0]
    G, D, Fj = wg.shape
    tm = _tile(S, tm)

    def body(da_ref, db_ref, wg_ref, wu_ref, *rest):
        o_ref = rest[-1]
        part = _nt(da_ref[...], wg_ref[...]) + _nt(db_ref[...], wu_ref[...])

        @pl.when(pl.program_id(1) == 0)
        def _():
            o_ref[...] = part

        @pl.when(pl.program_id(1) > 0)
        def _():
            o_ref[...] += part

    act = pl.BlockSpec((tm, Fj), lambda i, g: (i, g))
    w_spec = pl.BlockSpec((None, D, Fj), lambda i, g: (g, 0, 0))
    in_specs = [act, act, w_spec, w_spec] + [pl.BlockSpec(d.shape, lambda i, g: (0, 0)) for d in dep]
    return pl.pallas_call(body, name=name, grid=(S // tm, G), in_specs=in_specs,
                          out_specs=pl.BlockSpec((tm, D), lambda i, g: (i, 0)),
                          out_shape=jax.ShapeDtypeStruct((S, D), F32), compiler_params=_params())(da, db, wg, wu, *dep)


def _pool_fwd(proj, pool_w, pool_scale, name, tm=512):
    S = proj.shape[0]
    NG, C, _ = pool_w.shape
    DP = NG * C
    tm = _tile(S, tm)
    hb = tm // POOL_HALO
    n_ext = tm + POOL_HALO

    def body(u_ref, halo_ref, w_ref, sc_ref, y_ref, d_ref):
        i = pl.program_id(0)
        t = lax.broadcasted_iota(jnp.int32, (tm, 1), 0) + i * tm
        for g, win in enumerate(POOL_WINDOWS):
            cols = slice(g * C, (g + 1) * C)
            ug = u_ref[:, cols]
            halo = jnp.where(i > 0, halo_ref[:, cols], 0.0)
            acc = jnp.concatenate([halo, ug], axis=0)
            step = 1
            while step < win:
                acc = acc + pltpu.roll(acc, step, 0)
                step *= 2
            count = jnp.minimum(t + 1, win).astype(F32)
            d = (acc[POOL_HALO:, :] / count - ug).astype(BF16)
            d_ref[:, cols] = d
            y_ref[:, cols] = (_nn(d, w_ref[g]) * sc_ref[:, cols]).astype(BF16)

    del n_ext
    return pl.pallas_call(
        body, name=name, grid=(S // tm,),
        in_specs=[pl.BlockSpec((tm, DP), lambda i: (i, 0)),
                  pl.BlockSpec((POOL_HALO, DP), lambda i: (jnp.maximum(i * hb - 1, 0), 0)),
                  pl.BlockSpec((NG, C, C), lambda i: (0, 0, 0)), pl.BlockSpec((1, DP), lambda i: (0, 0))],
        out_specs=[pl.BlockSpec((tm, DP), lambda i: (i, 0)), pl.BlockSpec((tm, DP), lambda i: (i, 0))],
        out_shape=[jax.ShapeDtypeStruct((S, DP), BF16), jax.ShapeDtypeStruct((S, DP), BF16)],
        compiler_params=_params())(proj, proj, pool_w, pool_scale)


def _pool_bwd(dymix, d, pool_w, pool_scale, name, tm=512):
    S = dymix.shape[0]
    NG, C, _ = pool_w.shape
    DP = NG * C
    tm = _tile(S, tm)
    hb = tm // POOL_HALO
    nb = S // tm
    n_ext = tm + POOL_HALO
    last_halo = S // POOL_HALO - 1

    def body(dy_ref, halo_ref, d_ref, w_ref, sc_ref, du_ref, dw_ref, dsc_ref):
        i = pl.program_id(0)
        t = lax.broadcasted_iota(jnp.int32, (n_ext, 1), 0) + i * tm
        for g, win in enumerate(POOL_WINDOWS):
            cols = slice(g * C, (g + 1) * C)
            dy = dy_ref[:, cols]
            halo = jnp.where(i < nb - 1, halo_ref[:, cols], 0.0)
            sc = sc_ref[:, cols]
            dv = d_ref[:, cols]
            e_ext = (jnp.concatenate([dy, halo], axis=0) * sc).astype(BF16)
            dd = _nt(e_ext, w_ref[g])
            count = jnp.minimum(t + 1, win).astype(F32)
            acc = dd / count
            step = 1
            while step < win:
                acc = acc + pltpu.roll(acc, n_ext - step, 0)
                step *= 2
            du_ref[:, cols] = (acc[:tm, :] - dd[:tm, :]).astype(BF16)
            dw = _tn(dv, e_ext[:tm, :])
            dsc = jnp.sum(dy * _nn(dv, w_ref[g]), axis=0, keepdims=True)

            @pl.when(i == 0)
            def _():
                dw_ref[g] = dw
                dsc_ref[:, cols] = dsc

            @pl.when(i > 0)
            def _():
                dw_ref[g] += dw
                dsc_ref[:, cols] += dsc

    return pl.pallas_call(
        body, name=name, grid=(nb,),
        in_specs=[pl.BlockSpec((tm, DP), lambda i: (i, 0)),
                  pl.BlockSpec((POOL_HALO, DP), lambda i: (jnp.minimum((i + 1) * hb, last_halo), 0)),
                  pl.BlockSpec((tm, DP), lambda i: (i, 0)),
                  pl.BlockSpec((NG, C, C), lambda i: (0, 0, 0)), pl.BlockSpec((1, DP), lambda i: (0, 0))],
        out_specs=[pl.BlockSpec((tm, DP), lambda i: (i, 0)), pl.BlockSpec((NG, C, C), lambda i: (0, 0, 0)),
                   pl.BlockSpec((1, DP), lambda i: (0, 0))],
        out_shape=[jax.ShapeDtypeStruct((S, DP), BF16), jax.ShapeDtypeStruct((NG, C, C), F32),
                   jax.ShapeDtypeStruct((1, DP), F32)],
        compiler_params=_params())(dymix, dymix, d, pool_w, pool_scale)


def _chunk_scan(v, rows, reverse):
    n = v.shape[0]
    step = 1
    while step < CHUNK:
        if reverse:
            v = v + jnp.where(rows < CHUNK - step, pltpu.roll(v, n - step, 0), 0.0)
        else:
            v = v + jnp.where(rows >= step, pltpu.roll(v, step, 0), 0.0)
        step *= 2
    return v


def _log_decay(alr, w_a2, b_a):
    z = _nn(alr.astype(BF16), w_a2) + b_a
    la = (jnp.minimum(z, 0.0) - jnp.log(1.0 + jnp.exp(-jnp.abs(z)))) * (1.0 / GATE_TEMP)
    return z, la


def _gla_specs(DP, DKT, DV, tb, bmap):
    return [pl.BlockSpec((tb, DKT), lambda i: (bmap(i), DP // DKT)),
            pl.BlockSpec((tb, DKT), lambda i: (bmap(i), DP // DKT + 1)),
            pl.BlockSpec((tb, DV), lambda i: (bmap(i), (DP + 2 * DKT) // DV)),
            pl.BlockSpec((tb, DV), lambda i: (bmap(i), (DP + 2 * DKT) // DV + 1)),
            pl.BlockSpec((tb, LANES), lambda i: (bmap(i), (DP + 2 * DKT + 2 * DV) // LANES))]


def _gla_fwd(proj, y_pool, w_a2, b_a, head_norm, name, tb=512):
    S = proj.shape[0]
    DP = y_pool.shape[1]
    DKT = b_a.shape[1]
    DV = head_norm.shape[1]
    dk, dv = DKT // N_HEADS, DV // N_HEADS
    tb = _tile(S, tb)
    ncb = tb // CHUNK
    qscale = dk ** -0.5

    def body(q_ref, k_ref, v_ref, g_ref, alr_ref, yp_ref, wa_ref, ba_ref, hn_ref, y_ref, st_out_ref, st_ref, kdec_ref,
             gam_ref):
        @pl.when(pl.program_id(0) == 0)
        def _():
            st_ref[...] = jnp.zeros_like(st_ref)

        y_ref[:, :DP] = yp_ref[...]

        rows = lax.broadcasted_iota(jnp.int32, (tb, 1), 0) % CHUNK
        _, la = _log_decay(alr_ref[...], wa_ref[...], ba_ref[...])
        tail = _chunk_scan(la, rows, True)
        kdec_ref[...] = k_ref[...] * jnp.exp(tail - la)
        gam_ref[...] = jnp.exp(tail)

        def chunk(c, carry):
            r0 = pl.multiple_of(c * CHUNK, CHUNK)
            rs = pl.ds(r0, CHUNK)
            gam = gam_ref[pl.ds(r0, 1), :]
            heads = range(N_HEADS)
            kcs = [slice(h * dk, (h + 1) * dk) for h in heads]
            vcs = [slice(h * dv, (h + 1) * dv) for h in heads]
            upd = [_tn(v_ref[rs, vcs[h]].astype(BF16), kdec_ref[rs, kcs[h]].astype(BF16)) for h in heads]
            st = [st_ref[h] * gam[:, kcs[h]] + upd[h] for h in heads]
            o = [_nt((q_ref[rs, kcs[h]] * qscale).astype(BF16), st[h].astype(BF16)) for h in heads]
            for h in heads:
                st_ref[h] = st[h]
                st_out_ref[c, h] = st[h]
                r = lax.rsqrt(jnp.mean(o[h] * o[h], axis=-1, keepdims=True) + RMS_EPS)
                gv = g_ref[rs, vcs[h]]
                y_ref[rs, DP + h * dv:DP + (h + 1) * dv] = (o[h] * r * hn_ref[:, vcs[h]] * (gv * _sigmoid(gv))).astype(BF16)
            return carry

        lax.fori_loop(0, ncb, chunk, 0, unroll=2)

    full = lambda shape: pl.BlockSpec(shape, lambda i: (0,) * len(shape))
    return pl.pallas_call(
        body, name=name, grid=(S // tb,),
        in_specs=_gla_specs(DP, DKT, DV, tb, lambda i: i) + [pl.BlockSpec((tb, DP), lambda i: (i, 0)),
                                                            full((LANES, DKT)), full((1, DKT)), full((1, DV))],
        out_specs=[pl.BlockSpec((tb, DP + DV), lambda i: (i, 0)),
                   pl.BlockSpec((ncb, N_HEADS, dv, dk), lambda i: (i, 0, 0, 0))],
        out_shape=[jax.ShapeDtypeStruct((S, DP + DV), BF16), jax.ShapeDtypeStruct((S // CHUNK, N_HEADS, dv, dk), F32)],
        scratch_shapes=[pltpu.VMEM((N_HEADS, dv, dk), F32), pltpu.VMEM((tb, DKT), F32), pltpu.VMEM((tb, DKT), F32)],
        compiler_params=_params())(proj, proj, proj, proj, proj, y_pool, w_a2, b_a, head_norm)


def _gla_bwd(proj, states, dymix, du, w_a2, b_a, head_norm, name, tb=512):
    S = proj.shape[0]
    DP = du.shape[1]
    DKT = b_a.shape[1]
    DV = head_norm.shape[1]
    dk, dv = DKT // N_HEADS, DV // N_HEADS
    tb = _tile(S, tb)
    ncb = tb // CHUNK
    nb = S // tb
    qscale = dk ** -0.5
    rev = lambda i: nb - 1 - i

    q0, k0, v0, g0, a0 = DP, DP + DKT, DP + 2 * DKT, DP + 2 * DKT + DV, DP + 2 * DKT + 2 * DV

    def body(q_ref, k_ref, v_ref, g_ref, alr_ref, st_blk_ref, st_prev_ref, dy_ref, du_ref, wa_ref, ba_ref, hn_ref,
             dp_ref, dwa_ref, dba_ref, dhn_ref,
             dst_ref, kdec_ref, dec_ref, gam_ref, e_ref, dla_ref, dhn_acc_ref):
        i = pl.program_id(0)
        blk = rev(i)
        dp_ref[:, :DP] = du_ref[...]

        @pl.when(i == 0)
        def _():
            dst_ref[...] = jnp.zeros_like(dst_ref)

        dhn_acc_ref[...] = jnp.zeros_like(dhn_acc_ref)
        rows = lax.broadcasted_iota(jnp.int32, (tb, 1), 0) % CHUNK
        z, la = _log_decay(alr_ref[...], wa_ref[...], ba_ref[...])
        tail = _chunk_scan(la, rows, True)
        dec_ref[...] = jnp.exp(tail - la)
        kdec_ref[...] = k_ref[...] * dec_ref[...]
        gam_ref[...] = jnp.exp(tail)

        def chunk(cc, carry):
            c = ncb - 1 - cc
            r0 = pl.multiple_of(c * CHUNK, CHUNK)
            rs = pl.ds(r0, CHUNK)
            gam = gam_ref[pl.ds(r0, 1), :]
            first = jnp.logical_and(blk == 0, c == 0)
            heads = range(N_HEADS)
            kcs = [slice(h * dk, (h + 1) * dk) for h in heads]
            vcs = [slice(h * dv, (h + 1) * dv) for h in heads]
            qs = [(q_ref[rs, kcs[h]] * qscale).astype(BF16) for h in heads]
            stb = [st_blk_ref[c, h].astype(BF16) for h in heads]
            o = [_nt(qs[h], stb[h]) for h in heads]
            do = []
            for h in heads:
                oh = o[h]
                r = lax.rsqrt(jnp.mean(oh * oh, axis=-1, keepdims=True) + RMS_EPS)
                gv = g_ref[rs, vcs[h]]
                sg = _sigmoid(gv)
                dy = dy_ref[rs, vcs[h]]
                hn = hn_ref[:, vcs[h]]
                on = oh * r
                dp_ref[rs, g0 + h * dv:g0 + (h + 1) * dv] = (dy * on * hn * (sg * (1.0 + gv * (1.0 - sg)))).astype(BF16)
                don = dy * (gv * sg)
                dhn_acc_ref[:, vcs[h]] += jnp.sum(don * on, axis=0, keepdims=True)
                dn = don * hn
                do.append((r * dn - oh * (r * r * r) * jnp.mean(dn * oh, axis=-1, keepdims=True)).astype(BF16))
            dqs = [_nn(do[h], stb[h]) for h in heads]
            dst = [dst_ref[h] + _tn(do[h], qs[h]) for h in heads]
            for h in heads:
                dp_ref[rs, q0 + h * dk:q0 + (h + 1) * dk] = (dqs[h] * qscale).astype(BF16)
            dstb = [dst[h].astype(BF16) for h in heads]
            dvh = [_nt(kdec_ref[rs, kcs[h]].astype(BF16), dstb[h]) for h in heads]
            dkdec = [_nn(v_ref[rs, vcs[h]].astype(BF16), dstb[h]) for h in heads]
            gdg = []
            for h in heads:
                dp_ref[rs, v0 + h * dv:v0 + (h + 1) * dv] = dvh[h].astype(BF16)
                dp_ref[rs, k0 + h * dk:k0 + (h + 1) * dk] = (dkdec[h] * dec_ref[rs, kcs[h]]).astype(BF16)
                e_ref[rs, kcs[h]] = dkdec[h] * kdec_ref[rs, kcs[h]]
                st_prev = jnp.where(c > 0, st_blk_ref[jnp.maximum(c - 1, 0), h], st_prev_ref[0, h])
                st_prev = jnp.where(first, 0.0, st_prev)
                gdg.append(jnp.sum(dst[h] * st_prev, axis=0, keepdims=True) * gam[:, kcs[h]])
                dst_ref[h] = dst[h] * gam[:, kcs[h]]
            dla_ref[rs, :] = jnp.broadcast_to(jnp.concatenate(gdg, axis=1), (CHUNK, DKT))
            return carry

        lax.fori_loop(0, ncb, chunk, 0, unroll=2)

        ev = e_ref[...]
        dla = dla_ref[...] + _chunk_scan(ev, rows, False) - ev
        dz = dla * (1.0 / GATE_TEMP) * (1.0 - _sigmoid(z))
        dzb = dz.astype(BF16)
        dp_ref[:, a0:a0 + LANES] = _nt(dzb, wa_ref[...]).astype(BF16)
        dwa = _tn(alr_ref[...].astype(BF16), dzb)
        dba = jnp.sum(dz, axis=0, keepdims=True)

        @pl.when(i == 0)
        def _():
            dwa_ref[...] = dwa
            dba_ref[...] = dba
            dhn_ref[...] = dhn_acc_ref[...]

        @pl.when(i > 0)
        def _():
            dwa_ref[...] += dwa
            dba_ref[...] += dba
            dhn_ref[...] += dhn_acc_ref[...]

    full = lambda shape: pl.BlockSpec(shape, lambda i: (0,) * len(shape))
    rowblk = lambda w: pl.BlockSpec((tb, w), lambda i: (rev(i), 0))
    return pl.pallas_call(
        body, name=name, grid=(nb,),
        in_specs=_gla_specs(DP, DKT, DV, tb, rev) + [
            pl.BlockSpec((ncb, N_HEADS, dv, dk), lambda i: (rev(i), 0, 0, 0)),
            pl.BlockSpec((1, N_HEADS, dv, dk), lambda i: (jnp.maximum(rev(i) * ncb - 1, 0), 0, 0, 0)),
            pl.BlockSpec((tb, DV), lambda i: (rev(i), DP // DV)), rowblk(DP),
            full((LANES, DKT)), full((1, DKT)), full((1, DV))],
        out_specs=[rowblk(a0 + LANES), full((LANES, DKT)), full((1, DKT)), full((1, DV))],
        out_shape=[jax.ShapeDtypeStruct((S, a0 + LANES), BF16), jax.ShapeDtypeStruct((LANES, DKT), F32),
                   jax.ShapeDtypeStruct((1, DKT), F32), jax.ShapeDtypeStruct((1, DV), F32)],
        scratch_shapes=[pltpu.VMEM((N_HEADS, dv, dk), F32)] + [pltpu.VMEM((tb, DKT), F32)] * 5 + [pltpu.VMEM((1, DV), F32)],
        compiler_params=_params())(proj, proj, proj, proj, proj, states, states, dymix, du, w_a2, b_a, head_norm)


def _xattn_fwd(q, kv, name, tm=512):
    S, D = q.shape
    M = kv.shape[0]
    hd = D // N_HEADS
    tm = _tile(S, tm)
    scale = hd ** -0.5

    def body(q_ref, k_ref, v_ref, o_ref):
        heads = range(N_HEADS)
        hcs = [slice(h * hd, (h + 1) * hd) for h in heads]
        s = [_nt(q_ref[:, hc], k_ref[:, hc]) * scale for hc in hcs]
        p = []
        for h in heads:
            e = jnp.exp(s[h] - jnp.max(s[h], axis=-1, keepdims=True))
            p.append((e / jnp.sum(e, axis=-1, keepdims=True)).astype(BF16))
        o = [_nn(p[h], v_ref[:, hcs[h]]) for h in heads]
        for h in heads:
            o_ref[:, hcs[h]] = o[h].astype(BF16)

    return pl.pallas_call(body, name=name, grid=(S // tm,),
                          in_specs=[pl.BlockSpec((tm, D), lambda i: (i, 0)), pl.BlockSpec((M, D), lambda i: (0, 0)),
                                    pl.BlockSpec((M, D), lambda i: (0, 1))],
                          out_specs=pl.BlockSpec((tm, D), lambda i: (i, 0)),
                          out_shape=jax.ShapeDtypeStruct((S, D), BF16), compiler_params=_params())(q, kv, kv)


def _xattn_bwd(q, kv, do, name, tm=512):
    S, D = q.shape
    M = kv.shape[0]
    hd = D // N_HEADS
    tm = _tile(S, tm)
    scale = hd ** -0.5

    def body(q_ref, k_ref, v_ref, do_ref, dq_ref, dkv_ref):
        first = pl.program_id(0) == 0
        heads = range(N_HEADS)
        hcs = [slice(h * hd, (h + 1) * hd) for h in heads]
        s = [_nt(q_ref[:, hc], k_ref[:, hc]) * scale for hc in hcs]
        dp = [_nt(do_ref[:, hc], v_ref[:, hc]) for hc in hcs]
        p = []
        for h in heads:
            e = jnp.exp(s[h] - jnp.max(s[h], axis=-1, keepdims=True))
            p.append(e / jnp.sum(e, axis=-1, keepdims=True))
        dvh = [_tn(p[h].astype(BF16), do_ref[:, hcs[h]]) for h in heads]
        ds = [((p[h] * (dp[h] - jnp.sum(dp[h] * p[h], axis=-1, keepdims=True))) * scale).astype(BF16) for h in heads]
        dqh = [_nn(ds[h], k_ref[:, hcs[h]]) for h in heads]
        dkh = [_tn(ds[h], q_ref[:, hcs[h]]) for h in heads]
        for h in heads:
            dq_ref[:, hcs[h]] = dqh[h].astype(BF16)

        @pl.when(first)
        def _():
            for h in heads:
                dkv_ref[:, hcs[h]] = dkh[h]
                dkv_ref[:, D + h * hd:D + (h + 1) * hd] = dvh[h]

        @pl.when(jnp.logical_not(first))
        def _():
            for h in heads:
                dkv_ref[:, hcs[h]] += dkh[h]
                dkv_ref[:, D + h * hd:D + (h + 1) * hd] += dvh[h]

    row = pl.BlockSpec((tm, D), lambda i: (i, 0))
    return pl.pallas_call(body, name=name, grid=(S // tm,),
                          in_specs=[row, pl.BlockSpec((M, D), lambda i: (0, 0)), pl.BlockSpec((M, D), lambda i: (0, 1)), row],
                          out_specs=[row, pl.BlockSpec((M, 2 * D), lambda i: (0, 0))],
                          out_shape=[jax.ShapeDtypeStruct((S, D), BF16), jax.ShapeDtypeStruct((M, 2 * D), F32)],
                          compiler_params=_params())(q, kv, kv, do)


def _local_step(x, mem, target, vec, weight, emit, dep0):
    g = {}
    pending = []
    begun = []
    summed = []
    emit_begin, emit_finish, emit_send, early_update = emit

    def behind(fn, *a, **kw):
        dep = tuple(pending)
        pending.clear()
        out = fn(*a, dep=dep, **kw)
        while summed:
            pending.append(emit_send(summed.pop(0)))
        while begun:
            name = begun.pop(0)
            token = emit_finish(name, out)
            if token is None:
                summed.append(name)
            else:
                pending.append(token)
        return out

    def mm(a, b, **kw):
        return behind(_matmul, a, b, **kw)

    def send(name, gfull):
        pending.append(emit_begin(name, gfull))
        begun.append(name)

    def ffn_fwd(xin, tag, dep):
        h = _rms_fwd(xin, vec[f"{tag}_norm"], f"{tag}_norm", dep=dep)
        ga, gb, hid = _ffn_up(h, weight(f"{tag}_w_gate", h), weight(f"{tag}_w_up", h), f"{tag}_up")
        wd = weight(f"{tag}_w_down", hid)
        G, Fj, D = wd.shape
        xo = _matmul(hid, wd.reshape(G * Fj, D), mode="nn", name=f"{tag}_down", out_dtype=F32, res=xin, scale=0.5,
                     tn=1024, tk=G * Fj)
        return xo, (h, ga, gb, hid)

    def ffn_bwd(dxh, saved, tag, kept_back=None):
        h, ga, gb, hid = saved
        wg, wu, wd = weight(f"{tag}_w_gate"), weight(f"{tag}_w_up"), weight(f"{tag}_w_down")
        G, Fj, D = wd.shape
        send(f"{tag}_w_down", mm(hid, dxh, mode="tn", name=f"{tag}_dwd", out_dtype=F32, tm=Fj, tn=1024))
        da, db = _ffn_dact(dxh, wd, ga, gb, f"{tag}_dact")
        send(f"{tag}_w_gate", mm(h, da, mode="tn", name=f"{tag}_dwg", out_dtype=F32, tm=1024, tn=Fj, tk=TOKENS_PER_STEP,
                                  out_groups=G))
        send(f"{tag}_w_up", mm(h, db, mode="tn", name=f"{tag}_dwu", out_dtype=F32, tm=1024, tn=Fj, tk=TOKENS_PER_STEP,
                                out_groups=G))
        if kept_back is not None:
            name, gfull = kept_back()
            pending.append(emit_begin(name, gfull))
            pending.append(emit_finish(name, pending[-1]))
        return behind(_ffn_dh, da, db, wg, wu, f"{tag}_dh")

    x1, ffn1_saved = ffn_fwd(x, "ffn1", dep0)
    h2 = _rms_fwd(x1, vec["mix_norm"], "mix_norm")
    w_in = weight("w_in", h2)
    proj = _matmul(h2, w_in, mode="nn", name="w_in", out_dtype=F32, tn=1408)
    pool_w, w_a2 = weight("pool_w", h2), weight("gla_w_a2", h2)
    y_pool, dpool = _pool_fwd(proj, pool_w, vec["pool_scale"], "pool_fwd")
    ymix, states = _gla_fwd(proj, y_pool, w_a2, vec["gla_b_a"], vec["gla_head_norm"], "gla_fwd")
    w_out = weight("w_out", ymix)
    x2 = _matmul(ymix, w_out, mode="nn", name="w_out", out_dtype=F32, res=x1)
    h3 = _rms_fwd(x2, vec["xattn_norm"], "xattn_norm")
    mh = _rms_fwd(mem, vec["mem_norm"], "mem_norm")
    w_q = weight("xattn_w_q", h3)
    q = _matmul(h3, w_q, mode="nn", name="xattn_q", out_dtype=BF16)
    w_kv = weight("xattn_w_kv", q)
    kv = _matmul(mh, w_kv, mode="nn", name="xattn_kv", out_dtype=BF16, b_groups=True, tn=1024)
    o = _xattn_fwd(q, kv, "xattn_fwd")
    w_o = weight("xattn_w_o", o)
    x3 = _matmul(o, w_o, mode="nn", name="xattn_o", out_dtype=F32, res=x2)
    x4, ffn2_saved = ffn_fwd(x3, "ffn2", None)
    sq, dx4, dx4h, g["final_norm"] = _loss_head(x4, vec["final_norm"], target, "loss_head")
    loss = lax.psum(0.5 * jnp.sum(sq) / x.shape[-1], ("x", "y", "c"))
    pending.append(loss.reshape(1, 1))

    dh = ffn_bwd(dx4h, ffn2_saved, "ffn2")
    dx3, dx3b, g["ffn2_norm"] = _rms_bwd(x3, vec["ffn2_norm"], dh, dx4, "ffn2_norm_bwd", lowp=1.0)
    send("xattn_w_o", mm(o, dx3b, mode="tn", name="xattn_dwo", out_dtype=F32, tm=1024, tn=1024, tk=TOKENS_PER_STEP))
    do = mm(dx3b, w_o, mode="nt", name="xattn_do", out_dtype=BF16)
    dq, dkv = _xattn_bwd(q, kv, do, "xattn_bwd")
    send("xattn_w_q", mm(h3, dq, mode="tn", name="xattn_dwq", out_dtype=F32, tm=1024, tn=1024, tk=TOKENS_PER_STEP))
    dh3 = mm(dq, w_q, mode="nt", name="xattn_dh", out_dtype=F32)
    dkvb = _cast(dkv, BF16, "dkv_cast")
    send("xattn_w_kv", mm(mh, dkvb, mode="tn", name="xattn_dwkv", out_dtype=F32, tm=1024, tn=1024, out_groups=N_SHARDS))
    dmh = mm(dkvb, w_kv, mode="nt", name="xattn_dmh", out_dtype=F32, b_groups=True, tk=1024)
    _, g["mem_norm"] = _rms_bwd(mem, vec["mem_norm"], dmh, None, "mem_norm_bwd")
    pending.append(g["mem_norm"])
    dx2, dx2b, g["xattn_norm"] = _rms_bwd(x2, vec["xattn_norm"], dh3, dx3, "xattn_norm_bwd", lowp=1.0)
    send("w_out", mm(ymix, dx2b, mode="tn", name="dw_out", out_dtype=F32, tm=1024, tn=1024, tk=TOKENS_PER_STEP))
    dymix = mm(dx2b, w_out, mode="nt", name="dymix", out_dtype=F32)
    du, dpool_w, g["pool_scale"] = _pool_bwd(dymix, dpool, pool_w, vec["pool_scale"], "pool_bwd")
    send("pool_w", dpool_w)
    dproj, dw_a2, g["gla_b_a"], g["gla_head_norm"] = _gla_bwd(
        proj, states, dymix, du, w_a2, vec["gla_b_a"], vec["gla_head_norm"], "gla_bwd")
    send("gla_w_a2", dw_a2)
    dh2 = mm(dproj, w_in, mode="nt", name="dh2", out_dtype=F32, tn=1024, tk=dproj.shape[1])
    pending.extend(early_update(dh2))
    dx1, dx1h, g["mix_norm"] = _rms_bwd(x1, vec["mix_norm"], dh2, dx2, "mix_norm_bwd", lowp=0.5)
    dh = ffn_bwd(dx1h, ffn1_saved, "ffn1", kept_back=lambda: (
        "w_in", mm(h2, dproj, mode="tn", name="dw_in", out_dtype=F32, tm=1024, tn=1408, tk=TOKENS_PER_STEP)))
    dx0, g["ffn1_norm"] = _rms_bwd(x, vec["ffn1_norm"], dh, dx1, "ffn1_norm_bwd")
    return loss, dx0, g


def _place():
    x, y, c = lax.axis_index("x"), lax.axis_index("y"), lax.axis_index("c")
    chips = [(1 - x, y), (x, 1 - y), (1 - x, 1 - y)]
    return x, y, c, chips


def _ids():
    return jnp.stack([2 * lax.axis_index("x") + lax.axis_index("y"), lax.axis_index("c")]).astype(jnp.int32)


def _hbm(a):
    return pltpu.with_memory_space_constraint(a, pltpu.HBM)


def _cast_to_slot(w2d, dtype, name, dep=None):
    R, C = w2d.shape
    tr = _tile(R, max(16, (4 << 20) // (4 * C) // 16 * 16))

    def body(i_ref, w_ref, *rest):
        rest[-1][...] = w_ref[...].astype(dtype)

    in_specs = [pl.BlockSpec((tr, C), lambda r, i: (r, 0))]
    operands = [w2d]
    if dep is not None:
        in_specs.append(pl.BlockSpec(dep.shape, lambda r, i: (0, 0)))
        operands.append(dep)
    grid_spec = pltpu.PrefetchScalarGridSpec(num_scalar_prefetch=1, grid=(R // tr,), in_specs=in_specs,
                                             out_specs=pl.BlockSpec((None, tr, C), lambda r, i: (i[0], r, 0)))
    return pl.pallas_call(body, name=name, grid_spec=grid_spec, out_shape=jax.ShapeDtypeStruct((N_SHARDS, R, C), dtype),
                          compiler_params=_params())(_ids(), *operands)


def _gather_copies(buf_ref, send_sems, recv_sems, incoming, whole):
    x, y, c, chips = _place()
    hr = buf_ref.shape[1] // 2
    copies = []
    for j, (px, py) in enumerate(chips):
        slot = 2 * px + py if incoming else 2 * x + y
        part = buf_ref.at[slot] if whole else buf_ref.at[slot, pl.ds(c * hr, hr), :]
        copies.append(pltpu.make_async_remote_copy(src_ref=part, dst_ref=part, send_sem=send_sems.at[j],
                                                   recv_sem=recv_sems.at[j], device_id=(px, py, c), device_id_type=MESH))
    return copies


def _gather_start(buf, name, whole):
    def body(b_ref, send_sems, recv_sems, b_thru, token):
        for cp in _gather_copies(b_ref, send_sems, recv_sems, False, whole):
            cp.start()
        token[...] = jnp.zeros_like(token)

    return pl.pallas_call(
        body, name=name,
        out_shape=(pltpu.SemaphoreType.DMA((3,)), pltpu.SemaphoreType.DMA((3,)), pltpu.HBM(buf.shape, buf.dtype),
                   jax.ShapeDtypeStruct((8, LANES), F32)),
        in_specs=(HBM,), out_specs=(SEM, SEM, HBM, pl.BlockSpec(memory_space=pltpu.VMEM)), input_output_aliases={0: 2},
        compiler_params=pltpu.CompilerParams(has_side_effects=EFFECT))(_hbm(buf))


def _gather_wait(send_sems, recv_sems, buf_thru, after, name, whole):
    def body(b_ref, send_sems, recv_sems, after_ref, b_out):
        for cp in _gather_copies(b_ref, send_sems, recv_sems, False, whole):
            cp.wait_send()
        for cp in _gather_copies(b_ref, send_sems, recv_sems, True, whole):
            cp.wait_recv()

    return pl.pallas_call(
        body, name=name, out_shape=pltpu.HBM(buf_thru.shape, buf_thru.dtype),
        in_specs=(HBM, SEM, SEM, ANY), out_specs=HBM, input_output_aliases={0: 0},
        compiler_params=pltpu.CompilerParams(has_side_effects=EFFECT))(buf_thru, send_sems, recv_sems, after)


def _gather_forward(buf, name):
    G, R, C = buf.shape
    hr = R // 2

    def body(b_ref, o_ref, send_sems, recv_sems):
        x, y, c, chips = _place()
        copies = []
        for j, (px, py) in enumerate(chips):
            half = o_ref.at[2 * px + py, pl.ds(c * hr, hr), :]
            copies.append(pltpu.make_async_remote_copy(src_ref=half, dst_ref=half, send_sem=send_sems.at[j],
                                                       recv_sem=recv_sems.at[j], device_id=(x, y, 1 - c),
                                                       device_id_type=MESH))
        for cp in copies:
            cp.start()
        for j, (px, py) in enumerate(chips):
            half = o_ref.at[2 * px + py, pl.ds((1 - c) * hr, hr), :]
            pltpu.make_async_remote_copy(src_ref=half, dst_ref=half, send_sem=send_sems.at[j], recv_sem=recv_sems.at[j],
                                         device_id=(x, y, 1 - c), device_id_type=MESH).wait_recv()
        for cp in copies:
            cp.wait_send()

    return pl.pallas_call(body, name=name, in_specs=[ANY], out_specs=ANY, out_shape=jax.ShapeDtypeStruct(buf.shape, buf.dtype),
                          input_output_aliases={0: 0},
                          scratch_shapes=[pltpu.SemaphoreType.DMA((3,)), pltpu.SemaphoreType.DMA((3,))])(buf)


def _pair_copy(g_ref, land_ref, send_sem, recv_sem):
    x, y, c, _ = _place()
    hr = g_ref.shape[1] // 2
    return pltpu.make_async_remote_copy(src_ref=g_ref.at[:, pl.ds((1 - c) * hr, hr), :], dst_ref=land_ref,
                                        send_sem=send_sem, recv_sem=recv_sem, device_id=(x, y, 1 - c), device_id_type=MESH)


def _pair_start(gfull, name):
    G, R, C = gfull.shape

    def body(g_ref, land_ref, send_sem, recv_sem, g_thru, land_thru, token):
        _pair_copy(g_ref, land_ref, send_sem, recv_sem).start()
        token[...] = jnp.zeros_like(token)

    return pl.pallas_call(
        body, name=name,
        out_shape=(pltpu.SemaphoreType.DMA(()), pltpu.SemaphoreType.DMA(()), pltpu.HBM(gfull.shape, F32),
                   pltpu.HBM((G, R // 2, C), F32), jax.ShapeDtypeStruct((8, LANES), F32)),
        in_specs=(HBM, HBM), out_specs=(SEM, SEM, HBM, HBM, pl.BlockSpec(memory_space=pltpu.VMEM)),
        input_output_aliases={0: 2, 1: 3},
        compiler_params=pltpu.CompilerParams(has_side_effects=EFFECT))(_hbm(gfull), _hbm(lax.empty((G, R // 2, C), F32)))


def _pair_wait(send_sem, recv_sem, g_thru, land_thru, after, name):
    def body(g_ref, land_ref, send_sem, recv_sem, after_ref, g_out, land_out):
        cp = _pair_copy(g_ref, land_ref, send_sem, recv_sem)
        cp.wait_send()
        cp.wait_recv()

    return pl.pallas_call(
        body, name=name, out_shape=(pltpu.HBM(g_thru.shape, F32), pltpu.HBM(land_thru.shape, F32)),
        in_specs=(HBM, HBM, SEM, SEM, ANY), out_specs=(HBM, HBM), input_output_aliases={0: 0, 1: 1},
        compiler_params=pltpu.CompilerParams(has_side_effects=EFFECT))(g_thru, land_thru, send_sem, recv_sem, after)


def _pair_add(gfull, other, name):
    G, R, C = gfull.shape
    hr = R // 2
    tr = _tile(hr, max(8, (2 * 1024 * 1024) // (4 * C) // 8 * 8))
    nr = hr // tr
    c = lax.axis_index("c")
    cidx = jnp.reshape(c, (1,)).astype(jnp.int32)

    def body(c_ref, a_ref, b_ref, o_ref):
        o_ref[...] = a_ref[...] + b_ref[...]

    grid_spec = pltpu.PrefetchScalarGridSpec(
        num_scalar_prefetch=1, grid=(G, nr),
        in_specs=[pl.BlockSpec((None, tr, C), lambda g, r, cr: (g, cr[0] * nr + r, 0)),
                  pl.BlockSpec((None, tr, C), lambda g, r, cr: (g, r, 0))],
        out_specs=pl.BlockSpec((None, tr, C), lambda g, r, cr: (g, r, 0)))
    return pl.pallas_call(body, name=name, grid_spec=grid_spec, out_shape=jax.ShapeDtypeStruct((G, hr, C), F32),
                          compiler_params=_params())(cidx, gfull, other)


def _chip_copies(p_ref, land_ref, send_sems, recv_sems, incoming):
    x, y, c, chips = _place()
    me = 2 * x + y
    copies = []
    for j, (px, py) in enumerate(chips):
        dst = land_ref.at[2 * px + py] if incoming else land_ref.at[me]
        copies.append(pltpu.make_async_remote_copy(src_ref=p_ref.at[2 * px + py], dst_ref=dst, send_sem=send_sems.at[j],
                                                   recv_sem=recv_sems.at[j], device_id=(px, py, c), device_id_type=MESH))
    return copies


def _chip_start(part, name):
    def body(p_ref, land_ref, send_sems, recv_sems, p_thru, land_thru, token):
        for cp in _chip_copies(p_ref, land_ref, send_sems, recv_sems, False):
            cp.start()
        token[...] = jnp.zeros_like(token)

    return pl.pallas_call(
        body, name=name,
        out_shape=(pltpu.SemaphoreType.DMA((3,)), pltpu.SemaphoreType.DMA((3,)), pltpu.HBM(part.shape, F32),
                   pltpu.HBM(part.shape, F32), jax.ShapeDtypeStruct((8, LANES), F32)),
        in_specs=(HBM, HBM), out_specs=(SEM, SEM, HBM, HBM, pl.BlockSpec(memory_space=pltpu.VMEM)),
        input_output_aliases={0: 2, 1: 3},
        compiler_params=pltpu.CompilerParams(has_side_effects=EFFECT))(_hbm(part), _hbm(lax.empty(part.shape, F32)))


def _chip_wait(send_sems, recv_sems, p_thru, land_thru, after, name):
    def body(p_ref, land_ref, send_sems, recv_sems, after_ref, p_out, land_out):
        for cp in _chip_copies(p_ref, land_ref, send_sems, recv_sems, False):
            cp.wait_send()
        for cp in _chip_copies(p_ref, land_ref, send_sems, recv_sems, True):
            cp.wait_recv()

    return pl.pallas_call(
        body, name=name, out_shape=(pltpu.HBM(p_thru.shape, F32), pltpu.HBM(p_thru.shape, F32)),
        in_specs=(HBM, HBM, SEM, SEM, ANY), out_specs=(HBM, HBM), input_output_aliases={0: 0, 1: 1},
        compiler_params=pltpu.CompilerParams(has_side_effects=EFFECT))(p_thru, land_thru, send_sems, recv_sems, after)


def _chip_sum(part, slots, name):
    G, R2, C = part.shape
    tr = _tile(R2, max(8, (1 << 20) // (4 * C) // 8 * 8))
    nr = R2 // tr

    def body(i_ref, p_ref, *rest):
        o_ref = rest[-1]
        acc = None
        for u in range(G):
            val = jnp.where(i_ref[0] == u, p_ref[...], rest[u][...])
            acc = val if acc is None else acc + val
        o_ref[...] = acc

    def slot_spec(u):
        return pl.BlockSpec((None, tr, C), lambda r, i: (jnp.where(i[0] == u, (u + 1) % G, u), r, 0))

    grid_spec = pltpu.PrefetchScalarGridSpec(
        num_scalar_prefetch=1, grid=(nr,),
        in_specs=[pl.BlockSpec((None, tr, C), lambda r, i: (i[0], r, 0))] + [slot_spec(u) for u in range(G)],
        out_specs=pl.BlockSpec((tr, C), lambda r, i: (i[1] * nr + r, 0)))
    return pl.pallas_call(body, name=name, grid_spec=grid_spec, out_shape=jax.ShapeDtypeStruct((2 * R2, C), F32),
                          compiler_params=_params())(_ids(), part, slots, slots, slots, slots)


def _sum_slots(slots, name):
    G, R2, C = slots.shape
    tr = _tile(R2, max(8, (1024 * 1024) // (4 * C) // 8 * 8))

    def body(s_ref, o_ref):
        acc = s_ref[0]
        for u in range(1, G):
            acc = acc + s_ref[u]
        o_ref[...] = acc

    return pl.pallas_call(body, name=name, grid=(R2 // tr,), in_specs=[pl.BlockSpec((G, tr, C), lambda r: (0, r, 0))],
                          out_specs=pl.BlockSpec((tr, C), lambda r: (r, 0)), out_shape=jax.ShapeDtypeStruct((R2, C), F32),
                          compiler_params=_params())(slots)


def _pair_join(full, name):
    R, C = full.shape
    R2 = R // 2

    def body(f_ref, o_ref, token, send_sem, recv_sem):
        x, y, c, _ = _place()
        token[...] = jnp.zeros_like(token)
        mine = o_ref.at[pl.ds(c * R2, R2), :]
        theirs = o_ref.at[pl.ds((1 - c) * R2, R2), :]
        cp = pltpu.make_async_remote_copy(src_ref=mine, dst_ref=mine, send_sem=send_sem, recv_sem=recv_sem,
                                          device_id=(x, y, 1 - c), device_id_type=MESH)
        cp.start()
        pltpu.make_async_remote_copy(src_ref=theirs, dst_ref=theirs, send_sem=send_sem, recv_sem=recv_sem,
                                     device_id=(x, y, 1 - c), device_id_type=MESH).wait_recv()
        cp.wait_send()

    return pl.pallas_call(body, name=name, in_specs=[ANY], out_specs=[ANY, pl.BlockSpec(memory_space=pltpu.VMEM)],
                          out_shape=[jax.ShapeDtypeStruct((R, C), F32), jax.ShapeDtypeStruct((8, LANES), F32)],
                          input_output_aliases={0: 0},
                          scratch_shapes=[pltpu.SemaphoreType.DMA, pltpu.SemaphoreType.DMA])(full)


def _all_reduce_small(v, name):
    R, C = v.shape

    def gather_body(v_ref, out_ref, send_sems, recv_sems, local_sem):
        x, y, c, _ = _place()
        me = 4 * x + 2 * y + c
        mine = pltpu.make_async_copy(v_ref, out_ref.at[me], local_sem)
        mine.start()
        flips = [(fx, fy, fc) for fx in (0, 1) for fy in (0, 1) for fc in (0, 1)][1:]
        copies = []
        for j, (fx, fy, fc) in enumerate(flips):
            peer = (x ^ fx, y ^ fy, c ^ fc)
            copies.append(pltpu.make_async_remote_copy(src_ref=v_ref, dst_ref=out_ref.at[me], send_sem=send_sems.at[j],
                                                       recv_sem=recv_sems.at[j], device_id=peer, device_id_type=MESH))
        for cp in copies:
            cp.start()
        for j, (fx, fy, fc) in enumerate(flips):
            peer = (x ^ fx, y ^ fy, c ^ fc)
            pltpu.make_async_remote_copy(src_ref=v_ref, dst_ref=out_ref.at[4 * peer[0] + 2 * peer[1] + peer[2]],
                                         send_sem=send_sems.at[j], recv_sem=recv_sems.at[j], device_id=peer,
                                         device_id_type=MESH).wait_recv()
        for cp in copies:
            cp.wait_send()
        mine.wait()

    slots = pl.pallas_call(gather_body, name=name, in_specs=[ANY], out_specs=ANY,
                           out_shape=jax.ShapeDtypeStruct((8, R, C), F32),
                           scratch_shapes=[pltpu.SemaphoreType.DMA((7,)), pltpu.SemaphoreType.DMA((7,)),
                                           pltpu.SemaphoreType.DMA])(v)
    return _sum_slots(slots, f"{name}_sum")


def _adamw(w, g, m, v, name, dep=()):
    R, C = w.shape
    tr = _tile(R, max(8, (2 << 20) // (4 * C) // 8 * 8))
    bc1 = 1.0 - ADAM_B1 ** ADAM_STEP
    bc2 = 1.0 - ADAM_B2 ** ADAM_STEP

    def body(w_ref, g_ref, m_ref, v_ref, *rest):
        go_ref, d_ref, nm_ref, nv_ref = rest[len(dep):]
        gv = g_ref[...]
        go_ref[...] = gv
        nm = ADAM_B1 * m_ref[...] + (1.0 - ADAM_B1) * gv
        nv = ADAM_B2 * v_ref[...] + (1.0 - ADAM_B2) * (gv * gv)
        nm_ref[...] = nm
        nv_ref[...] = nv
        d_ref[...] = -ADAM_LR * ((nm / bc1) / (jnp.sqrt(nv / bc2) + ADAM_EPS) + ADAM_WD * w_ref[...])

    blk = pl.BlockSpec((tr, C), lambda r: (r, 0))
    out = jax.ShapeDtypeStruct((R, C), F32)
    in_specs = [blk] * 4 + [pl.BlockSpec(d.shape, lambda r: (0, 0)) for d in dep]
    return pl.pallas_call(body, name=name, grid=(R // tr,), in_specs=in_specs, out_specs=[blk] * 4, out_shape=[out] * 4,
                          compiler_params=_params())(w, g, m, v, *dep)


SC_TILES = 32
SC_LANES = 16
SC_ROWS = 8


def _sc_mesh():
    return plsc.VectorSubcoreMesh(core_axis_name="sc_core", subcore_axis_name="sc_subcore")


def _pair_add_sc(gfull, other, name):
    G, R, C = gfull.shape
    hr = R // 2
    tiles_per_shard = SC_TILES // G
    per_tile = hr // SC_ROWS // tiles_per_shard

    def body(g_hbm, o_hbm, out_hbm, gb, ob):
        c = lax.axis_index("c")
        tile = lax.axis_index("sc_subcore") * 2 + lax.axis_index("sc_core")
        t = tile // tiles_per_shard
        first = (tile % tiles_per_shard) * per_tile

        @pl.loop(0, per_tile)
        def _(k):
            rr = (first + k) * SC_ROWS
            pltpu.sync_copy(g_hbm.at[t, pl.ds(c * hr + rr, SC_ROWS), :], gb)
            pltpu.sync_copy(o_hbm.at[t, pl.ds(rr, SC_ROWS), :], ob)

            @pl.loop(0, SC_ROWS)
            def _(i):
                @pl.loop(0, C, step=SC_LANES)
                def _(j):
                    at = (i, pl.ds(j, SC_LANES))
                    gb[at] = gb[at] + ob[at]

            pltpu.sync_copy(gb, out_hbm.at[t, pl.ds(rr, SC_ROWS), :])

    buf = pltpu.VMEM((SC_ROWS, C), F32)
    return pl.kernel(body, name=name, out_type=jax.ShapeDtypeStruct((G, hr, C), F32), mesh=_sc_mesh(),
                     scratch_types=[buf, buf])(gfull, other)


def _adamw_sc(w, g, m, v, name):
    R, C = w.shape
    tasks = R // SC_ROWS
    bc1 = 1.0 - ADAM_B1 ** ADAM_STEP
    bc2 = 1.0 - ADAM_B2 ** ADAM_STEP

    def body(w_hbm, g_hbm, m_hbm, v_hbm, go_hbm, d_hbm, nm_hbm, nv_hbm, wb, gb, mb, vb):
        tile = lax.axis_index("sc_subcore") * 2 + lax.axis_index("sc_core")

        @pl.loop((tile * tasks) // SC_TILES, ((tile + 1) * tasks) // SC_TILES)
        def _(task):
            rows = pl.ds(task * SC_ROWS, SC_ROWS)
            pltpu.sync_copy(w_hbm.at[rows, :], wb)
            pltpu.sync_copy(g_hbm.at[rows, :], gb)
            pltpu.sync_copy(m_hbm.at[rows, :], mb)
            pltpu.sync_copy(v_hbm.at[rows, :], vb)

            @pl.loop(0, SC_ROWS)
            def _(i):
                @pl.loop(0, C, step=SC_LANES)
                def _(j):
                    at = (i, pl.ds(j, SC_LANES))
                    gv = gb[at]
                    nm = ADAM_B1 * mb[at] + (1.0 - ADAM_B1) * gv
                    nv = ADAM_B2 * vb[at] + (1.0 - ADAM_B2) * (gv * gv)
                    mb[at] = nm
                    vb[at] = nv
                    wb[at] = -ADAM_LR * ((nm / bc1) / (jnp.sqrt(nv / bc2) + ADAM_EPS) + ADAM_WD * wb[at])

            pltpu.sync_copy(gb, go_hbm.at[rows, :])
            pltpu.sync_copy(wb, d_hbm.at[rows, :])
            pltpu.sync_copy(mb, nm_hbm.at[rows, :])
            pltpu.sync_copy(vb, nv_hbm.at[rows, :])

    out = jax.ShapeDtypeStruct((R, C), F32)
    buf = pltpu.VMEM((SC_ROWS, C), F32)
    return pl.kernel(body, name=name, out_type=(out, out, out, out), mesh=_sc_mesh(),
                     scratch_types=[buf, buf, buf, buf])(w, g, m, v)


WEIGHTS = ['ffn1_norm', 'ffn1_w_gate', 'ffn1_w_up', 'ffn1_w_down', 'mix_norm', 'w_in', 'pool_w', 'pool_scale', 'gla_w_a2',
           'gla_b_a', 'gla_head_norm', 'w_out', 'xattn_norm', 'mem_norm', 'xattn_w_q', 'xattn_w_kv', 'xattn_w_o', 'ffn2_norm',
           'ffn2_w_gate', 'ffn2_w_up', 'ffn2_w_down', 'final_norm']
SHARDED = ['ffn1_w_gate', 'ffn1_w_up', 'ffn1_w_down', 'w_in', 'pool_w', 'gla_w_a2', 'w_out', 'xattn_w_q', 'xattn_w_kv',
           'xattn_w_o', 'ffn2_w_gate', 'ffn2_w_up', 'ffn2_w_down']
REPLICATED = [n for n in WEIGHTS if n not in SHARDED]
ON_SPARSECORE = ['ffn2_w_gate', 'ffn2_w_up', 'w_out', 'xattn_w_q', 'xattn_w_kv', 'xattn_w_o']
PAIR_SUM_ON_SPARSECORE = ['ffn2_w_down', 'ffn2_w_gate', 'ffn2_w_up', 'xattn_w_o', 'xattn_w_kv', 'pool_w', 'ffn1_w_down']
SMALL_COLS = 512


def _as2d(a):
    return a.reshape(-1, a.shape[-1])


def _finish_weight(name, gathered, wl):
    G, R, C = gathered.shape
    rank = wl["gla_w_a2"].shape[1]
    if name in ("w_out", "xattn_w_q", "xattn_w_o"):
        return gathered.reshape(G * R, C)
    if name == "w_in":
        w_in = jnp.transpose(gathered, (1, 0, 2)).reshape(R, G * C)
        main = G * C - rank
        return jnp.concatenate([w_in[:, :main], jnp.pad(w_in[:, main:], ((0, 0), (0, LANES - rank)))], axis=1)
    if name == "pool_w":
        NG, CJ, _ = wl[name].shape[1:]
        return jnp.transpose(gathered.reshape(G, NG, CJ, C), (1, 0, 2, 3)).reshape(NG, G * CJ, C)
    if name == "gla_w_a2":
        a2 = jnp.transpose(gathered, (1, 0, 2)).reshape(rank, G * C)
        return jnp.pad(a2, ((0, LANES - rank), (0, 0))).astype(BF16)
    return gathered


def _start_gathers(wl):
    started = {}
    token = None
    for n in SHARDED:
        whole = n not in ("ffn1_w_gate", "ffn1_w_up")
        buf = _cast_to_slot(_as2d(wl[n]), BF16, f"slot_{n}", dep=token)
        send_sems, recv_sems, thru, token = _gather_start(buf, f"gather_start_{n}", whole)
        started[n] = (send_sems, recv_sems, thru, whole)
    cache = {}

    def weight(n, after=None):
        if n not in cache:
            *handles, whole = started[n]
            buf = _gather_wait(*handles, after, f"gather_wait_{n}", whole)
            if not whole:
                buf = _gather_forward(buf, f"gather_forward_{n}")
            cache[n] = _finish_weight(n, buf, wl)
        return cache[n]

    return weight, token


def _shard_major(name, gfull, wl):
    R, C = _as2d(wl[name]).shape
    if name in ("ffn1_w_gate", "ffn1_w_up", "ffn2_w_gate", "ffn2_w_up", "xattn_w_kv"):
        return gfull
    if name in ("ffn1_w_down", "ffn2_w_down", "w_out", "xattn_w_q", "xattn_w_o"):
        return gfull.reshape(N_SHARDS, R, C)
    if name == "w_in":
        return jnp.transpose(gfull[:, :N_SHARDS * C].reshape(R, N_SHARDS, C), (1, 0, 2))
    if name == "pool_w":
        NG, CJ, _ = wl[name].shape[1:]
        return jnp.transpose(gfull.reshape(NG, N_SHARDS, CJ, C), (1, 0, 2, 3)).reshape(N_SHARDS, R, C)
    assert name == "gla_w_a2"
    return jnp.transpose(gfull[:R].reshape(R, N_SHARDS, C), (1, 0, 2))


def kernel(x, mem, ffn1_norm, ffn1_w_gate, ffn1_w_up, ffn1_w_down, mix_norm, w_in, pool_w, pool_scale, gla_w_a2, gla_b_a, gla_head_norm, w_out, xattn_norm, mem_norm, xattn_w_q, xattn_w_kv, xattn_w_o, ffn2_norm, ffn2_w_gate, ffn2_w_up, ffn2_w_down, final_norm, loss_target, m_ffn1_norm, m_ffn1_w_gate, m_ffn1_w_up, m_ffn1_w_down, m_mix_norm, m_w_in, m_pool_w, m_pool_scale, m_gla_w_a2, m_gla_b_a, m_gla_head_norm, m_w_out, m_xattn_norm, m_mem_norm, m_xattn_w_q, m_xattn_w_kv, m_xattn_w_o, m_ffn2_norm, m_ffn2_w_gate, m_ffn2_w_up, m_ffn2_w_down, m_final_norm, v_ffn1_norm, v_ffn1_w_gate, v_ffn1_w_up, v_ffn1_w_down, v_mix_norm, v_w_in, v_pool_w, v_pool_scale, v_gla_w_a2, v_gla_b_a, v_gla_head_norm, v_w_out, v_xattn_norm, v_mem_norm, v_xattn_w_q, v_xattn_w_kv, v_xattn_w_o, v_ffn2_norm, v_ffn2_w_gate, v_ffn2_w_up, v_ffn2_w_down, v_final_norm):
    given = dict(locals())
    wl = {n: given[n] for n in WEIGHTS}
    ml = {n: given["m_" + n] for n in WEIGHTS}
    vl = {n: given["v_" + n] for n in WEIGHTS}

    vec = {n: wl[n].reshape(1, -1) for n in REPLICATED}
    weight, dep0 = _start_gathers(wl)
    in_flight = {}

    pair_flight = {}

    def emit_begin(n, gfull):
        *pair_flight[n], token = _pair_start(_shard_major(n, gfull, wl), f"{n}_pair_start")
        return token

    summing = {}

    def emit_finish(n, after):
        gsm, other = _pair_wait(*pair_flight.pop(n), after, f"{n}_pair_wait")
        if n in PAIR_SUM_ON_SPARSECORE:
            summing[n] = _pair_add_sc(gsm, other, f"{n}_pair_add_sc")
            return None
        *in_flight[n], token = _chip_start(_pair_add(gsm, other, f"{n}_pair_add"), f"{n}_chip_start")
        return token

    def emit_send(n):
        *in_flight[n], token = _chip_start(summing.pop(n), f"{n}_chip_start")
        return token

    grads = {}
    updates = {}

    def reduce_done(n, after):
        part, slots = _chip_wait(*in_flight.pop(n), after, f"{n}_chip_wait")
        grads[n], token = _pair_join(_chip_sum(part, slots, f"{n}_chip_sum"), f"{n}_pair_join")
        return token

    def early_update(after):
        tokens = [reduce_done(n, after) for n in ON_SPARSECORE]
        for n in ON_SPARSECORE:
            g2 = grads[n]
            updates[n] = _adamw_sc(wl[n].reshape(g2.shape), g2, ml[n].reshape(g2.shape), vl[n].reshape(g2.shape),
                                   f"adamw_sc_{n}")
        return tokens

    loss, dx0, g = _local_step(x[0], mem[0], loss_target[0], vec, weight,
                               (emit_begin, emit_finish, emit_send, early_update), dep0)
    assert not summing

    for n in list(in_flight):
        reduce_done(n, dx0)
    widths = [wl[n].size for n in REPLICATED]
    total = sum(widths)
    rows = -(-total // SMALL_COLS)
    rows = -(-rows // 8) * 8
    packed = jnp.concatenate([g[n].reshape(-1) for n in REPLICATED] + [jnp.zeros((rows * SMALL_COLS - total,), F32)])
    summed = _all_reduce_small(packed.reshape(rows, SMALL_COLS), "small_all_reduce").reshape(-1)
    off = 0
    for n, width in zip(REPLICATED, widths):
        grads[n] = summed[off:off + width].reshape(1, width)
        off += width

    out_g, out_d, out_m, out_v = [], [], [], []
    for n in WEIGHTS:
        shape = wl[n].shape
        g2 = grads[n]
        if n in updates:
            go, d, nm, nv = updates[n]
        else:
            dep = tuple(updates[k][1][:8, :LANES] for k in updates) if n == "w_in" else ()
            go, d, nm, nv = _adamw(wl[n].reshape(g2.shape), g2, ml[n].reshape(g2.shape), vl[n].reshape(g2.shape),
                                   f"adamw_{n}", dep)
        out_g.append(go.reshape(shape))
        out_d.append(d.reshape(shape))
        out_m.append(nm.reshape(shape))
        out_v.append(nv.reshape(shape))
    return (loss, dx0.reshape(x.shape), *out_g, *out_d, *out_m, *out_v)
```

```python
import functools

import jax
import jax.numpy as jnp
from jax import lax
from jax.experimental import pallas as pl
from jax.experimental.pallas import tpu as pltpu
from jax.experimental.pallas import tpu_sc as plsc

F32 = jnp.float32
BF16 = jnp.bfloat16
MESH = pl.DeviceIdType.MESH

RMS_EPS = 1e-6
CHUNK = 64
POOL_WINDOWS = (2, 4, 8, 16)
POOL_HALO = 16
N_HEADS = 4
GATE_TEMP = 16.0
ADAM_LR, ADAM_B1, ADAM_B2, ADAM_EPS, ADAM_WD, ADAM_STEP = 0.001, 0.9, 0.999, 1e-08, 0.01, 10
N_SHARDS = 4
LANES = 128
MXU_COLS = 256
TOKENS_PER_STEP = 2048
VMEM_LIMIT = 58 * 1024 * 1024

ANY = pl.BlockSpec(memory_space=pl.ANY)
HBM = pl.BlockSpec(memory_space=pltpu.HBM)
SEM = pl.BlockSpec(memory_space=pltpu.SEMAPHORE)
EFFECT = pltpu.SideEffectType.DATAFLOW_SIDE_EFFECTING


def _params(**kw):
    return pltpu.CompilerParams(vmem_limit_bytes=VMEM_LIMIT, **kw)


def _tile(n, want):
    for unit in (LANES, 8):
        t = (min(want, n) // unit) * unit
        while t >= unit:
            if n % t == 0:
                return t
            t -= unit
    return n


def _dot(a, b, dims):
    return lax.dot_general(a, b, (dims, ((), ())), preferred_element_type=F32)


def _nn(a, b):
    return _dot(a, b, ((1,), (0,)))


def _nt(a, b):
    return _dot(a, b, ((1,), (1,)))


def _tn(a, b):
    return _dot(a, b, ((0,), (0,)))


def _sigmoid(x):
    return 1.0 / (1.0 + jnp.exp(-x))


def _matmul(a, b, *, mode, name, out_dtype, tm=512, tn=2048, tk=2048, res=None, scale=1.0, b_groups=False, out_groups=0,
            dep=()):
    if mode == "tn":
        K, M = a.shape
    else:
        M, K = a.shape
    if mode == "nn":
        if b_groups:
            G, _, Nj = b.shape
            N = G * Nj
        else:
            N = b.shape[1]
    elif mode == "nt":
        if b_groups:
            G, N, Kj = b.shape
            assert G * Kj == K
        else:
            N = b.shape[0]
    else:
        N = b.shape[1]
    tm = _tile(M, tm)
    if mode == "nn" and b_groups:
        tn = _tile(Nj, tn)
    elif out_groups:
        tn = _tile(N // out_groups, tn)
    else:
        tn = _tile(N, tn)
    if mode == "nt" and b_groups:
        tk = _tile(Kj, tk)
    else:
        tk = _tile(K, tk)
    nk = K // tk
    grid = (M // tm, N // tn, nk)

    if mode == "tn":
        a_spec = pl.BlockSpec((tk, tm), lambda i, j, k: (k, i))
        b_spec = pl.BlockSpec((tk, tn), lambda i, j, k: (k, j))
        dims = ((0,), (0,))
    elif mode == "nn":
        a_spec = pl.BlockSpec((tm, tk), lambda i, j, k: (i, k))
        if b_groups:
            npj = Nj // tn
            b_spec = pl.BlockSpec((None, tk, tn), lambda i, j, k: (j // npj, k, j % npj))
        else:
            b_spec = pl.BlockSpec((tk, tn), lambda i, j, k: (k, j))
        dims = ((1,), (0,))
    else:
        a_spec = pl.BlockSpec((tm, tk), lambda i, j, k: (i, k))
        if b_groups:
            kpj = Kj // tk
            b_spec = pl.BlockSpec((None, tn, tk), lambda i, j, k: (k // kpj, j, k % kpj))
        else:
            b_spec = pl.BlockSpec((tn, tk), lambda i, j, k: (j, k))
        dims = ((1,), (1,))
    if out_groups:
        npj = (N // out_groups) // tn
        o_spec = pl.BlockSpec((None, tm, tn), lambda i, j, k: (j // npj, i, j % npj))
        out_shape = jax.ShapeDtypeStruct((out_groups, M, N // out_groups), out_dtype)
    else:
        o_spec = pl.BlockSpec((tm, tn), lambda i, j, k: (i, j))
        out_shape = jax.ShapeDtypeStruct((M, N), out_dtype)
    in_specs = [a_spec, b_spec]
    operands = [a, b]
    if res is not None:
        in_specs.append(pl.BlockSpec((tm, tn), lambda i, j, k: (i, j)))
        operands.append(res)
    has_res = res is not None
    n_dep = len(dep)
    for d in dep:
        in_specs.append(pl.BlockSpec(d.shape, lambda i, j, k: (0, 0)))
        operands.append(d)

    def body(*refs):
        if has_res:
            a_ref, b_ref, r_ref = refs[:3]
        else:
            a_ref, b_ref = refs[:2]
            r_ref = None
        o_ref = refs[2 + has_res + n_dep]

        def finish(acc):
            if scale != 1.0:
                acc = acc * scale
            if r_ref is not None:
                acc = r_ref[...] + acc
            o_ref[...] = acc.astype(o_ref.dtype)

        part = _dot(a_ref[...], b_ref[...], dims)
        if nk == 1:
            finish(part)
        else:
            acc_ref = o_ref if in_place else refs[-1]
            k = pl.program_id(2)

            @pl.when(k == 0)
            def _():
                acc_ref[...] = part

            @pl.when(k > 0)
            def _():
                acc_ref[...] += part

            if not in_place:
                @pl.when(k == nk - 1)
                def _():
                    finish(acc_ref[...])

    in_place = out_dtype == F32 and res is None and scale == 1.0
    scratch = [] if nk == 1 or in_place else [pltpu.VMEM((tm, tn), F32)]
    return pl.pallas_call(body, name=name, grid=grid, in_specs=in_specs, out_specs=o_spec, out_shape=out_shape,
                          scratch_shapes=scratch, compiler_params=_params())(*operands)


def _rms_fwd(x, gain, name, tm=512, dep=None):
    S, D = x.shape
    tm = _tile(S, tm)

    def body(x_ref, g_ref, *rest):
        o_ref = rest[-1]
        xv = x_ref[...]
        r = lax.rsqrt(jnp.mean(xv * xv, axis=-1, keepdims=True) + RMS_EPS)
        o_ref[...] = (xv * r * g_ref[...]).astype(o_ref.dtype)

    in_specs = [pl.BlockSpec((tm, D), lambda i: (i, 0)), pl.BlockSpec((1, D), lambda i: (0, 0))]
    operands = [x, gain]
    if dep is not None:
        in_specs.append(pl.BlockSpec(dep.shape, lambda i: (0, 0)))
        operands.append(dep)
    return pl.pallas_call(body, name=name, grid=(S // tm,), in_specs=in_specs,
                          out_specs=pl.BlockSpec((tm, D), lambda i: (i, 0)),
                          out_shape=jax.ShapeDtypeStruct((S, D), BF16), compiler_params=_params())(*operands)


def _rms_bwd(x, gain, dh, dres, name, lowp=None, tm=512):
    half = lowp is not None
    S, D = x.shape
    tm = _tile(S, tm)
    has_res = dres is not None

    def body(*refs):
        if has_res:
            x_ref, g_ref, dh_ref, dr_ref = refs[:4]
            outs = refs[4:]
        else:
            x_ref, g_ref, dh_ref = refs[:3]
            dr_ref = None
            outs = refs[3:]
        dx_ref, dg_ref = outs[0], outs[-1]
        xv = x_ref[...]
        dhv = dh_ref[...].astype(F32)
        r = lax.rsqrt(jnp.mean(xv * xv, axis=-1, keepdims=True) + RMS_EPS)
        gy = dhv * g_ref[...]
        dx = r * gy - xv * (r * r * r) * jnp.mean(gy * xv, axis=-1, keepdims=True)
        if dr_ref is not None:
            dx = dx + dr_ref[...]
        dx_ref[...] = dx
        if half:
            outs[1][...] = (dx if lowp == 1.0 else lowp * dx).astype(BF16)
        part = jnp.sum(dhv * xv * r, axis=0, keepdims=True)

        @pl.when(pl.program_id(0) == 0)
        def _():
            dg_ref[...] = part

        @pl.when(pl.program_id(0) > 0)
        def _():
            dg_ref[...] += part

    row = pl.BlockSpec((tm, D), lambda i: (i, 0))
    vec = pl.BlockSpec((1, D), lambda i: (0, 0))
    in_specs = [row, vec, row] + ([row] if has_res else [])
    operands = [x, gain, dh] + ([dres] if has_res else [])
    out_specs = [row] + ([row] if half else []) + [vec]
    out_shape = [jax.ShapeDtypeStruct((S, D), F32)] + ([jax.ShapeDtypeStruct((S, D), BF16)] if half else []) + [
        jax.ShapeDtypeStruct((1, D), F32)]
    return pl.pallas_call(body, name=name, grid=(S // tm,), in_specs=in_specs, out_specs=out_specs, out_shape=out_shape,
                          compiler_params=_params())(*operands)


def _loss_head(x, gain, target, name, tm=512):
    S, D = x.shape
    tm = _tile(S, tm)

    def body(x_ref, g_ref, t_ref, sq_ref, dx_ref, dxh_ref, dg_ref):
        xv = x_ref[...]
        r = lax.rsqrt(jnp.mean(xv * xv, axis=-1, keepdims=True) + RMS_EPS)
        xn = xv * r
        err = xn * g_ref[...] - t_ref[...]
        dout = err * (1.0 / D)
        gy = dout * g_ref[...]
        dx = r * gy - xv * (r * r * r) * jnp.mean(gy * xv, axis=-1, keepdims=True)
        dx_ref[...] = dx
        dxh_ref[...] = (0.5 * dx).astype(BF16)
        sq = jnp.sum(err * err, axis=0, keepdims=True)
        dg = jnp.sum(dout * xn, axis=0, keepdims=True)

        @pl.when(pl.program_id(0) == 0)
        def _():
            sq_ref[...] = sq
            dg_ref[...] = dg

        @pl.when(pl.program_id(0) > 0)
        def _():
            sq_ref[...] += sq
            dg_ref[...] += dg

    row = pl.BlockSpec((tm, D), lambda i: (i, 0))
    vec = pl.BlockSpec((1, D), lambda i: (0, 0))
    return pl.pallas_call(body, name=name, grid=(S // tm,), in_specs=[row, vec, row], out_specs=[vec, row, row, vec],
                          out_shape=[jax.ShapeDtypeStruct((1, D), F32), jax.ShapeDtypeStruct((S, D), F32),
                                     jax.ShapeDtypeStruct((S, D), BF16), jax.ShapeDtypeStruct((1, D), F32)],
                          compiler_params=_params())(x, gain, target)


def _cast(x, dtype, name, scale=1.0, tm=256):
    S, D = x.shape
    tm = _tile(S, tm)

    def body(x_ref, o_ref):
        o_ref[...] = (x_ref[...] * scale).astype(o_ref.dtype)

    row = pl.BlockSpec((tm, D), lambda i: (i, 0))
    return pl.pallas_call(body, name=name, grid=(S // tm,), in_specs=[row], out_specs=row,
                          out_shape=jax.ShapeDtypeStruct((S, D), dtype), compiler_params=_params())(x)


def _ffn_up(h, wg, wu, name, tm=512):
    S, D = h.shape
    G, _, Fj = wg.shape
    tm = _tile(S, tm)

    def body(h_ref, wg_ref, wu_ref, ga_ref, gb_ref, hid_ref):
        hv = h_ref[...]
        a = _nn(hv, wg_ref[...])
        b = _nn(hv, wu_ref[...])
        s = _sigmoid(a)
        silu = a * s
        ga_ref[...] = (b * (s * (1.0 + a * (1.0 - s)))).astype(BF16)
        gb_ref[...] = silu.astype(BF16)
        hid_ref[...] = (silu * b).astype(BF16)

    w_spec = pl.BlockSpec((None, D, Fj), lambda g, i: (g, 0, 0))
    o_spec = pl.BlockSpec((tm, Fj), lambda g, i: (i, g))
    out = jax.ShapeDtypeStruct((S, G * Fj), BF16)
    return pl.pallas_call(body, name=name, grid=(G, S // tm),
                          in_specs=[pl.BlockSpec((tm, D), lambda g, i: (i, 0)), w_spec, w_spec],
                          out_specs=[o_spec, o_spec, o_spec], out_shape=[out, out, out], compiler_params=_params())(h, wg, wu)


def _ffn_dact(dxh, wd, ga, gb, name, tm=512):
    S, D = dxh.shape
    G, Fj, _ = wd.shape
    tm = _tile(S, tm)

    def body(dx_ref, wd_ref, ga_ref, gb_ref, da_ref, db_ref):
        dhid = _nt(dx_ref[...], wd_ref[...])
        da_ref[...] = (dhid * ga_ref[...].astype(F32)).astype(BF16)
        db_ref[...] = (dhid * gb_ref[...].astype(F32)).astype(BF16)

    blk = pl.BlockSpec((tm, Fj), lambda g, i: (i, g))
    out = jax.ShapeDtypeStruct((S, G * Fj), BF16)
    return pl.pallas_call(body, name=name, grid=(G, S // tm),
                          in_specs=[pl.BlockSpec((tm, D), lambda g, i: (i, 0)),
                                    pl.BlockSpec((None, Fj, D), lambda g, i: (g, 0, 0)), blk, blk],
                          out_specs=[blk, blk], out_shape=[out, out], compiler_params=_params())(dxh, wd, ga, gb)


def _ffn_dh(da, db, wg, wu, name, dep=(), tm=512):
    S = da.shape[0]
    G, D, Fj = wg.shape
    tm = _tile(S, tm)

    def body(da_ref, db_ref, wg_ref, wu_ref, *rest):
        o_ref = rest[-1]
        part = _nt(da_ref[...], wg_ref[...]) + _nt(db_ref[...], wu_ref[...])

        @pl.when(pl.program_id(1) == 0)
        def _():
            o_ref[...] = part

        @pl.when(pl.program_id(1) > 0)
        def _():
            o_ref[...] += part

    act = pl.BlockSpec((tm, Fj), lambda i, g: (i, g))
    w_spec = pl.BlockSpec((None, D, Fj), lambda i, g: (g, 0, 0))
    in_specs = [act, act, w_spec, w_spec] + [pl.BlockSpec(d.shape, lambda i, g: (0, 0)) for d in dep]
    return pl.pallas_call(body, name=name, grid=(S // tm, G), in_specs=in_specs,
                          out_specs=pl.BlockSpec((tm, D), lambda i, g: (i, 0)),
                          out_shape=jax.ShapeDtypeStruct((S, D), F32), compiler_params=_params())(da, db, wg, wu, *dep)


def _pool_fwd(proj, pool_w, pool_scale, name, tm=512):
    S = proj.shape[0]
    NG, C, _ = pool_w.shape
    DP = NG * C
    tm = _tile(S, tm)
    hb = tm // POOL_HALO
    n_ext = tm + POOL_HALO

    def body(u_ref, halo_ref, w_ref, sc_ref, y_ref, d_ref):
        i = pl.program_id(0)
        t = lax.broadcasted_iota(jnp.int32, (tm, 1), 0) + i * tm
        for g, win in enumerate(POOL_WINDOWS):
            cols = slice(g * C, (g + 1) * C)
            ug = u_ref[:, cols]
            halo = jnp.where(i > 0, halo_ref[:, cols], 0.0)
            acc = jnp.concatenate([halo, ug], axis=0)
            step = 1
            while step < win:
                acc = acc + pltpu.roll(acc, step, 0)
                step *= 2
            count = jnp.minimum(t + 1, win).astype(F32)
            d = (acc[POOL_HALO:, :] / count - ug).astype(BF16)
            d_ref[:, cols] = d
            y_ref[:, cols] = (_nn(d, w_ref[g]) * sc_ref[:, cols]).astype(BF16)

    del n_ext
    return pl.pallas_call(
        body, name=name, grid=(S // tm,),
        in_specs=[pl.BlockSpec((tm, DP), lambda i: (i, 0)),
                  pl.BlockSpec((POOL_HALO, DP), lambda i: (jnp.maximum(i * hb - 1, 0), 0)),
                  pl.BlockSpec((NG, C, C), lambda i: (0, 0, 0)), pl.BlockSpec((1, DP), lambda i: (0, 0))],
        out_specs=[pl.BlockSpec((tm, DP), lambda i: (i, 0)), pl.BlockSpec((tm, DP), lambda i: (i, 0))],
        out_shape=[jax.ShapeDtypeStruct((S, DP), BF16), jax.ShapeDtypeStruct((S, DP), BF16)],
        compiler_params=_params())(proj, proj, pool_w, pool_scale)


def _pool_bwd(dymix, d, pool_w, pool_scale, name, tm=512):
    S = dymix.shape[0]
    NG, C, _ = pool_w.shape
    DP = NG * C
    tm = _tile(S, tm)
    hb = tm // POOL_HALO
    nb = S // tm
    n_ext = tm + POOL_HALO
    last_halo = S // POOL_HALO - 1

    def body(dy_ref, halo_ref, d_ref, w_ref, sc_ref, du_ref, dw_ref, dsc_ref):
        i = pl.program_id(0)
        t = lax.broadcasted_iota(jnp.int32, (n_ext, 1), 0) + i * tm
        for g, win in enumerate(POOL_WINDOWS):
            cols = slice(g * C, (g + 1) * C)
            dy = dy_ref[:, cols]
            halo = jnp.where(i < nb - 1, halo_ref[:, cols], 0.0)
            sc = sc_ref[:, cols]
            dv = d_ref[:, cols]
            e_ext = (jnp.concatenate([dy, halo], axis=0) * sc).astype(BF16)
            dd = _nt(e_ext, w_ref[g])
            count = jnp.minimum(t + 1, win).astype(F32)
            acc = dd / count
            step = 1
            while step < win:
                acc = acc + pltpu.roll(acc, n_ext - step, 0)
                step *= 2
            du_ref[:, cols] = (acc[:tm, :] - dd[:tm, :]).astype(BF16)
            dw = _tn(dv, e_ext[:tm, :])
            dsc = jnp.sum(dy * _nn(dv, w_ref[g]), axis=0, keepdims=True)

            @pl.when(i == 0)
            def _():
                dw_ref[g] = dw
                dsc_ref[:, cols] = dsc

            @pl.when(i > 0)
            def _():
                dw_ref[g] += dw
                dsc_ref[:, cols] += dsc

    return pl.pallas_call(
        body, name=name, grid=(nb,),
        in_specs=[pl.BlockSpec((tm, DP), lambda i: (i, 0)),
                  pl.BlockSpec((POOL_HALO, DP), lambda i: (jnp.minimum((i + 1) * hb, last_halo), 0)),
                  pl.BlockSpec((tm, DP), lambda i: (i, 0)),
                  pl.BlockSpec((NG, C, C), lambda i: (0, 0, 0)), pl.BlockSpec((1, DP), lambda i: (0, 0))],
        out_specs=[pl.BlockSpec((tm, DP), lambda i: (i, 0)), pl.BlockSpec((NG, C, C), lambda i: (0, 0, 0)),
                   pl.BlockSpec((1, DP), lambda i: (0, 0))],
        out_shape=[jax.ShapeDtypeStruct((S, DP), BF16), jax.ShapeDtypeStruct((NG, C, C), F32),
                   jax.ShapeDtypeStruct((1, DP), F32)],
        compiler_params=_params())(dymix, dymix, d, pool_w, pool_scale)


def _chunk_scan(v, rows, reverse):
    n = v.shape[0]
    step = 1
    while step < CHUNK:
        if reverse:
            v = v + jnp.where(rows < CHUNK - step, pltpu.roll(v, n - step, 0), 0.0)
        else:
            v = v + jnp.where(rows >= step, pltpu.roll(v, step, 0), 0.0)
        step *= 2
    return v


def _log_decay(alr, w_a2, b_a):
    z = _nn(alr.astype(BF16), w_a2) + b_a
    la = (jnp.minimum(z, 0.0) - jnp.log(1.0 + jnp.exp(-jnp.abs(z)))) * (1.0 / GATE_TEMP)
    return z, la


def _gla_specs(DP, DKT, DV, tb, bmap):
    return [pl.BlockSpec((tb, DKT), lambda i: (bmap(i), DP // DKT)),
            pl.BlockSpec((tb, DKT), lambda i: (bmap(i), DP // DKT + 1)),
            pl.BlockSpec((tb, DV), lambda i: (bmap(i), (DP + 2 * DKT) // DV)),
            pl.BlockSpec((tb, DV), lambda i: (bmap(i), (DP + 2 * DKT) // DV + 1)),
            pl.BlockSpec((tb, LANES), lambda i: (bmap(i), (DP + 2 * DKT + 2 * DV) // LANES))]


def _gla_fwd(proj, y_pool, w_a2, b_a, head_norm, name, tb=512):
    S = proj.shape[0]
    DP = y_pool.shape[1]
    DKT = b_a.shape[1]
    DV = head_norm.shape[1]
    dk, dv = DKT // N_HEADS, DV // N_HEADS
    tb = _tile(S, tb)
    ncb = tb // CHUNK
    qscale = dk ** -0.5

    def body(q_ref, k_ref, v_ref, g_ref, alr_ref, yp_ref, wa_ref, ba_ref, hn_ref, y_ref, st_out_ref, st_ref, kdec_ref,
             gam_ref):
        @pl.when(pl.program_id(0) == 0)
        def _():
            st_ref[...] = jnp.zeros_like(st_ref)

        y_ref[:, :DP] = yp_ref[...]

        rows = lax.broadcasted_iota(jnp.int32, (tb, 1), 0) % CHUNK
        _, la = _log_decay(alr_ref[...], wa_ref[...], ba_ref[...])
        tail = _chunk_scan(la, rows, True)
        kdec_ref[...] = k_ref[...] * jnp.exp(tail - la)
        gam_ref[...] = jnp.exp(tail)

        def chunk(c, carry):
            r0 = pl.multiple_of(c * CHUNK, CHUNK)
            rs = pl.ds(r0, CHUNK)
            gam = gam_ref[pl.ds(r0, 1), :]
            heads = range(N_HEADS)
            kcs = [slice(h * dk, (h + 1) * dk) for h in heads]
            vcs = [slice(h * dv, (h + 1) * dv) for h in heads]
            upd = [_tn(v_ref[rs, vcs[h]].astype(BF16), kdec_ref[rs, kcs[h]].astype(BF16)) for h in heads]
            st = [st_ref[h] * gam[:, kcs[h]] + upd[h] for h in heads]
            o = [_nt((q_ref[rs, kcs[h]] * qscale).astype(BF16), st[h].astype(BF16)) for h in heads]
            for h in heads:
                st_ref[h] = st[h]
                st_out_ref[c, h] = st[h]
                r = lax.rsqrt(jnp.mean(o[h] * o[h], axis=-1, keepdims=True) + RMS_EPS)
                gv = g_ref[rs, vcs[h]]
                y_ref[rs, DP + h * dv:DP + (h + 1) * dv] = (o[h] * r * hn_ref[:, vcs[h]] * (gv * _sigmoid(gv))).astype(BF16)
            return carry

        lax.fori_loop(0, ncb, chunk, 0, unroll=2)

    full = lambda shape: pl.BlockSpec(shape, lambda i: (0,) * len(shape))
    return pl.pallas_call(
        body, name=name, grid=(S // tb,),
        in_specs=_gla_specs(DP, DKT, DV, tb, lambda i: i) + [pl.BlockSpec((tb, DP), lambda i: (i, 0)),
                                                            full((LANES, DKT)), full((1, DKT)), full((1, DV))],
        out_specs=[pl.BlockSpec((tb, DP + DV), lambda i: (i, 0)),
                   pl.BlockSpec((ncb, N_HEADS, dv, dk), lambda i: (i, 0, 0, 0))],
        out_shape=[jax.ShapeDtypeStruct((S, DP + DV), BF16), jax.ShapeDtypeStruct((S // CHUNK, N_HEADS, dv, dk), F32)],
        scratch_shapes=[pltpu.VMEM((N_HEADS, dv, dk), F32), pltpu.VMEM((tb, DKT), F32), pltpu.VMEM((tb, DKT), F32)],
        compiler_params=_params())(proj, proj, proj, proj, proj, y_pool, w_a2, b_a, head_norm)


def _gla_bwd(proj, states, dymix, du, w_a2, b_a, head_norm, name, tb=512):
    S = proj.shape[0]
    DP = du.shape[1]
    DKT = b_a.shape[1]
    DV = head_norm.shape[1]
    dk, dv = DKT // N_HEADS, DV // N_HEADS
    tb = _tile(S, tb)
    ncb = tb // CHUNK
    nb = S // tb
    qscale = dk ** -0.5
    rev = lambda i: nb - 1 - i

    q0, k0, v0, g0, a0 = DP, DP + DKT, DP + 2 * DKT, DP + 2 * DKT + DV, DP + 2 * DKT + 2 * DV

    def body(q_ref, k_ref, v_ref, g_ref, alr_ref, st_blk_ref, st_prev_ref, dy_ref, du_ref, wa_ref, ba_ref, hn_ref,
             dp_ref, dwa_ref, dba_ref, dhn_ref,
             dst_ref, kdec_ref, dec_ref, gam_ref, e_ref, dla_ref, dhn_acc_ref):
        i = pl.program_id(0)
        blk = rev(i)
        dp_ref[:, :DP] = du_ref[...]

        @pl.when(i == 0)
        def _():
            dst_ref[...] = jnp.zeros_like(dst_ref)

        dhn_acc_ref[...] = jnp.zeros_like(dhn_acc_ref)
        rows = lax.broadcasted_iota(jnp.int32, (tb, 1), 0) % CHUNK
        z, la = _log_decay(alr_ref[...], wa_ref[...], ba_ref[...])
        tail = _chunk_scan(la, rows, True)
        dec_ref[...] = jnp.exp(tail - la)
        kdec_ref[...] = k_ref[...] * dec_ref[...]
        gam_ref[...] = jnp.exp(tail)

        def chunk(cc, carry):
            c = ncb - 1 - cc
            r0 = pl.multiple_of(c * CHUNK, CHUNK)
            rs = pl.ds(r0, CHUNK)
            gam = gam_ref[pl.ds(r0, 1), :]
            first = jnp.logical_and(blk == 0, c == 0)
            heads = range(N_HEADS)
            kcs = [slice(h * dk, (h + 1) * dk) for h in heads]
            vcs = [slice(h * dv, (h + 1) * dv) for h in heads]
            qs = [(q_ref[rs, kcs[h]] * qscale).astype(BF16) for h in heads]
            stb = [st_blk_ref[c, h].astype(BF16) for h in heads]
            o = [_nt(qs[h], stb[h]) for h in heads]
            do = []
            for h in heads:
                oh = o[h]
                r = lax.rsqrt(jnp.mean(oh * oh, axis=-1, keepdims=True) + RMS_EPS)
                gv = g_ref[rs, vcs[h]]
                sg = _sigmoid(gv)
                dy = dy_ref[rs, vcs[h]]
                hn = hn_ref[:, vcs[h]]
                on = oh * r
                dp_ref[rs, g0 + h * dv:g0 + (h + 1) * dv] = (dy * on * hn * (sg * (1.0 + gv * (1.0 - sg)))).astype(BF16)
                don = dy * (gv * sg)
                dhn_acc_ref[:, vcs[h]] += jnp.sum(don * on, axis=0, keepdims=True)
                dn = don * hn
                do.append((r * dn - oh * (r * r * r) * jnp.mean(dn * oh, axis=-1, keepdims=True)).astype(BF16))
            dqs = [_nn(do[h], stb[h]) for h in heads]
            dst = [dst_ref[h] + _tn(do[h], qs[h]) for h in heads]
            for h in heads:
                dp_ref[rs, q0 + h * dk:q0 + (h + 1) * dk] = (dqs[h] * qscale).astype(BF16)
            dstb = [dst[h].astype(BF16) for h in heads]
            dvh = [_nt(kdec_ref[rs, kcs[h]].astype(BF16), dstb[h]) for h in heads]
            dkdec = [_nn(v_ref[rs, vcs[h]].astype(BF16), dstb[h]) for h in heads]
            gdg = []
            for h in heads:
                dp_ref[rs, v0 + h * dv:v0 + (h + 1) * dv] = dvh[h].astype(BF16)
                dp_ref[rs, k0 + h * dk:k0 + (h + 1) * dk] = (dkdec[h] * dec_ref[rs, kcs[h]]).astype(BF16)
                e_ref[rs, kcs[h]] = dkdec[h] * kdec_ref[rs, kcs[h]]
                st_prev = jnp.where(c > 0, st_blk_ref[jnp.maximum(c - 1, 0), h], st_prev_ref[0, h])
                st_prev = jnp.where(first, 0.0, st_prev)
                gdg.append(jnp.sum(dst[h] * st_prev, axis=0, keepdims=True) * gam[:, kcs[h]])
                dst_ref[h] = dst[h] * gam[:, kcs[h]]
            dla_ref[rs, :] = jnp.broadcast_to(jnp.concatenate(gdg, axis=1), (CHUNK, DKT))
            return carry

        lax.fori_loop(0, ncb, chunk, 0, unroll=2)

        ev = e_ref[...]
        dla = dla_ref[...] + _chunk_scan(ev, rows, False) - ev
        dz = dla * (1.0 / GATE_TEMP) * (1.0 - _sigmoid(z))
        dzb = dz.astype(BF16)
        dp_ref[:, a0:a0 + LANES] = _nt(dzb, wa_ref[...]).astype(BF16)
        dwa = _tn(alr_ref[...].astype(BF16), dzb)
        dba = jnp.sum(dz, axis=0, keepdims=True)

        @pl.when(i == 0)
        def _():
            dwa_ref[...] = dwa
            dba_ref[...] = dba
            dhn_ref[...] = dhn_acc_ref[...]

        @pl.when(i > 0)
        def _():
            dwa_ref[...] += dwa
            dba_ref[...] += dba
            dhn_ref[...] += dhn_acc_ref[...]

    full = lambda shape: pl.BlockSpec(shape, lambda i: (0,) * len(shape))
    rowblk = lambda w: pl.BlockSpec((tb, w), lambda i: (rev(i), 0))
    return pl.pallas_call(
        body, name=name, grid=(nb,),
        in_specs=_gla_specs(DP, DKT, DV, tb, rev) + [
            pl.BlockSpec((ncb, N_HEADS, dv, dk), lambda i: (rev(i), 0, 0, 0)),
            pl.BlockSpec((1, N_HEADS, dv, dk), lambda i: (jnp.maximum(rev(i) * ncb - 1, 0), 0, 0, 0)),
            pl.BlockSpec((tb, DV), lambda i: (rev(i), DP // DV)), rowblk(DP),
            full((LANES, DKT)), full((1, DKT)), full((1, DV))],
        out_specs=[rowblk(a0 + LANES), full((LANES, DKT)), full((1, DKT)), full((1, DV))],
        out_shape=[jax.ShapeDtypeStruct((S, a0 + LANES), BF16), jax.ShapeDtypeStruct((LANES, DKT), F32),
                   jax.ShapeDtypeStruct((1, DKT), F32), jax.ShapeDtypeStruct((1, DV), F32)],
        scratch_shapes=[pltpu.VMEM((N_HEADS, dv, dk), F32)] + [pltpu.VMEM((tb, DKT), F32)] * 5 + [pltpu.VMEM((1, DV), F32)],
        compiler_params=_params())(proj, proj, proj, proj, proj, states, states, dymix, du, w_a2, b_a, head_norm)


def _xattn_fwd(q, kv, name, tm=512):
    S, D = q.shape
    M = kv.shape[0]
    hd = D // N_HEADS
    tm = _tile(S, tm)
    scale = hd ** -0.5

    def body(q_ref, k_ref, v_ref, o_ref):
        heads = range(N_HEADS)
        hcs = [slice(h * hd, (h + 1) * hd) for h in heads]
        s = [_nt(q_ref[:, hc], k_ref[:, hc]) * scale for hc in hcs]
        p = []
        for h in heads:
            e = jnp.exp(s[h] - jnp.max(s[h], axis=-1, keepdims=True))
            p.append((e / jnp.sum(e, axis=-1, keepdims=True)).astype(BF16))
        o = [_nn(p[h], v_ref[:, hcs[h]]) for h in heads]
        for h in heads:
            o_ref[:, hcs[h]] = o[h].astype(BF16)

    return pl.pallas_call(body, name=name, grid=(S // tm,),
                          in_specs=[pl.BlockSpec((tm, D), lambda i: (i, 0)), pl.BlockSpec((M, D), lambda i: (0, 0)),
                                    pl.BlockSpec((M, D), lambda i: (0, 1))],
                          out_specs=pl.BlockSpec((tm, D), lambda i: (i, 0)),
                          out_shape=jax.ShapeDtypeStruct((S, D), BF16), compiler_params=_params())(q, kv, kv)


def _xattn_bwd(q, kv, do, name, tm=512):
    S, D = q.shape
    M = kv.shape[0]
    hd = D // N_HEADS
    tm = _tile(S, tm)
    scale = hd ** -0.5

    def body(q_ref, k_ref, v_ref, do_ref, dq_ref, dkv_ref):
        first = pl.program_id(0) == 0
        heads = range(N_HEADS)
        hcs = [slice(h * hd, (h + 1) * hd) for h in heads]
        s = [_nt(q_ref[:, hc], k_ref[:, hc]) * scale for hc in hcs]
        dp = [_nt(do_ref[:, hc], v_ref[:, hc]) for hc in hcs]
        p = []
        for h in heads:
            e = jnp.exp(s[h] - jnp.max(s[h], axis=-1, keepdims=True))
            p.append(e / jnp.sum(e, axis=-1, keepdims=True))
        dvh = [_tn(p[h].astype(BF16), do_ref[:, hcs[h]]) for h in heads]
        ds = [((p[h] * (dp[h] - jnp.sum(dp[h] * p[h], axis=-1, keepdims=True))) * scale).astype(BF16) for h in heads]
        dqh = [_nn(ds[h], k_ref[:, hcs[h]]) for h in heads]
        dkh = [_tn(ds[h], q_ref[:, hcs[h]]) for h in heads]
        for h in heads:
            dq_ref[:, hcs[h]] = dqh[h].astype(BF16)

        @pl.when(first)
        def _():
            for h in heads:
                dkv_ref[:, hcs[h]] = dkh[h]
                dkv_ref[:, D + h * hd:D + (h + 1) * hd] = dvh[h]

        @pl.when(jnp.logical_not(first))
        def _():
            for h in heads:
                dkv_ref[:, hcs[h]] += dkh[h]
                dkv_ref[:, D + h * hd:D + (h + 1) * hd] += dvh[h]

    row = pl.BlockSpec((tm, D), lambda i: (i, 0))
    return pl.pallas_call(body, name=name, grid=(S // tm,),
                          in_specs=[row, pl.BlockSpec((M, D), lambda i: (0, 0)), pl.BlockSpec((M, D), lambda i: (0, 1)), row],
                          out_specs=[row, pl.BlockSpec((M, 2 * D), lambda i: (0, 0))],
                          out_shape=[jax.ShapeDtypeStruct((S, D), BF16), jax.ShapeDtypeStruct((M, 2 * D), F32)],
                          compiler_params=_params())(q, kv, kv, do)


def _local_step(x, mem, target, vec, weight, emit, dep0):
    g = {}
    pending = []
    begun = []
    summed = []
    emit_begin, emit_finish, emit_send, early_update = emit

    def behind(fn, *a, **kw):
        dep = tuple(pending)
        pending.clear()
        out = fn(*a, dep=dep, **kw)
        while summed:
            pending.append(emit_send(summed.pop(0)))
        while begun:
            name = begun.pop(0)
            token = emit_finish(name, out)
            if token is None:
                summed.append(name)
            else:
                pending.append(token)
        return out

    def mm(a, b, **kw):
        return behind(_matmul, a, b, **kw)

    def send(name, gfull):
        pending.append(emit_begin(name, gfull))
        begun.append(name)

    def ffn_fwd(xin, tag, dep):
        h = _rms_fwd(xin, vec[f"{tag}_norm"], f"{tag}_norm", dep=dep)
        ga, gb, hid = _ffn_up(h, weight(f"{tag}_w_gate", h), weight(f"{tag}_w_up", h), f"{tag}_up")
        wd = weight(f"{tag}_w_down", hid)
        G, Fj, D = wd.shape
        xo = _matmul(hid, wd.reshape(G * Fj, D), mode="nn", name=f"{tag}_down", out_dtype=F32, res=xin, scale=0.5,
                     tn=1024, tk=G * Fj)
        return xo, (h, ga, gb, hid)

    def ffn_bwd(dxh, saved, tag, kept_back=None):
        h, ga, gb, hid = saved
        wg, wu, wd = weight(f"{tag}_w_gate"), weight(f"{tag}_w_up"), weight(f"{tag}_w_down")
        G, Fj, D = wd.shape
        send(f"{tag}_w_down", mm(hid, dxh, mode="tn", name=f"{tag}_dwd", out_dtype=F32, tm=Fj, tn=1024))
        da, db = _ffn_dact(dxh, wd, ga, gb, f"{tag}_dact")
        send(f"{tag}_w_gate", mm(h, da, mode="tn", name=f"{tag}_dwg", out_dtype=F32, tm=1024, tn=Fj, tk=TOKENS_PER_STEP,
                                  out_groups=G))
        send(f"{tag}_w_up", mm(h, db, mode="tn", name=f"{tag}_dwu", out_dtype=F32, tm=1024, tn=Fj, tk=TOKENS_PER_STEP,
                                out_groups=G))
        if kept_back is not None:
            name, gfull = kept_back()
            pending.append(emit_begin(name, gfull))
            pending.append(emit_finish(name, pending[-1]))
        return behind(_ffn_dh, da, db, wg, wu, f"{tag}_dh")

    x1, ffn1_saved = ffn_fwd(x, "ffn1", dep0)
    h2 = _rms_fwd(x1, vec["mix_norm"], "mix_norm")
    w_in = weight("w_in", h2)
    proj = _matmul(h2, w_in, mode="nn", name="w_in", out_dtype=F32, tn=1408)
    pool_w, w_a2 = weight("pool_w", h2), weight("gla_w_a2", h2)
    y_pool, dpool = _pool_fwd(proj, pool_w, vec["pool_scale"], "pool_fwd")
    ymix, states = _gla_fwd(proj, y_pool, w_a2, vec["gla_b_a"], vec["gla_head_norm"], "gla_fwd")
    w_out = weight("w_out", ymix)
    x2 = _matmul(ymix, w_out, mode="nn", name="w_out", out_dtype=F32, res=x1)
    h3 = _rms_fwd(x2, vec["xattn_norm"], "xattn_norm")
    mh = _rms_fwd(mem, vec["mem_norm"], "mem_norm")
    w_q = weight("xattn_w_q", h3)
    q = _matmul(h3, w_q, mode="nn", name="xattn_q", out_dtype=BF16)
    w_kv = weight("xattn_w_kv", q)
    kv = _matmul(mh, w_kv, mode="nn", name="xattn_kv", out_dtype=BF16, b_groups=True, tn=1024)
    o = _xattn_fwd(q, kv, "xattn_fwd")
    w_o = weight("xattn_w_o", o)
    x3 = _matmul(o, w_o, mode="nn", name="xattn_o", out_dtype=F32, res=x2)
    x4, ffn2_saved = ffn_fwd(x3, "ffn2", None)
    sq, dx4, dx4h, g["final_norm"] = _loss_head(x4, vec["final_norm"], target, "loss_head")
    loss = lax.psum(0.5 * jnp.sum(sq) / x.shape[-1], ("x", "y", "c"))
    pending.append(loss.reshape(1, 1))

    dh = ffn_bwd(dx4h, ffn2_saved, "ffn2")
    dx3, dx3b, g["ffn2_norm"] = _rms_bwd(x3, vec["ffn2_norm"], dh, dx4, "ffn2_norm_bwd", lowp=1.0)
    send("xattn_w_o", mm(o, dx3b, mode="tn", name="xattn_dwo", out_dtype=F32, tm=1024, tn=1024, tk=TOKENS_PER_STEP))
    do = mm(dx3b, w_o, mode="nt", name="xattn_do", out_dtype=BF16)
    dq, dkv = _xattn_bwd(q, kv, do, "xattn_bwd")
    send("xattn_w_q", mm(h3, dq, mode="tn", name="xattn_dwq", out_dtype=F32, tm=1024, tn=1024, tk=TOKENS_PER_STEP))
    dh3 = mm(dq, w_q, mode="nt", name="xattn_dh", out_dtype=F32)
    dkvb = _cast(dkv, BF16, "dkv_cast")
    send("xattn_w_kv", mm(mh, dkvb, mode="tn", name="xattn_dwkv", out_dtype=F32, tm=1024, tn=1024, out_groups=N_SHARDS))
    dmh = mm(dkvb, w_kv, mode="nt", name="xattn_dmh", out_dtype=F32, b_groups=True, tk=1024)
    _, g["mem_norm"] = _rms_bwd(mem, vec["mem_norm"], dmh, None, "mem_norm_bwd")
    pending.append(g["mem_norm"])
    dx2, dx2b, g["xattn_norm"] = _rms_bwd(x2, vec["xattn_norm"], dh3, dx3, "xattn_norm_bwd", lowp=1.0)
    send("w_out", mm(ymix, dx2b, mode="tn", name="dw_out", out_dtype=F32, tm=1024, tn=1024, tk=TOKENS_PER_STEP))
    dymix = mm(dx2b, w_out, mode="nt", name="dymix", out_dtype=F32)
    du, dpool_w, g["pool_scale"] = _pool_bwd(dymix, dpool, pool_w, vec["pool_scale"], "pool_bwd")
    send("pool_w", dpool_w)
    dproj, dw_a2, g["gla_b_a"], g["gla_head_norm"] = _gla_bwd(
        proj, states, dymix, du, w_a2, vec["gla_b_a"], vec["gla_head_norm"], "gla_bwd")
    send("gla_w_a2", dw_a2)
    dh2 = mm(dproj, w_in, mode="nt", name="dh2", out_dtype=F32, tn=1024, tk=dproj.shape[1])
    pending.extend(early_update(dh2))
    dx1, dx1h, g["mix_norm"] = _rms_bwd(x1, vec["mix_norm"], dh2, dx2, "mix_norm_bwd", lowp=0.5)
    dh = ffn_bwd(dx1h, ffn1_saved, "ffn1", kept_back=lambda: (
        "w_in", mm(h2, dproj, mode="tn", name="dw_in", out_dtype=F32, tm=1024, tn=1408, tk=TOKENS_PER_STEP)))
    dx0, g["ffn1_norm"] = _rms_bwd(x, vec["ffn1_norm"], dh, dx1, "ffn1_norm_bwd")
    return loss, dx0, g


def _place():
    x, y, c = lax.axis_index("x"), lax.axis_index("y"), lax.axis_index("c")
    chips = [(1 - x, y), (x, 1 - y), (1 - x, 1 - y)]
    return x, y, c, chips


def _ids():
    return jnp.stack([2 * lax.axis_index("x") + lax.axis_index("y"), lax.axis_index("c")]).astype(jnp.int32)


def _hbm(a):
    return pltpu.with_memory_space_constraint(a, pltpu.HBM)


def _cast_to_slot(w2d, dtype, name, dep=None):
    R, C = w2d.shape
    tr = _tile(R, max(16, (4 << 20) // (4 * C) // 16 * 16))

    def body(i_ref, w_ref, *rest):
        rest[-1][...] = w_ref[...].astype(dtype)

    in_specs = [pl.BlockSpec((tr, C), lambda r, i: (r, 0))]
    operands = [w2d]
    if dep is not None:
        in_specs.append(pl.BlockSpec(dep.shape, lambda r, i: (0, 0)))
        operands.append(dep)
    grid_spec = pltpu.PrefetchScalarGridSpec(num_scalar_prefetch=1, grid=(R // tr,), in_specs=in_specs,
                                             out_specs=pl.BlockSpec((None, tr, C), lambda r, i: (i[0], r, 0)))
    return pl.pallas_call(body, name=name, grid_spec=grid_spec, out_shape=jax.ShapeDtypeStruct((N_SHARDS, R, C), dtype),
                          compiler_params=_params())(_ids(), *operands)


def _gather_copies(buf_ref, send_sems, recv_sems, incoming, whole):
    x, y, c, chips = _place()
    hr = buf_ref.shape[1] // 2
    copies = []
    for j, (px, py) in enumerate(chips):
        slot = 2 * px + py if incoming else 2 * x + y
        part = buf_ref.at[slot] if whole else buf_ref.at[slot, pl.ds(c * hr, hr), :]
        copies.append(pltpu.make_async_remote_copy(src_ref=part, dst_ref=part, send_sem=send_sems.at[j],
                                                   recv_sem=recv_sems.at[j], device_id=(px, py, c), device_id_type=MESH))
    return copies


def _gather_start(buf, name, whole):
    def body(b_ref, send_sems, recv_sems, b_thru, token):
        for cp in _gather_copies(b_ref, send_sems, recv_sems, False, whole):
            cp.start()
        token[...] = jnp.zeros_like(token)

    return pl.pallas_call(
        body, name=name,
        out_shape=(pltpu.SemaphoreType.DMA((3,)), pltpu.SemaphoreType.DMA((3,)), pltpu.HBM(buf.shape, buf.dtype),
                   jax.ShapeDtypeStruct((8, LANES), F32)),
        in_specs=(HBM,), out_specs=(SEM, SEM, HBM, pl.BlockSpec(memory_space=pltpu.VMEM)), input_output_aliases={0: 2},
        compiler_params=pltpu.CompilerParams(has_side_effects=EFFECT))(_hbm(buf))


def _gather_wait(send_sems, recv_sems, buf_thru, after, name, whole):
    def body(b_ref, send_sems, recv_sems, after_ref, b_out):
        for cp in _gather_copies(b_ref, send_sems, recv_sems, False, whole):
            cp.wait_send()
        for cp in _gather_copies(b_ref, send_sems, recv_sems, True, whole):
            cp.wait_recv()

    return pl.pallas_call(
        body, name=name, out_shape=pltpu.HBM(buf_thru.shape, buf_thru.dtype),
        in_specs=(HBM, SEM, SEM, ANY), out_specs=HBM, input_output_aliases={0: 0},
        compiler_params=pltpu.CompilerParams(has_side_effects=EFFECT))(buf_thru, send_sems, recv_sems, after)


def _gather_forward(buf, name):
    G, R, C = buf.shape
    hr = R // 2

    def body(b_ref, o_ref, send_sems, recv_sems):
        x, y, c, chips = _place()
        copies = []
        for j, (px, py) in enumerate(chips):
            half = o_ref.at[2 * px + py, pl.ds(c * hr, hr), :]
            copies.append(pltpu.make_async_remote_copy(src_ref=half, dst_ref=half, send_sem=send_sems.at[j],
                                                       recv_sem=recv_sems.at[j], device_id=(x, y, 1 - c),
                                                       device_id_type=MESH))
        for cp in copies:
            cp.start()
        for j, (px, py) in enumerate(chips):
            half = o_ref.at[2 * px + py, pl.ds((1 - c) * hr, hr), :]
            pltpu.make_async_remote_copy(src_ref=half, dst_ref=half, send_sem=send_sems.at[j], recv_sem=recv_sems.at[j],
                                         device_id=(x, y, 1 - c), device_id_type=MESH).wait_recv()
        for cp in copies:
            cp.wait_send()

    return pl.pallas_call(body, name=name, in_specs=[ANY], out_specs=ANY, out_shape=jax.ShapeDtypeStruct(buf.shape, buf.dtype),
                          input_output_aliases={0: 0},
                          scratch_shapes=[pltpu.SemaphoreType.DMA((3,)), pltpu.SemaphoreType.DMA((3,))])(buf)


def _pair_copy(g_ref, land_ref, send_sem, recv_sem):
    x, y, c, _ = _place()
    hr = g_ref.shape[1] // 2
    return pltpu.make_async_remote_copy(src_ref=g_ref.at[:, pl.ds((1 - c) * hr, hr), :], dst_ref=land_ref,
                                        send_sem=send_sem, recv_sem=recv_sem, device_id=(x, y, 1 - c), device_id_type=MESH)


def _pair_start(gfull, name):
    G, R, C = gfull.shape

    def body(g_ref, land_ref, send_sem, recv_sem, g_thru, land_thru, token):
        _pair_copy(g_ref, land_ref, send_sem, recv_sem).start()
        token[...] = jnp.zeros_like(token)

    return pl.pallas_call(
        body, name=name,
        out_shape=(pltpu.SemaphoreType.DMA(()), pltpu.SemaphoreType.DMA(()), pltpu.HBM(gfull.shape, F32),
                   pltpu.HBM((G, R // 2, C), F32), jax.ShapeDtypeStruct((8, LANES), F32)),
        in_specs=(HBM, HBM), out_specs=(SEM, SEM, HBM, HBM, pl.BlockSpec(memory_space=pltpu.VMEM)),
        input_output_aliases={0: 2, 1: 3},
        compiler_params=pltpu.CompilerParams(has_side_effects=EFFECT))(_hbm(gfull), _hbm(lax.empty((G, R // 2, C), F32)))


def _pair_wait(send_sem, recv_sem, g_thru, land_thru, after, name):
    def body(g_ref, land_ref, send_sem, recv_sem, after_ref, g_out, land_out):
        cp = _pair_copy(g_ref, land_ref, send_sem, recv_sem)
        cp.wait_send()
        cp.wait_recv()

    return pl.pallas_call(
        body, name=name, out_shape=(pltpu.HBM(g_thru.shape, F32), pltpu.HBM(land_thru.shape, F32)),
        in_specs=(HBM, HBM, SEM, SEM, ANY), out_specs=(HBM, HBM), input_output_aliases={0: 0, 1: 1},
        compiler_params=pltpu.CompilerParams(has_side_effects=EFFECT))(g_thru, land_thru, send_sem, recv_sem, after)


def _pair_add(gfull, other, name):
    G, R, C = gfull.shape
    hr = R // 2
    tr = _tile(hr, max(8, (2 * 1024 * 1024) // (4 * C) // 8 * 8))
    nr = hr // tr
    c = lax.axis_index("c")
    cidx = jnp.reshape(c, (1,)).astype(jnp.int32)

    def body(c_ref, a_ref, b_ref, o_ref):
        o_ref[...] = a_ref[...] + b_ref[...]

    grid_spec = pltpu.PrefetchScalarGridSpec(
        num_scalar_prefetch=1, grid=(G, nr),
        in_specs=[pl.BlockSpec((None, tr, C), lambda g, r, cr: (g, cr[0] * nr + r, 0)),
                  pl.BlockSpec((None, tr, C), lambda g, r, cr: (g, r, 0))],
        out_specs=pl.BlockSpec((None, tr, C), lambda g, r, cr: (g, r, 0)))
    return pl.pallas_call(body, name=name, grid_spec=grid_spec, out_shape=jax.ShapeDtypeStruct((G, hr, C), F32),
                          compiler_params=_params())(cidx, gfull, other)


def _chip_copies(p_ref, land_ref, send_sems, recv_sems, incoming):
    x, y, c, chips = _place()
    me = 2 * x + y
    copies = []
    for j, (px, py) in enumerate(chips):
        dst = land_ref.at[2 * px + py] if incoming else land_ref.at[me]
        copies.append(pltpu.make_async_remote_copy(src_ref=p_ref.at[2 * px + py], dst_ref=dst, send_sem=send_sems.at[j],
                                                   recv_sem=recv_sems.at[j], device_id=(px, py, c), device_id_type=MESH))
    return copies


def _chip_start(part, name):
    def body(p_ref, land_ref, send_sems, recv_sems, p_thru, land_thru, token):
        for cp in _chip_copies(p_ref, land_ref, send_sems, recv_sems, False):
            cp.start()
        token[...] = jnp.zeros_like(token)

    return pl.pallas_call(
        body, name=name,
        out_shape=(pltpu.SemaphoreType.DMA((3,)), pltpu.SemaphoreType.DMA((3,)), pltpu.HBM(part.shape, F32),
                   pltpu.HBM(part.shape, F32), jax.ShapeDtypeStruct((8, LANES), F32)),
        in_specs=(HBM, HBM), out_specs=(SEM, SEM, HBM, HBM, pl.BlockSpec(memory_space=pltpu.VMEM)),
        input_output_aliases={0: 2, 1: 3},
        compiler_params=pltpu.CompilerParams(has_side_effects=EFFECT))(_hbm(part), _hbm(lax.empty(part.shape, F32)))


def _chip_wait(send_sems, recv_sems, p_thru, land_thru, after, name):
    def body(p_ref, land_ref, send_sems, recv_sems, after_ref, p_out, land_out):
        for cp in _chip_copies(p_ref, land_ref, send_sems, recv_sems, False):
            cp.wait_send()
        for cp in _chip_copies(p_ref, land_ref, send_sems, recv_sems, True):
            cp.wait_recv()

    return pl.pallas_call(
        body, name=name, out_shape=(pltpu.HBM(p_thru.shape, F32), pltpu.HBM(p_thru.shape, F32)),
        in_specs=(HBM, HBM, SEM, SEM, ANY), out_specs=(HBM, HBM), input_output_aliases={0: 0, 1: 1},
        compiler_params=pltpu.CompilerParams(has_side_effects=EFFECT))(p_thru, land_thru, send_sems, recv_sems, after)


def _chip_sum(part, slots, name):
    G, R2, C = part.shape
    tr = _tile(R2, max(8, (1 << 20) // (4 * C) // 8 * 8))
    nr = R2 // tr

    def body(i_ref, p_ref, *rest):
        o_ref = rest[-1]
        acc = None
        for u in range(G):
            val = jnp.where(i_ref[0] == u, p_ref[...], rest[u][...])
            acc = val if acc is None else acc + val
        o_ref[...] = acc

    def slot_spec(u):
        return pl.BlockSpec((None, tr, C), lambda r, i: (jnp.where(i[0] == u, (u + 1) % G, u), r, 0))

    grid_spec = pltpu.PrefetchScalarGridSpec(
        num_scalar_prefetch=1, grid=(nr,),
        in_specs=[pl.BlockSpec((None, tr, C), lambda r, i: (i[0], r, 0))] + [slot_spec(u) for u in range(G)],
        out_specs=pl.BlockSpec((tr, C), lambda r, i: (i[1] * nr + r, 0)))
    return pl.pallas_call(body, name=name, grid_spec=grid_spec, out_shape=jax.ShapeDtypeStruct((2 * R2, C), F32),
                          compiler_params=_params())(_ids(), part, slots, slots, slots, slots)


def _sum_slots(slots, name):
    G, R2, C = slots.shape
    tr = _tile(R2, max(8, (1024 * 1024) // (4 * C) // 8 * 8))

    def body(s_ref, o_ref):
        acc = s_ref[0]
        for u in range(1, G):
            acc = acc + s_ref[u]
        o_ref[...] = acc

    return pl.pallas_call(body, name=name, grid=(R2 // tr,), in_specs=[pl.BlockSpec((G, tr, C), lambda r: (0, r, 0))],
                          out_specs=pl.BlockSpec((tr, C), lambda r: (r, 0)), out_shape=jax.ShapeDtypeStruct((R2, C), F32),
                          compiler_params=_params())(slots)


def _pair_join(full, name):
    R, C = full.shape
    R2 = R // 2

    def body(f_ref, o_ref, token, send_sem, recv_sem):
        x, y, c, _ = _place()
        token[...] = jnp.zeros_like(token)
        mine = o_ref.at[pl.ds(c * R2, R2), :]
        theirs = o_ref.at[pl.ds((1 - c) * R2, R2), :]
        cp = pltpu.make_async_remote_copy(src_ref=mine, dst_ref=mine, send_sem=send_sem, recv_sem=recv_sem,
                                          device_id=(x, y, 1 - c), device_id_type=MESH)
        cp.start()
        pltpu.make_async_remote_copy(src_ref=theirs, dst_ref=theirs, send_sem=send_sem, recv_sem=recv_sem,
                                     device_id=(x, y, 1 - c), device_id_type=MESH).wait_recv()
        cp.wait_send()

    return pl.pallas_call(body, name=name, in_specs=[ANY], out_specs=[ANY, pl.BlockSpec(memory_space=pltpu.VMEM)],
                          out_shape=[jax.ShapeDtypeStruct((R, C), F32), jax.ShapeDtypeStruct((8, LANES), F32)],
                          input_output_aliases={0: 0},
                          scratch_shapes=[pltpu.SemaphoreType.DMA, pltpu.SemaphoreType.DMA])(full)


def _all_reduce_small(v, name):
    R, C = v.shape

    def gather_body(v_ref, out_ref, send_sems, recv_sems, local_sem):
        x, y, c, _ = _place()
        me = 4 * x + 2 * y + c
        mine = pltpu.make_async_copy(v_ref, out_ref.at[me], local_sem)
        mine.start()
        flips = [(fx, fy, fc) for fx in (0, 1) for fy in (0, 1) for fc in (0, 1)][1:]
        copies = []
        for j, (fx, fy, fc) in enumerate(flips):
            peer = (x ^ fx, y ^ fy, c ^ fc)
            copies.append(pltpu.make_async_remote_copy(src_ref=v_ref, dst_ref=out_ref.at[me], send_sem=send_sems.at[j],
                                                       recv_sem=recv_sems.at[j], device_id=peer, device_id_type=MESH))
        for cp in copies:
            cp.start()
        for j, (fx, fy, fc) in enumerate(flips):
            peer = (x ^ fx, y ^ fy, c ^ fc)
            pltpu.make_async_remote_copy(src_ref=v_ref, dst_ref=out_ref.at[4 * peer[0] + 2 * peer[1] + peer[2]],
                                         send_sem=send_sems.at[j], recv_sem=recv_sems.at[j], device_id=peer,
                                         device_id_type=MESH).wait_recv()
        for cp in copies:
            cp.wait_send()
        mine.wait()

    slots = pl.pallas_call(gather_body, name=name, in_specs=[ANY], out_specs=ANY,
                           out_shape=jax.ShapeDtypeStruct((8, R, C), F32),
                           scratch_shapes=[pltpu.SemaphoreType.DMA((7,)), pltpu.SemaphoreType.DMA((7,)),
                                           pltpu.SemaphoreType.DMA])(v)
    return _sum_slots(slots, f"{name}_sum")


def _adamw(w, g, m, v, name, dep=()):
    R, C = w.shape
    tr = _tile(R, max(8, (2 << 20) // (4 * C) // 8 * 8))
    bc1 = 1.0 - ADAM_B1 ** ADAM_STEP
    bc2 = 1.0 - ADAM_B2 ** ADAM_STEP

    def body(w_ref, g_ref, m_ref, v_ref, *rest):
        go_ref, d_ref, nm_ref, nv_ref = rest[len(dep):]
        gv = g_ref[...]
        go_ref[...] = gv
        nm = ADAM_B1 * m_ref[...] + (1.0 - ADAM_B1) * gv
        nv = ADAM_B2 * v_ref[...] + (1.0 - ADAM_B2) * (gv * gv)
        nm_ref[...] = nm
        nv_ref[...] = nv
        d_ref[...] = -ADAM_LR * ((nm / bc1) / (jnp.sqrt(nv / bc2) + ADAM_EPS) + ADAM_WD * w_ref[...])

    blk = pl.BlockSpec((tr, C), lambda r: (r, 0))
    out = jax.ShapeDtypeStruct((R, C), F32)
    in_specs = [blk] * 4 + [pl.BlockSpec(d.shape, lambda r: (0, 0)) for d in dep]
    return pl.pallas_call(body, name=name, grid=(R // tr,), in_specs=in_specs, out_specs=[blk] * 4, out_shape=[out] * 4,
                          compiler_params=_params())(w, g, m, v, *dep)


SC_TILES = 32
SC_LANES = 16
SC_ROWS = 8


def _sc_mesh():
    return plsc.VectorSubcoreMesh(core_axis_name="sc_core", subcore_axis_name="sc_subcore")


def _pair_add_sc(gfull, other, name):
    G, R, C = gfull.shape
    hr = R // 2
    tiles_per_shard = SC_TILES // G
    per_tile = hr // SC_ROWS // tiles_per_shard

    def body(g_hbm, o_hbm, out_hbm, gb, ob):
        c = lax.axis_index("c")
        tile = lax.axis_index("sc_subcore") * 2 + lax.axis_index("sc_core")
        t = tile // tiles_per_shard
        first = (tile % tiles_per_shard) * per_tile

        @pl.loop(0, per_tile)
        def _(k):
            rr = (first + k) * SC_ROWS
            pltpu.sync_copy(g_hbm.at[t, pl.ds(c * hr + rr, SC_ROWS), :], gb)
            pltpu.sync_copy(o_hbm.at[t, pl.ds(rr, SC_ROWS), :], ob)

            @pl.loop(0, SC_ROWS)
            def _(i):
                @pl.loop(0, C, step=SC_LANES)
                def _(j):
                    at = (i, pl.ds(j, SC_LANES))
                    gb[at] = gb[at] + ob[at]

            pltpu.sync_copy(gb, out_hbm.at[t, pl.ds(rr, SC_ROWS), :])

    buf = pltpu.VMEM((SC_ROWS, C), F32)
    return pl.kernel(body, name=name, out_type=jax.ShapeDtypeStruct((G, hr, C), F32), mesh=_sc_mesh(),
                     scratch_types=[buf, buf])(gfull, other)


def _adamw_sc(w, g, m, v, name):
    R, C = w.shape
    tasks = R // SC_ROWS
    bc1 = 1.0 - ADAM_B1 ** ADAM_STEP
    bc2 = 1.0 - ADAM_B2 ** ADAM_STEP

    def body(w_hbm, g_hbm, m_hbm, v_hbm, go_hbm, d_hbm, nm_hbm, nv_hbm, wb, gb, mb, vb):
        tile = lax.axis_index("sc_subcore") * 2 + lax.axis_index("sc_core")

        @pl.loop((tile * tasks) // SC_TILES, ((tile + 1) * tasks) // SC_TILES)
        def _(task):
            rows = pl.ds(task * SC_ROWS, SC_ROWS)
            pltpu.sync_copy(w_hbm.at[rows, :], wb)
            pltpu.sync_copy(g_hbm.at[rows, :], gb)
            pltpu.sync_copy(m_hbm.at[rows, :], mb)
            pltpu.sync_copy(v_hbm.at[rows, :], vb)

            @pl.loop(0, SC_ROWS)
            def _(i):
                @pl.loop(0, C, step=SC_LANES)
                def _(j):
                    at = (i, pl.ds(j, SC_LANES))
                    gv = gb[at]
                    nm = ADAM_B1 * mb[at] + (1.0 - ADAM_B1) * gv
                    nv = ADAM_B2 * vb[at] + (1.0 - ADAM_B2) * (gv * gv)
                    mb[at] = nm
                    vb[at] = nv
                    wb[at] = -ADAM_LR * ((nm / bc1) / (jnp.sqrt(nv / bc2) + ADAM_EPS) + ADAM_WD * wb[at])

            pltpu.sync_copy(gb, go_hbm.at[rows, :])
            pltpu.sync_copy(wb, d_hbm.at[rows, :])
            pltpu.sync_copy(mb, nm_hbm.at[rows, :])
            pltpu.sync_copy(vb, nv_hbm.at[rows, :])

    out = jax.ShapeDtypeStruct((R, C), F32)
    buf = pltpu.VMEM((SC_ROWS, C), F32)
    return pl.kernel(body, name=name, out_type=(out, out, out, out), mesh=_sc_mesh(),
                     scratch_types=[buf, buf, buf, buf])(w, g, m, v)


WEIGHTS = ['ffn1_norm', 'ffn1_w_gate', 'ffn1_w_up', 'ffn1_w_down', 'mix_norm', 'w_in', 'pool_w', 'pool_scale', 'gla_w_a2',
           'gla_b_a', 'gla_head_norm', 'w_out', 'xattn_norm', 'mem_norm', 'xattn_w_q', 'xattn_w_kv', 'xattn_w_o', 'ffn2_norm',
           'ffn2_w_gate', 'ffn2_w_up', 'ffn2_w_down', 'final_norm']
SHARDED = ['ffn1_w_gate', 'ffn1_w_up', 'ffn1_w_down', 'w_in', 'pool_w', 'gla_w_a2', 'w_out', 'xattn_w_q', 'xattn_w_kv',
           'xattn_w_o', 'ffn2_w_gate', 'ffn2_w_up', 'ffn2_w_down']
REPLICATED = [n for n in WEIGHTS if n not in SHARDED]
ON_SPARSECORE = ['ffn2_w_gate', 'ffn2_w_up', 'w_out', 'xattn_w_q', 'xattn_w_kv', 'xattn_w_o']
PAIR_SUM_ON_SPARSECORE = ['ffn2_w_down', 'ffn2_w_gate', 'ffn2_w_up', 'xattn_w_o', 'xattn_w_kv', 'pool_w', 'ffn1_w_down',
                          'ffn1_w_gate']
SMALL_COLS = 512


def _as2d(a):
    return a.reshape(-1, a.shape[-1])


def _finish_weight(name, gathered, wl):
    G, R, C = gathered.shape
    rank = wl["gla_w_a2"].shape[1]
    if name in ("w_out", "xattn_w_q", "xattn_w_o"):
        return gathered.reshape(G * R, C)
    if name == "w_in":
        w_in = jnp.transpose(gathered, (1, 0, 2)).reshape(R, G * C)
        main = G * C - rank
        return jnp.concatenate([w_in[:, :main], jnp.pad(w_in[:, main:], ((0, 0), (0, LANES - rank)))], axis=1)
    if name == "pool_w":
        NG, CJ, _ = wl[name].shape[1:]
        return jnp.transpose(gathered.reshape(G, NG, CJ, C), (1, 0, 2, 3)).reshape(NG, G * CJ, C)
    if name == "gla_w_a2":
        a2 = jnp.transpose(gathered, (1, 0, 2)).reshape(rank, G * C)
        return jnp.pad(a2, ((0, LANES - rank), (0, 0))).astype(BF16)
    return gathered


def _start_gathers(wl):
    started = {}
    token = None
    for n in SHARDED:
        whole = n not in ("ffn1_w_gate", "ffn1_w_up")
        buf = _cast_to_slot(_as2d(wl[n]), BF16, f"slot_{n}", dep=token)
        send_sems, recv_sems, thru, token = _gather_start(buf, f"gather_start_{n}", whole)
        started[n] = (send_sems, recv_sems, thru, whole)
    cache = {}

    def weight(n, after=None):
        if n not in cache:
            *handles, whole = started[n]
            buf = _gather_wait(*handles, after, f"gather_wait_{n}", whole)
            if not whole:
                buf = _gather_forward(buf, f"gather_forward_{n}")
            cache[n] = _finish_weight(n, buf, wl)
        return cache[n]

    return weight, token


def _shard_major(name, gfull, wl):
    R, C = _as2d(wl[name]).shape
    if name in ("ffn1_w_gate", "ffn1_w_up", "ffn2_w_gate", "ffn2_w_up", "xattn_w_kv"):
        return gfull
    if name in ("ffn1_w_down", "ffn2_w_down", "w_out", "xattn_w_q", "xattn_w_o"):
        return gfull.reshape(N_SHARDS, R, C)
    if name == "w_in":
        return jnp.transpose(gfull[:, :N_SHARDS * C].reshape(R, N_SHARDS, C), (1, 0, 2))
    if name == "pool_w":
        NG, CJ, _ = wl[name].shape[1:]
        return jnp.transpose(gfull.reshape(NG, N_SHARDS, CJ, C), (1, 0, 2, 3)).reshape(N_SHARDS, R, C)
    assert name == "gla_w_a2"
    return jnp.transpose(gfull[:R].reshape(R, N_SHARDS, C), (1, 0, 2))


def kernel(x, mem, ffn1_norm, ffn1_w_gate, ffn1_w_up, ffn1_w_down, mix_norm, w_in, pool_w, pool_scale, gla_w_a2, gla_b_a, gla_head_norm, w_out, xattn_norm, mem_norm, xattn_w_q, xattn_w_kv, xattn_w_o, ffn2_norm, ffn2_w_gate, ffn2_w_up, ffn2_w_down, final_norm, loss_target, m_ffn1_norm, m_ffn1_w_gate, m_ffn1_w_up, m_ffn1_w_down, m_mix_norm, m_w_in, m_pool_w, m_pool_scale, m_gla_w_a2, m_gla_b_a, m_gla_head_norm, m_w_out, m_xattn_norm, m_mem_norm, m_xattn_w_q, m_xattn_w_kv, m_xattn_w_o, m_ffn2_norm, m_ffn2_w_gate, m_ffn2_w_up, m_ffn2_w_down, m_final_norm, v_ffn1_norm, v_ffn1_w_gate, v_ffn1_w_up, v_ffn1_w_down, v_mix_norm, v_w_in, v_pool_w, v_pool_scale, v_gla_w_a2, v_gla_b_a, v_gla_head_norm, v_w_out, v_xattn_norm, v_mem_norm, v_xattn_w_q, v_xattn_w_kv, v_xattn_w_o, v_ffn2_norm, v_ffn2_w_gate, v_ffn2_w_up, v_ffn2_w_down, v_final_norm):
    given = dict(locals())
    wl = {n: given[n] for n in WEIGHTS}
    ml = {n: given["m_" + n] for n in WEIGHTS}
    vl = {n: given["v_" + n] for n in WEIGHTS}

    vec = {n: wl[n].reshape(1, -1) for n in REPLICATED}
    weight, dep0 = _start_gathers(wl)
    in_flight = {}

    pair_flight = {}

    def emit_begin(n, gfull):
        *pair_flight[n], token = _pair_start(_shard_major(n, gfull, wl), f"{n}_pair_start")
        return token

    summing = {}

    def emit_finish(n, after):
        gsm, other = _pair_wait(*pair_flight.pop(n), after, f"{n}_pair_wait")
        if n in PAIR_SUM_ON_SPARSECORE:
            summing[n] = _pair_add_sc(gsm, other, f"{n}_pair_add_sc")
            return None
        *in_flight[n], token = _chip_start(_pair_add(gsm, other, f"{n}_pair_add"), f"{n}_chip_start")
        return token

    def emit_send(n):
        *in_flight[n], token = _chip_start(summing.pop(n), f"{n}_chip_start")
        return token

    grads = {}
    updates = {}

    def reduce_done(n, after):
        part, slots = _chip_wait(*in_flight.pop(n), after, f"{n}_chip_wait")
        grads[n], token = _pair_join(_chip_sum(part, slots, f"{n}_chip_sum"), f"{n}_pair_join")
        return token

    def early_update(after):
        tokens = [reduce_done(n, after) for n in ON_SPARSECORE]
        for n in ON_SPARSECORE:
            g2 = grads[n]
            updates[n] = _adamw_sc(wl[n].reshape(g2.shape), g2, ml[n].reshape(g2.shape), vl[n].reshape(g2.shape),
                                   f"adamw_sc_{n}")
        return tokens

    loss, dx0, g = _local_step(x[0], mem[0], loss_target[0], vec, weight,
                               (emit_begin, emit_finish, emit_send, early_update), dep0)
    assert not summing

    for n in list(in_flight):
        reduce_done(n, dx0)
    widths = [wl[n].size for n in REPLICATED]
    total = sum(widths)
    rows = -(-total // SMALL_COLS)
    rows = -(-rows // 8) * 8
    packed = jnp.concatenate([g[n].reshape(-1) for n in REPLICATED] + [jnp.zeros((rows * SMALL_COLS - total,), F32)])
    summed = _all_reduce_small(packed.reshape(rows, SMALL_COLS), "small_all_reduce").reshape(-1)
    off = 0
    for n, width in zip(REPLICATED, widths):
        grads[n] = summed[off:off + width].reshape(1, width)
        off += width

    out_g, out_d, out_m, out_v = [], [], [], []
    for n in WEIGHTS:
        shape = wl[n].shape
        g2 = grads[n]
        if n in updates:
            go, d, nm, nv = updates[n]
        else:
            dep = tuple(updates[k][1][:8, :LANES] for k in updates) if n == "w_in" else ()
            go, d, nm, nv = _adamw(wl[n].reshape(g2.shape), g2, ml[n].reshape(g2.shape), vl[n].reshape(g2.shape),
                                   f"adamw_{n}", dep)
        out_g.append(go.reshape(shape))
        out_d.append(d.reshape(shape))
        out_m.append(nm.reshape(shape))
        out_v.append(nv.reshape(shape))
    return (loss, dx0.reshape(x.shape), *out_g, *out_d, *out_m, *out_v)
```

```python
import functools

import jax
import jax.numpy as jnp
from jax import lax
from jax.experimental import pallas as pl
from jax.experimental.pallas import tpu as pltpu
from jax.experimental.pallas import tpu_sc as plsc

F32 = jnp.float32
BF16 = jnp.bfloat16
MESH = pl.DeviceIdType.MESH

RMS_EPS = 1e-6
CHUNK = 64
POOL_WINDOWS = (2, 4, 8, 16)
POOL_HALO = 16
N_HEADS = 4
GATE_TEMP = 16.0
ADAM_LR, ADAM_B1, ADAM_B2, ADAM_EPS, ADAM_WD, ADAM_STEP = 0.001, 0.9, 0.999, 1e-08, 0.01, 10
N_SHARDS = 4
LANES = 128
MXU_COLS = 256
TOKENS_PER_STEP = 2048
VMEM_LIMIT = 58 * 1024 * 1024

ANY = pl.BlockSpec(memory_space=pl.ANY)
HBM = pl.BlockSpec(memory_space=pltpu.HBM)
SEM = pl.BlockSpec(memory_space=pltpu.SEMAPHORE)
EFFECT = pltpu.SideEffectType.DATAFLOW_SIDE_EFFECTING


def _params(**kw):
    return pltpu.CompilerParams(vmem_limit_bytes=VMEM_LIMIT, **kw)


def _tile(n, want):
    for unit in (LANES, 8):
        t = (min(want, n) // unit) * unit
        while t >= unit:
            if n % t == 0:
                return t
            t -= unit
    return n


def _dot(a, b, dims):
    return lax.dot_general(a, b, (dims, ((), ())), preferred_element_type=F32)


def _nn(a, b):
    return _dot(a, b, ((1,), (0,)))


def _nt(a, b):
    return _dot(a, b, ((1,), (1,)))


def _tn(a, b):
    return _dot(a, b, ((0,), (0,)))


def _sigmoid(x):
    return 1.0 / (1.0 + jnp.exp(-x))


def _matmul(a, b, *, mode, name, out_dtype, tm=512, tn=2048, tk=2048, res=None, scale=1.0, b_groups=False, out_groups=0,
            dep=()):
    if mode == "tn":
        K, M = a.shape
    else:
        M, K = a.shape
    if mode == "nn":
        if b_groups:
            G, _, Nj = b.shape
            N = G * Nj
        else:
            N = b.shape[1]
    elif mode == "nt":
        if b_groups:
            G, N, Kj = b.shape
            assert G * Kj == K
        else:
            N = b.shape[0]
    else:
        N = b.shape[1]
    tm = _tile(M, tm)
    if mode == "nn" and b_groups:
        tn = _tile(Nj, tn)
    elif out_groups:
        tn = _tile(N // out_groups, tn)
    else:
        tn = _tile(N, tn)
    if mode == "nt" and b_groups:
        tk = _tile(Kj, tk)
    else:
        tk = _tile(K, tk)
    nk = K // tk
    grid = (M // tm, N // tn, nk)

    if mode == "tn":
        a_spec = pl.BlockSpec((tk, tm), lambda i, j, k: (k, i))
        b_spec = pl.BlockSpec((tk, tn), lambda i, j, k: (k, j))
        dims = ((0,), (0,))
    elif mode == "nn":
        a_spec = pl.BlockSpec((tm, tk), lambda i, j, k: (i, k))
        if b_groups:
            npj = Nj // tn
            b_spec = pl.BlockSpec((None, tk, tn), lambda i, j, k: (j // npj, k, j % npj))
        else:
            b_spec = pl.BlockSpec((tk, tn), lambda i, j, k: (k, j))
        dims = ((1,), (0,))
    else:
        a_spec = pl.BlockSpec((tm, tk), lambda i, j, k: (i, k))
        if b_groups:
            kpj = Kj // tk
            b_spec = pl.BlockSpec((None, tn, tk), lambda i, j, k: (k // kpj, j, k % kpj))
        else:
            b_spec = pl.BlockSpec((tn, tk), lambda i, j, k: (j, k))
        dims = ((1,), (1,))
    if out_groups:
        npj = (N // out_groups) // tn
        o_spec = pl.BlockSpec((None, tm, tn), lambda i, j, k: (j // npj, i, j % npj))
        out_shape = jax.ShapeDtypeStruct((out_groups, M, N // out_groups), out_dtype)
    else:
        o_spec = pl.BlockSpec((tm, tn), lambda i, j, k: (i, j))
        out_shape = jax.ShapeDtypeStruct((M, N), out_dtype)
    in_specs = [a_spec, b_spec]
    operands = [a, b]
    if res is not None:
        in_specs.append(pl.BlockSpec((tm, tn), lambda i, j, k: (i, j)))
        operands.append(res)
    has_res = res is not None
    n_dep = len(dep)
    for d in dep:
        in_specs.append(pl.BlockSpec(d.shape, lambda i, j, k: (0, 0)))
        operands.append(d)

    def body(*refs):
        if has_res:
            a_ref, b_ref, r_ref = refs[:3]
        else:
            a_ref, b_ref = refs[:2]
            r_ref = None
        o_ref = refs[2 + has_res + n_dep]

        def finish(acc):
            if scale != 1.0:
                acc = acc * scale
            if r_ref is not None:
                acc = r_ref[...] + acc
            o_ref[...] = acc.astype(o_ref.dtype)

        part = _dot(a_ref[...], b_ref[...], dims)
        if nk == 1:
            finish(part)
        else:
            acc_ref = o_ref if in_place else refs[-1]
            k = pl.program_id(2)

            @pl.when(k == 0)
            def _():
                acc_ref[...] = part

            @pl.when(k > 0)
            def _():
                acc_ref[...] += part

            if not in_place:
                @pl.when(k == nk - 1)
                def _():
                    finish(acc_ref[...])

    in_place = out_dtype == F32 and res is None and scale == 1.0
    scratch = [] if nk == 1 or in_place else [pltpu.VMEM((tm, tn), F32)]
    return pl.pallas_call(body, name=name, grid=grid, in_specs=in_specs, out_specs=o_spec, out_shape=out_shape,
                          scratch_shapes=scratch, compiler_params=_params())(*operands)


def _rms_fwd(x, gain, name, tm=512, dep=None):
    S, D = x.shape
    tm = _tile(S, tm)

    def body(x_ref, g_ref, *rest):
        o_ref = rest[-1]
        xv = x_ref[...]
        r = lax.rsqrt(jnp.mean(xv * xv, axis=-1, keepdims=True) + RMS_EPS)
        o_ref[...] = (xv * r * g_ref[...]).astype(o_ref.dtype)

    in_specs = [pl.BlockSpec((tm, D), lambda i: (i, 0)), pl.BlockSpec((1, D), lambda i: (0, 0))]
    operands = [x, gain]
    if dep is not None:
        in_specs.append(pl.BlockSpec(dep.shape, lambda i: (0, 0)))
        operands.append(dep)
    return pl.pallas_call(body, name=name, grid=(S // tm,), in_specs=in_specs,
                          out_specs=pl.BlockSpec((tm, D), lambda i: (i, 0)),
                          out_shape=jax.ShapeDtypeStruct((S, D), BF16), compiler_params=_params())(*operands)


def _rms_bwd(x, gain, dh, dres, name, lowp=None, tm=512):
    half = lowp is not None
    S, D = x.shape
    tm = _tile(S, tm)
    has_res = dres is not None

    def body(*refs):
        if has_res:
            x_ref, g_ref, dh_ref, dr_ref = refs[:4]
            outs = refs[4:]
        else:
            x_ref, g_ref, dh_ref = refs[:3]
            dr_ref = None
            outs = refs[3:]
        dx_ref, dg_ref = outs[0], outs[-1]
        xv = x_ref[...]
        dhv = dh_ref[...].astype(F32)
        r = lax.rsqrt(jnp.mean(xv * xv, axis=-1, keepdims=True) + RMS_EPS)
        gy = dhv * g_ref[...]
        dx = r * gy - xv * (r * r * r) * jnp.mean(gy * xv, axis=-1, keepdims=True)
        if dr_ref is not None:
            dx = dx + dr_ref[...]
        dx_ref[...] = dx
        if half:
            outs[1][...] = (dx if lowp == 1.0 else lowp * dx).astype(BF16)
        part = jnp.sum(dhv * xv * r, axis=0, keepdims=True)

        @pl.when(pl.program_id(0) == 0)
        def _():
            dg_ref[...] = part

        @pl.when(pl.program_id(0) > 0)
        def _():
            dg_ref[...] += part

    row = pl.BlockSpec((tm, D), lambda i: (i, 0))
    vec = pl.BlockSpec((1, D), lambda i: (0, 0))
    in_specs = [row, vec, row] + ([row] if has_res else [])
    operands = [x, gain, dh] + ([dres] if has_res else [])
    out_specs = [row] + ([row] if half else []) + [vec]
    out_shape = [jax.ShapeDtypeStruct((S, D), F32)] + ([jax.ShapeDtypeStruct((S, D), BF16)] if half else []) + [
        jax.ShapeDtypeStruct((1, D), F32)]
    return pl.pallas_call(body, name=name, grid=(S // tm,), in_specs=in_specs, out_specs=out_specs, out_shape=out_shape,
                          compiler_params=_params())(*operands)


def _loss_head(x, gain, target, name, tm=512):
    S, D = x.shape
    tm = _tile(S, tm)

    def body(x_ref, g_ref, t_ref, sq_ref, dx_ref, dxh_ref, dg_ref):
        xv = x_ref[...]
        r = lax.rsqrt(jnp.mean(xv * xv, axis=-1, keepdims=True) + RMS_EPS)
        xn = xv * r
        err = xn * g_ref[...] - t_ref[...]
        dout = err * (1.0 / D)
        gy = dout * g_ref[...]
        dx = r * gy - xv * (r * r * r) * jnp.mean(gy * xv, axis=-1, keepdims=True)
        dx_ref[...] = dx
        dxh_ref[...] = (0.5 * dx).astype(BF16)
        sq = jnp.sum(err * err, axis=0, keepdims=True)
        dg = jnp.sum(dout * xn, axis=0, keepdims=True)

        @pl.when(pl.program_id(0) == 0)
        def _():
            sq_ref[...] = sq
            dg_ref[...] = dg

        @pl.when(pl.program_id(0) > 0)
        def _():
            sq_ref[...] += sq
            dg_ref[...] += dg

    row = pl.BlockSpec((tm, D), lambda i: (i, 0))
    vec = pl.BlockSpec((1, D), lambda i: (0, 0))
    return pl.pallas_call(body, name=name, grid=(S // tm,), in_specs=[row, vec, row], out_specs=[vec, row, row, vec],
                          out_shape=[jax.ShapeDtypeStruct((1, D), F32), jax.ShapeDtypeStruct((S, D), F32),
                                     jax.ShapeDtypeStruct((S, D), BF16), jax.ShapeDtypeStruct((1, D), F32)],
                          compiler_params=_params())(x, gain, target)


def _cast(x, dtype, name, scale=1.0, tm=256):
    S, D = x.shape
    tm = _tile(S, tm)

    def body(x_ref, o_ref):
        o_ref[...] = (x_ref[...] * scale).astype(o_ref.dtype)

    row = pl.BlockSpec((tm, D), lambda i: (i, 0))
    return pl.pallas_call(body, name=name, grid=(S // tm,), in_specs=[row], out_specs=row,
                          out_shape=jax.ShapeDtypeStruct((S, D), dtype), compiler_params=_params())(x)


def _ffn_up(h, wg, wu, name, tm=512):
    S, D = h.shape
    G, _, Fj = wg.shape
    tm = _tile(S, tm)

    def body(h_ref, wg_ref, wu_ref, ga_ref, gb_ref, hid_ref):
        hv = h_ref[...]
        a = _nn(hv, wg_ref[...])
        b = _nn(hv, wu_ref[...])
        s = _sigmoid(a)
        silu = a * s
        ga_ref[...] = (b * (s * (1.0 + a * (1.0 - s)))).astype(BF16)
        gb_ref[...] = silu.astype(BF16)
        hid_ref[...] = (silu * b).astype(BF16)

    w_spec = pl.BlockSpec((None, D, Fj), lambda g, i: (g, 0, 0))
    o_spec = pl.BlockSpec((tm, Fj), lambda g, i: (i, g))
    out = jax.ShapeDtypeStruct((S, G * Fj), BF16)
    return pl.pallas_call(body, name=name, grid=(G, S // tm),
                          in_specs=[pl.BlockSpec((tm, D), lambda g, i: (i, 0)), w_spec, w_spec],
                          out_specs=[o_spec, o_spec, o_spec], out_shape=[out, out, out], compiler_params=_params())(h, wg, wu)


def _ffn_dact(dxh, wd, ga, gb, name, tm=512):
    S, D = dxh.shape
    G, Fj, _ = wd.shape
    tm = _tile(S, tm)

    def body(dx_ref, wd_ref, ga_ref, gb_ref, da_ref, db_ref):
        dhid = _nt(dx_ref[...], wd_ref[...])
        da_ref[...] = (dhid * ga_ref[...].astype(F32)).astype(BF16)
        db_ref[...] = (dhid * gb_ref[...].astype(F32)).astype(BF16)

    blk = pl.BlockSpec((tm, Fj), lambda g, i: (i, g))
    out = jax.ShapeDtypeStruct((S, G * Fj), BF16)
    return pl.pallas_call(body, name=name, grid=(G, S // tm),
                          in_specs=[pl.BlockSpec((tm, D), lambda g, i: (i, 0)),
                                    pl.BlockSpec((None, Fj, D), lambda g, i: (g, 0, 0)), blk, blk],
                          out_specs=[blk, blk], out_shape=[out, out], compiler_params=_params())(dxh, wd, ga, gb)


def _ffn_dh(da, db, wg, wu, name, dep=(), tm=512):
    S = da.shape[0]
    G, D, Fj = wg.shape
    tm = _tile(S, tm)

    def body(da_ref, db_ref, wg_ref, wu_ref, *rest):
        o_ref = rest[-1]
        part = _nt(da_ref[...], wg_ref[...]) + _nt(db_ref[...], wu_ref[...])

        @pl.when(pl.program_id(1) == 0)
        def _():
            o_ref[...] = part

        @pl.when(pl.program_id(1) > 0)
        def _():
            o_ref[...] += part

    act = pl.BlockSpec((tm, Fj), lambda i, g: (i, g))
    w_spec = pl.BlockSpec((None, D, Fj), lambda i, g: (g, 0, 0))
    in_specs = [act, act, w_spec, w_spec] + [pl.BlockSpec(d.shape, lambda i, g: (0, 0)) for d in dep]
    return pl.pallas_call(body, name=name, grid=(S // tm, G), in_specs=in_specs,
                          out_specs=pl.BlockSpec((tm, D), lambda i, g: (i, 0)),
                          out_shape=jax.ShapeDtypeStruct((S, D), F32), compiler_params=_params())(da, db, wg, wu, *dep)


def _pool_fwd(proj, pool_w, pool_scale, name, tm=512):
    S = proj.shape[0]
    NG, C, _ = pool_w.shape
    DP = NG * C
    tm = _tile(S, tm)
    hb = tm // POOL_HALO
    n_ext = tm + POOL_HALO

    def body(u_ref, halo_ref, w_ref, sc_ref, y_ref, d_ref):
        i = pl.program_id(0)
        t = lax.broadcasted_iota(jnp.int32, (tm, 1), 0) + i * tm
        for g, win in enumerate(POOL_WINDOWS):
            cols = slice(g * C, (g + 1) * C)
            ug = u_ref[:, cols]
            halo = jnp.where(i > 0, halo_ref[:, cols], 0.0)
            acc = jnp.concatenate([halo, ug], axis=0)
            step = 1
            while step < win:
                acc = acc + pltpu.roll(acc, step, 0)
                step *= 2
            count = jnp.minimum(t + 1, win).astype(F32)
            d = (acc[POOL_HALO:, :] / count - ug).astype(BF16)
            d_ref[:, cols] = d
            y_ref[:, cols] = (_nn(d, w_ref[g]) * sc_ref[:, cols]).astype(BF16)

    del n_ext
    return pl.pallas_call(
        body, name=name, grid=(S // tm,),
        in_specs=[pl.BlockSpec((tm, DP), lambda i: (i, 0)),
                  pl.BlockSpec((POOL_HALO, DP), lambda i: (jnp.maximum(i * hb - 1, 0), 0)),
                  pl.BlockSpec((NG, C, C), lambda i: (0, 0, 0)), pl.BlockSpec((1, DP), lambda i: (0, 0))],
        out_specs=[pl.BlockSpec((tm, DP), lambda i: (i, 0)), pl.BlockSpec((tm, DP), lambda i: (i, 0))],
        out_shape=[jax.ShapeDtypeStruct((S, DP), BF16), jax.ShapeDtypeStruct((S, DP), BF16)],
        compiler_params=_params())(proj, proj, pool_w, pool_scale)


def _pool_bwd(dymix, d, pool_w, pool_scale, name, tm=512):
    S = dymix.shape[0]
    NG, C, _ = pool_w.shape
    DP = NG * C
    tm = _tile(S, tm)
    hb = tm // POOL_HALO
    nb = S // tm
    n_ext = tm + POOL_HALO
    last_halo = S // POOL_HALO - 1

    def body(dy_ref, halo_ref, d_ref, w_ref, sc_ref, du_ref, dw_ref, dsc_ref):
        i = pl.program_id(0)
        t = lax.broadcasted_iota(jnp.int32, (n_ext, 1), 0) + i * tm
        for g, win in enumerate(POOL_WINDOWS):
            cols = slice(g * C, (g + 1) * C)
            dy = dy_ref[:, cols]
            halo = jnp.where(i < nb - 1, halo_ref[:, cols], 0.0)
            sc = sc_ref[:, cols]
            dv = d_ref[:, cols]
            e_ext = (jnp.concatenate([dy, halo], axis=0) * sc).astype(BF16)
            dd = _nt(e_ext, w_ref[g])
            count = jnp.minimum(t + 1, win).astype(F32)
            acc = dd / count
            step = 1
            while step < win:
                acc = acc + pltpu.roll(acc, n_ext - step, 0)
                step *= 2
            du_ref[:, cols] = (acc[:tm, :] - dd[:tm, :]).astype(BF16)
            dw = _tn(dv, e_ext[:tm, :])
            dsc = jnp.sum(dy * _nn(dv, w_ref[g]), axis=0, keepdims=True)

            @pl.when(i == 0)
            def _():
                dw_ref[g] = dw
                dsc_ref[:, cols] = dsc

            @pl.when(i > 0)
            def _():
                dw_ref[g] += dw
                dsc_ref[:, cols] += dsc

    return pl.pallas_call(
        body, name=name, grid=(nb,),
        in_specs=[pl.BlockSpec((tm, DP), lambda i: (i, 0)),
                  pl.BlockSpec((POOL_HALO, DP), lambda i: (jnp.minimum((i + 1) * hb, last_halo), 0)),
                  pl.BlockSpec((tm, DP), lambda i: (i, 0)),
                  pl.BlockSpec((NG, C, C), lambda i: (0, 0, 0)), pl.BlockSpec((1, DP), lambda i: (0, 0))],
        out_specs=[pl.BlockSpec((tm, DP), lambda i: (i, 0)), pl.BlockSpec((NG, C, C), lambda i: (0, 0, 0)),
                   pl.BlockSpec((1, DP), lambda i: (0, 0))],
        out_shape=[jax.ShapeDtypeStruct((S, DP), BF16), jax.ShapeDtypeStruct((NG, C, C), F32),
                   jax.ShapeDtypeStruct((1, DP), F32)],
        compiler_params=_params())(dymix, dymix, d, pool_w, pool_scale)


def _chunk_scan(v, rows, reverse):
    n = v.shape[0]
    step = 1
    while step < CHUNK:
        if reverse:
            v = v + jnp.where(rows < CHUNK - step, pltpu.roll(v, n - step, 0), 0.0)
        else:
            v = v + jnp.where(rows >= step, pltpu.roll(v, step, 0), 0.0)
        step *= 2
    return v


def _log_decay(alr, w_a2, b_a):
    z = _nn(alr.astype(BF16), w_a2) + b_a
    la = (jnp.minimum(z, 0.0) - jnp.log(1.0 + jnp.exp(-jnp.abs(z)))) * (1.0 / GATE_TEMP)
    return z, la


def _gla_specs(DP, DKT, DV, tb, bmap):
    return [pl.BlockSpec((tb, DKT), lambda i: (bmap(i), DP // DKT)),
            pl.BlockSpec((tb, DKT), lambda i: (bmap(i), DP // DKT + 1)),
            pl.BlockSpec((tb, DV), lambda i: (bmap(i), (DP + 2 * DKT) // DV)),
            pl.BlockSpec((tb, DV), lambda i: (bmap(i), (DP + 2 * DKT) // DV + 1)),
            pl.BlockSpec((tb, LANES), lambda i: (bmap(i), (DP + 2 * DKT + 2 * DV) // LANES))]


def _gla_fwd(proj, y_pool, w_a2, b_a, head_norm, name, tb=512):
    S = proj.shape[0]
    DP = y_pool.shape[1]
    DKT = b_a.shape[1]
    DV = head_norm.shape[1]
    dk, dv = DKT // N_HEADS, DV // N_HEADS
    tb = _tile(S, tb)
    ncb = tb // CHUNK
    qscale = dk ** -0.5

    def body(q_ref, k_ref, v_ref, g_ref, alr_ref, yp_ref, wa_ref, ba_ref, hn_ref, y_ref, st_out_ref, st_ref, kdec_ref,
             gam_ref):
        @pl.when(pl.program_id(0) == 0)
        def _():
            st_ref[...] = jnp.zeros_like(st_ref)

        y_ref[:, :DP] = yp_ref[...]

        rows = lax.broadcasted_iota(jnp.int32, (tb, 1), 0) % CHUNK
        _, la = _log_decay(alr_ref[...], wa_ref[...], ba_ref[...])
        tail = _chunk_scan(la, rows, True)
        kdec_ref[...] = k_ref[...] * jnp.exp(tail - la)
        gam_ref[...] = jnp.exp(tail)

        def chunk(c, carry):
            r0 = pl.multiple_of(c * CHUNK, CHUNK)
            rs = pl.ds(r0, CHUNK)
            gam = gam_ref[pl.ds(r0, 1), :]
            heads = range(N_HEADS)
            kcs = [slice(h * dk, (h + 1) * dk) for h in heads]
            vcs = [slice(h * dv, (h + 1) * dv) for h in heads]
            upd = [_tn(v_ref[rs, vcs[h]].astype(BF16), kdec_ref[rs, kcs[h]].astype(BF16)) for h in heads]
            st = [st_ref[h] * gam[:, kcs[h]] + upd[h] for h in heads]
            o = [_nt((q_ref[rs, kcs[h]] * qscale).astype(BF16), st[h].astype(BF16)) for h in heads]
            for h in heads:
                st_ref[h] = st[h]
                st_out_ref[c, h] = st[h]
                r = lax.rsqrt(jnp.mean(o[h] * o[h], axis=-1, keepdims=True) + RMS_EPS)
                gv = g_ref[rs, vcs[h]]
                y_ref[rs, DP + h * dv:DP + (h + 1) * dv] = (o[h] * r * hn_ref[:, vcs[h]] * (gv * _sigmoid(gv))).astype(BF16)
            return carry

        lax.fori_loop(0, ncb, chunk, 0, unroll=2)

    full = lambda shape: pl.BlockSpec(shape, lambda i: (0,) * len(shape))
    return pl.pallas_call(
        body, name=name, grid=(S // tb,),
        in_specs=_gla_specs(DP, DKT, DV, tb, lambda i: i) + [pl.BlockSpec((tb, DP), lambda i: (i, 0)),
                                                            full((LANES, DKT)), full((1, DKT)), full((1, DV))],
        out_specs=[pl.BlockSpec((tb, DP + DV), lambda i: (i, 0)),
                   pl.BlockSpec((ncb, N_HEADS, dv, dk), lambda i: (i, 0, 0, 0))],
        out_shape=[jax.ShapeDtypeStruct((S, DP + DV), BF16), jax.ShapeDtypeStruct((S // CHUNK, N_HEADS, dv, dk), F32)],
        scratch_shapes=[pltpu.VMEM((N_HEADS, dv, dk), F32), pltpu.VMEM((tb, DKT), F32), pltpu.VMEM((tb, DKT), F32)],
        compiler_params=_params())(proj, proj, proj, proj, proj, y_pool, w_a2, b_a, head_norm)


def _gla_bwd(proj, states, dymix, du, w_a2, b_a, head_norm, name, tb=512):
    S = proj.shape[0]
    DP = du.shape[1]
    DKT = b_a.shape[1]
    DV = head_norm.shape[1]
    dk, dv = DKT // N_HEADS, DV // N_HEADS
    tb = _tile(S, tb)
    ncb = tb // CHUNK
    nb = S // tb
    qscale = dk ** -0.5
    rev = lambda i: nb - 1 - i

    q0, k0, v0, g0, a0 = DP, DP + DKT, DP + 2 * DKT, DP + 2 * DKT + DV, DP + 2 * DKT + 2 * DV

    def body(q_ref, k_ref, v_ref, g_ref, alr_ref, st_blk_ref, st_prev_ref, dy_ref, du_ref, wa_ref, ba_ref, hn_ref,
             dp_ref, dwa_ref, dba_ref, dhn_ref,
             dst_ref, kdec_ref, dec_ref, gam_ref, e_ref, dla_ref, dhn_acc_ref):
        i = pl.program_id(0)
        blk = rev(i)
        dp_ref[:, :DP] = du_ref[...]

        @pl.when(i == 0)
        def _():
            dst_ref[...] = jnp.zeros_like(dst_ref)

        dhn_acc_ref[...] = jnp.zeros_like(dhn_acc_ref)
        rows = lax.broadcasted_iota(jnp.int32, (tb, 1), 0) % CHUNK
        z, la = _log_decay(alr_ref[...], wa_ref[...], ba_ref[...])
        tail = _chunk_scan(la, rows, True)
        dec_ref[...] = jnp.exp(tail - la)
        kdec_ref[...] = k_ref[...] * dec_ref[...]
        gam_ref[...] = jnp.exp(tail)

        def chunk(cc, carry):
            c = ncb - 1 - cc
            r0 = pl.multiple_of(c * CHUNK, CHUNK)
            rs = pl.ds(r0, CHUNK)
            gam = gam_ref[pl.ds(r0, 1), :]
            first = jnp.logical_and(blk == 0, c == 0)
            heads = range(N_HEADS)
            kcs = [slice(h * dk, (h + 1) * dk) for h in heads]
            vcs = [slice(h * dv, (h + 1) * dv) for h in heads]
            qs = [(q_ref[rs, kcs[h]] * qscale).astype(BF16) for h in heads]
            stb = [st_blk_ref[c, h].astype(BF16) for h in heads]
            o = [_nt(qs[h], stb[h]) for h in heads]
            do = []
            for h in heads:
                oh = o[h]
                r = lax.rsqrt(jnp.mean(oh * oh, axis=-1, keepdims=True) + RMS_EPS)
                gv = g_ref[rs, vcs[h]]
                sg = _sigmoid(gv)
                dy = dy_ref[rs, vcs[h]]
                hn = hn_ref[:, vcs[h]]
                on = oh * r
                dp_ref[rs, g0 + h * dv:g0 + (h + 1) * dv] = (dy * on * hn * (sg * (1.0 + gv * (1.0 - sg)))).astype(BF16)
                don = dy * (gv * sg)
                dhn_acc_ref[:, vcs[h]] += jnp.sum(don * on, axis=0, keepdims=True)
                dn = don * hn
                do.append((r * dn - oh * (r * r * r) * jnp.mean(dn * oh, axis=-1, keepdims=True)).astype(BF16))
            dqs = [_nn(do[h], stb[h]) for h in heads]
            dst = [dst_ref[h] + _tn(do[h], qs[h]) for h in heads]
            for h in heads:
                dp_ref[rs, q0 + h * dk:q0 + (h + 1) * dk] = (dqs[h] * qscale).astype(BF16)
            dstb = [dst[h].astype(BF16) for h in heads]
            dvh = [_nt(kdec_ref[rs, kcs[h]].astype(BF16), dstb[h]) for h in heads]
            dkdec = [_nn(v_ref[rs, vcs[h]].astype(BF16), dstb[h]) for h in heads]
            gdg = []
            for h in heads:
                dp_ref[rs, v0 + h * dv:v0 + (h + 1) * dv] = dvh[h].astype(BF16)
                dp_ref[rs, k0 + h * dk:k0 + (h + 1) * dk] = (dkdec[h] * dec_ref[rs, kcs[h]]).astype(BF16)
                e_ref[rs, kcs[h]] = dkdec[h] * kdec_ref[rs, kcs[h]]
                st_prev = jnp.where(c > 0, st_blk_ref[jnp.maximum(c - 1, 0), h], st_prev_ref[0, h])
                st_prev = jnp.where(first, 0.0, st_prev)
                gdg.append(jnp.sum(dst[h] * st_prev, axis=0, keepdims=True) * gam[:, kcs[h]])
                dst_ref[h] = dst[h] * gam[:, kcs[h]]
            dla_ref[rs, :] = jnp.broadcast_to(jnp.concatenate(gdg, axis=1), (CHUNK, DKT))
            return carry

        lax.fori_loop(0, ncb, chunk, 0, unroll=2)

        ev = e_ref[...]
        dla = dla_ref[...] + _chunk_scan(ev, rows, False) - ev
        dz = dla * (1.0 / GATE_TEMP) * (1.0 - _sigmoid(z))
        dzb = dz.astype(BF16)
        dp_ref[:, a0:a0 + LANES] = _nt(dzb, wa_ref[...]).astype(BF16)
        dwa = _tn(alr_ref[...].astype(BF16), dzb)
        dba = jnp.sum(dz, axis=0, keepdims=True)

        @pl.when(i == 0)
        def _():
            dwa_ref[...] = dwa
            dba_ref[...] = dba
            dhn_ref[...] = dhn_acc_ref[...]

        @pl.when(i > 0)
        def _():
            dwa_ref[...] += dwa
            dba_ref[...] += dba
            dhn_ref[...] += dhn_acc_ref[...]

    full = lambda shape: pl.BlockSpec(shape, lambda i: (0,) * len(shape))
    rowblk = lambda w: pl.BlockSpec((tb, w), lambda i: (rev(i), 0))
    return pl.pallas_call(
        body, name=name, grid=(nb,),
        in_specs=_gla_specs(DP, DKT, DV, tb, rev) + [
            pl.BlockSpec((ncb, N_HEADS, dv, dk), lambda i: (rev(i), 0, 0, 0)),
            pl.BlockSpec((1, N_HEADS, dv, dk), lambda i: (jnp.maximum(rev(i) * ncb - 1, 0), 0, 0, 0)),
            pl.BlockSpec((tb, DV), lambda i: (rev(i), DP // DV)), rowblk(DP),
            full((LANES, DKT)), full((1, DKT)), full((1, DV))],
        out_specs=[rowblk(a0 + LANES), full((LANES, DKT)), full((1, DKT)), full((1, DV))],
        out_shape=[jax.ShapeDtypeStruct((S, a0 + LANES), BF16), jax.ShapeDtypeStruct((LANES, DKT), F32),
                   jax.ShapeDtypeStruct((1, DKT), F32), jax.ShapeDtypeStruct((1, DV), F32)],
        scratch_shapes=[pltpu.VMEM((N_HEADS, dv, dk), F32)] + [pltpu.VMEM((tb, DKT), F32)] * 5 + [pltpu.VMEM((1, DV), F32)],
        compiler_params=_params())(proj, proj, proj, proj, proj, states, states, dymix, du, w_a2, b_a, head_norm)


def _xattn_fwd(q, kv, name, tm=512):
    S, D = q.shape
    M = kv.shape[0]
    hd = D // N_HEADS
    tm = _tile(S, tm)
    scale = hd ** -0.5

    def body(q_ref, k_ref, v_ref, o_ref):
        heads = range(N_HEADS)
        hcs = [slice(h * hd, (h + 1) * hd) for h in heads]
        s = [_nt(q_ref[:, hc], k_ref[:, hc]) * scale for hc in hcs]
        p = []
        for h in heads:
            e = jnp.exp(s[h] - jnp.max(s[h], axis=-1, keepdims=True))
            p.append((e / jnp.sum(e, axis=-1, keepdims=True)).astype(BF16))
        o = [_nn(p[h], v_ref[:, hcs[h]]) for h in heads]
        for h in heads:
            o_ref[:, hcs[h]] = o[h].astype(BF16)

    return pl.pallas_call(body, name=name, grid=(S // tm,),
                          in_specs=[pl.BlockSpec((tm, D), lambda i: (i, 0)), pl.BlockSpec((M, D), lambda i: (0, 0)),
                                    pl.BlockSpec((M, D), lambda i: (0, 1))],
                          out_specs=pl.BlockSpec((tm, D), lambda i: (i, 0)),
                          out_shape=jax.ShapeDtypeStruct((S, D), BF16), compiler_params=_params())(q, kv, kv)


def _xattn_bwd(q, kv, do, name, tm=512):
    S, D = q.shape
    M = kv.shape[0]
    hd = D // N_HEADS
    tm = _tile(S, tm)
    scale = hd ** -0.5

    def body(q_ref, k_ref, v_ref, do_ref, dq_ref, dkv_ref):
        first = pl.program_id(0) == 0
        heads = range(N_HEADS)
        hcs = [slice(h * hd, (h + 1) * hd) for h in heads]
        s = [_nt(q_ref[:, hc], k_ref[:, hc]) * scale for hc in hcs]
        dp = [_nt(do_ref[:, hc], v_ref[:, hc]) for hc in hcs]
        p = []
        for h in heads:
            e = jnp.exp(s[h] - jnp.max(s[h], axis=-1, keepdims=True))
            p.append(e / jnp.sum(e, axis=-1, keepdims=True))
        dvh = [_tn(p[h].astype(BF16), do_ref[:, hcs[h]]) for h in heads]
        ds = [((p[h] * (dp[h] - jnp.sum(dp[h] * p[h], axis=-1, keepdims=True))) * scale).astype(BF16) for h in heads]
        dqh = [_nn(ds[h], k_ref[:, hcs[h]]) for h in heads]
        dkh = [_tn(ds[h], q_ref[:, hcs[h]]) for h in heads]
        for h in heads:
            dq_ref[:, hcs[h]] = dqh[h].astype(BF16)

        @pl.when(first)
        def _():
            for h in heads:
                dkv_ref[:, hcs[h]] = dkh[h]
                dkv_ref[:, D + h * hd:D + (h + 1) * hd] = dvh[h]

        @pl.when(jnp.logical_not(first))
        def _():
            for h in heads:
                dkv_ref[:, hcs[h]] += dkh[h]
                dkv_ref[:, D + h * hd:D + (h + 1) * hd] += dvh[h]

    row = pl.BlockSpec((tm, D), lambda i: (i, 0))
    return pl.pallas_call(body, name=name, grid=(S // tm,),
                          in_specs=[row, pl.BlockSpec((M, D), lambda i: (0, 0)), pl.BlockSpec((M, D), lambda i: (0, 1)), row],
                          out_specs=[row, pl.BlockSpec((M, 2 * D), lambda i: (0, 0))],
                          out_shape=[jax.ShapeDtypeStruct((S, D), BF16), jax.ShapeDtypeStruct((M, 2 * D), F32)],
                          compiler_params=_params())(q, kv, kv, do)


def _local_step(x, mem, target, vec, weight, emit, dep0):
    g = {}
    pending = []
    begun = []
    summed = []
    emit_begin, emit_finish, emit_send, early_update = emit

    def behind(fn, *a, **kw):
        dep = tuple(pending)
        pending.clear()
        out = fn(*a, dep=dep, **kw)
        while summed:
            pending.append(emit_send(summed.pop(0)))
        while begun:
            name = begun.pop(0)
            token = emit_finish(name, out)
            if token is None:
                summed.append(name)
            else:
                pending.append(token)
        return out

    def mm(a, b, **kw):
        return behind(_matmul, a, b, **kw)

    def send(name, gfull):
        pending.append(emit_begin(name, gfull))
        begun.append(name)

    def ffn_fwd(xin, tag, dep):
        h = _rms_fwd(xin, vec[f"{tag}_norm"], f"{tag}_norm", dep=dep)
        ga, gb, hid = _ffn_up(h, weight(f"{tag}_w_gate", h), weight(f"{tag}_w_up", h), f"{tag}_up")
        wd = weight(f"{tag}_w_down", hid)
        G, Fj, D = wd.shape
        xo = _matmul(hid, wd.reshape(G * Fj, D), mode="nn", name=f"{tag}_down", out_dtype=F32, res=xin, scale=0.5,
                     tn=1024, tk=G * Fj)
        return xo, (h, ga, gb, hid)

    def ffn_bwd(dxh, saved, tag, kept_back=None):
        h, ga, gb, hid = saved
        wg, wu, wd = weight(f"{tag}_w_gate"), weight(f"{tag}_w_up"), weight(f"{tag}_w_down")
        G, Fj, D = wd.shape
        send(f"{tag}_w_down", mm(hid, dxh, mode="tn", name=f"{tag}_dwd", out_dtype=F32, tm=Fj, tn=1024))
        da, db = _ffn_dact(dxh, wd, ga, gb, f"{tag}_dact")
        send(f"{tag}_w_gate", mm(h, da, mode="tn", name=f"{tag}_dwg", out_dtype=F32, tm=1024, tn=Fj, tk=TOKENS_PER_STEP,
                                  out_groups=G))
        send(f"{tag}_w_up", mm(h, db, mode="tn", name=f"{tag}_dwu", out_dtype=F32, tm=1024, tn=Fj, tk=TOKENS_PER_STEP,
                                out_groups=G))
        if kept_back is not None:
            name, gfull = kept_back()
            pending.append(emit_begin(name, gfull))
            pending.append(emit_finish(name, pending[-1]))
        return behind(_ffn_dh, da, db, wg, wu, f"{tag}_dh")

    x1, ffn1_saved = ffn_fwd(x, "ffn1", dep0)
    h2 = _rms_fwd(x1, vec["mix_norm"], "mix_norm")
    w_in = weight("w_in", h2)
    proj = _matmul(h2, w_in, mode="nn", name="w_in", out_dtype=F32, tn=1408)
    pool_w, w_a2 = weight("pool_w", h2), weight("gla_w_a2", h2)
    y_pool, dpool = _pool_fwd(proj, pool_w, vec["pool_scale"], "pool_fwd")
    ymix, states = _gla_fwd(proj, y_pool, w_a2, vec["gla_b_a"], vec["gla_head_norm"], "gla_fwd")
    w_out = weight("w_out", ymix)
    x2 = _matmul(ymix, w_out, mode="nn", name="w_out", out_dtype=F32, res=x1)
    h3 = _rms_fwd(x2, vec["xattn_norm"], "xattn_norm")
    mh = _rms_fwd(mem, vec["mem_norm"], "mem_norm")
    w_q = weight("xattn_w_q", h3)
    q = _matmul(h3, w_q, mode="nn", name="xattn_q", out_dtype=BF16)
    w_kv = weight("xattn_w_kv", q)
    kv = _matmul(mh, w_kv, mode="nn", name="xattn_kv", out_dtype=BF16, b_groups=True, tn=1024)
    o = _xattn_fwd(q, kv, "xattn_fwd")
    w_o = weight("xattn_w_o", o)
    x3 = _matmul(o, w_o, mode="nn", name="xattn_o", out_dtype=F32, res=x2)
    x4, ffn2_saved = ffn_fwd(x3, "ffn2", None)
    sq, dx4, dx4h, g["final_norm"] = _loss_head(x4, vec["final_norm"], target, "loss_head")
    loss = lax.psum(0.5 * jnp.sum(sq) / x.shape[-1], ("x", "y", "c"))
    pending.append(loss.reshape(1, 1))

    dh = ffn_bwd(dx4h, ffn2_saved, "ffn2")
    dx3, dx3b, g["ffn2_norm"] = _rms_bwd(x3, vec["ffn2_norm"], dh, dx4, "ffn2_norm_bwd", lowp=1.0)
    send("xattn_w_o", mm(o, dx3b, mode="tn", name="xattn_dwo", out_dtype=F32, tm=1024, tn=1024, tk=TOKENS_PER_STEP))
    do = mm(dx3b, w_o, mode="nt", name="xattn_do", out_dtype=BF16)
    dq, dkv = _xattn_bwd(q, kv, do, "xattn_bwd")
    send("xattn_w_q", mm(h3, dq, mode="tn", name="xattn_dwq", out_dtype=F32, tm=1024, tn=1024, tk=TOKENS_PER_STEP))
    dh3 = mm(dq, w_q, mode="nt", name="xattn_dh", out_dtype=F32)
    dkvb = _cast(dkv, BF16, "dkv_cast")
    send("xattn_w_kv", mm(mh, dkvb, mode="tn", name="xattn_dwkv", out_dtype=F32, tm=1024, tn=1024, out_groups=N_SHARDS))
    dmh = mm(dkvb, w_kv, mode="nt", name="xattn_dmh", out_dtype=F32, b_groups=True, tk=1024)
    _, g["mem_norm"] = _rms_bwd(mem, vec["mem_norm"], dmh, None, "mem_norm_bwd")
    pending.append(g["mem_norm"])
    dx2, dx2b, g["xattn_norm"] = _rms_bwd(x2, vec["xattn_norm"], dh3, dx3, "xattn_norm_bwd", lowp=1.0)
    send("w_out", mm(ymix, dx2b, mode="tn", name="dw_out", out_dtype=F32, tm=1024, tn=1024, tk=TOKENS_PER_STEP))
    dymix = mm(dx2b, w_out, mode="nt", name="dymix", out_dtype=F32)
    du, dpool_w, g["pool_scale"] = _pool_bwd(dymix, dpool, pool_w, vec["pool_scale"], "pool_bwd")
    send("pool_w", dpool_w)
    dproj, dw_a2, g["gla_b_a"], g["gla_head_norm"] = _gla_bwd(
        proj, states, dymix, du, w_a2, vec["gla_b_a"], vec["gla_head_norm"], "gla_bwd")
    send("gla_w_a2", dw_a2)
    dh2 = mm(dproj, w_in, mode="nt", name="dh2", out_dtype=F32, tn=1024, tk=dproj.shape[1])
    pending.extend(early_update(dh2))
    dx1, dx1h, g["mix_norm"] = _rms_bwd(x1, vec["mix_norm"], dh2, dx2, "mix_norm_bwd", lowp=0.5)
    dh = ffn_bwd(dx1h, ffn1_saved, "ffn1", kept_back=lambda: (
        "w_in", mm(h2, dproj, mode="tn", name="dw_in", out_dtype=F32, tm=1024, tn=1408, tk=TOKENS_PER_STEP)))
    dx0, g["ffn1_norm"] = _rms_bwd(x, vec["ffn1_norm"], dh, dx1, "ffn1_norm_bwd")
    return loss, dx0, g


def _place():
    x, y, c = lax.axis_index("x"), lax.axis_index("y"), lax.axis_index("c")
    chips = [(1 - x, y), (x, 1 - y), (1 - x, 1 - y)]
    return x, y, c, chips


def _ids():
    return jnp.stack([2 * lax.axis_index("x") + lax.axis_index("y"), lax.axis_index("c")]).astype(jnp.int32)


def _hbm(a):
    return pltpu.with_memory_space_constraint(a, pltpu.HBM)


def _cast_to_slot(w2d, dtype, name, dep=None):
    R, C = w2d.shape
    tr = _tile(R, max(16, (4 << 20) // (4 * C) // 16 * 16))

    def body(i_ref, w_ref, *rest):
        rest[-1][...] = w_ref[...].astype(dtype)

    in_specs = [pl.BlockSpec((tr, C), lambda r, i: (r, 0))]
    operands = [w2d]
    if dep is not None:
        in_specs.append(pl.BlockSpec(dep.shape, lambda r, i: (0, 0)))
        operands.append(dep)
    grid_spec = pltpu.PrefetchScalarGridSpec(num_scalar_prefetch=1, grid=(R // tr,), in_specs=in_specs,
                                             out_specs=pl.BlockSpec((None, tr, C), lambda r, i: (i[0], r, 0)))
    return pl.pallas_call(body, name=name, grid_spec=grid_spec, out_shape=jax.ShapeDtypeStruct((N_SHARDS, R, C), dtype),
                          compiler_params=_params())(_ids(), *operands)


def _gather_copies(buf_ref, send_sems, recv_sems, incoming, whole):
    x, y, c, chips = _place()
    hr = buf_ref.shape[1] // 2
    copies = []
    for j, (px, py) in enumerate(chips):
        slot = 2 * px + py if incoming else 2 * x + y
        part = buf_ref.at[slot] if whole else buf_ref.at[slot, pl.ds(c * hr, hr), :]
        copies.append(pltpu.make_async_remote_copy(src_ref=part, dst_ref=part, send_sem=send_sems.at[j],
                                                   recv_sem=recv_sems.at[j], device_id=(px, py, c), device_id_type=MESH))
    return copies


def _gather_start(buf, name, whole):
    def body(b_ref, send_sems, recv_sems, b_thru, token):
        for cp in _gather_copies(b_ref, send_sems, recv_sems, False, whole):
            cp.start()
        token[...] = jnp.zeros_like(token)

    return pl.pallas_call(
        body, name=name,
        out_shape=(pltpu.SemaphoreType.DMA((3,)), pltpu.SemaphoreType.DMA((3,)), pltpu.HBM(buf.shape, buf.dtype),
                   jax.ShapeDtypeStruct((8, LANES), F32)),
        in_specs=(HBM,), out_specs=(SEM, SEM, HBM, pl.BlockSpec(memory_space=pltpu.VMEM)), input_output_aliases={0: 2},
        compiler_params=pltpu.CompilerParams(has_side_effects=EFFECT))(_hbm(buf))


def _gather_wait(send_sems, recv_sems, buf_thru, after, name, whole):
    def body(b_ref, send_sems, recv_sems, after_ref, b_out):
        for cp in _gather_copies(b_ref, send_sems, recv_sems, False, whole):
            cp.wait_send()
        for cp in _gather_copies(b_ref, send_sems, recv_sems, True, whole):
            cp.wait_recv()

    return pl.pallas_call(
        body, name=name, out_shape=pltpu.HBM(buf_thru.shape, buf_thru.dtype),
        in_specs=(HBM, SEM, SEM, ANY), out_specs=HBM, input_output_aliases={0: 0},
        compiler_params=pltpu.CompilerParams(has_side_effects=EFFECT))(buf_thru, send_sems, recv_sems, after)


def _gather_forward(buf, name):
    G, R, C = buf.shape
    hr = R // 2

    def body(b_ref, o_ref, send_sems, recv_sems):
        x, y, c, chips = _place()
        copies = []
        for j, (px, py) in enumerate(chips):
            half = o_ref.at[2 * px + py, pl.ds(c * hr, hr), :]
            copies.append(pltpu.make_async_remote_copy(src_ref=half, dst_ref=half, send_sem=send_sems.at[j],
                                                       recv_sem=recv_sems.at[j], device_id=(x, y, 1 - c),
                                                       device_id_type=MESH))
        for cp in copies:
            cp.start()
        for j, (px, py) in enumerate(chips):
            half = o_ref.at[2 * px + py, pl.ds((1 - c) * hr, hr), :]
            pltpu.make_async_remote_copy(src_ref=half, dst_ref=half, send_sem=send_sems.at[j], recv_sem=recv_sems.at[j],
                                         device_id=(x, y, 1 - c), device_id_type=MESH).wait_recv()
        for cp in copies:
            cp.wait_send()

    return pl.pallas_call(body, name=name, in_specs=[ANY], out_specs=ANY, out_shape=jax.ShapeDtypeStruct(buf.shape, buf.dtype),
                          input_output_aliases={0: 0},
                          scratch_shapes=[pltpu.SemaphoreType.DMA((3,)), pltpu.SemaphoreType.DMA((3,))])(buf)


def _pair_copy(g_ref, land_ref, send_sem, recv_sem):
    x, y, c, _ = _place()
    hr = g_ref.shape[1] // 2
    return pltpu.make_async_remote_copy(src_ref=g_ref.at[:, pl.ds((1 - c) * hr, hr), :], dst_ref=land_ref,
                                        send_sem=send_sem, recv_sem=recv_sem, device_id=(x, y, 1 - c), device_id_type=MESH)


def _pair_start(gfull, name):
    G, R, C = gfull.shape

    def body(g_ref, land_ref, send_sem, recv_sem, g_thru, land_thru, token):
        _pair_copy(g_ref, land_ref, send_sem, recv_sem).start()
        token[...] = jnp.zeros_like(token)

    return pl.pallas_call(
        body, name=name,
        out_shape=(pltpu.SemaphoreType.DMA(()), pltpu.SemaphoreType.DMA(()), pltpu.HBM(gfull.shape, F32),
                   pltpu.HBM((G, R // 2, C), F32), jax.ShapeDtypeStruct((8, LANES), F32)),
        in_specs=(HBM, HBM), out_specs=(SEM, SEM, HBM, HBM, pl.BlockSpec(memory_space=pltpu.VMEM)),
        input_output_aliases={0: 2, 1: 3},
        compiler_params=pltpu.CompilerParams(has_side_effects=EFFECT))(_hbm(gfull), _hbm(lax.empty((G, R // 2, C), F32)))


def _pair_wait(send_sem, recv_sem, g_thru, land_thru, after, name):
    def body(g_ref, land_ref, send_sem, recv_sem, after_ref, g_out, land_out):
        cp = _pair_copy(g_ref, land_ref, send_sem, recv_sem)
        cp.wait_send()
        cp.wait_recv()

    return pl.pallas_call(
        body, name=name, out_shape=(pltpu.HBM(g_thru.shape, F32), pltpu.HBM(land_thru.shape, F32)),
        in_specs=(HBM, HBM, SEM, SEM, ANY), out_specs=(HBM, HBM), input_output_aliases={0: 0, 1: 1},
        compiler_params=pltpu.CompilerParams(has_side_effects=EFFECT))(g_thru, land_thru, send_sem, recv_sem, after)


def _pair_add(gfull, other, name):
    G, R, C = gfull.shape
    hr = R // 2
    tr = _tile(hr, max(8, (2 * 1024 * 1024) // (4 * C) // 8 * 8))
    nr = hr // tr
    c = lax.axis_index("c")
    cidx = jnp.reshape(c, (1,)).astype(jnp.int32)

    def body(c_ref, a_ref, b_ref, o_ref):
        o_ref[...] = a_ref[...] + b_ref[...]

    grid_spec = pltpu.PrefetchScalarGridSpec(
        num_scalar_prefetch=1, grid=(G, nr),
        in_specs=[pl.BlockSpec((None, tr, C), lambda g, r, cr: (g, cr[0] * nr + r, 0)),
                  pl.BlockSpec((None, tr, C), lambda g, r, cr: (g, r, 0))],
        out_specs=pl.BlockSpec((None, tr, C), lambda g, r, cr: (g, r, 0)))
    return pl.pallas_call(body, name=name, grid_spec=grid_spec, out_shape=jax.ShapeDtypeStruct((G, hr, C), F32),
                          compiler_params=_params())(cidx, gfull, other)


def _chip_copies(p_ref, land_ref, send_sems, recv_sems, incoming):
    x, y, c, chips = _place()
    me = 2 * x + y
    copies = []
    for j, (px, py) in enumerate(chips):
        dst = land_ref.at[2 * px + py] if incoming else land_ref.at[me]
        copies.append(pltpu.make_async_remote_copy(src_ref=p_ref.at[2 * px + py], dst_ref=dst, send_sem=send_sems.at[j],
                                                   recv_sem=recv_sems.at[j], device_id=(px, py, c), device_id_type=MESH))
    return copies


def _chip_start(part, name):
    def body(p_ref, land_ref, send_sems, recv_sems, p_thru, land_thru, token):
        for cp in _chip_copies(p_ref, land_ref, send_sems, recv_sems, False):
            cp.start()
        token[...] = jnp.zeros_like(token)

    return pl.pallas_call(
        body, name=name,
        out_shape=(pltpu.SemaphoreType.DMA((3,)), pltpu.SemaphoreType.DMA((3,)), pltpu.HBM(part.shape, F32),
                   pltpu.HBM(part.shape, F32), jax.ShapeDtypeStruct((8, LANES), F32)),
        in_specs=(HBM, HBM), out_specs=(SEM, SEM, HBM, HBM, pl.BlockSpec(memory_space=pltpu.VMEM)),
        input_output_aliases={0: 2, 1: 3},
        compiler_params=pltpu.CompilerParams(has_side_effects=EFFECT))(_hbm(part), _hbm(lax.empty(part.shape, F32)))


def _chip_wait(send_sems, recv_sems, p_thru, land_thru, after, name):
    def body(p_ref, land_ref, send_sems, recv_sems, after_ref, p_out, land_out):
        for cp in _chip_copies(p_ref, land_ref, send_sems, recv_sems, False):
            cp.wait_send()
        for cp in _chip_copies(p_ref, land_ref, send_sems, recv_sems, True):
            cp.wait_recv()

    return pl.pallas_call(
        body, name=name, out_shape=(pltpu.HBM(p_thru.shape, F32), pltpu.HBM(p_thru.shape, F32)),
        in_specs=(HBM, HBM, SEM, SEM, ANY), out_specs=(HBM, HBM), input_output_aliases={0: 0, 1: 1},
        compiler_params=pltpu.CompilerParams(has_side_effects=EFFECT))(p_thru, land_thru, send_sems, recv_sems, after)


def _chip_sum(part, slots, name):
    G, R2, C = part.shape
    tr = _tile(R2, max(8, (1 << 20) // (4 * C) // 8 * 8))
    nr = R2 // tr

    def body(i_ref, p_ref, *rest):
        o_ref = rest[-1]
        acc = None
        for u in range(G):
            val = jnp.where(i_ref[0] == u, p_ref[...], rest[u][...])
            acc = val if acc is None else acc + val
        o_ref[...] = acc

    def slot_spec(u):
        return pl.BlockSpec((None, tr, C), lambda r, i: (jnp.where(i[0] == u, (u + 1) % G, u), r, 0))

    grid_spec = pltpu.PrefetchScalarGridSpec(
        num_scalar_prefetch=1, grid=(nr,),
        in_specs=[pl.BlockSpec((None, tr, C), lambda r, i: (i[0], r, 0))] + [slot_spec(u) for u in range(G)],
        out_specs=pl.BlockSpec((tr, C), lambda r, i: (i[1] * nr + r, 0)))
    return pl.pallas_call(body, name=name, grid_spec=grid_spec, out_shape=jax.ShapeDtypeStruct((2 * R2, C), F32),
                          compiler_params=_params())(_ids(), part, slots, slots, slots, slots)


def _sum_slots(slots, name):
    G, R2, C = slots.shape
    tr = _tile(R2, max(8, (1024 * 1024) // (4 * C) // 8 * 8))

    def body(s_ref, o_ref):
        acc = s_ref[0]
        for u in range(1, G):
            acc = acc + s_ref[u]
        o_ref[...] = acc

    return pl.pallas_call(body, name=name, grid=(R2 // tr,), in_specs=[pl.BlockSpec((G, tr, C), lambda r: (0, r, 0))],
                          out_specs=pl.BlockSpec((tr, C), lambda r: (r, 0)), out_shape=jax.ShapeDtypeStruct((R2, C), F32),
                          compiler_params=_params())(slots)


def _pair_join(full, name):
    R, C = full.shape
    R2 = R // 2

    def body(f_ref, o_ref, token, send_sem, recv_sem):
        x, y, c, _ = _place()
        token[...] = jnp.zeros_like(token)
        mine = o_ref.at[pl.ds(c * R2, R2), :]
        theirs = o_ref.at[pl.ds((1 - c) * R2, R2), :]
        cp = pltpu.make_async_remote_copy(src_ref=mine, dst_ref=mine, send_sem=send_sem, recv_sem=recv_sem,
                                          device_id=(x, y, 1 - c), device_id_type=MESH)
        cp.start()
        pltpu.make_async_remote_copy(src_ref=theirs, dst_ref=theirs, send_sem=send_sem, recv_sem=recv_sem,
                                     device_id=(x, y, 1 - c), device_id_type=MESH).wait_recv()
        cp.wait_send()

    return pl.pallas_call(body, name=name, in_specs=[ANY], out_specs=[ANY, pl.BlockSpec(memory_space=pltpu.VMEM)],
                          out_shape=[jax.ShapeDtypeStruct((R, C), F32), jax.ShapeDtypeStruct((8, LANES), F32)],
                          input_output_aliases={0: 0},
                          scratch_shapes=[pltpu.SemaphoreType.DMA, pltpu.SemaphoreType.DMA])(full)


def _all_reduce_small(v, name):
    R, C = v.shape

    def gather_body(v_ref, out_ref, send_sems, recv_sems, local_sem):
        x, y, c, _ = _place()
        me = 4 * x + 2 * y + c
        mine = pltpu.make_async_copy(v_ref, out_ref.at[me], local_sem)
        mine.start()
        flips = [(fx, fy, fc) for fx in (0, 1) for fy in (0, 1) for fc in (0, 1)][1:]
        copies = []
        for j, (fx, fy, fc) in enumerate(flips):
            peer = (x ^ fx, y ^ fy, c ^ fc)
            copies.append(pltpu.make_async_remote_copy(src_ref=v_ref, dst_ref=out_ref.at[me], send_sem=send_sems.at[j],
                                                       recv_sem=recv_sems.at[j], device_id=peer, device_id_type=MESH))
        for cp in copies:
            cp.start()
        for j, (fx, fy, fc) in enumerate(flips):
            peer = (x ^ fx, y ^ fy, c ^ fc)
            pltpu.make_async_remote_copy(src_ref=v_ref, dst_ref=out_ref.at[4 * peer[0] + 2 * peer[1] + peer[2]],
                                         send_sem=send_sems.at[j], recv_sem=recv_sems.at[j], device_id=peer,
                                         device_id_type=MESH).wait_recv()
        for cp in copies:
            cp.wait_send()
        mine.wait()

    slots = pl.pallas_call(gather_body, name=name, in_specs=[ANY], out_specs=ANY,
                           out_shape=jax.ShapeDtypeStruct((8, R, C), F32),
                           scratch_shapes=[pltpu.SemaphoreType.DMA((7,)), pltpu.SemaphoreType.DMA((7,)),
                                           pltpu.SemaphoreType.DMA])(v)
    return _sum_slots(slots, f"{name}_sum")


def _adamw(w, g, m, v, name, dep=()):
    R, C = w.shape
    tr = _tile(R, max(8, (2 << 20) // (4 * C) // 8 * 8))
    bc1 = 1.0 - ADAM_B1 ** ADAM_STEP
    bc2 = 1.0 - ADAM_B2 ** ADAM_STEP

    def body(w_ref, g_ref, m_ref, v_ref, *rest):
        go_ref, d_ref, nm_ref, nv_ref = rest[len(dep):]
        gv = g_ref[...]
        go_ref[...] = gv
        nm = ADAM_B1 * m_ref[...] + (1.0 - ADAM_B1) * gv
        nv = ADAM_B2 * v_ref[...] + (1.0 - ADAM_B2) * (gv * gv)
        nm_ref[...] = nm
        nv_ref[...] = nv
        d_ref[...] = -ADAM_LR * ((nm / bc1) / (jnp.sqrt(nv / bc2) + ADAM_EPS) + ADAM_WD * w_ref[...])

    blk = pl.BlockSpec((tr, C), lambda r: (r, 0))
    out = jax.ShapeDtypeStruct((R, C), F32)
    in_specs = [blk] * 4 + [pl.BlockSpec(d.shape, lambda r: (0, 0)) for d in dep]
    return pl.pallas_call(body, name=name, grid=(R // tr,), in_specs=in_specs, out_specs=[blk] * 4, out_shape=[out] * 4,
                          compiler_params=_params())(w, g, m, v, *dep)


SC_TILES = 32
SC_LANES = 16
SC_ROWS = 8


def _sc_mesh():
    return plsc.VectorSubcoreMesh(core_axis_name="sc_core", subcore_axis_name="sc_subcore")


def _pair_add_sc(gfull, other, name):
    G, R, C = gfull.shape
    hr = R // 2
    tiles_per_shard = SC_TILES // G
    per_tile = hr // SC_ROWS // tiles_per_shard

    def body(g_hbm, o_hbm, out_hbm, gb, ob):
        c = lax.axis_index("c")
        tile = lax.axis_index("sc_subcore") * 2 + lax.axis_index("sc_core")
        t = tile // tiles_per_shard
        first = (tile % tiles_per_shard) * per_tile

        @pl.loop(0, per_tile)
        def _(k):
            rr = (first + k) * SC_ROWS
            pltpu.sync_copy(g_hbm.at[t, pl.ds(c * hr + rr, SC_ROWS), :], gb)
            pltpu.sync_copy(o_hbm.at[t, pl.ds(rr, SC_ROWS), :], ob)

            @pl.loop(0, SC_ROWS)
            def _(i):
                @pl.loop(0, C, step=SC_LANES)
                def _(j):
                    at = (i, pl.ds(j, SC_LANES))
                    gb[at] = gb[at] + ob[at]

            pltpu.sync_copy(gb, out_hbm.at[t, pl.ds(rr, SC_ROWS), :])

    buf = pltpu.VMEM((SC_ROWS, C), F32)
    return pl.kernel(body, name=name, out_type=jax.ShapeDtypeStruct((G, hr, C), F32), mesh=_sc_mesh(),
                     scratch_types=[buf, buf])(gfull, other)


def _adamw_sc(w, g, m, v, name):
    R, C = w.shape
    tasks = R // SC_ROWS
    bc1 = 1.0 - ADAM_B1 ** ADAM_STEP
    bc2 = 1.0 - ADAM_B2 ** ADAM_STEP

    def body(w_hbm, g_hbm, m_hbm, v_hbm, go_hbm, d_hbm, nm_hbm, nv_hbm, wb, gb, mb, vb):
        tile = lax.axis_index("sc_subcore") * 2 + lax.axis_index("sc_core")

        @pl.loop((tile * tasks) // SC_TILES, ((tile + 1) * tasks) // SC_TILES)
        def _(task):
            rows = pl.ds(task * SC_ROWS, SC_ROWS)
            pltpu.sync_copy(w_hbm.at[rows, :], wb)
            pltpu.sync_copy(g_hbm.at[rows, :], gb)
            pltpu.sync_copy(m_hbm.at[rows, :], mb)
            pltpu.sync_copy(v_hbm.at[rows, :], vb)

            @pl.loop(0, SC_ROWS)
            def _(i):
                @pl.loop(0, C, step=SC_LANES)
                def _(j):
                    at = (i, pl.ds(j, SC_LANES))
                    gv = gb[at]
                    nm = ADAM_B1 * mb[at] + (1.0 - ADAM_B1) * gv
                    nv = ADAM_B2 * vb[at] + (1.0 - ADAM_B2) * (gv * gv)
                    mb[at] = nm
                    vb[at] = nv
                    wb[at] = -ADAM_LR * ((nm / bc1) / (jnp.sqrt(nv / bc2) + ADAM_EPS) + ADAM_WD * wb[at])

            pltpu.sync_copy(gb, go_hbm.at[rows, :])
            pltpu.sync_copy(wb, d_hbm.at[rows, :])
            pltpu.sync_copy(mb, nm_hbm.at[rows, :])
            pltpu.sync_copy(vb, nv_hbm.at[rows, :])

    out = jax.ShapeDtypeStruct((R, C), F32)
    buf = pltpu.VMEM((SC_ROWS, C), F32)
    return pl.kernel(body, name=name, out_type=(out, out, out, out), mesh=_sc_mesh(),
                     scratch_types=[buf, buf, buf, buf])(w, g, m, v)


WEIGHTS = ['ffn1_norm', 'ffn1_w_gate', 'ffn1_w_up', 'ffn1_w_down', 'mix_norm', 'w_in', 'pool_w', 'pool_scale', 'gla_w_a2',
           'gla_b_a', 'gla_head_norm', 'w_out', 'xattn_norm', 'mem_norm', 'xattn_w_q', 'xattn_w_kv', 'xattn_w_o', 'ffn2_norm',
           'ffn2_w_gate', 'ffn2_w_up', 'ffn2_w_down', 'final_norm']
SHARDED = ['ffn1_w_gate', 'ffn1_w_up', 'ffn1_w_down', 'w_in', 'pool_w', 'gla_w_a2', 'w_out', 'xattn_w_q', 'xattn_w_kv',
           'xattn_w_o', 'ffn2_w_gate', 'ffn2_w_up', 'ffn2_w_down']
REPLICATED = [n for n in WEIGHTS if n not in SHARDED]
ON_SPARSECORE = ['ffn2_w_gate', 'ffn2_w_up', 'w_out', 'xattn_w_q', 'xattn_w_kv', 'xattn_w_o', 'ffn2_w_down']
SLOW_ON_SPARSECORE = 'ffn2_w_down'
PAIR_SUM_ON_SPARSECORE = ['ffn2_w_down', 'ffn2_w_gate', 'ffn2_w_up', 'xattn_w_o', 'xattn_w_kv', 'pool_w', 'ffn1_w_down']
SMALL_COLS = 512


def _as2d(a):
    return a.reshape(-1, a.shape[-1])


def _finish_weight(name, gathered, wl):
    G, R, C = gathered.shape
    rank = wl["gla_w_a2"].shape[1]
    if name in ("w_out", "xattn_w_q", "xattn_w_o"):
        return gathered.reshape(G * R, C)
    if name == "w_in":
        w_in = jnp.transpose(gathered, (1, 0, 2)).reshape(R, G * C)
        main = G * C - rank
        return jnp.concatenate([w_in[:, :main], jnp.pad(w_in[:, main:], ((0, 0), (0, LANES - rank)))], axis=1)
    if name == "pool_w":
        NG, CJ, _ = wl[name].shape[1:]
        return jnp.transpose(gathered.reshape(G, NG, CJ, C), (1, 0, 2, 3)).reshape(NG, G * CJ, C)
    if name == "gla_w_a2":
        a2 = jnp.transpose(gathered, (1, 0, 2)).reshape(rank, G * C)
        return jnp.pad(a2, ((0, LANES - rank), (0, 0))).astype(BF16)
    return gathered


def _start_gathers(wl):
    started = {}
    token = None
    for n in SHARDED:
        whole = n not in ("ffn1_w_gate", "ffn1_w_up")
        buf = _cast_to_slot(_as2d(wl[n]), BF16, f"slot_{n}", dep=token)
        send_sems, recv_sems, thru, token = _gather_start(buf, f"gather_start_{n}", whole)
        started[n] = (send_sems, recv_sems, thru, whole)
    cache = {}

    def weight(n, after=None):
        if n not in cache:
            *handles, whole = started[n]
            buf = _gather_wait(*handles, after, f"gather_wait_{n}", whole)
            if not whole:
                buf = _gather_forward(buf, f"gather_forward_{n}")
            cache[n] = _finish_weight(n, buf, wl)
        return cache[n]

    return weight, token


def _shard_major(name, gfull, wl):
    R, C = _as2d(wl[name]).shape
    if name in ("ffn1_w_gate", "ffn1_w_up", "ffn2_w_gate", "ffn2_w_up", "xattn_w_kv"):
        return gfull
    if name in ("ffn1_w_down", "ffn2_w_down", "w_out", "xattn_w_q", "xattn_w_o"):
        return gfull.reshape(N_SHARDS, R, C)
    if name == "w_in":
        return jnp.transpose(gfull[:, :N_SHARDS * C].reshape(R, N_SHARDS, C), (1, 0, 2))
    if name == "pool_w":
        NG, CJ, _ = wl[name].shape[1:]
        return jnp.transpose(gfull.reshape(NG, N_SHARDS, CJ, C), (1, 0, 2, 3)).reshape(N_SHARDS, R, C)
    assert name == "gla_w_a2"
    return jnp.transpose(gfull[:R].reshape(R, N_SHARDS, C), (1, 0, 2))


def kernel(x, mem, ffn1_norm, ffn1_w_gate, ffn1_w_up, ffn1_w_down, mix_norm, w_in, pool_w, pool_scale, gla_w_a2, gla_b_a, gla_head_norm, w_out, xattn_norm, mem_norm, xattn_w_q, xattn_w_kv, xattn_w_o, ffn2_norm, ffn2_w_gate, ffn2_w_up, ffn2_w_down, final_norm, loss_target, m_ffn1_norm, m_ffn1_w_gate, m_ffn1_w_up, m_ffn1_w_down, m_mix_norm, m_w_in, m_pool_w, m_pool_scale, m_gla_w_a2, m_gla_b_a, m_gla_head_norm, m_w_out, m_xattn_norm, m_mem_norm, m_xattn_w_q, m_xattn_w_kv, m_xattn_w_o, m_ffn2_norm, m_ffn2_w_gate, m_ffn2_w_up, m_ffn2_w_down, m_final_norm, v_ffn1_norm, v_ffn1_w_gate, v_ffn1_w_up, v_ffn1_w_down, v_mix_norm, v_w_in, v_pool_w, v_pool_scale, v_gla_w_a2, v_gla_b_a, v_gla_head_norm, v_w_out, v_xattn_norm, v_mem_norm, v_xattn_w_q, v_xattn_w_kv, v_xattn_w_o, v_ffn2_norm, v_ffn2_w_gate, v_ffn2_w_up, v_ffn2_w_down, v_final_norm):
    given = dict(locals())
    wl = {n: given[n] for n in WEIGHTS}
    ml = {n: given["m_" + n] for n in WEIGHTS}
    vl = {n: given["v_" + n] for n in WEIGHTS}

    vec = {n: wl[n].reshape(1, -1) for n in REPLICATED}
    weight, dep0 = _start_gathers(wl)
    in_flight = {}

    pair_flight = {}

    def emit_begin(n, gfull):
        *pair_flight[n], token = _pair_start(_shard_major(n, gfull, wl), f"{n}_pair_start")
        return token

    summing = {}

    def emit_finish(n, after):
        gsm, other = _pair_wait(*pair_flight.pop(n), after, f"{n}_pair_wait")
        if n in PAIR_SUM_ON_SPARSECORE:
            summing[n] = _pair_add_sc(gsm, other, f"{n}_pair_add_sc")
            return None
        *in_flight[n], token = _chip_start(_pair_add(gsm, other, f"{n}_pair_add"), f"{n}_chip_start")
        return token

    def emit_send(n):
        *in_flight[n], token = _chip_start(summing.pop(n), f"{n}_chip_start")
        return token

    grads = {}
    updates = {}

    def reduce_done(n, after):
        part, slots = _chip_wait(*in_flight.pop(n), after, f"{n}_chip_wait")
        grads[n], token = _pair_join(_chip_sum(part, slots, f"{n}_chip_sum"), f"{n}_pair_join")
        return token

    def early_update(after):
        tokens = [reduce_done(n, after) for n in ON_SPARSECORE]
        for n in ON_SPARSECORE:
            g2 = grads[n]
            updates[n] = _adamw_sc(wl[n].reshape(g2.shape), g2, ml[n].reshape(g2.shape), vl[n].reshape(g2.shape),
                                   f"adamw_sc_{n}")
        return tokens

    loss, dx0, g = _local_step(x[0], mem[0], loss_target[0], vec, weight,
                               (emit_begin, emit_finish, emit_send, early_update), dep0)
    assert not summing

    for n in list(in_flight):
        reduce_done(n, dx0)
    widths = [wl[n].size for n in REPLICATED]
    total = sum(widths)
    rows = -(-total // SMALL_COLS)
    rows = -(-rows // 8) * 8
    packed = jnp.concatenate([g[n].reshape(-1) for n in REPLICATED] + [jnp.zeros((rows * SMALL_COLS - total,), F32)])
    summed = _all_reduce_small(packed.reshape(rows, SMALL_COLS), "small_all_reduce").reshape(-1)
    off = 0
    for n, width in zip(REPLICATED, widths):
        grads[n] = summed[off:off + width].reshape(1, width)
        off += width

    out_g, out_d, out_m, out_v = [], [], [], []
    for n in WEIGHTS:
        shape = wl[n].shape
        g2 = grads[n]
        if n in updates:
            go, d, nm, nv = updates[n]
        else:
            dep = ()
            if n == "w_in":
                dep = tuple(updates[k][1][:8, :LANES] for k in updates if k != SLOW_ON_SPARSECORE)
            if n == "ffn1_w_gate":
                dep = (updates[SLOW_ON_SPARSECORE][1][:8, :LANES],)
            go, d, nm, nv = _adamw(wl[n].reshape(g2.shape), g2, ml[n].reshape(g2.shape), vl[n].reshape(g2.shape),
                                   f"adamw_{n}", dep)
        out_g.append(go.reshape(shape))
        out_d.append(d.reshape(shape))
        out_m.append(nm.reshape(shape))
        out_v.append(nv.reshape(shape))
    return (loss, dx0.reshape(x.shape), *out_g, *out_d, *out_m, *out_v)
```

```python
import functools

import jax
import jax.numpy as jnp
from jax import lax
from jax.experimental import pallas as pl
from jax.experimental.pallas import tpu as pltpu
from jax.experimental.pallas import tpu_sc as plsc

F32 = jnp.float32
BF16 = jnp.bfloat16
MESH = pl.DeviceIdType.MESH

RMS_EPS = 1e-6
CHUNK = 64
POOL_WINDOWS = (2, 4, 8, 16)
POOL_HALO = 16
N_HEADS = 4
GATE_TEMP = 16.0
ADAM_LR, ADAM_B1, ADAM_B2, ADAM_EPS, ADAM_WD, ADAM_STEP = 0.001, 0.9, 0.999, 1e-08, 0.01, 10
N_SHARDS = 4
LANES = 128
MXU_COLS = 256
TOKENS_PER_STEP = 2048
VMEM_LIMIT = 58 * 1024 * 1024

ANY = pl.BlockSpec(memory_space=pl.ANY)
HBM = pl.BlockSpec(memory_space=pltpu.HBM)
SEM = pl.BlockSpec(memory_space=pltpu.SEMAPHORE)
EFFECT = pltpu.SideEffectType.DATAFLOW_SIDE_EFFECTING


def _params(**kw):
    return pltpu.CompilerParams(vmem_limit_bytes=VMEM_LIMIT, **kw)


def _tile(n, want):
    for unit in (LANES, 8):
        t = (min(want, n) // unit) * unit
        while t >= unit:
            if n % t == 0:
                return t
            t -= unit
    return n


def _dot(a, b, dims):
    return lax.dot_general(a, b, (dims, ((), ())), preferred_element_type=F32)


def _nn(a, b):
    return _dot(a, b, ((1,), (0,)))


def _nt(a, b):
    return _dot(a, b, ((1,), (1,)))


def _tn(a, b):
    return _dot(a, b, ((0,), (0,)))


def _sigmoid(x):
    return 1.0 / (1.0 + jnp.exp(-x))


def _matmul(a, b, *, mode, name, out_dtype, tm=512, tn=2048, tk=2048, res=None, scale=1.0, b_groups=False, out_groups=0,
            dep=(), cols_outer=False):
    if mode == "tn":
        K, M = a.shape
    else:
        M, K = a.shape
    if mode == "nn":
        if b_groups:
            G, _, Nj = b.shape
            N = G * Nj
        else:
            N = b.shape[1]
    elif mode == "nt":
        if b_groups:
            G, N, Kj = b.shape
            assert G * Kj == K
        else:
            N = b.shape[0]
    else:
        N = b.shape[1]
    tm = _tile(M, tm)
    if mode == "nn" and b_groups:
        tn = _tile(Nj, tn)
    elif out_groups:
        tn = _tile(N // out_groups, tn)
    else:
        tn = _tile(N, tn)
    if mode == "nt" and b_groups:
        tk = _tile(Kj, tk)
    else:
        tk = _tile(K, tk)
    nk = K // tk
    grid = (M // tm, N // tn, nk)

    if mode == "tn":
        a_spec = pl.BlockSpec((tk, tm), lambda i, j, k: (k, i))
        b_spec = pl.BlockSpec((tk, tn), lambda i, j, k: (k, j))
        dims = ((0,), (0,))
    elif mode == "nn":
        a_spec = pl.BlockSpec((tm, tk), lambda i, j, k: (i, k))
        if b_groups:
            npj = Nj // tn
            b_spec = pl.BlockSpec((None, tk, tn), lambda i, j, k: (j // npj, k, j % npj))
        else:
            b_spec = pl.BlockSpec((tk, tn), lambda i, j, k: (k, j))
        dims = ((1,), (0,))
    else:
        a_spec = pl.BlockSpec((tm, tk), lambda i, j, k: (i, k))
        if b_groups:
            kpj = Kj // tk
            b_spec = pl.BlockSpec((None, tn, tk), lambda i, j, k: (k // kpj, j, k % kpj))
        else:
            b_spec = pl.BlockSpec((tn, tk), lambda i, j, k: (j, k))
        dims = ((1,), (1,))
    if out_groups:
        npj = (N // out_groups) // tn
        o_spec = pl.BlockSpec((None, tm, tn), lambda i, j, k: (j // npj, i, j % npj))
        out_shape = jax.ShapeDtypeStruct((out_groups, M, N // out_groups), out_dtype)
    else:
        o_spec = pl.BlockSpec((tm, tn), lambda i, j, k: (i, j))
        out_shape = jax.ShapeDtypeStruct((M, N), out_dtype)
    in_specs = [a_spec, b_spec]
    operands = [a, b]
    if res is not None:
        in_specs.append(pl.BlockSpec((tm, tn), lambda i, j, k: (i, j)))
        operands.append(res)
    has_res = res is not None
    n_dep = len(dep)
    for d in dep:
        in_specs.append(pl.BlockSpec(d.shape, lambda i, j, k: (0, 0)))
        operands.append(d)
    if cols_outer:
        def swapped(spec):
            return pl.BlockSpec(spec.block_shape, lambda j, i, k, f=spec.index_map: f(i, j, k))
        in_specs = [swapped(s) for s in in_specs]
        o_spec = swapped(o_spec)
        grid = (grid[1], grid[0], grid[2])

    def body(*refs):
        if has_res:
            a_ref, b_ref, r_ref = refs[:3]
        else:
            a_ref, b_ref = refs[:2]
            r_ref = None
        o_ref = refs[2 + has_res + n_dep]

        def finish(acc):
            if scale != 1.0:
                acc = acc * scale
            if r_ref is not None:
                acc = r_ref[...] + acc
            o_ref[...] = acc.astype(o_ref.dtype)

        part = _dot(a_ref[...], b_ref[...], dims)
        if nk == 1:
            finish(part)
        else:
            acc_ref = o_ref if in_place else refs[-1]
            k = pl.program_id(2)

            @pl.when(k == 0)
            def _():
                acc_ref[...] = part

            @pl.when(k > 0)
            def _():
                acc_ref[...] += part

            if not in_place:
                @pl.when(k == nk - 1)
                def _():
                    finish(acc_ref[...])

    in_place = out_dtype == F32 and res is None and scale == 1.0
    scratch = [] if nk == 1 or in_place else [pltpu.VMEM((tm, tn), F32)]
    return pl.pallas_call(body, name=name, grid=grid, in_specs=in_specs, out_specs=o_spec, out_shape=out_shape,
                          scratch_shapes=scratch, compiler_params=_params())(*operands)


def _rms_fwd(x, gain, name, tm=512, dep=None):
    S, D = x.shape
    tm = _tile(S, tm)

    def body(x_ref, g_ref, *rest):
        o_ref = rest[-1]
        xv = x_ref[...]
        r = lax.rsqrt(jnp.mean(xv * xv, axis=-1, keepdims=True) + RMS_EPS)
        o_ref[...] = (xv * r * g_ref[...]).astype(o_ref.dtype)

    in_specs = [pl.BlockSpec((tm, D), lambda i: (i, 0)), pl.BlockSpec((1, D), lambda i: (0, 0))]
    operands = [x, gain]
    if dep is not None:
        in_specs.append(pl.BlockSpec(dep.shape, lambda i: (0, 0)))
        operands.append(dep)
    return pl.pallas_call(body, name=name, grid=(S // tm,), in_specs=in_specs,
                          out_specs=pl.BlockSpec((tm, D), lambda i: (i, 0)),
                          out_shape=jax.ShapeDtypeStruct((S, D), BF16), compiler_params=_params())(*operands)


def _rms_bwd(x, gain, dh, dres, name, lowp=None, tm=512):
    half = lowp is not None
    S, D = x.shape
    tm = _tile(S, tm)
    has_res = dres is not None

    def body(*refs):
        if has_res:
            x_ref, g_ref, dh_ref, dr_ref = refs[:4]
            outs = refs[4:]
        else:
            x_ref, g_ref, dh_ref = refs[:3]
            dr_ref = None
            outs = refs[3:]
        dx_ref, dg_ref = outs[0], outs[-1]
        xv = x_ref[...]
        dhv = dh_ref[...].astype(F32)
        r = lax.rsqrt(jnp.mean(xv * xv, axis=-1, keepdims=True) + RMS_EPS)
        gy = dhv * g_ref[...]
        dx = r * gy - xv * (r * r * r) * jnp.mean(gy * xv, axis=-1, keepdims=True)
        if dr_ref is not None:
            dx = dx + dr_ref[...]
        dx_ref[...] = dx
        if half:
            outs[1][...] = (dx if lowp == 1.0 else lowp * dx).astype(BF16)
        part = jnp.sum(dhv * xv * r, axis=0, keepdims=True)

        @pl.when(pl.program_id(0) == 0)
        def _():
            dg_ref[...] = part

        @pl.when(pl.program_id(0) > 0)
        def _():
            dg_ref[...] += part

    row = pl.BlockSpec((tm, D), lambda i: (i, 0))
    vec = pl.BlockSpec((1, D), lambda i: (0, 0))
    in_specs = [row, vec, row] + ([row] if has_res else [])
    operands = [x, gain, dh] + ([dres] if has_res else [])
    out_specs = [row] + ([row] if half else []) + [vec]
    out_shape = [jax.ShapeDtypeStruct((S, D), F32)] + ([jax.ShapeDtypeStruct((S, D), BF16)] if half else []) + [
        jax.ShapeDtypeStruct((1, D), F32)]
    return pl.pallas_call(body, name=name, grid=(S // tm,), in_specs=in_specs, out_specs=out_specs, out_shape=out_shape,
                          compiler_params=_params())(*operands)


def _loss_head(x, gain, target, name, tm=512):
    S, D = x.shape
    tm = _tile(S, tm)

    def body(x_ref, g_ref, t_ref, sq_ref, dx_ref, dxh_ref, dg_ref):
        xv = x_ref[...]
        r = lax.rsqrt(jnp.mean(xv * xv, axis=-1, keepdims=True) + RMS_EPS)
        xn = xv * r
        err = xn * g_ref[...] - t_ref[...]
        dout = err * (1.0 / D)
        gy = dout * g_ref[...]
        dx = r * gy - xv * (r * r * r) * jnp.mean(gy * xv, axis=-1, keepdims=True)
        dx_ref[...] = dx
        dxh_ref[...] = (0.5 * dx).astype(BF16)
        sq = jnp.sum(err * err, axis=0, keepdims=True)
        dg = jnp.sum(dout * xn, axis=0, keepdims=True)

        @pl.when(pl.program_id(0) == 0)
        def _():
            sq_ref[...] = sq
            dg_ref[...] = dg

        @pl.when(pl.program_id(0) > 0)
        def _():
            sq_ref[...] += sq
            dg_ref[...] += dg

    row = pl.BlockSpec((tm, D), lambda i: (i, 0))
    vec = pl.BlockSpec((1, D), lambda i: (0, 0))
    return pl.pallas_call(body, name=name, grid=(S // tm,), in_specs=[row, vec, row], out_specs=[vec, row, row, vec],
                          out_shape=[jax.ShapeDtypeStruct((1, D), F32), jax.ShapeDtypeStruct((S, D), F32),
                                     jax.ShapeDtypeStruct((S, D), BF16), jax.ShapeDtypeStruct((1, D), F32)],
                          compiler_params=_params())(x, gain, target)


def _cast(x, dtype, name, scale=1.0, tm=256):
    S, D = x.shape
    tm = _tile(S, tm)

    def body(x_ref, o_ref):
        o_ref[...] = (x_ref[...] * scale).astype(o_ref.dtype)

    row = pl.BlockSpec((tm, D), lambda i: (i, 0))
    return pl.pallas_call(body, name=name, grid=(S // tm,), in_specs=[row], out_specs=row,
                          out_shape=jax.ShapeDtypeStruct((S, D), dtype), compiler_params=_params())(x)


def _ffn_up(h, wg, wu, name, tm=512):
    S, D = h.shape
    G, _, Fj = wg.shape
    tm = _tile(S, tm)

    def body(h_ref, wg_ref, wu_ref, ga_ref, gb_ref, hid_ref):
        hv = h_ref[...]
        a = _nn(hv, wg_ref[...])
        b = _nn(hv, wu_ref[...])
        s = _sigmoid(a)
        silu = a * s
        ga_ref[...] = (b * (s * (1.0 + a * (1.0 - s)))).astype(BF16)
        gb_ref[...] = silu.astype(BF16)
        hid_ref[...] = (silu * b).astype(BF16)

    w_spec = pl.BlockSpec((None, D, Fj), lambda g, i: (g, 0, 0))
    o_spec = pl.BlockSpec((tm, Fj), lambda g, i: (i, g))
    out = jax.ShapeDtypeStruct((S, G * Fj), BF16)
    return pl.pallas_call(body, name=name, grid=(G, S // tm),
                          in_specs=[pl.BlockSpec((tm, D), lambda g, i: (i, 0)), w_spec, w_spec],
                          out_specs=[o_spec, o_spec, o_spec], out_shape=[out, out, out], compiler_params=_params())(h, wg, wu)


def _ffn_dact(dxh, wd, ga, gb, name, tm=512):
    S, D = dxh.shape
    G, Fj, _ = wd.shape
    tm = _tile(S, tm)

    def body(dx_ref, wd_ref, ga_ref, gb_ref, da_ref, db_ref):
        dhid = _nt(dx_ref[...], wd_ref[...])
        da_ref[...] = (dhid * ga_ref[...].astype(F32)).astype(BF16)
        db_ref[...] = (dhid * gb_ref[...].astype(F32)).astype(BF16)

    blk = pl.BlockSpec((tm, Fj), lambda g, i: (i, g))
    out = jax.ShapeDtypeStruct((S, G * Fj), BF16)
    return pl.pallas_call(body, name=name, grid=(G, S // tm),
                          in_specs=[pl.BlockSpec((tm, D), lambda g, i: (i, 0)),
                                    pl.BlockSpec((None, Fj, D), lambda g, i: (g, 0, 0)), blk, blk],
                          out_specs=[blk, blk], out_shape=[out, out], compiler_params=_params())(dxh, wd, ga, gb)


def _ffn_dh(da, db, wg, wu, name, dep=(), tm=512):
    S = da.shape[0]
    G, D, Fj = wg.shape
    tm = _tile(S, tm)

    def body(da_ref, db_ref, wg_ref, wu_ref, *rest):
        o_ref = rest[-1]
        part = _nt(da_ref[...], wg_ref[...]) + _nt(db_ref[...], wu_ref[...])

        @pl.when(pl.program_id(1) == 0)
        def _():
            o_ref[...] = part

        @pl.when(pl.program_id(1) > 0)
        def _():
            o_ref[...] += part

    act = pl.BlockSpec((tm, Fj), lambda i, g: (i, g))
    w_spec = pl.BlockSpec((None, D, Fj), lambda i, g: (g, 0, 0))
    in_specs = [act, act, w_spec, w_spec] + [pl.BlockSpec(d.shape, lambda i, g: (0, 0)) for d in dep]
    return pl.pallas_call(body, name=name, grid=(S // tm, G), in_specs=in_specs,
                          out_specs=pl.BlockSpec((tm, D), lambda i, g: (i, 0)),
                          out_shape=jax.ShapeDtypeStruct((S, D), F32), compiler_params=_params())(da, db, wg, wu, *dep)


def _pool_fwd(proj, pool_w, pool_scale, name, tm=512):
    S = proj.shape[0]
    NG, C, _ = pool_w.shape
    DP = NG * C
    tm = _tile(S, tm)
    hb = tm // POOL_HALO
    n_ext = tm + POOL_HALO

    def body(u_ref, halo_ref, w_ref, sc_ref, y_ref, d_ref):
        i = pl.program_id(0)
        t = lax.broadcasted_iota(jnp.int32, (tm, 1), 0) + i * tm
        for g, win in enumerate(POOL_WINDOWS):
            cols = slice(g * C, (g + 1) * C)
            ug = u_ref[:, cols]
            halo = jnp.where(i > 0, halo_ref[:, cols], 0.0)
            acc = jnp.concatenate([halo, ug], axis=0)
            step = 1
            while step < win:
                acc = acc + pltpu.roll(acc, step, 0)
                step *= 2
            count = jnp.minimum(t + 1, win).astype(F32)
            d = (acc[POOL_HALO:, :] / count - ug).astype(BF16)
            d_ref[:, cols] = d
            y_ref[:, cols] = (_nn(d, w_ref[g]) * sc_ref[:, cols]).astype(BF16)

    del n_ext
    return pl.pallas_call(
        body, name=name, grid=(S // tm,),
        in_specs=[pl.BlockSpec((tm, DP), lambda i: (i, 0)),
                  pl.BlockSpec((POOL_HALO, DP), lambda i: (jnp.maximum(i * hb - 1, 0), 0)),
                  pl.BlockSpec((NG, C, C), lambda i: (0, 0, 0)), pl.BlockSpec((1, DP), lambda i: (0, 0))],
        out_specs=[pl.BlockSpec((tm, DP), lambda i: (i, 0)), pl.BlockSpec((tm, DP), lambda i: (i, 0))],
        out_shape=[jax.ShapeDtypeStruct((S, DP), BF16), jax.ShapeDtypeStruct((S, DP), BF16)],
        compiler_params=_params())(proj, proj, pool_w, pool_scale)


def _pool_bwd(dymix, d, pool_w, pool_scale, name, tm=512):
    S = dymix.shape[0]
    NG, C, _ = pool_w.shape
    DP = NG * C
    tm = _tile(S, tm)
    hb = tm // POOL_HALO
    nb = S // tm
    n_ext = tm + POOL_HALO
    last_halo = S // POOL_HALO - 1

    def body(dy_ref, halo_ref, d_ref, w_ref, sc_ref, du_ref, dw_ref, dsc_ref):
        i = pl.program_id(0)
        t = lax.broadcasted_iota(jnp.int32, (n_ext, 1), 0) + i * tm
        for g, win in enumerate(POOL_WINDOWS):
            cols = slice(g * C, (g + 1) * C)
            dy = dy_ref[:, cols]
            halo = jnp.where(i < nb - 1, halo_ref[:, cols], 0.0)
            sc = sc_ref[:, cols]
            dv = d_ref[:, cols]
            e_ext = (jnp.concatenate([dy, halo], axis=0) * sc).astype(BF16)
            dd = _nt(e_ext, w_ref[g])
            count = jnp.minimum(t + 1, win).astype(F32)
            acc = dd / count
            step = 1
            while step < win:
                acc = acc + pltpu.roll(acc, n_ext - step, 0)
                step *= 2
            du_ref[:, cols] = (acc[:tm, :] - dd[:tm, :]).astype(BF16)
            dw = _tn(dv, e_ext[:tm, :])
            dsc = jnp.sum(dy * _nn(dv, w_ref[g]), axis=0, keepdims=True)

            @pl.when(i == 0)
            def _():
                dw_ref[g] = dw
                dsc_ref[:, cols] = dsc

            @pl.when(i > 0)
            def _():
                dw_ref[g] += dw
                dsc_ref[:, cols] += dsc

    return pl.pallas_call(
        body, name=name, grid=(nb,),
        in_specs=[pl.BlockSpec((tm, DP), lambda i: (i, 0)),
                  pl.BlockSpec((POOL_HALO, DP), lambda i: (jnp.minimum((i + 1) * hb, last_halo), 0)),
                  pl.BlockSpec((tm, DP), lambda i: (i, 0)),
                  pl.BlockSpec((NG, C, C), lambda i: (0, 0, 0)), pl.BlockSpec((1, DP), lambda i: (0, 0))],
        out_specs=[pl.BlockSpec((tm, DP), lambda i: (i, 0)), pl.BlockSpec((NG, C, C), lambda i: (0, 0, 0)),
                   pl.BlockSpec((1, DP), lambda i: (0, 0))],
        out_shape=[jax.ShapeDtypeStruct((S, DP), BF16), jax.ShapeDtypeStruct((NG, C, C), F32),
                   jax.ShapeDtypeStruct((1, DP), F32)],
        compiler_params=_params())(dymix, dymix, d, pool_w, pool_scale)


def _chunk_scan(v, rows, reverse):
    n = v.shape[0]
    step = 1
    while step < CHUNK:
        if reverse:
            v = v + jnp.where(rows < CHUNK - step, pltpu.roll(v, n - step, 0), 0.0)
        else:
            v = v + jnp.where(rows >= step, pltpu.roll(v, step, 0), 0.0)
        step *= 2
    return v


def _log_decay(alr, w_a2, b_a):
    z = _nn(alr.astype(BF16), w_a2) + b_a
    la = (jnp.minimum(z, 0.0) - jnp.log(1.0 + jnp.exp(-jnp.abs(z)))) * (1.0 / GATE_TEMP)
    return z, la


def _gla_specs(DP, DKT, DV, tb, bmap):
    return [pl.BlockSpec((tb, DKT), lambda i: (bmap(i), DP // DKT)),
            pl.BlockSpec((tb, DKT), lambda i: (bmap(i), DP // DKT + 1)),
            pl.BlockSpec((tb, DV), lambda i: (bmap(i), (DP + 2 * DKT) // DV)),
            pl.BlockSpec((tb, DV), lambda i: (bmap(i), (DP + 2 * DKT) // DV + 1)),
            pl.BlockSpec((tb, LANES), lambda i: (bmap(i), (DP + 2 * DKT + 2 * DV) // LANES))]


def _gla_fwd(proj, y_pool, w_a2, b_a, head_norm, name, tb=512):
    S = proj.shape[0]
    DP = y_pool.shape[1]
    DKT = b_a.shape[1]
    DV = head_norm.shape[1]
    dk, dv = DKT // N_HEADS, DV // N_HEADS
    tb = _tile(S, tb)
    ncb = tb // CHUNK
    qscale = dk ** -0.5

    def body(q_ref, k_ref, v_ref, g_ref, alr_ref, yp_ref, wa_ref, ba_ref, hn_ref, y_ref, st_out_ref, st_ref, kdec_ref,
             gam_ref):
        @pl.when(pl.program_id(0) == 0)
        def _():
            st_ref[...] = jnp.zeros_like(st_ref)

        y_ref[:, :DP] = yp_ref[...]

        rows = lax.broadcasted_iota(jnp.int32, (tb, 1), 0) % CHUNK
        _, la = _log_decay(alr_ref[...], wa_ref[...], ba_ref[...])
        tail = _chunk_scan(la, rows, True)
        kdec_ref[...] = k_ref[...] * jnp.exp(tail - la)
        gam_ref[...] = jnp.exp(tail)

        def chunk(c, carry):
            r0 = pl.multiple_of(c * CHUNK, CHUNK)
            rs = pl.ds(r0, CHUNK)
            gam = gam_ref[pl.ds(r0, 1), :]
            heads = range(N_HEADS)
            kcs = [slice(h * dk, (h + 1) * dk) for h in heads]
            vcs = [slice(h * dv, (h + 1) * dv) for h in heads]
            upd = [_tn(v_ref[rs, vcs[h]].astype(BF16), kdec_ref[rs, kcs[h]].astype(BF16)) for h in heads]
            st = [st_ref[h] * gam[:, kcs[h]] + upd[h] for h in heads]
            o = [_nt((q_ref[rs, kcs[h]] * qscale).astype(BF16), st[h].astype(BF16)) for h in heads]
            for h in heads:
                st_ref[h] = st[h]
                st_out_ref[c, h] = st[h]
                r = lax.rsqrt(jnp.mean(o[h] * o[h], axis=-1, keepdims=True) + RMS_EPS)
                gv = g_ref[rs, vcs[h]]
                y_ref[rs, DP + h * dv:DP + (h + 1) * dv] = (o[h] * r * hn_ref[:, vcs[h]] * (gv * _sigmoid(gv))).astype(BF16)
            return carry

        lax.fori_loop(0, ncb, chunk, 0, unroll=4)

    full = lambda shape: pl.BlockSpec(shape, lambda i: (0,) * len(shape))
    return pl.pallas_call(
        body, name=name, grid=(S // tb,),
        in_specs=_gla_specs(DP, DKT, DV, tb, lambda i: i) + [pl.BlockSpec((tb, DP), lambda i: (i, 0)),
                                                            full((LANES, DKT)), full((1, DKT)), full((1, DV))],
        out_specs=[pl.BlockSpec((tb, DP + DV), lambda i: (i, 0)),
                   pl.BlockSpec((ncb, N_HEADS, dv, dk), lambda i: (i, 0, 0, 0))],
        out_shape=[jax.ShapeDtypeStruct((S, DP + DV), BF16), jax.ShapeDtypeStruct((S // CHUNK, N_HEADS, dv, dk), F32)],
        scratch_shapes=[pltpu.VMEM((N_HEADS, dv, dk), F32), pltpu.VMEM((tb, DKT), F32), pltpu.VMEM((tb, DKT), F32)],
        compiler_params=_params())(proj, proj, proj, proj, proj, y_pool, w_a2, b_a, head_norm)


def _gla_bwd(proj, states, dymix, du, w_a2, b_a, head_norm, name, tb=512):
    S = proj.shape[0]
    DP = du.shape[1]
    DKT = b_a.shape[1]
    DV = head_norm.shape[1]
    dk, dv = DKT // N_HEADS, DV // N_HEADS
    tb = _tile(S, tb)
    ncb = tb // CHUNK
    nb = S // tb
    qscale = dk ** -0.5
    rev = lambda i: nb - 1 - i

    q0, k0, v0, g0, a0 = DP, DP + DKT, DP + 2 * DKT, DP + 2 * DKT + DV, DP + 2 * DKT + 2 * DV

    def body(q_ref, k_ref, v_ref, g_ref, alr_ref, st_blk_ref, st_prev_ref, dy_ref, du_ref, wa_ref, ba_ref, hn_ref,
             dp_ref, dwa_ref, dba_ref, dhn_ref,
             dst_ref, kdec_ref, dec_ref, gam_ref, e_ref, dla_ref, dhn_acc_ref):
        i = pl.program_id(0)
        blk = rev(i)
        dp_ref[:, :DP] = du_ref[...]

        @pl.when(i == 0)
        def _():
            dst_ref[...] = jnp.zeros_like(dst_ref)

        dhn_acc_ref[...] = jnp.zeros_like(dhn_acc_ref)
        rows = lax.broadcasted_iota(jnp.int32, (tb, 1), 0) % CHUNK
        z, la = _log_decay(alr_ref[...], wa_ref[...], ba_ref[...])
        tail = _chunk_scan(la, rows, True)
        dec_ref[...] = jnp.exp(tail - la)
        kdec_ref[...] = k_ref[...] * dec_ref[...]
        gam_ref[...] = jnp.exp(tail)

        def chunk(cc, carry):
            c = ncb - 1 - cc
            r0 = pl.multiple_of(c * CHUNK, CHUNK)
            rs = pl.ds(r0, CHUNK)
            gam = gam_ref[pl.ds(r0, 1), :]
            first = jnp.logical_and(blk == 0, c == 0)
            heads = range(N_HEADS)
            kcs = [slice(h * dk, (h + 1) * dk) for h in heads]
            vcs = [slice(h * dv, (h + 1) * dv) for h in heads]
            qs = [(q_ref[rs, kcs[h]] * qscale).astype(BF16) for h in heads]
            stb = [st_blk_ref[c, h].astype(BF16) for h in heads]
            o = [_nt(qs[h], stb[h]) for h in heads]
            do = []
            for h in heads:
                oh = o[h]
                r = lax.rsqrt(jnp.mean(oh * oh, axis=-1, keepdims=True) + RMS_EPS)
                gv = g_ref[rs, vcs[h]]
                sg = _sigmoid(gv)
                dy = dy_ref[rs, vcs[h]]
                hn = hn_ref[:, vcs[h]]
                on = oh * r
                dp_ref[rs, g0 + h * dv:g0 + (h + 1) * dv] = (dy * on * hn * (sg * (1.0 + gv * (1.0 - sg)))).astype(BF16)
                don = dy * (gv * sg)
                dhn_acc_ref[:, vcs[h]] += jnp.sum(don * on, axis=0, keepdims=True)
                dn = don * hn
                do.append((r * dn - oh * (r * r * r) * jnp.mean(dn * oh, axis=-1, keepdims=True)).astype(BF16))
            dqs = [_nn(do[h], stb[h]) for h in heads]
            dst = [dst_ref[h] + _tn(do[h], qs[h]) for h in heads]
            for h in heads:
                dp_ref[rs, q0 + h * dk:q0 + (h + 1) * dk] = (dqs[h] * qscale).astype(BF16)
            dstb = [dst[h].astype(BF16) for h in heads]
            dvh = [_nt(kdec_ref[rs, kcs[h]].astype(BF16), dstb[h]) for h in heads]
            dkdec = [_nn(v_ref[rs, vcs[h]].astype(BF16), dstb[h]) for h in heads]
            gdg = []
            for h in heads:
                dp_ref[rs, v0 + h * dv:v0 + (h + 1) * dv] = dvh[h].astype(BF16)
                dp_ref[rs, k0 + h * dk:k0 + (h + 1) * dk] = (dkdec[h] * dec_ref[rs, kcs[h]]).astype(BF16)
                e_ref[rs, kcs[h]] = dkdec[h] * kdec_ref[rs, kcs[h]]
                st_prev = jnp.where(c > 0, st_blk_ref[jnp.maximum(c - 1, 0), h], st_prev_ref[0, h])
                st_prev = jnp.where(first, 0.0, st_prev)
                gdg.append(jnp.sum(dst[h] * st_prev, axis=0, keepdims=True) * gam[:, kcs[h]])
                dst_ref[h] = dst[h] * gam[:, kcs[h]]
            dla_ref[rs, :] = jnp.broadcast_to(jnp.concatenate(gdg, axis=1), (CHUNK, DKT))
            return carry

        lax.fori_loop(0, ncb, chunk, 0, unroll=4)

        ev = e_ref[...]
        dla = dla_ref[...] + _chunk_scan(ev, rows, False) - ev
        dz = dla * (1.0 / GATE_TEMP) * (1.0 - _sigmoid(z))
        dzb = dz.astype(BF16)
        dp_ref[:, a0:a0 + LANES] = _nt(dzb, wa_ref[...]).astype(BF16)
        dwa = _tn(alr_ref[...].astype(BF16), dzb)
        dba = jnp.sum(dz, axis=0, keepdims=True)

        @pl.when(i == 0)
        def _():
            dwa_ref[...] = dwa
            dba_ref[...] = dba
            dhn_ref[...] = dhn_acc_ref[...]

        @pl.when(i > 0)
        def _():
            dwa_ref[...] += dwa
            dba_ref[...] += dba
            dhn_ref[...] += dhn_acc_ref[...]

    full = lambda shape: pl.BlockSpec(shape, lambda i: (0,) * len(shape))
    rowblk = lambda w: pl.BlockSpec((tb, w), lambda i: (rev(i), 0))
    return pl.pallas_call(
        body, name=name, grid=(nb,),
        in_specs=_gla_specs(DP, DKT, DV, tb, rev) + [
            pl.BlockSpec((ncb, N_HEADS, dv, dk), lambda i: (rev(i), 0, 0, 0)),
            pl.BlockSpec((1, N_HEADS, dv, dk), lambda i: (jnp.maximum(rev(i) * ncb - 1, 0), 0, 0, 0)),
            pl.BlockSpec((tb, DV), lambda i: (rev(i), DP // DV)), rowblk(DP),
            full((LANES, DKT)), full((1, DKT)), full((1, DV))],
        out_specs=[rowblk(a0 + LANES), full((LANES, DKT)), full((1, DKT)), full((1, DV))],
        out_shape=[jax.ShapeDtypeStruct((S, a0 + LANES), BF16), jax.ShapeDtypeStruct((LANES, DKT), F32),
                   jax.ShapeDtypeStruct((1, DKT), F32), jax.ShapeDtypeStruct((1, DV), F32)],
        scratch_shapes=[pltpu.VMEM((N_HEADS, dv, dk), F32)] + [pltpu.VMEM((tb, DKT), F32)] * 5 + [pltpu.VMEM((1, DV), F32)],
        compiler_params=_params())(proj, proj, proj, proj, proj, states, states, dymix, du, w_a2, b_a, head_norm)


def _xattn_fwd(q, kv, name, tm=512):
    S, D = q.shape
    M = kv.shape[0]
    hd = D // N_HEADS
    tm = _tile(S, tm)
    scale = hd ** -0.5

    def body(q_ref, k_ref, v_ref, o_ref):
        heads = range(N_HEADS)
        hcs = [slice(h * hd, (h + 1) * hd) for h in heads]
        s = [_nt(q_ref[:, hc], k_ref[:, hc]) * scale for hc in hcs]
        p = []
        for h in heads:
            e = jnp.exp(s[h] - jnp.max(s[h], axis=-1, keepdims=True))
            p.append((e / jnp.sum(e, axis=-1, keepdims=True)).astype(BF16))
        o = [_nn(p[h], v_ref[:, hcs[h]]) for h in heads]
        for h in heads:
            o_ref[:, hcs[h]] = o[h].astype(BF16)

    return pl.pallas_call(body, name=name, grid=(S // tm,),
                          in_specs=[pl.BlockSpec((tm, D), lambda i: (i, 0)), pl.BlockSpec((M, D), lambda i: (0, 0)),
                                    pl.BlockSpec((M, D), lambda i: (0, 1))],
                          out_specs=pl.BlockSpec((tm, D), lambda i: (i, 0)),
                          out_shape=jax.ShapeDtypeStruct((S, D), BF16), compiler_params=_params())(q, kv, kv)


def _xattn_bwd(q, kv, do, name, tm=512):
    S, D = q.shape
    M = kv.shape[0]
    hd = D // N_HEADS
    tm = _tile(S, tm)
    scale = hd ** -0.5

    def body(q_ref, k_ref, v_ref, do_ref, dq_ref, dkv_ref):
        first = pl.program_id(0) == 0
        heads = range(N_HEADS)
        hcs = [slice(h * hd, (h + 1) * hd) for h in heads]
        s = [_nt(q_ref[:, hc], k_ref[:, hc]) * scale for hc in hcs]
        dp = [_nt(do_ref[:, hc], v_ref[:, hc]) for hc in hcs]
        p = []
        for h in heads:
            e = jnp.exp(s[h] - jnp.max(s[h], axis=-1, keepdims=True))
            p.append(e / jnp.sum(e, axis=-1, keepdims=True))
        dvh = [_tn(p[h].astype(BF16), do_ref[:, hcs[h]]) for h in heads]
        ds = [((p[h] * (dp[h] - jnp.sum(dp[h] * p[h], axis=-1, keepdims=True))) * scale).astype(BF16) for h in heads]
        dqh = [_nn(ds[h], k_ref[:, hcs[h]]) for h in heads]
        dkh = [_tn(ds[h], q_ref[:, hcs[h]]) for h in heads]
        for h in heads:
            dq_ref[:, hcs[h]] = dqh[h].astype(BF16)

        @pl.when(first)
        def _():
            for h in heads:
                dkv_ref[:, hcs[h]] = dkh[h]
                dkv_ref[:, D + h * hd:D + (h + 1) * hd] = dvh[h]

        @pl.when(jnp.logical_not(first))
        def _():
            for h in heads:
                dkv_ref[:, hcs[h]] += dkh[h]
                dkv_ref[:, D + h * hd:D + (h + 1) * hd] += dvh[h]

    row = pl.BlockSpec((tm, D), lambda i: (i, 0))
    return pl.pallas_call(body, name=name, grid=(S // tm,),
                          in_specs=[row, pl.BlockSpec((M, D), lambda i: (0, 0)), pl.BlockSpec((M, D), lambda i: (0, 1)), row],
                          out_specs=[row, pl.BlockSpec((M, 2 * D), lambda i: (0, 0))],
                          out_shape=[jax.ShapeDtypeStruct((S, D), BF16), jax.ShapeDtypeStruct((M, 2 * D), F32)],
                          compiler_params=_params())(q, kv, kv, do)


def _local_step(x, mem, target, vec, weight, emit, dep0):
    g = {}
    pending = []
    begun = []
    summed = []
    emit_begin, emit_finish, emit_send, early_update = emit

    def behind(fn, *a, **kw):
        dep = tuple(pending)
        pending.clear()
        out = fn(*a, dep=dep, **kw)
        while summed:
            pending.append(emit_send(summed.pop(0)))
        while begun:
            name = begun.pop(0)
            token = emit_finish(name, out)
            if token is None:
                summed.append(name)
            else:
                pending.append(token)
        return out

    def mm(a, b, **kw):
        return behind(_matmul, a, b, **kw)

    def send(name, gfull):
        pending.append(emit_begin(name, gfull))
        begun.append(name)

    def ffn_fwd(xin, tag, dep):
        h = _rms_fwd(xin, vec[f"{tag}_norm"], f"{tag}_norm", dep=dep)
        ga, gb, hid = _ffn_up(h, weight(f"{tag}_w_gate", h), weight(f"{tag}_w_up", h), f"{tag}_up")
        wd = weight(f"{tag}_w_down", hid)
        G, Fj, D = wd.shape
        xo = _matmul(hid, wd.reshape(G * Fj, D), mode="nn", name=f"{tag}_down", out_dtype=F32, res=xin, scale=0.5,
                     tn=1024, tk=G * Fj)
        return xo, (h, ga, gb, hid)

    def ffn_bwd(dxh, saved, tag, kept_back=None):
        h, ga, gb, hid = saved
        wg, wu, wd = weight(f"{tag}_w_gate"), weight(f"{tag}_w_up"), weight(f"{tag}_w_down")
        G, Fj, D = wd.shape
        send(f"{tag}_w_down", mm(hid, dxh, mode="tn", name=f"{tag}_dwd", out_dtype=F32, tm=Fj, tn=1024))
        da, db = _ffn_dact(dxh, wd, ga, gb, f"{tag}_dact")
        send(f"{tag}_w_gate", mm(h, da, mode="tn", name=f"{tag}_dwg", out_dtype=F32, tm=1024, tn=Fj, tk=TOKENS_PER_STEP,
                                  out_groups=G))
        send(f"{tag}_w_up", mm(h, db, mode="tn", name=f"{tag}_dwu", out_dtype=F32, tm=1024, tn=Fj, tk=TOKENS_PER_STEP,
                                out_groups=G))
        if kept_back is not None:
            name, gfull = kept_back()
            pending.append(emit_begin(name, gfull))
            pending.append(emit_finish(name, pending[-1]))
        return behind(_ffn_dh, da, db, wg, wu, f"{tag}_dh")

    x1, ffn1_saved = ffn_fwd(x, "ffn1", dep0)
    h2 = _rms_fwd(x1, vec["mix_norm"], "mix_norm")
    w_in = weight("w_in", h2)
    proj = _matmul(h2, w_in, mode="nn", name="w_in", out_dtype=F32, tn=1408, cols_outer=True)
    pool_w, w_a2 = weight("pool_w", h2), weight("gla_w_a2", h2)
    y_pool, dpool = _pool_fwd(proj, pool_w, vec["pool_scale"], "pool_fwd")
    ymix, states = _gla_fwd(proj, y_pool, w_a2, vec["gla_b_a"], vec["gla_head_norm"], "gla_fwd")
    w_out = weight("w_out", ymix)
    x2 = _matmul(ymix, w_out, mode="nn", name="w_out", out_dtype=F32, res=x1)
    h3 = _rms_fwd(x2, vec["xattn_norm"], "xattn_norm")
    mh = _rms_fwd(mem, vec["mem_norm"], "mem_norm")
    w_q = weight("xattn_w_q", h3)
    q = _matmul(h3, w_q, mode="nn", name="xattn_q", out_dtype=BF16)
    w_kv = weight("xattn_w_kv", q)
    kv = _matmul(mh, w_kv, mode="nn", name="xattn_kv", out_dtype=BF16, b_groups=True, tn=1024)
    o = _xattn_fwd(q, kv, "xattn_fwd")
    w_o = weight("xattn_w_o", o)
    x3 = _matmul(o, w_o, mode="nn", name="xattn_o", out_dtype=F32, res=x2)
    x4, ffn2_saved = ffn_fwd(x3, "ffn2", None)
    sq, dx4, dx4h, g["final_norm"] = _loss_head(x4, vec["final_norm"], target, "loss_head")
    loss = lax.psum(0.5 * jnp.sum(sq) / x.shape[-1], ("x", "y", "c"))
    pending.append(loss.reshape(1, 1))

    dh = ffn_bwd(dx4h, ffn2_saved, "ffn2")
    dx3, dx3b, g["ffn2_norm"] = _rms_bwd(x3, vec["ffn2_norm"], dh, dx4, "ffn2_norm_bwd", lowp=1.0)
    send("xattn_w_o", mm(o, dx3b, mode="tn", name="xattn_dwo", out_dtype=F32, tm=1024, tn=1024, tk=TOKENS_PER_STEP))
    do = mm(dx3b, w_o, mode="nt", name="xattn_do", out_dtype=BF16)
    dq, dkv = _xattn_bwd(q, kv, do, "xattn_bwd")
    send("xattn_w_q", mm(h3, dq, mode="tn", name="xattn_dwq", out_dtype=F32, tm=1024, tn=1024, tk=TOKENS_PER_STEP))
    dh3 = mm(dq, w_q, mode="nt", name="xattn_dh", out_dtype=F32)
    dkvb = _cast(dkv, BF16, "dkv_cast")
    send("xattn_w_kv", mm(mh, dkvb, mode="tn", name="xattn_dwkv", out_dtype=F32, tm=1024, tn=1024, out_groups=N_SHARDS))
    dmh = mm(dkvb, w_kv, mode="nt", name="xattn_dmh", out_dtype=F32, b_groups=True, tk=1024)
    _, g["mem_norm"] = _rms_bwd(mem, vec["mem_norm"], dmh, None, "mem_norm_bwd")
    pending.append(g["mem_norm"])
    dx2, dx2b, g["xattn_norm"] = _rms_bwd(x2, vec["xattn_norm"], dh3, dx3, "xattn_norm_bwd", lowp=1.0)
    send("w_out", mm(ymix, dx2b, mode="tn", name="dw_out", out_dtype=F32, tm=1024, tn=1024, tk=TOKENS_PER_STEP))
    dymix = mm(dx2b, w_out, mode="nt", name="dymix", out_dtype=F32)
    du, dpool_w, g["pool_scale"] = _pool_bwd(dymix, dpool, pool_w, vec["pool_scale"], "pool_bwd")
    send("pool_w", dpool_w)
    dproj, dw_a2, g["gla_b_a"], g["gla_head_norm"] = _gla_bwd(
        proj, states, dymix, du, w_a2, vec["gla_b_a"], vec["gla_head_norm"], "gla_bwd")
    send("gla_w_a2", dw_a2)
    dh2 = mm(dproj, w_in, mode="nt", name="dh2", out_dtype=F32, tn=1024, tk=dproj.shape[1])
    pending.extend(early_update(dh2))
    dx1, dx1h, g["mix_norm"] = _rms_bwd(x1, vec["mix_norm"], dh2, dx2, "mix_norm_bwd", lowp=0.5)
    dh = ffn_bwd(dx1h, ffn1_saved, "ffn1", kept_back=lambda: (
        "w_in", mm(h2, dproj, mode="tn", name="dw_in", out_dtype=F32, tm=1024, tn=1408, tk=TOKENS_PER_STEP)))
    dx0, g["ffn1_norm"] = _rms_bwd(x, vec["ffn1_norm"], dh, dx1, "ffn1_norm_bwd")
    return loss, dx0, g


def _place():
    x, y, c = lax.axis_index("x"), lax.axis_index("y"), lax.axis_index("c")
    chips = [(1 - x, y), (x, 1 - y), (1 - x, 1 - y)]
    return x, y, c, chips


def _ids():
    return jnp.stack([2 * lax.axis_index("x") + lax.axis_index("y"), lax.axis_index("c")]).astype(jnp.int32)


def _hbm(a):
    return pltpu.with_memory_space_constraint(a, pltpu.HBM)


def _cast_to_slot(w2d, dtype, name, dep=None):
    R, C = w2d.shape
    tr = _tile(R, max(16, (4 << 20) // (4 * C) // 16 * 16))

    def body(i_ref, w_ref, *rest):
        rest[-1][...] = w_ref[...].astype(dtype)

    in_specs = [pl.BlockSpec((tr, C), lambda r, i: (r, 0))]
    operands = [w2d]
    if dep is not None:
        in_specs.append(pl.BlockSpec(dep.shape, lambda r, i: (0, 0)))
        operands.append(dep)
    grid_spec = pltpu.PrefetchScalarGridSpec(num_scalar_prefetch=1, grid=(R // tr,), in_specs=in_specs,
                                             out_specs=pl.BlockSpec((None, tr, C), lambda r, i: (i[0], r, 0)))
    return pl.pallas_call(body, name=name, grid_spec=grid_spec, out_shape=jax.ShapeDtypeStruct((N_SHARDS, R, C), dtype),
                          compiler_params=_params())(_ids(), *operands)


def _gather_copies(buf_ref, send_sems, recv_sems, incoming, whole):
    x, y, c, chips = _place()
    hr = buf_ref.shape[1] // 2
    copies = []
    for j, (px, py) in enumerate(chips):
        slot = 2 * px + py if incoming else 2 * x + y
        part = buf_ref.at[slot] if whole else buf_ref.at[slot, pl.ds(c * hr, hr), :]
        copies.append(pltpu.make_async_remote_copy(src_ref=part, dst_ref=part, send_sem=send_sems.at[j],
                                                   recv_sem=recv_sems.at[j], device_id=(px, py, c), device_id_type=MESH))
    return copies


def _gather_start(buf, name, whole):
    def body(b_ref, send_sems, recv_sems, b_thru, token):
        for cp in _gather_copies(b_ref, send_sems, recv_sems, False, whole):
            cp.start()
        token[...] = jnp.zeros_like(token)

    return pl.pallas_call(
        body, name=name,
        out_shape=(pltpu.SemaphoreType.DMA((3,)), pltpu.SemaphoreType.DMA((3,)), pltpu.HBM(buf.shape, buf.dtype),
                   jax.ShapeDtypeStruct((8, LANES), F32)),
        in_specs=(HBM,), out_specs=(SEM, SEM, HBM, pl.BlockSpec(memory_space=pltpu.VMEM)), input_output_aliases={0: 2},
        compiler_params=pltpu.CompilerParams(has_side_effects=EFFECT))(_hbm(buf))


def _gather_wait(send_sems, recv_sems, buf_thru, after, name, whole):
    def body(b_ref, send_sems, recv_sems, after_ref, b_out):
        for cp in _gather_copies(b_ref, send_sems, recv_sems, False, whole):
            cp.wait_send()
        for cp in _gather_copies(b_ref, send_sems, recv_sems, True, whole):
            cp.wait_recv()

    return pl.pallas_call(
        body, name=name, out_shape=pltpu.HBM(buf_thru.shape, buf_thru.dtype),
        in_specs=(HBM, SEM, SEM, ANY), out_specs=HBM, input_output_aliases={0: 0},
        compiler_params=pltpu.CompilerParams(has_side_effects=EFFECT))(buf_thru, send_sems, recv_sems, after)


def _gather_forward(buf, name):
    G, R, C = buf.shape
    hr = R // 2

    def body(b_ref, o_ref, send_sems, recv_sems):
        x, y, c, chips = _place()
        copies = []
        for j, (px, py) in enumerate(chips):
            half = o_ref.at[2 * px + py, pl.ds(c * hr, hr), :]
            copies.append(pltpu.make_async_remote_copy(src_ref=half, dst_ref=half, send_sem=send_sems.at[j],
                                                       recv_sem=recv_sems.at[j], device_id=(x, y, 1 - c),
                                                       device_id_type=MESH))
        for cp in copies:
            cp.start()
        for j, (px, py) in enumerate(chips):
            half = o_ref.at[2 * px + py, pl.ds((1 - c) * hr, hr), :]
            pltpu.make_async_remote_copy(src_ref=half, dst_ref=half, send_sem=send_sems.at[j], recv_sem=recv_sems.at[j],
                                         device_id=(x, y, 1 - c), device_id_type=MESH).wait_recv()
        for cp in copies:
            cp.wait_send()

    return pl.pallas_call(body, name=name, in_specs=[ANY], out_specs=ANY, out_shape=jax.ShapeDtypeStruct(buf.shape, buf.dtype),
                          input_output_aliases={0: 0},
                          scratch_shapes=[pltpu.SemaphoreType.DMA((3,)), pltpu.SemaphoreType.DMA((3,))])(buf)


def _pair_copy(g_ref, land_ref, send_sem, recv_sem):
    x, y, c, _ = _place()
    hr = g_ref.shape[1] // 2
    return pltpu.make_async_remote_copy(src_ref=g_ref.at[:, pl.ds((1 - c) * hr, hr), :], dst_ref=land_ref,
                                        send_sem=send_sem, recv_sem=recv_sem, device_id=(x, y, 1 - c), device_id_type=MESH)


def _pair_start(gfull, name):
    G, R, C = gfull.shape

    def body(g_ref, land_ref, send_sem, recv_sem, g_thru, land_thru, token):
        _pair_copy(g_ref, land_ref, send_sem, recv_sem).start()
        token[...] = jnp.zeros_like(token)

    return pl.pallas_call(
        body, name=name,
        out_shape=(pltpu.SemaphoreType.DMA(()), pltpu.SemaphoreType.DMA(()), pltpu.HBM(gfull.shape, F32),
                   pltpu.HBM((G, R // 2, C), F32), jax.ShapeDtypeStruct((8, LANES), F32)),
        in_specs=(HBM, HBM), out_specs=(SEM, SEM, HBM, HBM, pl.BlockSpec(memory_space=pltpu.VMEM)),
        input_output_aliases={0: 2, 1: 3},
        compiler_params=pltpu.CompilerParams(has_side_effects=EFFECT))(_hbm(gfull), _hbm(lax.empty((G, R // 2, C), F32)))


def _pair_wait(send_sem, recv_sem, g_thru, land_thru, after, name):
    def body(g_ref, land_ref, send_sem, recv_sem, after_ref, g_out, land_out):
        cp = _pair_copy(g_ref, land_ref, send_sem, recv_sem)
        cp.wait_send()
        cp.wait_recv()

    return pl.pallas_call(
        body, name=name, out_shape=(pltpu.HBM(g_thru.shape, F32), pltpu.HBM(land_thru.shape, F32)),
        in_specs=(HBM, HBM, SEM, SEM, ANY), out_specs=(HBM, HBM), input_output_aliases={0: 0, 1: 1},
        compiler_params=pltpu.CompilerParams(has_side_effects=EFFECT))(g_thru, land_thru, send_sem, recv_sem, after)


def _pair_add(gfull, other, name):
    G, R, C = gfull.shape
    hr = R // 2
    tr = _tile(hr, max(8, (2 * 1024 * 1024) // (4 * C) // 8 * 8))
    nr = hr // tr
    c = lax.axis_index("c")
    cidx = jnp.reshape(c, (1,)).astype(jnp.int32)

    def body(c_ref, a_ref, b_ref, o_ref):
        o_ref[...] = a_ref[...] + b_ref[...]

    grid_spec = pltpu.PrefetchScalarGridSpec(
        num_scalar_prefetch=1, grid=(G, nr),
        in_specs=[pl.BlockSpec((None, tr, C), lambda g, r, cr: (g, cr[0] * nr + r, 0)),
                  pl.BlockSpec((None, tr, C), lambda g, r, cr: (g, r, 0))],
        out_specs=pl.BlockSpec((None, tr, C), lambda g, r, cr: (g, r, 0)))
    return pl.pallas_call(body, name=name, grid_spec=grid_spec, out_shape=jax.ShapeDtypeStruct((G, hr, C), F32),
                          compiler_params=_params())(cidx, gfull, other)


def _chip_copies(p_ref, land_ref, send_sems, recv_sems, incoming):
    x, y, c, chips = _place()
    me = 2 * x + y
    copies = []
    for j, (px, py) in enumerate(chips):
        dst = land_ref.at[2 * px + py] if incoming else land_ref.at[me]
        copies.append(pltpu.make_async_remote_copy(src_ref=p_ref.at[2 * px + py], dst_ref=dst, send_sem=send_sems.at[j],
                                                   recv_sem=recv_sems.at[j], device_id=(px, py, c), device_id_type=MESH))
    return copies


def _chip_start(part, name):
    def body(p_ref, land_ref, send_sems, recv_sems, p_thru, land_thru, token):
        for cp in _chip_copies(p_ref, land_ref, send_sems, recv_sems, False):
            cp.start()
        token[...] = jnp.zeros_like(token)

    return pl.pallas_call(
        body, name=name,
        out_shape=(pltpu.SemaphoreType.DMA((3,)), pltpu.SemaphoreType.DMA((3,)), pltpu.HBM(part.shape, F32),
                   pltpu.HBM(part.shape, F32), jax.ShapeDtypeStruct((8, LANES), F32)),
        in_specs=(HBM, HBM), out_specs=(SEM, SEM, HBM, HBM, pl.BlockSpec(memory_space=pltpu.VMEM)),
        input_output_aliases={0: 2, 1: 3},
        compiler_params=pltpu.CompilerParams(has_side_effects=EFFECT))(_hbm(part), _hbm(lax.empty(part.shape, F32)))


def _chip_wait(send_sems, recv_sems, p_thru, land_thru, after, name):
    def body(p_ref, land_ref, send_sems, recv_sems, after_ref, p_out, land_out):
        for cp in _chip_copies(p_ref, land_ref, send_sems, recv_sems, False):
            cp.wait_send()
        for cp in _chip_copies(p_ref, land_ref, send_sems, recv_sems, True):
            cp.wait_recv()

    return pl.pallas_call(
        body, name=name, out_shape=(pltpu.HBM(p_thru.shape, F32), pltpu.HBM(p_thru.shape, F32)),
        in_specs=(HBM, HBM, SEM, SEM, ANY), out_specs=(HBM, HBM), input_output_aliases={0: 0, 1: 1},
        compiler_params=pltpu.CompilerParams(has_side_effects=EFFECT))(p_thru, land_thru, send_sems, recv_sems, after)


def _chip_sum(part, slots, name):
    G, R2, C = part.shape
    tr = _tile(R2, max(8, (1 << 20) // (4 * C) // 8 * 8))
    nr = R2 // tr

    def body(i_ref, p_ref, *rest):
        o_ref = rest[-1]
        acc = None
        for u in range(G):
            val = jnp.where(i_ref[0] == u, p_ref[...], rest[u][...])
            acc = val if acc is None else acc + val
        o_ref[...] = acc

    def slot_spec(u):
        return pl.BlockSpec((None, tr, C), lambda r, i: (jnp.where(i[0] == u, (u + 1) % G, u), r, 0))

    grid_spec = pltpu.PrefetchScalarGridSpec(
        num_scalar_prefetch=1, grid=(nr,),
        in_specs=[pl.BlockSpec((None, tr, C), lambda r, i: (i[0], r, 0))] + [slot_spec(u) for u in range(G)],
        out_specs=pl.BlockSpec((tr, C), lambda r, i: (i[1] * nr + r, 0)))
    return pl.pallas_call(body, name=name, grid_spec=grid_spec, out_shape=jax.ShapeDtypeStruct((2 * R2, C), F32),
                          compiler_params=_params())(_ids(), part, slots, slots, slots, slots)


def _sum_slots(slots, name):
    G, R2, C = slots.shape
    tr = _tile(R2, max(8, (1024 * 1024) // (4 * C) // 8 * 8))

    def body(s_ref, o_ref):
        acc = s_ref[0]
        for u in range(1, G):
            acc = acc + s_ref[u]
        o_ref[...] = acc

    return pl.pallas_call(body, name=name, grid=(R2 // tr,), in_specs=[pl.BlockSpec((G, tr, C), lambda r: (0, r, 0))],
                          out_specs=pl.BlockSpec((tr, C), lambda r: (r, 0)), out_shape=jax.ShapeDtypeStruct((R2, C), F32),
                          compiler_params=_params())(slots)


def _pair_join(full, name):
    R, C = full.shape
    R2 = R // 2

    def body(f_ref, o_ref, token, send_sem, recv_sem):
        x, y, c, _ = _place()
        token[...] = jnp.zeros_like(token)
        mine = o_ref.at[pl.ds(c * R2, R2), :]
        theirs = o_ref.at[pl.ds((1 - c) * R2, R2), :]
        cp = pltpu.make_async_remote_copy(src_ref=mine, dst_ref=mine, send_sem=send_sem, recv_sem=recv_sem,
                                          device_id=(x, y, 1 - c), device_id_type=MESH)
        cp.start()
        pltpu.make_async_remote_copy(src_ref=theirs, dst_ref=theirs, send_sem=send_sem, recv_sem=recv_sem,
                                     device_id=(x, y, 1 - c), device_id_type=MESH).wait_recv()
        cp.wait_send()

    return pl.pallas_call(body, name=name, in_specs=[ANY], out_specs=[ANY, pl.BlockSpec(memory_space=pltpu.VMEM)],
                          out_shape=[jax.ShapeDtypeStruct((R, C), F32), jax.ShapeDtypeStruct((8, LANES), F32)],
                          input_output_aliases={0: 0},
                          scratch_shapes=[pltpu.SemaphoreType.DMA, pltpu.SemaphoreType.DMA])(full)


def _all_reduce_small(v, name):
    R, C = v.shape

    def gather_body(v_ref, out_ref, send_sems, recv_sems, local_sem):
        x, y, c, _ = _place()
        me = 4 * x + 2 * y + c
        mine = pltpu.make_async_copy(v_ref, out_ref.at[me], local_sem)
        mine.start()
        flips = [(fx, fy, fc) for fx in (0, 1) for fy in (0, 1) for fc in (0, 1)][1:]
        copies = []
        for j, (fx, fy, fc) in enumerate(flips):
            peer = (x ^ fx, y ^ fy, c ^ fc)
            copies.append(pltpu.make_async_remote_copy(src_ref=v_ref, dst_ref=out_ref.at[me], send_sem=send_sems.at[j],
                                                       recv_sem=recv_sems.at[j], device_id=peer, device_id_type=MESH))
        for cp in copies:
            cp.start()
        for j, (fx, fy, fc) in enumerate(flips):
            peer = (x ^ fx, y ^ fy, c ^ fc)
            pltpu.make_async_remote_copy(src_ref=v_ref, dst_ref=out_ref.at[4 * peer[0] + 2 * peer[1] + peer[2]],
                                         send_sem=send_sems.at[j], recv_sem=recv_sems.at[j], device_id=peer,
                                         device_id_type=MESH).wait_recv()
        for cp in copies:
            cp.wait_send()
        mine.wait()

    slots = pl.pallas_call(gather_body, name=name, in_specs=[ANY], out_specs=ANY,
                           out_shape=jax.ShapeDtypeStruct((8, R, C), F32),
                           scratch_shapes=[pltpu.SemaphoreType.DMA((7,)), pltpu.SemaphoreType.DMA((7,)),
                                           pltpu.SemaphoreType.DMA])(v)
    return _sum_slots(slots, f"{name}_sum")


def _adamw(w, g, m, v, name, dep=()):
    R, C = w.shape
    tr = _tile(R, max(8, (2 << 20) // (4 * C) // 8 * 8))
    bc1 = 1.0 - ADAM_B1 ** ADAM_STEP
    bc2 = 1.0 - ADAM_B2 ** ADAM_STEP

    def body(w_ref, g_ref, m_ref, v_ref, *rest):
        go_ref, d_ref, nm_ref, nv_ref = rest[len(dep):]
        gv = g_ref[...]
        go_ref[...] = gv
        nm = ADAM_B1 * m_ref[...] + (1.0 - ADAM_B1) * gv
        nv = ADAM_B2 * v_ref[...] + (1.0 - ADAM_B2) * (gv * gv)
        nm_ref[...] = nm
        nv_ref[...] = nv
        d_ref[...] = -ADAM_LR * ((nm / bc1) / (jnp.sqrt(nv / bc2) + ADAM_EPS) + ADAM_WD * w_ref[...])

    blk = pl.BlockSpec((tr, C), lambda r: (r, 0))
    out = jax.ShapeDtypeStruct((R, C), F32)
    in_specs = [blk] * 4 + [pl.BlockSpec(d.shape, lambda r: (0, 0)) for d in dep]
    return pl.pallas_call(body, name=name, grid=(R // tr,), in_specs=in_specs, out_specs=[blk] * 4, out_shape=[out] * 4,
                          compiler_params=_params())(w, g, m, v, *dep)


SC_TILES = 32
SC_LANES = 16
SC_ROWS = 8


def _sc_mesh():
    return plsc.VectorSubcoreMesh(core_axis_name="sc_core", subcore_axis_name="sc_subcore")


def _pair_add_sc(gfull, other, name):
    G, R, C = gfull.shape
    hr = R // 2
    tiles_per_shard = SC_TILES // G
    per_tile = hr // SC_ROWS // tiles_per_shard

    def body(g_hbm, o_hbm, out_hbm, gb, ob):
        c = lax.axis_index("c")
        tile = lax.axis_index("sc_subcore") * 2 + lax.axis_index("sc_core")
        t = tile // tiles_per_shard
        first = (tile % tiles_per_shard) * per_tile

        @pl.loop(0, per_tile)
        def _(k):
            rr = (first + k) * SC_ROWS
            pltpu.sync_copy(g_hbm.at[t, pl.ds(c * hr + rr, SC_ROWS), :], gb)
            pltpu.sync_copy(o_hbm.at[t, pl.ds(rr, SC_ROWS), :], ob)

            @pl.loop(0, SC_ROWS)
            def _(i):
                @pl.loop(0, C, step=SC_LANES)
                def _(j):
                    at = (i, pl.ds(j, SC_LANES))
                    gb[at] = gb[at] + ob[at]

            pltpu.sync_copy(gb, out_hbm.at[t, pl.ds(rr, SC_ROWS), :])

    buf = pltpu.VMEM((SC_ROWS, C), F32)
    return pl.kernel(body, name=name, out_type=jax.ShapeDtypeStruct((G, hr, C), F32), mesh=_sc_mesh(),
                     scratch_types=[buf, buf])(gfull, other)


def _adamw_sc(w, g, m, v, name):
    R, C = w.shape
    tasks = R // SC_ROWS
    bc1 = 1.0 - ADAM_B1 ** ADAM_STEP
    bc2 = 1.0 - ADAM_B2 ** ADAM_STEP

    def body(w_hbm, g_hbm, m_hbm, v_hbm, go_hbm, d_hbm, nm_hbm, nv_hbm, wb, gb, mb, vb):
        tile = lax.axis_index("sc_subcore") * 2 + lax.axis_index("sc_core")

        @pl.loop((tile * tasks) // SC_TILES, ((tile + 1) * tasks) // SC_TILES)
        def _(task):
            rows = pl.ds(task * SC_ROWS, SC_ROWS)
            pltpu.sync_copy(w_hbm.at[rows, :], wb)
            pltpu.sync_copy(g_hbm.at[rows, :], gb)
            pltpu.sync_copy(m_hbm.at[rows, :], mb)
            pltpu.sync_copy(v_hbm.at[rows, :], vb)

            @pl.loop(0, SC_ROWS)
            def _(i):
                @pl.loop(0, C, step=SC_LANES)
                def _(j):
                    at = (i, pl.ds(j, SC_LANES))
                    gv = gb[at]
                    nm = ADAM_B1 * mb[at] + (1.0 - ADAM_B1) * gv
                    nv = ADAM_B2 * vb[at] + (1.0 - ADAM_B2) * (gv * gv)
                    mb[at] = nm
                    vb[at] = nv
                    wb[at] = -ADAM_LR * ((nm / bc1) / (jnp.sqrt(nv / bc2) + ADAM_EPS) + ADAM_WD * wb[at])

            pltpu.sync_copy(gb, go_hbm.at[rows, :])
            pltpu.sync_copy(wb, d_hbm.at[rows, :])
            pltpu.sync_copy(mb, nm_hbm.at[rows, :])
            pltpu.sync_copy(vb, nv_hbm.at[rows, :])

    out = jax.ShapeDtypeStruct((R, C), F32)
    buf = pltpu.VMEM((SC_ROWS, C), F32)
    return pl.kernel(body, name=name, out_type=(out, out, out, out), mesh=_sc_mesh(),
                     scratch_types=[buf, buf, buf, buf])(w, g, m, v)


WEIGHTS = ['ffn1_norm', 'ffn1_w_gate', 'ffn1_w_up', 'ffn1_w_down', 'mix_norm', 'w_in', 'pool_w', 'pool_scale', 'gla_w_a2',
           'gla_b_a', 'gla_head_norm', 'w_out', 'xattn_norm', 'mem_norm', 'xattn_w_q', 'xattn_w_kv', 'xattn_w_o', 'ffn2_norm',
           'ffn2_w_gate', 'ffn2_w_up', 'ffn2_w_down', 'final_norm']
SHARDED = ['ffn1_w_gate', 'ffn1_w_up', 'ffn1_w_down', 'w_in', 'pool_w', 'gla_w_a2', 'w_out', 'xattn_w_q', 'xattn_w_kv',
           'xattn_w_o', 'ffn2_w_gate', 'ffn2_w_up', 'ffn2_w_down']
REPLICATED = [n for n in WEIGHTS if n not in SHARDED]
ON_SPARSECORE = ['ffn2_w_gate', 'ffn2_w_up', 'w_out', 'xattn_w_q', 'xattn_w_kv', 'xattn_w_o', 'ffn2_w_down']
SLOW_ON_SPARSECORE = 'ffn2_w_down'
PAIR_SUM_ON_SPARSECORE = ['ffn2_w_down', 'ffn2_w_gate', 'ffn2_w_up', 'xattn_w_o', 'xattn_w_kv', 'pool_w', 'ffn1_w_down']
SMALL_COLS = 512


def _as2d(a):
    return a.reshape(-1, a.shape[-1])


def _finish_weight(name, gathered, wl):
    G, R, C = gathered.shape
    rank = wl["gla_w_a2"].shape[1]
    if name in ("w_out", "xattn_w_q", "xattn_w_o"):
        return gathered.reshape(G * R, C)
    if name == "w_in":
        w_in = jnp.transpose(gathered, (1, 0, 2)).reshape(R, G * C)
        main = G * C - rank
        return jnp.concatenate([w_in[:, :main], jnp.pad(w_in[:, main:], ((0, 0), (0, LANES - rank)))], axis=1)
    if name == "pool_w":
        NG, CJ, _ = wl[name].shape[1:]
        return jnp.transpose(gathered.reshape(G, NG, CJ, C), (1, 0, 2, 3)).reshape(NG, G * CJ, C)
    if name == "gla_w_a2":
        a2 = jnp.transpose(gathered, (1, 0, 2)).reshape(rank, G * C)
        return jnp.pad(a2, ((0, LANES - rank), (0, 0))).astype(BF16)
    return gathered


def _start_gathers(wl):
    started = {}
    token = None
    for n in SHARDED:
        whole = n not in ("ffn1_w_gate", "ffn1_w_up")
        buf = _cast_to_slot(_as2d(wl[n]), BF16, f"slot_{n}", dep=token)
        send_sems, recv_sems, thru, token = _gather_start(buf, f"gather_start_{n}", whole)
        started[n] = (send_sems, recv_sems, thru, whole)
    cache = {}

    def weight(n, after=None):
        if n not in cache:
            *handles, whole = started[n]
            buf = _gather_wait(*handles, after, f"gather_wait_{n}", whole)
            if not whole:
                buf = _gather_forward(buf, f"gather_forward_{n}")
            cache[n] = _finish_weight(n, buf, wl)
        return cache[n]

    return weight, token


def _shard_major(name, gfull, wl):
    R, C = _as2d(wl[name]).shape
    if name in ("ffn1_w_gate", "ffn1_w_up", "ffn2_w_gate", "ffn2_w_up", "xattn_w_kv"):
        return gfull
    if name in ("ffn1_w_down", "ffn2_w_down", "w_out", "xattn_w_q", "xattn_w_o"):
        return gfull.reshape(N_SHARDS, R, C)
    if name == "w_in":
        return jnp.transpose(gfull[:, :N_SHARDS * C].reshape(R, N_SHARDS, C), (1, 0, 2))
    if name == "pool_w":
        NG, CJ, _ = wl[name].shape[1:]
        return jnp.transpose(gfull.reshape(NG, N_SHARDS, CJ, C), (1, 0, 2, 3)).reshape(N_SHARDS, R, C)
    assert name == "gla_w_a2"
    return jnp.transpose(gfull[:R].reshape(R, N_SHARDS, C), (1, 0, 2))


def kernel(x, mem, ffn1_norm, ffn1_w_gate, ffn1_w_up, ffn1_w_down, mix_norm, w_in, pool_w, pool_scale, gla_w_a2, gla_b_a, gla_head_norm, w_out, xattn_norm, mem_norm, xattn_w_q, xattn_w_kv, xattn_w_o, ffn2_norm, ffn2_w_gate, ffn2_w_up, ffn2_w_down, final_norm, loss_target, m_ffn1_norm, m_ffn1_w_gate, m_ffn1_w_up, m_ffn1_w_down, m_mix_norm, m_w_in, m_pool_w, m_pool_scale, m_gla_w_a2, m_gla_b_a, m_gla_head_norm, m_w_out, m_xattn_norm, m_mem_norm, m_xattn_w_q, m_xattn_w_kv, m_xattn_w_o, m_ffn2_norm, m_ffn2_w_gate, m_ffn2_w_up, m_ffn2_w_down, m_final_norm, v_ffn1_norm, v_ffn1_w_gate, v_ffn1_w_up, v_ffn1_w_down, v_mix_norm, v_w_in, v_pool_w, v_pool_scale, v_gla_w_a2, v_gla_b_a, v_gla_head_norm, v_w_out, v_xattn_norm, v_mem_norm, v_xattn_w_q, v_xattn_w_kv, v_xattn_w_o, v_ffn2_norm, v_ffn2_w_gate, v_ffn2_w_up, v_ffn2_w_down, v_final_norm):
    given = dict(locals())
    wl = {n: given[n] for n in WEIGHTS}
    ml = {n: given["m_" + n] for n in WEIGHTS}
    vl = {n: given["v_" + n] for n in WEIGHTS}

    vec = {n: wl[n].reshape(1, -1) for n in REPLICATED}
    weight, dep0 = _start_gathers(wl)
    in_flight = {}

    pair_flight = {}

    def emit_begin(n, gfull):
        *pair_flight[n], token = _pair_start(_shard_major(n, gfull, wl), f"{n}_pair_start")
        return token

    summing = {}

    def emit_finish(n, after):
        gsm, other = _pair_wait(*pair_flight.pop(n), after, f"{n}_pair_wait")
        if n in PAIR_SUM_ON_SPARSECORE:
            summing[n] = _pair_add_sc(gsm, other, f"{n}_pair_add_sc")
            return None
        *in_flight[n], token = _chip_start(_pair_add(gsm, other, f"{n}_pair_add"), f"{n}_chip_start")
        return token

    def emit_send(n):
        *in_flight[n], token = _chip_start(summing.pop(n), f"{n}_chip_start")
        return token

    grads = {}
    updates = {}

    def reduce_done(n, after):
        part, slots = _chip_wait(*in_flight.pop(n), after, f"{n}_chip_wait")
        grads[n], token = _pair_join(_chip_sum(part, slots, f"{n}_chip_sum"), f"{n}_pair_join")
        return token

    def early_update(after):
        tokens = [reduce_done(n, after) for n in ON_SPARSECORE]
        for n in ON_SPARSECORE:
            g2 = grads[n]
            updates[n] = _adamw_sc(wl[n].reshape(g2.shape), g2, ml[n].reshape(g2.shape), vl[n].reshape(g2.shape),
                                   f"adamw_sc_{n}")
        return tokens

    loss, dx0, g = _local_step(x[0], mem[0], loss_target[0], vec, weight,
                               (emit_begin, emit_finish, emit_send, early_update), dep0)
    assert not summing

    for n in list(in_flight):
        reduce_done(n, dx0)
    widths = [wl[n].size for n in REPLICATED]
    total = sum(widths)
    rows = -(-total // SMALL_COLS)
    rows = -(-rows // 8) * 8
    packed = jnp.concatenate([g[n].reshape(-1) for n in REPLICATED] + [jnp.zeros((rows * SMALL_COLS - total,), F32)])
    summed = _all_reduce_small(packed.reshape(rows, SMALL_COLS), "small_all_reduce").reshape(-1)
    off = 0
    for n, width in zip(REPLICATED, widths):
        grads[n] = summed[off:off + width].reshape(1, width)
        off += width

    out_g, out_d, out_m, out_v = [], [], [], []
    for n in WEIGHTS:
        shape = wl[n].shape
        g2 = grads[n]
        if n in updates:
            go, d, nm, nv = updates[n]
        else:
            dep = ()
            if n == "w_in":
                dep = tuple(updates[k][1][:8, :LANES] for k in updates if k != SLOW_ON_SPARSECORE)
            if n == "ffn1_w_gate":
                dep = (updates[SLOW_ON_SPARSECORE][1][:8, :LANES],)
            go, d, nm, nv = _adamw(wl[n].reshape(g2.shape), g2, ml[n].reshape(g2.shape), vl[n].reshape(g2.shape),
                                   f"adamw_{n}", dep)
        out_g.append(go.reshape(shape))
        out_d.append(d.reshape(shape))
        out_m.append(nm.reshape(shape))
        out_v.append(nv.reshape(shape))
    return (loss, dx0.reshape(x.shape), *out_g, *out_d, *out_m, *out_v)
```

```python
import functools

import jax
import jax.numpy as jnp
from jax import lax
from jax.experimental import pallas as pl
from jax.experimental.pallas import tpu as pltpu
from jax.experimental.pallas import tpu_sc as plsc

F32 = jnp.float32
BF16 = jnp.bfloat16
MESH = pl.DeviceIdType.MESH

RMS_EPS = 1e-6
CHUNK = 64
POOL_WINDOWS = (2, 4, 8, 16)
POOL_HALO = 16
N_HEADS = 4
GATE_TEMP = 16.0
ADAM_LR, ADAM_B1, ADAM_B2, ADAM_EPS, ADAM_WD, ADAM_STEP = 0.001, 0.9, 0.999, 1e-08, 0.01, 10
N_SHARDS = 4
LANES = 128
MXU_COLS = 256
TOKENS_PER_STEP = 2048
VMEM_LIMIT = 58 * 1024 * 1024

ANY = pl.BlockSpec(memory_space=pl.ANY)
HBM = pl.BlockSpec(memory_space=pltpu.HBM)
SEM = pl.BlockSpec(memory_space=pltpu.SEMAPHORE)
EFFECT = pltpu.SideEffectType.DATAFLOW_SIDE_EFFECTING


def _params(**kw):
    return pltpu.CompilerParams(vmem_limit_bytes=VMEM_LIMIT, **kw)


def _tile(n, want):
    for unit in (LANES, 8):
        t = (min(want, n) // unit) * unit
        while t >= unit:
            if n % t == 0:
                return t
            t -= unit
    return n


def _dot(a, b, dims):
    return lax.dot_general(a, b, (dims, ((), ())), preferred_element_type=F32)


def _nn(a, b):
    return _dot(a, b, ((1,), (0,)))


def _nt(a, b):
    return _dot(a, b, ((1,), (1,)))


def _tn(a, b):
    return _dot(a, b, ((0,), (0,)))


def _sigmoid(x):
    return 1.0 / (1.0 + jnp.exp(-x))


def _matmul(a, b, *, mode, name, out_dtype, tm=512, tn=2048, tk=2048, res=None, scale=1.0, b_groups=False, out_groups=0,
            dep=(), cols_outer=False):
    if mode == "tn":
        K, M = a.shape
    else:
        M, K = a.shape
    if mode == "nn":
        if b_groups:
            G, _, Nj = b.shape
            N = G * Nj
        else:
            N = b.shape[1]
    elif mode == "nt":
        if b_groups:
            G, N, Kj = b.shape
            assert G * Kj == K
        else:
            N = b.shape[0]
    else:
        N = b.shape[1]
    tm = _tile(M, tm)
    if mode == "nn" and b_groups:
        tn = _tile(Nj, tn)
    elif out_groups:
        tn = _tile(N // out_groups, tn)
    else:
        tn = _tile(N, tn)
    if mode == "nt" and b_groups:
        tk = _tile(Kj, tk)
    else:
        tk = _tile(K, tk)
    nk = K // tk
    grid = (M // tm, N // tn, nk)

    if mode == "tn":
        a_spec = pl.BlockSpec((tk, tm), lambda i, j, k: (k, i))
        b_spec = pl.BlockSpec((tk, tn), lambda i, j, k: (k, j))
        dims = ((0,), (0,))
    elif mode == "nn":
        a_spec = pl.BlockSpec((tm, tk), lambda i, j, k: (i, k))
        if b_groups:
            npj = Nj // tn
            b_spec = pl.BlockSpec((None, tk, tn), lambda i, j, k: (j // npj, k, j % npj))
        else:
            b_spec = pl.BlockSpec((tk, tn), lambda i, j, k: (k, j))
        dims = ((1,), (0,))
    else:
        a_spec = pl.BlockSpec((tm, tk), lambda i, j, k: (i, k))
        if b_groups:
            kpj = Kj // tk
            b_spec = pl.BlockSpec((None, tn, tk), lambda i, j, k: (k // kpj, j, k % kpj))
        else:
            b_spec = pl.BlockSpec((tn, tk), lambda i, j, k: (j, k))
        dims = ((1,), (1,))
    if out_groups:
        npj = (N // out_groups) // tn
        o_spec = pl.BlockSpec((None, tm, tn), lambda i, j, k: (j // npj, i, j % npj))
        out_shape = jax.ShapeDtypeStruct((out_groups, M, N // out_groups), out_dtype)
    else:
        o_spec = pl.BlockSpec((tm, tn), lambda i, j, k: (i, j))
        out_shape = jax.ShapeDtypeStruct((M, N), out_dtype)
    in_specs = [a_spec, b_spec]
    operands = [a, b]
    if res is not None:
        in_specs.append(pl.BlockSpec((tm, tn), lambda i, j, k: (i, j)))
        operands.append(res)
    has_res = res is not None
    n_dep = len(dep)
    for d in dep:
        in_specs.append(pl.BlockSpec(d.shape, lambda i, j, k: (0, 0)))
        operands.append(d)
    if cols_outer:
        def swapped(spec):
            return pl.BlockSpec(spec.block_shape, lambda j, i, k, f=spec.index_map: f(i, j, k))
        in_specs = [swapped(s) for s in in_specs]
        o_spec = swapped(o_spec)
        grid = (grid[1], grid[0], grid[2])

    def body(*refs):
        if has_res:
            a_ref, b_ref, r_ref = refs[:3]
        else:
            a_ref, b_ref = refs[:2]
            r_ref = None
        o_ref = refs[2 + has_res + n_dep]

        def finish(acc):
            if scale != 1.0:
                acc = acc * scale
            if r_ref is not None:
                acc = r_ref[...] + acc
            o_ref[...] = acc.astype(o_ref.dtype)

        part = _dot(a_ref[...], b_ref[...], dims)
        if nk == 1:
            finish(part)
        else:
            acc_ref = o_ref if in_place else refs[-1]
            k = pl.program_id(2)

            @pl.when(k == 0)
            def _():
                acc_ref[...] = part

            @pl.when(k > 0)
            def _():
                acc_ref[...] += part

            if not in_place:
                @pl.when(k == nk - 1)
                def _():
                    finish(acc_ref[...])

    in_place = out_dtype == F32 and res is None and scale == 1.0
    scratch = [] if nk == 1 or in_place else [pltpu.VMEM((tm, tn), F32)]
    return pl.pallas_call(body, name=name, grid=grid, in_specs=in_specs, out_specs=o_spec, out_shape=out_shape,
                          scratch_shapes=scratch, compiler_params=_params())(*operands)


def _rms_fwd(x, gain, name, tm=512, dep=None):
    S, D = x.shape
    tm = _tile(S, tm)

    def body(x_ref, g_ref, *rest):
        o_ref = rest[-1]
        xv = x_ref[...]
        r = lax.rsqrt(jnp.mean(xv * xv, axis=-1, keepdims=True) + RMS_EPS)
        o_ref[...] = (xv * r * g_ref[...]).astype(o_ref.dtype)

    in_specs = [pl.BlockSpec((tm, D), lambda i: (i, 0)), pl.BlockSpec((1, D), lambda i: (0, 0))]
    operands = [x, gain]
    if dep is not None:
        in_specs.append(pl.BlockSpec(dep.shape, lambda i: (0, 0)))
        operands.append(dep)
    return pl.pallas_call(body, name=name, grid=(S // tm,), in_specs=in_specs,
                          out_specs=pl.BlockSpec((tm, D), lambda i: (i, 0)),
                          out_shape=jax.ShapeDtypeStruct((S, D), BF16), compiler_params=_params())(*operands)


def _rms_bwd(x, gain, dh, dres, name, lowp=None, tm=512):
    half = lowp is not None
    S, D = x.shape
    tm = _tile(S, tm)
    has_res = dres is not None

    def body(*refs):
        if has_res:
            x_ref, g_ref, dh_ref, dr_ref = refs[:4]
            outs = refs[4:]
        else:
            x_ref, g_ref, dh_ref = refs[:3]
            dr_ref = None
            outs = refs[3:]
        dx_ref, dg_ref = outs[0], outs[-1]
        xv = x_ref[...]
        dhv = dh_ref[...].astype(F32)
        r = lax.rsqrt(jnp.mean(xv * xv, axis=-1, keepdims=True) + RMS_EPS)
        gy = dhv * g_ref[...]
        dx = r * gy - xv * (r * r * r) * jnp.mean(gy * xv, axis=-1, keepdims=True)
        if dr_ref is not None:
            dx = dx + dr_ref[...]
        dx_ref[...] = dx
        if half:
            outs[1][...] = (dx if lowp == 1.0 else lowp * dx).astype(BF16)
        part = jnp.sum(dhv * xv * r, axis=0, keepdims=True)

        @pl.when(pl.program_id(0) == 0)
        def _():
            dg_ref[...] = part

        @pl.when(pl.program_id(0) > 0)
        def _():
            dg_ref[...] += part

    row = pl.BlockSpec((tm, D), lambda i: (i, 0))
    vec = pl.BlockSpec((1, D), lambda i: (0, 0))
    in_specs = [row, vec, row] + ([row] if has_res else [])
    operands = [x, gain, dh] + ([dres] if has_res else [])
    out_specs = [row] + ([row] if half else []) + [vec]
    out_shape = [jax.ShapeDtypeStruct((S, D), F32)] + ([jax.ShapeDtypeStruct((S, D), BF16)] if half else []) + [
        jax.ShapeDtypeStruct((1, D), F32)]
    return pl.pallas_call(body, name=name, grid=(S // tm,), in_specs=in_specs, out_specs=out_specs, out_shape=out_shape,
                          compiler_params=_params())(*operands)


def _loss_head(x, gain, target, name, tm=512):
    S, D = x.shape
    tm = _tile(S, tm)

    def body(x_ref, g_ref, t_ref, sq_ref, dx_ref, dxh_ref, dg_ref):
        xv = x_ref[...]
        r = lax.rsqrt(jnp.mean(xv * xv, axis=-1, keepdims=True) + RMS_EPS)
        xn = xv * r
        err = xn * g_ref[...] - t_ref[...]
        dout = err * (1.0 / D)
        gy = dout * g_ref[...]
        dx = r * gy - xv * (r * r * r) * jnp.mean(gy * xv, axis=-1, keepdims=True)
        dx_ref[...] = dx
        dxh_ref[...] = (0.5 * dx).astype(BF16)
        sq = jnp.sum(err * err, axis=0, keepdims=True)
        dg = jnp.sum(dout * xn, axis=0, keepdims=True)

        @pl.when(pl.program_id(0) == 0)
        def _():
            sq_ref[...] = sq
            dg_ref[...] = dg

        @pl.when(pl.program_id(0) > 0)
        def _():
            sq_ref[...] += sq
            dg_ref[...] += dg

    row = pl.BlockSpec((tm, D), lambda i: (i, 0))
    vec = pl.BlockSpec((1, D), lambda i: (0, 0))
    return pl.pallas_call(body, name=name, grid=(S // tm,), in_specs=[row, vec, row], out_specs=[vec, row, row, vec],
                          out_shape=[jax.ShapeDtypeStruct((1, D), F32), jax.ShapeDtypeStruct((S, D), F32),
                                     jax.ShapeDtypeStruct((S, D), BF16), jax.ShapeDtypeStruct((1, D), F32)],
                          compiler_params=_params())(x, gain, target)


def _cast(x, dtype, name, scale=1.0, tm=256):
    S, D = x.shape
    tm = _tile(S, tm)

    def body(x_ref, o_ref):
        o_ref[...] = (x_ref[...] * scale).astype(o_ref.dtype)

    row = pl.BlockSpec((tm, D), lambda i: (i, 0))
    return pl.pallas_call(body, name=name, grid=(S // tm,), in_specs=[row], out_specs=row,
                          out_shape=jax.ShapeDtypeStruct((S, D), dtype), compiler_params=_params())(x)


def _ffn_up(h, wg, wu, name, tm=512):
    S, D = h.shape
    G, _, Fj = wg.shape
    tm = _tile(S, tm)

    def body(h_ref, wg_ref, wu_ref, ga_ref, gb_ref, hid_ref):
        hv = h_ref[...]
        a = _nn(hv, wg_ref[...])
        b = _nn(hv, wu_ref[...])
        s = _sigmoid(a)
        silu = a * s
        ga_ref[...] = (b * (s * (1.0 + a * (1.0 - s)))).astype(BF16)
        gb_ref[...] = silu.astype(BF16)
        hid_ref[...] = (silu * b).astype(BF16)

    w_spec = pl.BlockSpec((None, D, Fj), lambda g, i: (g, 0, 0))
    o_spec = pl.BlockSpec((tm, Fj), lambda g, i: (i, g))
    out = jax.ShapeDtypeStruct((S, G * Fj), BF16)
    return pl.pallas_call(body, name=name, grid=(G, S // tm),
                          in_specs=[pl.BlockSpec((tm, D), lambda g, i: (i, 0)), w_spec, w_spec],
                          out_specs=[o_spec, o_spec, o_spec], out_shape=[out, out, out], compiler_params=_params())(h, wg, wu)


def _ffn_dact(dxh, wd, ga, gb, name, tm=512):
    S, D = dxh.shape
    G, Fj, _ = wd.shape
    tm = _tile(S, tm)

    def body(dx_ref, wd_ref, ga_ref, gb_ref, da_ref, db_ref):
        dhid = _nt(dx_ref[...], wd_ref[...])
        da_ref[...] = (dhid * ga_ref[...].astype(F32)).astype(BF16)
        db_ref[...] = (dhid * gb_ref[...].astype(F32)).astype(BF16)

    blk = pl.BlockSpec((tm, Fj), lambda g, i: (i, g))
    out = jax.ShapeDtypeStruct((S, G * Fj), BF16)
    return pl.pallas_call(body, name=name, grid=(G, S // tm),
                          in_specs=[pl.BlockSpec((tm, D), lambda g, i: (i, 0)),
                                    pl.BlockSpec((None, Fj, D), lambda g, i: (g, 0, 0)), blk, blk],
                          out_specs=[blk, blk], out_shape=[out, out], compiler_params=_params())(dxh, wd, ga, gb)


def _ffn_dh(da, db, wg, wu, name, dep=(), tm=512):
    S = da.shape[0]
    G, D, Fj = wg.shape
    tm = _tile(S, tm)

    def body(da_ref, db_ref, wg_ref, wu_ref, *rest):
        o_ref = rest[-1]
        part = _nt(da_ref[...], wg_ref[...]) + _nt(db_ref[...], wu_ref[...])

        @pl.when(pl.program_id(1) == 0)
        def _():
            o_ref[...] = part

        @pl.when(pl.program_id(1) > 0)
        def _():
            o_ref[...] += part

    act = pl.BlockSpec((tm, Fj), lambda i, g: (i, g))
    w_spec = pl.BlockSpec((None, D, Fj), lambda i, g: (g, 0, 0))
    in_specs = [act, act, w_spec, w_spec] + [pl.BlockSpec(d.shape, lambda i, g: (0, 0)) for d in dep]
    return pl.pallas_call(body, name=name, grid=(S // tm, G), in_specs=in_specs,
                          out_specs=pl.BlockSpec((tm, D), lambda i, g: (i, 0)),
                          out_shape=jax.ShapeDtypeStruct((S, D), F32), compiler_params=_params())(da, db, wg, wu, *dep)


def _pool_fwd(proj, pool_w, pool_scale, name, tm=512):
    S = proj.shape[0]
    NG, C, _ = pool_w.shape
    DP = NG * C
    tm = _tile(S, tm)
    hb = tm // POOL_HALO
    n_ext = tm + POOL_HALO

    def body(u_ref, halo_ref, w_ref, sc_ref, y_ref, d_ref):
        i = pl.program_id(0)
        t = lax.broadcasted_iota(jnp.int32, (tm, 1), 0) + i * tm
        for g, win in enumerate(POOL_WINDOWS):
            cols = slice(g * C, (g + 1) * C)
            ug = u_ref[:, cols]
            halo = jnp.where(i > 0, halo_ref[:, cols], 0.0)
            acc = jnp.concatenate([halo, ug], axis=0)
            step = 1
            while step < win:
                acc = acc + pltpu.roll(acc, step, 0)
                step *= 2
            count = jnp.minimum(t + 1, win).astype(F32)
            d = (acc[POOL_HALO:, :] / count - ug).astype(BF16)
            d_ref[:, cols] = d
            y_ref[:, cols] = (_nn(d, w_ref[g]) * sc_ref[:, cols]).astype(BF16)

    del n_ext
    return pl.pallas_call(
        body, name=name, grid=(S // tm,),
        in_specs=[pl.BlockSpec((tm, DP), lambda i: (i, 0)),
                  pl.BlockSpec((POOL_HALO, DP), lambda i: (jnp.maximum(i * hb - 1, 0), 0)),
                  pl.BlockSpec((NG, C, C), lambda i: (0, 0, 0)), pl.BlockSpec((1, DP), lambda i: (0, 0))],
        out_specs=[pl.BlockSpec((tm, DP), lambda i: (i, 0)), pl.BlockSpec((tm, DP), lambda i: (i, 0))],
        out_shape=[jax.ShapeDtypeStruct((S, DP), BF16), jax.ShapeDtypeStruct((S, DP), BF16)],
        compiler_params=_params())(proj, proj, pool_w, pool_scale)


def _pool_bwd(dymix, d, pool_w, pool_scale, name, tm=512):
    S = dymix.shape[0]
    NG, C, _ = pool_w.shape
    DP = NG * C
    tm = _tile(S, tm)
    hb = tm // POOL_HALO
    nb = S // tm
    n_ext = tm + POOL_HALO
    last_halo = S // POOL_HALO - 1

    def body(dy_ref, halo_ref, d_ref, w_ref, sc_ref, du_ref, dw_ref, dsc_ref):
        i = pl.program_id(0)
        t = lax.broadcasted_iota(jnp.int32, (n_ext, 1), 0) + i * tm
        for g, win in enumerate(POOL_WINDOWS):
            cols = slice(g * C, (g + 1) * C)
            dy = dy_ref[:, cols]
            halo = jnp.where(i < nb - 1, halo_ref[:, cols], 0.0)
            sc = sc_ref[:, cols]
            dv = d_ref[:, cols]
            e_ext = (jnp.concatenate([dy, halo], axis=0) * sc).astype(BF16)
            dd = _nt(e_ext, w_ref[g])
            count = jnp.minimum(t + 1, win).astype(F32)
            acc = dd / count
            step = 1
            while step < win:
                acc = acc + pltpu.roll(acc, n_ext - step, 0)
                step *= 2
            du_ref[:, cols] = (acc[:tm, :] - dd[:tm, :]).astype(BF16)
            dw = _tn(dv, e_ext[:tm, :])
            dsc = jnp.sum(dy * _nn(dv, w_ref[g]), axis=0, keepdims=True)

            @pl.when(i == 0)
            def _():
                dw_ref[g] = dw
                dsc_ref[:, cols] = dsc

            @pl.when(i > 0)
            def _():
                dw_ref[g] += dw
                dsc_ref[:, cols] += dsc

    return pl.pallas_call(
        body, name=name, grid=(nb,),
        in_specs=[pl.BlockSpec((tm, DP), lambda i: (i, 0)),
                  pl.BlockSpec((POOL_HALO, DP), lambda i: (jnp.minimum((i + 1) * hb, last_halo), 0)),
                  pl.BlockSpec((tm, DP), lambda i: (i, 0)),
                  pl.BlockSpec((NG, C, C), lambda i: (0, 0, 0)), pl.BlockSpec((1, DP), lambda i: (0, 0))],
        out_specs=[pl.BlockSpec((tm, DP), lambda i: (i, 0)), pl.BlockSpec((NG, C, C), lambda i: (0, 0, 0)),
                   pl.BlockSpec((1, DP), lambda i: (0, 0))],
        out_shape=[jax.ShapeDtypeStruct((S, DP), BF16), jax.ShapeDtypeStruct((NG, C, C), F32),
                   jax.ShapeDtypeStruct((1, DP), F32)],
        compiler_params=_params())(dymix, dymix, d, pool_w, pool_scale)


def _chunk_scan(v, rows, reverse):
    n = v.shape[0]
    step = 1
    while step < CHUNK:
        if reverse:
            v = v + jnp.where(rows < CHUNK - step, pltpu.roll(v, n - step, 0), 0.0)
        else:
            v = v + jnp.where(rows >= step, pltpu.roll(v, step, 0), 0.0)
        step *= 2
    return v


def _log_decay(alr, w_a2, b_a):
    z = _nn(alr.astype(BF16), w_a2) + b_a
    la = (jnp.minimum(z, 0.0) - jnp.log(1.0 + jnp.exp(-jnp.abs(z)))) * (1.0 / GATE_TEMP)
    return z, la


def _gla_specs(DP, DKT, DV, tb, bmap):
    return [pl.BlockSpec((tb, DKT), lambda i: (bmap(i), DP // DKT)),
            pl.BlockSpec((tb, DKT), lambda i: (bmap(i), DP // DKT + 1)),
            pl.BlockSpec((tb, DV), lambda i: (bmap(i), (DP + 2 * DKT) // DV)),
            pl.BlockSpec((tb, DV), lambda i: (bmap(i), (DP + 2 * DKT) // DV + 1)),
            pl.BlockSpec((tb, LANES), lambda i: (bmap(i), (DP + 2 * DKT + 2 * DV) // LANES))]


def _gla_fwd(proj, y_pool, w_a2, b_a, head_norm, name, tb=512):
    S = proj.shape[0]
    DP = y_pool.shape[1]
    DKT = b_a.shape[1]
    DV = head_norm.shape[1]
    dk, dv = DKT // N_HEADS, DV // N_HEADS
    tb = _tile(S, tb)
    ncb = tb // CHUNK
    qscale = dk ** -0.5

    def body(q_ref, k_ref, v_ref, g_ref, alr_ref, yp_ref, wa_ref, ba_ref, hn_ref, y_ref, st_out_ref, st_ref, kdec_ref,
             gam_ref):
        @pl.when(pl.program_id(0) == 0)
        def _():
            st_ref[...] = jnp.zeros_like(st_ref)

        y_ref[:, :DP] = yp_ref[...]

        rows = lax.broadcasted_iota(jnp.int32, (tb, 1), 0) % CHUNK
        _, la = _log_decay(alr_ref[...], wa_ref[...], ba_ref[...])
        tail = _chunk_scan(la, rows, True)
        kdec_ref[...] = k_ref[...] * jnp.exp(tail - la)
        gam_ref[...] = jnp.exp(tail)

        def chunk(c, carry):
            r0 = pl.multiple_of(c * CHUNK, CHUNK)
            rs = pl.ds(r0, CHUNK)
            gam = gam_ref[pl.ds(r0, 1), :]
            heads = range(N_HEADS)
            kcs = [slice(h * dk, (h + 1) * dk) for h in heads]
            vcs = [slice(h * dv, (h + 1) * dv) for h in heads]
            upd = [_tn(v_ref[rs, vcs[h]].astype(BF16), kdec_ref[rs, kcs[h]].astype(BF16)) for h in heads]
            st = [st_ref[h] * gam[:, kcs[h]] + upd[h] for h in heads]
            o = [_nt((q_ref[rs, kcs[h]] * qscale).astype(BF16), st[h].astype(BF16)) for h in heads]
            for h in heads:
                st_ref[h] = st[h]
                st_out_ref[c, h] = st[h]
                r = lax.rsqrt(jnp.mean(o[h] * o[h], axis=-1, keepdims=True) + RMS_EPS)
                gv = g_ref[rs, vcs[h]]
                y_ref[rs, DP + h * dv:DP + (h + 1) * dv] = (o[h] * r * hn_ref[:, vcs[h]] * (gv * _sigmoid(gv))).astype(BF16)
            return carry

        lax.fori_loop(0, ncb, chunk, 0, unroll=4)

    full = lambda shape: pl.BlockSpec(shape, lambda i: (0,) * len(shape))
    return pl.pallas_call(
        body, name=name, grid=(S // tb,),
        in_specs=_gla_specs(DP, DKT, DV, tb, lambda i: i) + [pl.BlockSpec((tb, DP), lambda i: (i, 0)),
                                                            full((LANES, DKT)), full((1, DKT)), full((1, DV))],
        out_specs=[pl.BlockSpec((tb, DP + DV), lambda i: (i, 0)),
                   pl.BlockSpec((ncb, N_HEADS, dv, dk), lambda i: (i, 0, 0, 0))],
        out_shape=[jax.ShapeDtypeStruct((S, DP + DV), BF16), jax.ShapeDtypeStruct((S // CHUNK, N_HEADS, dv, dk), F32)],
        scratch_shapes=[pltpu.VMEM((N_HEADS, dv, dk), F32), pltpu.VMEM((tb, DKT), F32), pltpu.VMEM((tb, DKT), F32)],
        compiler_params=_params())(proj, proj, proj, proj, proj, y_pool, w_a2, b_a, head_norm)


def _gla_bwd(proj, states, dymix, du, w_a2, b_a, head_norm, name, tb=512):
    S = proj.shape[0]
    DP = du.shape[1]
    DKT = b_a.shape[1]
    DV = head_norm.shape[1]
    dk, dv = DKT // N_HEADS, DV // N_HEADS
    tb = _tile(S, tb)
    ncb = tb // CHUNK
    nb = S // tb
    qscale = dk ** -0.5
    rev = lambda i: nb - 1 - i

    q0, k0, v0, g0, a0 = DP, DP + DKT, DP + 2 * DKT, DP + 2 * DKT + DV, DP + 2 * DKT + 2 * DV

    def body(q_ref, k_ref, v_ref, g_ref, alr_ref, st_blk_ref, st_prev_ref, dy_ref, du_ref, wa_ref, ba_ref, hn_ref,
             dp_ref, dwa_ref, dba_ref, dhn_ref,
             dst_ref, kdec_ref, dec_ref, gam_ref, e_ref, dla_ref, dhn_acc_ref):
        i = pl.program_id(0)
        blk = rev(i)
        dp_ref[:, :DP] = du_ref[...]

        @pl.when(i == 0)
        def _():
            dst_ref[...] = jnp.zeros_like(dst_ref)

        dhn_acc_ref[...] = jnp.zeros_like(dhn_acc_ref)
        rows = lax.broadcasted_iota(jnp.int32, (tb, 1), 0) % CHUNK
        z, la = _log_decay(alr_ref[...], wa_ref[...], ba_ref[...])
        tail = _chunk_scan(la, rows, True)
        dec_ref[...] = jnp.exp(tail - la)
        kdec_ref[...] = k_ref[...] * dec_ref[...]
        gam_ref[...] = jnp.exp(tail)

        def chunk(cc, carry):
            c = ncb - 1 - cc
            r0 = pl.multiple_of(c * CHUNK, CHUNK)
            rs = pl.ds(r0, CHUNK)
            gam = gam_ref[pl.ds(r0, 1), :]
            first = jnp.logical_and(blk == 0, c == 0)
            heads = range(N_HEADS)
            kcs = [slice(h * dk, (h + 1) * dk) for h in heads]
            vcs = [slice(h * dv, (h + 1) * dv) for h in heads]
            qs = [(q_ref[rs, kcs[h]] * qscale).astype(BF16) for h in heads]
            stb = [st_blk_ref[c, h].astype(BF16) for h in heads]
            o = [_nt(qs[h], stb[h]) for h in heads]
            do = []
            for h in heads:
                oh = o[h]
                r = lax.rsqrt(jnp.mean(oh * oh, axis=-1, keepdims=True) + RMS_EPS)
                gv = g_ref[rs, vcs[h]]
                sg = _sigmoid(gv)
                dy = dy_ref[rs, vcs[h]]
                hn = hn_ref[:, vcs[h]]
                on = oh * r
                dp_ref[rs, g0 + h * dv:g0 + (h + 1) * dv] = (dy * on * hn * (sg * (1.0 + gv * (1.0 - sg)))).astype(BF16)
                don = dy * (gv * sg)
                dhn_acc_ref[:, vcs[h]] += jnp.sum(don * on, axis=0, keepdims=True)
                dn = don * hn
                do.append((r * dn - oh * (r * r * r) * jnp.mean(dn * oh, axis=-1, keepdims=True)).astype(BF16))
            dqs = [_nn(do[h], stb[h]) for h in heads]
            dst = [dst_ref[h] + _tn(do[h], qs[h]) for h in heads]
            for h in heads:
                dp_ref[rs, q0 + h * dk:q0 + (h + 1) * dk] = (dqs[h] * qscale).astype(BF16)
            dstb = [dst[h].astype(BF16) for h in heads]
            dvh = [_nt(kdec_ref[rs, kcs[h]].astype(BF16), dstb[h]) for h in heads]
            dkdec = [_nn(v_ref[rs, vcs[h]].astype(BF16), dstb[h]) for h in heads]
            gdg = []
            for h in heads:
                dp_ref[rs, v0 + h * dv:v0 + (h + 1) * dv] = dvh[h].astype(BF16)
                dp_ref[rs, k0 + h * dk:k0 + (h + 1) * dk] = (dkdec[h] * dec_ref[rs, kcs[h]]).astype(BF16)
                e_ref[rs, kcs[h]] = dkdec[h] * kdec_ref[rs, kcs[h]]
                st_prev = jnp.where(c > 0, st_blk_ref[jnp.maximum(c - 1, 0), h], st_prev_ref[0, h])
                st_prev = jnp.where(first, 0.0, st_prev)
                gdg.append(jnp.sum(dst[h] * st_prev, axis=0, keepdims=True) * gam[:, kcs[h]])
                dst_ref[h] = dst[h] * gam[:, kcs[h]]
            dla_ref[rs, :] = jnp.broadcast_to(jnp.concatenate(gdg, axis=1), (CHUNK, DKT))
            return carry

        lax.fori_loop(0, ncb, chunk, 0, unroll=4)

        ev = e_ref[...]
        dla = dla_ref[...] + _chunk_scan(ev, rows, False) - ev
        dz = dla * (1.0 / GATE_TEMP) * (1.0 - _sigmoid(z))
        dzb = dz.astype(BF16)
        dp_ref[:, a0:a0 + LANES] = _nt(dzb, wa_ref[...]).astype(BF16)
        dwa = _tn(alr_ref[...].astype(BF16), dzb)
        dba = jnp.sum(dz, axis=0, keepdims=True)

        @pl.when(i == 0)
        def _():
            dwa_ref[...] = dwa
            dba_ref[...] = dba
            dhn_ref[...] = dhn_acc_ref[...]

        @pl.when(i > 0)
        def _():
            dwa_ref[...] += dwa
            dba_ref[...] += dba
            dhn_ref[...] += dhn_acc_ref[...]

    full = lambda shape: pl.BlockSpec(shape, lambda i: (0,) * len(shape))
    rowblk = lambda w: pl.BlockSpec((tb, w), lambda i: (rev(i), 0))
    return pl.pallas_call(
        body, name=name, grid=(nb,),
        in_specs=_gla_specs(DP, DKT, DV, tb, rev) + [
            pl.BlockSpec((ncb, N_HEADS, dv, dk), lambda i: (rev(i), 0, 0, 0)),
            pl.BlockSpec((1, N_HEADS, dv, dk), lambda i: (jnp.maximum(rev(i) * ncb - 1, 0), 0, 0, 0)),
            pl.BlockSpec((tb, DV), lambda i: (rev(i), DP // DV)), rowblk(DP),
            full((LANES, DKT)), full((1, DKT)), full((1, DV))],
        out_specs=[rowblk(a0 + LANES), full((LANES, DKT)), full((1, DKT)), full((1, DV))],
        out_shape=[jax.ShapeDtypeStruct((S, a0 + LANES), BF16), jax.ShapeDtypeStruct((LANES, DKT), F32),
                   jax.ShapeDtypeStruct((1, DKT), F32), jax.ShapeDtypeStruct((1, DV), F32)],
        scratch_shapes=[pltpu.VMEM((N_HEADS, dv, dk), F32)] + [pltpu.VMEM((tb, DKT), F32)] * 5 + [pltpu.VMEM((1, DV), F32)],
        compiler_params=_params())(proj, proj, proj, proj, proj, states, states, dymix, du, w_a2, b_a, head_norm)


def _xattn_fwd(q, kv, name, tm=512):
    S, D = q.shape
    M = kv.shape[0]
    hd = D // N_HEADS
    tm = _tile(S, tm)
    scale = hd ** -0.5

    def body(q_ref, k_ref, v_ref, o_ref):
        heads = range(N_HEADS)
        hcs = [slice(h * hd, (h + 1) * hd) for h in heads]
        s = [_nt(q_ref[:, hc], k_ref[:, hc]) * scale for hc in hcs]
        p = []
        for h in heads:
            e = jnp.exp(s[h] - jnp.max(s[h], axis=-1, keepdims=True))
            p.append((e / jnp.sum(e, axis=-1, keepdims=True)).astype(BF16))
        o = [_nn(p[h], v_ref[:, hcs[h]]) for h in heads]
        for h in heads:
            o_ref[:, hcs[h]] = o[h].astype(BF16)

    return pl.pallas_call(body, name=name, grid=(S // tm,),
                          in_specs=[pl.BlockSpec((tm, D), lambda i: (i, 0)), pl.BlockSpec((M, D), lambda i: (0, 0)),
                                    pl.BlockSpec((M, D), lambda i: (0, 1))],
                          out_specs=pl.BlockSpec((tm, D), lambda i: (i, 0)),
                          out_shape=jax.ShapeDtypeStruct((S, D), BF16), compiler_params=_params())(q, kv, kv)


def _xattn_bwd(q, kv, do, name, tm=512):
    S, D = q.shape
    M = kv.shape[0]
    hd = D // N_HEADS
    tm = _tile(S, tm)
    scale = hd ** -0.5

    def body(q_ref, k_ref, v_ref, do_ref, dq_ref, dkv_ref):
        first = pl.program_id(0) == 0
        heads = range(N_HEADS)
        hcs = [slice(h * hd, (h + 1) * hd) for h in heads]
        s = [_nt(q_ref[:, hc], k_ref[:, hc]) * scale for hc in hcs]
        dp = [_nt(do_ref[:, hc], v_ref[:, hc]) for hc in hcs]
        p = []
        for h in heads:
            e = jnp.exp(s[h] - jnp.max(s[h], axis=-1, keepdims=True))
            p.append(e / jnp.sum(e, axis=-1, keepdims=True))
        dvh = [_tn(p[h].astype(BF16), do_ref[:, hcs[h]]) for h in heads]
        ds = [((p[h] * (dp[h] - jnp.sum(dp[h] * p[h], axis=-1, keepdims=True))) * scale).astype(BF16) for h in heads]
        dqh = [_nn(ds[h], k_ref[:, hcs[h]]) for h in heads]
        dkh = [_tn(ds[h], q_ref[:, hcs[h]]) for h in heads]
        for h in heads:
            dq_ref[:, hcs[h]] = dqh[h].astype(BF16)

        @pl.when(first)
        def _():
            for h in heads:
                dkv_ref[:, hcs[h]] = dkh[h]
                dkv_ref[:, D + h * hd:D + (h + 1) * hd] = dvh[h]

        @pl.when(jnp.logical_not(first))
        def _():
            for h in heads:
                dkv_ref[:, hcs[h]] += dkh[h]
                dkv_ref[:, D + h * hd:D + (h + 1) * hd] += dvh[h]

    row = pl.BlockSpec((tm, D), lambda i: (i, 0))
    return pl.pallas_call(body, name=name, grid=(S // tm,),
                          in_specs=[row, pl.BlockSpec((M, D), lambda i: (0, 0)), pl.BlockSpec((M, D), lambda i: (0, 1)), row],
                          out_specs=[row, pl.BlockSpec((M, 2 * D), lambda i: (0, 0))],
                          out_shape=[jax.ShapeDtypeStruct((S, D), BF16), jax.ShapeDtypeStruct((M, 2 * D), F32)],
                          compiler_params=_params())(q, kv, kv, do)


def _local_step(x, mem, target, vec, weight, emit, dep0):
    g = {}
    pending = []
    begun = []
    summed = []
    emit_begin, emit_finish, emit_send, early_update = emit

    def behind(fn, *a, **kw):
        dep = tuple(pending)
        pending.clear()
        out = fn(*a, dep=dep, **kw)
        while summed:
            pending.append(emit_send(summed.pop(0)))
        while begun:
            name = begun.pop(0)
            token = emit_finish(name, out)
            if token is None:
                summed.append(name)
            else:
                pending.append(token)
        return out

    def mm(a, b, **kw):
        return behind(_matmul, a, b, **kw)

    def send(name, gfull):
        pending.append(emit_begin(name, gfull))
        begun.append(name)

    def ffn_fwd(xin, tag, dep):
        h = _rms_fwd(xin, vec[f"{tag}_norm"], f"{tag}_norm", dep=dep)
        ga, gb, hid = _ffn_up(h, weight(f"{tag}_w_gate", h), weight(f"{tag}_w_up", h), f"{tag}_up")
        wd = weight(f"{tag}_w_down", hid)
        G, Fj, D = wd.shape
        xo = _matmul(hid, wd.reshape(G * Fj, D), mode="nn", name=f"{tag}_down", out_dtype=F32, res=xin, scale=0.5,
                     tn=1024, tk=G * Fj, cols_outer=True)
        return xo, (h, ga, gb, hid)

    def ffn_bwd(dxh, saved, tag, kept_back=None):
        h, ga, gb, hid = saved
        wg, wu, wd = weight(f"{tag}_w_gate"), weight(f"{tag}_w_up"), weight(f"{tag}_w_down")
        G, Fj, D = wd.shape
        send(f"{tag}_w_down", mm(hid, dxh, mode="tn", name=f"{tag}_dwd", out_dtype=F32, tm=Fj, tn=1024))
        da, db = _ffn_dact(dxh, wd, ga, gb, f"{tag}_dact")
        send(f"{tag}_w_gate", mm(h, da, mode="tn", name=f"{tag}_dwg", out_dtype=F32, tm=1024, tn=Fj, tk=TOKENS_PER_STEP,
                                  out_groups=G))
        send(f"{tag}_w_up", mm(h, db, mode="tn", name=f"{tag}_dwu", out_dtype=F32, tm=1024, tn=Fj, tk=TOKENS_PER_STEP,
                                out_groups=G))
        if kept_back is not None:
            name, gfull = kept_back()
            pending.append(emit_begin(name, gfull))
            pending.append(emit_finish(name, pending[-1]))
        return behind(_ffn_dh, da, db, wg, wu, f"{tag}_dh")

    x1, ffn1_saved = ffn_fwd(x, "ffn1", dep0)
    h2 = _rms_fwd(x1, vec["mix_norm"], "mix_norm")
    w_in = weight("w_in", h2)
    proj = _matmul(h2, w_in, mode="nn", name="w_in", out_dtype=F32, tn=1408, cols_outer=True)
    pool_w, w_a2 = weight("pool_w", h2), weight("gla_w_a2", h2)
    y_pool, dpool = _pool_fwd(proj, pool_w, vec["pool_scale"], "pool_fwd")
    ymix, states = _gla_fwd(proj, y_pool, w_a2, vec["gla_b_a"], vec["gla_head_norm"], "gla_fwd")
    w_out = weight("w_out", ymix)
    x2 = _matmul(ymix, w_out, mode="nn", name="w_out", out_dtype=F32, res=x1)
    h3 = _rms_fwd(x2, vec["xattn_norm"], "xattn_norm")
    mh = _rms_fwd(mem, vec["mem_norm"], "mem_norm")
    w_q = weight("xattn_w_q", h3)
    q = _matmul(h3, w_q, mode="nn", name="xattn_q", out_dtype=BF16)
    w_kv = weight("xattn_w_kv", q)
    kv = _matmul(mh, w_kv, mode="nn", name="xattn_kv", out_dtype=BF16, b_groups=True, tn=1024)
    o = _xattn_fwd(q, kv, "xattn_fwd")
    w_o = weight("xattn_w_o", o)
    x3 = _matmul(o, w_o, mode="nn", name="xattn_o", out_dtype=F32, res=x2)
    x4, ffn2_saved = ffn_fwd(x3, "ffn2", None)
    sq, dx4, dx4h, g["final_norm"] = _loss_head(x4, vec["final_norm"], target, "loss_head")
    loss = lax.psum(0.5 * jnp.sum(sq) / x.shape[-1], ("x", "y", "c"))
    pending.append(loss.reshape(1, 1))

    dh = ffn_bwd(dx4h, ffn2_saved, "ffn2")
    dx3, dx3b, g["ffn2_norm"] = _rms_bwd(x3, vec["ffn2_norm"], dh, dx4, "ffn2_norm_bwd", lowp=1.0)
    send("xattn_w_o", mm(o, dx3b, mode="tn", name="xattn_dwo", out_dtype=F32, tm=1024, tn=1024, tk=TOKENS_PER_STEP))
    do = mm(dx3b, w_o, mode="nt", name="xattn_do", out_dtype=BF16)
    dq, dkv = _xattn_bwd(q, kv, do, "xattn_bwd")
    send("xattn_w_q", mm(h3, dq, mode="tn", name="xattn_dwq", out_dtype=F32, tm=1024, tn=1024, tk=TOKENS_PER_STEP))
    dh3 = mm(dq, w_q, mode="nt", name="xattn_dh", out_dtype=F32)
    dkvb = _cast(dkv, BF16, "dkv_cast")
    send("xattn_w_kv", mm(mh, dkvb, mode="tn", name="xattn_dwkv", out_dtype=F32, tm=1024, tn=1024, out_groups=N_SHARDS))
    dmh = mm(dkvb, w_kv, mode="nt", name="xattn_dmh", out_dtype=F32, b_groups=True, tk=1024)
    _, g["mem_norm"] = _rms_bwd(mem, vec["mem_norm"], dmh, None, "mem_norm_bwd")
    pending.append(g["mem_norm"])
    dx2, dx2b, g["xattn_norm"] = _rms_bwd(x2, vec["xattn_norm"], dh3, dx3, "xattn_norm_bwd", lowp=1.0)
    send("w_out", mm(ymix, dx2b, mode="tn", name="dw_out", out_dtype=F32, tm=1024, tn=1024, tk=TOKENS_PER_STEP))
    dymix = mm(dx2b, w_out, mode="nt", name="dymix", out_dtype=F32)
    du, dpool_w, g["pool_scale"] = _pool_bwd(dymix, dpool, pool_w, vec["pool_scale"], "pool_bwd")
    send("pool_w", dpool_w)
    dproj, dw_a2, g["gla_b_a"], g["gla_head_norm"] = _gla_bwd(
        proj, states, dymix, du, w_a2, vec["gla_b_a"], vec["gla_head_norm"], "gla_bwd")
    send("gla_w_a2", dw_a2)
    dh2 = mm(dproj, w_in, mode="nt", name="dh2", out_dtype=F32, tn=1024, tk=dproj.shape[1], cols_outer=True)
    pending.extend(early_update(dh2))
    dx1, dx1h, g["mix_norm"] = _rms_bwd(x1, vec["mix_norm"], dh2, dx2, "mix_norm_bwd", lowp=0.5)
    dh = ffn_bwd(dx1h, ffn1_saved, "ffn1", kept_back=lambda: (
        "w_in", mm(h2, dproj, mode="tn", name="dw_in", out_dtype=F32, tm=1024, tn=1408, tk=TOKENS_PER_STEP)))
    dx0, g["ffn1_norm"] = _rms_bwd(x, vec["ffn1_norm"], dh, dx1, "ffn1_norm_bwd")
    return loss, dx0, g


def _place():
    x, y, c = lax.axis_index("x"), lax.axis_index("y"), lax.axis_index("c")
    chips = [(1 - x, y), (x, 1 - y), (1 - x, 1 - y)]
    return x, y, c, chips


def _ids():
    return jnp.stack([2 * lax.axis_index("x") + lax.axis_index("y"), lax.axis_index("c")]).astype(jnp.int32)


def _hbm(a):
    return pltpu.with_memory_space_constraint(a, pltpu.HBM)


def _cast_to_slot(w2d, dtype, name, dep=None):
    R, C = w2d.shape
    tr = _tile(R, max(16, (4 << 20) // (4 * C) // 16 * 16))

    def body(i_ref, w_ref, *rest):
        rest[-1][...] = w_ref[...].astype(dtype)

    in_specs = [pl.BlockSpec((tr, C), lambda r, i: (r, 0))]
    operands = [w2d]
    if dep is not None:
        in_specs.append(pl.BlockSpec(dep.shape, lambda r, i: (0, 0)))
        operands.append(dep)
    grid_spec = pltpu.PrefetchScalarGridSpec(num_scalar_prefetch=1, grid=(R // tr,), in_specs=in_specs,
                                             out_specs=pl.BlockSpec((None, tr, C), lambda r, i: (i[0], r, 0)))
    return pl.pallas_call(body, name=name, grid_spec=grid_spec, out_shape=jax.ShapeDtypeStruct((N_SHARDS, R, C), dtype),
                          compiler_params=_params())(_ids(), *operands)


def _gather_copies(buf_ref, send_sems, recv_sems, incoming, whole):
    x, y, c, chips = _place()
    hr = buf_ref.shape[1] // 2
    copies = []
    for j, (px, py) in enumerate(chips):
        slot = 2 * px + py if incoming else 2 * x + y
        part = buf_ref.at[slot] if whole else buf_ref.at[slot, pl.ds(c * hr, hr), :]
        copies.append(pltpu.make_async_remote_copy(src_ref=part, dst_ref=part, send_sem=send_sems.at[j],
                                                   recv_sem=recv_sems.at[j], device_id=(px, py, c), device_id_type=MESH))
    return copies


def _gather_start(buf, name, whole):
    def body(b_ref, send_sems, recv_sems, b_thru, token):
        for cp in _gather_copies(b_ref, send_sems, recv_sems, False, whole):
            cp.start()
        token[...] = jnp.zeros_like(token)

    return pl.pallas_call(
        body, name=name,
        out_shape=(pltpu.SemaphoreType.DMA((3,)), pltpu.SemaphoreType.DMA((3,)), pltpu.HBM(buf.shape, buf.dtype),
                   jax.ShapeDtypeStruct((8, LANES), F32)),
        in_specs=(HBM,), out_specs=(SEM, SEM, HBM, pl.BlockSpec(memory_space=pltpu.VMEM)), input_output_aliases={0: 2},
        compiler_params=pltpu.CompilerParams(has_side_effects=EFFECT))(_hbm(buf))


def _gather_wait(send_sems, recv_sems, buf_thru, after, name, whole):
    def body(b_ref, send_sems, recv_sems, after_ref, b_out):
        for cp in _gather_copies(b_ref, send_sems, recv_sems, False, whole):
            cp.wait_send()
        for cp in _gather_copies(b_ref, send_sems, recv_sems, True, whole):
            cp.wait_recv()

    return pl.pallas_call(
        body, name=name, out_shape=pltpu.HBM(buf_thru.shape, buf_thru.dtype),
        in_specs=(HBM, SEM, SEM, ANY), out_specs=HBM, input_output_aliases={0: 0},
        compiler_params=pltpu.CompilerParams(has_side_effects=EFFECT))(buf_thru, send_sems, recv_sems, after)


def _gather_forward(buf, name):
    G, R, C = buf.shape
    hr = R // 2

    def body(b_ref, o_ref, send_sems, recv_sems):
        x, y, c, chips = _place()
        copies = []
        for j, (px, py) in enumerate(chips):
            half = o_ref.at[2 * px + py, pl.ds(c * hr, hr), :]
            copies.append(pltpu.make_async_remote_copy(src_ref=half, dst_ref=half, send_sem=send_sems.at[j],
                                                       recv_sem=recv_sems.at[j], device_id=(x, y, 1 - c),
                                                       device_id_type=MESH))
        for cp in copies:
            cp.start()
        for j, (px, py) in enumerate(chips):
            half = o_ref.at[2 * px + py, pl.ds((1 - c) * hr, hr), :]
            pltpu.make_async_remote_copy(src_ref=half, dst_ref=half, send_sem=send_sems.at[j], recv_sem=recv_sems.at[j],
                                         device_id=(x, y, 1 - c), device_id_type=MESH).wait_recv()
        for cp in copies:
            cp.wait_send()

    return pl.pallas_call(body, name=name, in_specs=[ANY], out_specs=ANY, out_shape=jax.ShapeDtypeStruct(buf.shape, buf.dtype),
                          input_output_aliases={0: 0},
                          scratch_shapes=[pltpu.SemaphoreType.DMA((3,)), pltpu.SemaphoreType.DMA((3,))])(buf)


def _pair_copy(g_ref, land_ref, send_sem, recv_sem):
    x, y, c, _ = _place()
    hr = g_ref.shape[1] // 2
    return pltpu.make_async_remote_copy(src_ref=g_ref.at[:, pl.ds((1 - c) * hr, hr), :], dst_ref=land_ref,
                                        send_sem=send_sem, recv_sem=recv_sem, device_id=(x, y, 1 - c), device_id_type=MESH)


def _pair_start(gfull, name):
    G, R, C = gfull.shape

    def body(g_ref, land_ref, send_sem, recv_sem, g_thru, land_thru, token):
        _pair_copy(g_ref, land_ref, send_sem, recv_sem).start()
        token[...] = jnp.zeros_like(token)

    return pl.pallas_call(
        body, name=name,
        out_shape=(pltpu.SemaphoreType.DMA(()), pltpu.SemaphoreType.DMA(()), pltpu.HBM(gfull.shape, F32),
                   pltpu.HBM((G, R // 2, C), F32), jax.ShapeDtypeStruct((8, LANES), F32)),
        in_specs=(HBM, HBM), out_specs=(SEM, SEM, HBM, HBM, pl.BlockSpec(memory_space=pltpu.VMEM)),
        input_output_aliases={0: 2, 1: 3},
        compiler_params=pltpu.CompilerParams(has_side_effects=EFFECT))(_hbm(gfull), _hbm(lax.empty((G, R // 2, C), F32)))


def _pair_wait(send_sem, recv_sem, g_thru, land_thru, after, name):
    def body(g_ref, land_ref, send_sem, recv_sem, after_ref, g_out, land_out):
        cp = _pair_copy(g_ref, land_ref, send_sem, recv_sem)
        cp.wait_send()
        cp.wait_recv()

    return pl.pallas_call(
        body, name=name, out_shape=(pltpu.HBM(g_thru.shape, F32), pltpu.HBM(land_thru.shape, F32)),
        in_specs=(HBM, HBM, SEM, SEM, ANY), out_specs=(HBM, HBM), input_output_aliases={0: 0, 1: 1},
        compiler_params=pltpu.CompilerParams(has_side_effects=EFFECT))(g_thru, land_thru, send_sem, recv_sem, after)


def _pair_add(gfull, other, name):
    G, R, C = gfull.shape
    hr = R // 2
    tr = _tile(hr, max(8, (2 * 1024 * 1024) // (4 * C) // 8 * 8))
    nr = hr // tr
    c = lax.axis_index("c")
    cidx = jnp.reshape(c, (1,)).astype(jnp.int32)

    def body(c_ref, a_ref, b_ref, o_ref):
        o_ref[...] = a_ref[...] + b_ref[...]

    grid_spec = pltpu.PrefetchScalarGridSpec(
        num_scalar_prefetch=1, grid=(G, nr),
        in_specs=[pl.BlockSpec((None, tr, C), lambda g, r, cr: (g, cr[0] * nr + r, 0)),
                  pl.BlockSpec((None, tr, C), lambda g, r, cr: (g, r, 0))],
        out_specs=pl.BlockSpec((None, tr, C), lambda g, r, cr: (g, r, 0)))
    return pl.pallas_call(body, name=name, grid_spec=grid_spec, out_shape=jax.ShapeDtypeStruct((G, hr, C), F32),
                          compiler_params=_params())(cidx, gfull, other)


def _chip_copies(p_ref, land_ref, send_sems, recv_sems, incoming):
    x, y, c, chips = _place()
    me = 2 * x + y
    copies = []
    for j, (px, py) in enumerate(chips):
        dst = land_ref.at[2 * px + py] if incoming else land_ref.at[me]
        copies.append(pltpu.make_async_remote_copy(src_ref=p_ref.at[2 * px + py], dst_ref=dst, send_sem=send_sems.at[j],
                                                   recv_sem=recv_sems.at[j], device_id=(px, py, c), device_id_type=MESH))
    return copies


def _chip_start(part, name):
    def body(p_ref, land_ref, send_sems, recv_sems, p_thru, land_thru, token):
        for cp in _chip_copies(p_ref, land_ref, send_sems, recv_sems, False):
            cp.start()
        token[...] = jnp.zeros_like(token)

    return pl.pallas_call(
        body, name=name,
        out_shape=(pltpu.SemaphoreType.DMA((3,)), pltpu.SemaphoreType.DMA((3,)), pltpu.HBM(part.shape, F32),
                   pltpu.HBM(part.shape, F32), jax.ShapeDtypeStruct((8, LANES), F32)),
        in_specs=(HBM, HBM), out_specs=(SEM, SEM, HBM, HBM, pl.BlockSpec(memory_space=pltpu.VMEM)),
        input_output_aliases={0: 2, 1: 3},
        compiler_params=pltpu.CompilerParams(has_side_effects=EFFECT))(_hbm(part), _hbm(lax.empty(part.shape, F32)))


def _chip_wait(send_sems, recv_sems, p_thru, land_thru, after, name):
    def body(p_ref, land_ref, send_sems, recv_sems, after_ref, p_out, land_out):
        for cp in _chip_copies(p_ref, land_ref, send_sems, recv_sems, False):
            cp.wait_send()
        for cp in _chip_copies(p_ref, land_ref, send_sems, recv_sems, True):
            cp.wait_recv()

    return pl.pallas_call(
        body, name=name, out_shape=(pltpu.HBM(p_thru.shape, F32), pltpu.HBM(p_thru.shape, F32)),
        in_specs=(HBM, HBM, SEM, SEM, ANY), out_specs=(HBM, HBM), input_output_aliases={0: 0, 1: 1},
        compiler_params=pltpu.CompilerParams(has_side_effects=EFFECT))(p_thru, land_thru, send_sems, recv_sems, after)


def _chip_sum(part, slots, name):
    G, R2, C = part.shape
    tr = _tile(R2, max(8, (1 << 20) // (4 * C) // 8 * 8))
    nr = R2 // tr

    def body(i_ref, p_ref, *rest):
        o_ref = rest[-1]
        acc = None
        for u in range(G):
            val = jnp.where(i_ref[0] == u, p_ref[...], rest[u][...])
            acc = val if acc is None else acc + val
        o_ref[...] = acc

    def slot_spec(u):
        return pl.BlockSpec((None, tr, C), lambda r, i: (jnp.where(i[0] == u, (u + 1) % G, u), r, 0))

    grid_spec = pltpu.PrefetchScalarGridSpec(
        num_scalar_prefetch=1, grid=(nr,),
        in_specs=[pl.BlockSpec((None, tr, C), lambda r, i: (i[0], r, 0))] + [slot_spec(u) for u in range(G)],
        out_specs=pl.BlockSpec((tr, C), lambda r, i: (i[1] * nr + r, 0)))
    return pl.pallas_call(body, name=name, grid_spec=grid_spec, out_shape=jax.ShapeDtypeStruct((2 * R2, C), F32),
                          compiler_params=_params())(_ids(), part, slots, slots, slots, slots)


def _sum_slots(slots, name):
    G, R2, C = slots.shape
    tr = _tile(R2, max(8, (1024 * 1024) // (4 * C) // 8 * 8))

    def body(s_ref, o_ref):
        acc = s_ref[0]
        for u in range(1, G):
            acc = acc + s_ref[u]
        o_ref[...] = acc

    return pl.pallas_call(body, name=name, grid=(R2 // tr,), in_specs=[pl.BlockSpec((G, tr, C), lambda r: (0, r, 0))],
                          out_specs=pl.BlockSpec((tr, C), lambda r: (r, 0)), out_shape=jax.ShapeDtypeStruct((R2, C), F32),
                          compiler_params=_params())(slots)


def _pair_join(full, name):
    R, C = full.shape
    R2 = R // 2

    def body(f_ref, o_ref, token, send_sem, recv_sem):
        x, y, c, _ = _place()
        token[...] = jnp.zeros_like(token)
        mine = o_ref.at[pl.ds(c * R2, R2), :]
        theirs = o_ref.at[pl.ds((1 - c) * R2, R2), :]
        cp = pltpu.make_async_remote_copy(src_ref=mine, dst_ref=mine, send_sem=send_sem, recv_sem=recv_sem,
                                          device_id=(x, y, 1 - c), device_id_type=MESH)
        cp.start()
        pltpu.make_async_remote_copy(src_ref=theirs, dst_ref=theirs, send_sem=send_sem, recv_sem=recv_sem,
                                     device_id=(x, y, 1 - c), device_id_type=MESH).wait_recv()
        cp.wait_send()

    return pl.pallas_call(body, name=name, in_specs=[ANY], out_specs=[ANY, pl.BlockSpec(memory_space=pltpu.VMEM)],
                          out_shape=[jax.ShapeDtypeStruct((R, C), F32), jax.ShapeDtypeStruct((8, LANES), F32)],
                          input_output_aliases={0: 0},
                          scratch_shapes=[pltpu.SemaphoreType.DMA, pltpu.SemaphoreType.DMA])(full)


def _all_reduce_small(v, name):
    R, C = v.shape

    def gather_body(v_ref, out_ref, send_sems, recv_sems, local_sem):
        x, y, c, _ = _place()
        me = 4 * x + 2 * y + c
        mine = pltpu.make_async_copy(v_ref, out_ref.at[me], local_sem)
        mine.start()
        flips = [(fx, fy, fc) for fx in (0, 1) for fy in (0, 1) for fc in (0, 1)][1:]
        copies = []
        for j, (fx, fy, fc) in enumerate(flips):
            peer = (x ^ fx, y ^ fy, c ^ fc)
            copies.append(pltpu.make_async_remote_copy(src_ref=v_ref, dst_ref=out_ref.at[me], send_sem=send_sems.at[j],
                                                       recv_sem=recv_sems.at[j], device_id=peer, device_id_type=MESH))
        for cp in copies:
            cp.start()
        for j, (fx, fy, fc) in enumerate(flips):
            peer = (x ^ fx, y ^ fy, c ^ fc)
            pltpu.make_async_remote_copy(src_ref=v_ref, dst_ref=out_ref.at[4 * peer[0] + 2 * peer[1] + peer[2]],
                                         send_sem=send_sems.at[j], recv_sem=recv_sems.at[j], device_id=peer,
                                         device_id_type=MESH).wait_recv()
        for cp in copies:
            cp.wait_send()
        mine.wait()

    slots = pl.pallas_call(gather_body, name=name, in_specs=[ANY], out_specs=ANY,
                           out_shape=jax.ShapeDtypeStruct((8, R, C), F32),
                           scratch_shapes=[pltpu.SemaphoreType.DMA((7,)), pltpu.SemaphoreType.DMA((7,)),
                                           pltpu.SemaphoreType.DMA])(v)
    return _sum_slots(slots, f"{name}_sum")


def _adamw(w, g, m, v, name, dep=()):
    R, C = w.shape
    tr = _tile(R, max(8, (2 << 20) // (4 * C) // 8 * 8))
    bc1 = 1.0 - ADAM_B1 ** ADAM_STEP
    bc2 = 1.0 - ADAM_B2 ** ADAM_STEP

    def body(w_ref, g_ref, m_ref, v_ref, *rest):
        go_ref, d_ref, nm_ref, nv_ref = rest[len(dep):]
        gv = g_ref[...]
        go_ref[...] = gv
        nm = ADAM_B1 * m_ref[...] + (1.0 - ADAM_B1) * gv
        nv = ADAM_B2 * v_ref[...] + (1.0 - ADAM_B2) * (gv * gv)
        nm_ref[...] = nm
        nv_ref[...] = nv
        d_ref[...] = -ADAM_LR * ((nm / bc1) / (jnp.sqrt(nv / bc2) + ADAM_EPS) + ADAM_WD * w_ref[...])

    blk = pl.BlockSpec((tr, C), lambda r: (r, 0))
    out = jax.ShapeDtypeStruct((R, C), F32)
    in_specs = [blk] * 4 + [pl.BlockSpec(d.shape, lambda r: (0, 0)) for d in dep]
    return pl.pallas_call(body, name=name, grid=(R // tr,), in_specs=in_specs, out_specs=[blk] * 4, out_shape=[out] * 4,
                          compiler_params=_params())(w, g, m, v, *dep)


SC_TILES = 32
SC_LANES = 16
SC_ROWS = 8


def _sc_mesh():
    return plsc.VectorSubcoreMesh(core_axis_name="sc_core", subcore_axis_name="sc_subcore")


def _pair_add_sc(gfull, other, name):
    G, R, C = gfull.shape
    hr = R // 2
    tiles_per_shard = SC_TILES // G
    per_tile = hr // SC_ROWS // tiles_per_shard

    def body(g_hbm, o_hbm, out_hbm, gb, ob):
        c = lax.axis_index("c")
        tile = lax.axis_index("sc_subcore") * 2 + lax.axis_index("sc_core")
        t = tile // tiles_per_shard
        first = (tile % tiles_per_shard) * per_tile

        @pl.loop(0, per_tile)
        def _(k):
            rr = (first + k) * SC_ROWS
            pltpu.sync_copy(g_hbm.at[t, pl.ds(c * hr + rr, SC_ROWS), :], gb)
            pltpu.sync_copy(o_hbm.at[t, pl.ds(rr, SC_ROWS), :], ob)

            @pl.loop(0, SC_ROWS)
            def _(i):
                @pl.loop(0, C, step=SC_LANES)
                def _(j):
                    at = (i, pl.ds(j, SC_LANES))
                    gb[at] = gb[at] + ob[at]

            pltpu.sync_copy(gb, out_hbm.at[t, pl.ds(rr, SC_ROWS), :])

    buf = pltpu.VMEM((SC_ROWS, C), F32)
    return pl.kernel(body, name=name, out_type=jax.ShapeDtypeStruct((G, hr, C), F32), mesh=_sc_mesh(),
                     scratch_types=[buf, buf])(gfull, other)


def _adamw_sc(w, g, m, v, name):
    R, C = w.shape
    tasks = R // SC_ROWS
    bc1 = 1.0 - ADAM_B1 ** ADAM_STEP
    bc2 = 1.0 - ADAM_B2 ** ADAM_STEP

    def body(w_hbm, g_hbm, m_hbm, v_hbm, go_hbm, d_hbm, nm_hbm, nv_hbm, wb, gb, mb, vb):
        tile = lax.axis_index("sc_subcore") * 2 + lax.axis_index("sc_core")

        @pl.loop((tile * tasks) // SC_TILES, ((tile + 1) * tasks) // SC_TILES)
        def _(task):
            rows = pl.ds(task * SC_ROWS, SC_ROWS)
            pltpu.sync_copy(w_hbm.at[rows, :], wb)
            pltpu.sync_copy(g_hbm.at[rows, :], gb)
            pltpu.sync_copy(m_hbm.at[rows, :], mb)
            pltpu.sync_copy(v_hbm.at[rows, :], vb)

            @pl.loop(0, SC_ROWS)
            def _(i):
                @pl.loop(0, C, step=SC_LANES)
                def _(j):
                    at = (i, pl.ds(j, SC_LANES))
                    gv = gb[at]
                    nm = ADAM_B1 * mb[at] + (1.0 - ADAM_B1) * gv
                    nv = ADAM_B2 * vb[at] + (1.0 - ADAM_B2) * (gv * gv)
                    mb[at] = nm
                    vb[at] = nv
                    wb[at] = -ADAM_LR * ((nm / bc1) / (jnp.sqrt(nv / bc2) + ADAM_EPS) + ADAM_WD * wb[at])

            pltpu.sync_copy(gb, go_hbm.at[rows, :])
            pltpu.sync_copy(wb, d_hbm.at[rows, :])
            pltpu.sync_copy(mb, nm_hbm.at[rows, :])
            pltpu.sync_copy(vb, nv_hbm.at[rows, :])

    out = jax.ShapeDtypeStruct((R, C), F32)
    buf = pltpu.VMEM((SC_ROWS, C), F32)
    return pl.kernel(body, name=name, out_type=(out, out, out, out), mesh=_sc_mesh(),
                     scratch_types=[buf, buf, buf, buf])(w, g, m, v)


WEIGHTS = ['ffn1_norm', 'ffn1_w_gate', 'ffn1_w_up', 'ffn1_w_down', 'mix_norm', 'w_in', 'pool_w', 'pool_scale', 'gla_w_a2',
           'gla_b_a', 'gla_head_norm', 'w_out', 'xattn_norm', 'mem_norm', 'xattn_w_q', 'xattn_w_kv', 'xattn_w_o', 'ffn2_norm',
           'ffn2_w_gate', 'ffn2_w_up', 'ffn2_w_down', 'final_norm']
SHARDED = ['ffn1_w_gate', 'ffn1_w_up', 'ffn1_w_down', 'w_in', 'pool_w', 'gla_w_a2', 'w_out', 'xattn_w_q', 'xattn_w_kv',
           'xattn_w_o', 'ffn2_w_gate', 'ffn2_w_up', 'ffn2_w_down']
REPLICATED = [n for n in WEIGHTS if n not in SHARDED]
ON_SPARSECORE = ['ffn2_w_gate', 'ffn2_w_up', 'w_out', 'xattn_w_q', 'xattn_w_kv', 'xattn_w_o', 'ffn2_w_down']
SLOW_ON_SPARSECORE = 'ffn2_w_down'
PAIR_SUM_ON_SPARSECORE = ['ffn2_w_down', 'ffn2_w_gate', 'ffn2_w_up', 'xattn_w_o', 'xattn_w_kv', 'pool_w', 'ffn1_w_down']
SMALL_COLS = 512


def _as2d(a):
    return a.reshape(-1, a.shape[-1])


def _finish_weight(name, gathered, wl):
    G, R, C = gathered.shape
    rank = wl["gla_w_a2"].shape[1]
    if name in ("w_out", "xattn_w_q", "xattn_w_o"):
        return gathered.reshape(G * R, C)
    if name == "w_in":
        w_in = jnp.transpose(gathered, (1, 0, 2)).reshape(R, G * C)
        main = G * C - rank
        return jnp.concatenate([w_in[:, :main], jnp.pad(w_in[:, main:], ((0, 0), (0, LANES - rank)))], axis=1)
    if name == "pool_w":
        NG, CJ, _ = wl[name].shape[1:]
        return jnp.transpose(gathered.reshape(G, NG, CJ, C), (1, 0, 2, 3)).reshape(NG, G * CJ, C)
    if name == "gla_w_a2":
        a2 = jnp.transpose(gathered, (1, 0, 2)).reshape(rank, G * C)
        return jnp.pad(a2, ((0, LANES - rank), (0, 0))).astype(BF16)
    return gathered


def _start_gathers(wl):
    started = {}
    token = None
    for n in SHARDED:
        whole = n not in ("ffn1_w_gate", "ffn1_w_up")
        buf = _cast_to_slot(_as2d(wl[n]), BF16, f"slot_{n}", dep=token)
        send_sems, recv_sems, thru, token = _gather_start(buf, f"gather_start_{n}", whole)
        started[n] = (send_sems, recv_sems, thru, whole)
    cache = {}

    def weight(n, after=None):
        if n not in cache:
            *handles, whole = started[n]
            buf = _gather_wait(*handles, after, f"gather_wait_{n}", whole)
            if not whole:
                buf = _gather_forward(buf, f"gather_forward_{n}")
            cache[n] = _finish_weight(n, buf, wl)
        return cache[n]

    return weight, token


def _shard_major(name, gfull, wl):
    R, C = _as2d(wl[name]).shape
    if name in ("ffn1_w_gate", "ffn1_w_up", "ffn2_w_gate", "ffn2_w_up", "xattn_w_kv"):
        return gfull
    if name in ("ffn1_w_down", "ffn2_w_down", "w_out", "xattn_w_q", "xattn_w_o"):
        return gfull.reshape(N_SHARDS, R, C)
    if name == "w_in":
        return jnp.transpose(gfull[:, :N_SHARDS * C].reshape(R, N_SHARDS, C), (1, 0, 2))
    if name == "pool_w":
        NG, CJ, _ = wl[name].shape[1:]
        return jnp.transpose(gfull.reshape(NG, N_SHARDS, CJ, C), (1, 0, 2, 3)).reshape(N_SHARDS, R, C)
    assert name == "gla_w_a2"
    return jnp.transpose(gfull[:R].reshape(R, N_SHARDS, C), (1, 0, 2))


def kernel(x, mem, ffn1_norm, ffn1_w_gate, ffn1_w_up, ffn1_w_down, mix_norm, w_in, pool_w, pool_scale, gla_w_a2, gla_b_a, gla_head_norm, w_out, xattn_norm, mem_norm, xattn_w_q, xattn_w_kv, xattn_w_o, ffn2_norm, ffn2_w_gate, ffn2_w_up, ffn2_w_down, final_norm, loss_target, m_ffn1_norm, m_ffn1_w_gate, m_ffn1_w_up, m_ffn1_w_down, m_mix_norm, m_w_in, m_pool_w, m_pool_scale, m_gla_w_a2, m_gla_b_a, m_gla_head_norm, m_w_out, m_xattn_norm, m_mem_norm, m_xattn_w_q, m_xattn_w_kv, m_xattn_w_o, m_ffn2_norm, m_ffn2_w_gate, m_ffn2_w_up, m_ffn2_w_down, m_final_norm, v_ffn1_norm, v_ffn1_w_gate, v_ffn1_w_up, v_ffn1_w_down, v_mix_norm, v_w_in, v_pool_w, v_pool_scale, v_gla_w_a2, v_gla_b_a, v_gla_head_norm, v_w_out, v_xattn_norm, v_mem_norm, v_xattn_w_q, v_xattn_w_kv, v_xattn_w_o, v_ffn2_norm, v_ffn2_w_gate, v_ffn2_w_up, v_ffn2_w_down, v_final_norm):
    given = dict(locals())
    wl = {n: given[n] for n in WEIGHTS}
    ml = {n: given["m_" + n] for n in WEIGHTS}
    vl = {n: given["v_" + n] for n in WEIGHTS}

    vec = {n: wl[n].reshape(1, -1) for n in REPLICATED}
    weight, dep0 = _start_gathers(wl)
    in_flight = {}

    pair_flight = {}

    def emit_begin(n, gfull):
        *pair_flight[n], token = _pair_start(_shard_major(n, gfull, wl), f"{n}_pair_start")
        return token

    summing = {}

    def emit_finish(n, after):
        gsm, other = _pair_wait(*pair_flight.pop(n), after, f"{n}_pair_wait")
        if n in PAIR_SUM_ON_SPARSECORE:
            summing[n] = _pair_add_sc(gsm, other, f"{n}_pair_add_sc")
            return None
        *in_flight[n], token = _chip_start(_pair_add(gsm, other, f"{n}_pair_add"), f"{n}_chip_start")
        return token

    def emit_send(n):
        *in_flight[n], token = _chip_start(summing.pop(n), f"{n}_chip_start")
        return token

    grads = {}
    updates = {}

    def reduce_done(n, after):
        part, slots = _chip_wait(*in_flight.pop(n), after, f"{n}_chip_wait")
        grads[n], token = _pair_join(_chip_sum(part, slots, f"{n}_chip_sum"), f"{n}_pair_join")
        return token

    def early_update(after):
        tokens = [reduce_done(n, after) for n in ON_SPARSECORE]
        for n in ON_SPARSECORE:
            g2 = grads[n]
            updates[n] = _adamw_sc(wl[n].reshape(g2.shape), g2, ml[n].reshape(g2.shape), vl[n].reshape(g2.shape),
                                   f"adamw_sc_{n}")
        return tokens

    loss, dx0, g = _local_step(x[0], mem[0], loss_target[0], vec, weight,
                               (emit_begin, emit_finish, emit_send, early_update), dep0)
    assert not summing

    for n in list(in_flight):
        reduce_done(n, dx0)
    widths = [wl[n].size for n in REPLICATED]
    total = sum(widths)
    rows = -(-total // SMALL_COLS)
    rows = -(-rows // 8) * 8
    packed = jnp.concatenate([g[n].reshape(-1) for n in REPLICATED] + [jnp.zeros((rows * SMALL_COLS - total,), F32)])
    summed = _all_reduce_small(packed.reshape(rows, SMALL_COLS), "small_all_reduce").reshape(-1)
    off = 0
    for n, width in zip(REPLICATED, widths):
        grads[n] = summed[off:off + width].reshape(1, width)
        off += width

    out_g, out_d, out_m, out_v = [], [], [], []
    for n in WEIGHTS:
        shape = wl[n].shape
        g2 = grads[n]
        if n in updates:
            go, d, nm, nv = updates[n]
        else:
            dep = ()
            if n == "w_in":
                dep = tuple(updates[k][1][:8, :LANES] for k in updates if k != SLOW_ON_SPARSECORE)
            if n == "ffn1_w_gate":
                dep = (updates[SLOW_ON_SPARSECORE][1][:8, :LANES],)
            go, d, nm, nv = _adamw(wl[n].reshape(g2.shape), g2, ml[n].reshape(g2.shape), vl[n].reshape(g2.shape),
                                   f"adamw_{n}", dep)
        out_g.append(go.reshape(shape))
        out_d.append(d.reshape(shape))
        out_m.append(nm.reshape(shape))
        out_v.append(nv.reshape(shape))
    return (loss, dx0.reshape(x.shape), *out_g, *out_d, *out_m, *out_v)
```

```python
import functools

import jax
import jax.numpy as jnp
from jax import lax
from jax.experimental import pallas as pl
from jax.experimental.pallas import tpu as pltpu
from jax.experimental.pallas import tpu_sc as plsc

F32 = jnp.float32
BF16 = jnp.bfloat16
MESH = pl.DeviceIdType.MESH

RMS_EPS = 1e-6
CHUNK = 64
POOL_WINDOWS = (2, 4, 8, 16)
POOL_HALO = 16
N_HEADS = 4
GATE_TEMP = 16.0
ADAM_LR, ADAM_B1, ADAM_B2, ADAM_EPS, ADAM_WD, ADAM_STEP = 0.001, 0.9, 0.999, 1e-08, 0.01, 10
N_SHARDS = 4
LANES = 128
MXU_COLS = 256
TOKENS_PER_STEP = 2048
VMEM_LIMIT = 58 * 1024 * 1024

ANY = pl.BlockSpec(memory_space=pl.ANY)
HBM = pl.BlockSpec(memory_space=pltpu.HBM)
SEM = pl.BlockSpec(memory_space=pltpu.SEMAPHORE)
EFFECT = pltpu.SideEffectType.DATAFLOW_SIDE_EFFECTING


def _params(**kw):
    return pltpu.CompilerParams(vmem_limit_bytes=VMEM_LIMIT, **kw)


def _tile(n, want):
    for unit in (LANES, 8):
        t = (min(want, n) // unit) * unit
        while t >= unit:
            if n % t == 0:
                return t
            t -= unit
    return n


def _dot(a, b, dims):
    return lax.dot_general(a, b, (dims, ((), ())), preferred_element_type=F32)


def _nn(a, b):
    return _dot(a, b, ((1,), (0,)))


def _nt(a, b):
    return _dot(a, b, ((1,), (1,)))


def _tn(a, b):
    return _dot(a, b, ((0,), (0,)))


def _sigmoid(x):
    return 1.0 / (1.0 + jnp.exp(-x))


def _matmul(a, b, *, mode, name, out_dtype, tm=512, tn=2048, tk=2048, res=None, scale=1.0, b_groups=False, out_groups=0,
            dep=(), cols_outer=False):
    if mode == "tn":
        K, M = a.shape
    else:
        M, K = a.shape
    if mode == "nn":
        if b_groups:
            G, _, Nj = b.shape
            N = G * Nj
        else:
            N = b.shape[1]
    elif mode == "nt":
        if b_groups:
            G, N, Kj = b.shape
            assert G * Kj == K
        else:
            N = b.shape[0]
    else:
        N = b.shape[1]
    tm = _tile(M, tm)
    if mode == "nn" and b_groups:
        tn = _tile(Nj, tn)
    elif out_groups:
        tn = _tile(N // out_groups, tn)
    else:
        tn = _tile(N, tn)
    if mode == "nt" and b_groups:
        tk = _tile(Kj, tk)
    else:
        tk = _tile(K, tk)
    nk = K // tk
    grid = (M // tm, N // tn, nk)

    if mode == "tn":
        a_spec = pl.BlockSpec((tk, tm), lambda i, j, k: (k, i))
        b_spec = pl.BlockSpec((tk, tn), lambda i, j, k: (k, j))
        dims = ((0,), (0,))
    elif mode == "nn":
        a_spec = pl.BlockSpec((tm, tk), lambda i, j, k: (i, k))
        if b_groups:
            npj = Nj // tn
            b_spec = pl.BlockSpec((None, tk, tn), lambda i, j, k: (j // npj, k, j % npj))
        else:
            b_spec = pl.BlockSpec((tk, tn), lambda i, j, k: (k, j))
        dims = ((1,), (0,))
    else:
        a_spec = pl.BlockSpec((tm, tk), lambda i, j, k: (i, k))
        if b_groups:
            kpj = Kj // tk
            b_spec = pl.BlockSpec((None, tn, tk), lambda i, j, k: (k // kpj, j, k % kpj))
        else:
            b_spec = pl.BlockSpec((tn, tk), lambda i, j, k: (j, k))
        dims = ((1,), (1,))
    if out_groups:
        npj = (N // out_groups) // tn
        o_spec = pl.BlockSpec((None, tm, tn), lambda i, j, k: (j // npj, i, j % npj))
        out_shape = jax.ShapeDtypeStruct((out_groups, M, N // out_groups), out_dtype)
    else:
        o_spec = pl.BlockSpec((tm, tn), lambda i, j, k: (i, j))
        out_shape = jax.ShapeDtypeStruct((M, N), out_dtype)
    in_specs = [a_spec, b_spec]
    operands = [a, b]
    if res is not None:
        in_specs.append(pl.BlockSpec((tm, tn), lambda i, j, k: (i, j)))
        operands.append(res)
    has_res = res is not None
    n_dep = len(dep)
    for d in dep:
        in_specs.append(pl.BlockSpec(d.shape, lambda i, j, k: (0, 0)))
        operands.append(d)
    if cols_outer:
        def swapped(spec):
            return pl.BlockSpec(spec.block_shape, lambda j, i, k, f=spec.index_map: f(i, j, k))
        in_specs = [swapped(s) for s in in_specs]
        o_spec = swapped(o_spec)
        grid = (grid[1], grid[0], grid[2])

    def body(*refs):
        if has_res:
            a_ref, b_ref, r_ref = refs[:3]
        else:
            a_ref, b_ref = refs[:2]
            r_ref = None
        o_ref = refs[2 + has_res + n_dep]

        def finish(acc):
            if scale != 1.0:
                acc = acc * scale
            if r_ref is not None:
                acc = r_ref[...] + acc
            o_ref[...] = acc.astype(o_ref.dtype)

        part = _dot(a_ref[...], b_ref[...], dims)
        if nk == 1:
            finish(part)
        else:
            acc_ref = o_ref if in_place else refs[-1]
            k = pl.program_id(2)

            @pl.when(k == 0)
            def _():
                acc_ref[...] = part

            @pl.when(k > 0)
            def _():
                acc_ref[...] += part

            if not in_place:
                @pl.when(k == nk - 1)
                def _():
                    finish(acc_ref[...])

    in_place = out_dtype == F32 and res is None and scale == 1.0
    scratch = [] if nk == 1 or in_place else [pltpu.VMEM((tm, tn), F32)]
    return pl.pallas_call(body, name=name, grid=grid, in_specs=in_specs, out_specs=o_spec, out_shape=out_shape,
                          scratch_shapes=scratch, compiler_params=_params())(*operands)


def _rms_fwd(x, gain, name, tm=512, dep=None):
    S, D = x.shape
    tm = _tile(S, tm)

    def body(x_ref, g_ref, *rest):
        o_ref = rest[-1]
        xv = x_ref[...]
        r = lax.rsqrt(jnp.mean(xv * xv, axis=-1, keepdims=True) + RMS_EPS)
        o_ref[...] = (xv * r * g_ref[...]).astype(o_ref.dtype)

    in_specs = [pl.BlockSpec((tm, D), lambda i: (i, 0)), pl.BlockSpec((1, D), lambda i: (0, 0))]
    operands = [x, gain]
    if dep is not None:
        in_specs.append(pl.BlockSpec(dep.shape, lambda i: (0, 0)))
        operands.append(dep)
    return pl.pallas_call(body, name=name, grid=(S // tm,), in_specs=in_specs,
                          out_specs=pl.BlockSpec((tm, D), lambda i: (i, 0)),
                          out_shape=jax.ShapeDtypeStruct((S, D), BF16), compiler_params=_params())(*operands)


def _rms_bwd(x, gain, dh, dres, name, lowp=None, tm=512):
    half = lowp is not None
    S, D = x.shape
    tm = _tile(S, tm)
    has_res = dres is not None

    def body(*refs):
        if has_res:
            x_ref, g_ref, dh_ref, dr_ref = refs[:4]
            outs = refs[4:]
        else:
            x_ref, g_ref, dh_ref = refs[:3]
            dr_ref = None
            outs = refs[3:]
        dx_ref, dg_ref = outs[0], outs[-1]
        xv = x_ref[...]
        dhv = dh_ref[...].astype(F32)
        r = lax.rsqrt(jnp.mean(xv * xv, axis=-1, keepdims=True) + RMS_EPS)
        gy = dhv * g_ref[...]
        dx = r * gy - xv * (r * r * r) * jnp.mean(gy * xv, axis=-1, keepdims=True)
        if dr_ref is not None:
            dx = dx + dr_ref[...]
        dx_ref[...] = dx
        if half:
            outs[1][...] = (dx if lowp == 1.0 else lowp * dx).astype(BF16)
        part = jnp.sum(dhv * xv * r, axis=0, keepdims=True)

        @pl.when(pl.program_id(0) == 0)
        def _():
            dg_ref[...] = part

        @pl.when(pl.program_id(0) > 0)
        def _():
            dg_ref[...] += part

    row = pl.BlockSpec((tm, D), lambda i: (i, 0))
    vec = pl.BlockSpec((1, D), lambda i: (0, 0))
    in_specs = [row, vec, row] + ([row] if has_res else [])
    operands = [x, gain, dh] + ([dres] if has_res else [])
    out_specs = [row] + ([row] if half else []) + [vec]
    out_shape = [jax.ShapeDtypeStruct((S, D), F32)] + ([jax.ShapeDtypeStruct((S, D), BF16)] if half else []) + [
        jax.ShapeDtypeStruct((1, D), F32)]
    return pl.pallas_call(body, name=name, grid=(S // tm,), in_specs=in_specs, out_specs=out_specs, out_shape=out_shape,
                          compiler_params=_params())(*operands)


def _loss_head(x, gain, target, name, tm=512):
    S, D = x.shape
    tm = _tile(S, tm)

    def body(x_ref, g_ref, t_ref, sq_ref, dx_ref, dxh_ref, dg_ref):
        xv = x_ref[...]
        r = lax.rsqrt(jnp.mean(xv * xv, axis=-1, keepdims=True) + RMS_EPS)
        xn = xv * r
        err = xn * g_ref[...] - t_ref[...]
        dout = err * (1.0 / D)
        gy = dout * g_ref[...]
        dx = r * gy - xv * (r * r * r) * jnp.mean(gy * xv, axis=-1, keepdims=True)
        dx_ref[...] = dx
        dxh_ref[...] = (0.5 * dx).astype(BF16)
        sq = jnp.sum(err * err, axis=0, keepdims=True)
        dg = jnp.sum(dout * xn, axis=0, keepdims=True)

        @pl.when(pl.program_id(0) == 0)
        def _():
            sq_ref[...] = sq
            dg_ref[...] = dg

        @pl.when(pl.program_id(0) > 0)
        def _():
            sq_ref[...] += sq
            dg_ref[...] += dg

    row = pl.BlockSpec((tm, D), lambda i: (i, 0))
    vec = pl.BlockSpec((1, D), lambda i: (0, 0))
    return pl.pallas_call(body, name=name, grid=(S // tm,), in_specs=[row, vec, row], out_specs=[vec, row, row, vec],
                          out_shape=[jax.ShapeDtypeStruct((1, D), F32), jax.ShapeDtypeStruct((S, D), F32),
                                     jax.ShapeDtypeStruct((S, D), BF16), jax.ShapeDtypeStruct((1, D), F32)],
                          compiler_params=_params())(x, gain, target)


def _cast(x, dtype, name, scale=1.0, tm=256):
    S, D = x.shape
    tm = _tile(S, tm)

    def body(x_ref, o_ref):
        o_ref[...] = (x_ref[...] * scale).astype(o_ref.dtype)

    row = pl.BlockSpec((tm, D), lambda i: (i, 0))
    return pl.pallas_call(body, name=name, grid=(S // tm,), in_specs=[row], out_specs=row,
                          out_shape=jax.ShapeDtypeStruct((S, D), dtype), compiler_params=_params())(x)


def _ffn_up(h, wg, wu, name, tm=512):
    S, D = h.shape
    G, _, Fj = wg.shape
    tm = _tile(S, tm)

    def body(h_ref, wg_ref, wu_ref, ga_ref, gb_ref, hid_ref):
        hv = h_ref[...]
        a = _nn(hv, wg_ref[...])
        b = _nn(hv, wu_ref[...])
        s = _sigmoid(a)
        silu = a * s
        ga_ref[...] = (b * (s * (1.0 + a * (1.0 - s)))).astype(BF16)
        gb_ref[...] = silu.astype(BF16)
        hid_ref[...] = (silu * b).astype(BF16)

    w_spec = pl.BlockSpec((None, D, Fj), lambda g, i: (g, 0, 0))
    o_spec = pl.BlockSpec((tm, Fj), lambda g, i: (i, g))
    out = jax.ShapeDtypeStruct((S, G * Fj), BF16)
    return pl.pallas_call(body, name=name, grid=(G, S // tm),
                          in_specs=[pl.BlockSpec((tm, D), lambda g, i: (i, 0)), w_spec, w_spec],
                          out_specs=[o_spec, o_spec, o_spec], out_shape=[out, out, out], compiler_params=_params())(h, wg, wu)


def _ffn_dact(dxh, wd, ga, gb, name, tm=512):
    S, D = dxh.shape
    G, Fj, _ = wd.shape
    tm = _tile(S, tm)

    def body(dx_ref, wd_ref, ga_ref, gb_ref, da_ref, db_ref):
        dhid = _nt(dx_ref[...], wd_ref[...])
        da_ref[...] = (dhid * ga_ref[...].astype(F32)).astype(BF16)
        db_ref[...] = (dhid * gb_ref[...].astype(F32)).astype(BF16)

    blk = pl.BlockSpec((tm, Fj), lambda g, i: (i, g))
    out = jax.ShapeDtypeStruct((S, G * Fj), BF16)
    return pl.pallas_call(body, name=name, grid=(G, S // tm),
                          in_specs=[pl.BlockSpec((tm, D), lambda g, i: (i, 0)),
                                    pl.BlockSpec((None, Fj, D), lambda g, i: (g, 0, 0)), blk, blk],
                          out_specs=[blk, blk], out_shape=[out, out], compiler_params=_params())(dxh, wd, ga, gb)


def _ffn_dh(da, db, wg, wu, name, dep=(), tm=1024, rows=512):
    S = da.shape[0]
    G, D, Fj = wg.shape
    tm = _tile(S, tm)
    rows = _tile(tm, rows)

    def body(da_ref, db_ref, wg_ref, wu_ref, *rest):
        o_ref = rest[-1]
        for r0 in range(0, tm, rows):
            rs = slice(r0, r0 + rows)
            part = _nt(da_ref[rs, :], wg_ref[...]) + _nt(db_ref[rs, :], wu_ref[...])

            @pl.when(pl.program_id(1) == 0)
            def _():
                o_ref[rs, :] = part

            @pl.when(pl.program_id(1) > 0)
            def _():
                o_ref[rs, :] += part

    act = pl.BlockSpec((tm, Fj), lambda i, g: (i, g))
    w_spec = pl.BlockSpec((None, D, Fj), lambda i, g: (g, 0, 0))
    in_specs = [act, act, w_spec, w_spec] + [pl.BlockSpec(d.shape, lambda i, g: (0, 0)) for d in dep]
    return pl.pallas_call(body, name=name, grid=(S // tm, G), in_specs=in_specs,
                          out_specs=pl.BlockSpec((tm, D), lambda i, g: (i, 0)),
                          out_shape=jax.ShapeDtypeStruct((S, D), F32), compiler_params=_params())(da, db, wg, wu, *dep)


def _pool_fwd(proj, pool_w, pool_scale, name, tm=512):
    S = proj.shape[0]
    NG, C, _ = pool_w.shape
    DP = NG * C
    tm = _tile(S, tm)
    hb = tm // POOL_HALO
    n_ext = tm + POOL_HALO

    def body(u_ref, halo_ref, w_ref, sc_ref, y_ref, d_ref):
        i = pl.program_id(0)
        t = lax.broadcasted_iota(jnp.int32, (tm, 1), 0) + i * tm
        for g, win in enumerate(POOL_WINDOWS):
            cols = slice(g * C, (g + 1) * C)
            ug = u_ref[:, cols]
            halo = jnp.where(i > 0, halo_ref[:, cols], 0.0)
            acc = jnp.concatenate([halo, ug], axis=0)
            step = 1
            while step < win:
                acc = acc + pltpu.roll(acc, step, 0)
                step *= 2
            count = jnp.minimum(t + 1, win).astype(F32)
            d = (acc[POOL_HALO:, :] / count - ug).astype(BF16)
            d_ref[:, cols] = d
            y_ref[:, cols] = (_nn(d, w_ref[g]) * sc_ref[:, cols]).astype(BF16)

    del n_ext
    return pl.pallas_call(
        body, name=name, grid=(S // tm,),
        in_specs=[pl.BlockSpec((tm, DP), lambda i: (i, 0)),
                  pl.BlockSpec((POOL_HALO, DP), lambda i: (jnp.maximum(i * hb - 1, 0), 0)),
                  pl.BlockSpec((NG, C, C), lambda i: (0, 0, 0)), pl.BlockSpec((1, DP), lambda i: (0, 0))],
        out_specs=[pl.BlockSpec((tm, DP), lambda i: (i, 0)), pl.BlockSpec((tm, DP), lambda i: (i, 0))],
        out_shape=[jax.ShapeDtypeStruct((S, DP), BF16), jax.ShapeDtypeStruct((S, DP), BF16)],
        compiler_params=_params())(proj, proj, pool_w, pool_scale)


def _pool_bwd(dymix, d, pool_w, pool_scale, name, tm=512):
    S = dymix.shape[0]
    NG, C, _ = pool_w.shape
    DP = NG * C
    tm = _tile(S, tm)
    hb = tm // POOL_HALO
    nb = S // tm
    n_ext = tm + POOL_HALO
    last_halo = S // POOL_HALO - 1

    def body(dy_ref, halo_ref, d_ref, w_ref, sc_ref, du_ref, dw_ref, dsc_ref):
        i = pl.program_id(0)
        t = lax.broadcasted_iota(jnp.int32, (n_ext, 1), 0) + i * tm
        for g, win in enumerate(POOL_WINDOWS):
            cols = slice(g * C, (g + 1) * C)
            dy = dy_ref[:, cols]
            halo = jnp.where(i < nb - 1, halo_ref[:, cols], 0.0)
            sc = sc_ref[:, cols]
            dv = d_ref[:, cols]
            e_ext = (jnp.concatenate([dy, halo], axis=0) * sc).astype(BF16)
            dd = _nt(e_ext, w_ref[g])
            count = jnp.minimum(t + 1, win).astype(F32)
            acc = dd / count
            step = 1
            while step < win:
                acc = acc + pltpu.roll(acc, n_ext - step, 0)
                step *= 2
            du_ref[:, cols] = (acc[:tm, :] - dd[:tm, :]).astype(BF16)
            dw = _tn(dv, e_ext[:tm, :])
            dsc = jnp.sum(dy * _nn(dv, w_ref[g]), axis=0, keepdims=True)

            @pl.when(i == 0)
            def _():
                dw_ref[g] = dw
                dsc_ref[:, cols] = dsc

            @pl.when(i > 0)
            def _():
                dw_ref[g] += dw
                dsc_ref[:, cols] += dsc

    return pl.pallas_call(
        body, name=name, grid=(nb,),
        in_specs=[pl.BlockSpec((tm, DP), lambda i: (i, 0)),
                  pl.BlockSpec((POOL_HALO, DP), lambda i: (jnp.minimum((i + 1) * hb, last_halo), 0)),
                  pl.BlockSpec((tm, DP), lambda i: (i, 0)),
                  pl.BlockSpec((NG, C, C), lambda i: (0, 0, 0)), pl.BlockSpec((1, DP), lambda i: (0, 0))],
        out_specs=[pl.BlockSpec((tm, DP), lambda i: (i, 0)), pl.BlockSpec((NG, C, C), lambda i: (0, 0, 0)),
                   pl.BlockSpec((1, DP), lambda i: (0, 0))],
        out_shape=[jax.ShapeDtypeStruct((S, DP), BF16), jax.ShapeDtypeStruct((NG, C, C), F32),
                   jax.ShapeDtypeStruct((1, DP), F32)],
        compiler_params=_params())(dymix, dymix, d, pool_w, pool_scale)


def _chunk_scan(v, rows, reverse):
    n = v.shape[0]
    step = 1
    while step < CHUNK:
        if reverse:
            v = v + jnp.where(rows < CHUNK - step, pltpu.roll(v, n - step, 0), 0.0)
        else:
            v = v + jnp.where(rows >= step, pltpu.roll(v, step, 0), 0.0)
        step *= 2
    return v


def _log_decay(alr, w_a2, b_a):
    z = _nn(alr.astype(BF16), w_a2) + b_a
    la = (jnp.minimum(z, 0.0) - jnp.log(1.0 + jnp.exp(-jnp.abs(z)))) * (1.0 / GATE_TEMP)
    return z, la


def _gla_specs(DP, DKT, DV, tb, bmap):
    return [pl.BlockSpec((tb, DKT), lambda i: (bmap(i), DP // DKT)),
            pl.BlockSpec((tb, DKT), lambda i: (bmap(i), DP // DKT + 1)),
            pl.BlockSpec((tb, DV), lambda i: (bmap(i), (DP + 2 * DKT) // DV)),
            pl.BlockSpec((tb, DV), lambda i: (bmap(i), (DP + 2 * DKT) // DV + 1)),
            pl.BlockSpec((tb, LANES), lambda i: (bmap(i), (DP + 2 * DKT + 2 * DV) // LANES))]


def _gla_fwd(proj, y_pool, w_a2, b_a, head_norm, name, tb=512):
    S = proj.shape[0]
    DP = y_pool.shape[1]
    DKT = b_a.shape[1]
    DV = head_norm.shape[1]
    dk, dv = DKT // N_HEADS, DV // N_HEADS
    tb = _tile(S, tb)
    ncb = tb // CHUNK
    qscale = dk ** -0.5

    def body(q_ref, k_ref, v_ref, g_ref, alr_ref, yp_ref, wa_ref, ba_ref, hn_ref, y_ref, st_out_ref, st_ref, kdec_ref,
             gam_ref):
        @pl.when(pl.program_id(0) == 0)
        def _():
            st_ref[...] = jnp.zeros_like(st_ref)

        y_ref[:, :DP] = yp_ref[...]

        rows = lax.broadcasted_iota(jnp.int32, (tb, 1), 0) % CHUNK
        _, la = _log_decay(alr_ref[...], wa_ref[...], ba_ref[...])
        tail = _chunk_scan(la, rows, True)
        kdec_ref[...] = k_ref[...] * jnp.exp(tail - la)
        gam_ref[...] = jnp.exp(tail)

        def chunk(c, carry):
            r0 = pl.multiple_of(c * CHUNK, CHUNK)
            rs = pl.ds(r0, CHUNK)
            gam = gam_ref[pl.ds(r0, 1), :]
            heads = range(N_HEADS)
            kcs = [slice(h * dk, (h + 1) * dk) for h in heads]
            vcs = [slice(h * dv, (h + 1) * dv) for h in heads]
            upd = [_tn(v_ref[rs, vcs[h]].astype(BF16), kdec_ref[rs, kcs[h]].astype(BF16)) for h in heads]
            st = [st_ref[h] * gam[:, kcs[h]] + upd[h] for h in heads]
            o = [_nt((q_ref[rs, kcs[h]] * qscale).astype(BF16), st[h].astype(BF16)) for h in heads]
            for h in heads:
                st_ref[h] = st[h]
                st_out_ref[c, h] = st[h]
                r = lax.rsqrt(jnp.mean(o[h] * o[h], axis=-1, keepdims=True) + RMS_EPS)
                gv = g_ref[rs, vcs[h]]
                y_ref[rs, DP + h * dv:DP + (h + 1) * dv] = (o[h] * r * hn_ref[:, vcs[h]] * (gv * _sigmoid(gv))).astype(BF16)
            return carry

        lax.fori_loop(0, ncb, chunk, 0, unroll=4)

    full = lambda shape: pl.BlockSpec(shape, lambda i: (0,) * len(shape))
    return pl.pallas_call(
        body, name=name, grid=(S // tb,),
        in_specs=_gla_specs(DP, DKT, DV, tb, lambda i: i) + [pl.BlockSpec((tb, DP), lambda i: (i, 0)),
                                                            full((LANES, DKT)), full((1, DKT)), full((1, DV))],
        out_specs=[pl.BlockSpec((tb, DP + DV), lambda i: (i, 0)),
                   pl.BlockSpec((ncb, N_HEADS, dv, dk), lambda i: (i, 0, 0, 0))],
        out_shape=[jax.ShapeDtypeStruct((S, DP + DV), BF16), jax.ShapeDtypeStruct((S // CHUNK, N_HEADS, dv, dk), F32)],
        scratch_shapes=[pltpu.VMEM((N_HEADS, dv, dk), F32), pltpu.VMEM((tb, DKT), F32), pltpu.VMEM((tb, DKT), F32)],
        compiler_params=_params())(proj, proj, proj, proj, proj, y_pool, w_a2, b_a, head_norm)


def _gla_bwd(proj, states, dymix, du, w_a2, b_a, head_norm, name, tb=512):
    S = proj.shape[0]
    DP = du.shape[1]
    DKT = b_a.shape[1]
    DV = head_norm.shape[1]
    dk, dv = DKT // N_HEADS, DV // N_HEADS
    tb = _tile(S, tb)
    ncb = tb // CHUNK
    nb = S // tb
    qscale = dk ** -0.5
    rev = lambda i: nb - 1 - i

    q0, k0, v0, g0, a0 = DP, DP + DKT, DP + 2 * DKT, DP + 2 * DKT + DV, DP + 2 * DKT + 2 * DV

    def body(q_ref, k_ref, v_ref, g_ref, alr_ref, st_blk_ref, st_prev_ref, dy_ref, du_ref, wa_ref, ba_ref, hn_ref,
             dp_ref, dwa_ref, dba_ref, dhn_ref,
             dst_ref, kdec_ref, dec_ref, gam_ref, e_ref, dla_ref, dhn_acc_ref):
        i = pl.program_id(0)
        blk = rev(i)
        dp_ref[:, :DP] = du_ref[...]

        @pl.when(i == 0)
        def _():
            dst_ref[...] = jnp.zeros_like(dst_ref)

        dhn_acc_ref[...] = jnp.zeros_like(dhn_acc_ref)
        rows = lax.broadcasted_iota(jnp.int32, (tb, 1), 0) % CHUNK
        z, la = _log_decay(alr_ref[...], wa_ref[...], ba_ref[...])
        tail = _chunk_scan(la, rows, True)
        dec_ref[...] = jnp.exp(tail - la)
        kdec_ref[...] = k_ref[...] * dec_ref[...]
        gam_ref[...] = jnp.exp(tail)

        def chunk(cc, carry):
            c = ncb - 1 - cc
            r0 = pl.multiple_of(c * CHUNK, CHUNK)
            rs = pl.ds(r0, CHUNK)
            gam = gam_ref[pl.ds(r0, 1), :]
            first = jnp.logical_and(blk == 0, c == 0)
            heads = range(N_HEADS)
            kcs = [slice(h * dk, (h + 1) * dk) for h in heads]
            vcs = [slice(h * dv, (h + 1) * dv) for h in heads]
            qs = [(q_ref[rs, kcs[h]] * qscale).astype(BF16) for h in heads]
            stb = [st_blk_ref[c, h].astype(BF16) for h in heads]
            o = [_nt(qs[h], stb[h]) for h in heads]
            do = []
            for h in heads:
                oh = o[h]
                r = lax.rsqrt(jnp.mean(oh * oh, axis=-1, keepdims=True) + RMS_EPS)
                gv = g_ref[rs, vcs[h]]
                sg = _sigmoid(gv)
                dy = dy_ref[rs, vcs[h]]
                hn = hn_ref[:, vcs[h]]
                on = oh * r
                dp_ref[rs, g0 + h * dv:g0 + (h + 1) * dv] = (dy * on * hn * (sg * (1.0 + gv * (1.0 - sg)))).astype(BF16)
                don = dy * (gv * sg)
                dhn_acc_ref[:, vcs[h]] += jnp.sum(don * on, axis=0, keepdims=True)
                dn = don * hn
                do.append((r * dn - oh * (r * r * r) * jnp.mean(dn * oh, axis=-1, keepdims=True)).astype(BF16))
            dqs = [_nn(do[h], stb[h]) for h in heads]
            dst = [dst_ref[h] + _tn(do[h], qs[h]) for h in heads]
            for h in heads:
                dp_ref[rs, q0 + h * dk:q0 + (h + 1) * dk] = (dqs[h] * qscale).astype(BF16)
            dstb = [dst[h].astype(BF16) for h in heads]
            dvh = [_nt(kdec_ref[rs, kcs[h]].astype(BF16), dstb[h]) for h in heads]
            dkdec = [_nn(v_ref[rs, vcs[h]].astype(BF16), dstb[h]) for h in heads]
            gdg = []
            for h in heads:
                dp_ref[rs, v0 + h * dv:v0 + (h + 1) * dv] = dvh[h].astype(BF16)
                dp_ref[rs, k0 + h * dk:k0 + (h + 1) * dk] = (dkdec[h] * dec_ref[rs, kcs[h]]).astype(BF16)
                e_ref[rs, kcs[h]] = dkdec[h] * kdec_ref[rs, kcs[h]]
                st_prev = jnp.where(c > 0, st_blk_ref[jnp.maximum(c - 1, 0), h], st_prev_ref[0, h])
                st_prev = jnp.where(first, 0.0, st_prev)
                gdg.append(jnp.sum(dst[h] * st_prev, axis=0, keepdims=True) * gam[:, kcs[h]])
                dst_ref[h] = dst[h] * gam[:, kcs[h]]
            dla_ref[rs, :] = jnp.broadcast_to(jnp.concatenate(gdg, axis=1), (CHUNK, DKT))
            return carry

        lax.fori_loop(0, ncb, chunk, 0, unroll=4)

        ev = e_ref[...]
        dla = dla_ref[...] + _chunk_scan(ev, rows, False) - ev
        dz = dla * (1.0 / GATE_TEMP) * (1.0 - _sigmoid(z))
        dzb = dz.astype(BF16)
        dp_ref[:, a0:a0 + LANES] = _nt(dzb, wa_ref[...]).astype(BF16)
        dwa = _tn(alr_ref[...].astype(BF16), dzb)
        dba = jnp.sum(dz, axis=0, keepdims=True)

        @pl.when(i == 0)
        def _():
            dwa_ref[...] = dwa
            dba_ref[...] = dba
            dhn_ref[...] = dhn_acc_ref[...]

        @pl.when(i > 0)
        def _():
            dwa_ref[...] += dwa
            dba_ref[...] += dba
            dhn_ref[...] += dhn_acc_ref[...]

    full = lambda shape: pl.BlockSpec(shape, lambda i: (0,) * len(shape))
    rowblk = lambda w: pl.BlockSpec((tb, w), lambda i: (rev(i), 0))
    return pl.pallas_call(
        body, name=name, grid=(nb,),
        in_specs=_gla_specs(DP, DKT, DV, tb, rev) + [
            pl.BlockSpec((ncb, N_HEADS, dv, dk), lambda i: (rev(i), 0, 0, 0)),
            pl.BlockSpec((1, N_HEADS, dv, dk), lambda i: (jnp.maximum(rev(i) * ncb - 1, 0), 0, 0, 0)),
            pl.BlockSpec((tb, DV), lambda i: (rev(i), DP // DV)), rowblk(DP),
            full((LANES, DKT)), full((1, DKT)), full((1, DV))],
        out_specs=[rowblk(a0 + LANES), full((LANES, DKT)), full((1, DKT)), full((1, DV))],
        out_shape=[jax.ShapeDtypeStruct((S, a0 + LANES), BF16), jax.ShapeDtypeStruct((LANES, DKT), F32),
                   jax.ShapeDtypeStruct((1, DKT), F32), jax.ShapeDtypeStruct((1, DV), F32)],
        scratch_shapes=[pltpu.VMEM((N_HEADS, dv, dk), F32)] + [pltpu.VMEM((tb, DKT), F32)] * 5 + [pltpu.VMEM((1, DV), F32)],
        compiler_params=_params())(proj, proj, proj, proj, proj, states, states, dymix, du, w_a2, b_a, head_norm)


def _xattn_fwd(q, kv, name, tm=512):
    S, D = q.shape
    M = kv.shape[0]
    hd = D // N_HEADS
    tm = _tile(S, tm)
    scale = hd ** -0.5

    def body(q_ref, k_ref, v_ref, o_ref):
        heads = range(N_HEADS)
        hcs = [slice(h * hd, (h + 1) * hd) for h in heads]
        s = [_nt(q_ref[:, hc], k_ref[:, hc]) * scale for hc in hcs]
        p = []
        for h in heads:
            e = jnp.exp(s[h] - jnp.max(s[h], axis=-1, keepdims=True))
            p.append((e / jnp.sum(e, axis=-1, keepdims=True)).astype(BF16))
        o = [_nn(p[h], v_ref[:, hcs[h]]) for h in heads]
        for h in heads:
            o_ref[:, hcs[h]] = o[h].astype(BF16)

    return pl.pallas_call(body, name=name, grid=(S // tm,),
                          in_specs=[pl.BlockSpec((tm, D), lambda i: (i, 0)), pl.BlockSpec((M, D), lambda i: (0, 0)),
                                    pl.BlockSpec((M, D), lambda i: (0, 1))],
                          out_specs=pl.BlockSpec((tm, D), lambda i: (i, 0)),
                          out_shape=jax.ShapeDtypeStruct((S, D), BF16), compiler_params=_params())(q, kv, kv)


def _xattn_bwd(q, kv, do, name, tm=512):
    S, D = q.shape
    M = kv.shape[0]
    hd = D // N_HEADS
    tm = _tile(S, tm)
    scale = hd ** -0.5

    def body(q_ref, k_ref, v_ref, do_ref, dq_ref, dkv_ref):
        first = pl.program_id(0) == 0
        heads = range(N_HEADS)
        hcs = [slice(h * hd, (h + 1) * hd) for h in heads]
        s = [_nt(q_ref[:, hc], k_ref[:, hc]) * scale for hc in hcs]
        dp = [_nt(do_ref[:, hc], v_ref[:, hc]) for hc in hcs]
        p = []
        for h in heads:
            e = jnp.exp(s[h] - jnp.max(s[h], axis=-1, keepdims=True))
            p.append(e / jnp.sum(e, axis=-1, keepdims=True))
        dvh = [_tn(p[h].astype(BF16), do_ref[:, hcs[h]]) for h in heads]
        ds = [((p[h] * (dp[h] - jnp.sum(dp[h] * p[h], axis=-1, keepdims=True))) * scale).astype(BF16) for h in heads]
        dqh = [_nn(ds[h], k_ref[:, hcs[h]]) for h in heads]
        dkh = [_tn(ds[h], q_ref[:, hcs[h]]) for h in heads]
        for h in heads:
            dq_ref[:, hcs[h]] = dqh[h].astype(BF16)

        @pl.when(first)
        def _():
            for h in heads:
                dkv_ref[:, hcs[h]] = dkh[h]
                dkv_ref[:, D + h * hd:D + (h + 1) * hd] = dvh[h]

        @pl.when(jnp.logical_not(first))
        def _():
            for h in heads:
                dkv_ref[:, hcs[h]] += dkh[h]
                dkv_ref[:, D + h * hd:D + (h + 1) * hd] += dvh[h]

    row = pl.BlockSpec((tm, D), lambda i: (i, 0))
    return pl.pallas_call(body, name=name, grid=(S // tm,),
                          in_specs=[row, pl.BlockSpec((M, D), lambda i: (0, 0)), pl.BlockSpec((M, D), lambda i: (0, 1)), row],
                          out_specs=[row, pl.BlockSpec((M, 2 * D), lambda i: (0, 0))],
                          out_shape=[jax.ShapeDtypeStruct((S, D), BF16), jax.ShapeDtypeStruct((M, 2 * D), F32)],
                          compiler_params=_params())(q, kv, kv, do)


def _local_step(x, mem, target, vec, weight, emit, dep0):
    g = {}
    pending = []
    begun = []
    summed = []
    emit_begin, emit_finish, emit_send, early_update = emit

    def behind(fn, *a, **kw):
        dep = tuple(pending)
        pending.clear()
        out = fn(*a, dep=dep, **kw)
        while summed:
            pending.append(emit_send(summed.pop(0)))
        while begun:
            name = begun.pop(0)
            token = emit_finish(name, out)
            if token is None:
                summed.append(name)
            else:
                pending.append(token)
        return out

    def mm(a, b, **kw):
        return behind(_matmul, a, b, **kw)

    def send(name, gfull):
        pending.append(emit_begin(name, gfull))
        begun.append(name)

    def ffn_fwd(xin, tag, dep):
        h = _rms_fwd(xin, vec[f"{tag}_norm"], f"{tag}_norm", dep=dep)
        ga, gb, hid = _ffn_up(h, weight(f"{tag}_w_gate", h), weight(f"{tag}_w_up", h), f"{tag}_up")
        wd = weight(f"{tag}_w_down", hid)
        G, Fj, D = wd.shape
        xo = _matmul(hid, wd.reshape(G * Fj, D), mode="nn", name=f"{tag}_down", out_dtype=F32, res=xin, scale=0.5,
                     tn=1024, tk=G * Fj, cols_outer=True)
        return xo, (h, ga, gb, hid)

    def ffn_bwd(dxh, saved, tag, kept_back=None):
        h, ga, gb, hid = saved
        wg, wu, wd = weight(f"{tag}_w_gate"), weight(f"{tag}_w_up"), weight(f"{tag}_w_down")
        G, Fj, D = wd.shape
        send(f"{tag}_w_down", mm(hid, dxh, mode="tn", name=f"{tag}_dwd", out_dtype=F32, tm=Fj, tn=1024))
        da, db = _ffn_dact(dxh, wd, ga, gb, f"{tag}_dact")
        send(f"{tag}_w_gate", mm(h, da, mode="tn", name=f"{tag}_dwg", out_dtype=F32, tm=1024, tn=Fj, tk=TOKENS_PER_STEP,
                                  out_groups=G))
        send(f"{tag}_w_up", mm(h, db, mode="tn", name=f"{tag}_dwu", out_dtype=F32, tm=1024, tn=Fj, tk=TOKENS_PER_STEP,
                                out_groups=G))
        if kept_back is not None:
            name, gfull = kept_back()
            pending.append(emit_begin(name, gfull))
            pending.append(emit_finish(name, pending[-1]))
        return behind(_ffn_dh, da, db, wg, wu, f"{tag}_dh")

    x1, ffn1_saved = ffn_fwd(x, "ffn1", dep0)
    h2 = _rms_fwd(x1, vec["mix_norm"], "mix_norm")
    w_in = weight("w_in", h2)
    proj = _matmul(h2, w_in, mode="nn", name="w_in", out_dtype=F32, tn=1408, cols_outer=True)
    pool_w, w_a2 = weight("pool_w", h2), weight("gla_w_a2", h2)
    y_pool, dpool = _pool_fwd(proj, pool_w, vec["pool_scale"], "pool_fwd")
    ymix, states = _gla_fwd(proj, y_pool, w_a2, vec["gla_b_a"], vec["gla_head_norm"], "gla_fwd")
    w_out = weight("w_out", ymix)
    x2 = _matmul(ymix, w_out, mode="nn", name="w_out", out_dtype=F32, res=x1)
    h3 = _rms_fwd(x2, vec["xattn_norm"], "xattn_norm")
    mh = _rms_fwd(mem, vec["mem_norm"], "mem_norm")
    w_q = weight("xattn_w_q", h3)
    q = _matmul(h3, w_q, mode="nn", name="xattn_q", out_dtype=BF16)
    w_kv = weight("xattn_w_kv", q)
    kv = _matmul(mh, w_kv, mode="nn", name="xattn_kv", out_dtype=BF16, b_groups=True, tn=1024)
    o = _xattn_fwd(q, kv, "xattn_fwd")
    w_o = weight("xattn_w_o", o)
    x3 = _matmul(o, w_o, mode="nn", name="xattn_o", out_dtype=F32, res=x2)
    x4, ffn2_saved = ffn_fwd(x3, "ffn2", None)
    sq, dx4, dx4h, g["final_norm"] = _loss_head(x4, vec["final_norm"], target, "loss_head")
    loss = lax.psum(0.5 * jnp.sum(sq) / x.shape[-1], ("x", "y", "c"))
    pending.append(loss.reshape(1, 1))

    dh = ffn_bwd(dx4h, ffn2_saved, "ffn2")
    dx3, dx3b, g["ffn2_norm"] = _rms_bwd(x3, vec["ffn2_norm"], dh, dx4, "ffn2_norm_bwd", lowp=1.0)
    send("xattn_w_o", mm(o, dx3b, mode="tn", name="xattn_dwo", out_dtype=F32, tm=1024, tn=1024, tk=TOKENS_PER_STEP))
    do = mm(dx3b, w_o, mode="nt", name="xattn_do", out_dtype=BF16)
    dq, dkv = _xattn_bwd(q, kv, do, "xattn_bwd")
    send("xattn_w_q", mm(h3, dq, mode="tn", name="xattn_dwq", out_dtype=F32, tm=1024, tn=1024, tk=TOKENS_PER_STEP))
    dh3 = mm(dq, w_q, mode="nt", name="xattn_dh", out_dtype=F32)
    dkvb = _cast(dkv, BF16, "dkv_cast")
    send("xattn_w_kv", mm(mh, dkvb, mode="tn", name="xattn_dwkv", out_dtype=F32, tm=1024, tn=1024, out_groups=N_SHARDS))
    dmh = mm(dkvb, w_kv, mode="nt", name="xattn_dmh", out_dtype=F32, b_groups=True, tk=1024)
    _, g["mem_norm"] = _rms_bwd(mem, vec["mem_norm"], dmh, None, "mem_norm_bwd")
    pending.append(g["mem_norm"])
    dx2, dx2b, g["xattn_norm"] = _rms_bwd(x2, vec["xattn_norm"], dh3, dx3, "xattn_norm_bwd", lowp=1.0)
    send("w_out", mm(ymix, dx2b, mode="tn", name="dw_out", out_dtype=F32, tm=1024, tn=1024, tk=TOKENS_PER_STEP))
    dymix = mm(dx2b, w_out, mode="nt", name="dymix", out_dtype=F32)
    du, dpool_w, g["pool_scale"] = _pool_bwd(dymix, dpool, pool_w, vec["pool_scale"], "pool_bwd")
    send("pool_w", dpool_w)
    dproj, dw_a2, g["gla_b_a"], g["gla_head_norm"] = _gla_bwd(
        proj, states, dymix, du, w_a2, vec["gla_b_a"], vec["gla_head_norm"], "gla_bwd")
    send("gla_w_a2", dw_a2)
    dh2 = mm(dproj, w_in, mode="nt", name="dh2", out_dtype=F32, tn=1024, tk=dproj.shape[1], cols_outer=True)
    pending.extend(early_update(dh2))
    dx1, dx1h, g["mix_norm"] = _rms_bwd(x1, vec["mix_norm"], dh2, dx2, "mix_norm_bwd", lowp=0.5)
    dh = ffn_bwd(dx1h, ffn1_saved, "ffn1", kept_back=lambda: (
        "w_in", mm(h2, dproj, mode="tn", name="dw_in", out_dtype=F32, tm=1024, tn=1408, tk=TOKENS_PER_STEP)))
    dx0, g["ffn1_norm"] = _rms_bwd(x, vec["ffn1_norm"], dh, dx1, "ffn1_norm_bwd")
    return loss, dx0, g


def _place():
    x, y, c = lax.axis_index("x"), lax.axis_index("y"), lax.axis_index("c")
    chips = [(1 - x, y), (x, 1 - y), (1 - x, 1 - y)]
    return x, y, c, chips


def _ids():
    return jnp.stack([2 * lax.axis_index("x") + lax.axis_index("y"), lax.axis_index("c")]).astype(jnp.int32)


def _hbm(a):
    return pltpu.with_memory_space_constraint(a, pltpu.HBM)


def _cast_to_slot(w2d, dtype, name, dep=None):
    R, C = w2d.shape
    tr = _tile(R, max(16, (4 << 20) // (4 * C) // 16 * 16))

    def body(i_ref, w_ref, *rest):
        rest[-1][...] = w_ref[...].astype(dtype)

    in_specs = [pl.BlockSpec((tr, C), lambda r, i: (r, 0))]
    operands = [w2d]
    if dep is not None:
        in_specs.append(pl.BlockSpec(dep.shape, lambda r, i: (0, 0)))
        operands.append(dep)
    grid_spec = pltpu.PrefetchScalarGridSpec(num_scalar_prefetch=1, grid=(R // tr,), in_specs=in_specs,
                                             out_specs=pl.BlockSpec((None, tr, C), lambda r, i: (i[0], r, 0)))
    return pl.pallas_call(body, name=name, grid_spec=grid_spec, out_shape=jax.ShapeDtypeStruct((N_SHARDS, R, C), dtype),
                          compiler_params=_params())(_ids(), *operands)


def _gather_copies(buf_ref, send_sems, recv_sems, incoming, whole):
    x, y, c, chips = _place()
    hr = buf_ref.shape[1] // 2
    copies = []
    for j, (px, py) in enumerate(chips):
        slot = 2 * px + py if incoming else 2 * x + y
        part = buf_ref.at[slot] if whole else buf_ref.at[slot, pl.ds(c * hr, hr), :]
        copies.append(pltpu.make_async_remote_copy(src_ref=part, dst_ref=part, send_sem=send_sems.at[j],
                                                   recv_sem=recv_sems.at[j], device_id=(px, py, c), device_id_type=MESH))
    return copies


def _gather_start(buf, name, whole):
    def body(b_ref, send_sems, recv_sems, b_thru, token):
        for cp in _gather_copies(b_ref, send_sems, recv_sems, False, whole):
            cp.start()
        token[...] = jnp.zeros_like(token)

    return pl.pallas_call(
        body, name=name,
        out_shape=(pltpu.SemaphoreType.DMA((3,)), pltpu.SemaphoreType.DMA((3,)), pltpu.HBM(buf.shape, buf.dtype),
                   jax.ShapeDtypeStruct((8, LANES), F32)),
        in_specs=(HBM,), out_specs=(SEM, SEM, HBM, pl.BlockSpec(memory_space=pltpu.VMEM)), input_output_aliases={0: 2},
        compiler_params=pltpu.CompilerParams(has_side_effects=EFFECT))(_hbm(buf))


def _gather_wait(send_sems, recv_sems, buf_thru, after, name, whole):
    def body(b_ref, send_sems, recv_sems, after_ref, b_out):
        for cp in _gather_copies(b_ref, send_sems, recv_sems, False, whole):
            cp.wait_send()
        for cp in _gather_copies(b_ref, send_sems, recv_sems, True, whole):
            cp.wait_recv()

    return pl.pallas_call(
        body, name=name, out_shape=pltpu.HBM(buf_thru.shape, buf_thru.dtype),
        in_specs=(HBM, SEM, SEM, ANY), out_specs=HBM, input_output_aliases={0: 0},
        compiler_params=pltpu.CompilerParams(has_side_effects=EFFECT))(buf_thru, send_sems, recv_sems, after)


def _gather_forward(buf, name):
    G, R, C = buf.shape
    hr = R // 2

    def body(b_ref, o_ref, send_sems, recv_sems):
        x, y, c, chips = _place()
        copies = []
        for j, (px, py) in enumerate(chips):
            half = o_ref.at[2 * px + py, pl.ds(c * hr, hr), :]
            copies.append(pltpu.make_async_remote_copy(src_ref=half, dst_ref=half, send_sem=send_sems.at[j],
                                                       recv_sem=recv_sems.at[j], device_id=(x, y, 1 - c),
                                                       device_id_type=MESH))
        for cp in copies:
            cp.start()
        for j, (px, py) in enumerate(chips):
            half = o_ref.at[2 * px + py, pl.ds((1 - c) * hr, hr), :]
            pltpu.make_async_remote_copy(src_ref=half, dst_ref=half, send_sem=send_sems.at[j], recv_sem=recv_sems.at[j],
                                         device_id=(x, y, 1 - c), device_id_type=MESH).wait_recv()
        for cp in copies:
            cp.wait_send()

    return pl.pallas_call(body, name=name, in_specs=[ANY], out_specs=ANY, out_shape=jax.ShapeDtypeStruct(buf.shape, buf.dtype),
                          input_output_aliases={0: 0},
                          scratch_shapes=[pltpu.SemaphoreType.DMA((3,)), pltpu.SemaphoreType.DMA((3,))])(buf)


def _pair_copy(g_ref, land_ref, send_sem, recv_sem):
    x, y, c, _ = _place()
    hr = g_ref.shape[1] // 2
    return pltpu.make_async_remote_copy(src_ref=g_ref.at[:, pl.ds((1 - c) * hr, hr), :], dst_ref=land_ref,
                                        send_sem=send_sem, recv_sem=recv_sem, device_id=(x, y, 1 - c), device_id_type=MESH)


def _pair_start(gfull, name):
    G, R, C = gfull.shape

    def body(g_ref, land_ref, send_sem, recv_sem, g_thru, land_thru, token):
        _pair_copy(g_ref, land_ref, send_sem, recv_sem).start()
        token[...] = jnp.zeros_like(token)

    return pl.pallas_call(
        body, name=name,
        out_shape=(pltpu.SemaphoreType.DMA(()), pltpu.SemaphoreType.DMA(()), pltpu.HBM(gfull.shape, F32),
                   pltpu.HBM((G, R // 2, C), F32), jax.ShapeDtypeStruct((8, LANES), F32)),
        in_specs=(HBM, HBM), out_specs=(SEM, SEM, HBM, HBM, pl.BlockSpec(memory_space=pltpu.VMEM)),
        input_output_aliases={0: 2, 1: 3},
        compiler_params=pltpu.CompilerParams(has_side_effects=EFFECT))(_hbm(gfull), _hbm(lax.empty((G, R // 2, C), F32)))


def _pair_wait(send_sem, recv_sem, g_thru, land_thru, after, name):
    def body(g_ref, land_ref, send_sem, recv_sem, after_ref, g_out, land_out):
        cp = _pair_copy(g_ref, land_ref, send_sem, recv_sem)
        cp.wait_send()
        cp.wait_recv()

    return pl.pallas_call(
        body, name=name, out_shape=(pltpu.HBM(g_thru.shape, F32), pltpu.HBM(land_thru.shape, F32)),
        in_specs=(HBM, HBM, SEM, SEM, ANY), out_specs=(HBM, HBM), input_output_aliases={0: 0, 1: 1},
        compiler_params=pltpu.CompilerParams(has_side_effects=EFFECT))(g_thru, land_thru, send_sem, recv_sem, after)


def _pair_add(gfull, other, name):
    G, R, C = gfull.shape
    hr = R // 2
    tr = _tile(hr, max(8, (2 * 1024 * 1024) // (4 * C) // 8 * 8))
    nr = hr // tr
    c = lax.axis_index("c")
    cidx = jnp.reshape(c, (1,)).astype(jnp.int32)

    def body(c_ref, a_ref, b_ref, o_ref):
        o_ref[...] = a_ref[...] + b_ref[...]

    grid_spec = pltpu.PrefetchScalarGridSpec(
        num_scalar_prefetch=1, grid=(G, nr),
        in_specs=[pl.BlockSpec((None, tr, C), lambda g, r, cr: (g, cr[0] * nr + r, 0)),
                  pl.BlockSpec((None, tr, C), lambda g, r, cr: (g, r, 0))],
        out_specs=pl.BlockSpec((None, tr, C), lambda g, r, cr: (g, r, 0)))
    return pl.pallas_call(body, name=name, grid_spec=grid_spec, out_shape=jax.ShapeDtypeStruct((G, hr, C), F32),
                          compiler_params=_params())(cidx, gfull, other)


def _chip_copies(p_ref, land_ref, send_sems, recv_sems, incoming):
    x, y, c, chips = _place()
    me = 2 * x + y
    copies = []
    for j, (px, py) in enumerate(chips):
        dst = land_ref.at[2 * px + py] if incoming else land_ref.at[me]
        copies.append(pltpu.make_async_remote_copy(src_ref=p_ref.at[2 * px + py], dst_ref=dst, send_sem=send_sems.at[j],
                                                   recv_sem=recv_sems.at[j], device_id=(px, py, c), device_id_type=MESH))
    return copies


def _chip_start(part, name):
    def body(p_ref, land_ref, send_sems, recv_sems, p_thru, land_thru, token):
        for cp in _chip_copies(p_ref, land_ref, send_sems, recv_sems, False):
            cp.start()
        token[...] = jnp.zeros_like(token)

    return pl.pallas_call(
        body, name=name,
        out_shape=(pltpu.SemaphoreType.DMA((3,)), pltpu.SemaphoreType.DMA((3,)), pltpu.HBM(part.shape, F32),
                   pltpu.HBM(part.shape, F32), jax.ShapeDtypeStruct((8, LANES), F32)),
        in_specs=(HBM, HBM), out_specs=(SEM, SEM, HBM, HBM, pl.BlockSpec(memory_space=pltpu.VMEM)),
        input_output_aliases={0: 2, 1: 3},
        compiler_params=pltpu.CompilerParams(has_side_effects=EFFECT))(_hbm(part), _hbm(lax.empty(part.shape, F32)))


def _chip_wait(send_sems, recv_sems, p_thru, land_thru, after, name):
    def body(p_ref, land_ref, send_sems, recv_sems, after_ref, p_out, land_out):
        for cp in _chip_copies(p_ref, land_ref, send_sems, recv_sems, False):
            cp.wait_send()
        for cp in _chip_copies(p_ref, land_ref, send_sems, recv_sems, True):
            cp.wait_recv()

    return pl.pallas_call(
        body, name=name, out_shape=(pltpu.HBM(p_thru.shape, F32), pltpu.HBM(p_thru.shape, F32)),
        in_specs=(HBM, HBM, SEM, SEM, ANY), out_specs=(HBM, HBM), input_output_aliases={0: 0, 1: 1},
        compiler_params=pltpu.CompilerParams(has_side_effects=EFFECT))(p_thru, land_thru, send_sems, recv_sems, after)


def _chip_sum(part, slots, name):
    G, R2, C = part.shape
    tr = _tile(R2, max(8, (1 << 20) // (4 * C) // 8 * 8))
    nr = R2 // tr

    def body(i_ref, p_ref, *rest):
        o_ref = rest[-1]
        acc = None
        for u in range(G):
            val = jnp.where(i_ref[0] == u, p_ref[...], rest[u][...])
            acc = val if acc is None else acc + val
        o_ref[...] = acc

    def slot_spec(u):
        return pl.BlockSpec((None, tr, C), lambda r, i: (jnp.where(i[0] == u, (u + 1) % G, u), r, 0))

    grid_spec = pltpu.PrefetchScalarGridSpec(
        num_scalar_prefetch=1, grid=(nr,),
        in_specs=[pl.BlockSpec((None, tr, C), lambda r, i: (i[0], r, 0))] + [slot_spec(u) for u in range(G)],
        out_specs=pl.BlockSpec((tr, C), lambda r, i: (i[1] * nr + r, 0)))
    return pl.pallas_call(body, name=name, grid_spec=grid_spec, out_shape=jax.ShapeDtypeStruct((2 * R2, C), F32),
                          compiler_params=_params())(_ids(), part, slots, slots, slots, slots)


def _sum_slots(slots, name):
    G, R2, C = slots.shape
    tr = _tile(R2, max(8, (1024 * 1024) // (4 * C) // 8 * 8))

    def body(s_ref, o_ref):
        acc = s_ref[0]
        for u in range(1, G):
            acc = acc + s_ref[u]
        o_ref[...] = acc

    return pl.pallas_call(body, name=name, grid=(R2 // tr,), in_specs=[pl.BlockSpec((G, tr, C), lambda r: (0, r, 0))],
                          out_specs=pl.BlockSpec((tr, C), lambda r: (r, 0)), out_shape=jax.ShapeDtypeStruct((R2, C), F32),
                          compiler_params=_params())(slots)


def _pair_join(full, name):
    R, C = full.shape
    R2 = R // 2

    def body(f_ref, o_ref, token, send_sem, recv_sem):
        x, y, c, _ = _place()
        token[...] = jnp.zeros_like(token)
        mine = o_ref.at[pl.ds(c * R2, R2), :]
        theirs = o_ref.at[pl.ds((1 - c) * R2, R2), :]
        cp = pltpu.make_async_remote_copy(src_ref=mine, dst_ref=mine, send_sem=send_sem, recv_sem=recv_sem,
                                          device_id=(x, y, 1 - c), device_id_type=MESH)
        cp.start()
        pltpu.make_async_remote_copy(src_ref=theirs, dst_ref=theirs, send_sem=send_sem, recv_sem=recv_sem,
                                     device_id=(x, y, 1 - c), device_id_type=MESH).wait_recv()
        cp.wait_send()

    return pl.pallas_call(body, name=name, in_specs=[ANY], out_specs=[ANY, pl.BlockSpec(memory_space=pltpu.VMEM)],
                          out_shape=[jax.ShapeDtypeStruct((R, C), F32), jax.ShapeDtypeStruct((8, LANES), F32)],
                          input_output_aliases={0: 0},
                          scratch_shapes=[pltpu.SemaphoreType.DMA, pltpu.SemaphoreType.DMA])(full)


def _all_reduce_small(v, name):
    R, C = v.shape

    def gather_body(v_ref, out_ref, send_sems, recv_sems, local_sem):
        x, y, c, _ = _place()
        me = 4 * x + 2 * y + c
        mine = pltpu.make_async_copy(v_ref, out_ref.at[me], local_sem)
        mine.start()
        flips = [(fx, fy, fc) for fx in (0, 1) for fy in (0, 1) for fc in (0, 1)][1:]
        copies = []
        for j, (fx, fy, fc) in enumerate(flips):
            peer = (x ^ fx, y ^ fy, c ^ fc)
            copies.append(pltpu.make_async_remote_copy(src_ref=v_ref, dst_ref=out_ref.at[me], send_sem=send_sems.at[j],
                                                       recv_sem=recv_sems.at[j], device_id=peer, device_id_type=MESH))
        for cp in copies:
            cp.start()
        for j, (fx, fy, fc) in enumerate(flips):
            peer = (x ^ fx, y ^ fy, c ^ fc)
            pltpu.make_async_remote_copy(src_ref=v_ref, dst_ref=out_ref.at[4 * peer[0] + 2 * peer[1] + peer[2]],
                                         send_sem=send_sems.at[j], recv_sem=recv_sems.at[j], device_id=peer,
                                         device_id_type=MESH).wait_recv()
        for cp in copies:
            cp.wait_send()
        mine.wait()

    slots = pl.pallas_call(gather_body, name=name, in_specs=[ANY], out_specs=ANY,
                           out_shape=jax.ShapeDtypeStruct((8, R, C), F32),
                           scratch_shapes=[pltpu.SemaphoreType.DMA((7,)), pltpu.SemaphoreType.DMA((7,)),
                                           pltpu.SemaphoreType.DMA])(v)
    return _sum_slots(slots, f"{name}_sum")


def _adamw(w, g, m, v, name, dep=()):
    R, C = w.shape
    tr = _tile(R, max(8, (2 << 20) // (4 * C) // 8 * 8))
    bc1 = 1.0 - ADAM_B1 ** ADAM_STEP
    bc2 = 1.0 - ADAM_B2 ** ADAM_STEP

    def body(w_ref, g_ref, m_ref, v_ref, *rest):
        go_ref, d_ref, nm_ref, nv_ref = rest[len(dep):]
        gv = g_ref[...]
        go_ref[...] = gv
        nm = ADAM_B1 * m_ref[...] + (1.0 - ADAM_B1) * gv
        nv = ADAM_B2 * v_ref[...] + (1.0 - ADAM_B2) * (gv * gv)
        nm_ref[...] = nm
        nv_ref[...] = nv
        d_ref[...] = -ADAM_LR * ((nm / bc1) / (jnp.sqrt(nv / bc2) + ADAM_EPS) + ADAM_WD * w_ref[...])

    blk = pl.BlockSpec((tr, C), lambda r: (r, 0))
    out = jax.ShapeDtypeStruct((R, C), F32)
    in_specs = [blk] * 4 + [pl.BlockSpec(d.shape, lambda r: (0, 0)) for d in dep]
    return pl.pallas_call(body, name=name, grid=(R // tr,), in_specs=in_specs, out_specs=[blk] * 4, out_shape=[out] * 4,
                          compiler_params=_params())(w, g, m, v, *dep)


SC_TILES = 32
SC_LANES = 16
SC_ROWS = 8


def _sc_mesh():
    return plsc.VectorSubcoreMesh(core_axis_name="sc_core", subcore_axis_name="sc_subcore")


def _pair_add_sc(gfull, other, name):
    G, R, C = gfull.shape
    hr = R // 2
    tiles_per_shard = SC_TILES // G
    per_tile = hr // SC_ROWS // tiles_per_shard

    def body(g_hbm, o_hbm, out_hbm, gb, ob):
        c = lax.axis_index("c")
        tile = lax.axis_index("sc_subcore") * 2 + lax.axis_index("sc_core")
        t = tile // tiles_per_shard
        first = (tile % tiles_per_shard) * per_tile

        @pl.loop(0, per_tile)
        def _(k):
            rr = (first + k) * SC_ROWS
            pltpu.sync_copy(g_hbm.at[t, pl.ds(c * hr + rr, SC_ROWS), :], gb)
            pltpu.sync_copy(o_hbm.at[t, pl.ds(rr, SC_ROWS), :], ob)

            @pl.loop(0, SC_ROWS)
            def _(i):
                @pl.loop(0, C, step=SC_LANES)
                def _(j):
                    at = (i, pl.ds(j, SC_LANES))
                    gb[at] = gb[at] + ob[at]

            pltpu.sync_copy(gb, out_hbm.at[t, pl.ds(rr, SC_ROWS), :])

    buf = pltpu.VMEM((SC_ROWS, C), F32)
    return pl.kernel(body, name=name, out_type=jax.ShapeDtypeStruct((G, hr, C), F32), mesh=_sc_mesh(),
                     scratch_types=[buf, buf])(gfull, other)


def _adamw_sc(w, g, m, v, name):
    R, C = w.shape
    tasks = R // SC_ROWS
    bc1 = 1.0 - ADAM_B1 ** ADAM_STEP
    bc2 = 1.0 - ADAM_B2 ** ADAM_STEP

    def body(w_hbm, g_hbm, m_hbm, v_hbm, go_hbm, d_hbm, nm_hbm, nv_hbm, wb, gb, mb, vb):
        tile = lax.axis_index("sc_subcore") * 2 + lax.axis_index("sc_core")

        @pl.loop((tile * tasks) // SC_TILES, ((tile + 1) * tasks) // SC_TILES)
        def _(task):
            rows = pl.ds(task * SC_ROWS, SC_ROWS)
            pltpu.sync_copy(w_hbm.at[rows, :], wb)
            pltpu.sync_copy(g_hbm.at[rows, :], gb)
            pltpu.sync_copy(m_hbm.at[rows, :], mb)
            pltpu.sync_copy(v_hbm.at[rows, :], vb)

            @pl.loop(0, SC_ROWS)
            def _(i):
                @pl.loop(0, C, step=SC_LANES)
                def _(j):
                    at = (i, pl.ds(j, SC_LANES))
                    gv = gb[at]
                    nm = ADAM_B1 * mb[at] + (1.0 - ADAM_B1) * gv
                    nv = ADAM_B2 * vb[at] + (1.0 - ADAM_B2) * (gv * gv)
                    mb[at] = nm
                    vb[at] = nv
                    wb[at] = -ADAM_LR * ((nm / bc1) / (jnp.sqrt(nv / bc2) + ADAM_EPS) + ADAM_WD * wb[at])

            pltpu.sync_copy(gb, go_hbm.at[rows, :])
            pltpu.sync_copy(wb, d_hbm.at[rows, :])
            pltpu.sync_copy(mb, nm_hbm.at[rows, :])
            pltpu.sync_copy(vb, nv_hbm.at[rows, :])

    out = jax.ShapeDtypeStruct((R, C), F32)
    buf = pltpu.VMEM((SC_ROWS, C), F32)
    return pl.kernel(body, name=name, out_type=(out, out, out, out), mesh=_sc_mesh(),
                     scratch_types=[buf, buf, buf, buf])(w, g, m, v)


WEIGHTS = ['ffn1_norm', 'ffn1_w_gate', 'ffn1_w_up', 'ffn1_w_down', 'mix_norm', 'w_in', 'pool_w', 'pool_scale', 'gla_w_a2',
           'gla_b_a', 'gla_head_norm', 'w_out', 'xattn_norm', 'mem_norm', 'xattn_w_q', 'xattn_w_kv', 'xattn_w_o', 'ffn2_norm',
           'ffn2_w_gate', 'ffn2_w_up', 'ffn2_w_down', 'final_norm']
SHARDED = ['ffn1_w_gate', 'ffn1_w_up', 'ffn1_w_down', 'w_in', 'pool_w', 'gla_w_a2', 'w_out', 'xattn_w_q', 'xattn_w_kv',
           'xattn_w_o', 'ffn2_w_gate', 'ffn2_w_up', 'ffn2_w_down']
REPLICATED = [n for n in WEIGHTS if n not in SHARDED]
ON_SPARSECORE = ['ffn2_w_gate', 'ffn2_w_up', 'w_out', 'xattn_w_q', 'xattn_w_kv', 'xattn_w_o', 'ffn2_w_down']
SLOW_ON_SPARSECORE = 'ffn2_w_down'
PAIR_SUM_ON_SPARSECORE = ['ffn2_w_down', 'ffn2_w_gate', 'ffn2_w_up', 'xattn_w_o', 'xattn_w_kv', 'pool_w', 'ffn1_w_down']
SMALL_COLS = 512


def _as2d(a):
    return a.reshape(-1, a.shape[-1])


def _finish_weight(name, gathered, wl):
    G, R, C = gathered.shape
    rank = wl["gla_w_a2"].shape[1]
    if name in ("w_out", "xattn_w_q", "xattn_w_o"):
        return gathered.reshape(G * R, C)
    if name == "w_in":
        w_in = jnp.transpose(gathered, (1, 0, 2)).reshape(R, G * C)
        main = G * C - rank
        return jnp.concatenate([w_in[:, :main], jnp.pad(w_in[:, main:], ((0, 0), (0, LANES - rank)))], axis=1)
    if name == "pool_w":
        NG, CJ, _ = wl[name].shape[1:]
        return jnp.transpose(gathered.reshape(G, NG, CJ, C), (1, 0, 2, 3)).reshape(NG, G * CJ, C)
    if name == "gla_w_a2":
        a2 = jnp.transpose(gathered, (1, 0, 2)).reshape(rank, G * C)
        return jnp.pad(a2, ((0, LANES - rank), (0, 0))).astype(BF16)
    return gathered


def _start_gathers(wl):
    started = {}
    token = None
    for n in SHARDED:
        whole = n not in ("ffn1_w_gate", "ffn1_w_up")
        buf = _cast_to_slot(_as2d(wl[n]), BF16, f"slot_{n}", dep=token)
        send_sems, recv_sems, thru, token = _gather_start(buf, f"gather_start_{n}", whole)
        started[n] = (send_sems, recv_sems, thru, whole)
    cache = {}

    def weight(n, after=None):
        if n not in cache:
            *handles, whole = started[n]
            buf = _gather_wait(*handles, after, f"gather_wait_{n}", whole)
            if not whole:
                buf = _gather_forward(buf, f"gather_forward_{n}")
            cache[n] = _finish_weight(n, buf, wl)
        return cache[n]

    return weight, token


def _shard_major(name, gfull, wl):
    R, C = _as2d(wl[name]).shape
    if name in ("ffn1_w_gate", "ffn1_w_up", "ffn2_w_gate", "ffn2_w_up", "xattn_w_kv"):
        return gfull
    if name in ("ffn1_w_down", "ffn2_w_down", "w_out", "xattn_w_q", "xattn_w_o"):
        return gfull.reshape(N_SHARDS, R, C)
    if name == "w_in":
        return jnp.transpose(gfull[:, :N_SHARDS * C].reshape(R, N_SHARDS, C), (1, 0, 2))
    if name == "pool_w":
        NG, CJ, _ = wl[name].shape[1:]
        return jnp.transpose(gfull.reshape(NG, N_SHARDS, CJ, C), (1, 0, 2, 3)).reshape(N_SHARDS, R, C)
    assert name == "gla_w_a2"
    return jnp.transpose(gfull[:R].reshape(R, N_SHARDS, C), (1, 0, 2))


def kernel(x, mem, ffn1_norm, ffn1_w_gate, ffn1_w_up, ffn1_w_down, mix_norm, w_in, pool_w, pool_scale, gla_w_a2, gla_b_a, gla_head_norm, w_out, xattn_norm, mem_norm, xattn_w_q, xattn_w_kv, xattn_w_o, ffn2_norm, ffn2_w_gate, ffn2_w_up, ffn2_w_down, final_norm, loss_target, m_ffn1_norm, m_ffn1_w_gate, m_ffn1_w_up, m_ffn1_w_down, m_mix_norm, m_w_in, m_pool_w, m_pool_scale, m_gla_w_a2, m_gla_b_a, m_gla_head_norm, m_w_out, m_xattn_norm, m_mem_norm, m_xattn_w_q, m_xattn_w_kv, m_xattn_w_o, m_ffn2_norm, m_ffn2_w_gate, m_ffn2_w_up, m_ffn2_w_down, m_final_norm, v_ffn1_norm, v_ffn1_w_gate, v_ffn1_w_up, v_ffn1_w_down, v_mix_norm, v_w_in, v_pool_w, v_pool_scale, v_gla_w_a2, v_gla_b_a, v_gla_head_norm, v_w_out, v_xattn_norm, v_mem_norm, v_xattn_w_q, v_xattn_w_kv, v_xattn_w_o, v_ffn2_norm, v_ffn2_w_gate, v_ffn2_w_up, v_ffn2_w_down, v_final_norm):
    given = dict(locals())
    wl = {n: given[n] for n in WEIGHTS}
    ml = {n: given["m_" + n] for n in WEIGHTS}
    vl = {n: given["v_" + n] for n in WEIGHTS}

    vec = {n: wl[n].reshape(1, -1) for n in REPLICATED}
    weight, dep0 = _start_gathers(wl)
    in_flight = {}

    pair_flight = {}

    def emit_begin(n, gfull):
        *pair_flight[n], token = _pair_start(_shard_major(n, gfull, wl), f"{n}_pair_start")
        return token

    summing = {}

    def emit_finish(n, after):
        gsm, other = _pair_wait(*pair_flight.pop(n), after, f"{n}_pair_wait")
        if n in PAIR_SUM_ON_SPARSECORE:
            summing[n] = _pair_add_sc(gsm, other, f"{n}_pair_add_sc")
            return None
        *in_flight[n], token = _chip_start(_pair_add(gsm, other, f"{n}_pair_add"), f"{n}_chip_start")
        return token

    def emit_send(n):
        *in_flight[n], token = _chip_start(summing.pop(n), f"{n}_chip_start")
        return token

    grads = {}
    updates = {}

    def reduce_done(n, after):
        part, slots = _chip_wait(*in_flight.pop(n), after, f"{n}_chip_wait")
        grads[n], token = _pair_join(_chip_sum(part, slots, f"{n}_chip_sum"), f"{n}_pair_join")
        return token

    def early_update(after):
        tokens = [reduce_done(n, after) for n in ON_SPARSECORE]
        for n in ON_SPARSECORE:
            g2 = grads[n]
            updates[n] = _adamw_sc(wl[n].reshape(g2.shape), g2, ml[n].reshape(g2.shape), vl[n].reshape(g2.shape),
                                   f"adamw_sc_{n}")
        return tokens

    loss, dx0, g = _local_step(x[0], mem[0], loss_target[0], vec, weight,
                               (emit_begin, emit_finish, emit_send, early_update), dep0)
    assert not summing

    for n in list(in_flight):
        reduce_done(n, dx0)
    widths = [wl[n].size for n in REPLICATED]
    total = sum(widths)
    rows = -(-total // SMALL_COLS)
    rows = -(-rows // 8) * 8
    packed = jnp.concatenate([g[n].reshape(-1) for n in REPLICATED] + [jnp.zeros((rows * SMALL_COLS - total,), F32)])
    summed = _all_reduce_small(packed.reshape(rows, SMALL_COLS), "small_all_reduce").reshape(-1)
    off = 0
    for n, width in zip(REPLICATED, widths):
        grads[n] = summed[off:off + width].reshape(1, width)
        off += width

    out_g, out_d, out_m, out_v = [], [], [], []
    for n in WEIGHTS:
        shape = wl[n].shape
        g2 = grads[n]
        if n in updates:
            go, d, nm, nv = updates[n]
        else:
            dep = ()
            if n == "w_in":
                dep = tuple(updates[k][1][:8, :LANES] for k in updates if k != SLOW_ON_SPARSECORE)
            if n == "ffn1_w_gate":
                dep = (updates[SLOW_ON_SPARSECORE][1][:8, :LANES],)
            go, d, nm, nv = _adamw(wl[n].reshape(g2.shape), g2, ml[n].reshape(g2.shape), vl[n].reshape(g2.shape),
                                   f"adamw_{n}", dep)
        out_g.append(go.reshape(shape))
        out_d.append(d.reshape(shape))
        out_m.append(nm.reshape(shape))
        out_v.append(nv.reshape(shape))
    return (loss, dx0.reshape(x.shape), *out_g, *out_d, *out_m, *out_v)
```
